```python
import jax, jax.numpy as jnp
from jax import lax
import numpy as np

D_MODEL = 1024
BATCH = 8
SEQ = 2048
DEPTH = 4

N_MEM = 256
EPS = 1e-6
D_A = D_MODEL
CONV_A_WIDTH = 3
FOX_HEADS = 16
FOX_HEAD_DIM = D_MODEL // FOX_HEADS
D_B = FOX_HEADS * FOX_HEAD_DIM
Q_BLOCK = 128
D_C = D_MODEL
CONV_C_WIDTH = 31
N_BRANCH = 3
X_HEADS = 4
X_HEAD_DIM = 128
D_X = X_HEADS * X_HEAD_DIM
D_FF = -(-8 * D_MODEL // (3 * 256)) * 256
IN_SIZES = (D_A, D_A, D_A, D_B, D_B, D_B, FOX_HEADS, D_C, D_C, D_MODEL, D_MODEL, D_MODEL)
D_IN_TOTAL = sum(IN_SIZES)

kernel_name = "hybrid_gated_conv_fox_conformer_decoder"


def rms_norm(x, g):
    xf = x.astype(jnp.float32)
    y = xf * lax.rsqrt(jnp.mean(xf * xf, axis=-1, keepdims=True) + EPS)
    return (y * g.astype(jnp.float32)).astype(x.dtype)


def layer_norm(x, g, b):
    xf = x.astype(jnp.float32)
    mu = jnp.mean(xf, axis=-1, keepdims=True)
    xc = xf - mu
    y = xc * lax.rsqrt(jnp.mean(xc * xc, axis=-1, keepdims=True) + EPS)
    return (y * g.astype(jnp.float32) + b.astype(jnp.float32)).astype(x.dtype)


def causal_depthwise_conv(x, w):
    K, C = w.shape
    return lax.conv_general_dilated(
        x, w[:, None, :].astype(x.dtype), window_strides=(1,), padding=[(K - 1, 0)],
        dimension_numbers=("NWC", "WIO", "NWC"), feature_group_count=C)


def split_columns(z):
    offs = np.cumsum(np.array(IN_SIZES))[:-1]
    return jnp.split(z, [int(o) for o in offs], axis=-1)


def forgetting_attention(q, k, v, log_f):
    B, T, H, Dh = q.shape
    c = jnp.cumsum(log_f, axis=1).transpose(0, 2, 1)
    scale = Dh ** -0.5
    outs = []
    for i in range(T // Q_BLOCK):
        q0 = i * Q_BLOCK
        k_end = q0 + Q_BLOCK
        s = jnp.einsum("bqhd,bkhd->bhqk", q[:, q0:k_end], k[:, :k_end]).astype(jnp.float32) * scale
        decay = c[:, :, q0:k_end, None] - c[:, :, None, :k_end]
        causal = (q0 + jnp.arange(Q_BLOCK))[:, None] >= jnp.arange(k_end)[None, :]
        s = jnp.where(causal, s + decay, -jnp.inf)
        p = jax.nn.softmax(s, axis=-1)
        outs.append(jnp.einsum("bhqk,bkhd->bqhd", p.astype(v.dtype), v[:, :k_end]))
    return jnp.concatenate(outs, axis=1)


def hybrid_mixer(h, w_in, b_gate, b_forget, conv_a, w_out_a, w_out_b, conv_c, conv_c_bias,
                 ln_c_gain, ln_c_bias, w_out_c, w_o):
    B, T, _ = h.shape
    z = jnp.einsum("btd,de->bte", h, w_in)
    (a_b, a_c, a_u, q, k, v, f_logit, c_val, c_gate, g_a, g_b, g_c) = split_columns(z)
    y_a = a_b * causal_depthwise_conv(a_c * a_u, conv_a)
    y_a = jnp.einsum("btc,cd->btd", y_a, w_out_a)
    log_f = jax.nn.log_sigmoid(f_logit.astype(jnp.float32) + b_forget.astype(jnp.float32))
    o = forgetting_attention(q.reshape(B, T, FOX_HEADS, FOX_HEAD_DIM),
                             k.reshape(B, T, FOX_HEADS, FOX_HEAD_DIM),
                             v.reshape(B, T, FOX_HEADS, FOX_HEAD_DIM), log_f)
    y_b = jnp.einsum("btc,cd->btd", o.reshape(B, T, D_B), w_out_b)
    u = c_val * jax.nn.sigmoid(c_gate)
    u = causal_depthwise_conv(u, conv_c) + conv_c_bias.astype(u.dtype)
    u = jax.nn.silu(layer_norm(u, ln_c_gain, ln_c_bias))
    y_c = jnp.einsum("btc,cd->btd", u, w_out_c)
    ga, gb, gc = jnp.split(jax.nn.sigmoid(jnp.concatenate([g_a, g_b, g_c], axis=-1) + b_gate), N_BRANCH, axis=-1)
    merged = ga * y_a + gb * y_b + gc * y_c
    return jnp.einsum("btd,de->bte", merged, w_o)


def memory_cross_attention(h, mem_n, w_xq, w_xkv, w_xo):
    B, T, _ = h.shape
    q = jnp.einsum("btd,de->bte", h, w_xq).reshape(B, T, X_HEADS, X_HEAD_DIM)
    kv = jnp.einsum("bmd,de->bme", mem_n, w_xkv)
    k, v = jnp.split(kv, 2, axis=-1)
    k = k.reshape(B, -1, X_HEADS, X_HEAD_DIM)
    v = v.reshape(B, -1, X_HEADS, X_HEAD_DIM)
    s = jnp.einsum("bthd,bmhd->bhtm", q, k).astype(jnp.float32) * (X_HEAD_DIM ** -0.5)
    p = jax.nn.softmax(s, axis=-1)
    o = jnp.einsum("bhtm,bmhd->bthd", p.astype(v.dtype), v).reshape(B, T, D_X)
    return jnp.einsum("btc,cd->btd", o, w_xo)


def swiglu_ffn(h, w_gate_up, w_down):
    g, u = jnp.split(jnp.einsum("btd,df->btf", h, w_gate_up), 2, axis=-1)
    return jnp.einsum("btf,fd->btd", jax.nn.silu(g) * u, w_down)


def _fwd_setup_inputs(seed: int = 0) -> dict:
    key = jax.random.key(seed)
    ks = jax.random.split(key, 24)
    f32 = jnp.float32

    def w(k, shape, fan_in):
        return jax.random.normal(k, shape, f32) * (fan_in ** -0.5)

    def gain(k, shape):
        return 1.0 + 0.02 * jax.random.normal(k, shape, f32)

    def small(k, shape, s=0.02):
        return s * jax.random.normal(k, shape, f32)

    L, D = DEPTH, D_MODEL
    return {
        "x": jax.random.normal(ks[0], (BATCH, SEQ, D), f32),
        "mem": jax.random.normal(ks[1], (BATCH, N_MEM, D), f32),
        "mix_norm": gain(ks[2], (L, D)),
        "w_in": w(ks[3], (L, D, D_IN_TOTAL), D),
        "b_gate": small(ks[4], (L, N_BRANCH * D)),
        "b_forget": 2.0 + 0.5 * jax.random.normal(ks[5], (L, FOX_HEADS), f32),
        "conv_a": w(ks[6], (L, CONV_A_WIDTH, D_A), CONV_A_WIDTH),
        "w_out_a": w(ks[7], (L, D_A, D), D_A),
        "w_out_b": w(ks[8], (L, D_B, D), D_B),
        "conv_c": w(ks[9], (L, CONV_C_WIDTH, D_C), CONV_C_WIDTH),
        "conv_c_bias": small(ks[10], (L, D_C)),
        "ln_c_gain": gain(ks[11], (L, D_C)),
        "ln_c_bias": small(ks[12], (L, D_C)),
        "w_out_c": w(ks[13], (L, D_C, D), D_C),
        "w_o": w(ks[14], (L, D, D), D),
        "xattn_norm": gain(ks[15], (L, D)),
        "mem_norm": gain(ks[16], (D,)),
        "w_xq": w(ks[17], (L, D, D_X), D),
        "w_xkv": w(ks[18], (L, D, 2 * D_X), D),
        "w_xo": w(ks[19], (L, D_X, D), D_X),
        "ffn_norm": gain(ks[20], (L, D)),
        "w_gate_up": w(ks[21], (L, D, 2 * D_FF), D),
        "w_down": w(ks[22], (L, D_FF, D), D_FF),
        "final_norm": gain(ks[23], (D,)),
    }


def _fwd_reference(x, mem, mix_norm, w_in, b_gate, b_forget, conv_a, w_out_a, w_out_b, conv_c,
              conv_c_bias, ln_c_gain, ln_c_bias, w_out_c, w_o, xattn_norm, mem_norm, w_xq,
              w_xkv, w_xo, ffn_norm, w_gate_up, w_down, final_norm):
    mem_n = rms_norm(mem, mem_norm)
    for l in range(DEPTH):
        x = x + hybrid_mixer(rms_norm(x, mix_norm[l]), w_in[l], b_gate[l], b_forget[l], conv_a[l],
                             w_out_a[l], w_out_b[l], conv_c[l], conv_c_bias[l], ln_c_gain[l],
                             ln_c_bias[l], w_out_c[l], w_o[l])
        x = x + memory_cross_attention(rms_norm(x, xattn_norm[l]), mem_n, w_xq[l], w_xkv[l], w_xo[l])
        x = x + swiglu_ffn(rms_norm(x, ffn_norm[l]), w_gate_up[l], w_down[l])
    return rms_norm(x, final_norm)


import jax as _jax
import jax.numpy as _jnp

TWIN_FORMAT = 'train_step'
FWD_PARAMS = ['x', 'mem', 'mix_norm', 'w_in', 'b_gate', 'b_forget', 'conv_a', 'w_out_a', 'w_out_b', 'conv_c', 'conv_c_bias', 'ln_c_gain', 'ln_c_bias', 'w_out_c', 'w_o', 'xattn_norm', 'mem_norm', 'w_xq', 'w_xkv', 'w_xo', 'ffn_norm', 'w_gate_up', 'w_down', 'final_norm']
TWIN_WEIGHTS = ['mix_norm', 'w_in', 'b_gate', 'b_forget', 'conv_a', 'w_out_a', 'w_out_b', 'conv_c', 'conv_c_bias', 'ln_c_gain', 'ln_c_bias', 'w_out_c', 'w_o', 'xattn_norm', 'mem_norm', 'w_xq', 'w_xkv', 'w_xo', 'ffn_norm', 'w_gate_up', 'w_down', 'final_norm']
TWIN_DIFF_INPUT = 'x'
TWIN_INPUTS = ['x', 'mem', 'mix_norm', 'w_in', 'b_gate', 'b_forget', 'conv_a', 'w_out_a', 'w_out_b', 'conv_c', 'conv_c_bias', 'ln_c_gain', 'ln_c_bias', 'w_out_c', 'w_o', 'xattn_norm', 'mem_norm', 'w_xq', 'w_xkv', 'w_xo', 'ffn_norm', 'w_gate_up', 'w_down', 'final_norm', 'loss_target', 'm_mix_norm', 'm_w_in', 'm_b_gate', 'm_b_forget', 'm_conv_a', 'm_w_out_a', 'm_w_out_b', 'm_conv_c', 'm_conv_c_bias', 'm_ln_c_gain', 'm_ln_c_bias', 'm_w_out_c', 'm_w_o', 'm_xattn_norm', 'm_mem_norm', 'm_w_xq', 'm_w_xkv', 'm_w_xo', 'm_ffn_norm', 'm_w_gate_up', 'm_w_down', 'm_final_norm', 'v_mix_norm', 'v_w_in', 'v_b_gate', 'v_b_forget', 'v_conv_a', 'v_w_out_a', 'v_w_out_b', 'v_conv_c', 'v_conv_c_bias', 'v_ln_c_gain', 'v_ln_c_bias', 'v_w_out_c', 'v_w_o', 'v_xattn_norm', 'v_mem_norm', 'v_w_xq', 'v_w_xkv', 'v_w_xo', 'v_ffn_norm', 'v_w_gate_up', 'v_w_down', 'v_final_norm']
TWIN_OUTPUTS = ['loss', 'grad_x', 'grad_mix_norm', 'grad_w_in', 'grad_b_gate', 'grad_b_forget', 'grad_conv_a', 'grad_w_out_a', 'grad_w_out_b', 'grad_conv_c', 'grad_conv_c_bias', 'grad_ln_c_gain', 'grad_ln_c_bias', 'grad_w_out_c', 'grad_w_o', 'grad_xattn_norm', 'grad_mem_norm', 'grad_w_xq', 'grad_w_xkv', 'grad_w_xo', 'grad_ffn_norm', 'grad_w_gate_up', 'grad_w_down', 'grad_final_norm', 'delta_mix_norm', 'delta_w_in', 'delta_b_gate', 'delta_b_forget', 'delta_conv_a', 'delta_w_out_a', 'delta_w_out_b', 'delta_conv_c', 'delta_conv_c_bias', 'delta_ln_c_gain', 'delta_ln_c_bias', 'delta_w_out_c', 'delta_w_o', 'delta_xattn_norm', 'delta_mem_norm', 'delta_w_xq', 'delta_w_xkv', 'delta_w_xo', 'delta_ffn_norm', 'delta_w_gate_up', 'delta_w_down', 'delta_final_norm', 'new_m_mix_norm', 'new_m_w_in', 'new_m_b_gate', 'new_m_b_forget', 'new_m_conv_a', 'new_m_w_out_a', 'new_m_w_out_b', 'new_m_conv_c', 'new_m_conv_c_bias', 'new_m_ln_c_gain', 'new_m_ln_c_bias', 'new_m_w_out_c', 'new_m_w_o', 'new_m_xattn_norm', 'new_m_mem_norm', 'new_m_w_xq', 'new_m_w_xkv', 'new_m_w_xo', 'new_m_ffn_norm', 'new_m_w_gate_up', 'new_m_w_down', 'new_m_final_norm', 'new_v_mix_norm', 'new_v_w_in', 'new_v_b_gate', 'new_v_b_forget', 'new_v_conv_a', 'new_v_w_out_a', 'new_v_w_out_b', 'new_v_conv_c', 'new_v_conv_c_bias', 'new_v_ln_c_gain', 'new_v_ln_c_bias', 'new_v_w_out_c', 'new_v_w_o', 'new_v_xattn_norm', 'new_v_mem_norm', 'new_v_w_xq', 'new_v_w_xkv', 'new_v_w_xo', 'new_v_ffn_norm', 'new_v_w_gate_up', 'new_v_w_down', 'new_v_final_norm']
TWIN_LEAF_KINDS = {'loss': 'loss', 'grad_x': 'grad_x', 'grad_mix_norm': 'grad_w', 'grad_w_in': 'grad_w', 'grad_b_gate': 'grad_w', 'grad_b_forget': 'grad_w', 'grad_conv_a': 'grad_w', 'grad_w_out_a': 'grad_w', 'grad_w_out_b': 'grad_w', 'grad_conv_c': 'grad_w', 'grad_conv_c_bias': 'grad_w', 'grad_ln_c_gain': 'grad_w', 'grad_ln_c_bias': 'grad_w', 'grad_w_out_c': 'grad_w', 'grad_w_o': 'grad_w', 'grad_xattn_norm': 'grad_w', 'grad_mem_norm': 'grad_w', 'grad_w_xq': 'grad_w', 'grad_w_xkv': 'grad_w', 'grad_w_xo': 'grad_w', 'grad_ffn_norm': 'grad_w', 'grad_w_gate_up': 'grad_w', 'grad_w_down': 'grad_w', 'grad_final_norm': 'grad_w', 'delta_mix_norm': 'delta_w', 'delta_w_in': 'delta_w', 'delta_b_gate': 'delta_w', 'delta_b_forget': 'delta_w', 'delta_conv_a': 'delta_w', 'delta_w_out_a': 'delta_w', 'delta_w_out_b': 'delta_w', 'delta_conv_c': 'delta_w', 'delta_conv_c_bias': 'delta_w', 'delta_ln_c_gain': 'delta_w', 'delta_ln_c_bias': 'delta_w', 'delta_w_out_c': 'delta_w', 'delta_w_o': 'delta_w', 'delta_xattn_norm': 'delta_w', 'delta_mem_norm': 'delta_w', 'delta_w_xq': 'delta_w', 'delta_w_xkv': 'delta_w', 'delta_w_xo': 'delta_w', 'delta_ffn_norm': 'delta_w', 'delta_w_gate_up': 'delta_w', 'delta_w_down': 'delta_w', 'delta_final_norm': 'delta_w', 'new_m_mix_norm': 'new_m', 'new_m_w_in': 'new_m', 'new_m_b_gate': 'new_m', 'new_m_b_forget': 'new_m', 'new_m_conv_a': 'new_m', 'new_m_w_out_a': 'new_m', 'new_m_w_out_b': 'new_m', 'new_m_conv_c': 'new_m', 'new_m_conv_c_bias': 'new_m', 'new_m_ln_c_gain': 'new_m', 'new_m_ln_c_bias': 'new_m', 'new_m_w_out_c': 'new_m', 'new_m_w_o': 'new_m', 'new_m_xattn_norm': 'new_m', 'new_m_mem_norm': 'new_m', 'new_m_w_xq': 'new_m', 'new_m_w_xkv': 'new_m', 'new_m_w_xo': 'new_m', 'new_m_ffn_norm': 'new_m', 'new_m_w_gate_up': 'new_m', 'new_m_w_down': 'new_m', 'new_m_final_norm': 'new_m', 'new_v_mix_norm': 'new_v', 'new_v_w_in': 'new_v', 'new_v_b_gate': 'new_v', 'new_v_b_forget': 'new_v', 'new_v_conv_a': 'new_v', 'new_v_w_out_a': 'new_v', 'new_v_w_out_b': 'new_v', 'new_v_conv_c': 'new_v', 'new_v_conv_c_bias': 'new_v', 'new_v_ln_c_gain': 'new_v', 'new_v_ln_c_bias': 'new_v', 'new_v_w_out_c': 'new_v', 'new_v_w_o': 'new_v', 'new_v_xattn_norm': 'new_v', 'new_v_mem_norm': 'new_v', 'new_v_w_xq': 'new_v', 'new_v_w_xkv': 'new_v', 'new_v_w_xo': 'new_v', 'new_v_ffn_norm': 'new_v', 'new_v_w_gate_up': 'new_v', 'new_v_w_down': 'new_v', 'new_v_final_norm': 'new_v'}


def _forward(args):
    return _fwd_reference(*[args[k] for k in FWD_PARAMS])


def _output_shape():
    out = _jax.eval_shape(lambda: _forward(_fwd_setup_inputs(0)))
    return out.shape, out.dtype

N_MICROBATCH = 1
ADAM_LR = 0.001
ADAM_B1 = 0.9
ADAM_B2 = 0.999
ADAM_EPS = 1e-08
ADAM_WD = 0.01
ADAM_STEP = 10
PER_EXAMPLE_BATCH_AXIS = {'x': 0, 'mem': 0, 'loss_target': 0}
SHARED_INPUTS = []
_WEIGHT_DTYPES = {'mix_norm': _jnp.float32, 'w_in': _jnp.float32, 'b_gate': _jnp.float32, 'b_forget': _jnp.float32, 'conv_a': _jnp.float32, 'w_out_a': _jnp.float32, 'w_out_b': _jnp.float32, 'conv_c': _jnp.float32, 'conv_c_bias': _jnp.float32, 'ln_c_gain': _jnp.float32, 'ln_c_bias': _jnp.float32, 'w_out_c': _jnp.float32, 'w_o': _jnp.float32, 'xattn_norm': _jnp.float32, 'mem_norm': _jnp.float32, 'w_xq': _jnp.float32, 'w_xkv': _jnp.float32, 'w_xo': _jnp.float32, 'ffn_norm': _jnp.float32, 'w_gate_up': _jnp.float32, 'w_down': _jnp.float32, 'final_norm': _jnp.float32}
MOMENT_SCALE = {'mix_norm': 1.366912e-01, 'w_in': 4.049102e-02, 'b_gate': 1.893497e-02, 'b_forget': 1.194123e-01, 'conv_a': 6.887579e-02, 'w_out_a': 6.721564e-02, 'w_out_b': 2.840241e-02, 'conv_c': 4.145013e-02, 'conv_c_bias': 9.063826e-02, 'ln_c_gain': 5.073917e-02, 'ln_c_bias': 4.343163e-02, 'w_out_c': 4.048456e-02, 'w_o': 8.281039e-02, 'xattn_norm': 1.290506e-02, 'mem_norm': 3.876032e-02, 'w_xq': 1.839456e-02, 'w_xkv': 1.859596e-02, 'w_xo': 1.324098e-02, 'ffn_norm': 8.994199e-02, 'w_gate_up': 3.823499e-02, 'w_down': 6.232605e-02, 'final_norm': 1.597756e+01}


def _to_microbatches(a, axis):
    t = _jnp.moveaxis(a, axis, 0)
    t = t.reshape((N_MICROBATCH, t.shape[0] // N_MICROBATCH) + t.shape[1:])
    return _jnp.moveaxis(t, 1, axis + 1)


def setup_inputs(seed: int = 0) -> dict:
    inp = _fwd_setup_inputs(seed)
    key = _jax.random.fold_in(_jax.random.key(seed), 7919)
    shape, _ = _output_shape()
    out = dict(inp)
    out["loss_target"] = _jax.random.normal(_jax.random.fold_in(key, 0), shape, _jnp.float32)
    for i, name in enumerate(TWIN_WEIGHTS):
        w = inp[name].astype(_jnp.float32)
        if MOMENT_SCALE is None:
            s = _jnp.sqrt(_jnp.mean(_jnp.square(w)) + 1e-30)
        else:
            s = MOMENT_SCALE[name]
        km, kv = _jax.random.split(_jax.random.fold_in(key, i + 1))
        out[name] = w
        out["m_" + name] = s * _jax.random.normal(km, w.shape, _jnp.float32)
        out["v_" + name] = (s * s) * _jax.random.uniform(kv, w.shape, _jnp.float32, 0.5, 1.5)
    if N_MICROBATCH > 1:
        for name, axis in PER_EXAMPLE_BATCH_AXIS.items():
            out[name] = _to_microbatches(out[name], axis)
    return {'x': out['x'], 'mem': out['mem'], 'mix_norm': out['mix_norm'], 'w_in': out['w_in'], 'b_gate': out['b_gate'], 'b_forget': out['b_forget'], 'conv_a': out['conv_a'], 'w_out_a': out['w_out_a'], 'w_out_b': out['w_out_b'], 'conv_c': out['conv_c'], 'conv_c_bias': out['conv_c_bias'], 'ln_c_gain': out['ln_c_gain'], 'ln_c_bias': out['ln_c_bias'], 'w_out_c': out['w_out_c'], 'w_o': out['w_o'], 'xattn_norm': out['xattn_norm'], 'mem_norm': out['mem_norm'], 'w_xq': out['w_xq'], 'w_xkv': out['w_xkv'], 'w_xo': out['w_xo'], 'ffn_norm': out['ffn_norm'], 'w_gate_up': out['w_gate_up'], 'w_down': out['w_down'], 'final_norm': out['final_norm'], 'loss_target': out['loss_target'], 'm_mix_norm': out['m_mix_norm'], 'm_w_in': out['m_w_in'], 'm_b_gate': out['m_b_gate'], 'm_b_forget': out['m_b_forget'], 'm_conv_a': out['m_conv_a'], 'm_w_out_a': out['m_w_out_a'], 'm_w_out_b': out['m_w_out_b'], 'm_conv_c': out['m_conv_c'], 'm_conv_c_bias': out['m_conv_c_bias'], 'm_ln_c_gain': out['m_ln_c_gain'], 'm_ln_c_bias': out['m_ln_c_bias'], 'm_w_out_c': out['m_w_out_c'], 'm_w_o': out['m_w_o'], 'm_xattn_norm': out['m_xattn_norm'], 'm_mem_norm': out['m_mem_norm'], 'm_w_xq': out['m_w_xq'], 'm_w_xkv': out['m_w_xkv'], 'm_w_xo': out['m_w_xo'], 'm_ffn_norm': out['m_ffn_norm'], 'm_w_gate_up': out['m_w_gate_up'], 'm_w_down': out['m_w_down'], 'm_final_norm': out['m_final_norm'], 'v_mix_norm': out['v_mix_norm'], 'v_w_in': out['v_w_in'], 'v_b_gate': out['v_b_gate'], 'v_b_forget': out['v_b_forget'], 'v_conv_a': out['v_conv_a'], 'v_w_out_a': out['v_w_out_a'], 'v_w_out_b': out['v_w_out_b'], 'v_conv_c': out['v_conv_c'], 'v_conv_c_bias': out['v_conv_c_bias'], 'v_ln_c_gain': out['v_ln_c_gain'], 'v_ln_c_bias': out['v_ln_c_bias'], 'v_w_out_c': out['v_w_out_c'], 'v_w_o': out['v_w_o'], 'v_xattn_norm': out['v_xattn_norm'], 'v_mem_norm': out['v_mem_norm'], 'v_w_xq': out['v_w_xq'], 'v_w_xkv': out['v_w_xkv'], 'v_w_xo': out['v_w_xo'], 'v_ffn_norm': out['v_ffn_norm'], 'v_w_gate_up': out['v_w_gate_up'], 'v_w_down': out['v_w_down'], 'v_final_norm': out['v_final_norm']}


def _loss(weights, diff, rest, loss_target):
    with _jax.named_scope("forward"):
        args = {**rest, TWIN_DIFF_INPUT: diff, **{k: w.astype(_WEIGHT_DTYPES[k]) for k, w in weights.items()}}
        y = _forward(args)
    with _jax.named_scope("loss_head"):
        err = _jnp.square(y.astype(_jnp.float32) - loss_target)
        return 0.5 * _jnp.sum(_jnp.mean(err, axis=-1)) if err.ndim else 0.5 * err


def _adamw(w, g, m, v):
    m = ADAM_B1 * m + (1.0 - ADAM_B1) * g
    v = ADAM_B2 * v + (1.0 - ADAM_B2) * _jnp.square(g)
    m_hat = m / (1.0 - ADAM_B1 ** ADAM_STEP)
    v_hat = v / (1.0 - ADAM_B2 ** ADAM_STEP)
    delta = -ADAM_LR * (m_hat / (_jnp.sqrt(v_hat) + ADAM_EPS) + ADAM_WD * w)
    return delta, m, v


def reference(x, mem, mix_norm, w_in, b_gate, b_forget, conv_a, w_out_a, w_out_b, conv_c, conv_c_bias, ln_c_gain, ln_c_bias, w_out_c, w_o, xattn_norm, mem_norm, w_xq, w_xkv, w_xo, ffn_norm, w_gate_up, w_down, final_norm, loss_target, m_mix_norm, m_w_in, m_b_gate, m_b_forget, m_conv_a, m_w_out_a, m_w_out_b, m_conv_c, m_conv_c_bias, m_ln_c_gain, m_ln_c_bias, m_w_out_c, m_w_o, m_xattn_norm, m_mem_norm, m_w_xq, m_w_xkv, m_w_xo, m_ffn_norm, m_w_gate_up, m_w_down, m_final_norm, v_mix_norm, v_w_in, v_b_gate, v_b_forget, v_conv_a, v_w_out_a, v_w_out_b, v_conv_c, v_conv_c_bias, v_ln_c_gain, v_ln_c_bias, v_w_out_c, v_w_o, v_xattn_norm, v_mem_norm, v_w_xq, v_w_xkv, v_w_xo, v_ffn_norm, v_w_gate_up, v_w_down, v_final_norm):
    given = dict(x=x, mem=mem, mix_norm=mix_norm, w_in=w_in, b_gate=b_gate, b_forget=b_forget, conv_a=conv_a, w_out_a=w_out_a, w_out_b=w_out_b, conv_c=conv_c, conv_c_bias=conv_c_bias, ln_c_gain=ln_c_gain, ln_c_bias=ln_c_bias, w_out_c=w_out_c, w_o=w_o, xattn_norm=xattn_norm, mem_norm=mem_norm, w_xq=w_xq, w_xkv=w_xkv, w_xo=w_xo, ffn_norm=ffn_norm, w_gate_up=w_gate_up, w_down=w_down, final_norm=final_norm, loss_target=loss_target, m_mix_norm=m_mix_norm, m_w_in=m_w_in, m_b_gate=m_b_gate, m_b_forget=m_b_forget, m_conv_a=m_conv_a, m_w_out_a=m_w_out_a, m_w_out_b=m_w_out_b, m_conv_c=m_conv_c, m_conv_c_bias=m_conv_c_bias, m_ln_c_gain=m_ln_c_gain, m_ln_c_bias=m_ln_c_bias, m_w_out_c=m_w_out_c, m_w_o=m_w_o, m_xattn_norm=m_xattn_norm, m_mem_norm=m_mem_norm, m_w_xq=m_w_xq, m_w_xkv=m_w_xkv, m_w_xo=m_w_xo, m_ffn_norm=m_ffn_norm, m_w_gate_up=m_w_gate_up, m_w_down=m_w_down, m_final_norm=m_final_norm, v_mix_norm=v_mix_norm, v_w_in=v_w_in, v_b_gate=v_b_gate, v_b_forget=v_b_forget, v_conv_a=v_conv_a, v_w_out_a=v_w_out_a, v_w_out_b=v_w_out_b, v_conv_c=v_conv_c, v_conv_c_bias=v_conv_c_bias, v_ln_c_gain=v_ln_c_gain, v_ln_c_bias=v_ln_c_bias, v_w_out_c=v_w_out_c, v_w_o=v_w_o, v_xattn_norm=v_xattn_norm, v_mem_norm=v_mem_norm, v_w_xq=v_w_xq, v_w_xkv=v_w_xkv, v_w_xo=v_w_xo, v_ffn_norm=v_ffn_norm, v_w_gate_up=v_w_gate_up, v_w_down=v_w_down, v_final_norm=v_final_norm)
    weights = {n: given[n] for n in TWIN_WEIGHTS}
    shared = {n: given[n] for n in SHARED_INPUTS}
    per_example = {n: given[n] for n in ['x', 'mem']}
    grad_fn = _jax.value_and_grad(_loss, argnums=(0, 1))

    def one_microbatch(ex, loss_target):
        ex = dict(ex)
        diff = ex.pop(TWIN_DIFF_INPUT)
        return grad_fn(weights, diff, {**shared, **ex}, loss_target)

    if N_MICROBATCH == 1:
        loss, (grad_w, grad_x) = one_microbatch(per_example, given["loss_target"])
    else:
        def body(carry, xs):
            loss_sum, grad_sum = carry
            l_k, (gw_k, gx_k) = one_microbatch(xs[0], xs[1])
            with _jax.named_scope("update"):
                return (loss_sum + l_k, _jax.tree.map(_jnp.add, grad_sum, gw_k)), gx_k

        init = (_jnp.zeros((), _jnp.float32), _jax.tree.map(_jnp.zeros_like, weights))
        (loss, grad_w), grad_x = _jax.lax.scan(body, init, (per_example, given["loss_target"]))
    with _jax.named_scope("update"):
        delta_w, new_m, new_v = {}, {}, {}
        for n in TWIN_WEIGHTS:
            delta_w[n], new_m[n], new_v[n] = _adamw(weights[n], grad_w[n], given["m_" + n], given["v_" + n])
    return (loss, grad_x, *[grad_w[n] for n in TWIN_WEIGHTS], *[delta_w[n] for n in TWIN_WEIGHTS],
            *[new_m[n] for n in TWIN_WEIGHTS], *[new_v[n] for n in TWIN_WEIGHTS])
```

```python
import functools
import math

import jax
import jax.numpy as jnp
from jax import lax
from jax.experimental import pallas as pl
from jax.experimental.pallas import tpu as pltpu

F32, BF16 = jnp.float32, jnp.bfloat16
EPS = 1e-6
LANES = 128
SUBLANES = 8
VMEM_LIMIT = 56 * 1024 * 1024
MM_VMEM_BUDGET = 28 * 1024 * 1024
MESH = pl.DeviceIdType.MESH
N_CHIPS = 4
N_DEV = 8

ADAM_LR, ADAM_B1, ADAM_B2, ADAM_EPS, ADAM_WD, ADAM_STEP = 0.001, 0.9, 0.999, 1e-08, 0.01, 10


def _params(**kw):
    return pltpu.CompilerParams(vmem_limit_bytes=VMEM_LIMIT, **kw)


def _sds(shape, dtype):
    return jax.ShapeDtypeStruct(tuple(shape), dtype)


def _part(arr, off=0, width=None):
    return (arr, off, arr.shape[1] - off if width is None else width)


def _unit(parts):
    u = 0
    for _, off, w in parts:
        u = math.gcd(u, math.gcd(off, w))
    return u


def _pick(dim, unit, cands, lane):
    for c in cands:
        if c <= dim and dim % c == 0 and unit % c == 0 and (not lane or c % LANES == 0):
            return c
    if not lane or dim % LANES == 0 or unit == dim:
        return dim
    raise ValueError(f"no tile for {dim} {unit}")


_COL_TILES = (1024, 1408, 512, 256, 128)
_ROW_TILES = (512, 256, 1408, 128)


def _mm(name, a, b, *, ta=False, tb=False, out_dtype=F32, add=None):
    if not isinstance(a, list):
        a = [_part(a)]
    if not isinstance(b, list):
        b = [_part(b)]
    a_rows, a_cols = a[0][0].shape[0], sum(p[2] for p in a)
    b_rows, b_cols = b[0][0].shape[0], sum(p[2] for p in b)
    M, Ka = (a_cols, a_rows) if ta else (a_rows, a_cols)
    Kb, N = (b_cols, b_rows) if tb else (b_rows, b_cols)
    assert Ka == Kb, (name, Ka, Kb)
    K = Ka
    ua, ub = _unit(a), _unit(b)
    tm = _pick(M, ua if ta else M, _ROW_TILES, ta)
    tn = _pick(N, N if tb else ub, _COL_TILES, not tb)
    ku = math.gcd(K if ta else ua, ub if tb else K)
    tk = _pick(K, ku, _COL_TILES, (not ta) or tb)
    a_size = max(p[0].dtype.itemsize for p in a)
    b_size = max(p[0].dtype.itemsize for p in b)

    def vmem_estimate():
        return (2 * len(a) * tm * tk * a_size + 2 * len(b) * tk * tn * b_size + tm * tn * (2 * jnp.dtype(out_dtype).itemsize + 4)
                + (8 * tm * tn if add is not None else 0))

    while vmem_estimate() > MM_VMEM_BUDGET:
        if tk % 256 == 0 and tk >= tn:
            tk //= 2
        elif tn % 256 == 0:
            tn //= 2
        elif tm % 256 == 0:
            tm //= 2
        elif tk % 256 == 0:
            tk //= 2
        else:
            break
    nm, nn, nk = M // tm, N // tn, K // tk
    a_tile, b_tile = (tm if ta else tk), (tk if tb else tn)
    a_nb, b_nb = a[0][2] // a_tile, b[0][2] // b_tile

    def col_index(parts, nb, tile, p, col):
        base = parts[p][1] // tile
        if len(parts) == 1:
            return base + col
        return base + jnp.clip(col - p * nb, 0, nb - 1)

    in_specs, operands = [], []
    for p in range(len(a)):
        if ta:
            in_specs.append(pl.BlockSpec((tk, tm), lambda mi, nj, kk, p=p: (kk, col_index(a, a_nb, a_tile, p, mi))))
        else:
            in_specs.append(pl.BlockSpec((tm, tk), lambda mi, nj, kk, p=p: (mi, col_index(a, a_nb, a_tile, p, kk))))
        operands.append(a[p][0])
    for p in range(len(b)):
        if tb:
            in_specs.append(pl.BlockSpec((tn, tk), lambda mi, nj, kk, p=p: (nj, col_index(b, b_nb, b_tile, p, kk))))
        else:
            in_specs.append(pl.BlockSpec((tk, tn), lambda mi, nj, kk, p=p: (kk, col_index(b, b_nb, b_tile, p, nj))))
        operands.append(b[p][0])
    if add is not None:
        in_specs.append(pl.BlockSpec((tm, tn), lambda mi, nj, kk: (mi, nj)))
        operands.append(add)
    dims = (((0 if ta else 1,), (1 if tb else 0,)), ((), ()))
    na, nb_ = len(a), len(b)

    def body(*refs):
        a_refs, b_refs = refs[:na], refs[na:na + nb_]
        pos = na + nb_
        add_ref = None
        if add is not None:
            add_ref = refs[pos]
            pos += 1
        o_ref = refs[pos]
        acc_ref = refs[pos + 1] if nk > 1 else None
        mi, nj, kk = pl.program_id(0), pl.program_id(1), pl.program_id(2)
        a_col = mi if ta else kk
        b_col = kk if tb else nj

        def finish(r):
            if add_ref is not None:
                r = r + add_ref[...].astype(F32)
            o_ref[...] = r.astype(o_ref.dtype)

        def step(a_ref, b_ref):
            r = lax.dot_general(a_ref[...].astype(BF16), b_ref[...].astype(BF16), dims, preferred_element_type=F32)
            if nk == 1:
                finish(r)
            else:
                @pl.when(kk == 0)
                def _():
                    acc_ref[...] = r

                @pl.when(kk > 0)
                def _():
                    acc_ref[...] += r

        for pa in range(na):
            for pb in range(nb_):
                if na * nb_ == 1:
                    step(a_refs[pa], b_refs[pb])
                else:
                    cond = jnp.logical_and(a_col // a_nb == pa, b_col // b_nb == pb)
                    pl.when(cond)(functools.partial(step, a_refs[pa], b_refs[pb]))
        if nk > 1:
            @pl.when(kk == nk - 1)
            def _():
                finish(acc_ref[...])

    return pl.pallas_call(
        body, grid=(nm, nn, nk), in_specs=in_specs,
        out_specs=pl.BlockSpec((tm, tn), lambda mi, nj, kk: (mi, nj)),
        out_shape=_sds((M, N), out_dtype),
        scratch_shapes=[pltpu.VMEM((tm, tn), F32)] if nk > 1 else [],
        compiler_params=_params(dimension_semantics=("parallel", "parallel", "arbitrary")),
        name=name)(*operands)


def _row_spec(tb, part):
    _, off, w = part
    return pl.BlockSpec((tb, w), lambda i, o=off // w: (i, o))


def _full_spec(arr):
    return pl.BlockSpec(arr.shape, lambda i: (0,) * arr.ndim)


def _rows(name, fn, rows, pars, outs, tb=256):
    n = rows[0][0].shape[0]
    tb = min(tb, n)
    nr, npar = len(rows), len(pars)

    def body(*refs):
        r = [ref[...].astype(F32) for ref in refs[:nr]]
        p = [ref[...].astype(F32) for ref in refs[nr:nr + npar]]
        res = fn(*r, *p)
        for o_ref, v in zip(refs[nr + npar:], res):
            o_ref[...] = v.astype(o_ref.dtype)

    out = pl.pallas_call(
        body, grid=(n // tb,),
        in_specs=[_row_spec(tb, p) for p in rows] + [_full_spec(p) for p in pars],
        out_specs=[pl.BlockSpec((tb, w), lambda i: (i, 0)) for w, _ in outs],
        out_shape=[_sds((n, w), dt) for w, dt in outs],
        compiler_params=_params(), name=name)(*[p[0] for p in rows], *pars)
    return out


def _rows_vjp(name, fn, rows, pars, cots, row_grads, par_grads, adds=None, tb=256):
    adds = adds or {}
    n = rows[0][0].shape[0]
    tb = min(tb, n)
    nr, npar, nc = len(rows), len(pars), len(cots)
    add_keys = sorted(adds)
    rg = [j for j, dt in enumerate(row_grads) if dt is not None]
    pg = [j for j, f in enumerate(par_grads) if f]

    def body(*refs):
        pos = 0
        r = [ref[...].astype(F32) for ref in refs[pos:pos + nr]]
        pos += nr
        p = [ref[...].astype(F32) for ref in refs[pos:pos + npar]]
        pos += npar
        ct = tuple(ref[...].astype(F32) for ref in refs[pos:pos + nc])
        pos += nc
        add_v = {k: ref[...].astype(F32) for k, ref in zip(add_keys, refs[pos:pos + len(add_keys)])}
        pos += len(add_keys)
        _, vjp = jax.vjp(lambda *args: tuple(fn(*args)), *r, *p)
        g = vjp(ct)
        for j in rg:
            v = g[j]
            if j in add_v:
                v = v + add_v[j]
            refs[pos][...] = v.astype(refs[pos].dtype)
            pos += 1
        first = pl.program_id(0) == 0
        for j in pg:
            ref = refs[pos]
            pos += 1

            @pl.when(first)
            def _(ref=ref):
                ref[...] = jnp.zeros_like(ref)

            ref[...] += g[nr + j]

    in_specs = ([_row_spec(tb, p) for p in rows] + [_full_spec(p) for p in pars] + [_row_spec(tb, p) for p in cots]
                + [_row_spec(tb, adds[k]) for k in add_keys])
    out_specs = ([pl.BlockSpec((tb, rows[j][2]), lambda i: (i, 0)) for j in rg] + [_full_spec(pars[j]) for j in pg])
    out_shape = ([_sds((n, rows[j][2]), row_grads[j]) for j in rg] + [_sds(pars[j].shape, F32) for j in pg])
    out = pl.pallas_call(
        body, grid=(n // tb,), in_specs=in_specs, out_specs=out_specs, out_shape=out_shape,
        compiler_params=_params(), name=name)(
            *[p[0] for p in rows], *pars, *[p[0] for p in cots], *[adds[k][0] for k in add_keys])
    return out[:len(rg)], out[len(rg):]


def _rms_fn(x, g):
    return (x * lax.rsqrt(jnp.mean(x * x, axis=-1, keepdims=True) + EPS) * g,)


def _lnsilu_fn(u, g, b):
    mu = jnp.mean(u, axis=-1, keepdims=True)
    xc = u - mu
    y = xc * lax.rsqrt(jnp.mean(xc * xc, axis=-1, keepdims=True) + EPS) * g + b
    return (y * jax.nn.sigmoid(y),)


def _merge_fn(ga, gb, gc, ya, yb, yc, ba, bb, bc):
    return (jax.nn.sigmoid(ga + ba) * ya + jax.nn.sigmoid(gb + bb) * yb + jax.nn.sigmoid(gc + bc) * yc,)


def _swiglu_fn(g, u):
    return (g * jax.nn.sigmoid(g) * u,)


def _loss_call(x, tgt, gain):
    n, d = x.shape
    tb = min(256, n)

    def body(x_ref, t_ref, g_ref, l_ref, dx_ref, dg_ref):
        (y,), vjp = jax.vjp(lambda a, b: tuple(_rms_fn(a, b)), x_ref[...], g_ref[...])
        e = y - t_ref[...]
        part = 0.5 * jnp.sum(jnp.mean(e * e, axis=-1, keepdims=True))
        dx, dg = vjp((e * (1.0 / d),))
        dx_ref[...] = dx

        @pl.when(pl.program_id(0) == 0)
        def _():
            l_ref[...] = jnp.zeros_like(l_ref)
            dg_ref[...] = jnp.zeros_like(dg_ref)

        l_ref[...] += jnp.full(l_ref.shape, part, F32)
        dg_ref[...] += dg

    return pl.pallas_call(
        body, grid=(n // tb,),
        in_specs=[pl.BlockSpec((tb, d), lambda i: (i, 0)), pl.BlockSpec((tb, d), lambda i: (i, 0)), _full_spec(gain)],
        out_specs=[pl.BlockSpec((1, LANES), lambda i: (0, 0)), pl.BlockSpec((tb, d), lambda i: (i, 0)), _full_spec(gain)],
        out_shape=[_sds((1, LANES), F32), _sds((n, d), F32), _sds(gain.shape, F32)],
        compiler_params=_params(), name="loss_head")(x, tgt, gain)


CONV_ROWS = 64
CONV_HALO = 32


def _chunk(i):
    return pl.ds(pl.multiple_of(i * CONV_ROWS, CONV_ROWS), CONV_ROWS)


def _delayed(ext, s):
    return (ext if s == 0 else pltpu.roll(ext, s, 0))[CONV_HALO:]


def _advanced(ext, s):
    return (ext if s == 0 else pltpu.roll(ext, ext.shape[0] - s, 0))[:CONV_ROWS]


def _conv_pass(t, taps, w_ref, lanes, get_u, emit, get_dy=None, dw_acc=None):
    def body(i, carry):
        rows = _chunk(i)
        u = get_u(rows)
        ext = jnp.concatenate([carry, u], axis=0)
        dy = None if get_dy is None else get_dy(rows)
        acc = None
        for k in range(taps):
            sh = _delayed(ext, taps - 1 - k)
            term = w_ref[k:k + 1, lanes] * sh
            acc = term if acc is None else acc + term
            if dy is not None:
                dw_acc[k] += jnp.sum((dy * sh).reshape(CONV_ROWS // SUBLANES, SUBLANES, LANES), axis=0)
        emit(rows, u, acc)
        return u[CONV_ROWS - CONV_HALO:]

    lax.fori_loop(0, t // CONV_ROWS, body, jnp.zeros((CONV_HALO, LANES), F32))


def _conv_pass_t(t, taps, w_ref, lanes, get_dy, emit):
    n = t // CONV_ROWS

    def body(j, carry):
        rows = _chunk(n - 1 - j)
        dy = get_dy(rows)
        ext = jnp.concatenate([dy, carry], axis=0)
        acc = None
        for k in range(taps):
            term = w_ref[k:k + 1, lanes] * _advanced(ext, taps - 1 - k)
            acc = term if acc is None else acc + term
        emit(rows, acc)
        return dy[:CONV_HALO]

    lax.fori_loop(0, n, body, jnp.zeros((CONV_HALO, LANES), F32))


def _chan_block(d):
    return min(256, d)


def _lane_groups(cb):
    return [slice(g * LANES, (g + 1) * LANES) for g in range(cb // LANES)]


def _conv_a_fwd(za, w):
    t, d = za.shape[0], za.shape[1] // 3
    taps, cb = w.shape[0], _chan_block(d)
    nb = d // cb
    assert taps - 1 <= CONV_HALO and t % CONV_ROWS == 0

    def body(ab_ref, ac_ref, au_ref, w_ref, o_ref):
        for lanes in _lane_groups(cb):
            def emit(rows, u, s, lanes=lanes):
                o_ref[rows, lanes] = (ab_ref[rows, lanes] * s).astype(o_ref.dtype)

            _conv_pass(t, taps, w_ref, lanes, lambda rows, lanes=lanes: ac_ref[rows, lanes] * au_ref[rows, lanes], emit)

    return pl.pallas_call(
        body, grid=(nb,),
        in_specs=[pl.BlockSpec((t, cb), lambda j, o=o: (0, o * nb + j)) for o in range(3)]
        + [pl.BlockSpec((taps, cb), lambda j: (0, j))],
        out_specs=pl.BlockSpec((t, cb), lambda j: (0, j)),
        out_shape=_sds((t, d), BF16), compiler_params=_params(), name="conv_a_fwd")(za, za, za, w)


def _finish_taps(dw_ref, dw_acc, lanes, taps):
    for k in range(taps):
        dw_ref[k:k + 1, lanes] = jnp.sum(dw_acc[k], axis=0, keepdims=True)


def _conv_a_bwd(za, w, dpa):
    t, d = za.shape[0], za.shape[1] // 3
    taps, cb = w.shape[0], _chan_block(d)
    nb = d // cb

    def body(ab_ref, ac_ref, au_ref, w_ref, dy_ref, dab_ref, dac_ref, dau_ref, dw_ref, dw_acc):
        for lanes in _lane_groups(cb):
            dw_acc[...] = jnp.zeros_like(dw_acc)

            def ds(rows, lanes=lanes):
                return dy_ref[rows, lanes] * ab_ref[rows, lanes]

            def emit_s(rows, u, s, lanes=lanes):
                dab_ref[rows, lanes] = (dy_ref[rows, lanes] * s).astype(dab_ref.dtype)

            def emit_du(rows, du, lanes=lanes):
                dac_ref[rows, lanes] = (du * au_ref[rows, lanes]).astype(dac_ref.dtype)
                dau_ref[rows, lanes] = (du * ac_ref[rows, lanes]).astype(dau_ref.dtype)

            _conv_pass(t, taps, w_ref, lanes, lambda rows, lanes=lanes: ac_ref[rows, lanes] * au_ref[rows, lanes], emit_s,
                       get_dy=ds, dw_acc=dw_acc)
            _conv_pass_t(t, taps, w_ref, lanes, ds, emit_du)
            _finish_taps(dw_ref, dw_acc, lanes, taps)

    blk = pl.BlockSpec((t, cb), lambda j: (0, j))
    return pl.pallas_call(
        body, grid=(nb,),
        in_specs=[pl.BlockSpec((t, cb), lambda j, o=o: (0, o * nb + j)) for o in range(3)]
        + [pl.BlockSpec((taps, cb), lambda j: (0, j)), blk],
        out_specs=[blk, blk, blk, pl.BlockSpec((taps, cb), lambda j: (0, j))],
        out_shape=[_sds((t, d), BF16)] * 3 + [_sds((taps, d), F32)],
        scratch_shapes=[pltpu.VMEM((taps, SUBLANES, LANES), F32)],
        compiler_params=_params(), name="conv_a_bwd")(za, za, za, w, dpa)


def _conv_c_fwd(zcg, w, bias):
    t, d = zcg.shape[0], w.shape[1]
    taps, cb = w.shape[0], _chan_block(d)
    nb = d // cb
    assert taps - 1 <= CONV_HALO and t % CONV_ROWS == 0

    def body(cv_ref, cg_ref, w_ref, b_ref, o_ref):
        for lanes in _lane_groups(cb):
            def emit(rows, u, s, lanes=lanes):
                o_ref[rows, lanes] = s + b_ref[:, lanes]

            _conv_pass(t, taps, w_ref, lanes,
                       lambda rows, lanes=lanes: cv_ref[rows, lanes] * jax.nn.sigmoid(cg_ref[rows, lanes]), emit)

    return pl.pallas_call(
        body, grid=(nb,),
        in_specs=[pl.BlockSpec((t, cb), lambda j, o=o: (0, o * nb + j)) for o in range(2)]
        + [pl.BlockSpec((taps, cb), lambda j: (0, j)), pl.BlockSpec((1, cb), lambda j: (0, j))],
        out_specs=pl.BlockSpec((t, cb), lambda j: (0, j)),
        out_shape=_sds((t, d), F32), compiler_params=_params(), name="conv_c_fwd")(zcg, zcg, w, bias)


def _conv_c_bwd(zcg, w, du2):
    t, d = zcg.shape[0], w.shape[1]
    taps, cb = w.shape[0], _chan_block(d)
    nb = d // cb

    def body(cv_ref, cg_ref, w_ref, dy_ref, dcv_ref, dcg_ref, dw_ref, db_ref, dw_acc, db_acc):
        for lanes in _lane_groups(cb):
            dw_acc[...] = jnp.zeros_like(dw_acc)
            db_acc[...] = jnp.zeros_like(db_acc)

            def dy(rows, lanes=lanes):
                return dy_ref[rows, lanes]

            def emit_s(rows, u, s, lanes=lanes):
                db_acc[...] += jnp.sum(dy_ref[rows, lanes].reshape(CONV_ROWS // SUBLANES, SUBLANES, LANES), axis=0)

            def emit_du(rows, du, lanes=lanes):
                cv, sg = cv_ref[rows, lanes], jax.nn.sigmoid(cg_ref[rows, lanes])
                dcv_ref[rows, lanes] = (du * sg).astype(dcv_ref.dtype)
                dcg_ref[rows, lanes] = (du * cv * sg * (1.0 - sg)).astype(dcg_ref.dtype)

            _conv_pass(t, taps, w_ref, lanes,
                       lambda rows, lanes=lanes: cv_ref[rows, lanes] * jax.nn.sigmoid(cg_ref[rows, lanes]), emit_s,
                       get_dy=dy, dw_acc=dw_acc)
            _conv_pass_t(t, taps, w_ref, lanes, dy, emit_du)
            _finish_taps(dw_ref, dw_acc, lanes, taps)
            db_ref[:, lanes] = jnp.sum(db_acc[...], axis=0, keepdims=True)

    blk = pl.BlockSpec((t, cb), lambda j: (0, j))
    return pl.pallas_call(
        body, grid=(nb,),
        in_specs=[pl.BlockSpec((t, cb), lambda j, o=o: (0, o * nb + j)) for o in range(2)]
        + [pl.BlockSpec((taps, cb), lambda j: (0, j)), blk],
        out_specs=[blk, blk, pl.BlockSpec((taps, cb), lambda j: (0, j)), pl.BlockSpec((1, cb), lambda j: (0, j))],
        out_shape=[_sds((t, d), BF16)] * 2 + [_sds((taps, d), F32), _sds((1, d), F32)],
        scratch_shapes=[pltpu.VMEM((taps, SUBLANES, LANES), F32), pltpu.VMEM((SUBLANES, LANES), F32)],
        compiler_params=_params(), name="conv_c_bwd")(zcg, zcg, w, du2)


def _tri(tb, t, i, lower):
    r = lax.broadcasted_iota(jnp.int32, (tb, t), 0) + i * tb
    c = lax.broadcasted_iota(jnp.int32, (tb, t), 1)
    return ((r >= c) if lower else (c >= r)).astype(F32)


def _fox_cumsum(zf, bf):
    t = zf.shape[0]
    tb = min(256, t)

    def body(z_ref, b_ref, c_ref):
        logf = jax.nn.log_sigmoid(z_ref[...] + b_ref[...])
        c_ref[...] = jnp.dot(_tri(tb, t, pl.program_id(0), True), logf, precision=lax.Precision.HIGHEST,
                             preferred_element_type=F32)

    return pl.pallas_call(
        body, grid=(t // tb,), in_specs=[_full_spec(zf), _full_spec(bf)],
        out_specs=pl.BlockSpec((tb, LANES), lambda i: (i, 0)), out_shape=_sds((t, LANES), F32),
        compiler_params=_params(), name="fox_cumsum")(zf, bf)


def _fox_cumsum_bwd(zf, bf, dcc, dcr_t):
    t = zf.shape[0]
    tb = min(256, t)

    def body(z_ref, b_ref, dcc_ref, dcr_ref, dz_ref, db_ref, dc_ref):
        i = pl.program_id(0)

        @pl.when(i == 0)
        def _():
            dc_ref[...] = dcr_ref[...] + jnp.sum(dcc_ref[...], axis=0)
            db_ref[...] = jnp.zeros_like(db_ref)

        dlogf = jnp.dot(_tri(tb, t, i, False), dc_ref[...], precision=lax.Precision.HIGHEST, preferred_element_type=F32)
        dz = dlogf * jax.nn.sigmoid(-(z_ref[...] + b_ref[...]))
        dz_ref[...] = dz.astype(dz_ref.dtype)
        db_ref[...] += jnp.sum(dz, axis=0, keepdims=True)

    return pl.pallas_call(
        body, grid=(t // tb,),
        in_specs=[pl.BlockSpec((tb, LANES), lambda i: (i, 0)), _full_spec(bf), _full_spec(dcc), _full_spec(dcr_t)],
        out_specs=[pl.BlockSpec((tb, LANES), lambda i: (i, 0)), pl.BlockSpec((1, LANES), lambda i: (0, 0))],
        out_shape=[_sds((t, LANES), BF16), _sds((1, LANES), F32)],
        scratch_shapes=[pltpu.VMEM((t, LANES), F32)],
        compiler_params=_params(), name="fox_cumsum_bwd")(zf, bf, dcc, dcr_t)


def _fox_block(q, k, v, cc, cr, *, p, qi, tq, t, hd):
    nh = LANES // hd
    lane = lax.broadcasted_iota(jnp.int32, (1, LANES), 1)
    rowi = lax.broadcasted_iota(jnp.int32, (tq, t), 0) + qi * tq
    coli = lax.broadcasted_iota(jnp.int32, (tq, t), 1)
    kb, vb = k.astype(BF16), v.astype(BF16)
    out = jnp.zeros((tq, LANES), F32)
    for h in range(nh):
        m = (lane // hd) == h
        qh = jnp.where(m, q, 0.0).astype(BF16)
        s = lax.dot_general(qh, kb, (((1,), (1,)), ((), ())), preferred_element_type=F32) * (hd ** -0.5)
        c_q = jnp.sum(jnp.where(lane == p * nh + h, cc, 0.0), axis=1, keepdims=True)
        s = jnp.where(rowi >= coli, s + (c_q - cr[h]), -1e30)
        e = jnp.exp(s - lax.stop_gradient(jnp.max(s, axis=1, keepdims=True)))
        prob = e / jnp.sum(e, axis=1, keepdims=True)
        o = jnp.dot(prob.astype(BF16), vb, preferred_element_type=F32)
        out = out + jnp.where(m, o, 0.0)
    return out


def _fox_dims(zqkv, crow):
    t, d = zqkv.shape[0], zqkv.shape[1] // 3
    heads = crow.shape[0]
    hd = d // heads
    return t, d, hd, d // LANES, LANES // hd, min(256, t)


def _fox_fwd(zqkv, c, crow):
    t, d, hd, nblk, nh, tq = _fox_dims(zqkv, crow)

    def body(q_ref, k_ref, v_ref, cc_ref, cr_ref, o_ref):
        o_ref[...] = _fox_block(q_ref[...].astype(F32), k_ref[...], v_ref[...], cc_ref[...], cr_ref[...],
                                p=pl.program_id(0), qi=pl.program_id(1), tq=tq, t=t, hd=hd).astype(o_ref.dtype)

    return pl.pallas_call(
        body, grid=(nblk, t // tq),
        in_specs=[pl.BlockSpec((tq, LANES), lambda p, i: (i, p)), pl.BlockSpec((t, LANES), lambda p, i: (0, nblk + p)),
                  pl.BlockSpec((t, LANES), lambda p, i: (0, 2 * nblk + p)), pl.BlockSpec((tq, LANES), lambda p, i: (i, 0)),
                  pl.BlockSpec((nh, 1, t), lambda p, i: (p, 0, 0))],
        out_specs=pl.BlockSpec((tq, LANES), lambda p, i: (i, p)), out_shape=_sds((t, d), BF16),
        compiler_params=_params(), name="fox_fwd")(zqkv, zqkv, zqkv, c, crow)


def _fox_bwd(zqkv, c, crow, do):
    t, d, hd, nblk, nh, tq = _fox_dims(zqkv, crow)

    def body(q_ref, k_ref, v_ref, cc_ref, cr_ref, do_ref, dq_ref, dk_ref, dv_ref, dcc_ref, dcr_ref):
        qi = pl.program_id(1)
        f = functools.partial(_fox_block, p=pl.program_id(0), qi=qi, tq=tq, t=t, hd=hd)
        _, vjp = jax.vjp(f, q_ref[...].astype(F32), k_ref[...].astype(F32), v_ref[...].astype(F32), cc_ref[...], cr_ref[...])
        dq, dk, dv, dcc, dcr = vjp(do_ref[...].astype(F32))
        dq_ref[...] = dq.astype(dq_ref.dtype)
        dcc_ref[...] = dcc

        @pl.when(qi == 0)
        def _():
            dk_ref[...] = jnp.zeros_like(dk_ref)
            dv_ref[...] = jnp.zeros_like(dv_ref)
            dcr_ref[...] = jnp.zeros_like(dcr_ref)

        dk_ref[...] += dk
        dv_ref[...] += dv
        dcr_ref[...] += dcr

    qblk = pl.BlockSpec((tq, LANES), lambda p, i: (i, p))
    kblk = pl.BlockSpec((t, LANES), lambda p, i: (0, p))
    return pl.pallas_call(
        body, grid=(nblk, t // tq),
        in_specs=[qblk, pl.BlockSpec((t, LANES), lambda p, i: (0, nblk + p)),
                  pl.BlockSpec((t, LANES), lambda p, i: (0, 2 * nblk + p)), pl.BlockSpec((tq, LANES), lambda p, i: (i, 0)),
                  pl.BlockSpec((nh, 1, t), lambda p, i: (p, 0, 0)), qblk],
        out_specs=[qblk, kblk, kblk, pl.BlockSpec((None, tq, LANES), lambda p, i: (p, i, 0)),
                   pl.BlockSpec((nh, 1, t), lambda p, i: (p, 0, 0))],
        out_shape=[_sds((t, d), BF16), _sds((t, d), F32), _sds((t, d), F32), _sds((nblk, t, LANES), F32),
                   _sds(crow.shape, F32)],
        compiler_params=_params(), name="fox_bwd")(zqkv, zqkv, zqkv, c, crow, do)


def _xattn_block(q, k, v):
    s = lax.dot_general(q.astype(BF16), k.astype(BF16), (((1,), (1,)), ((), ())), preferred_element_type=F32)
    s = s * (q.shape[1] ** -0.5)
    e = jnp.exp(s - lax.stop_gradient(jnp.max(s, axis=1, keepdims=True)))
    prob = e / jnp.sum(e, axis=1, keepdims=True)
    return jnp.dot(prob.astype(BF16), v.astype(BF16), preferred_element_type=F32)


def _xattn_fwd(qx, kv):
    t, dx = qx.shape
    nm, nh, tq = kv.shape[0], qx.shape[1] // LANES, min(512, qx.shape[0])

    def body(q_ref, k_ref, v_ref, o_ref):
        o_ref[...] = _xattn_block(q_ref[...], k_ref[...], v_ref[...]).astype(o_ref.dtype)

    return pl.pallas_call(
        body, grid=(nh, t // tq),
        in_specs=[pl.BlockSpec((tq, LANES), lambda h, i: (i, h)), pl.BlockSpec((nm, LANES), lambda h, i: (0, h)),
                  pl.BlockSpec((nm, LANES), lambda h, i: (0, nh + h))],
        out_specs=pl.BlockSpec((tq, LANES), lambda h, i: (i, h)), out_shape=_sds((t, dx), BF16),
        compiler_params=_params(), name="xattn_fwd")(qx, kv, kv)


def _xattn_bwd(qx, kv, do):
    t, dx = qx.shape
    nm, nh, tq = kv.shape[0], qx.shape[1] // LANES, min(512, qx.shape[0])

    def body(q_ref, k_ref, v_ref, do_ref, dq_ref, dk_ref, dv_ref):
        _, vjp = jax.vjp(_xattn_block, q_ref[...].astype(F32), k_ref[...].astype(F32), v_ref[...].astype(F32))
        dq, dk, dv = vjp(do_ref[...].astype(F32))
        dq_ref[...] = dq.astype(dq_ref.dtype)

        @pl.when(pl.program_id(1) == 0)
        def _():
            dk_ref[...] = jnp.zeros_like(dk_ref)
            dv_ref[...] = jnp.zeros_like(dv_ref)

        dk_ref[...] += dk
        dv_ref[...] += dv

    qblk = pl.BlockSpec((tq, LANES), lambda h, i: (i, h))
    kblk = pl.BlockSpec((nm, LANES), lambda h, i: (0, h))
    return pl.pallas_call(
        body, grid=(nh, t // tq),
        in_specs=[qblk, kblk, pl.BlockSpec((nm, LANES), lambda h, i: (0, nh + h)), qblk],
        out_specs=[qblk, kblk, kblk],
        out_shape=[_sds((t, dx), BF16), _sds((nm, dx), F32), _sds((nm, dx), F32)],
        compiler_params=_params(), name="xattn_bwd")(qx, kv, kv, do)


def _row_tile(rows, cols, budget=1 << 20):
    tb = rows
    while tb % (4 * SUBLANES) == 0 and tb * cols * 4 > budget:
        tb //= 2
    return tb


def _adamw(w, g, m, v):
    shape = w.shape
    cols = shape[-1]
    w2, g2, m2, v2 = (a.reshape(-1, cols) for a in (w, g, m, v))
    rows = w2.shape[0]
    tb = _row_tile(rows, cols)

    def body(w_ref, g_ref, m_ref, v_ref, d_ref, nm_ref, nv_ref):
        gg = g_ref[...]
        nm = ADAM_B1 * m_ref[...] + (1.0 - ADAM_B1) * gg
        nv = ADAM_B2 * v_ref[...] + (1.0 - ADAM_B2) * (gg * gg)
        m_hat = nm / (1.0 - ADAM_B1 ** ADAM_STEP)
        v_hat = nv / (1.0 - ADAM_B2 ** ADAM_STEP)
        d_ref[...] = -ADAM_LR * (m_hat / (jnp.sqrt(v_hat) + ADAM_EPS) + ADAM_WD * w_ref[...])
        nm_ref[...] = nm
        nv_ref[...] = nv

    blk = pl.BlockSpec((tb, cols), lambda i: (i, 0))
    out = pl.pallas_call(
        body, grid=(rows // tb,), in_specs=[blk] * 4, out_specs=[blk] * 3, out_shape=[_sds((rows, cols), F32)] * 3,
        compiler_params=_params(), name="adamw")(w2, g2, m2, v2)
    return tuple(o.reshape(shape) for o in out)


def _sum_devices(gathered, rows):
    cols = gathered.shape[1]

    def body(g_ref, o_ref):
        acc = g_ref[0:rows, :]
        for dev in range(1, N_DEV):
            acc = acc + g_ref[dev * rows:(dev + 1) * rows, :]
        o_ref[...] = acc

    return pl.pallas_call(body, out_shape=_sds((rows, cols), F32), compiler_params=_params(), name="sum_devices")(gathered)


def _place():
    x, y, c = lax.axis_index("x"), lax.axis_index("y"), lax.axis_index("c")
    chips = [(1 - x, y), (x, 1 - y), (1 - x, 1 - y)]
    return x, y, c, chips


def _chip_id(chip):
    return 2 * chip[0] + chip[1]


def _half(c, rows):
    rh = rows // 2
    return pl.ds(pl.multiple_of(c * rh, 16), rh)


_HBM = pl.BlockSpec(memory_space=pltpu.HBM)


def _allgather8(x_shard):
    m_per, n = x_shard.shape

    def body(x_ref, out_ref, send_sems, recv_sems, local_sem):
        x, y, c, chips = _place()
        me, sibling = (x, y, c), (x, y, 1 - c)

        def rows(px, py, pc):
            return out_ref.at[pl.ds((4 * px + 2 * py + pc) * m_per, m_per), :]

        def copy(k, block, to, src=None):
            return pltpu.make_async_remote_copy(
                src_ref=rows(*block) if src is None else src, dst_ref=rows(*block),
                send_sem=send_sems.at[k], recv_sem=recv_sems.at[k], device_id=to, device_id_type=MESH)

        mine = pltpu.make_async_copy(x_ref, rows(*me), local_sem)
        mine.start()
        first = [copy(0, me, sibling, src=x_ref)]
        first += [copy(1 + j, me, (*chip, c), src=x_ref) for j, chip in enumerate(chips)]
        for cp in first:
            cp.start()
        passed = [copy(4 + j, (*chip, c), sibling) for j, chip in enumerate(chips)]
        for j, chip in enumerate(chips):
            copy(1 + j, (*chip, c), me).wait_recv()
            passed[j].start()
        copy(0, sibling, me).wait_recv()
        for j, chip in enumerate(chips):
            copy(4 + j, (*chip, 1 - c), me).wait_recv()
        for cp in first + passed:
            cp.wait_send()
        mine.wait()

    return pl.pallas_call(
        body, out_shape=_sds((N_DEV * m_per, n), x_shard.dtype),
        in_specs=[pl.BlockSpec(memory_space=pltpu.VMEM)], out_specs=pl.BlockSpec(memory_space=pltpu.VMEM),
        scratch_shapes=[pltpu.SemaphoreType.DMA((7,)), pltpu.SemaphoreType.DMA((7,)), pltpu.SemaphoreType.DMA],
        compiler_params=_params(), name="allgather8")(x_shard)


def _gather_weights(ws):
    n = len(ws)

    def body(*refs):
        w, out = refs[:n], refs[n:2 * n]
        send_i, recv_i, send_d, recv_d, loc = refs[2 * n:]
        x, y, c, chips = _place()
        sibling = (x, y, 1 - c)
        me_chip = 2 * x + y

        def slab(i, chip, half):
            return out[i].at[:, chip, _half(half, w[i].shape[1]), :]

        def over_ici(i, j, src_chip, to, src=None):
            return pltpu.make_async_remote_copy(
                src_ref=slab(i, src_chip, c) if src is None else src, dst_ref=slab(i, src_chip, c),
                send_sem=send_i.at[3 * i + j], recv_sem=recv_i.at[3 * i + j], device_id=to, device_id_type=MESH)

        def over_d2d(i, j, half):
            return pltpu.make_async_remote_copy(
                src_ref=slab(i, _chip_id(chips[j]), half), dst_ref=slab(i, _chip_id(chips[j]), half),
                send_sem=send_d.at[3 * i + j], recv_sem=recv_d.at[3 * i + j], device_id=sibling, device_id_type=MESH)

        own = [pltpu.make_async_copy(w[i], out[i].at[:, me_chip], loc.at[i]) for i in range(n)]
        for cp in own:
            cp.start()
        sent = []
        for i in range(n):
            for j, chip in enumerate(chips):
                cp = over_ici(i, j, me_chip, (*chip, c), src=w[i].at[:, _half(c, w[i].shape[1]), :])
                cp.start()
                sent.append(cp)
        for i in range(n):
            for j, chip in enumerate(chips):
                over_ici(i, j, _chip_id(chip), (x, y, c)).wait_recv()
                cp = over_d2d(i, j, c)
                cp.start()
                sent.append(cp)
        for i in range(n):
            for j in range(3):
                over_d2d(i, j, 1 - c).wait_recv()
        for cp in sent:
            cp.wait_send()
        for cp in own:
            cp.wait()

    return pl.pallas_call(
        body, in_specs=[_HBM] * n, out_specs=[_HBM] * n,
        out_shape=[_sds((a.shape[0], N_CHIPS) + a.shape[1:], a.dtype) for a in ws],
        scratch_shapes=[pltpu.SemaphoreType.DMA((3 * n,)) for _ in range(4)] + [pltpu.SemaphoreType.DMA((n,))],
        compiler_params=_params(), name="gather_weights")(*ws)


def _swap_halves(gs):
    n = len(gs)

    def body(*refs):
        g, out = refs[:n], refs[n:2 * n]
        send, recv = refs[2 * n:]
        x, y, c, _ = _place()
        cps = [pltpu.make_async_remote_copy(
            src_ref=g[i].at[:, :, _half(1 - c, g[i].shape[2]), :], dst_ref=out[i],
            send_sem=send.at[i], recv_sem=recv.at[i], device_id=(x, y, 1 - c), device_id_type=MESH) for i in range(n)]
        for cp in cps:
            cp.start()
        for cp in cps:
            cp.wait()

    return pl.pallas_call(
        body, in_specs=[_HBM] * n, out_specs=[_HBM] * n,
        out_shape=[_sds(a.shape[:2] + (a.shape[2] // 2, a.shape[3]), a.dtype) for a in gs],
        scratch_shapes=[pltpu.SemaphoreType.DMA((n,)), pltpu.SemaphoreType.DMA((n,))],
        compiler_params=_params(), name="swap_halves")(*gs)


def _exchange_chips(ss):
    n = len(ss)

    def body(*refs):
        s, out = refs[:n], refs[n:2 * n]
        send, recv = refs[2 * n:]
        x, y, c, chips = _place()
        cps = []
        for i in range(n):
            for j, chip in enumerate(chips):
                cps.append(pltpu.make_async_remote_copy(
                    src_ref=s[i].at[:, _chip_id(chip)], dst_ref=out[i].at[j],
                    send_sem=send.at[3 * i + j], recv_sem=recv.at[3 * i + j], device_id=(*chip, c), device_id_type=MESH))
        for cp in cps:
            cp.start()
        for cp in cps:
            cp.wait()

    return pl.pallas_call(
        body, in_specs=[_HBM] * n, out_specs=[_HBM] * n,
        out_shape=[_sds((3, a.shape[0]) + a.shape[2:], a.dtype) for a in ss],
        scratch_shapes=[pltpu.SemaphoreType.DMA((3 * n,)), pltpu.SemaphoreType.DMA((3 * n,))],
        compiler_params=_params(), name="exchange_chips")(*ss)


def _join_halves(fs):
    n = len(fs)

    def body(*refs):
        f, out = refs[:n], refs[n:2 * n]
        send, recv, loc = refs[2 * n:]
        x, y, c, _ = _place()
        rows = [2 * f[i].shape[1] for i in range(n)]
        own = [pltpu.make_async_copy(f[i], out[i].at[:, _half(c, rows[i]), :], loc.at[i]) for i in range(n)]
        for cp in own:
            cp.start()
        for i in range(n):
            pltpu.make_async_remote_copy(
                src_ref=f[i], dst_ref=out[i].at[:, _half(c, rows[i]), :],
                send_sem=send.at[i], recv_sem=recv.at[i], device_id=(x, y, 1 - c), device_id_type=MESH).start()
        for i in range(n):
            arrive = pltpu.make_async_remote_copy(
                src_ref=f[i], dst_ref=out[i].at[:, _half(1 - c, rows[i]), :],
                send_sem=send.at[i], recv_sem=recv.at[i], device_id=(x, y, 1 - c), device_id_type=MESH)
            arrive.wait_recv()
            arrive.wait_send()
        for cp in own:
            cp.wait()

    return pl.pallas_call(
        body, in_specs=[_HBM] * n, out_specs=[_HBM] * n,
        out_shape=[_sds((a.shape[0], 2 * a.shape[1], a.shape[2]), a.dtype) for a in fs],
        scratch_shapes=[pltpu.SemaphoreType.DMA((n,)), pltpu.SemaphoreType.DMA((n,)), pltpu.SemaphoreType.DMA((n,))],
        compiler_params=_params(), name="join_halves")(*fs)


def _add_sibling(g, got, c_idx):
    nl, ns, r, cols = g.shape
    rh = r // 2
    tb = _row_tile(rh, cols)
    nb = rh // tb
    g3, got3 = g.reshape(nl * ns, r, cols), got.reshape(nl * ns, rh, cols)

    def body(c_ref, a_ref, b_ref, o_ref):
        o_ref[...] = (a_ref[...].astype(F32) + b_ref[...].astype(F32)).astype(o_ref.dtype)

    out = pl.pallas_call(
        body,
        grid_spec=pltpu.PrefetchScalarGridSpec(
            num_scalar_prefetch=1, grid=(nl * ns, nb),
            in_specs=[pl.BlockSpec((None, tb, cols), lambda s, i, c_ref: (s, c_ref[0] * nb + i, 0)),
                      pl.BlockSpec((None, tb, cols), lambda s, i, c_ref: (s, i, 0))],
            out_specs=pl.BlockSpec((None, tb, cols), lambda s, i, c_ref: (s, i, 0))),
        out_shape=_sds((nl * ns, rh, cols), BF16), compiler_params=_params(), name="add_sibling")(c_idx, g3, got3)
    return out.reshape(nl, ns, rh, cols)


def _add_chips(s, got, chip_idx):
    nl, ns, rh, cols = s.shape
    tb = _row_tile(rh, cols)

    def body(k_ref, a_ref, b0_ref, b1_ref, b2_ref, o_ref):
        o_ref[...] = ((a_ref[...].astype(F32) + b0_ref[...].astype(F32)) + b1_ref[...].astype(F32)) + b2_ref[...].astype(F32)

    return pl.pallas_call(
        body,
        grid_spec=pltpu.PrefetchScalarGridSpec(
            num_scalar_prefetch=1, grid=(nl, rh // tb),
            in_specs=[pl.BlockSpec((None, None, tb, cols), lambda l, i, k_ref: (l, k_ref[0], i, 0))]
            + [pl.BlockSpec((None, None, tb, cols), lambda l, i, k_ref, j=j: (j, l, i, 0)) for j in range(3)],
            out_specs=pl.BlockSpec((None, tb, cols), lambda l, i, k_ref: (l, i, 0))),
        out_shape=_sds((nl, rh, cols), F32), compiler_params=_params(), name="add_chips")(chip_idx, s, got, got, got)


def _pad_lanes(a, width=LANES):
    return jnp.pad(a, ((0, 0), (0, width - a.shape[1])))


def _local_step(x, mem, tgt, sp, wt):
    t, d = x.shape
    depth = sp["mix_norm"].shape[0]
    heads = sp["b_forget"].shape[1]
    f = wt["down"][0].shape[0]
    row = lambda a, l: a[l:l + 1]

    memn, = _rows("rms_mem", _rms_fn, [_part(mem)], [sp["mem_norm"][None]], [(d, BF16)])
    saved = []
    for l in range(depth):
        sv = {"x0": x}
        h1, = _rows("rms_mix", _rms_fn, [_part(x)], [row(sp["mix_norm"], l)], [(d, BF16)])
        wm = wt["main"][l]
        za = _mm("mm_in_a", h1, [_part(wm, 0, 3 * d)])
        zqkv = _mm("mm_in_qkv", h1, [_part(wm, 3 * d, 3 * d)], out_dtype=BF16)
        zcg = _mm("mm_in_cg", h1, [_part(wm, 6 * d, 5 * d)])
        zf = _mm("mm_in_f", h1, wt["f"][l])
        bf = _pad_lanes(row(sp["b_forget"], l))
        c = _fox_cumsum(zf, bf)
        crow = c[:, :heads].T.reshape(heads, 1, t)
        o = _fox_fwd(zqkv, c, crow)
        pa = _conv_a_fwd(za, sp["conv_a"][l])
        u2 = _conv_c_fwd(zcg, sp["conv_c"][l], row(sp["conv_c_bias"], l))
        u3, = _rows("ln_silu", _lnsilu_fn, [_part(u2)], [row(sp["ln_c_gain"], l), row(sp["ln_c_bias"], l)], [(d, BF16)])
        ya = _mm("mm_out_a", pa, wt["out_a"][l])
        yb = _mm("mm_out_b", o, wt["out_b"][l])
        yc = _mm("mm_out_c", u3, wt["out_c"][l])
        bg = [sp["b_gate"][l:l + 1, k * d:(k + 1) * d] for k in range(3)]
        gate_rows = [_part(zcg, (2 + k) * d, d) for k in range(3)] + [_part(ya), _part(yb), _part(yc)]
        mg, = _rows("merge", _merge_fn, gate_rows, bg, [(d, BF16)])
        x1 = _mm("mm_o", mg, wt["o"][l], add=x)
        h2, = _rows("rms_x", _rms_fn, [_part(x1)], [row(sp["xattn_norm"], l)], [(d, BF16)])
        qx = _mm("mm_xq", h2, wt["xq"][l], out_dtype=BF16)
        kv = _mm("mm_xkv", memn, wt["xkv"][l], out_dtype=BF16)
        ox = _xattn_fwd(qx, kv)
        x2 = _mm("mm_xo", ox, wt["xo"][l], add=x1)
        h3, = _rows("rms_ffn", _rms_fn, [_part(x2)], [row(sp["ffn_norm"], l)], [(d, BF16)])
        gu = _mm("mm_gu", h3, wt["gu"][l])
        act, = _rows("swiglu", _swiglu_fn, [_part(gu, 0, f), _part(gu, f, f)], [], [(f, BF16)])
        x3 = _mm("mm_down", act, wt["down"][l], add=x2)
        sv.update(h1=h1, za=za, zqkv=zqkv, zcg=zcg, zf=zf, bf=bf, c=c, crow=crow, o=o, pa=pa, u2=u2, u3=u3, ya=ya, yb=yb,
                  yc=yc, bg=bg, gate_rows=gate_rows, mg=mg, x1=x1, h2=h2, qx=qx, kv=kv, ox=ox, x2=x2, h3=h3, gu=gu, act=act)
        saved.append(sv)
        x = x3

    loss_row, dx, d_final = _loss_call(x, tgt, sp["final_norm"][None])

    gs = {k: [None] * depth for k in ("mix_norm", "b_gate", "b_forget", "conv_a", "conv_c", "conv_c_bias", "ln_c_gain",
                                      "ln_c_bias", "xattn_norm", "ffn_norm")}
    gw = {k: [None] * depth for k in ("main", "f", "out_a", "out_b", "out_c", "o", "xq", "xkv", "xo", "gu", "down")}
    dmemn = None
    for l in reversed(range(depth)):
        sv = saved[l]
        dact = _mm("mm_down_dx", dx, wt["down"][l], tb=True)
        gw["down"][l] = _mm("mm_down_dw", sv["act"], dx, ta=True, out_dtype=BF16)
        (dg, du), _ = _rows_vjp("swiglu_bwd", _swiglu_fn, [_part(sv["gu"], 0, f), _part(sv["gu"], f, f)], [], [_part(dact)],
                                [BF16, BF16], [])
        dgu = [_part(dg), _part(du)]
        dh3 = _mm("mm_gu_dx", dgu, wt["gu"][l], tb=True)
        gw["gu"][l] = _mm("mm_gu_dw", sv["h3"], dgu, ta=True, out_dtype=BF16)
        (dx2,), (gs["ffn_norm"][l],) = _rows_vjp("rms_ffn_bwd", _rms_fn, [_part(sv["x2"])], [row(sp["ffn_norm"], l)],
                                                 [_part(dh3)], [F32], [True], adds={0: _part(dx)})
        dox = _mm("mm_xo_dx", dx2, wt["xo"][l], tb=True)
        gw["xo"][l] = _mm("mm_xo_dw", sv["ox"], dx2, ta=True, out_dtype=BF16)
        dqx, dkx, dvx = _xattn_bwd(sv["qx"], sv["kv"], dox)
        dh2 = _mm("mm_xq_dx", dqx, wt["xq"][l], tb=True)
        gw["xq"][l] = _mm("mm_xq_dw", sv["h2"], dqx, ta=True, out_dtype=BF16)
        dkv = [_part(dkx), _part(dvx)]
        dmemn = _mm("mm_xkv_dx", dkv, wt["xkv"][l], tb=True, add=dmemn)
        gw["xkv"][l] = _mm("mm_xkv_dw", memn, dkv, ta=True, out_dtype=BF16)
        (dx1,), (gs["xattn_norm"][l],) = _rows_vjp("rms_x_bwd", _rms_fn, [_part(sv["x1"])], [row(sp["xattn_norm"], l)],
                                                   [_part(dh2)], [F32], [True], adds={0: _part(dx2)})
        dmg = _mm("mm_o_dx", dx1, wt["o"][l], tb=True)
        gw["o"][l] = _mm("mm_o_dw", sv["mg"], dx1, ta=True, out_dtype=BF16)
        (dga, dgb, dgc, dya, dyb, dyc), dbg = _rows_vjp("merge_bwd", _merge_fn, sv["gate_rows"], sv["bg"], [_part(dmg)],
                                                        [BF16] * 6, [True] * 3)
        gs["b_gate"][l] = jnp.concatenate(dbg, axis=1)
        dpa = _mm("mm_out_a_dx", dya, wt["out_a"][l], tb=True)
        gw["out_a"][l] = _mm("mm_out_a_dw", sv["pa"], dya, ta=True, out_dtype=BF16)
        do = _mm("mm_out_b_dx", dyb, wt["out_b"][l], tb=True)
        gw["out_b"][l] = _mm("mm_out_b_dw", sv["o"], dyb, ta=True, out_dtype=BF16)
        du3 = _mm("mm_out_c_dx", dyc, wt["out_c"][l], tb=True)
        gw["out_c"][l] = _mm("mm_out_c_dw", sv["u3"], dyc, ta=True, out_dtype=BF16)
        (du2,), (gs["ln_c_gain"][l], gs["ln_c_bias"][l]) = _rows_vjp(
            "ln_silu_bwd", _lnsilu_fn, [_part(sv["u2"])], [row(sp["ln_c_gain"], l), row(sp["ln_c_bias"], l)], [_part(du3)],
            [F32], [True, True])
        dcv, dcg, gs["conv_c"][l], gs["conv_c_bias"][l] = _conv_c_bwd(sv["zcg"], sp["conv_c"][l], du2)
        dab, dac, dau, gs["conv_a"][l] = _conv_a_bwd(sv["za"], sp["conv_a"][l], dpa)
        dq, dk, dv, dcc, dcr = _fox_bwd(sv["zqkv"], sv["c"], sv["crow"], do)
        dcr_t = _pad_lanes(dcr.reshape(heads, t).T)
        dzf, dbf = _fox_cumsum_bwd(sv["zf"], sv["bf"], dcc, dcr_t)
        gs["b_forget"][l] = dbf[:, :heads]
        dz = [_part(a) for a in (dab, dac, dau, dq, dk, dv, dcv, dcg, dga, dgb, dgc)]
        dh1f = _mm("mm_in_f_dx", dzf, wt["f"][l], tb=True)
        dh1 = _mm("mm_in_dx", dz, wt["main"][l], tb=True, add=dh1f)
        gw["main"][l] = _mm("mm_in_dw", sv["h1"], dz, ta=True, out_dtype=BF16)
        gw["f"][l] = _mm("mm_in_f_dw", sv["h1"], dzf, ta=True, out_dtype=BF16)
        (dx,), (gs["mix_norm"][l],) = _rows_vjp("rms_mix_bwd", _rms_fn, [_part(sv["x0"])], [row(sp["mix_norm"], l)],
                                                [_part(dh1)], [F32], [True], adds={0: _part(dx1)})

    _, (d_mem_norm,) = _rows_vjp("rms_mem_bwd", _rms_fn, [_part(mem)], [sp["mem_norm"][None]], [_part(dmemn)], [None], [True])
    small = {k: jnp.stack([a.reshape(sp[k].shape[1:]) for a in v]) for k, v in gs.items()}
    small["mem_norm"] = d_mem_norm[0]
    small["final_norm"] = d_final[0]
    return loss_row, dx, small, gw


_BIG = ("w_in", "w_out_a", "w_out_b", "w_out_c", "w_o", "w_xq", "w_xkv", "w_xo", "w_gate_up", "w_down")
_COL_SHARDED = ("w_in", "w_xo", "w_gate_up")
_KEY = {"w_out_a": "out_a", "w_out_b": "out_b", "w_out_c": "out_c", "w_o": "o", "w_xq": "xq", "w_xkv": "xkv", "w_xo": "xo",
        "w_gate_up": "gu", "w_down": "down"}
_SMALL = ("mix_norm", "b_gate", "b_forget", "conv_c_bias", "ln_c_gain", "ln_c_bias", "xattn_norm", "ffn_norm", "mem_norm",
          "final_norm")
_SMALL_SHARDED = ("conv_a", "conv_c")


def _full_from_shards(name, g):
    if name in _COL_SHARDED:
        return g.transpose(1, 0, 2).reshape(g.shape[1], N_CHIPS * g.shape[2])
    return g.reshape(N_CHIPS * g.shape[1], g.shape[2])


def _shards_from_full(name, a):
    if name in _COL_SHARDED:
        return a.reshape(a.shape[0], N_CHIPS, a.shape[1] // N_CHIPS).transpose(1, 0, 2)
    return a.reshape(N_CHIPS, a.shape[0] // N_CHIPS, a.shape[1])


def _pack_rows(parts):
    padded = []
    for a in parts:
        pad = -a.shape[0] % SUBLANES
        padded.append(jnp.pad(a, ((0, pad), (0, 0))) if pad else a)
    return jnp.concatenate(padded, axis=0)


def _unpack_rows(packed, shapes):
    out, pos = [], 0
    for r in shapes:
        out.append(packed[pos:pos + r])
        pos += r + (-r % SUBLANES)
    return out


def kernel(x, mem, mix_norm, w_in, b_gate, b_forget, conv_a, w_out_a, w_out_b, conv_c, conv_c_bias, ln_c_gain, ln_c_bias, w_out_c, w_o, xattn_norm, mem_norm, w_xq, w_xkv, w_xo, ffn_norm, w_gate_up, w_down, final_norm, loss_target, m_mix_norm, m_w_in, m_b_gate, m_b_forget, m_conv_a, m_w_out_a, m_w_out_b, m_conv_c, m_conv_c_bias, m_ln_c_gain, m_ln_c_bias, m_w_out_c, m_w_o, m_xattn_norm, m_mem_norm, m_w_xq, m_w_xkv, m_w_xo, m_ffn_norm, m_w_gate_up, m_w_down, m_final_norm, v_mix_norm, v_w_in, v_b_gate, v_b_forget, v_conv_a, v_w_out_a, v_w_out_b, v_conv_c, v_conv_c_bias, v_ln_c_gain, v_ln_c_bias, v_w_out_c, v_w_o, v_xattn_norm, v_mem_norm, v_w_xq, v_w_xkv, v_w_xo, v_ffn_norm, v_w_gate_up, v_w_down, v_final_norm):
    args = dict(locals())
    names = ["mix_norm", "w_in", "b_gate", "b_forget", "conv_a", "w_out_a", "w_out_b", "conv_c", "conv_c_bias", "ln_c_gain",
             "ln_c_bias", "w_out_c", "w_o", "xattn_norm", "mem_norm", "w_xq", "w_xkv", "w_xo", "ffn_norm", "w_gate_up", "w_down",
             "final_norm"]
    depth, d = mix_norm.shape
    heads = b_forget.shape[1]
    cx, cy, cc = lax.axis_index("x"), lax.axis_index("y"), lax.axis_index("c")
    chip = 2 * cx + cy
    c_idx = jnp.reshape(cc, (1,)).astype(jnp.int32)
    chip_idx = jnp.reshape(chip, (1,)).astype(jnp.int32)

    gathered = _gather_weights([args[n].astype(BF16) for n in _BIG])
    gathered = dict(zip(_BIG, gathered))
    wt = {k: [] for k in ("main", "f", "out_a", "out_b", "out_c", "o", "xq", "xkv", "xo", "gu", "down")}
    for l in range(depth):
        full_in = _full_from_shards("w_in", gathered["w_in"][l])
        wt["main"].append(jnp.concatenate([full_in[:, :6 * d], full_in[:, 6 * d + heads:]], axis=1))
        wt["f"].append(_pad_lanes(full_in[:, 6 * d:6 * d + heads]))
        for n in _BIG[1:]:
            wt[_KEY[n]].append(_full_from_shards(n, gathered[n][l]))
    conv_rows = [conv_a.shape[1] * depth, conv_c.shape[1] * depth]
    conv_pack = _pack_rows([conv_a.reshape(conv_rows[0], -1), conv_c.reshape(conv_rows[1], -1)])
    conv_all = _allgather8(conv_pack).reshape(N_CHIPS, 2, conv_pack.shape[0], conv_pack.shape[1])[:, 0]
    conv_all = conv_all.transpose(1, 0, 2).reshape(conv_pack.shape[0], d)
    conv_a_full, conv_c_full = _unpack_rows(conv_all, conv_rows)
    sp = {n: args[n] for n in _SMALL}
    sp["conv_a"] = conv_a_full.reshape(depth, conv_a.shape[1], d)
    sp["conv_c"] = conv_c_full.reshape(depth, conv_c.shape[1], d)

    loss_row, grad_x, gsmall, gw = _local_step(x[0], mem[0], loss_target[0], sp, wt)

    gfull = {}
    for n in _BIG[1:]:
        gfull[n] = jnp.stack([_shards_from_full(n, gw[_KEY[n]][l]) for l in range(depth)])
    gin = []
    for l in range(depth):
        gm, gf = gw["main"][l], gw["f"][l]
        gin.append(_shards_from_full("w_in", jnp.concatenate([gm[:, :6 * d], gf[:, :heads], gm[:, 6 * d:]], axis=1)))
    gfull["w_in"] = jnp.stack(gin)
    local = [gfull[n] for n in _BIG]
    from_sibling = _swap_halves(local)
    chip_sums = [_add_sibling(g, r, c_idx) for g, r in zip(local, from_sibling)]
    from_chips = _exchange_chips(chip_sums)
    halves = [_add_chips(s, r, chip_idx) for s, r in zip(chip_sums, from_chips)]
    grads = dict(zip(_BIG, _join_halves(halves)))

    def as_rows(n, a):
        return _pad_lanes(a, d) if n == "b_forget" else a.reshape(-1, d)

    pieces = [as_rows(n, gsmall[n]) for n in _SMALL]
    for n in _SMALL_SHARDED:
        pieces.append(gsmall[n].reshape(-1, d))
    pieces.append(_pad_lanes(loss_row, d))
    pack = _pack_rows(pieces)
    total = _sum_devices(_allgather8(pack), pack.shape[0])
    shapes = [p.shape[0] for p in pieces]
    summed = _unpack_rows(total, shapes)
    for n, g in zip(_SMALL, summed):
        grads[n] = g[:, :heads] if n == "b_forget" else g.reshape(args[n].shape)
    for n, g in zip(_SMALL_SHARDED, summed[len(_SMALL):]):
        g = g.reshape(args[n].shape[:2] + (d,))
        grads[n] = lax.dynamic_slice_in_dim(g, chip * args[n].shape[2], args[n].shape[2], axis=2)
    loss = summed[-1][0, 0]

    delta, new_m, new_v = {}, {}, {}
    packed = [_pack_rows([as_rows(n, src[pre + n]) for n in _SMALL])
              for src, pre in ((args, ""), (grads, ""), (args, "m_"), (args, "v_"))]
    srows = [as_rows(n, args[n]).shape[0] for n in _SMALL]
    updated = [_unpack_rows(a, srows) for a in _adamw(*packed)]
    for k, n in enumerate(_SMALL):
        delta[n], new_m[n], new_v[n] = (
            (u[k][:, :heads] if n == "b_forget" else u[k]).reshape(args[n].shape) for u in updated)
    for n in _BIG + _SMALL_SHARDED:
        delta[n], new_m[n], new_v[n] = _adamw(args[n], grads[n], args["m_" + n], args["v_" + n])

    return (loss, grad_x[None], *[grads[n] for n in names], *[delta[n] for n in names], *[new_m[n] for n in names],
            *[new_v[n] for n in names])
```

```python
import functools
import math

import jax
import jax.numpy as jnp
from jax import lax
from jax.experimental import pallas as pl
from jax.experimental.pallas import tpu as pltpu

F32, BF16 = jnp.float32, jnp.bfloat16
EPS = 1e-6
LANES = 128
SUBLANES = 8
VMEM_LIMIT = 56 * 1024 * 1024
MM_VMEM_BUDGET = 36 * 1024 * 1024
MESH = pl.DeviceIdType.MESH
N_CHIPS = 4
N_DEV = 8

ADAM_LR, ADAM_B1, ADAM_B2, ADAM_EPS, ADAM_WD, ADAM_STEP = 0.001, 0.9, 0.999, 1e-08, 0.01, 10


def _params(**kw):
    return pltpu.CompilerParams(vmem_limit_bytes=VMEM_LIMIT, **kw)


def _sds(shape, dtype):
    return jax.ShapeDtypeStruct(tuple(shape), dtype)


def _part(arr, off=0, width=None):
    return (arr, off, arr.shape[1] - off if width is None else width)


def _unit(parts):
    u = 0
    for _, off, w in parts:
        u = math.gcd(u, math.gcd(off, w))
    return u


def _pick(dim, unit, cands, lane):
    for c in cands:
        if c <= dim and dim % c == 0 and unit % c == 0 and (not lane or c % LANES == 0):
            return c
    if not lane or dim % LANES == 0 or unit == dim:
        return dim
    raise ValueError(f"no tile for {dim} {unit}")


_COL_TILES = (1024, 1408, 512, 256, 128)
_ROW_TILES = (512, 256, 1408, 128)


def _mm(name, a, b, *, ta=False, tb=False, out_dtype=F32, add=None):
    if not isinstance(a, list):
        a = [_part(a)]
    if not isinstance(b, list):
        b = [_part(b)]
    a_rows, a_cols = a[0][0].shape[0], sum(p[2] for p in a)
    b_rows, b_cols = b[0][0].shape[0], sum(p[2] for p in b)
    M, Ka = (a_cols, a_rows) if ta else (a_rows, a_cols)
    Kb, N = (b_cols, b_rows) if tb else (b_rows, b_cols)
    assert Ka == Kb, (name, Ka, Kb)
    K = Ka
    ua, ub = _unit(a), _unit(b)
    tm = _pick(M, ua if ta else M, _ROW_TILES, ta)
    tn = _pick(N, N if tb else ub, _COL_TILES, not tb)
    ku = math.gcd(K if ta else ua, ub if tb else K)
    tk = _pick(K, ku, _COL_TILES, (not ta) or tb)
    a_size = sum(p[0].dtype.itemsize for p in a)
    b_size = sum(p[0].dtype.itemsize for p in b)

    def vmem_estimate():
        return (2 * tm * tk * a_size + 2 * tk * tn * b_size + tm * tn * (2 * jnp.dtype(out_dtype).itemsize + 4)
                + (8 * tm * tn if add is not None else 0))

    while vmem_estimate() > MM_VMEM_BUDGET:
        if tk % 256 == 0 and tk >= tn:
            tk //= 2
        elif tn % 256 == 0:
            tn //= 2
        elif tm % 256 == 0:
            tm //= 2
        elif tk % 256 == 0:
            tk //= 2
        else:
            break
    nm, nn, nk = M // tm, N // tn, K // tk
    a_tile, b_tile = (tm if ta else tk), (tk if tb else tn)
    a_nb, b_nb = a[0][2] // a_tile, b[0][2] // b_tile

    def col_index(parts, nb, tile, p, col):
        base = parts[p][1] // tile
        if len(parts) == 1:
            return base + col
        return base + jnp.clip(col - p * nb, 0, nb - 1)

    in_specs, operands = [], []
    for p in range(len(a)):
        if ta:
            in_specs.append(pl.BlockSpec((tk, tm), lambda mi, nj, kk, p=p: (kk, col_index(a, a_nb, a_tile, p, mi))))
        else:
            in_specs.append(pl.BlockSpec((tm, tk), lambda mi, nj, kk, p=p: (mi, col_index(a, a_nb, a_tile, p, kk))))
        operands.append(a[p][0])
    for p in range(len(b)):
        if tb:
            in_specs.append(pl.BlockSpec((tn, tk), lambda mi, nj, kk, p=p: (nj, col_index(b, b_nb, b_tile, p, kk))))
        else:
            in_specs.append(pl.BlockSpec((tk, tn), lambda mi, nj, kk, p=p: (kk, col_index(b, b_nb, b_tile, p, nj))))
        operands.append(b[p][0])
    if add is not None:
        in_specs.append(pl.BlockSpec((tm, tn), lambda mi, nj, kk: (mi, nj)))
        operands.append(add)
    dims = (((0 if ta else 1,), (1 if tb else 0,)), ((), ()))
    na, nb_ = len(a), len(b)

    def body(*refs):
        a_refs, b_refs = refs[:na], refs[na:na + nb_]
        pos = na + nb_
        add_ref = None
        if add is not None:
            add_ref = refs[pos]
            pos += 1
        o_ref = refs[pos]
        acc_ref = refs[pos + 1] if nk > 1 else None
        mi, nj, kk = pl.program_id(0), pl.program_id(1), pl.program_id(2)
        a_col = mi if ta else kk
        b_col = kk if tb else nj

        def finish(r):
            if add_ref is not None:
                r = r + add_ref[...].astype(F32)
            o_ref[...] = r.astype(o_ref.dtype)

        def step(a_ref, b_ref):
            r = lax.dot_general(a_ref[...].astype(BF16), b_ref[...].astype(BF16), dims, preferred_element_type=F32)
            if nk == 1:
                finish(r)
            else:
                @pl.when(kk == 0)
                def _():
                    acc_ref[...] = r

                @pl.when(kk > 0)
                def _():
                    acc_ref[...] += r

        for pa in range(na):
            for pb in range(nb_):
                if na * nb_ == 1:
                    step(a_refs[pa], b_refs[pb])
                else:
                    cond = jnp.logical_and(a_col // a_nb == pa, b_col // b_nb == pb)
                    pl.when(cond)(functools.partial(step, a_refs[pa], b_refs[pb]))
        if nk > 1:
            @pl.when(kk == nk - 1)
            def _():
                finish(acc_ref[...])

    return pl.pallas_call(
        body, grid=(nm, nn, nk), in_specs=in_specs,
        out_specs=pl.BlockSpec((tm, tn), lambda mi, nj, kk: (mi, nj)),
        out_shape=_sds((M, N), out_dtype),
        scratch_shapes=[pltpu.VMEM((tm, tn), F32)] if nk > 1 else [],
        compiler_params=_params(dimension_semantics=("parallel", "parallel", "arbitrary")),
        name=name)(*operands)


def _row_spec(tb, part):
    _, off, w = part
    return pl.BlockSpec((tb, w), lambda i, o=off // w: (i, o))


def _full_spec(arr):
    return pl.BlockSpec(arr.shape, lambda i: (0,) * arr.ndim)


def _rows(name, fn, rows, pars, outs, tb=256):
    n = rows[0][0].shape[0]
    tb = min(tb, n)
    nr, npar = len(rows), len(pars)

    def body(*refs):
        r = [ref[...].astype(F32) for ref in refs[:nr]]
        p = [ref[...].astype(F32) for ref in refs[nr:nr + npar]]
        res = fn(*r, *p)
        for o_ref, v in zip(refs[nr + npar:], res):
            o_ref[...] = v.astype(o_ref.dtype)

    out = pl.pallas_call(
        body, grid=(n // tb,),
        in_specs=[_row_spec(tb, p) for p in rows] + [_full_spec(p) for p in pars],
        out_specs=[pl.BlockSpec((tb, w), lambda i: (i, 0)) for w, _ in outs],
        out_shape=[_sds((n, w), dt) for w, dt in outs],
        compiler_params=_params(), name=name)(*[p[0] for p in rows], *pars)
    return out


def _rows_vjp(name, fn, rows, pars, cots, row_grads, par_grads, adds=None, tb=256):
    adds = adds or {}
    n = rows[0][0].shape[0]
    tb = min(tb, n)
    nr, npar, nc = len(rows), len(pars), len(cots)
    add_keys = sorted(adds)
    rg = [j for j, dt in enumerate(row_grads) if dt is not None]
    pg = [j for j, f in enumerate(par_grads) if f]

    def body(*refs):
        pos = 0
        r = [ref[...].astype(F32) for ref in refs[pos:pos + nr]]
        pos += nr
        p = [ref[...].astype(F32) for ref in refs[pos:pos + npar]]
        pos += npar
        ct = tuple(ref[...].astype(F32) for ref in refs[pos:pos + nc])
        pos += nc
        add_v = {k: ref[...].astype(F32) for k, ref in zip(add_keys, refs[pos:pos + len(add_keys)])}
        pos += len(add_keys)
        _, vjp = jax.vjp(lambda *args: tuple(fn(*args)), *r, *p)
        g = vjp(ct)
        for j in rg:
            v = g[j]
            if j in add_v:
                v = v + add_v[j]
            refs[pos][...] = v.astype(refs[pos].dtype)
            pos += 1
        first = pl.program_id(0) == 0
        for j in pg:
            ref = refs[pos]
            pos += 1

            @pl.when(first)
            def _(ref=ref):
                ref[...] = jnp.zeros_like(ref)

            ref[...] += g[nr + j]

    in_specs = ([_row_spec(tb, p) for p in rows] + [_full_spec(p) for p in pars] + [_row_spec(tb, p) for p in cots]
                + [_row_spec(tb, adds[k]) for k in add_keys])
    out_specs = ([pl.BlockSpec((tb, rows[j][2]), lambda i: (i, 0)) for j in rg] + [_full_spec(pars[j]) for j in pg])
    out_shape = ([_sds((n, rows[j][2]), row_grads[j]) for j in rg] + [_sds(pars[j].shape, F32) for j in pg])
    out = pl.pallas_call(
        body, grid=(n // tb,), in_specs=in_specs, out_specs=out_specs, out_shape=out_shape,
        compiler_params=_params(), name=name)(
            *[p[0] for p in rows], *pars, *[p[0] for p in cots], *[adds[k][0] for k in add_keys])
    return out[:len(rg)], out[len(rg):]


def _rms_fn(x, g):
    return (x * lax.rsqrt(jnp.mean(x * x, axis=-1, keepdims=True) + EPS) * g,)


def _lnsilu_fn(u, g, b):
    mu = jnp.mean(u, axis=-1, keepdims=True)
    xc = u - mu
    y = xc * lax.rsqrt(jnp.mean(xc * xc, axis=-1, keepdims=True) + EPS) * g + b
    return (y * jax.nn.sigmoid(y),)


def _merge_fn(ga, gb, gc, ya, yb, yc, ba, bb, bc):
    return (jax.nn.sigmoid(ga + ba) * ya + jax.nn.sigmoid(gb + bb) * yb + jax.nn.sigmoid(gc + bc) * yc,)


def _swiglu_fn(g, u):
    return (g * jax.nn.sigmoid(g) * u,)


def _loss_call(x, tgt, gain):
    n, d = x.shape
    tb = min(256, n)

    def body(x_ref, t_ref, g_ref, l_ref, dx_ref, dg_ref):
        (y,), vjp = jax.vjp(lambda a, b: tuple(_rms_fn(a, b)), x_ref[...], g_ref[...])
        e = y - t_ref[...]
        part = 0.5 * jnp.sum(jnp.mean(e * e, axis=-1, keepdims=True))
        dx, dg = vjp((e * (1.0 / d),))
        dx_ref[...] = dx

        @pl.when(pl.program_id(0) == 0)
        def _():
            l_ref[...] = jnp.zeros_like(l_ref)
            dg_ref[...] = jnp.zeros_like(dg_ref)

        l_ref[...] += jnp.full(l_ref.shape, part, F32)
        dg_ref[...] += dg

    return pl.pallas_call(
        body, grid=(n // tb,),
        in_specs=[pl.BlockSpec((tb, d), lambda i: (i, 0)), pl.BlockSpec((tb, d), lambda i: (i, 0)), _full_spec(gain)],
        out_specs=[pl.BlockSpec((1, LANES), lambda i: (0, 0)), pl.BlockSpec((tb, d), lambda i: (i, 0)), _full_spec(gain)],
        out_shape=[_sds((1, LANES), F32), _sds((n, d), F32), _sds(gain.shape, F32)],
        compiler_params=_params(), name="loss_head")(x, tgt, gain)


FOX_BLOCK = 256
CONV_ROWS = 64
CONV_HALO = 32


def _chunk(i):
    return pl.ds(pl.multiple_of(i * CONV_ROWS, CONV_ROWS), CONV_ROWS)


def _delayed(ext, s):
    return (ext if s == 0 else pltpu.roll(ext, s, 0))[CONV_HALO:]


def _advanced(ext, s):
    return (ext if s == 0 else pltpu.roll(ext, ext.shape[0] - s, 0))[:CONV_ROWS]


def _conv_pass(t, taps, w_ref, lanes, get_u, emit, get_dy=None, dw_acc=None):
    def body(i, carry):
        rows = _chunk(i)
        u = get_u(rows)
        ext = jnp.concatenate([carry, u], axis=0)
        dy = None if get_dy is None else get_dy(rows)
        acc = None
        for k in range(taps):
            sh = _delayed(ext, taps - 1 - k)
            term = w_ref[k:k + 1, lanes] * sh
            acc = term if acc is None else acc + term
            if dy is not None:
                dw_acc[k] += jnp.sum((dy * sh).reshape(CONV_ROWS // SUBLANES, SUBLANES, LANES), axis=0)
        emit(rows, u, acc)
        return u[CONV_ROWS - CONV_HALO:]

    lax.fori_loop(0, t // CONV_ROWS, body, jnp.zeros((CONV_HALO, LANES), F32))


def _conv_pass_t(t, taps, w_ref, lanes, get_dy, emit):
    n = t // CONV_ROWS

    def body(j, carry):
        rows = _chunk(n - 1 - j)
        dy = get_dy(rows)
        ext = jnp.concatenate([dy, carry], axis=0)
        acc = None
        for k in range(taps):
            term = w_ref[k:k + 1, lanes] * _advanced(ext, taps - 1 - k)
            acc = term if acc is None else acc + term
        emit(rows, acc)
        return dy[:CONV_HALO]

    lax.fori_loop(0, n, body, jnp.zeros((CONV_HALO, LANES), F32))


def _chan_block(d):
    return min(256, d)


def _lane_groups(cb):
    return [slice(g * LANES, (g + 1) * LANES) for g in range(cb // LANES)]


def _conv_a_fwd(za, w):
    t, d = za.shape[0], za.shape[1] // 3
    taps, cb = w.shape[0], _chan_block(d)
    nb = d // cb
    assert taps - 1 <= CONV_HALO and t % CONV_ROWS == 0

    def body(ab_ref, ac_ref, au_ref, w_ref, o_ref):
        for lanes in _lane_groups(cb):
            def emit(rows, u, s, lanes=lanes):
                o_ref[rows, lanes] = (ab_ref[rows, lanes] * s).astype(o_ref.dtype)

            _conv_pass(t, taps, w_ref, lanes, lambda rows, lanes=lanes: ac_ref[rows, lanes] * au_ref[rows, lanes], emit)

    return pl.pallas_call(
        body, grid=(nb,),
        in_specs=[pl.BlockSpec((t, cb), lambda j, o=o: (0, o * nb + j)) for o in range(3)]
        + [pl.BlockSpec((taps, cb), lambda j: (0, j))],
        out_specs=pl.BlockSpec((t, cb), lambda j: (0, j)),
        out_shape=_sds((t, d), BF16), compiler_params=_params(), name="conv_a_fwd")(za, za, za, w)


def _finish_taps(dw_ref, dw_acc, lanes, taps):
    for k in range(taps):
        dw_ref[k:k + 1, lanes] = jnp.sum(dw_acc[k], axis=0, keepdims=True)


def _conv_a_bwd(za, w, dpa):
    t, d = za.shape[0], za.shape[1] // 3
    taps, cb = w.shape[0], _chan_block(d)
    nb = d // cb

    def body(ab_ref, ac_ref, au_ref, w_ref, dy_ref, dab_ref, dac_ref, dau_ref, dw_ref, dw_acc):
        for lanes in _lane_groups(cb):
            dw_acc[...] = jnp.zeros_like(dw_acc)

            def ds(rows, lanes=lanes):
                return dy_ref[rows, lanes] * ab_ref[rows, lanes]

            def emit_s(rows, u, s, lanes=lanes):
                dab_ref[rows, lanes] = (dy_ref[rows, lanes] * s).astype(dab_ref.dtype)

            def emit_du(rows, du, lanes=lanes):
                dac_ref[rows, lanes] = (du * au_ref[rows, lanes]).astype(dac_ref.dtype)
                dau_ref[rows, lanes] = (du * ac_ref[rows, lanes]).astype(dau_ref.dtype)

            _conv_pass(t, taps, w_ref, lanes, lambda rows, lanes=lanes: ac_ref[rows, lanes] * au_ref[rows, lanes], emit_s,
                       get_dy=ds, dw_acc=dw_acc)
            _conv_pass_t(t, taps, w_ref, lanes, ds, emit_du)
            _finish_taps(dw_ref, dw_acc, lanes, taps)

    blk = pl.BlockSpec((t, cb), lambda j: (0, j))
    return pl.pallas_call(
        body, grid=(nb,),
        in_specs=[pl.BlockSpec((t, cb), lambda j, o=o: (0, o * nb + j)) for o in range(3)]
        + [pl.BlockSpec((taps, cb), lambda j: (0, j)), blk],
        out_specs=[blk, blk, blk, pl.BlockSpec((taps, cb), lambda j: (0, j))],
        out_shape=[_sds((t, d), BF16)] * 3 + [_sds((taps, d), F32)],
        scratch_shapes=[pltpu.VMEM((taps, SUBLANES, LANES), F32)],
        compiler_params=_params(), name="conv_a_bwd")(za, za, za, w, dpa)


def _conv_c_fwd(zcg, w, bias):
    t, d = zcg.shape[0], w.shape[1]
    taps, cb = w.shape[0], _chan_block(d)
    nb = d // cb
    assert taps - 1 <= CONV_HALO and t % CONV_ROWS == 0

    def body(cv_ref, cg_ref, w_ref, b_ref, o_ref):
        for lanes in _lane_groups(cb):
            def emit(rows, u, s, lanes=lanes):
                o_ref[rows, lanes] = s + b_ref[:, lanes]

            _conv_pass(t, taps, w_ref, lanes,
                       lambda rows, lanes=lanes: cv_ref[rows, lanes] * jax.nn.sigmoid(cg_ref[rows, lanes]), emit)

    return pl.pallas_call(
        body, grid=(nb,),
        in_specs=[pl.BlockSpec((t, cb), lambda j, o=o: (0, o * nb + j)) for o in range(2)]
        + [pl.BlockSpec((taps, cb), lambda j: (0, j)), pl.BlockSpec((1, cb), lambda j: (0, j))],
        out_specs=pl.BlockSpec((t, cb), lambda j: (0, j)),
        out_shape=_sds((t, d), F32), compiler_params=_params(), name="conv_c_fwd")(zcg, zcg, w, bias)


def _conv_c_bwd(zcg, w, du2):
    t, d = zcg.shape[0], w.shape[1]
    taps, cb = w.shape[0], _chan_block(d)
    nb = d // cb

    def body(cv_ref, cg_ref, w_ref, dy_ref, dcv_ref, dcg_ref, dw_ref, db_ref, dw_acc, db_acc):
        for lanes in _lane_groups(cb):
            dw_acc[...] = jnp.zeros_like(dw_acc)
            db_acc[...] = jnp.zeros_like(db_acc)

            def dy(rows, lanes=lanes):
                return dy_ref[rows, lanes]

            def emit_s(rows, u, s, lanes=lanes):
                db_acc[...] += jnp.sum(dy_ref[rows, lanes].reshape(CONV_ROWS // SUBLANES, SUBLANES, LANES), axis=0)

            def emit_du(rows, du, lanes=lanes):
                cv, sg = cv_ref[rows, lanes], jax.nn.sigmoid(cg_ref[rows, lanes])
                dcv_ref[rows, lanes] = (du * sg).astype(dcv_ref.dtype)
                dcg_ref[rows, lanes] = (du * cv * sg * (1.0 - sg)).astype(dcg_ref.dtype)

            _conv_pass(t, taps, w_ref, lanes,
                       lambda rows, lanes=lanes: cv_ref[rows, lanes] * jax.nn.sigmoid(cg_ref[rows, lanes]), emit_s,
                       get_dy=dy, dw_acc=dw_acc)
            _conv_pass_t(t, taps, w_ref, lanes, dy, emit_du)
            _finish_taps(dw_ref, dw_acc, lanes, taps)
            db_ref[:, lanes] = jnp.sum(db_acc[...], axis=0, keepdims=True)

    blk = pl.BlockSpec((t, cb), lambda j: (0, j))
    return pl.pallas_call(
        body, grid=(nb,),
        in_specs=[pl.BlockSpec((t, cb), lambda j, o=o: (0, o * nb + j)) for o in range(2)]
        + [pl.BlockSpec((taps, cb), lambda j: (0, j)), blk],
        out_specs=[blk, blk, pl.BlockSpec((taps, cb), lambda j: (0, j)), pl.BlockSpec((1, cb), lambda j: (0, j))],
        out_shape=[_sds((t, d), BF16)] * 2 + [_sds((taps, d), F32), _sds((1, d), F32)],
        scratch_shapes=[pltpu.VMEM((taps, SUBLANES, LANES), F32), pltpu.VMEM((SUBLANES, LANES), F32)],
        compiler_params=_params(), name="conv_c_bwd")(zcg, zcg, w, du2)


def _tri(tb, t, i, lower):
    r = lax.broadcasted_iota(jnp.int32, (tb, t), 0) + i * tb
    c = lax.broadcasted_iota(jnp.int32, (tb, t), 1)
    return ((r >= c) if lower else (c >= r)).astype(F32)


def _fox_cumsum(zf, bf):
    t = zf.shape[0]
    tb = min(256, t)

    def body(z_ref, b_ref, c_ref):
        logf = jax.nn.log_sigmoid(z_ref[...] + b_ref[...])
        c_ref[...] = jnp.dot(_tri(tb, t, pl.program_id(0), True), logf, precision=lax.Precision.HIGHEST,
                             preferred_element_type=F32)

    return pl.pallas_call(
        body, grid=(t // tb,), in_specs=[_full_spec(zf), _full_spec(bf)],
        out_specs=pl.BlockSpec((tb, LANES), lambda i: (i, 0)), out_shape=_sds((t, LANES), F32),
        compiler_params=_params(), name="fox_cumsum")(zf, bf)


def _fox_cumsum_bwd(zf, bf, dcc, dcr_t):
    t = zf.shape[0]
    tb = min(256, t)

    def body(z_ref, b_ref, dcc_ref, dcr_ref, dz_ref, db_ref, dc_ref):
        i = pl.program_id(0)

        @pl.when(i == 0)
        def _():
            dc_ref[...] = dcr_ref[...] + jnp.sum(dcc_ref[...], axis=0)
            db_ref[...] = jnp.zeros_like(db_ref)

        dlogf = jnp.dot(_tri(tb, t, i, False), dc_ref[...], precision=lax.Precision.HIGHEST, preferred_element_type=F32)
        dz = dlogf * jax.nn.sigmoid(-(z_ref[...] + b_ref[...]))
        dz_ref[...] = dz.astype(dz_ref.dtype)
        db_ref[...] += jnp.sum(dz, axis=0, keepdims=True)

    return pl.pallas_call(
        body, grid=(t // tb,),
        in_specs=[pl.BlockSpec((tb, LANES), lambda i: (i, 0)), _full_spec(bf), _full_spec(dcc), _full_spec(dcr_t)],
        out_specs=[pl.BlockSpec((tb, LANES), lambda i: (i, 0)), pl.BlockSpec((1, LANES), lambda i: (0, 0))],
        out_shape=[_sds((t, LANES), BF16), _sds((1, LANES), F32)],
        scratch_shapes=[pltpu.VMEM((t, LANES), F32)],
        compiler_params=_params(), name="fox_cumsum_bwd")(zf, bf, dcc, dcr_t)


def _fox_dims(zqkv, crow):
    t, d = zqkv.shape[0], zqkv.shape[1] // 3
    heads = crow.shape[0]
    hd = d // heads
    return t, d, hd, d // LANES, LANES // hd, crow.shape[3]


def _nt(a, b):
    return lax.dot_general(a, b, (((1,), (1,)), ((), ())), preferred_element_type=F32)


def _tn(a, b):
    return lax.dot_general(a, b, (((0,), (0,)), ((), ())), preferred_element_type=F32)


def _causal(tb):
    return lax.broadcasted_iota(jnp.int32, (tb, tb), 0) >= lax.broadcasted_iota(jnp.int32, (tb, tb), 1)


def _fox_fwd(zqkv, c, crow):
    t, d, hd, nblk, nh, tb = _fox_dims(zqkv, crow)
    scale = hd ** -0.5

    def body(q_ref, k_ref, v_ref, cc_ref, cr_ref, o_ref, cl_ref):
        p, qi = pl.program_id(0), pl.program_id(1)
        lane = lax.broadcasted_iota(jnp.int32, (1, LANES), 1)
        q, cc = q_ref[...], cc_ref[...]
        out = jnp.zeros((tb, LANES), F32)
        cl = jnp.zeros((tb, LANES), F32)
        for h in range(nh):
            in_head = (lane // hd) == h
            head_lane = lane == p * nh + h
            qh = jnp.where(in_head, q, jnp.zeros_like(q))
            c_q = jnp.sum(jnp.where(head_lane, cc, 0.0), axis=1, keepdims=True)

            def step(j, carry, diagonal, qh=qh, c_q=c_q):
                m, l, acc = carry
                rows = pl.ds(pl.multiple_of(j * tb, tb), tb)
                s = _nt(qh, k_ref[rows, :]) * scale + (c_q - cr_ref[h, j])
                if diagonal:
                    s = jnp.where(_causal(tb), s, -1e30)
                m_new = jnp.maximum(m, jnp.max(s, axis=1, keepdims=True))
                alpha = jnp.exp(m - m_new)
                e = jnp.exp(s - m_new)
                l = alpha * l + jnp.sum(e, axis=1, keepdims=True)
                acc = alpha * acc + jnp.dot(e.astype(BF16), v_ref[rows, :], preferred_element_type=F32)
                return m_new, l, acc

            init = (jnp.full((tb, 1), -1e30, F32), jnp.zeros((tb, 1), F32), jnp.zeros((tb, LANES), F32))
            carry = lax.fori_loop(0, qi, functools.partial(step, diagonal=False), init)
            m, l, acc = step(qi, carry, True)
            out = out + jnp.where(in_head, acc / l, 0.0)
            cl = cl + jnp.where(head_lane, c_q - (m + jnp.log(l)), 0.0)
        o_ref[...] = out.astype(o_ref.dtype)
        cl_ref[...] = cl

    return pl.pallas_call(
        body, grid=(nblk, t // tb),
        in_specs=[pl.BlockSpec((tb, LANES), lambda p, i: (i, p)), pl.BlockSpec((t, LANES), lambda p, i: (0, nblk + p)),
                  pl.BlockSpec((t, LANES), lambda p, i: (0, 2 * nblk + p)), pl.BlockSpec((tb, LANES), lambda p, i: (i, 0)),
                  pl.BlockSpec((nh, t // tb, 1, tb), lambda p, i: (p, 0, 0, 0))],
        out_specs=[pl.BlockSpec((tb, LANES), lambda p, i: (i, p)), pl.BlockSpec((None, tb, LANES), lambda p, i: (p, i, 0))],
        out_shape=[_sds((t, d), BF16), _sds((nblk, t, LANES), F32)],
        compiler_params=_params(), name="fox_fwd")(zqkv, zqkv, zqkv, c, crow)


def _fox_bwd(zqkv, cl, crow, o, do):
    t, d, hd, nblk, nh, tb = _fox_dims(zqkv, crow)
    nq = t // tb
    scale = hd ** -0.5

    def body(q_ref, k_ref, v_ref, cl_ref, cr_ref, o_ref, do_ref, dq_ref, dk_ref, dv_ref, dcl_ref, dcr_ref, dq_acc, dcl_acc):
        p, j = pl.program_id(0), pl.program_id(1)

        @pl.when(j == 0)
        def _():
            dq_acc[...] = jnp.zeros_like(dq_acc)
            dcl_acc[...] = jnp.zeros_like(dcl_acc)

        lane = lax.broadcasted_iota(jnp.int32, (1, LANES), 1)
        kj, vj = k_ref[...], v_ref[...]
        dk_out = jnp.zeros((tb, LANES), F32)
        dv_out = jnp.zeros((tb, LANES), F32)
        for h in range(nh):
            in_head = (lane // hd) == h
            head_lane = lane == p * nh + h
            kh = jnp.where(in_head, kj, jnp.zeros_like(kj))
            vh = jnp.where(in_head, vj, jnp.zeros_like(vj))
            c_k = cr_ref[h]

            def step(i, carry, diagonal, kh=kh, vh=vh, c_k=c_k, in_head=in_head, head_lane=head_lane):
                dk_h, dv_h, dcr_h = carry
                rows = pl.ds(pl.multiple_of(i * tb, tb), tb)
                qi, doi = q_ref[rows, :], do_ref[rows, :]
                cl_i = jnp.sum(jnp.where(head_lane, cl_ref[rows, :], 0.0), axis=1, keepdims=True)
                delta = jnp.sum(jnp.where(in_head, doi * o_ref[rows, :].astype(F32), 0.0), axis=1, keepdims=True)
                s = _nt(qi, kh) * scale + (cl_i - c_k)
                if diagonal:
                    s = jnp.where(_causal(tb), s, -1e30)
                prob = jnp.exp(s)
                dob = doi.astype(BF16)
                dv_h = dv_h + _tn(prob.astype(BF16), dob)
                ds = prob * (_nt(dob, vh) - delta)
                dsb = ds.astype(BF16)
                dk_h = dk_h + _tn(dsb, qi) * scale
                dq_acc[rows, :] += jnp.dot(dsb, kh, preferred_element_type=F32) * scale
                dcl_acc[rows, :] += jnp.where(head_lane, jnp.sum(ds, axis=1, keepdims=True), 0.0)
                return dk_h, dv_h, dcr_h - jnp.sum(ds, axis=0, keepdims=True)

            init = (jnp.zeros((tb, LANES), F32), jnp.zeros((tb, LANES), F32), jnp.zeros((1, tb), F32))
            carry = step(j, init, True)
            dk_h, dv_h, dcr_h = lax.fori_loop(j + 1, nq, functools.partial(step, diagonal=False), carry)
            dk_out = dk_out + jnp.where(in_head, dk_h, 0.0)
            dv_out = dv_out + jnp.where(in_head, dv_h, 0.0)
            dcr_ref[h] = dcr_h
        dk_ref[...] = dk_out.astype(dk_ref.dtype)
        dv_ref[...] = dv_out.astype(dv_ref.dtype)

        @pl.when(j == nq - 1)
        def _():
            dq_ref[...] = dq_acc[...].astype(dq_ref.dtype)
            dcl_ref[...] = dcl_acc[...]

    full = pl.BlockSpec((t, LANES), lambda p, j: (0, p))
    return pl.pallas_call(
        body, grid=(nblk, nq),
        in_specs=[full, pl.BlockSpec((tb, LANES), lambda p, j: (j, nblk + p)),
                  pl.BlockSpec((tb, LANES), lambda p, j: (j, 2 * nblk + p)), pl.BlockSpec((None, t, LANES), lambda p, j: (p, 0, 0)),
                  pl.BlockSpec((nh, None, 1, tb), lambda p, j: (p, j, 0, 0)), full, full],
        out_specs=[full, pl.BlockSpec((tb, LANES), lambda p, j: (j, p)), pl.BlockSpec((tb, LANES), lambda p, j: (j, p)),
                   pl.BlockSpec((None, t, LANES), lambda p, j: (p, 0, 0)),
                   pl.BlockSpec((nh, None, 1, tb), lambda p, j: (p, j, 0, 0))],
        out_shape=[_sds((t, d), BF16), _sds((t, d), BF16), _sds((t, d), BF16), _sds((nblk, t, LANES), F32),
                   _sds(crow.shape, F32)],
        scratch_shapes=[pltpu.VMEM((t, LANES), F32), pltpu.VMEM((t, LANES), F32)],
        compiler_params=_params(), name="fox_bwd")(zqkv, zqkv, zqkv, cl, crow, o, do)


def _xattn_block(q, k, v):
    s = lax.dot_general(q.astype(BF16), k.astype(BF16), (((1,), (1,)), ((), ())), preferred_element_type=F32)
    s = s * (q.shape[1] ** -0.5)
    e = jnp.exp(s - lax.stop_gradient(jnp.max(s, axis=1, keepdims=True)))
    prob = e / jnp.sum(e, axis=1, keepdims=True)
    return jnp.dot(prob.astype(BF16), v.astype(BF16), preferred_element_type=F32)


def _xattn_fwd(qx, kv):
    t, dx = qx.shape
    nm, nh, tq = kv.shape[0], qx.shape[1] // LANES, min(512, qx.shape[0])

    def body(q_ref, k_ref, v_ref, o_ref):
        o_ref[...] = _xattn_block(q_ref[...], k_ref[...], v_ref[...]).astype(o_ref.dtype)

    return pl.pallas_call(
        body, grid=(nh, t // tq),
        in_specs=[pl.BlockSpec((tq, LANES), lambda h, i: (i, h)), pl.BlockSpec((nm, LANES), lambda h, i: (0, h)),
                  pl.BlockSpec((nm, LANES), lambda h, i: (0, nh + h))],
        out_specs=pl.BlockSpec((tq, LANES), lambda h, i: (i, h)), out_shape=_sds((t, dx), BF16),
        compiler_params=_params(), name="xattn_fwd")(qx, kv, kv)


def _xattn_bwd(qx, kv, do):
    t, dx = qx.shape
    nm, nh, tq = kv.shape[0], qx.shape[1] // LANES, min(512, qx.shape[0])

    def body(q_ref, k_ref, v_ref, do_ref, dq_ref, dk_ref, dv_ref):
        _, vjp = jax.vjp(_xattn_block, q_ref[...].astype(F32), k_ref[...].astype(F32), v_ref[...].astype(F32))
        dq, dk, dv = vjp(do_ref[...].astype(F32))
        dq_ref[...] = dq.astype(dq_ref.dtype)

        @pl.when(pl.program_id(1) == 0)
        def _():
            dk_ref[...] = jnp.zeros_like(dk_ref)
            dv_ref[...] = jnp.zeros_like(dv_ref)

        dk_ref[...] += dk
        dv_ref[...] += dv

    qblk = pl.BlockSpec((tq, LANES), lambda h, i: (i, h))
    kblk = pl.BlockSpec((nm, LANES), lambda h, i: (0, h))
    return pl.pallas_call(
        body, grid=(nh, t // tq),
        in_specs=[qblk, kblk, pl.BlockSpec((nm, LANES), lambda h, i: (0, nh + h)), qblk],
        out_specs=[qblk, kblk, kblk],
        out_shape=[_sds((t, dx), BF16), _sds((nm, dx), F32), _sds((nm, dx), F32)],
        compiler_params=_params(), name="xattn_bwd")(qx, kv, kv, do)


def _row_tile(rows, cols, budget=1 << 20):
    tb = rows
    while tb % (4 * SUBLANES) == 0 and tb * cols * 4 > budget:
        tb //= 2
    return tb


def _adamw(w, g, m, v):
    shape = w.shape
    cols = shape[-1]
    w2, g2, m2, v2 = (a.reshape(-1, cols) for a in (w, g, m, v))
    rows = w2.shape[0]
    tb = _row_tile(rows, cols)

    def body(w_ref, g_ref, m_ref, v_ref, d_ref, nm_ref, nv_ref):
        gg = g_ref[...]
        nm = ADAM_B1 * m_ref[...] + (1.0 - ADAM_B1) * gg
        nv = ADAM_B2 * v_ref[...] + (1.0 - ADAM_B2) * (gg * gg)
        m_hat = nm / (1.0 - ADAM_B1 ** ADAM_STEP)
        v_hat = nv / (1.0 - ADAM_B2 ** ADAM_STEP)
        d_ref[...] = -ADAM_LR * (m_hat / (jnp.sqrt(v_hat) + ADAM_EPS) + ADAM_WD * w_ref[...])
        nm_ref[...] = nm
        nv_ref[...] = nv

    blk = pl.BlockSpec((tb, cols), lambda i: (i, 0))
    out = pl.pallas_call(
        body, grid=(rows // tb,), in_specs=[blk] * 4, out_specs=[blk] * 3, out_shape=[_sds((rows, cols), F32)] * 3,
        compiler_params=_params(), name="adamw")(w2, g2, m2, v2)
    return tuple(o.reshape(shape) for o in out)


def _sum_devices(gathered, rows):
    cols = gathered.shape[1]

    def body(g_ref, o_ref):
        acc = g_ref[0:rows, :]
        for dev in range(1, N_DEV):
            acc = acc + g_ref[dev * rows:(dev + 1) * rows, :]
        o_ref[...] = acc

    return pl.pallas_call(body, out_shape=_sds((rows, cols), F32), compiler_params=_params(), name="sum_devices")(gathered)


def _place():
    x, y, c = lax.axis_index("x"), lax.axis_index("y"), lax.axis_index("c")
    chips = [(1 - x, y), (x, 1 - y), (1 - x, 1 - y)]
    return x, y, c, chips


def _chip_id(chip):
    return 2 * chip[0] + chip[1]


def _half(c, rows):
    rh = rows // 2
    return pl.ds(pl.multiple_of(c * rh, 16), rh)


_HBM = pl.BlockSpec(memory_space=pltpu.HBM)


def _allgather8(x_shard):
    m_per, n = x_shard.shape

    def body(x_ref, out_ref, send_sems, recv_sems, local_sem):
        x, y, c, chips = _place()
        me, sibling = (x, y, c), (x, y, 1 - c)

        def rows(px, py, pc):
            return out_ref.at[pl.ds((4 * px + 2 * py + pc) * m_per, m_per), :]

        def copy(k, block, to, src=None):
            return pltpu.make_async_remote_copy(
                src_ref=rows(*block) if src is None else src, dst_ref=rows(*block),
                send_sem=send_sems.at[k], recv_sem=recv_sems.at[k], device_id=to, device_id_type=MESH)

        mine = pltpu.make_async_copy(x_ref, rows(*me), local_sem)
        mine.start()
        first = [copy(0, me, sibling, src=x_ref)]
        first += [copy(1 + j, me, (*chip, c), src=x_ref) for j, chip in enumerate(chips)]
        for cp in first:
            cp.start()
        passed = [copy(4 + j, (*chip, c), sibling) for j, chip in enumerate(chips)]
        for j, chip in enumerate(chips):
            copy(1 + j, (*chip, c), me).wait_recv()
            passed[j].start()
        copy(0, sibling, me).wait_recv()
        for j, chip in enumerate(chips):
            copy(4 + j, (*chip, 1 - c), me).wait_recv()
        for cp in first + passed:
            cp.wait_send()
        mine.wait()

    return pl.pallas_call(
        body, out_shape=_sds((N_DEV * m_per, n), x_shard.dtype),
        in_specs=[pl.BlockSpec(memory_space=pltpu.VMEM)], out_specs=pl.BlockSpec(memory_space=pltpu.VMEM),
        scratch_shapes=[pltpu.SemaphoreType.DMA((7,)), pltpu.SemaphoreType.DMA((7,)), pltpu.SemaphoreType.DMA],
        compiler_params=_params(), name="allgather8")(x_shard)


def _gather_weights(ws):
    n = len(ws)

    def body(*refs):
        w, out = refs[:n], refs[n:2 * n]
        send_i, recv_i, send_d, recv_d = refs[2 * n:]
        x, y, c, chips = _place()
        sibling = (x, y, 1 - c)
        me_chip = 2 * x + y

        def slab(i, chip, half):
            return out[i].at[:, chip, _half(half, w[i].shape[1]), :]

        def over_ici(i, j, src_chip, to, src=None):
            return pltpu.make_async_remote_copy(
                src_ref=slab(i, src_chip, c) if src is None else src, dst_ref=slab(i, src_chip, c),
                send_sem=send_i.at[3 * i + j], recv_sem=recv_i.at[3 * i + j], device_id=to, device_id_type=MESH)

        def over_d2d(i, j, half):
            return pltpu.make_async_remote_copy(
                src_ref=slab(i, _chip_id(chips[j]), half), dst_ref=slab(i, _chip_id(chips[j]), half),
                send_sem=send_d.at[3 * i + j], recv_sem=recv_d.at[3 * i + j], device_id=sibling, device_id_type=MESH)

        sent = []
        for i in range(n):
            for j, chip in enumerate(chips):
                cp = over_ici(i, j, me_chip, (*chip, c), src=w[i].at[:, _half(c, w[i].shape[1]), :])
                cp.start()
                sent.append(cp)
        for i in range(n):
            for j, chip in enumerate(chips):
                over_ici(i, j, _chip_id(chip), (x, y, c)).wait_recv()
                cp = over_d2d(i, j, c)
                cp.start()
                sent.append(cp)
        for i in range(n):
            for j in range(3):
                over_d2d(i, j, 1 - c).wait_recv()
        for cp in sent:
            cp.wait_send()

    return pl.pallas_call(
        body, in_specs=[_HBM] * n, out_specs=[_HBM] * n,
        out_shape=[_sds((a.shape[0], N_CHIPS) + a.shape[1:], a.dtype) for a in ws],
        scratch_shapes=[pltpu.SemaphoreType.DMA((3 * n,)) for _ in range(4)],
        compiler_params=_params(), name="gather_weights")(*ws)


def _swap_halves(gs):
    n = len(gs)

    def body(*refs):
        g, out = refs[:n], refs[n:2 * n]
        send, recv = refs[2 * n:]
        x, y, c, _ = _place()
        cps = [pltpu.make_async_remote_copy(
            src_ref=g[i].at[:, :, _half(1 - c, g[i].shape[2]), :], dst_ref=out[i],
            send_sem=send.at[i], recv_sem=recv.at[i], device_id=(x, y, 1 - c), device_id_type=MESH) for i in range(n)]
        for cp in cps:
            cp.start()
        for cp in cps:
            cp.wait()

    return pl.pallas_call(
        body, in_specs=[_HBM] * n, out_specs=[_HBM] * n,
        out_shape=[_sds(a.shape[:2] + (a.shape[2] // 2, a.shape[3]), a.dtype) for a in gs],
        scratch_shapes=[pltpu.SemaphoreType.DMA((n,)), pltpu.SemaphoreType.DMA((n,))],
        compiler_params=_params(), name="swap_halves")(*gs)


def _exchange_chips(ss):
    n = len(ss)

    def body(*refs):
        s, out = refs[:n], refs[n:2 * n]
        send, recv = refs[2 * n:]
        x, y, c, chips = _place()
        cps = []
        for i in range(n):
            for j, chip in enumerate(chips):
                cps.append(pltpu.make_async_remote_copy(
                    src_ref=s[i].at[:, _chip_id(chip)], dst_ref=out[i].at[j],
                    send_sem=send.at[3 * i + j], recv_sem=recv.at[3 * i + j], device_id=(*chip, c), device_id_type=MESH))
        for cp in cps:
            cp.start()
        for cp in cps:
            cp.wait()

    return pl.pallas_call(
        body, in_specs=[_HBM] * n, out_specs=[_HBM] * n,
        out_shape=[_sds((3, a.shape[0]) + a.shape[2:], a.dtype) for a in ss],
        scratch_shapes=[pltpu.SemaphoreType.DMA((3 * n,)), pltpu.SemaphoreType.DMA((3 * n,))],
        compiler_params=_params(), name="exchange_chips")(*ss)


def _join_halves(fs):
    n = len(fs)

    def body(*refs):
        f, out = refs[:n], refs[n:2 * n]
        send, recv = refs[2 * n:]
        x, y, c, _ = _place()

        def half(ref, which):
            return ref.at[:, _half(which, ref.shape[1]), :]

        for i in range(n):
            pltpu.make_async_remote_copy(
                src_ref=half(f[i], c), dst_ref=half(out[i], c), send_sem=send.at[i], recv_sem=recv.at[i],
                device_id=(x, y, 1 - c), device_id_type=MESH).start()
        for i in range(n):
            arrive = pltpu.make_async_remote_copy(
                src_ref=half(f[i], c), dst_ref=half(out[i], 1 - c), send_sem=send.at[i], recv_sem=recv.at[i],
                device_id=(x, y, 1 - c), device_id_type=MESH)
            arrive.wait_recv()
            arrive.wait_send()

    return pl.pallas_call(
        body, in_specs=[_HBM] * n, out_specs=[_HBM] * n, out_shape=[_sds(a.shape, a.dtype) for a in fs],
        input_output_aliases={i: i for i in range(n)},
        scratch_shapes=[pltpu.SemaphoreType.DMA((n,)), pltpu.SemaphoreType.DMA((n,))],
        compiler_params=_params(), name="join_halves")(*fs)


def _add_sibling(g, got, c_idx):
    nl, ns, r, cols = g.shape
    rh = r // 2
    tb = _row_tile(rh, cols)
    nb = rh // tb
    g3, got3 = g.reshape(nl * ns, r, cols), got.reshape(nl * ns, rh, cols)

    def body(c_ref, a_ref, b_ref, o_ref):
        o_ref[...] = (a_ref[...].astype(F32) + b_ref[...].astype(F32)).astype(o_ref.dtype)

    out = pl.pallas_call(
        body,
        grid_spec=pltpu.PrefetchScalarGridSpec(
            num_scalar_prefetch=1, grid=(nl * ns, nb),
            in_specs=[pl.BlockSpec((None, tb, cols), lambda s, i, c_ref: (s, c_ref[0] * nb + i, 0)),
                      pl.BlockSpec((None, tb, cols), lambda s, i, c_ref: (s, i, 0))],
            out_specs=pl.BlockSpec((None, tb, cols), lambda s, i, c_ref: (s, i, 0))),
        out_shape=_sds((nl * ns, rh, cols), BF16), compiler_params=_params(), name="add_sibling")(c_idx, g3, got3)
    return out.reshape(nl, ns, rh, cols)


def _add_chips(s, got, chip_idx, c_idx):
    nl, ns, rh, cols = s.shape
    tb = _row_tile(rh, cols)
    nb = rh // tb

    def body(k_ref, c_ref, a_ref, b0_ref, b1_ref, b2_ref, o_ref):
        o_ref[...] = ((a_ref[...].astype(F32) + b0_ref[...].astype(F32)) + b1_ref[...].astype(F32)) + b2_ref[...].astype(F32)

    return pl.pallas_call(
        body,
        grid_spec=pltpu.PrefetchScalarGridSpec(
            num_scalar_prefetch=2, grid=(nl, nb),
            in_specs=[pl.BlockSpec((None, None, tb, cols), lambda l, i, k_ref, c_ref: (l, k_ref[0], i, 0))]
            + [pl.BlockSpec((None, None, tb, cols), lambda l, i, k_ref, c_ref, j=j: (j, l, i, 0)) for j in range(3)],
            out_specs=pl.BlockSpec((None, tb, cols), lambda l, i, k_ref, c_ref: (l, c_ref[0] * nb + i, 0))),
        out_shape=_sds((nl, 2 * rh, cols), F32), compiler_params=_params(), name="add_chips")(chip_idx, c_idx, s, got, got, got)


def _pad_lanes(a, width=LANES):
    return jnp.pad(a, ((0, 0), (0, width - a.shape[1])))


def _local_step(x, mem, tgt, sp, wt):
    t, d = x.shape
    depth = sp["mix_norm"].shape[0]
    heads = sp["b_forget"].shape[1]
    f = wt["down"][0].shape[0]
    row = lambda a, l: a[l:l + 1]
    fox_tb = min(FOX_BLOCK, t)

    memn, = _rows("rms_mem", _rms_fn, [_part(mem)], [sp["mem_norm"][None]], [(d, BF16)])
    saved = []
    for l in range(depth):
        sv = {"x0": x}
        h1, = _rows("rms_mix", _rms_fn, [_part(x)], [row(sp["mix_norm"], l)], [(d, BF16)])
        wm = wt["main"][l]
        za = _mm("mm_in_a", h1, [_part(wm, 0, 3 * d)])
        zqkv = _mm("mm_in_qkv", h1, [_part(wm, 3 * d, 3 * d)], out_dtype=BF16)
        zcg = _mm("mm_in_cg", h1, [_part(wm, 6 * d, 5 * d)])
        zf = _mm("mm_in_f", h1, wt["f"][l])
        bf = _pad_lanes(row(sp["b_forget"], l))
        c = _fox_cumsum(zf, bf)
        crow = c[:, :heads].T.reshape(heads, t // fox_tb, 1, fox_tb)
        o, cl = _fox_fwd(zqkv, c, crow)
        pa = _conv_a_fwd(za, sp["conv_a"][l])
        u2 = _conv_c_fwd(zcg, sp["conv_c"][l], row(sp["conv_c_bias"], l))
        u3, = _rows("ln_silu", _lnsilu_fn, [_part(u2)], [row(sp["ln_c_gain"], l), row(sp["ln_c_bias"], l)], [(d, BF16)])
        ya = _mm("mm_out_a", pa, wt["out_a"][l])
        yb = _mm("mm_out_b", o, wt["out_b"][l])
        yc = _mm("mm_out_c", u3, wt["out_c"][l])
        bg = [sp["b_gate"][l:l + 1, k * d:(k + 1) * d] for k in range(3)]
        gate_rows = [_part(zcg, (2 + k) * d, d) for k in range(3)] + [_part(ya), _part(yb), _part(yc)]
        mg, = _rows("merge", _merge_fn, gate_rows, bg, [(d, BF16)])
        x1 = _mm("mm_o", mg, wt["o"][l], add=x)
        h2, = _rows("rms_x", _rms_fn, [_part(x1)], [row(sp["xattn_norm"], l)], [(d, BF16)])
        qx = _mm("mm_xq", h2, wt["xq"][l], out_dtype=BF16)
        kv = _mm("mm_xkv", memn, wt["xkv"][l], out_dtype=BF16)
        ox = _xattn_fwd(qx, kv)
        x2 = _mm("mm_xo", ox, wt["xo"][l], add=x1)
        h3, = _rows("rms_ffn", _rms_fn, [_part(x2)], [row(sp["ffn_norm"], l)], [(d, BF16)])
        gu = _mm("mm_gu", h3, wt["gu"][l])
        act, = _rows("swiglu", _swiglu_fn, [_part(gu, 0, f), _part(gu, f, f)], [], [(f, BF16)])
        x3 = _mm("mm_down", act, wt["down"][l], add=x2)
        sv.update(h1=h1, za=za, zqkv=zqkv, zcg=zcg, zf=zf, bf=bf, cl=cl, crow=crow, o=o, pa=pa, u2=u2, u3=u3, ya=ya, yb=yb,
                  yc=yc, bg=bg, gate_rows=gate_rows, mg=mg, x1=x1, h2=h2, qx=qx, kv=kv, ox=ox, x2=x2, h3=h3, gu=gu, act=act)
        saved.append(sv)
        x = x3

    loss_row, dx, d_final = _loss_call(x, tgt, sp["final_norm"][None])

    gs = {k: [None] * depth for k in ("mix_norm", "b_gate", "b_forget", "conv_a", "conv_c", "conv_c_bias", "ln_c_gain",
                                      "ln_c_bias", "xattn_norm", "ffn_norm")}
    gw = {k: [None] * depth for k in ("main", "f", "out_a", "out_b", "out_c", "o", "xq", "xkv", "xo", "gu", "down")}
    dmemn = None
    for l in reversed(range(depth)):
        sv = saved[l]
        dact = _mm("mm_down_dx", dx, wt["down"][l], tb=True)
        gw["down"][l] = _mm("mm_down_dw", sv["act"], dx, ta=True, out_dtype=BF16)
        (dg, du), _ = _rows_vjp("swiglu_bwd", _swiglu_fn, [_part(sv["gu"], 0, f), _part(sv["gu"], f, f)], [], [_part(dact)],
                                [BF16, BF16], [])
        dgu = [_part(dg), _part(du)]
        dh3 = _mm("mm_gu_dx", dgu, wt["gu"][l], tb=True)
        gw["gu"][l] = _mm("mm_gu_dw", sv["h3"], dgu, ta=True, out_dtype=BF16)
        (dx2,), (gs["ffn_norm"][l],) = _rows_vjp("rms_ffn_bwd", _rms_fn, [_part(sv["x2"])], [row(sp["ffn_norm"], l)],
                                                 [_part(dh3)], [F32], [True], adds={0: _part(dx)})
        dox = _mm("mm_xo_dx", dx2, wt["xo"][l], tb=True)
        gw["xo"][l] = _mm("mm_xo_dw", sv["ox"], dx2, ta=True, out_dtype=BF16)
        dqx, dkx, dvx = _xattn_bwd(sv["qx"], sv["kv"], dox)
        dh2 = _mm("mm_xq_dx", dqx, wt["xq"][l], tb=True)
        gw["xq"][l] = _mm("mm_xq_dw", sv["h2"], dqx, ta=True, out_dtype=BF16)
        dkv = [_part(dkx), _part(dvx)]
        dmemn = _mm("mm_xkv_dx", dkv, wt["xkv"][l], tb=True, add=dmemn)
        gw["xkv"][l] = _mm("mm_xkv_dw", memn, dkv, ta=True, out_dtype=BF16)
        (dx1,), (gs["xattn_norm"][l],) = _rows_vjp("rms_x_bwd", _rms_fn, [_part(sv["x1"])], [row(sp["xattn_norm"], l)],
                                                   [_part(dh2)], [F32], [True], adds={0: _part(dx2)})
        dmg = _mm("mm_o_dx", dx1, wt["o"][l], tb=True)
        gw["o"][l] = _mm("mm_o_dw", sv["mg"], dx1, ta=True, out_dtype=BF16)
        (dga, dgb, dgc, dya, dyb, dyc), dbg = _rows_vjp("merge_bwd", _merge_fn, sv["gate_rows"], sv["bg"], [_part(dmg)],
                                                        [BF16] * 6, [True] * 3)
        gs["b_gate"][l] = jnp.concatenate(dbg, axis=1)
        dpa = _mm("mm_out_a_dx", dya, wt["out_a"][l], tb=True)
        gw["out_a"][l] = _mm("mm_out_a_dw", sv["pa"], dya, ta=True, out_dtype=BF16)
        do = _mm("mm_out_b_dx", dyb, wt["out_b"][l], tb=True)
        gw["out_b"][l] = _mm("mm_out_b_dw", sv["o"], dyb, ta=True, out_dtype=BF16)
        du3 = _mm("mm_out_c_dx", dyc, wt["out_c"][l], tb=True)
        gw["out_c"][l] = _mm("mm_out_c_dw", sv["u3"], dyc, ta=True, out_dtype=BF16)
        (du2,), (gs["ln_c_gain"][l], gs["ln_c_bias"][l]) = _rows_vjp(
            "ln_silu_bwd", _lnsilu_fn, [_part(sv["u2"])], [row(sp["ln_c_gain"], l), row(sp["ln_c_bias"], l)], [_part(du3)],
            [F32], [True, True])
        dcv, dcg, gs["conv_c"][l], gs["conv_c_bias"][l] = _conv_c_bwd(sv["zcg"], sp["conv_c"][l], du2)
        dab, dac, dau, gs["conv_a"][l] = _conv_a_bwd(sv["za"], sp["conv_a"][l], dpa)
        dq, dk, dv, dcc, dcr = _fox_bwd(sv["zqkv"], sv["cl"], sv["crow"], sv["o"], do)
        dcr_t = _pad_lanes(dcr.reshape(heads, t).T)
        dzf, dbf = _fox_cumsum_bwd(sv["zf"], sv["bf"], dcc, dcr_t)
        gs["b_forget"][l] = dbf[:, :heads]
        dz = [_part(a) for a in (dab, dac, dau, dq, dk, dv, dcv, dcg, dga, dgb, dgc)]
        dh1f = _mm("mm_in_f_dx", dzf, wt["f"][l], tb=True)
        dh1 = _mm("mm_in_dx", dz, wt["main"][l], tb=True, add=dh1f)
        gw["main"][l] = _mm("mm_in_dw", sv["h1"], dz, ta=True, out_dtype=BF16)
        gw["f"][l] = _mm("mm_in_f_dw", sv["h1"], dzf, ta=True, out_dtype=BF16)
        (dx,), (gs["mix_norm"][l],) = _rows_vjp("rms_mix_bwd", _rms_fn, [_part(sv["x0"])], [row(sp["mix_norm"], l)],
                                                [_part(dh1)], [F32], [True], adds={0: _part(dx1)})

    _, (d_mem_norm,) = _rows_vjp("rms_mem_bwd", _rms_fn, [_part(mem)], [sp["mem_norm"][None]], [_part(dmemn)], [None], [True])
    small = {k: jnp.stack([a.reshape(sp[k].shape[1:]) for a in v]) for k, v in gs.items()}
    small["mem_norm"] = d_mem_norm[0]
    small["final_norm"] = d_final[0]
    return loss_row, dx, small, gw


_BIG = ("w_in", "w_out_a", "w_out_b", "w_out_c", "w_o", "w_xq", "w_xkv", "w_xo", "w_gate_up", "w_down")
_COL_SHARDED = ("w_in", "w_xo", "w_gate_up")
_KEY = {"w_out_a": "out_a", "w_out_b": "out_b", "w_out_c": "out_c", "w_o": "o", "w_xq": "xq", "w_xkv": "xkv", "w_xo": "xo",
        "w_gate_up": "gu", "w_down": "down"}
_SMALL = ("mix_norm", "b_gate", "b_forget", "conv_c_bias", "ln_c_gain", "ln_c_bias", "xattn_norm", "ffn_norm", "mem_norm",
          "final_norm")
_SMALL_SHARDED = ("conv_a", "conv_c")


def _full_from_shards(name, g):
    if name in _COL_SHARDED:
        return g.transpose(1, 0, 2).reshape(g.shape[1], N_CHIPS * g.shape[2])
    return g.reshape(N_CHIPS * g.shape[1], g.shape[2])


def _shards_from_full(name, a):
    if name in _COL_SHARDED:
        return a.reshape(a.shape[0], N_CHIPS, a.shape[1] // N_CHIPS).transpose(1, 0, 2)
    return a.reshape(N_CHIPS, a.shape[0] // N_CHIPS, a.shape[1])


def _pack_rows(parts):
    padded = []
    for a in parts:
        pad = -a.shape[0] % SUBLANES
        padded.append(jnp.pad(a, ((0, pad), (0, 0))) if pad else a)
    return jnp.concatenate(padded, axis=0)


def _unpack_rows(packed, shapes):
    out, pos = [], 0
    for r in shapes:
        out.append(packed[pos:pos + r])
        pos += r + (-r % SUBLANES)
    return out


def kernel(x, mem, mix_norm, w_in, b_gate, b_forget, conv_a, w_out_a, w_out_b, conv_c, conv_c_bias, ln_c_gain, ln_c_bias, w_out_c, w_o, xattn_norm, mem_norm, w_xq, w_xkv, w_xo, ffn_norm, w_gate_up, w_down, final_norm, loss_target, m_mix_norm, m_w_in, m_b_gate, m_b_forget, m_conv_a, m_w_out_a, m_w_out_b, m_conv_c, m_conv_c_bias, m_ln_c_gain, m_ln_c_bias, m_w_out_c, m_w_o, m_xattn_norm, m_mem_norm, m_w_xq, m_w_xkv, m_w_xo, m_ffn_norm, m_w_gate_up, m_w_down, m_final_norm, v_mix_norm, v_w_in, v_b_gate, v_b_forget, v_conv_a, v_w_out_a, v_w_out_b, v_conv_c, v_conv_c_bias, v_ln_c_gain, v_ln_c_bias, v_w_out_c, v_w_o, v_xattn_norm, v_mem_norm, v_w_xq, v_w_xkv, v_w_xo, v_ffn_norm, v_w_gate_up, v_w_down, v_final_norm):
    args = dict(locals())
    names = ["mix_norm", "w_in", "b_gate", "b_forget", "conv_a", "w_out_a", "w_out_b", "conv_c", "conv_c_bias", "ln_c_gain",
             "ln_c_bias", "w_out_c", "w_o", "xattn_norm", "mem_norm", "w_xq", "w_xkv", "w_xo", "ffn_norm", "w_gate_up", "w_down",
             "final_norm"]
    depth, d = mix_norm.shape
    heads = b_forget.shape[1]
    cx, cy, cc = lax.axis_index("x"), lax.axis_index("y"), lax.axis_index("c")
    chip = 2 * cx + cy
    c_idx = jnp.reshape(cc, (1,)).astype(jnp.int32)
    chip_idx = jnp.reshape(chip, (1,)).astype(jnp.int32)

    own = [args[n].astype(BF16) for n in _BIG]
    gathered = [lax.dynamic_update_slice(g, w[:, None], (0, chip, 0, 0)) for g, w in zip(_gather_weights(own), own)]
    gathered = dict(zip(_BIG, gathered))
    wt = {k: [] for k in ("main", "f", "out_a", "out_b", "out_c", "o", "xq", "xkv", "xo", "gu", "down")}
    for l in range(depth):
        full_in = _full_from_shards("w_in", gathered["w_in"][l])
        wt["main"].append(jnp.concatenate([full_in[:, :6 * d], full_in[:, 6 * d + heads:]], axis=1))
        wt["f"].append(_pad_lanes(full_in[:, 6 * d:6 * d + heads]))
        for n in _BIG[1:]:
            wt[_KEY[n]].append(_full_from_shards(n, gathered[n][l]))
    conv_rows = [conv_a.shape[1] * depth, conv_c.shape[1] * depth]
    conv_pack = _pack_rows([conv_a.reshape(conv_rows[0], -1), conv_c.reshape(conv_rows[1], -1)])
    conv_all = _allgather8(conv_pack).reshape(N_CHIPS, 2, conv_pack.shape[0], conv_pack.shape[1])[:, 0]
    conv_all = conv_all.transpose(1, 0, 2).reshape(conv_pack.shape[0], d)
    conv_a_full, conv_c_full = _unpack_rows(conv_all, conv_rows)
    sp = {n: args[n] for n in _SMALL}
    sp["conv_a"] = conv_a_full.reshape(depth, conv_a.shape[1], d)
    sp["conv_c"] = conv_c_full.reshape(depth, conv_c.shape[1], d)

    loss_row, grad_x, gsmall, gw = _local_step(x[0], mem[0], loss_target[0], sp, wt)

    gfull = {}
    for n in _BIG[1:]:
        gfull[n] = jnp.stack([_shards_from_full(n, gw[_KEY[n]][l]) for l in range(depth)])
    gin = []
    for l in range(depth):
        gm, gf = gw["main"][l], gw["f"][l]
        gin.append(_shards_from_full("w_in", jnp.concatenate([gm[:, :6 * d], gf[:, :heads], gm[:, 6 * d:]], axis=1)))
    gfull["w_in"] = jnp.stack(gin)
    local = [gfull[n] for n in _BIG]
    from_sibling = _swap_halves(local)
    chip_sums = [_add_sibling(g, r, c_idx) for g, r in zip(local, from_sibling)]
    from_chips = _exchange_chips(chip_sums)
    halves = [_add_chips(s, r, chip_idx, c_idx) for s, r in zip(chip_sums, from_chips)]
    grads = dict(zip(_BIG, _join_halves(halves)))

    def as_rows(n, a):
        return _pad_lanes(a, d) if n == "b_forget" else a.reshape(-1, d)

    pieces = [as_rows(n, gsmall[n]) for n in _SMALL]
    for n in _SMALL_SHARDED:
        pieces.append(gsmall[n].reshape(-1, d))
    pieces.append(_pad_lanes(loss_row, d))
    pack = _pack_rows(pieces)
    total = _sum_devices(_allgather8(pack), pack.shape[0])
    shapes = [p.shape[0] for p in pieces]
    summed = _unpack_rows(total, shapes)
    for n, g in zip(_SMALL, summed):
        grads[n] = g[:, :heads] if n == "b_forget" else g.reshape(args[n].shape)
    for n, g in zip(_SMALL_SHARDED, summed[len(_SMALL):]):
        g = g.reshape(args[n].shape[:2] + (d,))
        grads[n] = lax.dynamic_slice_in_dim(g, chip * args[n].shape[2], args[n].shape[2], axis=2)
    loss = summed[-1][0, 0]

    delta, new_m, new_v = {}, {}, {}
    packed = [_pack_rows([as_rows(n, src[pre + n]) for n in _SMALL])
              for src, pre in ((args, ""), (grads, ""), (args, "m_"), (args, "v_"))]
    srows = [as_rows(n, args[n]).shape[0] for n in _SMALL]
    updated = [_unpack_rows(a, srows) for a in _adamw(*packed)]
    for k, n in enumerate(_SMALL):
        delta[n], new_m[n], new_v[n] = (
            (u[k][:, :heads] if n == "b_forget" else u[k]).reshape(args[n].shape) for u in updated)
    for n in _BIG + _SMALL_SHARDED:
        delta[n], new_m[n], new_v[n] = _adamw(args[n], grads[n], args["m_" + n], args["v_" + n])

    return (loss, grad_x[None], *[grads[n] for n in names], *[delta[n] for n in names], *[new_m[n] for n in names],
            *[new_v[n] for n in names])
```

```python
import functools
import math

import jax
import jax.numpy as jnp
from jax import lax
from jax.experimental import pallas as pl
from jax.experimental.pallas import tpu as pltpu

F32, BF16 = jnp.float32, jnp.bfloat16
EPS = 1e-6
LANES = 128
SUBLANES = 8
VMEM_LIMIT = 56 * 1024 * 1024
MM_VMEM_BUDGET = 36 * 1024 * 1024
MESH = pl.DeviceIdType.MESH
N_CHIPS = 4
N_DEV = 8

ADAM_LR, ADAM_B1, ADAM_B2, ADAM_EPS, ADAM_WD, ADAM_STEP = 0.001, 0.9, 0.999, 1e-08, 0.01, 10


def _params(**kw):
    return pltpu.CompilerParams(vmem_limit_bytes=VMEM_LIMIT, **kw)


def _sds(shape, dtype):
    return jax.ShapeDtypeStruct(tuple(shape), dtype)


def _part(arr, off=0, width=None, lead=()):
    assert arr.ndim == len(lead) + 2
    return (arr, off, arr.shape[-1] - off if width is None else width, tuple(lead))


def _unit(parts):
    u = 0
    for _, off, w, _ in parts:
        u = math.gcd(u, math.gcd(off, w))
    return u


def _pick(dim, unit, cands, lane):
    for c in cands:
        if c <= dim and dim % c == 0 and unit % c == 0 and (not lane or c % LANES == 0):
            return c
    return min(dim, unit)


_COL_TILES = (1024, 1408, 512, 256, 128)
_ROW_TILES = (512, 256, 1408, 128)


def _mm(name, a, b, *, ta=False, tb=False, out_dtype=F32, add=None, dest=None):
    if not isinstance(a, list):
        a = [_part(a)]
    if not isinstance(b, list):
        b = [_part(b)]
    a_rows, a_cols = a[0][0].shape[-2], sum(p[2] for p in a)
    b_rows, b_cols = b[0][0].shape[-2], sum(p[2] for p in b)
    M, Ka = (a_cols, a_rows) if ta else (a_rows, a_cols)
    Kb, N = (b_cols, b_rows) if tb else (b_rows, b_cols)
    assert Ka == Kb, (name, Ka, Kb)
    K = Ka
    ua, ub = _unit(a), _unit(b)
    shards = 1 if dest is None else dest[3]
    tm = _pick(M, ua if ta else M, _ROW_TILES, ta)
    tn = _pick(N, math.gcd(N if tb else ub, N // shards), _COL_TILES, not tb)
    ku = math.gcd(K if ta else ua, ub if tb else K)
    tk = _pick(K, ku, _COL_TILES, (not ta) or tb)
    a_size = sum(p[0].dtype.itemsize for p in a)
    b_size = sum(p[0].dtype.itemsize for p in b)

    def vmem_estimate():
        return (2 * tm * tk * a_size + 2 * tk * tn * b_size + tm * tn * (2 * jnp.dtype(out_dtype).itemsize + 4)
                + (8 * tm * tn if add is not None else 0))

    while vmem_estimate() > MM_VMEM_BUDGET:
        if tk % 256 == 0 and tk >= tn:
            tk //= 2
        elif tn % 256 == 0:
            tn //= 2
        elif tm % 256 == 0:
            tm //= 2
        elif tk % 256 == 0:
            tk //= 2
        else:
            break
    nm, nn, nk = M // tm, N // tn, K // tk
    a_tile, b_tile = (tm if ta else tk), (tk if tb else tn)
    a_nb, b_nb = a[0][2] // a_tile, b[0][2] // b_tile

    def index(parts, nb, tile, p, row, col):
        base = parts[p][1] // tile
        if len(parts) == 1:
            return parts[p][3] + (row, base + col)
        inside = (col // nb) == p
        return parts[p][3] + (jnp.where(inside, row, 0), base + jnp.clip(col - p * nb, 0, nb - 1))

    def spec(parts, p, shape, fn):
        return pl.BlockSpec((None,) * len(parts[p][3]) + shape, fn)

    in_specs, operands = [], []
    for p in range(len(a)):
        if ta:
            in_specs.append(spec(a, p, (tk, tm), lambda mi, nj, kk, p=p: index(a, a_nb, a_tile, p, kk, mi)))
        else:
            in_specs.append(spec(a, p, (tm, tk), lambda mi, nj, kk, p=p: index(a, a_nb, a_tile, p, mi, kk)))
        operands.append(a[p][0])
    for p in range(len(b)):
        if tb:
            in_specs.append(spec(b, p, (tn, tk), lambda mi, nj, kk, p=p: index(b, b_nb, b_tile, p, nj, kk)))
        else:
            in_specs.append(spec(b, p, (tk, tn), lambda mi, nj, kk, p=p: index(b, b_nb, b_tile, p, kk, nj)))
        operands.append(b[p][0])
    if add is not None:
        in_specs.append(pl.BlockSpec((tm, tn), lambda mi, nj, kk: (mi, nj)))
        operands.append(add)
    aliases = {}
    if dest is None:
        out_spec = pl.BlockSpec((tm, tn), lambda mi, nj, kk: (mi, nj))
        out_shape = _sds((M, N), out_dtype)
    else:
        d_shape, d_prev, d_lead, _ = dest
        nbs = N // shards // tn
        if shards == 1:
            out_spec = pl.BlockSpec((None,) * len(d_lead) + (tm, tn), lambda mi, nj, kk: tuple(d_lead) + (mi, nj))
        else:
            out_spec = pl.BlockSpec((None,) * (len(d_lead) + 1) + (tm, tn),
                                    lambda mi, nj, kk: tuple(d_lead) + (nj // nbs, mi, nj % nbs))
        out_shape = _sds(d_shape, out_dtype)
        if d_prev is not None:
            aliases = {len(operands): 0}
            in_specs.append(pl.BlockSpec(memory_space=pl.ANY))
            operands.append(d_prev)
    n_extra = 1 if aliases else 0
    dims = (((0 if ta else 1,), (1 if tb else 0,)), ((), ()))
    na, nb_ = len(a), len(b)

    def body(*refs):
        a_refs, b_refs = refs[:na], refs[na:na + nb_]
        pos = na + nb_
        add_ref = None
        if add is not None:
            add_ref = refs[pos]
            pos += 1
        pos += n_extra
        o_ref = refs[pos]
        acc_ref = refs[pos + 1] if nk > 1 else None
        mi, nj, kk = pl.program_id(0), pl.program_id(1), pl.program_id(2)
        a_col = mi if ta else kk
        b_col = kk if tb else nj

        def finish(r):
            if add_ref is not None:
                r = r + add_ref[...].astype(F32)
            o_ref[...] = r.astype(o_ref.dtype)

        def step(a_ref, b_ref):
            r = lax.dot_general(a_ref[...].astype(BF16), b_ref[...].astype(BF16), dims, preferred_element_type=F32)
            if nk == 1:
                finish(r)
            else:
                @pl.when(kk == 0)
                def _():
                    acc_ref[...] = r

                @pl.when(kk > 0)
                def _():
                    acc_ref[...] += r

        for pa in range(na):
            for pb in range(nb_):
                if na * nb_ == 1:
                    step(a_refs[pa], b_refs[pb])
                else:
                    cond = jnp.logical_and(a_col // a_nb == pa, b_col // b_nb == pb)
                    pl.when(cond)(functools.partial(step, a_refs[pa], b_refs[pb]))
        if nk > 1:
            @pl.when(kk == nk - 1)
            def _():
                finish(acc_ref[...])

    return pl.pallas_call(
        body, grid=(nm, nn, nk), in_specs=in_specs, out_specs=out_spec, out_shape=out_shape, input_output_aliases=aliases,
        scratch_shapes=[pltpu.VMEM((tm, tn), F32)] if nk > 1 else [],
        compiler_params=_params(dimension_semantics=("parallel", "parallel", "arbitrary")),
        name=name)(*operands)


def _row_spec(tb, part):
    _, off, w, _ = part
    return pl.BlockSpec((tb, w), lambda i, o=off // w: (i, o))


def _full_spec(arr):
    return pl.BlockSpec(arr.shape, lambda i: (0,) * arr.ndim)


def _rows(name, fn, rows, pars, outs, tb=256):
    n = rows[0][0].shape[0]
    tb = min(tb, n)
    nr, npar = len(rows), len(pars)

    def body(*refs):
        r = [ref[...].astype(F32) for ref in refs[:nr]]
        p = [ref[...].astype(F32) for ref in refs[nr:nr + npar]]
        res = fn(*r, *p)
        for o_ref, v in zip(refs[nr + npar:], res):
            o_ref[...] = v.astype(o_ref.dtype)

    out = pl.pallas_call(
        body, grid=(n // tb,),
        in_specs=[_row_spec(tb, p) for p in rows] + [_full_spec(p) for p in pars],
        out_specs=[pl.BlockSpec((tb, w), lambda i: (i, 0)) for w, _ in outs],
        out_shape=[_sds((n, w), dt) for w, dt in outs],
        compiler_params=_params(), name=name)(*[p[0] for p in rows], *pars)
    return out


def _rows_vjp(name, fn, rows, pars, cots, row_grads, par_grads, adds=None, tb=256):
    adds = adds or {}
    n = rows[0][0].shape[0]
    tb = min(tb, n)
    nr, npar, nc = len(rows), len(pars), len(cots)
    add_keys = sorted(adds)
    rg = [j for j, dt in enumerate(row_grads) if dt is not None]
    pg = [j for j, f in enumerate(par_grads) if f]

    def body(*refs):
        pos = 0
        r = [ref[...].astype(F32) for ref in refs[pos:pos + nr]]
        pos += nr
        p = [ref[...].astype(F32) for ref in refs[pos:pos + npar]]
        pos += npar
        ct = tuple(ref[...].astype(F32) for ref in refs[pos:pos + nc])
        pos += nc
        add_v = {k: ref[...].astype(F32) for k, ref in zip(add_keys, refs[pos:pos + len(add_keys)])}
        pos += len(add_keys)
        _, vjp = jax.vjp(lambda *args: tuple(fn(*args)), *r, *p)
        g = vjp(ct)
        for j in rg:
            v = g[j]
            if j in add_v:
                v = v + add_v[j]
            refs[pos][...] = v.astype(refs[pos].dtype)
            pos += 1
        first = pl.program_id(0) == 0
        for j in pg:
            ref = refs[pos]
            pos += 1

            @pl.when(first)
            def _(ref=ref):
                ref[...] = jnp.zeros_like(ref)

            ref[...] += g[nr + j]

    in_specs = ([_row_spec(tb, p) for p in rows] + [_full_spec(p) for p in pars] + [_row_spec(tb, p) for p in cots]
                + [_row_spec(tb, adds[k]) for k in add_keys])
    out_specs = ([pl.BlockSpec((tb, rows[j][2]), lambda i: (i, 0)) for j in rg] + [_full_spec(pars[j]) for j in pg])
    out_shape = ([_sds((n, rows[j][2]), row_grads[j]) for j in rg] + [_sds(pars[j].shape, F32) for j in pg])
    out = pl.pallas_call(
        body, grid=(n // tb,), in_specs=in_specs, out_specs=out_specs, out_shape=out_shape,
        compiler_params=_params(), name=name)(
            *[p[0] for p in rows], *pars, *[p[0] for p in cots], *[adds[k][0] for k in add_keys])
    return out[:len(rg)], out[len(rg):]


def _rms_fn(x, g):
    return (x * lax.rsqrt(jnp.mean(x * x, axis=-1, keepdims=True) + EPS) * g,)


def _lnsilu_fn(u, g, b):
    mu = jnp.mean(u, axis=-1, keepdims=True)
    xc = u - mu
    y = xc * lax.rsqrt(jnp.mean(xc * xc, axis=-1, keepdims=True) + EPS) * g + b
    return (y * jax.nn.sigmoid(y),)


def _merge_fn(ga, gb, gc, ya, yb, yc, ba, bb, bc):
    return (jax.nn.sigmoid(ga + ba) * ya + jax.nn.sigmoid(gb + bb) * yb + jax.nn.sigmoid(gc + bc) * yc,)


def _swiglu_fn(g, u):
    return (g * jax.nn.sigmoid(g) * u,)


def _loss_call(x, tgt, gain):
    n, d = x.shape
    tb = min(256, n)

    def body(x_ref, t_ref, g_ref, l_ref, dx_ref, dg_ref):
        (y,), vjp = jax.vjp(lambda a, b: tuple(_rms_fn(a, b)), x_ref[...], g_ref[...])
        e = y - t_ref[...]
        part = 0.5 * jnp.sum(jnp.mean(e * e, axis=-1, keepdims=True))
        dx, dg = vjp((e * (1.0 / d),))
        dx_ref[...] = dx

        @pl.when(pl.program_id(0) == 0)
        def _():
            l_ref[...] = jnp.zeros_like(l_ref)
            dg_ref[...] = jnp.zeros_like(dg_ref)

        l_ref[...] += jnp.full(l_ref.shape, part, F32)
        dg_ref[...] += dg

    return pl.pallas_call(
        body, grid=(n // tb,),
        in_specs=[pl.BlockSpec((tb, d), lambda i: (i, 0)), pl.BlockSpec((tb, d), lambda i: (i, 0)), _full_spec(gain)],
        out_specs=[pl.BlockSpec((1, LANES), lambda i: (0, 0)), pl.BlockSpec((tb, d), lambda i: (i, 0)), _full_spec(gain)],
        out_shape=[_sds((1, LANES), F32), _sds((n, d), F32), _sds(gain.shape, F32)],
        compiler_params=_params(), name="loss_head")(x, tgt, gain)


FOX_BLOCK = 256
CONV_ROWS = 64
CONV_HALO = 32


def _chunk(i):
    return pl.ds(pl.multiple_of(i * CONV_ROWS, CONV_ROWS), CONV_ROWS)


def _delayed(ext, s):
    return (ext if s == 0 else pltpu.roll(ext, s, 0))[CONV_HALO:]


def _advanced(ext, s):
    return (ext if s == 0 else pltpu.roll(ext, ext.shape[0] - s, 0))[:CONV_ROWS]


def _conv_pass(t, taps, w_ref, lanes, get_u, emit, get_dy=None, dw_acc=None):
    def body(i, carry):
        rows = _chunk(i)
        u = get_u(rows)
        ext = jnp.concatenate([carry, u], axis=0)
        dy = None if get_dy is None else get_dy(rows)
        acc = None
        for k in range(taps):
            sh = _delayed(ext, taps - 1 - k)
            term = w_ref[k:k + 1, lanes] * sh
            acc = term if acc is None else acc + term
            if dy is not None:
                dw_acc[k] += jnp.sum((dy * sh).reshape(CONV_ROWS // SUBLANES, SUBLANES, LANES), axis=0)
        emit(rows, u, acc)
        return u[CONV_ROWS - CONV_HALO:]

    lax.fori_loop(0, t // CONV_ROWS, body, jnp.zeros((CONV_HALO, LANES), F32))


def _conv_pass_t(t, taps, w_ref, lanes, get_dy, emit):
    n = t // CONV_ROWS

    def body(j, carry):
        rows = _chunk(n - 1 - j)
        dy = get_dy(rows)
        ext = jnp.concatenate([dy, carry], axis=0)
        acc = None
        for k in range(taps):
            term = w_ref[k:k + 1, lanes] * _advanced(ext, taps - 1 - k)
            acc = term if acc is None else acc + term
        emit(rows, acc)
        return dy[:CONV_HALO]

    lax.fori_loop(0, n, body, jnp.zeros((CONV_HALO, LANES), F32))


def _chan_block(d):
    return min(256, d)


def _lane_groups(cb):
    return [slice(g * LANES, (g + 1) * LANES) for g in range(cb // LANES)]


def _conv_a_fwd(za, w):
    t, d = za.shape[0], za.shape[1] // 3
    taps, cb = w.shape[0], _chan_block(d)
    nb = d // cb
    assert taps - 1 <= CONV_HALO and t % CONV_ROWS == 0

    def body(ab_ref, ac_ref, au_ref, w_ref, o_ref):
        for lanes in _lane_groups(cb):
            def emit(rows, u, s, lanes=lanes):
                o_ref[rows, lanes] = (ab_ref[rows, lanes] * s).astype(o_ref.dtype)

            _conv_pass(t, taps, w_ref, lanes, lambda rows, lanes=lanes: ac_ref[rows, lanes] * au_ref[rows, lanes], emit)

    return pl.pallas_call(
        body, grid=(nb,),
        in_specs=[pl.BlockSpec((t, cb), lambda j, o=o: (0, o * nb + j)) for o in range(3)]
        + [pl.BlockSpec((taps, cb), lambda j: (0, j))],
        out_specs=pl.BlockSpec((t, cb), lambda j: (0, j)),
        out_shape=_sds((t, d), BF16), compiler_params=_params(), name="conv_a_fwd")(za, za, za, w)


def _finish_taps(dw_ref, dw_acc, lanes, taps):
    for k in range(taps):
        dw_ref[k:k + 1, lanes] = jnp.sum(dw_acc[k], axis=0, keepdims=True)


def _conv_a_bwd(za, w, dpa):
    t, d = za.shape[0], za.shape[1] // 3
    taps, cb = w.shape[0], _chan_block(d)
    nb = d // cb

    def body(ab_ref, ac_ref, au_ref, w_ref, dy_ref, dab_ref, dac_ref, dau_ref, dw_ref, dw_acc):
        for lanes in _lane_groups(cb):
            dw_acc[...] = jnp.zeros_like(dw_acc)

            def ds(rows, lanes=lanes):
                return dy_ref[rows, lanes] * ab_ref[rows, lanes]

            def emit_s(rows, u, s, lanes=lanes):
                dab_ref[rows, lanes] = (dy_ref[rows, lanes] * s).astype(dab_ref.dtype)

            def emit_du(rows, du, lanes=lanes):
                dac_ref[rows, lanes] = (du * au_ref[rows, lanes]).astype(dac_ref.dtype)
                dau_ref[rows, lanes] = (du * ac_ref[rows, lanes]).astype(dau_ref.dtype)

            _conv_pass(t, taps, w_ref, lanes, lambda rows, lanes=lanes: ac_ref[rows, lanes] * au_ref[rows, lanes], emit_s,
                       get_dy=ds, dw_acc=dw_acc)
            _conv_pass_t(t, taps, w_ref, lanes, ds, emit_du)
            _finish_taps(dw_ref, dw_acc, lanes, taps)

    blk = pl.BlockSpec((t, cb), lambda j: (0, j))
    return pl.pallas_call(
        body, grid=(nb,),
        in_specs=[pl.BlockSpec((t, cb), lambda j, o=o: (0, o * nb + j)) for o in range(3)]
        + [pl.BlockSpec((taps, cb), lambda j: (0, j)), blk],
        out_specs=[blk, blk, blk, pl.BlockSpec((taps, cb), lambda j: (0, j))],
        out_shape=[_sds((t, d), BF16)] * 3 + [_sds((taps, d), F32)],
        scratch_shapes=[pltpu.VMEM((taps, SUBLANES, LANES), F32)],
        compiler_params=_params(), name="conv_a_bwd")(za, za, za, w, dpa)


def _conv_c_fwd(zcg, w, bias):
    t, d = zcg.shape[0], w.shape[1]
    taps, cb = w.shape[0], _chan_block(d)
    nb = d // cb
    assert taps - 1 <= CONV_HALO and t % CONV_ROWS == 0

    def body(cv_ref, cg_ref, w_ref, b_ref, o_ref):
        for lanes in _lane_groups(cb):
            def emit(rows, u, s, lanes=lanes):
                o_ref[rows, lanes] = s + b_ref[:, lanes]

            _conv_pass(t, taps, w_ref, lanes,
                       lambda rows, lanes=lanes: cv_ref[rows, lanes] * jax.nn.sigmoid(cg_ref[rows, lanes]), emit)

    return pl.pallas_call(
        body, grid=(nb,),
        in_specs=[pl.BlockSpec((t, cb), lambda j, o=o: (0, o * nb + j)) for o in range(2)]
        + [pl.BlockSpec((taps, cb), lambda j: (0, j)), pl.BlockSpec((1, cb), lambda j: (0, j))],
        out_specs=pl.BlockSpec((t, cb), lambda j: (0, j)),
        out_shape=_sds((t, d), F32), compiler_params=_params(), name="conv_c_fwd")(zcg, zcg, w, bias)


def _conv_c_bwd(zcg, w, du2):
    t, d = zcg.shape[0], w.shape[1]
    taps, cb = w.shape[0], _chan_block(d)
    nb = d // cb

    def body(cv_ref, cg_ref, w_ref, dy_ref, dcv_ref, dcg_ref, dw_ref, db_ref, dw_acc, db_acc):
        for lanes in _lane_groups(cb):
            dw_acc[...] = jnp.zeros_like(dw_acc)
            db_acc[...] = jnp.zeros_like(db_acc)

            def dy(rows, lanes=lanes):
                return dy_ref[rows, lanes]

            def emit_s(rows, u, s, lanes=lanes):
                db_acc[...] += jnp.sum(dy_ref[rows, lanes].reshape(CONV_ROWS // SUBLANES, SUBLANES, LANES), axis=0)

            def emit_du(rows, du, lanes=lanes):
                cv, sg = cv_ref[rows, lanes], jax.nn.sigmoid(cg_ref[rows, lanes])
                dcv_ref[rows, lanes] = (du * sg).astype(dcv_ref.dtype)
                dcg_ref[rows, lanes] = (du * cv * sg * (1.0 - sg)).astype(dcg_ref.dtype)

            _conv_pass(t, taps, w_ref, lanes,
                       lambda rows, lanes=lanes: cv_ref[rows, lanes] * jax.nn.sigmoid(cg_ref[rows, lanes]), emit_s,
                       get_dy=dy, dw_acc=dw_acc)
            _conv_pass_t(t, taps, w_ref, lanes, dy, emit_du)
            _finish_taps(dw_ref, dw_acc, lanes, taps)
            db_ref[:, lanes] = jnp.sum(db_acc[...], axis=0, keepdims=True)

    blk = pl.BlockSpec((t, cb), lambda j: (0, j))
    return pl.pallas_call(
        body, grid=(nb,),
        in_specs=[pl.BlockSpec((t, cb), lambda j, o=o: (0, o * nb + j)) for o in range(2)]
        + [pl.BlockSpec((taps, cb), lambda j: (0, j)), blk],
        out_specs=[blk, blk, pl.BlockSpec((taps, cb), lambda j: (0, j)), pl.BlockSpec((1, cb), lambda j: (0, j))],
        out_shape=[_sds((t, d), BF16)] * 2 + [_sds((taps, d), F32), _sds((1, d), F32)],
        scratch_shapes=[pltpu.VMEM((taps, SUBLANES, LANES), F32), pltpu.VMEM((SUBLANES, LANES), F32)],
        compiler_params=_params(), name="conv_c_bwd")(zcg, zcg, w, du2)


def _tri(tb, t, i, lower):
    r = lax.broadcasted_iota(jnp.int32, (tb, t), 0) + i * tb
    c = lax.broadcasted_iota(jnp.int32, (tb, t), 1)
    return ((r >= c) if lower else (c >= r)).astype(F32)


def _fox_cumsum(zf, bf):
    t = zf.shape[0]
    tb = min(256, t)

    def body(z_ref, b_ref, c_ref):
        logf = jax.nn.log_sigmoid(z_ref[...] + b_ref[...])
        c_ref[...] = jnp.dot(_tri(tb, t, pl.program_id(0), True), logf, precision=lax.Precision.HIGHEST,
                             preferred_element_type=F32)

    return pl.pallas_call(
        body, grid=(t // tb,), in_specs=[_full_spec(zf), _full_spec(bf)],
        out_specs=pl.BlockSpec((tb, LANES), lambda i: (i, 0)), out_shape=_sds((t, LANES), F32),
        compiler_params=_params(), name="fox_cumsum")(zf, bf)


def _fox_cumsum_bwd(zf, bf, dcc, dcr_t):
    t = zf.shape[0]
    tb = min(256, t)

    def body(z_ref, b_ref, dcc_ref, dcr_ref, dz_ref, db_ref, dc_ref):
        i = pl.program_id(0)

        @pl.when(i == 0)
        def _():
            dc_ref[...] = dcr_ref[...] + jnp.sum(dcc_ref[...], axis=0)
            db_ref[...] = jnp.zeros_like(db_ref)

        dlogf = jnp.dot(_tri(tb, t, i, False), dc_ref[...], precision=lax.Precision.HIGHEST, preferred_element_type=F32)
        dz = dlogf * jax.nn.sigmoid(-(z_ref[...] + b_ref[...]))
        dz_ref[...] = dz.astype(dz_ref.dtype)
        db_ref[...] += jnp.sum(dz, axis=0, keepdims=True)

    return pl.pallas_call(
        body, grid=(t // tb,),
        in_specs=[pl.BlockSpec((tb, LANES), lambda i: (i, 0)), _full_spec(bf), _full_spec(dcc), _full_spec(dcr_t)],
        out_specs=[pl.BlockSpec((tb, LANES), lambda i: (i, 0)), pl.BlockSpec((1, LANES), lambda i: (0, 0))],
        out_shape=[_sds((t, LANES), BF16), _sds((1, LANES), F32)],
        scratch_shapes=[pltpu.VMEM((t, LANES), F32)],
        compiler_params=_params(), name="fox_cumsum_bwd")(zf, bf, dcc, dcr_t)


def _fox_block(q, k, v, cc, cr, *, p, q0, hd):
    tq, kext = q.shape[0], k.shape[0]
    nh = LANES // hd
    lane = lax.broadcasted_iota(jnp.int32, (1, LANES), 1)
    causal = (lax.broadcasted_iota(jnp.int32, (tq, kext), 0) + q0) >= lax.broadcasted_iota(jnp.int32, (tq, kext), 1)
    kb, vb = k.astype(BF16), v.astype(BF16)
    out = jnp.zeros((tq, LANES), F32)
    for h in range(nh):
        m = (lane // hd) == h
        qh = jnp.where(m, q, 0.0).astype(BF16)
        s = lax.dot_general(qh, kb, (((1,), (1,)), ((), ())), preferred_element_type=F32) * (hd ** -0.5)
        c_q = jnp.sum(jnp.where(lane == p * nh + h, cc, 0.0), axis=1, keepdims=True)
        s = jnp.where(causal, s + (c_q - cr[h]), -1e30)
        e = jnp.exp(s - lax.stop_gradient(jnp.max(s, axis=1, keepdims=True)))
        prob = e / jnp.sum(e, axis=1, keepdims=True)
        o = jnp.dot(prob.astype(BF16), vb, preferred_element_type=F32)
        out = out + jnp.where(m, o, 0.0)
    return out


def _fox_dims(zqkv, crow):
    t, d = zqkv.shape[0], zqkv.shape[1] // 3
    hd = d // crow.shape[0]
    return t, d, hd, d // LANES, LANES // hd, min(FOX_BLOCK, t)


def _fox_specs(i, tq, nblk, nh):
    kext = (i + 1) * tq
    return [pl.BlockSpec((tq, LANES), lambda p: (i, p)), pl.BlockSpec((kext, LANES), lambda p: (0, nblk + p)),
            pl.BlockSpec((kext, LANES), lambda p: (0, 2 * nblk + p)), pl.BlockSpec((tq, LANES), lambda p: (i, 0)),
            pl.BlockSpec((nh, 1, kext), lambda p: (p, 0, 0))]


def _fox_fwd(zqkv, c, crow):
    t, d, hd, nblk, nh, tq = _fox_dims(zqkv, crow)
    o = None
    for i in range(t // tq):
        def body(q_ref, k_ref, v_ref, cc_ref, cr_ref, *rest):
            o_ref = rest[-1]
            o_ref[...] = _fox_block(q_ref[...].astype(F32), k_ref[...], v_ref[...], cc_ref[...], cr_ref[...],
                                    p=pl.program_id(0), q0=i * tq, hd=hd).astype(o_ref.dtype)

        prev = [] if o is None else [o]
        o = pl.pallas_call(
            body, grid=(nblk,),
            in_specs=_fox_specs(i, tq, nblk, nh) + [pl.BlockSpec(memory_space=pl.ANY)] * len(prev),
            out_specs=pl.BlockSpec((tq, LANES), lambda p: (i, p)), out_shape=_sds((t, d), BF16),
            input_output_aliases={5: 0} if prev else {},
            compiler_params=_params(), name=f"fox_fwd_q{i}")(zqkv, zqkv, zqkv, c, crow, *prev)
    return o


def _fox_bwd(zqkv, c, crow, do):
    t, d, hd, nblk, nh, tq = _fox_dims(zqkv, crow)
    nq = t // tq
    acc = None
    for i in reversed(range(nq)):
        kext = (i + 1) * tq
        first = acc is None

        def body(q_ref, k_ref, v_ref, cc_ref, cr_ref, do_ref, *rest, first=first, i=i):
            if first:
                dq_ref, dk_ref, dv_ref, dcc_ref, dcr_ref = rest
            else:
                dk_in, dv_in, dcr_in = rest[2:5]
                dq_ref, dk_ref, dv_ref, dcc_ref, dcr_ref = rest[5:]
            f = functools.partial(_fox_block, p=pl.program_id(0), q0=i * tq, hd=hd)
            _, vjp = jax.vjp(f, q_ref[...].astype(F32), k_ref[...].astype(F32), v_ref[...].astype(F32), cc_ref[...], cr_ref[...])
            dq, dk, dv, dcc, dcr = vjp(do_ref[...].astype(F32))
            dq_ref[...] = dq.astype(dq_ref.dtype)
            dcc_ref[...] = dcc
            if first:
                dk_ref[...], dv_ref[...], dcr_ref[...] = dk, dv, dcr
            else:
                dk_ref[...] = dk_in[...] + dk
                dv_ref[...] = dv_in[...] + dv
                dcr_ref[...] = dcr_in[...] + dcr

        qblk = pl.BlockSpec((tq, LANES), lambda p: (i, p))
        kblk = pl.BlockSpec((kext, LANES), lambda p: (0, p))
        cblk = pl.BlockSpec((None, tq, LANES), lambda p: (p, i, 0))
        rblk = pl.BlockSpec((nh, 1, kext), lambda p: (p, 0, 0))
        any_ = pl.BlockSpec(memory_space=pl.ANY)
        if first:
            prev, prev_specs, aliases = [], [], {}
        else:
            prev = [acc[0], acc[3], acc[1], acc[2], acc[4]]
            prev_specs = [any_, any_, kblk, kblk, rblk]
            aliases = {6: 0, 7: 3, 8: 1, 9: 2, 10: 4}
        acc = pl.pallas_call(
            body, grid=(nblk,),
            in_specs=_fox_specs(i, tq, nblk, nh) + [qblk] + prev_specs,
            out_specs=[qblk, kblk, kblk, cblk, rblk],
            out_shape=[_sds((t, d), BF16), _sds((t, d), F32), _sds((t, d), F32), _sds((nblk, t, LANES), F32),
                       _sds(crow.shape, F32)],
            input_output_aliases=aliases,
            compiler_params=_params(), name=f"fox_bwd_q{i}")(zqkv, zqkv, zqkv, c, crow, do, *prev)
    return acc


def _xattn_block(q, k, v):
    s = lax.dot_general(q.astype(BF16), k.astype(BF16), (((1,), (1,)), ((), ())), preferred_element_type=F32)
    s = s * (q.shape[1] ** -0.5)
    e = jnp.exp(s - lax.stop_gradient(jnp.max(s, axis=1, keepdims=True)))
    prob = e / jnp.sum(e, axis=1, keepdims=True)
    return jnp.dot(prob.astype(BF16), v.astype(BF16), preferred_element_type=F32)


def _xattn_fwd(qx, kv):
    t, dx = qx.shape
    nm, nh, tq = kv.shape[0], qx.shape[1] // LANES, min(512, qx.shape[0])

    def body(q_ref, k_ref, v_ref, o_ref):
        o_ref[...] = _xattn_block(q_ref[...], k_ref[...], v_ref[...]).astype(o_ref.dtype)

    return pl.pallas_call(
        body, grid=(nh, t // tq),
        in_specs=[pl.BlockSpec((tq, LANES), lambda h, i: (i, h)), pl.BlockSpec((nm, LANES), lambda h, i: (0, h)),
                  pl.BlockSpec((nm, LANES), lambda h, i: (0, nh + h))],
        out_specs=pl.BlockSpec((tq, LANES), lambda h, i: (i, h)), out_shape=_sds((t, dx), BF16),
        compiler_params=_params(), name="xattn_fwd")(qx, kv, kv)


def _xattn_bwd(qx, kv, do):
    t, dx = qx.shape
    nm, nh, tq = kv.shape[0], qx.shape[1] // LANES, min(512, qx.shape[0])

    def body(q_ref, k_ref, v_ref, do_ref, dq_ref, dk_ref, dv_ref):
        _, vjp = jax.vjp(_xattn_block, q_ref[...].astype(F32), k_ref[...].astype(F32), v_ref[...].astype(F32))
        dq, dk, dv = vjp(do_ref[...].astype(F32))
        dq_ref[...] = dq.astype(dq_ref.dtype)

        @pl.when(pl.program_id(1) == 0)
        def _():
            dk_ref[...] = jnp.zeros_like(dk_ref)
            dv_ref[...] = jnp.zeros_like(dv_ref)

        dk_ref[...] += dk
        dv_ref[...] += dv

    qblk = pl.BlockSpec((tq, LANES), lambda h, i: (i, h))
    kblk = pl.BlockSpec((nm, LANES), lambda h, i: (0, h))
    return pl.pallas_call(
        body, grid=(nh, t // tq),
        in_specs=[qblk, kblk, pl.BlockSpec((nm, LANES), lambda h, i: (0, nh + h)), qblk],
        out_specs=[qblk, kblk, kblk],
        out_shape=[_sds((t, dx), BF16), _sds((nm, dx), F32), _sds((nm, dx), F32)],
        compiler_params=_params(), name="xattn_bwd")(qx, kv, kv, do)


def _row_tile(rows, cols, budget=1 << 20):
    tb = rows
    while tb % (4 * SUBLANES) == 0 and tb * cols * 4 > budget:
        tb //= 2
    return tb


def _adamw(w, g, m, v):
    shape = w.shape
    cols = shape[-1]
    w2, g2, m2, v2 = (a.reshape(-1, cols) for a in (w, g, m, v))
    rows = w2.shape[0]
    tb = _row_tile(rows, cols)

    def body(w_ref, g_ref, m_ref, v_ref, d_ref, nm_ref, nv_ref):
        gg = g_ref[...]
        nm = ADAM_B1 * m_ref[...] + (1.0 - ADAM_B1) * gg
        nv = ADAM_B2 * v_ref[...] + (1.0 - ADAM_B2) * (gg * gg)
        m_hat = nm / (1.0 - ADAM_B1 ** ADAM_STEP)
        v_hat = nv / (1.0 - ADAM_B2 ** ADAM_STEP)
        d_ref[...] = -ADAM_LR * (m_hat / (jnp.sqrt(v_hat) + ADAM_EPS) + ADAM_WD * w_ref[...])
        nm_ref[...] = nm
        nv_ref[...] = nv

    blk = pl.BlockSpec((tb, cols), lambda i: (i, 0))
    out = pl.pallas_call(
        body, grid=(rows // tb,), in_specs=[blk] * 4, out_specs=[blk] * 3, out_shape=[_sds((rows, cols), F32)] * 3,
        compiler_params=_params(), name="adamw")(w2, g2, m2, v2)
    return tuple(o.reshape(shape) for o in out)


def _sum_devices(gathered, rows):
    cols = gathered.shape[1]

    def body(g_ref, o_ref):
        acc = g_ref[0:rows, :]
        for dev in range(1, N_DEV):
            acc = acc + g_ref[dev * rows:(dev + 1) * rows, :]
        o_ref[...] = acc

    return pl.pallas_call(body, out_shape=_sds((rows, cols), F32), compiler_params=_params(), name="sum_devices")(gathered)


def _place():
    x, y, c = lax.axis_index("x"), lax.axis_index("y"), lax.axis_index("c")
    chips = [(1 - x, y), (x, 1 - y), (1 - x, 1 - y)]
    return x, y, c, chips


def _chip_id(chip):
    return 2 * chip[0] + chip[1]


def _half(c, rows):
    rh = rows // 2
    return pl.ds(pl.multiple_of(c * rh, 16), rh)


_HBM = pl.BlockSpec(memory_space=pltpu.HBM)


def _allgather8(x_shard):
    m_per, n = x_shard.shape

    def body(x_ref, out_ref, send_sems, recv_sems, local_sem):
        x, y, c, chips = _place()
        me, sibling = (x, y, c), (x, y, 1 - c)

        def rows(px, py, pc):
            return out_ref.at[pl.ds((4 * px + 2 * py + pc) * m_per, m_per), :]

        def copy(k, block, to, src=None):
            return pltpu.make_async_remote_copy(
                src_ref=rows(*block) if src is None else src, dst_ref=rows(*block),
                send_sem=send_sems.at[k], recv_sem=recv_sems.at[k], device_id=to, device_id_type=MESH)

        mine = pltpu.make_async_copy(x_ref, rows(*me), local_sem)
        mine.start()
        first = [copy(0, me, sibling, src=x_ref)]
        first += [copy(1 + j, me, (*chip, c), src=x_ref) for j, chip in enumerate(chips)]
        for cp in first:
            cp.start()
        passed = [copy(4 + j, (*chip, c), sibling) for j, chip in enumerate(chips)]
        for j, chip in enumerate(chips):
            copy(1 + j, (*chip, c), me).wait_recv()
            passed[j].start()
        copy(0, sibling, me).wait_recv()
        for j, chip in enumerate(chips):
            copy(4 + j, (*chip, 1 - c), me).wait_recv()
        for cp in first + passed:
            cp.wait_send()
        mine.wait()

    return pl.pallas_call(
        body, out_shape=_sds((N_DEV * m_per, n), x_shard.dtype),
        in_specs=[pl.BlockSpec(memory_space=pltpu.VMEM)], out_specs=pl.BlockSpec(memory_space=pltpu.VMEM),
        scratch_shapes=[pltpu.SemaphoreType.DMA((7,)), pltpu.SemaphoreType.DMA((7,)), pltpu.SemaphoreType.DMA],
        compiler_params=_params(), name="allgather8")(x_shard)


def _gather_weights(ws):
    n = len(ws)

    def body(*refs):
        w, out = refs[:n], refs[n:2 * n]
        send_i, recv_i, send_d, recv_d = refs[2 * n:]
        x, y, c, chips = _place()
        sibling = (x, y, 1 - c)
        me_chip = 2 * x + y

        def slab(i, chip, half):
            return out[i].at[:, chip, _half(half, w[i].shape[1]), :]

        def over_ici(i, j, src_chip, to, src=None):
            return pltpu.make_async_remote_copy(
                src_ref=slab(i, src_chip, c) if src is None else src, dst_ref=slab(i, src_chip, c),
                send_sem=send_i.at[3 * i + j], recv_sem=recv_i.at[3 * i + j], device_id=to, device_id_type=MESH)

        def over_d2d(i, j, half):
            return pltpu.make_async_remote_copy(
                src_ref=slab(i, _chip_id(chips[j]), half), dst_ref=slab(i, _chip_id(chips[j]), half),
                send_sem=send_d.at[3 * i + j], recv_sem=recv_d.at[3 * i + j], device_id=sibling, device_id_type=MESH)

        sent = []
        for i in range(n):
            for j, chip in enumerate(chips):
                cp = over_ici(i, j, me_chip, (*chip, c), src=w[i].at[:, _half(c, w[i].shape[1]), :])
                cp.start()
                sent.append(cp)
        for i in range(n):
            for j, chip in enumerate(chips):
                over_ici(i, j, _chip_id(chip), (x, y, c)).wait_recv()
                cp = over_d2d(i, j, c)
                cp.start()
                sent.append(cp)
        for i in range(n):
            for j in range(3):
                over_d2d(i, j, 1 - c).wait_recv()
        for cp in sent:
            cp.wait_send()

    return pl.pallas_call(
        body, in_specs=[_HBM] * n, out_specs=[_HBM] * n,
        out_shape=[_sds((a.shape[0], N_CHIPS) + a.shape[1:], a.dtype) for a in ws],
        scratch_shapes=[pltpu.SemaphoreType.DMA((3 * n,)) for _ in range(4)],
        compiler_params=_params(), name="gather_weights")(*ws)


def _swap_halves(gs):
    n = len(gs)

    def body(*refs):
        g, out = refs[:n], refs[n:2 * n]
        send, recv = refs[2 * n:]
        x, y, c, _ = _place()
        cps = [pltpu.make_async_remote_copy(
            src_ref=g[i].at[:, :, _half(1 - c, g[i].shape[2]), :], dst_ref=out[i],
            send_sem=send.at[i], recv_sem=recv.at[i], device_id=(x, y, 1 - c), device_id_type=MESH) for i in range(n)]
        for cp in cps:
            cp.start()
        for cp in cps:
            cp.wait()

    return pl.pallas_call(
        body, in_specs=[_HBM] * n, out_specs=[_HBM] * n,
        out_shape=[_sds(a.shape[:2] + (a.shape[2] // 2, a.shape[3]), a.dtype) for a in gs],
        scratch_shapes=[pltpu.SemaphoreType.DMA((n,)), pltpu.SemaphoreType.DMA((n,))],
        compiler_params=_params(), name="swap_halves")(*gs)


def _exchange_chips(ss):
    n = len(ss)

    def body(*refs):
        s, out = refs[:n], refs[n:2 * n]
        send, recv = refs[2 * n:]
        x, y, c, chips = _place()
        cps = []
        for i in range(n):
            for j, chip in enumerate(chips):
                cps.append(pltpu.make_async_remote_copy(
                    src_ref=s[i].at[:, _chip_id(chip)], dst_ref=out[i].at[j],
                    send_sem=send.at[3 * i + j], recv_sem=recv.at[3 * i + j], device_id=(*chip, c), device_id_type=MESH))
        for cp in cps:
            cp.start()
        for cp in cps:
            cp.wait()

    return pl.pallas_call(
        body, in_specs=[_HBM] * n, out_specs=[_HBM] * n,
        out_shape=[_sds((3, a.shape[0]) + a.shape[2:], a.dtype) for a in ss],
        scratch_shapes=[pltpu.SemaphoreType.DMA((3 * n,)), pltpu.SemaphoreType.DMA((3 * n,))],
        compiler_params=_params(), name="exchange_chips")(*ss)


def _join_halves(fs):
    n = len(fs)

    def body(*refs):
        f, out = refs[:n], refs[n:2 * n]
        send, recv = refs[2 * n:]
        x, y, c, _ = _place()

        def half(ref, which):
            return ref.at[:, _half(which, ref.shape[1]), :]

        for i in range(n):
            pltpu.make_async_remote_copy(
                src_ref=half(f[i], c), dst_ref=half(out[i], c), send_sem=send.at[i], recv_sem=recv.at[i],
                device_id=(x, y, 1 - c), device_id_type=MESH).start()
        for i in range(n):
            arrive = pltpu.make_async_remote_copy(
                src_ref=half(f[i], c), dst_ref=half(out[i], 1 - c), send_sem=send.at[i], recv_sem=recv.at[i],
                device_id=(x, y, 1 - c), device_id_type=MESH)
            arrive.wait_recv()
            arrive.wait_send()

    return pl.pallas_call(
        body, in_specs=[_HBM] * n, out_specs=[_HBM] * n, out_shape=[_sds(a.shape, a.dtype) for a in fs],
        input_output_aliases={i: i for i in range(n)},
        scratch_shapes=[pltpu.SemaphoreType.DMA((n,)), pltpu.SemaphoreType.DMA((n,))],
        compiler_params=_params(), name="join_halves")(*fs)


def _add_sibling(g, got, c_idx):
    nl, ns, r, cols = g.shape
    rh = r // 2
    tb = _row_tile(rh, cols)
    nb = rh // tb
    g3, got3 = g.reshape(nl * ns, r, cols), got.reshape(nl * ns, rh, cols)

    def body(c_ref, a_ref, b_ref, o_ref):
        o_ref[...] = (a_ref[...].astype(F32) + b_ref[...].astype(F32)).astype(o_ref.dtype)

    out = pl.pallas_call(
        body,
        grid_spec=pltpu.PrefetchScalarGridSpec(
            num_scalar_prefetch=1, grid=(nl * ns, nb),
            in_specs=[pl.BlockSpec((None, tb, cols), lambda s, i, c_ref: (s, c_ref[0] * nb + i, 0)),
                      pl.BlockSpec((None, tb, cols), lambda s, i, c_ref: (s, i, 0))],
            out_specs=pl.BlockSpec((None, tb, cols), lambda s, i, c_ref: (s, i, 0))),
        out_shape=_sds((nl * ns, rh, cols), BF16), compiler_params=_params(), name="add_sibling")(c_idx, g3, got3)
    return out.reshape(nl, ns, rh, cols)


def _add_chips(s, got, chip_idx, c_idx):
    nl, ns, rh, cols = s.shape
    tb = _row_tile(rh, cols)
    nb = rh // tb

    def body(k_ref, c_ref, a_ref, b0_ref, b1_ref, b2_ref, o_ref):
        o_ref[...] = ((a_ref[...].astype(F32) + b0_ref[...].astype(F32)) + b1_ref[...].astype(F32)) + b2_ref[...].astype(F32)

    return pl.pallas_call(
        body,
        grid_spec=pltpu.PrefetchScalarGridSpec(
            num_scalar_prefetch=2, grid=(nl, nb),
            in_specs=[pl.BlockSpec((None, None, tb, cols), lambda l, i, k_ref, c_ref: (l, k_ref[0], i, 0))]
            + [pl.BlockSpec((None, None, tb, cols), lambda l, i, k_ref, c_ref, j=j: (j, l, i, 0)) for j in range(3)],
            out_specs=pl.BlockSpec((None, tb, cols), lambda l, i, k_ref, c_ref: (l, c_ref[0] * nb + i, 0))),
        out_shape=_sds((nl, 2 * rh, cols), F32), compiler_params=_params(), name="add_chips")(chip_idx, c_idx, s, got, got, got)


def _in_pieces(d, heads, ws):
    groups, start = [], 0
    for k in range(12):
        if k == 6:
            groups.append(("f", 0, start, heads))
            start += heads
        else:
            groups.append(("main", (k - (k > 6)) * d, start, d))
            start += d
    pieces = []
    for chip in range(N_CHIPS):
        lo, hi = chip * ws, (chip + 1) * ws
        for dest, dcol, gstart, w in groups:
            a, b = max(lo, gstart), min(hi, gstart + w)
            if a < b:
                pieces.append((chip, a - lo, dest, dcol + a - gstart, b - a))
    return pieces


def _w_in_aligned(gathered, l, d, heads):
    rows, ws = gathered.shape[2], gathered.shape[3]
    tb = min(256, rows)
    pieces = _in_pieces(d, heads, ws)

    def body(g_ref, m_ref, f_ref):
        f_ref[...] = jnp.zeros_like(f_ref)
        for chip, sc, dest, dc, w in pieces:
            (m_ref if dest == "main" else f_ref)[:, dc:dc + w] = g_ref[chip, :, sc:sc + w]

    return pl.pallas_call(
        body, grid=(rows // tb,), in_specs=[pl.BlockSpec((None, N_CHIPS, tb, ws), lambda i: (l, 0, i, 0))],
        out_specs=[pl.BlockSpec((tb, 11 * d), lambda i: (i, 0)), pl.BlockSpec((tb, LANES), lambda i: (i, 0))],
        out_shape=[_sds((rows, 11 * d), gathered.dtype), _sds((rows, LANES), gathered.dtype)],
        compiler_params=_params(), name="w_in_aligned")(gathered)


def _w_in_shards(gm, gf, prev, l, shape, heads):
    rows, d = gm.shape[0], gm.shape[1] // 11
    ws = shape[3]
    tb = min(256, rows)
    pieces = _in_pieces(d, heads, ws)

    def body(m_ref, f_ref, *rest):
        o_ref = rest[-1]
        for chip, sc, dest, dc, w in pieces:
            o_ref[chip, :, sc:sc + w] = (m_ref if dest == "main" else f_ref)[:, dc:dc + w]

    extra = [] if prev is None else [prev]
    return pl.pallas_call(
        body, grid=(rows // tb,),
        in_specs=[pl.BlockSpec((tb, 11 * d), lambda i: (i, 0)), pl.BlockSpec((tb, LANES), lambda i: (i, 0))]
        + [pl.BlockSpec(memory_space=pl.ANY)] * len(extra),
        out_specs=pl.BlockSpec((None, N_CHIPS, tb, ws), lambda i: (l, 0, i, 0)), out_shape=_sds(shape, gm.dtype),
        input_output_aliases={2: 0} if extra else {},
        compiler_params=_params(), name="w_in_shards")(gm, gf, *extra)


def _pad_lanes(a, width=LANES):
    return jnp.pad(a, ((0, 0), (0, width - a.shape[1])))


def _cols(operand):
    return sum(p[2] for p in operand) if isinstance(operand, list) else operand.shape[1]


def _local_step(x, mem, tgt, sp, wt, dests=None):
    t, d = x.shape
    depth = sp["mix_norm"].shape[0]
    heads = sp["b_forget"].shape[1]
    f = _cols(wt["gu"][0]) // 2
    row = lambda a, l: a[l:l + 1]

    memn, = _rows("rms_mem", _rms_fn, [_part(mem)], [sp["mem_norm"][None]], [(d, BF16)])
    saved = []
    for l in range(depth):
        sv = {"x0": x}
        h1, = _rows("rms_mix", _rms_fn, [_part(x)], [row(sp["mix_norm"], l)], [(d, BF16)])
        wm = wt["main"][l]
        za = _mm("mm_in_a", h1, [_part(wm, 0, 3 * d)])
        zqkv = _mm("mm_in_qkv", h1, [_part(wm, 3 * d, 3 * d)], out_dtype=BF16)
        zcg = _mm("mm_in_cg", h1, [_part(wm, 6 * d, 5 * d)])
        zf = _mm("mm_in_f", h1, wt["f"][l])
        bf = _pad_lanes(row(sp["b_forget"], l))
        c = _fox_cumsum(zf, bf)
        crow = c[:, :heads].T.reshape(heads, 1, t)
        o = _fox_fwd(zqkv, c, crow)
        pa = _conv_a_fwd(za, sp["conv_a"][l])
        u2 = _conv_c_fwd(zcg, sp["conv_c"][l], row(sp["conv_c_bias"], l))
        u3, = _rows("ln_silu", _lnsilu_fn, [_part(u2)], [row(sp["ln_c_gain"], l), row(sp["ln_c_bias"], l)], [(d, BF16)])
        ya = _mm("mm_out_a", pa, wt["out_a"][l])
        yb = _mm("mm_out_b", o, wt["out_b"][l])
        yc = _mm("mm_out_c", u3, wt["out_c"][l])
        bg = [sp["b_gate"][l:l + 1, k * d:(k + 1) * d] for k in range(3)]
        gate_rows = [_part(zcg, (2 + k) * d, d) for k in range(3)] + [_part(ya), _part(yb), _part(yc)]
        mg, = _rows("merge", _merge_fn, gate_rows, bg, [(d, BF16)])
        x1 = _mm("mm_o", mg, wt["o"][l], add=x)
        h2, = _rows("rms_x", _rms_fn, [_part(x1)], [row(sp["xattn_norm"], l)], [(d, BF16)])
        qx = _mm("mm_xq", h2, wt["xq"][l], out_dtype=BF16)
        kv = _mm("mm_xkv", memn, wt["xkv"][l], out_dtype=BF16)
        ox = _xattn_fwd(qx, kv)
        x2 = _mm("mm_xo", ox, wt["xo"][l], add=x1)
        h3, = _rows("rms_ffn", _rms_fn, [_part(x2)], [row(sp["ffn_norm"], l)], [(d, BF16)])
        gu = _mm("mm_gu", h3, wt["gu"][l])
        act, = _rows("swiglu", _swiglu_fn, [_part(gu, 0, f), _part(gu, f, f)], [], [(f, BF16)])
        x3 = _mm("mm_down", act, wt["down"][l], add=x2)
        sv.update(h1=h1, za=za, zqkv=zqkv, zcg=zcg, zf=zf, bf=bf, c=c, crow=crow, o=o, pa=pa, u2=u2, u3=u3, ya=ya, yb=yb,
                  yc=yc, bg=bg, gate_rows=gate_rows, mg=mg, x1=x1, h2=h2, qx=qx, kv=kv, ox=ox, x2=x2, h3=h3, gu=gu, act=act)
        saved.append(sv)
        x = x3

    loss_row, dx, d_final = _loss_call(x, tgt, sp["final_norm"][None])

    gs = {k: [None] * depth for k in ("mix_norm", "b_gate", "b_forget", "conv_a", "conv_c", "conv_c_bias", "ln_c_gain",
                                      "ln_c_bias", "xattn_norm", "ffn_norm")}
    gw = {k: [None] * depth for k in ("main", "f", "out_a", "out_b", "out_c", "o", "xq", "xkv", "xo", "gu", "down")}
    stacked = {}

    def weight_grad(key, l, name, a_op, b_op):
        if dests is None or key not in dests:
            gw[key][l] = _mm(name, a_op, b_op, ta=True, out_dtype=BF16)
        else:
            shape, shards = dests[key]
            stacked[key] = _mm(name, a_op, b_op, ta=True, out_dtype=BF16, dest=(shape, stacked.get(key), (l,), shards))

    dmemn = None
    for l in reversed(range(depth)):
        sv = saved[l]
        dact = _mm("mm_down_dx", dx, wt["down"][l], tb=True)
        weight_grad("down", l, "mm_down_dw", sv["act"], dx)
        (dg, du), _ = _rows_vjp("swiglu_bwd", _swiglu_fn, [_part(sv["gu"], 0, f), _part(sv["gu"], f, f)], [], [_part(dact)],
                                [BF16, BF16], [])
        dgu = [_part(dg), _part(du)]
        dh3 = _mm("mm_gu_dx", dgu, wt["gu"][l], tb=True)
        weight_grad("gu", l, "mm_gu_dw", sv["h3"], dgu)
        (dx2,), (gs["ffn_norm"][l],) = _rows_vjp("rms_ffn_bwd", _rms_fn, [_part(sv["x2"])], [row(sp["ffn_norm"], l)],
                                                 [_part(dh3)], [F32], [True], adds={0: _part(dx)})
        dox = _mm("mm_xo_dx", dx2, wt["xo"][l], tb=True)
        weight_grad("xo", l, "mm_xo_dw", sv["ox"], dx2)
        dqx, dkx, dvx = _xattn_bwd(sv["qx"], sv["kv"], dox)
        dh2 = _mm("mm_xq_dx", dqx, wt["xq"][l], tb=True)
        weight_grad("xq", l, "mm_xq_dw", sv["h2"], dqx)
        dkv = [_part(dkx), _part(dvx)]
        dmemn = _mm("mm_xkv_dx", dkv, wt["xkv"][l], tb=True, add=dmemn)
        weight_grad("xkv", l, "mm_xkv_dw", memn, dkv)
        (dx1,), (gs["xattn_norm"][l],) = _rows_vjp("rms_x_bwd", _rms_fn, [_part(sv["x1"])], [row(sp["xattn_norm"], l)],
                                                   [_part(dh2)], [F32], [True], adds={0: _part(dx2)})
        dmg = _mm("mm_o_dx", dx1, wt["o"][l], tb=True)
        weight_grad("o", l, "mm_o_dw", sv["mg"], dx1)
        (dga, dgb, dgc, dya, dyb, dyc), dbg = _rows_vjp("merge_bwd", _merge_fn, sv["gate_rows"], sv["bg"], [_part(dmg)],
                                                        [BF16] * 6, [True] * 3)
        gs["b_gate"][l] = jnp.concatenate(dbg, axis=1)
        dpa = _mm("mm_out_a_dx", dya, wt["out_a"][l], tb=True)
        weight_grad("out_a", l, "mm_out_a_dw", sv["pa"], dya)
        do = _mm("mm_out_b_dx", dyb, wt["out_b"][l], tb=True)
        weight_grad("out_b", l, "mm_out_b_dw", sv["o"], dyb)
        du3 = _mm("mm_out_c_dx", dyc, wt["out_c"][l], tb=True)
        weight_grad("out_c", l, "mm_out_c_dw", sv["u3"], dyc)
        (du2,), (gs["ln_c_gain"][l], gs["ln_c_bias"][l]) = _rows_vjp(
            "ln_silu_bwd", _lnsilu_fn, [_part(sv["u2"])], [row(sp["ln_c_gain"], l), row(sp["ln_c_bias"], l)], [_part(du3)],
            [F32], [True, True])
        dcv, dcg, gs["conv_c"][l], gs["conv_c_bias"][l] = _conv_c_bwd(sv["zcg"], sp["conv_c"][l], du2)
        dab, dac, dau, gs["conv_a"][l] = _conv_a_bwd(sv["za"], sp["conv_a"][l], dpa)
        dq, dk, dv, dcc, dcr = _fox_bwd(sv["zqkv"], sv["c"], sv["crow"], do)
        dcr_t = _pad_lanes(dcr.reshape(heads, t).T)
        dzf, dbf = _fox_cumsum_bwd(sv["zf"], sv["bf"], dcc, dcr_t)
        gs["b_forget"][l] = dbf[:, :heads]
        dz = [_part(a) for a in (dab, dac, dau, dq, dk, dv, dcv, dcg, dga, dgb, dgc)]
        dh1f = _mm("mm_in_f_dx", dzf, wt["f"][l], tb=True)
        dh1 = _mm("mm_in_dx", dz, wt["main"][l], tb=True, add=dh1f)
        gw["main"][l] = _mm("mm_in_dw", sv["h1"], dz, ta=True, out_dtype=BF16)
        gw["f"][l] = _mm("mm_in_f_dw", sv["h1"], dzf, ta=True, out_dtype=BF16)
        (dx,), (gs["mix_norm"][l],) = _rows_vjp("rms_mix_bwd", _rms_fn, [_part(sv["x0"])], [row(sp["mix_norm"], l)],
                                                [_part(dh1)], [F32], [True], adds={0: _part(dx1)})

    _, (d_mem_norm,) = _rows_vjp("rms_mem_bwd", _rms_fn, [_part(mem)], [sp["mem_norm"][None]], [_part(dmemn)], [None], [True])
    small = {k: jnp.stack([a.reshape(sp[k].shape[1:]) for a in v]) for k, v in gs.items()}
    small["mem_norm"] = d_mem_norm[0]
    small["final_norm"] = d_final[0]
    gw.update(stacked)
    return loss_row, dx, small, gw


_BIG = ("w_in", "w_out_a", "w_out_b", "w_out_c", "w_o", "w_xq", "w_xkv", "w_xo", "w_gate_up", "w_down")
_COL_SHARDED = ("w_in", "w_xo", "w_gate_up")
_KEY = {"w_out_a": "out_a", "w_out_b": "out_b", "w_out_c": "out_c", "w_o": "o", "w_xq": "xq", "w_xkv": "xkv", "w_xo": "xo",
        "w_gate_up": "gu", "w_down": "down"}
_SMALL = ("mix_norm", "b_gate", "b_forget", "conv_c_bias", "ln_c_gain", "ln_c_bias", "xattn_norm", "ffn_norm", "mem_norm",
          "final_norm")
_SMALL_SHARDED = ("conv_a", "conv_c")


def _pack_rows(parts):
    padded = []
    for a in parts:
        pad = -a.shape[0] % SUBLANES
        padded.append(jnp.pad(a, ((0, pad), (0, 0))) if pad else a)
    return jnp.concatenate(padded, axis=0)


def _unpack_rows(packed, shapes):
    out, pos = [], 0
    for r in shapes:
        out.append(packed[pos:pos + r])
        pos += r + (-r % SUBLANES)
    return out


def kernel(x, mem, mix_norm, w_in, b_gate, b_forget, conv_a, w_out_a, w_out_b, conv_c, conv_c_bias, ln_c_gain, ln_c_bias, w_out_c, w_o, xattn_norm, mem_norm, w_xq, w_xkv, w_xo, ffn_norm, w_gate_up, w_down, final_norm, loss_target, m_mix_norm, m_w_in, m_b_gate, m_b_forget, m_conv_a, m_w_out_a, m_w_out_b, m_conv_c, m_conv_c_bias, m_ln_c_gain, m_ln_c_bias, m_w_out_c, m_w_o, m_xattn_norm, m_mem_norm, m_w_xq, m_w_xkv, m_w_xo, m_ffn_norm, m_w_gate_up, m_w_down, m_final_norm, v_mix_norm, v_w_in, v_b_gate, v_b_forget, v_conv_a, v_w_out_a, v_w_out_b, v_conv_c, v_conv_c_bias, v_ln_c_gain, v_ln_c_bias, v_w_out_c, v_w_o, v_xattn_norm, v_mem_norm, v_w_xq, v_w_xkv, v_w_xo, v_ffn_norm, v_w_gate_up, v_w_down, v_final_norm):
    args = dict(locals())
    names = ["mix_norm", "w_in", "b_gate", "b_forget", "conv_a", "w_out_a", "w_out_b", "conv_c", "conv_c_bias", "ln_c_gain",
             "ln_c_bias", "w_out_c", "w_o", "xattn_norm", "mem_norm", "w_xq", "w_xkv", "w_xo", "ffn_norm", "w_gate_up", "w_down",
             "final_norm"]
    depth, d = mix_norm.shape
    heads = b_forget.shape[1]
    cx, cy, cc = lax.axis_index("x"), lax.axis_index("y"), lax.axis_index("c")
    chip = 2 * cx + cy
    c_idx = jnp.reshape(cc, (1,)).astype(jnp.int32)
    chip_idx = jnp.reshape(chip, (1,)).astype(jnp.int32)

    own = [args[n].astype(BF16) for n in _BIG]
    gathered = [lax.dynamic_update_slice(g, w[:, None], (0, chip, 0, 0)) for g, w in zip(_gather_weights(own), own)]
    gathered = dict(zip(_BIG, gathered))
    wt = {k: [] for k in ("main", "f", "out_a", "out_b", "out_c", "o", "xq", "xkv", "xo", "gu", "down")}
    dests = {}
    for n in _BIG[1:]:
        g = gathered[n]
        merged = g.reshape(depth, N_CHIPS * g.shape[2], g.shape[3])
        for l in range(depth):
            if n in _COL_SHARDED:
                wt[_KEY[n]].append([_part(g, lead=(l, j)) for j in range(N_CHIPS)])
            else:
                wt[_KEY[n]].append([_part(merged, lead=(l,))])
        dests[_KEY[n]] = (g.shape, N_CHIPS) if n in _COL_SHARDED else (merged.shape, 1)
    for l in range(depth):
        main, f_ = _w_in_aligned(gathered["w_in"], l, d, heads)
        wt["main"].append(main)
        wt["f"].append(f_)
    conv_rows = [conv_a.shape[1] * depth, conv_c.shape[1] * depth]
    conv_pack = _pack_rows([conv_a.reshape(conv_rows[0], -1), conv_c.reshape(conv_rows[1], -1)])
    conv_all = _allgather8(conv_pack).reshape(N_CHIPS, 2, conv_pack.shape[0], conv_pack.shape[1])[:, 0]
    conv_all = conv_all.transpose(1, 0, 2).reshape(conv_pack.shape[0], d)
    conv_a_full, conv_c_full = _unpack_rows(conv_all, conv_rows)
    sp = {n: args[n] for n in _SMALL}
    sp["conv_a"] = conv_a_full.reshape(depth, conv_a.shape[1], d)
    sp["conv_c"] = conv_c_full.reshape(depth, conv_c.shape[1], d)

    loss_row, grad_x, gsmall, gw = _local_step(x[0], mem[0], loss_target[0], sp, wt, dests)

    gfull = {n: gw[_KEY[n]].reshape(gathered[n].shape) for n in _BIG[1:]}
    g_in = None
    for l in range(depth):
        g_in = _w_in_shards(gw["main"][l], gw["f"][l], g_in, l, gathered["w_in"].shape, heads)
    gfull["w_in"] = g_in
    local = [gfull[n] for n in _BIG]
    from_sibling = _swap_halves(local)
    chip_sums = [_add_sibling(g, r, c_idx) for g, r in zip(local, from_sibling)]
    from_chips = _exchange_chips(chip_sums)
    halves = [_add_chips(s, r, chip_idx, c_idx) for s, r in zip(chip_sums, from_chips)]
    grads = dict(zip(_BIG, _join_halves(halves)))

    def as_rows(n, a):
        return _pad_lanes(a, d) if n == "b_forget" else a.reshape(-1, d)

    pieces = [as_rows(n, gsmall[n]) for n in _SMALL]
    for n in _SMALL_SHARDED:
        pieces.append(gsmall[n].reshape(-1, d))
    pieces.append(_pad_lanes(loss_row, d))
    pack = _pack_rows(pieces)
    total = _sum_devices(_allgather8(pack), pack.shape[0])
    shapes = [p.shape[0] for p in pieces]
    summed = _unpack_rows(total, shapes)
    for n, g in zip(_SMALL, summed):
        grads[n] = g[:, :heads] if n == "b_forget" else g.reshape(args[n].shape)
    for n, g in zip(_SMALL_SHARDED, summed[len(_SMALL):]):
        g = g.reshape(args[n].shape[:2] + (d,))
        grads[n] = lax.dynamic_slice_in_dim(g, chip * args[n].shape[2], args[n].shape[2], axis=2)
    loss = summed[-1][0, 0]

    delta, new_m, new_v = {}, {}, {}
    packed = [_pack_rows([as_rows(n, src[pre + n]) for n in _SMALL])
              for src, pre in ((args, ""), (grads, ""), (args, "m_"), (args, "v_"))]
    srows = [as_rows(n, args[n]).shape[0] for n in _SMALL]
    updated = [_unpack_rows(a, srows) for a in _adamw(*packed)]
    for k, n in enumerate(_SMALL):
        delta[n], new_m[n], new_v[n] = (
            (u[k][:, :heads] if n == "b_forget" else u[k]).reshape(args[n].shape) for u in updated)
    for n in _BIG + _SMALL_SHARDED:
        delta[n], new_m[n], new_v[n] = _adamw(args[n], grads[n], args["m_" + n], args["v_" + n])

    return (loss, grad_x[None], *[grads[n] for n in names], *[delta[n] for n in names], *[new_m[n] for n in names],
            *[new_v[n] for n in names])
```

```python
import functools
import math

import jax
import jax.numpy as jnp
from jax import lax
from jax.experimental import pallas as pl
from jax.experimental.pallas import tpu as pltpu

F32, BF16 = jnp.float32, jnp.bfloat16
EPS = 1e-6
LANES = 128
SUBLANES = 8
VMEM_LIMIT = 56 * 1024 * 1024
MM_VMEM_BUDGET = 36 * 1024 * 1024
MESH = pl.DeviceIdType.MESH
N_CHIPS = 4
N_DEV = 8

ADAM_LR, ADAM_B1, ADAM_B2, ADAM_EPS, ADAM_WD, ADAM_STEP = 0.001, 0.9, 0.999, 1e-08, 0.01, 10


def _params(**kw):
    return pltpu.CompilerParams(vmem_limit_bytes=VMEM_LIMIT, **kw)


def _sds(shape, dtype):
    return jax.ShapeDtypeStruct(tuple(shape), dtype)


def _part(arr, off=0, width=None, lead=()):
    assert arr.ndim == len(lead) + 2
    return (arr, off, arr.shape[-1] - off if width is None else width, tuple(lead))


def _unit(parts):
    u = 0
    for _, off, w, _ in parts:
        u = math.gcd(u, math.gcd(off, w))
    return u


def _pick(dim, unit, cands, lane):
    for c in cands:
        if c <= dim and dim % c == 0 and unit % c == 0 and (not lane or c % LANES == 0):
            return c
    return min(dim, unit)


_COL_TILES = (1024, 1408, 512, 256, 128)
_ROW_TILES = (512, 256, 1408, 128)


def _mm(name, a, b, *, ta=False, tb=False, out_dtype=F32, add=None, dest=None):
    if not isinstance(a, list):
        a = [_part(a)]
    if not isinstance(b, list):
        b = [_part(b)]
    a_rows, a_cols = a[0][0].shape[-2], sum(p[2] for p in a)
    b_rows, b_cols = b[0][0].shape[-2], sum(p[2] for p in b)
    M, Ka = (a_cols, a_rows) if ta else (a_rows, a_cols)
    Kb, N = (b_cols, b_rows) if tb else (b_rows, b_cols)
    assert Ka == Kb, (name, Ka, Kb)
    K = Ka
    ua, ub = _unit(a), _unit(b)
    shards = 1 if dest is None else dest[3]
    tm = _pick(M, ua if ta else M, _ROW_TILES, ta)
    tn = _pick(N, math.gcd(N if tb else ub, N // shards), _COL_TILES, not tb)
    ku = math.gcd(K if ta else ua, ub if tb else K)
    tk = _pick(K, ku, _COL_TILES, (not ta) or tb)
    a_size = sum(p[0].dtype.itemsize for p in a)
    b_size = sum(p[0].dtype.itemsize for p in b)

    def vmem_estimate():
        return (2 * tm * tk * a_size + 2 * tk * tn * b_size + tm * tn * (2 * jnp.dtype(out_dtype).itemsize + 4)
                + (8 * tm * tn if add is not None else 0))

    while vmem_estimate() > MM_VMEM_BUDGET:
        if tk % 256 == 0 and tk >= tn:
            tk //= 2
        elif tn % 256 == 0:
            tn //= 2
        elif tm % 256 == 0:
            tm //= 2
        elif tk % 256 == 0:
            tk //= 2
        else:
            break
    nm, nn, nk = M // tm, N // tn, K // tk
    a_tile, b_tile = (tm if ta else tk), (tk if tb else tn)
    a_nb, b_nb = a[0][2] // a_tile, b[0][2] // b_tile

    def index(parts, nb, tile, p, row, col):
        base = parts[p][1] // tile
        if len(parts) == 1:
            return parts[p][3] + (row, base + col)
        inside = (col // nb) == p
        return parts[p][3] + (jnp.where(inside, row, 0), base + jnp.clip(col - p * nb, 0, nb - 1))

    def spec(parts, p, shape, fn):
        return pl.BlockSpec((None,) * len(parts[p][3]) + shape, fn)

    in_specs, operands = [], []
    for p in range(len(a)):
        if ta:
            in_specs.append(spec(a, p, (tk, tm), lambda mi, nj, kk, p=p: index(a, a_nb, a_tile, p, kk, mi)))
        else:
            in_specs.append(spec(a, p, (tm, tk), lambda mi, nj, kk, p=p: index(a, a_nb, a_tile, p, mi, kk)))
        operands.append(a[p][0])
    for p in range(len(b)):
        if tb:
            in_specs.append(spec(b, p, (tn, tk), lambda mi, nj, kk, p=p: index(b, b_nb, b_tile, p, nj, kk)))
        else:
            in_specs.append(spec(b, p, (tk, tn), lambda mi, nj, kk, p=p: index(b, b_nb, b_tile, p, kk, nj)))
        operands.append(b[p][0])
    if add is not None:
        in_specs.append(pl.BlockSpec((tm, tn), lambda mi, nj, kk: (mi, nj)))
        operands.append(add)
    aliases = {}
    if dest is None:
        out_spec = pl.BlockSpec((tm, tn), lambda mi, nj, kk: (mi, nj))
        out_shape = _sds((M, N), out_dtype)
    else:
        d_shape, d_prev, d_lead, _ = dest
        nbs = N // shards // tn
        if shards == 1:
            out_spec = pl.BlockSpec((None,) * len(d_lead) + (tm, tn), lambda mi, nj, kk: tuple(d_lead) + (mi, nj))
        else:
            out_spec = pl.BlockSpec((None,) * (len(d_lead) + 1) + (tm, tn),
                                    lambda mi, nj, kk: tuple(d_lead) + (nj // nbs, mi, nj % nbs))
        out_shape = _sds(d_shape, out_dtype)
        if d_prev is not None:
            aliases = {len(operands): 0}
            in_specs.append(pl.BlockSpec(memory_space=pl.ANY))
            operands.append(d_prev)
    n_extra = 1 if aliases else 0
    dims = (((0 if ta else 1,), (1 if tb else 0,)), ((), ()))
    na, nb_ = len(a), len(b)

    def body(*refs):
        a_refs, b_refs = refs[:na], refs[na:na + nb_]
        pos = na + nb_
        add_ref = None
        if add is not None:
            add_ref = refs[pos]
            pos += 1
        pos += n_extra
        o_ref = refs[pos]
        acc_ref = refs[pos + 1] if nk > 1 else None
        mi, nj, kk = pl.program_id(0), pl.program_id(1), pl.program_id(2)
        a_col = mi if ta else kk
        b_col = kk if tb else nj

        def finish(r):
            if add_ref is not None:
                r = r + add_ref[...].astype(F32)
            o_ref[...] = r.astype(o_ref.dtype)

        def step(a_ref, b_ref):
            r = lax.dot_general(a_ref[...].astype(BF16), b_ref[...].astype(BF16), dims, preferred_element_type=F32)
            if nk == 1:
                finish(r)
            else:
                @pl.when(kk == 0)
                def _():
                    acc_ref[...] = r

                @pl.when(kk > 0)
                def _():
                    acc_ref[...] += r

        for pa in range(na):
            for pb in range(nb_):
                if na * nb_ == 1:
                    step(a_refs[pa], b_refs[pb])
                else:
                    cond = jnp.logical_and(a_col // a_nb == pa, b_col // b_nb == pb)
                    pl.when(cond)(functools.partial(step, a_refs[pa], b_refs[pb]))
        if nk > 1:
            @pl.when(kk == nk - 1)
            def _():
                finish(acc_ref[...])

    return pl.pallas_call(
        body, grid=(nm, nn, nk), in_specs=in_specs, out_specs=out_spec, out_shape=out_shape, input_output_aliases=aliases,
        scratch_shapes=[pltpu.VMEM((tm, tn), F32)] if nk > 1 else [],
        compiler_params=_params(dimension_semantics=("parallel", "parallel", "arbitrary")),
        name=name)(*operands)


def _row_spec(tb, part):
    _, off, w, _ = part
    return pl.BlockSpec((tb, w), lambda i, o=off // w: (i, o))


def _full_spec(arr):
    return pl.BlockSpec(arr.shape, lambda i: (0,) * arr.ndim)


def _rows(name, fn, rows, pars, outs, tb=256):
    n = rows[0][0].shape[0]
    tb = min(tb, n)
    nr, npar = len(rows), len(pars)

    def body(*refs):
        r = [ref[...].astype(F32) for ref in refs[:nr]]
        p = [ref[...].astype(F32) for ref in refs[nr:nr + npar]]
        res = fn(*r, *p)
        for o_ref, v in zip(refs[nr + npar:], res):
            o_ref[...] = v.astype(o_ref.dtype)

    out = pl.pallas_call(
        body, grid=(n // tb,),
        in_specs=[_row_spec(tb, p) for p in rows] + [_full_spec(p) for p in pars],
        out_specs=[pl.BlockSpec((tb, w), lambda i: (i, 0)) for w, _ in outs],
        out_shape=[_sds((n, w), dt) for w, dt in outs],
        compiler_params=_params(), name=name)(*[p[0] for p in rows], *pars)
    return out


def _rows_vjp(name, fn, rows, pars, cots, row_grads, par_grads, adds=None, tb=256):
    adds = adds or {}
    n = rows[0][0].shape[0]
    tb = min(tb, n)
    nr, npar, nc = len(rows), len(pars), len(cots)
    add_keys = sorted(adds)
    rg = [j for j, dt in enumerate(row_grads) if dt is not None]
    pg = [j for j, f in enumerate(par_grads) if f]

    def body(*refs):
        pos = 0
        r = [ref[...].astype(F32) for ref in refs[pos:pos + nr]]
        pos += nr
        p = [ref[...].astype(F32) for ref in refs[pos:pos + npar]]
        pos += npar
        ct = tuple(ref[...].astype(F32) for ref in refs[pos:pos + nc])
        pos += nc
        add_v = {k: ref[...].astype(F32) for k, ref in zip(add_keys, refs[pos:pos + len(add_keys)])}
        pos += len(add_keys)
        _, vjp = jax.vjp(lambda *args: tuple(fn(*args)), *r, *p)
        g = vjp(ct)
        for j in rg:
            v = g[j]
            if j in add_v:
                v = v + add_v[j]
            refs[pos][...] = v.astype(refs[pos].dtype)
            pos += 1
        first = pl.program_id(0) == 0
        for j in pg:
            ref = refs[pos]
            pos += 1

            @pl.when(first)
            def _(ref=ref):
                ref[...] = jnp.zeros_like(ref)

            ref[...] += g[nr + j]

    in_specs = ([_row_spec(tb, p) for p in rows] + [_full_spec(p) for p in pars] + [_row_spec(tb, p) for p in cots]
                + [_row_spec(tb, adds[k]) for k in add_keys])
    out_specs = ([pl.BlockSpec((tb, rows[j][2]), lambda i: (i, 0)) for j in rg] + [_full_spec(pars[j]) for j in pg])
    out_shape = ([_sds((n, rows[j][2]), row_grads[j]) for j in rg] + [_sds(pars[j].shape, F32) for j in pg])
    out = pl.pallas_call(
        body, grid=(n // tb,), in_specs=in_specs, out_specs=out_specs, out_shape=out_shape,
        compiler_params=_params(), name=name)(
            *[p[0] for p in rows], *pars, *[p[0] for p in cots], *[adds[k][0] for k in add_keys])
    return out[:len(rg)], out[len(rg):]


def _rms_fn(x, g):
    return (x * lax.rsqrt(jnp.mean(x * x, axis=-1, keepdims=True) + EPS) * g,)


def _lnsilu_fn(u, g, b):
    mu = jnp.mean(u, axis=-1, keepdims=True)
    xc = u - mu
    y = xc * lax.rsqrt(jnp.mean(xc * xc, axis=-1, keepdims=True) + EPS) * g + b
    return (y * jax.nn.sigmoid(y),)


def _merge_fn(ga, gb, gc, ya, yb, yc, ba, bb, bc):
    return (jax.nn.sigmoid(ga + ba) * ya + jax.nn.sigmoid(gb + bb) * yb + jax.nn.sigmoid(gc + bc) * yc,)


def _swiglu_fn(g, u):
    return (g * jax.nn.sigmoid(g) * u,)


def _loss_call(x, tgt, gain):
    n, d = x.shape
    tb = min(256, n)

    def body(x_ref, t_ref, g_ref, l_ref, dx_ref, dg_ref):
        (y,), vjp = jax.vjp(lambda a, b: tuple(_rms_fn(a, b)), x_ref[...], g_ref[...])
        e = y - t_ref[...]
        part = 0.5 * jnp.sum(jnp.mean(e * e, axis=-1, keepdims=True))
        dx, dg = vjp((e * (1.0 / d),))
        dx_ref[...] = dx

        @pl.when(pl.program_id(0) == 0)
        def _():
            l_ref[...] = jnp.zeros_like(l_ref)
            dg_ref[...] = jnp.zeros_like(dg_ref)

        l_ref[...] += jnp.full(l_ref.shape, part, F32)
        dg_ref[...] += dg

    return pl.pallas_call(
        body, grid=(n // tb,),
        in_specs=[pl.BlockSpec((tb, d), lambda i: (i, 0)), pl.BlockSpec((tb, d), lambda i: (i, 0)), _full_spec(gain)],
        out_specs=[pl.BlockSpec((1, LANES), lambda i: (0, 0)), pl.BlockSpec((tb, d), lambda i: (i, 0)), _full_spec(gain)],
        out_shape=[_sds((1, LANES), F32), _sds((n, d), F32), _sds(gain.shape, F32)],
        compiler_params=_params(), name="loss_head")(x, tgt, gain)


FOX_BLOCK = 256
CONV_ROWS = 64
CONV_HALO = 32


def _chunk(i):
    return pl.ds(pl.multiple_of(i * CONV_ROWS, CONV_ROWS), CONV_ROWS)


def _delayed(ext, s):
    return (ext if s == 0 else pltpu.roll(ext, s, 0))[CONV_HALO:]


def _advanced(ext, s):
    return (ext if s == 0 else pltpu.roll(ext, ext.shape[0] - s, 0))[:CONV_ROWS]


def _conv_pass(t, taps, w_ref, lanes, get_u, emit, get_dy=None, dw_acc=None):
    def body(i, carry):
        rows = _chunk(i)
        u = get_u(rows)
        ext = jnp.concatenate([carry, u], axis=0)
        dy = None if get_dy is None else get_dy(rows)
        acc = None
        for k in range(taps):
            sh = _delayed(ext, taps - 1 - k)
            term = w_ref[k:k + 1, lanes] * sh
            acc = term if acc is None else acc + term
            if dy is not None:
                dw_acc[k] += jnp.sum((dy * sh).reshape(CONV_ROWS // SUBLANES, SUBLANES, LANES), axis=0)
        emit(rows, u, acc)
        return u[CONV_ROWS - CONV_HALO:]

    lax.fori_loop(0, t // CONV_ROWS, body, jnp.zeros((CONV_HALO, LANES), F32))


def _conv_pass_t(t, taps, w_ref, lanes, get_dy, emit):
    n = t // CONV_ROWS

    def body(j, carry):
        rows = _chunk(n - 1 - j)
        dy = get_dy(rows)
        ext = jnp.concatenate([dy, carry], axis=0)
        acc = None
        for k in range(taps):
            term = w_ref[k:k + 1, lanes] * _advanced(ext, taps - 1 - k)
            acc = term if acc is None else acc + term
        emit(rows, acc)
        return dy[:CONV_HALO]

    lax.fori_loop(0, n, body, jnp.zeros((CONV_HALO, LANES), F32))


def _chan_block(d):
    return min(256, d)


def _lane_groups(cb):
    return [slice(g * LANES, (g + 1) * LANES) for g in range(cb // LANES)]


def _conv_a_fwd(za, w):
    t, d = za.shape[0], za.shape[1] // 3
    taps, cb = w.shape[0], _chan_block(d)
    nb = d // cb
    assert taps - 1 <= CONV_HALO and t % CONV_ROWS == 0

    def body(ab_ref, ac_ref, au_ref, w_ref, o_ref):
        for lanes in _lane_groups(cb):
            def emit(rows, u, s, lanes=lanes):
                o_ref[rows, lanes] = (ab_ref[rows, lanes] * s).astype(o_ref.dtype)

            _conv_pass(t, taps, w_ref, lanes, lambda rows, lanes=lanes: ac_ref[rows, lanes] * au_ref[rows, lanes], emit)

    return pl.pallas_call(
        body, grid=(nb,),
        in_specs=[pl.BlockSpec((t, cb), lambda j, o=o: (0, o * nb + j)) for o in range(3)]
        + [pl.BlockSpec((taps, cb), lambda j: (0, j))],
        out_specs=pl.BlockSpec((t, cb), lambda j: (0, j)),
        out_shape=_sds((t, d), BF16), compiler_params=_params(), name="conv_a_fwd")(za, za, za, w)


def _finish_taps(dw_ref, dw_acc, lanes, taps):
    for k in range(taps):
        dw_ref[k:k + 1, lanes] = jnp.sum(dw_acc[k], axis=0, keepdims=True)


def _conv_a_bwd(za, w, dpa):
    t, d = za.shape[0], za.shape[1] // 3
    taps, cb = w.shape[0], _chan_block(d)
    nb = d // cb

    def body(ab_ref, ac_ref, au_ref, w_ref, dy_ref, dab_ref, dac_ref, dau_ref, dw_ref, dw_acc):
        for lanes in _lane_groups(cb):
            dw_acc[...] = jnp.zeros_like(dw_acc)

            def ds(rows, lanes=lanes):
                return dy_ref[rows, lanes] * ab_ref[rows, lanes]

            def emit_s(rows, u, s, lanes=lanes):
                dab_ref[rows, lanes] = (dy_ref[rows, lanes] * s).astype(dab_ref.dtype)

            def emit_du(rows, du, lanes=lanes):
                dac_ref[rows, lanes] = (du * au_ref[rows, lanes]).astype(dac_ref.dtype)
                dau_ref[rows, lanes] = (du * ac_ref[rows, lanes]).astype(dau_ref.dtype)

            _conv_pass(t, taps, w_ref, lanes, lambda rows, lanes=lanes: ac_ref[rows, lanes] * au_ref[rows, lanes], emit_s,
                       get_dy=ds, dw_acc=dw_acc)
            _conv_pass_t(t, taps, w_ref, lanes, ds, emit_du)
            _finish_taps(dw_ref, dw_acc, lanes, taps)

    blk = pl.BlockSpec((t, cb), lambda j: (0, j))
    return pl.pallas_call(
        body, grid=(nb,),
        in_specs=[pl.BlockSpec((t, cb), lambda j, o=o: (0, o * nb + j)) for o in range(3)]
        + [pl.BlockSpec((taps, cb), lambda j: (0, j)), blk],
        out_specs=[blk, blk, blk, pl.BlockSpec((taps, cb), lambda j: (0, j))],
        out_shape=[_sds((t, d), BF16)] * 3 + [_sds((taps, d), F32)],
        scratch_shapes=[pltpu.VMEM((taps, SUBLANES, LANES), F32)],
        compiler_params=_params(), name="conv_a_bwd")(za, za, za, w, dpa)


def _conv_c_fwd(zcg, w, bias):
    t, d = zcg.shape[0], w.shape[1]
    taps, cb = w.shape[0], _chan_block(d)
    nb = d // cb
    assert taps - 1 <= CONV_HALO and t % CONV_ROWS == 0

    def body(cv_ref, cg_ref, w_ref, b_ref, o_ref):
        for lanes in _lane_groups(cb):
            def emit(rows, u, s, lanes=lanes):
                o_ref[rows, lanes] = s + b_ref[:, lanes]

            _conv_pass(t, taps, w_ref, lanes,
                       lambda rows, lanes=lanes: cv_ref[rows, lanes] * jax.nn.sigmoid(cg_ref[rows, lanes]), emit)

    return pl.pallas_call(
        body, grid=(nb,),
        in_specs=[pl.BlockSpec((t, cb), lambda j, o=o: (0, o * nb + j)) for o in range(2)]
        + [pl.BlockSpec((taps, cb), lambda j: (0, j)), pl.BlockSpec((1, cb), lambda j: (0, j))],
        out_specs=pl.BlockSpec((t, cb), lambda j: (0, j)),
        out_shape=_sds((t, d), F32), compiler_params=_params(), name="conv_c_fwd")(zcg, zcg, w, bias)


def _conv_c_bwd(zcg, w, du2):
    t, d = zcg.shape[0], w.shape[1]
    taps, cb = w.shape[0], _chan_block(d)
    nb = d // cb

    def body(cv_ref, cg_ref, w_ref, dy_ref, dcv_ref, dcg_ref, dw_ref, db_ref, dw_acc, db_acc):
        for lanes in _lane_groups(cb):
            dw_acc[...] = jnp.zeros_like(dw_acc)
            db_acc[...] = jnp.zeros_like(db_acc)

            def dy(rows, lanes=lanes):
                return dy_ref[rows, lanes]

            def emit_s(rows, u, s, lanes=lanes):
                db_acc[...] += jnp.sum(dy_ref[rows, lanes].reshape(CONV_ROWS // SUBLANES, SUBLANES, LANES), axis=0)

            def emit_du(rows, du, lanes=lanes):
                cv, sg = cv_ref[rows, lanes], jax.nn.sigmoid(cg_ref[rows, lanes])
                dcv_ref[rows, lanes] = (du * sg).astype(dcv_ref.dtype)
                dcg_ref[rows, lanes] = (du * cv * sg * (1.0 - sg)).astype(dcg_ref.dtype)

            _conv_pass(t, taps, w_ref, lanes,
                       lambda rows, lanes=lanes: cv_ref[rows, lanes] * jax.nn.sigmoid(cg_ref[rows, lanes]), emit_s,
                       get_dy=dy, dw_acc=dw_acc)
            _conv_pass_t(t, taps, w_ref, lanes, dy, emit_du)
            _finish_taps(dw_ref, dw_acc, lanes, taps)
            db_ref[:, lanes] = jnp.sum(db_acc[...], axis=0, keepdims=True)

    blk = pl.BlockSpec((t, cb), lambda j: (0, j))
    return pl.pallas_call(
        body, grid=(nb,),
        in_specs=[pl.BlockSpec((t, cb), lambda j, o=o: (0, o * nb + j)) for o in range(2)]
        + [pl.BlockSpec((taps, cb), lambda j: (0, j)), blk],
        out_specs=[blk, blk, pl.BlockSpec((taps, cb), lambda j: (0, j)), pl.BlockSpec((1, cb), lambda j: (0, j))],
        out_shape=[_sds((t, d), BF16)] * 2 + [_sds((taps, d), F32), _sds((1, d), F32)],
        scratch_shapes=[pltpu.VMEM((taps, SUBLANES, LANES), F32), pltpu.VMEM((SUBLANES, LANES), F32)],
        compiler_params=_params(), name="conv_c_bwd")(zcg, zcg, w, du2)


def _tri(tb, t, i, lower):
    r = lax.broadcasted_iota(jnp.int32, (tb, t), 0) + i * tb
    c = lax.broadcasted_iota(jnp.int32, (tb, t), 1)
    return ((r >= c) if lower else (c >= r)).astype(F32)


def _fox_cumsum(zf, bf):
    t = zf.shape[0]
    tb = min(256, t)

    def body(z_ref, b_ref, c_ref):
        logf = jax.nn.log_sigmoid(z_ref[...] + b_ref[...])
        c_ref[...] = jnp.dot(_tri(tb, t, pl.program_id(0), True), logf, precision=lax.Precision.HIGHEST,
                             preferred_element_type=F32)

    return pl.pallas_call(
        body, grid=(t // tb,), in_specs=[_full_spec(zf), _full_spec(bf)],
        out_specs=pl.BlockSpec((tb, LANES), lambda i: (i, 0)), out_shape=_sds((t, LANES), F32),
        compiler_params=_params(), name="fox_cumsum")(zf, bf)


def _fox_cumsum_bwd(zf, bf, dcc, dcr_t):
    t = zf.shape[0]
    tb = min(256, t)

    def body(z_ref, b_ref, dcc_ref, dcr_ref, dz_ref, db_ref, dc_ref):
        i = pl.program_id(0)

        @pl.when(i == 0)
        def _():
            dc_ref[...] = dcr_ref[...] + jnp.sum(dcc_ref[...], axis=0)
            db_ref[...] = jnp.zeros_like(db_ref)

        dlogf = jnp.dot(_tri(tb, t, i, False), dc_ref[...], precision=lax.Precision.HIGHEST, preferred_element_type=F32)
        dz = dlogf * jax.nn.sigmoid(-(z_ref[...] + b_ref[...]))
        dz_ref[...] = dz.astype(dz_ref.dtype)
        db_ref[...] += jnp.sum(dz, axis=0, keepdims=True)

    return pl.pallas_call(
        body, grid=(t // tb,),
        in_specs=[pl.BlockSpec((tb, LANES), lambda i: (i, 0)), _full_spec(bf), _full_spec(dcc), _full_spec(dcr_t)],
        out_specs=[pl.BlockSpec((tb, LANES), lambda i: (i, 0)), pl.BlockSpec((1, LANES), lambda i: (0, 0))],
        out_shape=[_sds((t, LANES), BF16), _sds((1, LANES), F32)],
        scratch_shapes=[pltpu.VMEM((t, LANES), F32)],
        compiler_params=_params(), name="fox_cumsum_bwd")(zf, bf, dcc, dcr_t)


def _fox_block(q, k, v, cc, cr, *, p, q0, hd):
    tq, kext = q.shape[0], k.shape[0]
    nh = LANES // hd
    lane = lax.broadcasted_iota(jnp.int32, (1, LANES), 1)
    causal = (lax.broadcasted_iota(jnp.int32, (tq, kext), 0) + q0) >= lax.broadcasted_iota(jnp.int32, (tq, kext), 1)
    kb, vb = k.astype(BF16), v.astype(BF16)
    out = jnp.zeros((tq, LANES), F32)
    for h in range(nh):
        m = (lane // hd) == h
        qh = jnp.where(m, q, 0.0).astype(BF16)
        s = lax.dot_general(qh, kb, (((1,), (1,)), ((), ())), preferred_element_type=F32) * (hd ** -0.5)
        c_q = jnp.sum(jnp.where(lane == p * nh + h, cc, 0.0), axis=1, keepdims=True)
        s = jnp.where(causal, s + (c_q - cr[h]), -1e30)
        e = jnp.exp(s - lax.stop_gradient(jnp.max(s, axis=1, keepdims=True)))
        prob = e / jnp.sum(e, axis=1, keepdims=True)
        o = jnp.dot(prob.astype(BF16), vb, preferred_element_type=F32)
        out = out + jnp.where(m, o, 0.0)
    return out


def _fox_dims(zqkv, crow):
    t, d = zqkv.shape[0], zqkv.shape[1] // 3
    hd = d // crow.shape[0]
    return t, d, hd, d // LANES, LANES // hd, min(FOX_BLOCK, t)


def _fox_specs(i, tq, nblk, nh):
    kext = (i + 1) * tq
    return [pl.BlockSpec((tq, LANES), lambda p: (i, p)), pl.BlockSpec((kext, LANES), lambda p: (0, nblk + p)),
            pl.BlockSpec((kext, LANES), lambda p: (0, 2 * nblk + p)), pl.BlockSpec((tq, LANES), lambda p: (i, 0)),
            pl.BlockSpec((nh, 1, kext), lambda p: (p, 0, 0))]


def _fox_fwd(zqkv, c, crow):
    t, d, hd, nblk, nh, tq = _fox_dims(zqkv, crow)
    o = None
    for i in range(t // tq):
        def body(q_ref, k_ref, v_ref, cc_ref, cr_ref, *rest):
            o_ref = rest[-1]
            o_ref[...] = _fox_block(q_ref[...].astype(F32), k_ref[...], v_ref[...], cc_ref[...], cr_ref[...],
                                    p=pl.program_id(0), q0=i * tq, hd=hd).astype(o_ref.dtype)

        prev = [] if o is None else [o]
        o = pl.pallas_call(
            body, grid=(nblk,),
            in_specs=_fox_specs(i, tq, nblk, nh) + [pl.BlockSpec(memory_space=pl.ANY)] * len(prev),
            out_specs=pl.BlockSpec((tq, LANES), lambda p: (i, p)), out_shape=_sds((t, d), BF16),
            input_output_aliases={5: 0} if prev else {},
            compiler_params=_params(), name=f"fox_fwd_q{i}")(zqkv, zqkv, zqkv, c, crow, *prev)
    return o


def _fox_bwd(zqkv, c, crow, do):
    t, d, hd, nblk, nh, tq = _fox_dims(zqkv, crow)
    nq = t // tq
    acc = None
    for i in reversed(range(nq)):
        kext = (i + 1) * tq
        first = acc is None

        def body(q_ref, k_ref, v_ref, cc_ref, cr_ref, do_ref, *rest, first=first, i=i):
            if first:
                dq_ref, dk_ref, dv_ref, dcc_ref, dcr_ref = rest
            else:
                dk_in, dv_in, dcr_in = rest[2:5]
                dq_ref, dk_ref, dv_ref, dcc_ref, dcr_ref = rest[5:]
            f = functools.partial(_fox_block, p=pl.program_id(0), q0=i * tq, hd=hd)
            _, vjp = jax.vjp(f, q_ref[...].astype(F32), k_ref[...].astype(F32), v_ref[...].astype(F32), cc_ref[...], cr_ref[...])
            dq, dk, dv, dcc, dcr = vjp(do_ref[...].astype(F32))
            dq_ref[...] = dq.astype(dq_ref.dtype)
            dcc_ref[...] = dcc
            if first:
                dk_ref[...], dv_ref[...], dcr_ref[...] = dk, dv, dcr
            else:
                dk_ref[...] = dk_in[...] + dk
                dv_ref[...] = dv_in[...] + dv
                dcr_ref[...] = dcr_in[...] + dcr

        qblk = pl.BlockSpec((tq, LANES), lambda p: (i, p))
        kblk = pl.BlockSpec((kext, LANES), lambda p: (0, p))
        cblk = pl.BlockSpec((None, tq, LANES), lambda p: (p, i, 0))
        rblk = pl.BlockSpec((nh, 1, kext), lambda p: (p, 0, 0))
        any_ = pl.BlockSpec(memory_space=pl.ANY)
        if first:
            prev, prev_specs, aliases = [], [], {}
        else:
            prev = [acc[0], acc[3], acc[1], acc[2], acc[4]]
            prev_specs = [any_, any_, kblk, kblk, rblk]
            aliases = {6: 0, 7: 3, 8: 1, 9: 2, 10: 4}
        acc = pl.pallas_call(
            body, grid=(nblk,),
            in_specs=_fox_specs(i, tq, nblk, nh) + [qblk] + prev_specs,
            out_specs=[qblk, kblk, kblk, cblk, rblk],
            out_shape=[_sds((t, d), BF16), _sds((t, d), F32), _sds((t, d), F32), _sds((nblk, t, LANES), F32),
                       _sds(crow.shape, F32)],
            input_output_aliases=aliases,
            compiler_params=_params(), name=f"fox_bwd_q{i}")(zqkv, zqkv, zqkv, c, crow, do, *prev)
    return acc


def _xattn_block(q, k, v):
    s = lax.dot_general(q.astype(BF16), k.astype(BF16), (((1,), (1,)), ((), ())), preferred_element_type=F32)
    s = s * (q.shape[1] ** -0.5)
    e = jnp.exp(s - lax.stop_gradient(jnp.max(s, axis=1, keepdims=True)))
    prob = e / jnp.sum(e, axis=1, keepdims=True)
    return jnp.dot(prob.astype(BF16), v.astype(BF16), preferred_element_type=F32)


def _xattn_fwd(qx, kv):
    t, dx = qx.shape
    nm, nh, tq = kv.shape[0], qx.shape[1] // LANES, min(512, qx.shape[0])

    def body(q_ref, k_ref, v_ref, o_ref):
        o_ref[...] = _xattn_block(q_ref[...], k_ref[...], v_ref[...]).astype(o_ref.dtype)

    return pl.pallas_call(
        body, grid=(nh, t // tq),
        in_specs=[pl.BlockSpec((tq, LANES), lambda h, i: (i, h)), pl.BlockSpec((nm, LANES), lambda h, i: (0, h)),
                  pl.BlockSpec((nm, LANES), lambda h, i: (0, nh + h))],
        out_specs=pl.BlockSpec((tq, LANES), lambda h, i: (i, h)), out_shape=_sds((t, dx), BF16),
        compiler_params=_params(), name="xattn_fwd")(qx, kv, kv)


def _xattn_bwd(qx, kv, do):
    t, dx = qx.shape
    nm, nh, tq = kv.shape[0], qx.shape[1] // LANES, min(512, qx.shape[0])

    def body(q_ref, k_ref, v_ref, do_ref, dq_ref, dk_ref, dv_ref):
        _, vjp = jax.vjp(_xattn_block, q_ref[...].astype(F32), k_ref[...].astype(F32), v_ref[...].astype(F32))
        dq, dk, dv = vjp(do_ref[...].astype(F32))
        dq_ref[...] = dq.astype(dq_ref.dtype)

        @pl.when(pl.program_id(1) == 0)
        def _():
            dk_ref[...] = jnp.zeros_like(dk_ref)
            dv_ref[...] = jnp.zeros_like(dv_ref)

        dk_ref[...] += dk
        dv_ref[...] += dv

    qblk = pl.BlockSpec((tq, LANES), lambda h, i: (i, h))
    kblk = pl.BlockSpec((nm, LANES), lambda h, i: (0, h))
    return pl.pallas_call(
        body, grid=(nh, t // tq),
        in_specs=[qblk, kblk, pl.BlockSpec((nm, LANES), lambda h, i: (0, nh + h)), qblk],
        out_specs=[qblk, kblk, kblk],
        out_shape=[_sds((t, dx), BF16), _sds((nm, dx), F32), _sds((nm, dx), F32)],
        compiler_params=_params(), name="xattn_bwd")(qx, kv, kv, do)


def _row_tile(rows, cols, budget=1 << 20):
    tb = rows
    while tb % (4 * SUBLANES) == 0 and tb * cols * 4 > budget:
        tb //= 2
    return tb


def _adamw(w, g, m, v):
    shape = w.shape
    cols = shape[-1]
    w2, g2, m2, v2 = (a.reshape(-1, cols) for a in (w, g, m, v))
    rows = w2.shape[0]
    tb = _row_tile(rows, cols)

    def body(w_ref, g_ref, m_ref, v_ref, d_ref, nm_ref, nv_ref):
        gg = g_ref[...]
        nm = ADAM_B1 * m_ref[...] + (1.0 - ADAM_B1) * gg
        nv = ADAM_B2 * v_ref[...] + (1.0 - ADAM_B2) * (gg * gg)
        m_hat = nm / (1.0 - ADAM_B1 ** ADAM_STEP)
        v_hat = nv / (1.0 - ADAM_B2 ** ADAM_STEP)
        d_ref[...] = -ADAM_LR * (m_hat / (jnp.sqrt(v_hat) + ADAM_EPS) + ADAM_WD * w_ref[...])
        nm_ref[...] = nm
        nv_ref[...] = nv

    blk = pl.BlockSpec((tb, cols), lambda i: (i, 0))
    out = pl.pallas_call(
        body, grid=(rows // tb,), in_specs=[blk] * 4, out_specs=[blk] * 3, out_shape=[_sds((rows, cols), F32)] * 3,
        compiler_params=_params(), name="adamw")(w2, g2, m2, v2)
    return tuple(o.reshape(shape) for o in out)


def _sum_devices(gathered, rows):
    cols = gathered.shape[1]

    def body(g_ref, o_ref):
        acc = g_ref[0:rows, :]
        for dev in range(1, N_DEV):
            acc = acc + g_ref[dev * rows:(dev + 1) * rows, :]
        o_ref[...] = acc

    return pl.pallas_call(body, out_shape=_sds((rows, cols), F32), compiler_params=_params(), name="sum_devices")(gathered)


def _place():
    x, y, c = lax.axis_index("x"), lax.axis_index("y"), lax.axis_index("c")
    chips = [(1 - x, y), (x, 1 - y), (1 - x, 1 - y)]
    return x, y, c, chips


def _chip_id(chip):
    return 2 * chip[0] + chip[1]


def _half(c, rows):
    rh = rows // 2
    return pl.ds(pl.multiple_of(c * rh, 16), rh)


_HBM = pl.BlockSpec(memory_space=pltpu.HBM)


def _allgather8(x_shard):
    m_per, n = x_shard.shape

    def body(x_ref, out_ref, send_sems, recv_sems, local_sem):
        x, y, c, chips = _place()
        me, sibling = (x, y, c), (x, y, 1 - c)

        def rows(px, py, pc):
            return out_ref.at[pl.ds((4 * px + 2 * py + pc) * m_per, m_per), :]

        def copy(k, block, to, src=None):
            return pltpu.make_async_remote_copy(
                src_ref=rows(*block) if src is None else src, dst_ref=rows(*block),
                send_sem=send_sems.at[k], recv_sem=recv_sems.at[k], device_id=to, device_id_type=MESH)

        mine = pltpu.make_async_copy(x_ref, rows(*me), local_sem)
        mine.start()
        first = [copy(0, me, sibling, src=x_ref)]
        first += [copy(1 + j, me, (*chip, c), src=x_ref) for j, chip in enumerate(chips)]
        for cp in first:
            cp.start()
        passed = [copy(4 + j, (*chip, c), sibling) for j, chip in enumerate(chips)]
        for j, chip in enumerate(chips):
            copy(1 + j, (*chip, c), me).wait_recv()
            passed[j].start()
        copy(0, sibling, me).wait_recv()
        for j, chip in enumerate(chips):
            copy(4 + j, (*chip, 1 - c), me).wait_recv()
        for cp in first + passed:
            cp.wait_send()
        mine.wait()

    return pl.pallas_call(
        body, out_shape=_sds((N_DEV * m_per, n), x_shard.dtype),
        in_specs=[pl.BlockSpec(memory_space=pltpu.VMEM)], out_specs=pl.BlockSpec(memory_space=pltpu.VMEM),
        scratch_shapes=[pltpu.SemaphoreType.DMA((7,)), pltpu.SemaphoreType.DMA((7,)), pltpu.SemaphoreType.DMA],
        compiler_params=_params(), name="allgather8")(x_shard)


def _gather_weights(ws, layer):
    n = len(ws)

    def body(*refs):
        w, out = refs[:n], refs[n:2 * n]
        send_i, recv_i, send_d, recv_d = refs[2 * n:]
        x, y, c, chips = _place()
        sibling = (x, y, 1 - c)
        me_chip = 2 * x + y

        def slab(i, chip, half):
            return out[i].at[chip, _half(half, w[i].shape[1]), :]

        def over_ici(i, j, src_chip, to, src=None):
            return pltpu.make_async_remote_copy(
                src_ref=slab(i, src_chip, c) if src is None else src, dst_ref=slab(i, src_chip, c),
                send_sem=send_i.at[3 * i + j], recv_sem=recv_i.at[3 * i + j], device_id=to, device_id_type=MESH)

        def over_d2d(i, j, half):
            return pltpu.make_async_remote_copy(
                src_ref=slab(i, _chip_id(chips[j]), half), dst_ref=slab(i, _chip_id(chips[j]), half),
                send_sem=send_d.at[3 * i + j], recv_sem=recv_d.at[3 * i + j], device_id=sibling, device_id_type=MESH)

        sent = []
        for i in range(n):
            for j, chip in enumerate(chips):
                cp = over_ici(i, j, me_chip, (*chip, c), src=w[i].at[layer, _half(c, w[i].shape[1]), :])
                cp.start()
                sent.append(cp)
        for i in range(n):
            for j, chip in enumerate(chips):
                over_ici(i, j, _chip_id(chip), (x, y, c)).wait_recv()
                cp = over_d2d(i, j, c)
                cp.start()
                sent.append(cp)
        for i in range(n):
            for j in range(3):
                over_d2d(i, j, 1 - c).wait_recv()
        for cp in sent:
            cp.wait_send()

    return pl.pallas_call(
        body, in_specs=[_HBM] * n, out_specs=[_HBM] * n,
        out_shape=[_sds((N_CHIPS,) + a.shape[1:], a.dtype) for a in ws],
        scratch_shapes=[pltpu.SemaphoreType.DMA((3 * n,)) for _ in range(4)],
        compiler_params=_params(), name="gather_weights")(*ws)


_SEM = pl.BlockSpec(memory_space=pltpu.SEMAPHORE)
_DATAFLOW = pltpu.SideEffectType.DATAFLOW_SIDE_EFFECTING


def _gather_copies(w, land, send, recv, layer):
    x, y, c, chips = _place()
    return [pltpu.make_async_remote_copy(
        src_ref=w[i].at[layer], dst_ref=land[i].at[2 * x + y], send_sem=send.at[3 * i + j], recv_sem=recv.at[3 * i + j],
        device_id=(*chip, c), device_id_type=MESH) for i in range(len(w)) for j, chip in enumerate(chips)]


def _gather_start(ws, layer, after):
    n = len(ws)

    def body(*refs):
        w, land = refs[:n], refs[n:2 * n]
        send, recv = refs[2 * n + 1], refs[2 * n + 2]
        for cp in _gather_copies(w, land, send, recv, layer):
            cp.start()
        refs[-1][...] = jnp.zeros_like(refs[-1])

    lands = [lax.empty((N_CHIPS,) + a.shape[1:], a.dtype) for a in ws]
    operands = [pltpu.with_memory_space_constraint(a, pltpu.HBM) for a in list(ws) + lands]
    out = pl.pallas_call(
        body, name=f"gather_start_{layer}",
        out_shape=(pltpu.SemaphoreType.DMA((3 * n,)), pltpu.SemaphoreType.DMA((3 * n,)),
                   *[pltpu.HBM(a.shape, a.dtype) for a in operands], _sds((SUBLANES, LANES), F32)),
        in_specs=[_HBM] * (2 * n) + [pl.BlockSpec(memory_space=pl.ANY)],
        out_specs=(_SEM, _SEM, *[_HBM] * (2 * n), pl.BlockSpec(memory_space=pltpu.VMEM)),
        input_output_aliases={i: 2 + i for i in range(2 * n)},
        compiler_params=pltpu.CompilerParams(has_side_effects=_DATAFLOW))(*operands, after)
    return out[0], out[1], list(out[2:2 + n]), list(out[2 + n:2 + 2 * n]), out[-1]


def _gather_wait(send, recv, ws, lands, after, layer):
    n = len(ws)

    def body(*refs):
        for cp in _gather_copies(refs[:n], refs[n:2 * n], refs[2 * n], refs[2 * n + 1], layer):
            cp.wait_send()
            cp.wait_recv()

    out = pl.pallas_call(
        body, name=f"gather_wait_{layer}",
        out_shape=tuple(pltpu.HBM(a.shape, a.dtype) for a in list(ws) + list(lands)),
        in_specs=[_HBM] * (2 * n) + [_SEM, _SEM, pl.BlockSpec(memory_space=pl.ANY)], out_specs=tuple([_HBM] * (2 * n)),
        input_output_aliases={i: i for i in range(2 * n)},
        compiler_params=pltpu.CompilerParams(has_side_effects=_DATAFLOW))(*ws, *lands, send, recv, after)
    return list(out[:n]), list(out[n:])


def _swap_halves(gs):
    n = len(gs)

    def body(*refs):
        g, out = refs[:n], refs[n:2 * n]
        send, recv = refs[2 * n:]
        x, y, c, _ = _place()
        cps = [pltpu.make_async_remote_copy(
            src_ref=g[i].at[:, :, _half(1 - c, g[i].shape[2]), :], dst_ref=out[i],
            send_sem=send.at[i], recv_sem=recv.at[i], device_id=(x, y, 1 - c), device_id_type=MESH) for i in range(n)]
        for cp in cps:
            cp.start()
        for cp in cps:
            cp.wait()

    return pl.pallas_call(
        body, in_specs=[_HBM] * n, out_specs=[_HBM] * n,
        out_shape=[_sds(a.shape[:2] + (a.shape[2] // 2, a.shape[3]), a.dtype) for a in gs],
        scratch_shapes=[pltpu.SemaphoreType.DMA((n,)), pltpu.SemaphoreType.DMA((n,))],
        compiler_params=_params(), name="swap_halves")(*gs)


def _exchange_chips(ss):
    n = len(ss)

    def body(*refs):
        s, out = refs[:n], refs[n:2 * n]
        send, recv = refs[2 * n:]
        x, y, c, chips = _place()
        cps = []
        for i in range(n):
            for j, chip in enumerate(chips):
                cps.append(pltpu.make_async_remote_copy(
                    src_ref=s[i].at[:, _chip_id(chip)], dst_ref=out[i].at[j],
                    send_sem=send.at[3 * i + j], recv_sem=recv.at[3 * i + j], device_id=(*chip, c), device_id_type=MESH))
        for cp in cps:
            cp.start()
        for cp in cps:
            cp.wait()

    return pl.pallas_call(
        body, in_specs=[_HBM] * n, out_specs=[_HBM] * n,
        out_shape=[_sds((3, a.shape[0]) + a.shape[2:], a.dtype) for a in ss],
        scratch_shapes=[pltpu.SemaphoreType.DMA((3 * n,)), pltpu.SemaphoreType.DMA((3 * n,))],
        compiler_params=_params(), name="exchange_chips")(*ss)


def _exchange_copies(s, land, send, recv):
    x, y, c, chips = _place()
    return [pltpu.make_async_remote_copy(
        src_ref=s[i].at[:, _chip_id(chip)], dst_ref=land[i].at[j], send_sem=send.at[3 * i + j], recv_sem=recv.at[3 * i + j],
        device_id=(*chip, c), device_id_type=MESH) for i in range(len(s)) for j, chip in enumerate(chips)]


def _exchange_start(ss, layer):
    n = len(ss)

    def body(*refs):
        s, land = refs[:n], refs[n:2 * n]
        send, recv = refs[2 * n], refs[2 * n + 1]
        token = refs[-1]
        for cp in _exchange_copies(s, land, send, recv):
            cp.start()
        token[...] = jnp.zeros_like(token)

    lands = [lax.empty((3, a.shape[0]) + a.shape[2:], a.dtype) for a in ss]
    operands = [pltpu.with_memory_space_constraint(a, pltpu.HBM) for a in list(ss) + lands]
    out = pl.pallas_call(
        body, name=f"exchange_start_{layer}",
        out_shape=(pltpu.SemaphoreType.DMA((3 * n,)), pltpu.SemaphoreType.DMA((3 * n,)),
                   *[pltpu.HBM(a.shape, a.dtype) for a in operands], _sds((SUBLANES, LANES), F32)),
        in_specs=[_HBM] * (2 * n), out_specs=(_SEM, _SEM, *[_HBM] * (2 * n), pl.BlockSpec(memory_space=pltpu.VMEM)),
        input_output_aliases={i: 2 + i for i in range(2 * n)},
        compiler_params=pltpu.CompilerParams(has_side_effects=_DATAFLOW))(*operands)
    return out[0], out[1], list(out[2:2 + n]), list(out[2 + n:2 + 2 * n]), out[-1]


def _exchange_wait(send, recv, ss, lands, after, layer):
    n = len(ss)

    def body(*refs):
        s, land = refs[:n], refs[n:2 * n]
        for cp in _exchange_copies(s, land, refs[2 * n], refs[2 * n + 1]):
            cp.wait_send()
            cp.wait_recv()

    out = pl.pallas_call(
        body, name=f"exchange_wait_{layer}",
        out_shape=tuple(pltpu.HBM(a.shape, a.dtype) for a in list(ss) + list(lands)),
        in_specs=[_HBM] * (2 * n) + [_SEM, _SEM, pl.BlockSpec(memory_space=pl.ANY)], out_specs=tuple([_HBM] * (2 * n)),
        input_output_aliases={i: i for i in range(2 * n)},
        compiler_params=pltpu.CompilerParams(has_side_effects=_DATAFLOW))(*ss, *lands, send, recv, after)
    return list(out[:n]), list(out[n:])


def _join_halves(fs):
    n = len(fs)

    def body(*refs):
        f, out = refs[:n], refs[n:2 * n]
        send, recv = refs[2 * n:]
        x, y, c, _ = _place()

        def half(ref, which):
            return ref.at[:, _half(which, ref.shape[1]), :]

        for i in range(n):
            pltpu.make_async_remote_copy(
                src_ref=half(f[i], c), dst_ref=half(out[i], c), send_sem=send.at[i], recv_sem=recv.at[i],
                device_id=(x, y, 1 - c), device_id_type=MESH).start()
        for i in range(n):
            arrive = pltpu.make_async_remote_copy(
                src_ref=half(f[i], c), dst_ref=half(out[i], 1 - c), send_sem=send.at[i], recv_sem=recv.at[i],
                device_id=(x, y, 1 - c), device_id_type=MESH)
            arrive.wait_recv()
            arrive.wait_send()

    return pl.pallas_call(
        body, in_specs=[_HBM] * n, out_specs=[_HBM] * n, out_shape=[_sds(a.shape, a.dtype) for a in fs],
        input_output_aliases={i: i for i in range(n)},
        scratch_shapes=[pltpu.SemaphoreType.DMA((n,)), pltpu.SemaphoreType.DMA((n,))],
        compiler_params=_params(), name="join_halves")(*fs)


def _add_sibling(g, got, c_idx):
    nl, ns, r, cols = g.shape
    rh = r // 2
    tb = _row_tile(rh, cols)
    nb = rh // tb
    g3, got3 = g.reshape(nl * ns, r, cols), got.reshape(nl * ns, rh, cols)

    def body(c_ref, a_ref, b_ref, o_ref):
        o_ref[...] = (a_ref[...].astype(F32) + b_ref[...].astype(F32)).astype(o_ref.dtype)

    out = pl.pallas_call(
        body,
        grid_spec=pltpu.PrefetchScalarGridSpec(
            num_scalar_prefetch=1, grid=(nl * ns, nb),
            in_specs=[pl.BlockSpec((None, tb, cols), lambda s, i, c_ref: (s, c_ref[0] * nb + i, 0)),
                      pl.BlockSpec((None, tb, cols), lambda s, i, c_ref: (s, i, 0))],
            out_specs=pl.BlockSpec((None, tb, cols), lambda s, i, c_ref: (s, i, 0))),
        out_shape=_sds((nl * ns, rh, cols), BF16), compiler_params=_params(), name="add_sibling")(c_idx, g3, got3)
    return out.reshape(nl, ns, rh, cols)


def _add_chips(s, got, chip_idx, c_idx, layer, depth, prev):
    _, ns, rh, cols = s.shape
    tb = _row_tile(rh, cols)
    nb = rh // tb

    def body(k_ref, c_ref, a_ref, b0_ref, b1_ref, b2_ref, *rest):
        o_ref = rest[-1]
        o_ref[...] = ((a_ref[...].astype(F32) + b0_ref[...].astype(F32)) + b1_ref[...].astype(F32)) + b2_ref[...].astype(F32)

    extra = [] if prev is None else [prev]
    return pl.pallas_call(
        body,
        grid_spec=pltpu.PrefetchScalarGridSpec(
            num_scalar_prefetch=2, grid=(nb,),
            in_specs=[pl.BlockSpec((None, None, tb, cols), lambda i, k_ref, c_ref: (0, k_ref[0], i, 0))]
            + [pl.BlockSpec((None, None, tb, cols), lambda i, k_ref, c_ref, j=j: (j, 0, i, 0)) for j in range(3)]
            + [pl.BlockSpec(memory_space=pl.ANY)] * len(extra),
            out_specs=pl.BlockSpec((None, tb, cols), lambda i, k_ref, c_ref: (layer, c_ref[0] * nb + i, 0))),
        out_shape=_sds((depth, 2 * rh, cols), F32), input_output_aliases={6: 0} if extra else {},
        compiler_params=_params(), name="add_chips")(chip_idx, c_idx, s, got, got, got, *extra)


def _in_pieces(d, heads, ws):
    groups, start = [], 0
    for k in range(12):
        if k == 6:
            groups.append(("f", 0, start, heads))
            start += heads
        else:
            groups.append(("main", (k - (k > 6)) * d, start, d))
            start += d
    pieces = []
    for chip in range(N_CHIPS):
        lo, hi = chip * ws, (chip + 1) * ws
        for dest, dcol, gstart, w in groups:
            a, b = max(lo, gstart), min(hi, gstart + w)
            if a < b:
                pieces.append((chip, a - lo, dest, dcol + a - gstart, b - a))
    return pieces


def _w_in_aligned(gathered, d, heads):
    rows, ws = gathered.shape[1], gathered.shape[2]
    tb = min(256, rows)
    pieces = _in_pieces(d, heads, ws)

    def body(g_ref, m_ref, f_ref):
        f_ref[...] = jnp.zeros_like(f_ref)
        for chip, sc, dest, dc, w in pieces:
            (m_ref if dest == "main" else f_ref)[:, dc:dc + w] = g_ref[chip, :, sc:sc + w]

    return pl.pallas_call(
        body, grid=(rows // tb,), in_specs=[pl.BlockSpec((N_CHIPS, tb, ws), lambda i: (0, i, 0))],
        out_specs=[pl.BlockSpec((tb, 11 * d), lambda i: (i, 0)), pl.BlockSpec((tb, LANES), lambda i: (i, 0))],
        out_shape=[_sds((rows, 11 * d), gathered.dtype), _sds((rows, LANES), gathered.dtype)],
        compiler_params=_params(), name="w_in_aligned")(gathered)


def _w_in_shards(gm, gf, prev, l, shape, heads):
    rows, d = gm.shape[0], gm.shape[1] // 11
    ws = shape[3]
    tb = min(256, rows)
    pieces = _in_pieces(d, heads, ws)

    def body(m_ref, f_ref, *rest):
        o_ref = rest[-1]
        for chip, sc, dest, dc, w in pieces:
            o_ref[chip, :, sc:sc + w] = (m_ref if dest == "main" else f_ref)[:, dc:dc + w]

    extra = [] if prev is None else [prev]
    return pl.pallas_call(
        body, grid=(rows // tb,),
        in_specs=[pl.BlockSpec((tb, 11 * d), lambda i: (i, 0)), pl.BlockSpec((tb, LANES), lambda i: (i, 0))]
        + [pl.BlockSpec(memory_space=pl.ANY)] * len(extra),
        out_specs=pl.BlockSpec((None, N_CHIPS, tb, ws), lambda i: (l, 0, i, 0)), out_shape=_sds(shape, gm.dtype),
        input_output_aliases={2: 0} if extra else {},
        compiler_params=_params(), name="w_in_shards")(gm, gf, *extra)


def _pad_lanes(a, width=LANES):
    return jnp.pad(a, ((0, 0), (0, width - a.shape[1])))


def _cols(operand):
    return sum(p[2] for p in operand) if isinstance(operand, list) else operand.shape[1]


def _local_step(x, mem, tgt, sp, wt, dests=None, on_layer=None):
    t, d = x.shape
    depth = sp["mix_norm"].shape[0]
    heads = sp["b_forget"].shape[1]
    row = lambda a, l: a[l:l + 1]
    weights = wt if callable(wt) else (lambda l, x_in: {k: v[l] for k, v in wt.items()})
    layer_weights = []

    memn, = _rows("rms_mem", _rms_fn, [_part(mem)], [sp["mem_norm"][None]], [(d, BF16)])
    saved = []
    for l in range(depth):
        wl = weights(l, x)
        layer_weights.append(wl)
        f = _cols(wl["gu"]) // 2
        sv = {"x0": x}
        h1, = _rows("rms_mix", _rms_fn, [_part(x)], [row(sp["mix_norm"], l)], [(d, BF16)])
        wm = wl["main"]
        za = _mm("mm_in_a", h1, [_part(wm, 0, 3 * d)])
        zqkv = _mm("mm_in_qkv", h1, [_part(wm, 3 * d, 3 * d)], out_dtype=BF16)
        zcg = _mm("mm_in_cg", h1, [_part(wm, 6 * d, 5 * d)])
        zf = _mm("mm_in_f", h1, wl["f"])
        bf = _pad_lanes(row(sp["b_forget"], l))
        c = _fox_cumsum(zf, bf)
        crow = c[:, :heads].T.reshape(heads, 1, t)
        o = _fox_fwd(zqkv, c, crow)
        pa = _conv_a_fwd(za, sp["conv_a"][l])
        u2 = _conv_c_fwd(zcg, sp["conv_c"][l], row(sp["conv_c_bias"], l))
        u3, = _rows("ln_silu", _lnsilu_fn, [_part(u2)], [row(sp["ln_c_gain"], l), row(sp["ln_c_bias"], l)], [(d, BF16)])
        ya = _mm("mm_out_a", pa, wl["out_a"])
        yb = _mm("mm_out_b", o, wl["out_b"])
        yc = _mm("mm_out_c", u3, wl["out_c"])
        bg = [sp["b_gate"][l:l + 1, k * d:(k + 1) * d] for k in range(3)]
        gate_rows = [_part(zcg, (2 + k) * d, d) for k in range(3)] + [_part(ya), _part(yb), _part(yc)]
        mg, = _rows("merge", _merge_fn, gate_rows, bg, [(d, BF16)])
        x1 = _mm("mm_o", mg, wl["o"], add=x)
        h2, = _rows("rms_x", _rms_fn, [_part(x1)], [row(sp["xattn_norm"], l)], [(d, BF16)])
        qx = _mm("mm_xq", h2, wl["xq"], out_dtype=BF16)
        kv = _mm("mm_xkv", memn, wl["xkv"], out_dtype=BF16)
        ox = _xattn_fwd(qx, kv)
        x2 = _mm("mm_xo", ox, wl["xo"], add=x1)
        h3, = _rows("rms_ffn", _rms_fn, [_part(x2)], [row(sp["ffn_norm"], l)], [(d, BF16)])
        gu = _mm("mm_gu", h3, wl["gu"])
        act, = _rows("swiglu", _swiglu_fn, [_part(gu, 0, f), _part(gu, f, f)], [], [(f, BF16)])
        x3 = _mm("mm_down", act, wl["down"], add=x2)
        sv.update(h1=h1, za=za, zqkv=zqkv, zcg=zcg, zf=zf, bf=bf, c=c, crow=crow, o=o, pa=pa, u2=u2, u3=u3, ya=ya, yb=yb,
                  yc=yc, bg=bg, gate_rows=gate_rows, mg=mg, x1=x1, h2=h2, qx=qx, kv=kv, ox=ox, x2=x2, h3=h3, gu=gu, act=act)
        saved.append(sv)
        x = x3

    loss_row, dx, d_final = _loss_call(x, tgt, sp["final_norm"][None])

    gs = {k: [None] * depth for k in ("mix_norm", "b_gate", "b_forget", "conv_a", "conv_c", "conv_c_bias", "ln_c_gain",
                                      "ln_c_bias", "xattn_norm", "ffn_norm")}
    gw = {k: [None] * depth for k in ("main", "f", "out_a", "out_b", "out_c", "o", "xq", "xkv", "xo", "gu", "down")}
    def weight_grad(key, l, name, a_op, b_op):
        if dests is None or key not in dests:
            gw[key][l] = _mm(name, a_op, b_op, ta=True, out_dtype=BF16)
        else:
            shape, shards = dests[key]
            gw[key][l] = _mm(name, a_op, b_op, ta=True, out_dtype=BF16, dest=(shape, None, (0,), shards))

    dmemn = None
    for l in reversed(range(depth)):
        sv, wl = saved[l], layer_weights[l]
        dact = _mm("mm_down_dx", dx, wl["down"], tb=True)
        weight_grad("down", l, "mm_down_dw", sv["act"], dx)
        (dg, du), _ = _rows_vjp("swiglu_bwd", _swiglu_fn, [_part(sv["gu"], 0, f), _part(sv["gu"], f, f)], [], [_part(dact)],
                                [BF16, BF16], [])
        dgu = [_part(dg), _part(du)]
        dh3 = _mm("mm_gu_dx", dgu, wl["gu"], tb=True)
        weight_grad("gu", l, "mm_gu_dw", sv["h3"], dgu)
        (dx2,), (gs["ffn_norm"][l],) = _rows_vjp("rms_ffn_bwd", _rms_fn, [_part(sv["x2"])], [row(sp["ffn_norm"], l)],
                                                 [_part(dh3)], [F32], [True], adds={0: _part(dx)})
        dox = _mm("mm_xo_dx", dx2, wl["xo"], tb=True)
        weight_grad("xo", l, "mm_xo_dw", sv["ox"], dx2)
        dqx, dkx, dvx = _xattn_bwd(sv["qx"], sv["kv"], dox)
        dh2 = _mm("mm_xq_dx", dqx, wl["xq"], tb=True)
        weight_grad("xq", l, "mm_xq_dw", sv["h2"], dqx)
        dkv = [_part(dkx), _part(dvx)]
        dmemn = _mm("mm_xkv_dx", dkv, wl["xkv"], tb=True, add=dmemn)
        weight_grad("xkv", l, "mm_xkv_dw", memn, dkv)
        (dx1,), (gs["xattn_norm"][l],) = _rows_vjp("rms_x_bwd", _rms_fn, [_part(sv["x1"])], [row(sp["xattn_norm"], l)],
                                                   [_part(dh2)], [F32], [True], adds={0: _part(dx2)})
        dmg = _mm("mm_o_dx", dx1, wl["o"], tb=True)
        weight_grad("o", l, "mm_o_dw", sv["mg"], dx1)
        (dga, dgb, dgc, dya, dyb, dyc), dbg = _rows_vjp("merge_bwd", _merge_fn, sv["gate_rows"], sv["bg"], [_part(dmg)],
                                                        [BF16] * 6, [True] * 3)
        gs["b_gate"][l] = jnp.concatenate(dbg, axis=1)
        dpa = _mm("mm_out_a_dx", dya, wl["out_a"], tb=True)
        weight_grad("out_a", l, "mm_out_a_dw", sv["pa"], dya)
        do = _mm("mm_out_b_dx", dyb, wl["out_b"], tb=True)
        weight_grad("out_b", l, "mm_out_b_dw", sv["o"], dyb)
        du3 = _mm("mm_out_c_dx", dyc, wl["out_c"], tb=True)
        weight_grad("out_c", l, "mm_out_c_dw", sv["u3"], dyc)
        (du2,), (gs["ln_c_gain"][l], gs["ln_c_bias"][l]) = _rows_vjp(
            "ln_silu_bwd", _lnsilu_fn, [_part(sv["u2"])], [row(sp["ln_c_gain"], l), row(sp["ln_c_bias"], l)], [_part(du3)],
            [F32], [True, True])
        dcv, dcg, gs["conv_c"][l], gs["conv_c_bias"][l] = _conv_c_bwd(sv["zcg"], sp["conv_c"][l], du2)
        dab, dac, dau, gs["conv_a"][l] = _conv_a_bwd(sv["za"], sp["conv_a"][l], dpa)
        dq, dk, dv, dcc, dcr = _fox_bwd(sv["zqkv"], sv["c"], sv["crow"], do)
        dcr_t = _pad_lanes(dcr.reshape(heads, t).T)
        dzf, dbf = _fox_cumsum_bwd(sv["zf"], sv["bf"], dcc, dcr_t)
        gs["b_forget"][l] = dbf[:, :heads]
        dz = [_part(a) for a in (dab, dac, dau, dq, dk, dv, dcv, dcg, dga, dgb, dgc)]
        dh1f = _mm("mm_in_f_dx", dzf, wl["f"], tb=True)
        dh1 = _mm("mm_in_dx", dz, wl["main"], tb=True, add=dh1f)
        gw["main"][l] = _mm("mm_in_dw", sv["h1"], dz, ta=True, out_dtype=BF16)
        gw["f"][l] = _mm("mm_in_f_dw", sv["h1"], dzf, ta=True, out_dtype=BF16)
        (dx,), (gs["mix_norm"][l],) = _rows_vjp("rms_mix_bwd", _rms_fn, [_part(sv["x0"])], [row(sp["mix_norm"], l)],
                                                [_part(dh1)], [F32], [True], adds={0: _part(dx1)})
        if on_layer is not None:
            token = on_layer(l, {k: v[l] for k, v in gw.items()})
            if l > 0:
                dx = dx + token[0, 0]

    _, (d_mem_norm,) = _rows_vjp("rms_mem_bwd", _rms_fn, [_part(mem)], [sp["mem_norm"][None]], [_part(dmemn)], [None], [True])
    small = {k: jnp.stack([a.reshape(sp[k].shape[1:]) for a in v]) for k, v in gs.items()}
    small["mem_norm"] = d_mem_norm[0]
    small["final_norm"] = d_final[0]
    return loss_row, dx, small, gw


_BIG = ("w_in", "w_out_a", "w_out_b", "w_out_c", "w_o", "w_xq", "w_xkv", "w_xo", "w_gate_up", "w_down")
_COL_SHARDED = ("w_in", "w_xo", "w_gate_up")
_KEY = {"w_out_a": "out_a", "w_out_b": "out_b", "w_out_c": "out_c", "w_o": "o", "w_xq": "xq", "w_xkv": "xkv", "w_xo": "xo",
        "w_gate_up": "gu", "w_down": "down"}
_SMALL = ("mix_norm", "b_gate", "b_forget", "conv_c_bias", "ln_c_gain", "ln_c_bias", "xattn_norm", "ffn_norm", "mem_norm",
          "final_norm")
_SMALL_SHARDED = ("conv_a", "conv_c")


def _pack_rows(parts):
    padded = []
    for a in parts:
        pad = -a.shape[0] % SUBLANES
        padded.append(jnp.pad(a, ((0, pad), (0, 0))) if pad else a)
    return jnp.concatenate(padded, axis=0)


def _unpack_rows(packed, shapes):
    out, pos = [], 0
    for r in shapes:
        out.append(packed[pos:pos + r])
        pos += r + (-r % SUBLANES)
    return out


def kernel(x, mem, mix_norm, w_in, b_gate, b_forget, conv_a, w_out_a, w_out_b, conv_c, conv_c_bias, ln_c_gain, ln_c_bias, w_out_c, w_o, xattn_norm, mem_norm, w_xq, w_xkv, w_xo, ffn_norm, w_gate_up, w_down, final_norm, loss_target, m_mix_norm, m_w_in, m_b_gate, m_b_forget, m_conv_a, m_w_out_a, m_w_out_b, m_conv_c, m_conv_c_bias, m_ln_c_gain, m_ln_c_bias, m_w_out_c, m_w_o, m_xattn_norm, m_mem_norm, m_w_xq, m_w_xkv, m_w_xo, m_ffn_norm, m_w_gate_up, m_w_down, m_final_norm, v_mix_norm, v_w_in, v_b_gate, v_b_forget, v_conv_a, v_w_out_a, v_w_out_b, v_conv_c, v_conv_c_bias, v_ln_c_gain, v_ln_c_bias, v_w_out_c, v_w_o, v_xattn_norm, v_mem_norm, v_w_xq, v_w_xkv, v_w_xo, v_ffn_norm, v_w_gate_up, v_w_down, v_final_norm):
    args = dict(locals())
    names = ["mix_norm", "w_in", "b_gate", "b_forget", "conv_a", "w_out_a", "w_out_b", "conv_c", "conv_c_bias", "ln_c_gain",
             "ln_c_bias", "w_out_c", "w_o", "xattn_norm", "mem_norm", "w_xq", "w_xkv", "w_xo", "ffn_norm", "w_gate_up", "w_down",
             "final_norm"]
    depth, d = mix_norm.shape
    heads = b_forget.shape[1]
    cx, cy, cc = lax.axis_index("x"), lax.axis_index("y"), lax.axis_index("c")
    chip = 2 * cx + cy
    c_idx = jnp.reshape(cc, (1,)).astype(jnp.int32)
    chip_idx = jnp.reshape(chip, (1,)).astype(jnp.int32)

    own = [args[n].astype(BF16) for n in _BIG]
    first = _gather_weights(own, 0)
    started, tokens = {}, []
    for l in range(1, depth):
        send, recv, thru, lands, token = _gather_start(own, l, first[0])
        started[l] = (send, recv, thru, lands)
        tokens.append(token)

    def layer_weights(l, x_in):
        if l == 0:
            lands = first
        else:
            _, lands = _gather_wait(*started[l], x_in, l)
        full = [lax.dynamic_update_slice(g, w[l][None], (chip, 0, 0)) for g, w in zip(lands, own)]
        wl = {}
        wl["main"], wl["f"] = _w_in_aligned(full[0], d, heads)
        for n, g in zip(_BIG[1:], full[1:]):
            if n in _COL_SHARDED:
                wl[_KEY[n]] = [_part(g, lead=(j,)) for j in range(N_CHIPS)]
            else:
                wl[_KEY[n]] = g.reshape(N_CHIPS * g.shape[1], g.shape[2])
        return wl

    shard_shape = {n: (1, N_CHIPS) + args[n].shape[1:] for n in _BIG}
    dests = {_KEY[n]: (shard_shape[n], N_CHIPS) if n in _COL_SHARDED
             else ((1, N_CHIPS * args[n].shape[1], args[n].shape[2]), 1) for n in _BIG[1:]}
    conv_rows = [conv_a.shape[1] * depth, conv_c.shape[1] * depth]
    conv_pack = _pack_rows([conv_a.reshape(conv_rows[0], -1), conv_c.reshape(conv_rows[1], -1)])
    conv_all = _allgather8(conv_pack).reshape(N_CHIPS, 2, conv_pack.shape[0], conv_pack.shape[1])[:, 0]
    conv_all = conv_all.transpose(1, 0, 2).reshape(conv_pack.shape[0], d)
    conv_a_full, conv_c_full = _unpack_rows(conv_all, conv_rows)
    sp = {n: args[n] for n in _SMALL}
    sp["conv_a"] = conv_a_full.reshape(depth, conv_a.shape[1], d)
    sp["conv_c"] = conv_c_full.reshape(depth, conv_c.shape[1], d)

    in_flight = {}

    def on_layer(l, gl):
        local = [_w_in_shards(gl["main"], gl["f"], None, 0, shard_shape["w_in"], heads)]
        local += [gl[_KEY[n]].reshape(shard_shape[n]) for n in _BIG[1:]]
        chip_sums = [_add_sibling(g, r, c_idx) for g, r in zip(local, _swap_halves(local))]
        send, recv, chip_sums, lands, token = _exchange_start(chip_sums, l)
        in_flight[l] = (send, recv, chip_sums, lands)
        return token

    x_in = x[0]
    for token in tokens:
        x_in = x_in + token[0, 0]
    loss_row, grad_x, gsmall, gw = _local_step(x_in, mem[0], loss_target[0], sp, layer_weights, dests, on_layer)
    halves = [None] * len(_BIG)
    for l in reversed(range(depth)):
        chip_sums, lands = _exchange_wait(*in_flight[l], grad_x, l)
        halves = [_add_chips(s_, r, chip_idx, c_idx, l, depth, h) for s_, r, h in zip(chip_sums, lands, halves)]
    grads = dict(zip(_BIG, _join_halves(halves)))

    def as_rows(n, a):
        return _pad_lanes(a, d) if n == "b_forget" else a.reshape(-1, d)

    pieces = [as_rows(n, gsmall[n]) for n in _SMALL]
    for n in _SMALL_SHARDED:
        pieces.append(gsmall[n].reshape(-1, d))
    pieces.append(_pad_lanes(loss_row, d))
    pack = _pack_rows(pieces)
    total = _sum_devices(_allgather8(pack), pack.shape[0])
    shapes = [p.shape[0] for p in pieces]
    summed = _unpack_rows(total, shapes)
    for n, g in zip(_SMALL, summed):
        grads[n] = g[:, :heads] if n == "b_forget" else g.reshape(args[n].shape)
    for n, g in zip(_SMALL_SHARDED, summed[len(_SMALL):]):
        g = g.reshape(args[n].shape[:2] + (d,))
        grads[n] = lax.dynamic_slice_in_dim(g, chip * args[n].shape[2], args[n].shape[2], axis=2)
    loss = summed[-1][0, 0]

    delta, new_m, new_v = {}, {}, {}
    packed = [_pack_rows([as_rows(n, src[pre + n]) for n in _SMALL])
              for src, pre in ((args, ""), (grads, ""), (args, "m_"), (args, "v_"))]
    srows = [as_rows(n, args[n]).shape[0] for n in _SMALL]
    updated = [_unpack_rows(a, srows) for a in _adamw(*packed)]
    for k, n in enumerate(_SMALL):
        delta[n], new_m[n], new_v[n] = (
            (u[k][:, :heads] if n == "b_forget" else u[k]).reshape(args[n].shape) for u in updated)
    for n in _BIG + _SMALL_SHARDED:
        delta[n], new_m[n], new_v[n] = _adamw(args[n], grads[n], args["m_" + n], args["v_" + n])

    return (loss, grad_x[None], *[grads[n] for n in names], *[delta[n] for n in names], *[new_m[n] for n in names],
            *[new_v[n] for n in names])
```

```python
import functools
import math

import jax
import jax.numpy as jnp
from jax import lax
from jax.experimental import pallas as pl
from jax.experimental.pallas import tpu as pltpu

F32, BF16 = jnp.float32, jnp.bfloat16
EPS = 1e-6
LANES = 128
SUBLANES = 8
VMEM_LIMIT = 56 * 1024 * 1024
MM_VMEM_BUDGET = 36 * 1024 * 1024
MESH = pl.DeviceIdType.MESH
N_CHIPS = 4
N_DEV = 8

ADAM_LR, ADAM_B1, ADAM_B2, ADAM_EPS, ADAM_WD, ADAM_STEP = 0.001, 0.9, 0.999, 1e-08, 0.01, 10


def _params(**kw):
    return pltpu.CompilerParams(vmem_limit_bytes=VMEM_LIMIT, **kw)


def _sds(shape, dtype):
    return jax.ShapeDtypeStruct(tuple(shape), dtype)


def _part(arr, off=0, width=None, lead=()):
    assert arr.ndim == len(lead) + 2
    return (arr, off, arr.shape[-1] - off if width is None else width, tuple(lead))


def _unit(parts):
    u = 0
    for _, off, w, _ in parts:
        u = math.gcd(u, math.gcd(off, w))
    return u


def _pick(dim, unit, cands, lane):
    for c in cands:
        if c <= dim and dim % c == 0 and unit % c == 0 and (not lane or c % LANES == 0):
            return c
    return min(dim, unit)


_COL_TILES = (1024, 1408, 512, 256, 128)
_ROW_TILES = (512, 256, 1408, 128)


def _mm(name, a, b, *, ta=False, tb=False, out_dtype=F32, add=None, dest=None):
    if not isinstance(a, list):
        a = [_part(a)]
    if not isinstance(b, list):
        b = [_part(b)]
    a_rows, a_cols = a[0][0].shape[-2], sum(p[2] for p in a)
    b_rows, b_cols = b[0][0].shape[-2], sum(p[2] for p in b)
    M, Ka = (a_cols, a_rows) if ta else (a_rows, a_cols)
    Kb, N = (b_cols, b_rows) if tb else (b_rows, b_cols)
    assert Ka == Kb, (name, Ka, Kb)
    K = Ka
    ua, ub = _unit(a), _unit(b)
    shards = 1 if dest is None else dest[3]
    tm = _pick(M, ua if ta else M, _ROW_TILES, ta)
    tn = _pick(N, math.gcd(N if tb else ub, N // shards), _COL_TILES, not tb)
    ku = math.gcd(K if ta else ua, ub if tb else K)
    tk = _pick(K, ku, _COL_TILES, (not ta) or tb)
    a_size = sum(p[0].dtype.itemsize for p in a)
    b_size = sum(p[0].dtype.itemsize for p in b)

    def vmem_estimate():
        return (2 * tm * tk * a_size + 2 * tk * tn * b_size + tm * tn * (2 * jnp.dtype(out_dtype).itemsize + 4)
                + (8 * tm * tn if add is not None else 0))

    while vmem_estimate() > MM_VMEM_BUDGET:
        if tk % 256 == 0 and tk >= tn:
            tk //= 2
        elif tn % 256 == 0:
            tn //= 2
        elif tm % 256 == 0:
            tm //= 2
        elif tk % 256 == 0:
            tk //= 2
        else:
            break
    nm, nn, nk = M // tm, N // tn, K // tk
    a_tile, b_tile = (tm if ta else tk), (tk if tb else tn)
    a_nb, b_nb = a[0][2] // a_tile, b[0][2] // b_tile

    def index(parts, nb, tile, p, row, col):
        base = parts[p][1] // tile
        if len(parts) == 1:
            return parts[p][3] + (row, base + col)
        inside = (col // nb) == p
        return parts[p][3] + (jnp.where(inside, row, 0), base + jnp.clip(col - p * nb, 0, nb - 1))

    def spec(parts, p, shape, fn):
        return pl.BlockSpec((None,) * len(parts[p][3]) + shape, fn)

    in_specs, operands = [], []
    for p in range(len(a)):
        if ta:
            in_specs.append(spec(a, p, (tk, tm), lambda mi, nj, kk, p=p: index(a, a_nb, a_tile, p, kk, mi)))
        else:
            in_specs.append(spec(a, p, (tm, tk), lambda mi, nj, kk, p=p: index(a, a_nb, a_tile, p, mi, kk)))
        operands.append(a[p][0])
    for p in range(len(b)):
        if tb:
            in_specs.append(spec(b, p, (tn, tk), lambda mi, nj, kk, p=p: index(b, b_nb, b_tile, p, nj, kk)))
        else:
            in_specs.append(spec(b, p, (tk, tn), lambda mi, nj, kk, p=p: index(b, b_nb, b_tile, p, kk, nj)))
        operands.append(b[p][0])
    if add is not None:
        in_specs.append(pl.BlockSpec((tm, tn), lambda mi, nj, kk: (mi, nj)))
        operands.append(add)
    aliases = {}
    if dest is None:
        out_spec = pl.BlockSpec((tm, tn), lambda mi, nj, kk: (mi, nj))
        out_shape = _sds((M, N), out_dtype)
    else:
        d_shape, d_prev, d_lead, _ = dest
        nbs = N // shards // tn
        if shards == 1:
            out_spec = pl.BlockSpec((None,) * len(d_lead) + (tm, tn), lambda mi, nj, kk: tuple(d_lead) + (mi, nj))
        else:
            out_spec = pl.BlockSpec((None,) * (len(d_lead) + 1) + (tm, tn),
                                    lambda mi, nj, kk: tuple(d_lead) + (nj // nbs, mi, nj % nbs))
        out_shape = _sds(d_shape, out_dtype)
        if d_prev is not None:
            aliases = {len(operands): 0}
            in_specs.append(pl.BlockSpec(memory_space=pl.ANY))
            operands.append(d_prev)
    n_extra = 1 if aliases else 0
    dims = (((0 if ta else 1,), (1 if tb else 0,)), ((), ()))
    na, nb_ = len(a), len(b)

    def body(*refs):
        a_refs, b_refs = refs[:na], refs[na:na + nb_]
        pos = na + nb_
        add_ref = None
        if add is not None:
            add_ref = refs[pos]
            pos += 1
        pos += n_extra
        o_ref = refs[pos]
        acc_ref = refs[pos + 1] if nk > 1 else None
        mi, nj, kk = pl.program_id(0), pl.program_id(1), pl.program_id(2)
        a_col = mi if ta else kk
        b_col = kk if tb else nj

        def finish(r):
            if add_ref is not None:
                r = r + add_ref[...].astype(F32)
            o_ref[...] = r.astype(o_ref.dtype)

        def step(a_ref, b_ref):
            r = lax.dot_general(a_ref[...].astype(BF16), b_ref[...].astype(BF16), dims, preferred_element_type=F32)
            if nk == 1:
                finish(r)
            else:
                @pl.when(kk == 0)
                def _():
                    acc_ref[...] = r

                @pl.when(kk > 0)
                def _():
                    acc_ref[...] += r

        for pa in range(na):
            for pb in range(nb_):
                if na * nb_ == 1:
                    step(a_refs[pa], b_refs[pb])
                else:
                    cond = jnp.logical_and(a_col // a_nb == pa, b_col // b_nb == pb)
                    pl.when(cond)(functools.partial(step, a_refs[pa], b_refs[pb]))
        if nk > 1:
            @pl.when(kk == nk - 1)
            def _():
                finish(acc_ref[...])

    return pl.pallas_call(
        body, grid=(nm, nn, nk), in_specs=in_specs, out_specs=out_spec, out_shape=out_shape, input_output_aliases=aliases,
        scratch_shapes=[pltpu.VMEM((tm, tn), F32)] if nk > 1 else [],
        compiler_params=_params(dimension_semantics=("parallel", "parallel", "arbitrary")),
        name=name)(*operands)


def _row_spec(tb, part):
    _, off, w, _ = part
    return pl.BlockSpec((tb, w), lambda i, o=off // w: (i, o))


def _full_spec(arr):
    return pl.BlockSpec(arr.shape, lambda i: (0,) * arr.ndim)


def _rows(name, fn, rows, pars, outs, tb=256):
    n = rows[0][0].shape[0]
    tb = min(tb, n)
    nr, npar = len(rows), len(pars)

    def body(*refs):
        r = [ref[...].astype(F32) for ref in refs[:nr]]
        p = [ref[...].astype(F32) for ref in refs[nr:nr + npar]]
        res = fn(*r, *p)
        for o_ref, v in zip(refs[nr + npar:], res):
            o_ref[...] = v.astype(o_ref.dtype)

    out = pl.pallas_call(
        body, grid=(n // tb,),
        in_specs=[_row_spec(tb, p) for p in rows] + [_full_spec(p) for p in pars],
        out_specs=[pl.BlockSpec((tb, w), lambda i: (i, 0)) for w, _ in outs],
        out_shape=[_sds((n, w), dt) for w, dt in outs],
        compiler_params=_params(), name=name)(*[p[0] for p in rows], *pars)
    return out


def _rows_vjp(name, fn, rows, pars, cots, row_grads, par_grads, adds=None, tb=256):
    adds = adds or {}
    n = rows[0][0].shape[0]
    tb = min(tb, n)
    nr, npar, nc = len(rows), len(pars), len(cots)
    add_keys = sorted(adds)
    rg = [j for j, dt in enumerate(row_grads) if dt is not None]
    pg = [j for j, f in enumerate(par_grads) if f]

    def body(*refs):
        pos = 0
        r = [ref[...].astype(F32) for ref in refs[pos:pos + nr]]
        pos += nr
        p = [ref[...].astype(F32) for ref in refs[pos:pos + npar]]
        pos += npar
        ct = tuple(ref[...].astype(F32) for ref in refs[pos:pos + nc])
        pos += nc
        add_v = {k: ref[...].astype(F32) for k, ref in zip(add_keys, refs[pos:pos + len(add_keys)])}
        pos += len(add_keys)
        _, vjp = jax.vjp(lambda *args: tuple(fn(*args)), *r, *p)
        g = vjp(ct)
        for j in rg:
            v = g[j]
            if j in add_v:
                v = v + add_v[j]
            refs[pos][...] = v.astype(refs[pos].dtype)
            pos += 1
        first = pl.program_id(0) == 0
        for j in pg:
            ref = refs[pos]
            pos += 1

            @pl.when(first)
            def _(ref=ref):
                ref[...] = jnp.zeros_like(ref)

            ref[...] += g[nr + j]

    in_specs = ([_row_spec(tb, p) for p in rows] + [_full_spec(p) for p in pars] + [_row_spec(tb, p) for p in cots]
                + [_row_spec(tb, adds[k]) for k in add_keys])
    out_specs = ([pl.BlockSpec((tb, rows[j][2]), lambda i: (i, 0)) for j in rg] + [_full_spec(pars[j]) for j in pg])
    out_shape = ([_sds((n, rows[j][2]), row_grads[j]) for j in rg] + [_sds(pars[j].shape, F32) for j in pg])
    out = pl.pallas_call(
        body, grid=(n // tb,), in_specs=in_specs, out_specs=out_specs, out_shape=out_shape,
        compiler_params=_params(), name=name)(
            *[p[0] for p in rows], *pars, *[p[0] for p in cots], *[adds[k][0] for k in add_keys])
    return out[:len(rg)], out[len(rg):]


def _rms_fn(x, g):
    return (x * lax.rsqrt(jnp.mean(x * x, axis=-1, keepdims=True) + EPS) * g,)


def _lnsilu_fn(u, g, b):
    mu = jnp.mean(u, axis=-1, keepdims=True)
    xc = u - mu
    y = xc * lax.rsqrt(jnp.mean(xc * xc, axis=-1, keepdims=True) + EPS) * g + b
    return (y * jax.nn.sigmoid(y),)


def _merge_fn(ga, gb, gc, ya, yb, yc, ba, bb, bc):
    return (jax.nn.sigmoid(ga + ba) * ya + jax.nn.sigmoid(gb + bb) * yb + jax.nn.sigmoid(gc + bc) * yc,)


def _swiglu_fn(g, u):
    return (g * jax.nn.sigmoid(g) * u,)


def _loss_call(x, tgt, gain):
    n, d = x.shape
    tb = min(256, n)

    def body(x_ref, t_ref, g_ref, l_ref, dx_ref, dg_ref):
        (y,), vjp = jax.vjp(lambda a, b: tuple(_rms_fn(a, b)), x_ref[...], g_ref[...])
        e = y - t_ref[...]
        part = 0.5 * jnp.sum(jnp.mean(e * e, axis=-1, keepdims=True))
        dx, dg = vjp((e * (1.0 / d),))
        dx_ref[...] = dx

        @pl.when(pl.program_id(0) == 0)
        def _():
            l_ref[...] = jnp.zeros_like(l_ref)
            dg_ref[...] = jnp.zeros_like(dg_ref)

        l_ref[...] += jnp.full(l_ref.shape, part, F32)
        dg_ref[...] += dg

    return pl.pallas_call(
        body, grid=(n // tb,),
        in_specs=[pl.BlockSpec((tb, d), lambda i: (i, 0)), pl.BlockSpec((tb, d), lambda i: (i, 0)), _full_spec(gain)],
        out_specs=[pl.BlockSpec((1, LANES), lambda i: (0, 0)), pl.BlockSpec((tb, d), lambda i: (i, 0)), _full_spec(gain)],
        out_shape=[_sds((1, LANES), F32), _sds((n, d), F32), _sds(gain.shape, F32)],
        compiler_params=_params(), name="loss_head")(x, tgt, gain)


FOX_BLOCK = 256
CONV_ROWS = 64
CONV_HALO = 32


def _chunk(i):
    return pl.ds(pl.multiple_of(i * CONV_ROWS, CONV_ROWS), CONV_ROWS)


def _delayed(ext, s):
    return (ext if s == 0 else pltpu.roll(ext, s, 0))[CONV_HALO:]


def _advanced(ext, s):
    return (ext if s == 0 else pltpu.roll(ext, ext.shape[0] - s, 0))[:CONV_ROWS]


def _conv_pass(t, taps, w_ref, lanes, get_u, emit, get_dy=None, dw_acc=None):
    def body(i, carry):
        rows = _chunk(i)
        u = get_u(rows)
        ext = jnp.concatenate([carry, u], axis=0)
        dy = None if get_dy is None else get_dy(rows)
        acc = None
        for k in range(taps):
            sh = _delayed(ext, taps - 1 - k)
            term = w_ref[k:k + 1, lanes] * sh
            acc = term if acc is None else acc + term
            if dy is not None:
                dw_acc[k] += jnp.sum((dy * sh).reshape(CONV_ROWS // SUBLANES, SUBLANES, LANES), axis=0)
        emit(rows, u, acc)
        return u[CONV_ROWS - CONV_HALO:]

    lax.fori_loop(0, t // CONV_ROWS, body, jnp.zeros((CONV_HALO, LANES), F32))


def _conv_pass_t(t, taps, w_ref, lanes, get_dy, emit):
    n = t // CONV_ROWS

    def body(j, carry):
        rows = _chunk(n - 1 - j)
        dy = get_dy(rows)
        ext = jnp.concatenate([dy, carry], axis=0)
        acc = None
        for k in range(taps):
            term = w_ref[k:k + 1, lanes] * _advanced(ext, taps - 1 - k)
            acc = term if acc is None else acc + term
        emit(rows, acc)
        return dy[:CONV_HALO]

    lax.fori_loop(0, n, body, jnp.zeros((CONV_HALO, LANES), F32))


def _chan_block(d):
    return min(256, d)


def _lane_groups(cb):
    return [slice(g * LANES, (g + 1) * LANES) for g in range(cb // LANES)]


def _conv_a_fwd(za, w):
    t, d = za.shape[0], za.shape[1] // 3
    taps, cb = w.shape[0], _chan_block(d)
    nb = d // cb
    assert taps - 1 <= CONV_HALO and t % CONV_ROWS == 0

    def body(ab_ref, ac_ref, au_ref, w_ref, o_ref):
        for lanes in _lane_groups(cb):
            def emit(rows, u, s, lanes=lanes):
                o_ref[rows, lanes] = (ab_ref[rows, lanes] * s).astype(o_ref.dtype)

            _conv_pass(t, taps, w_ref, lanes, lambda rows, lanes=lanes: ac_ref[rows, lanes] * au_ref[rows, lanes], emit)

    return pl.pallas_call(
        body, grid=(nb,),
        in_specs=[pl.BlockSpec((t, cb), lambda j, o=o: (0, o * nb + j)) for o in range(3)]
        + [pl.BlockSpec((taps, cb), lambda j: (0, j))],
        out_specs=pl.BlockSpec((t, cb), lambda j: (0, j)),
        out_shape=_sds((t, d), BF16), compiler_params=_params(), name="conv_a_fwd")(za, za, za, w)


def _finish_taps(dw_ref, dw_acc, lanes, taps):
    for k in range(taps):
        dw_ref[k:k + 1, lanes] = jnp.sum(dw_acc[k], axis=0, keepdims=True)


def _conv_a_bwd(za, w, dpa):
    t, d = za.shape[0], za.shape[1] // 3
    taps, cb = w.shape[0], _chan_block(d)
    nb = d // cb

    def body(ab_ref, ac_ref, au_ref, w_ref, dy_ref, dab_ref, dac_ref, dau_ref, dw_ref, dw_acc):
        for lanes in _lane_groups(cb):
            dw_acc[...] = jnp.zeros_like(dw_acc)

            def ds(rows, lanes=lanes):
                return dy_ref[rows, lanes] * ab_ref[rows, lanes]

            def emit_s(rows, u, s, lanes=lanes):
                dab_ref[rows, lanes] = (dy_ref[rows, lanes] * s).astype(dab_ref.dtype)

            def emit_du(rows, du, lanes=lanes):
                dac_ref[rows, lanes] = (du * au_ref[rows, lanes]).astype(dac_ref.dtype)
                dau_ref[rows, lanes] = (du * ac_ref[rows, lanes]).astype(dau_ref.dtype)

            _conv_pass(t, taps, w_ref, lanes, lambda rows, lanes=lanes: ac_ref[rows, lanes] * au_ref[rows, lanes], emit_s,
                       get_dy=ds, dw_acc=dw_acc)
            _conv_pass_t(t, taps, w_ref, lanes, ds, emit_du)
            _finish_taps(dw_ref, dw_acc, lanes, taps)

    blk = pl.BlockSpec((t, cb), lambda j: (0, j))
    return pl.pallas_call(
        body, grid=(nb,),
        in_specs=[pl.BlockSpec((t, cb), lambda j, o=o: (0, o * nb + j)) for o in range(3)]
        + [pl.BlockSpec((taps, cb), lambda j: (0, j)), blk],
        out_specs=[blk, blk, blk, pl.BlockSpec((taps, cb), lambda j: (0, j))],
        out_shape=[_sds((t, d), BF16)] * 3 + [_sds((taps, d), F32)],
        scratch_shapes=[pltpu.VMEM((taps, SUBLANES, LANES), F32)],
        compiler_params=_params(), name="conv_a_bwd")(za, za, za, w, dpa)


def _conv_c_fwd(zcg, w, bias):
    t, d = zcg.shape[0], w.shape[1]
    taps, cb = w.shape[0], _chan_block(d)
    nb = d // cb
    assert taps - 1 <= CONV_HALO and t % CONV_ROWS == 0

    def body(cv_ref, cg_ref, w_ref, b_ref, o_ref):
        for lanes in _lane_groups(cb):
            def emit(rows, u, s, lanes=lanes):
                o_ref[rows, lanes] = s + b_ref[:, lanes]

            _conv_pass(t, taps, w_ref, lanes,
                       lambda rows, lanes=lanes: cv_ref[rows, lanes] * jax.nn.sigmoid(cg_ref[rows, lanes]), emit)

    return pl.pallas_call(
        body, grid=(nb,),
        in_specs=[pl.BlockSpec((t, cb), lambda j, o=o: (0, o * nb + j)) for o in range(2)]
        + [pl.BlockSpec((taps, cb), lambda j: (0, j)), pl.BlockSpec((1, cb), lambda j: (0, j))],
        out_specs=pl.BlockSpec((t, cb), lambda j: (0, j)),
        out_shape=_sds((t, d), F32), compiler_params=_params(), name="conv_c_fwd")(zcg, zcg, w, bias)


def _conv_c_bwd(zcg, w, du2):
    t, d = zcg.shape[0], w.shape[1]
    taps, cb = w.shape[0], _chan_block(d)
    nb = d // cb

    def body(cv_ref, cg_ref, w_ref, dy_ref, dcv_ref, dcg_ref, dw_ref, db_ref, dw_acc, db_acc):
        for lanes in _lane_groups(cb):
            dw_acc[...] = jnp.zeros_like(dw_acc)
            db_acc[...] = jnp.zeros_like(db_acc)

            def dy(rows, lanes=lanes):
                return dy_ref[rows, lanes]

            def emit_s(rows, u, s, lanes=lanes):
                db_acc[...] += jnp.sum(dy_ref[rows, lanes].reshape(CONV_ROWS // SUBLANES, SUBLANES, LANES), axis=0)

            def emit_du(rows, du, lanes=lanes):
                cv, sg = cv_ref[rows, lanes], jax.nn.sigmoid(cg_ref[rows, lanes])
                dcv_ref[rows, lanes] = (du * sg).astype(dcv_ref.dtype)
                dcg_ref[rows, lanes] = (du * cv * sg * (1.0 - sg)).astype(dcg_ref.dtype)

            _conv_pass(t, taps, w_ref, lanes,
                       lambda rows, lanes=lanes: cv_ref[rows, lanes] * jax.nn.sigmoid(cg_ref[rows, lanes]), emit_s,
                       get_dy=dy, dw_acc=dw_acc)
            _conv_pass_t(t, taps, w_ref, lanes, dy, emit_du)
            _finish_taps(dw_ref, dw_acc, lanes, taps)
            db_ref[:, lanes] = jnp.sum(db_acc[...], axis=0, keepdims=True)

    blk = pl.BlockSpec((t, cb), lambda j: (0, j))
    return pl.pallas_call(
        body, grid=(nb,),
        in_specs=[pl.BlockSpec((t, cb), lambda j, o=o: (0, o * nb + j)) for o in range(2)]
        + [pl.BlockSpec((taps, cb), lambda j: (0, j)), blk],
        out_specs=[blk, blk, pl.BlockSpec((taps, cb), lambda j: (0, j)), pl.BlockSpec((1, cb), lambda j: (0, j))],
        out_shape=[_sds((t, d), BF16)] * 2 + [_sds((taps, d), F32), _sds((1, d), F32)],
        scratch_shapes=[pltpu.VMEM((taps, SUBLANES, LANES), F32), pltpu.VMEM((SUBLANES, LANES), F32)],
        compiler_params=_params(), name="conv_c_bwd")(zcg, zcg, w, du2)


def _tri(tb, t, i, lower):
    r = lax.broadcasted_iota(jnp.int32, (tb, t), 0) + i * tb
    c = lax.broadcasted_iota(jnp.int32, (tb, t), 1)
    return ((r >= c) if lower else (c >= r)).astype(F32)


def _fox_cumsum(zf, bf):
    t = zf.shape[0]
    tb = min(256, t)

    def body(z_ref, b_ref, c_ref):
        logf = jax.nn.log_sigmoid(z_ref[...] + b_ref[...])
        c_ref[...] = jnp.dot(_tri(tb, t, pl.program_id(0), True), logf, precision=lax.Precision.HIGHEST,
                             preferred_element_type=F32)

    return pl.pallas_call(
        body, grid=(t // tb,), in_specs=[_full_spec(zf), _full_spec(bf)],
        out_specs=pl.BlockSpec((tb, LANES), lambda i: (i, 0)), out_shape=_sds((t, LANES), F32),
        compiler_params=_params(), name="fox_cumsum")(zf, bf)


def _fox_cumsum_bwd(zf, bf, dcc, dcr_t):
    t = zf.shape[0]
    tb = min(256, t)

    def body(z_ref, b_ref, dcc_ref, dcr_ref, dz_ref, db_ref, dc_ref):
        i = pl.program_id(0)

        @pl.when(i == 0)
        def _():
            dc_ref[...] = dcr_ref[...] + jnp.sum(dcc_ref[...], axis=0)
            db_ref[...] = jnp.zeros_like(db_ref)

        dlogf = jnp.dot(_tri(tb, t, i, False), dc_ref[...], precision=lax.Precision.HIGHEST, preferred_element_type=F32)
        dz = dlogf * jax.nn.sigmoid(-(z_ref[...] + b_ref[...]))
        dz_ref[...] = dz.astype(dz_ref.dtype)
        db_ref[...] += jnp.sum(dz, axis=0, keepdims=True)

    return pl.pallas_call(
        body, grid=(t // tb,),
        in_specs=[pl.BlockSpec((tb, LANES), lambda i: (i, 0)), _full_spec(bf), _full_spec(dcc), _full_spec(dcr_t)],
        out_specs=[pl.BlockSpec((tb, LANES), lambda i: (i, 0)), pl.BlockSpec((1, LANES), lambda i: (0, 0))],
        out_shape=[_sds((t, LANES), BF16), _sds((1, LANES), F32)],
        scratch_shapes=[pltpu.VMEM((t, LANES), F32)],
        compiler_params=_params(), name="fox_cumsum_bwd")(zf, bf, dcc, dcr_t)


def _fox_block(q, k, v, cc, cr, *, p, q0, hd):
    tq, kext = q.shape[0], k.shape[0]
    nh = LANES // hd
    lane = lax.broadcasted_iota(jnp.int32, (1, LANES), 1)
    causal = (lax.broadcasted_iota(jnp.int32, (tq, kext), 0) + q0) >= lax.broadcasted_iota(jnp.int32, (tq, kext), 1)
    kb, vb = k.astype(BF16), v.astype(BF16)
    out = jnp.zeros((tq, LANES), F32)
    for h in range(nh):
        m = (lane // hd) == h
        qh = jnp.where(m, q, 0.0).astype(BF16)
        s = lax.dot_general(qh, kb, (((1,), (1,)), ((), ())), preferred_element_type=F32) * (hd ** -0.5)
        c_q = jnp.sum(jnp.where(lane == p * nh + h, cc, 0.0), axis=1, keepdims=True)
        s = jnp.where(causal, s + (c_q - cr[h]), -1e30)
        e = jnp.exp(s - lax.stop_gradient(jnp.max(s, axis=1, keepdims=True)))
        prob = e / jnp.sum(e, axis=1, keepdims=True)
        o = jnp.dot(prob.astype(BF16), vb, preferred_element_type=F32)
        out = out + jnp.where(m, o, 0.0)
    return out


def _fox_dims(zqkv, crow):
    t, d = zqkv.shape[0], zqkv.shape[1] // 3
    hd = d // crow.shape[0]
    return t, d, hd, d // LANES, LANES // hd, min(FOX_BLOCK, t)


def _fox_specs(i, tq, nblk, nh):
    kext = (i + 1) * tq
    return [pl.BlockSpec((tq, LANES), lambda p: (i, p)), pl.BlockSpec((kext, LANES), lambda p: (0, nblk + p)),
            pl.BlockSpec((kext, LANES), lambda p: (0, 2 * nblk + p)), pl.BlockSpec((tq, LANES), lambda p: (i, 0)),
            pl.BlockSpec((nh, 1, kext), lambda p: (p, 0, 0))]


def _fox_fwd(zqkv, c, crow):
    t, d, hd, nblk, nh, tq = _fox_dims(zqkv, crow)
    o = None
    for i in range(t // tq):
        def body(q_ref, k_ref, v_ref, cc_ref, cr_ref, *rest):
            o_ref = rest[-1]
            o_ref[...] = _fox_block(q_ref[...].astype(F32), k_ref[...], v_ref[...], cc_ref[...], cr_ref[...],
                                    p=pl.program_id(0), q0=i * tq, hd=hd).astype(o_ref.dtype)

        prev = [] if o is None else [o]
        o = pl.pallas_call(
            body, grid=(nblk,),
            in_specs=_fox_specs(i, tq, nblk, nh) + [pl.BlockSpec(memory_space=pl.ANY)] * len(prev),
            out_specs=pl.BlockSpec((tq, LANES), lambda p: (i, p)), out_shape=_sds((t, d), BF16),
            input_output_aliases={5: 0} if prev else {},
            compiler_params=_params(), name=f"fox_fwd_q{i}")(zqkv, zqkv, zqkv, c, crow, *prev)
    return o


def _fox_bwd(zqkv, c, crow, do):
    t, d, hd, nblk, nh, tq = _fox_dims(zqkv, crow)
    nq = t // tq
    acc = None
    for i in reversed(range(nq)):
        kext = (i + 1) * tq
        first = acc is None

        def body(q_ref, k_ref, v_ref, cc_ref, cr_ref, do_ref, *rest, first=first, i=i):
            if first:
                dq_ref, dk_ref, dv_ref, dcc_ref, dcr_ref = rest
            else:
                dk_in, dv_in, dcr_in = rest[2:5]
                dq_ref, dk_ref, dv_ref, dcc_ref, dcr_ref = rest[5:]
            f = functools.partial(_fox_block, p=pl.program_id(0), q0=i * tq, hd=hd)
            _, vjp = jax.vjp(f, q_ref[...].astype(F32), k_ref[...].astype(F32), v_ref[...].astype(F32), cc_ref[...], cr_ref[...])
            dq, dk, dv, dcc, dcr = vjp(do_ref[...].astype(F32))
            dq_ref[...] = dq.astype(dq_ref.dtype)
            dcc_ref[...] = dcc
            if first:
                dk_ref[...], dv_ref[...], dcr_ref[...] = dk, dv, dcr
            else:
                dk_ref[...] = dk_in[...] + dk
                dv_ref[...] = dv_in[...] + dv
                dcr_ref[...] = dcr_in[...] + dcr

        qblk = pl.BlockSpec((tq, LANES), lambda p: (i, p))
        kblk = pl.BlockSpec((kext, LANES), lambda p: (0, p))
        cblk = pl.BlockSpec((None, tq, LANES), lambda p: (p, i, 0))
        rblk = pl.BlockSpec((nh, 1, kext), lambda p: (p, 0, 0))
        any_ = pl.BlockSpec(memory_space=pl.ANY)
        if first:
            prev, prev_specs, aliases = [], [], {}
        else:
            prev = [acc[0], acc[3], acc[1], acc[2], acc[4]]
            prev_specs = [any_, any_, kblk, kblk, rblk]
            aliases = {6: 0, 7: 3, 8: 1, 9: 2, 10: 4}
        acc = pl.pallas_call(
            body, grid=(nblk,),
            in_specs=_fox_specs(i, tq, nblk, nh) + [qblk] + prev_specs,
            out_specs=[qblk, kblk, kblk, cblk, rblk],
            out_shape=[_sds((t, d), BF16), _sds((t, d), F32), _sds((t, d), F32), _sds((nblk, t, LANES), F32),
                       _sds(crow.shape, F32)],
            input_output_aliases=aliases,
            compiler_params=_params(), name=f"fox_bwd_q{i}")(zqkv, zqkv, zqkv, c, crow, do, *prev)
    return acc


def _xattn_block(q, k, v):
    s = lax.dot_general(q.astype(BF16), k.astype(BF16), (((1,), (1,)), ((), ())), preferred_element_type=F32)
    s = s * (q.shape[1] ** -0.5)
    e = jnp.exp(s - lax.stop_gradient(jnp.max(s, axis=1, keepdims=True)))
    prob = e / jnp.sum(e, axis=1, keepdims=True)
    return jnp.dot(prob.astype(BF16), v.astype(BF16), preferred_element_type=F32)


def _xattn_fwd(qx, kv):
    t, dx = qx.shape
    nm, nh, tq = kv.shape[0], qx.shape[1] // LANES, min(512, qx.shape[0])

    def body(q_ref, k_ref, v_ref, o_ref):
        o_ref[...] = _xattn_block(q_ref[...], k_ref[...], v_ref[...]).astype(o_ref.dtype)

    return pl.pallas_call(
        body, grid=(nh, t // tq),
        in_specs=[pl.BlockSpec((tq, LANES), lambda h, i: (i, h)), pl.BlockSpec((nm, LANES), lambda h, i: (0, h)),
                  pl.BlockSpec((nm, LANES), lambda h, i: (0, nh + h))],
        out_specs=pl.BlockSpec((tq, LANES), lambda h, i: (i, h)), out_shape=_sds((t, dx), BF16),
        compiler_params=_params(), name="xattn_fwd")(qx, kv, kv)


def _xattn_bwd(qx, kv, do):
    t, dx = qx.shape
    nm, nh, tq = kv.shape[0], qx.shape[1] // LANES, min(512, qx.shape[0])

    def body(q_ref, k_ref, v_ref, do_ref, dq_ref, dk_ref, dv_ref):
        _, vjp = jax.vjp(_xattn_block, q_ref[...].astype(F32), k_ref[...].astype(F32), v_ref[...].astype(F32))
        dq, dk, dv = vjp(do_ref[...].astype(F32))
        dq_ref[...] = dq.astype(dq_ref.dtype)

        @pl.when(pl.program_id(1) == 0)
        def _():
            dk_ref[...] = jnp.zeros_like(dk_ref)
            dv_ref[...] = jnp.zeros_like(dv_ref)

        dk_ref[...] += dk
        dv_ref[...] += dv

    qblk = pl.BlockSpec((tq, LANES), lambda h, i: (i, h))
    kblk = pl.BlockSpec((nm, LANES), lambda h, i: (0, h))
    return pl.pallas_call(
        body, grid=(nh, t // tq),
        in_specs=[qblk, kblk, pl.BlockSpec((nm, LANES), lambda h, i: (0, nh + h)), qblk],
        out_specs=[qblk, kblk, kblk],
        out_shape=[_sds((t, dx), BF16), _sds((nm, dx), F32), _sds((nm, dx), F32)],
        compiler_params=_params(), name="xattn_bwd")(qx, kv, kv, do)


def _row_tile(rows, cols, budget=1 << 20):
    tb = rows
    while tb % (4 * SUBLANES) == 0 and tb * cols * 4 > budget:
        tb //= 2
    return tb


def _adamw(w, g, m, v):
    shape = w.shape
    cols = shape[-1]
    w2, g2, m2, v2 = (a.reshape(-1, cols) for a in (w, g, m, v))
    rows = w2.shape[0]
    tb = _row_tile(rows, cols)

    def body(w_ref, g_ref, m_ref, v_ref, d_ref, nm_ref, nv_ref):
        gg = g_ref[...]
        nm = ADAM_B1 * m_ref[...] + (1.0 - ADAM_B1) * gg
        nv = ADAM_B2 * v_ref[...] + (1.0 - ADAM_B2) * (gg * gg)
        m_hat = nm / (1.0 - ADAM_B1 ** ADAM_STEP)
        v_hat = nv / (1.0 - ADAM_B2 ** ADAM_STEP)
        d_ref[...] = -ADAM_LR * (m_hat / (jnp.sqrt(v_hat) + ADAM_EPS) + ADAM_WD * w_ref[...])
        nm_ref[...] = nm
        nv_ref[...] = nv

    blk = pl.BlockSpec((tb, cols), lambda i: (i, 0))
    out = pl.pallas_call(
        body, grid=(rows // tb,), in_specs=[blk] * 4, out_specs=[blk] * 3, out_shape=[_sds((rows, cols), F32)] * 3,
        compiler_params=_params(), name="adamw")(w2, g2, m2, v2)
    return tuple(o.reshape(shape) for o in out)


def _sum_devices(gathered, rows):
    cols = gathered.shape[1]

    def body(g_ref, o_ref):
        acc = g_ref[0:rows, :]
        for dev in range(1, N_DEV):
            acc = acc + g_ref[dev * rows:(dev + 1) * rows, :]
        o_ref[...] = acc

    return pl.pallas_call(body, out_shape=_sds((rows, cols), F32), compiler_params=_params(), name="sum_devices")(gathered)


def _place():
    x, y, c = lax.axis_index("x"), lax.axis_index("y"), lax.axis_index("c")
    chips = [(1 - x, y), (x, 1 - y), (1 - x, 1 - y)]
    return x, y, c, chips


def _chip_id(chip):
    return 2 * chip[0] + chip[1]


def _half(c, rows):
    rh = rows // 2
    return pl.ds(pl.multiple_of(c * rh, 16), rh)


_HBM = pl.BlockSpec(memory_space=pltpu.HBM)


def _allgather8(x_shard):
    m_per, n = x_shard.shape

    def body(x_ref, out_ref, send_sems, recv_sems, local_sem):
        x, y, c, chips = _place()
        me, sibling = (x, y, c), (x, y, 1 - c)

        def rows(px, py, pc):
            return out_ref.at[pl.ds((4 * px + 2 * py + pc) * m_per, m_per), :]

        def copy(k, block, to, src=None):
            return pltpu.make_async_remote_copy(
                src_ref=rows(*block) if src is None else src, dst_ref=rows(*block),
                send_sem=send_sems.at[k], recv_sem=recv_sems.at[k], device_id=to, device_id_type=MESH)

        mine = pltpu.make_async_copy(x_ref, rows(*me), local_sem)
        mine.start()
        first = [copy(0, me, sibling, src=x_ref)]
        first += [copy(1 + j, me, (*chip, c), src=x_ref) for j, chip in enumerate(chips)]
        for cp in first:
            cp.start()
        passed = [copy(4 + j, (*chip, c), sibling) for j, chip in enumerate(chips)]
        for j, chip in enumerate(chips):
            copy(1 + j, (*chip, c), me).wait_recv()
            passed[j].start()
        copy(0, sibling, me).wait_recv()
        for j, chip in enumerate(chips):
            copy(4 + j, (*chip, 1 - c), me).wait_recv()
        for cp in first + passed:
            cp.wait_send()
        mine.wait()

    return pl.pallas_call(
        body, out_shape=_sds((N_DEV * m_per, n), x_shard.dtype),
        in_specs=[pl.BlockSpec(memory_space=pltpu.VMEM)], out_specs=pl.BlockSpec(memory_space=pltpu.VMEM),
        scratch_shapes=[pltpu.SemaphoreType.DMA((7,)), pltpu.SemaphoreType.DMA((7,)), pltpu.SemaphoreType.DMA],
        compiler_params=_params(), name="allgather8")(x_shard)


def _gather_weights(ws, layer, after):
    n = len(ws)

    def body(*refs):
        w, out = refs[:n], refs[n + 1:2 * n + 1]
        send_i, recv_i, send_d, recv_d = refs[2 * n + 1:]
        x, y, c, chips = _place()
        sibling = (x, y, 1 - c)
        me_chip = 2 * x + y

        def slab(i, chip, half):
            return out[i].at[chip, _half(half, w[i].shape[1]), :]

        def over_ici(i, j, src_chip, to, src=None):
            return pltpu.make_async_remote_copy(
                src_ref=slab(i, src_chip, c) if src is None else src, dst_ref=slab(i, src_chip, c),
                send_sem=send_i.at[3 * i + j], recv_sem=recv_i.at[3 * i + j], device_id=to, device_id_type=MESH)

        def over_d2d(i, j, half):
            return pltpu.make_async_remote_copy(
                src_ref=slab(i, _chip_id(chips[j]), half), dst_ref=slab(i, _chip_id(chips[j]), half),
                send_sem=send_d.at[3 * i + j], recv_sem=recv_d.at[3 * i + j], device_id=sibling, device_id_type=MESH)

        sent = []
        for i in range(n):
            for j, chip in enumerate(chips):
                cp = over_ici(i, j, me_chip, (*chip, c), src=w[i].at[layer, _half(c, w[i].shape[1]), :])
                cp.start()
                sent.append(cp)
        for i in range(n):
            for j, chip in enumerate(chips):
                over_ici(i, j, _chip_id(chip), (x, y, c)).wait_recv()
                cp = over_d2d(i, j, c)
                cp.start()
                sent.append(cp)
        for i in range(n):
            for j in range(3):
                over_d2d(i, j, 1 - c).wait_recv()
        for cp in sent:
            cp.wait_send()

    return pl.pallas_call(
        body, in_specs=[_HBM] * n + [pl.BlockSpec(memory_space=pl.ANY)], out_specs=[_HBM] * n,
        out_shape=[_sds((N_CHIPS,) + a.shape[1:], a.dtype) for a in ws],
        scratch_shapes=[pltpu.SemaphoreType.DMA((3 * n,)) for _ in range(4)],
        compiler_params=_params(), name="gather_weights")(*ws, after)


_SEM = pl.BlockSpec(memory_space=pltpu.SEMAPHORE)
_DATAFLOW = pltpu.SideEffectType.DATAFLOW_SIDE_EFFECTING


def _gather_copies(w, land, send, recv, layer):
    x, y, c, chips = _place()
    return [pltpu.make_async_remote_copy(
        src_ref=w[i].at[layer], dst_ref=land[i].at[2 * x + y], send_sem=send.at[3 * i + j], recv_sem=recv.at[3 * i + j],
        device_id=(*chip, c), device_id_type=MESH) for i in range(len(w)) for j, chip in enumerate(chips)]


def _gather_start(ws, layer, after):
    n = len(ws)

    def body(*refs):
        w, land = refs[:n], refs[n:2 * n]
        send, recv = refs[2 * n + 1], refs[2 * n + 2]
        for cp in _gather_copies(w, land, send, recv, layer):
            cp.start()
        refs[-1][...] = jnp.zeros_like(refs[-1])

    lands = [lax.empty((N_CHIPS,) + a.shape[1:], a.dtype) for a in ws]
    operands = [pltpu.with_memory_space_constraint(a, pltpu.HBM) for a in list(ws) + lands]
    out = pl.pallas_call(
        body, name=f"gather_start_{layer}",
        out_shape=(pltpu.SemaphoreType.DMA((3 * n,)), pltpu.SemaphoreType.DMA((3 * n,)),
                   *[pltpu.HBM(a.shape, a.dtype) for a in operands], _sds((SUBLANES, LANES), F32)),
        in_specs=[_HBM] * (2 * n) + [pl.BlockSpec(memory_space=pl.ANY)],
        out_specs=(_SEM, _SEM, *[_HBM] * (2 * n), pl.BlockSpec(memory_space=pltpu.VMEM)),
        input_output_aliases={i: 2 + i for i in range(2 * n)},
        compiler_params=pltpu.CompilerParams(has_side_effects=_DATAFLOW))(*operands, after)
    return out[0], out[1], list(out[2:2 + n]), list(out[2 + n:2 + 2 * n]), out[-1]


def _gather_wait(send, recv, ws, lands, after, layer):
    n = len(ws)

    def body(*refs):
        for cp in _gather_copies(refs[:n], refs[n:2 * n], refs[2 * n], refs[2 * n + 1], layer):
            cp.wait_send()
            cp.wait_recv()

    out = pl.pallas_call(
        body, name=f"gather_wait_{layer}",
        out_shape=tuple(pltpu.HBM(a.shape, a.dtype) for a in list(ws) + list(lands)),
        in_specs=[_HBM] * (2 * n) + [_SEM, _SEM, pl.BlockSpec(memory_space=pl.ANY)], out_specs=tuple([_HBM] * (2 * n)),
        input_output_aliases={i: i for i in range(2 * n)},
        compiler_params=pltpu.CompilerParams(has_side_effects=_DATAFLOW))(*ws, *lands, send, recv, after)
    return list(out[:n]), list(out[n:])


def _swap_halves(gs):
    n = len(gs)

    def body(*refs):
        g, out = refs[:n], refs[n:2 * n]
        send, recv = refs[2 * n:]
        x, y, c, _ = _place()
        cps = [pltpu.make_async_remote_copy(
            src_ref=g[i].at[:, :, _half(1 - c, g[i].shape[2]), :], dst_ref=out[i],
            send_sem=send.at[i], recv_sem=recv.at[i], device_id=(x, y, 1 - c), device_id_type=MESH) for i in range(n)]
        for cp in cps:
            cp.start()
        for cp in cps:
            cp.wait()

    return pl.pallas_call(
        body, in_specs=[_HBM] * n, out_specs=[_HBM] * n,
        out_shape=[_sds(a.shape[:2] + (a.shape[2] // 2, a.shape[3]), a.dtype) for a in gs],
        scratch_shapes=[pltpu.SemaphoreType.DMA((n,)), pltpu.SemaphoreType.DMA((n,))],
        compiler_params=_params(), name="swap_halves")(*gs)


def _exchange_chips(ss):
    n = len(ss)

    def body(*refs):
        s, out = refs[:n], refs[n:2 * n]
        send, recv = refs[2 * n:]
        x, y, c, chips = _place()
        cps = []
        for i in range(n):
            for j, chip in enumerate(chips):
                cps.append(pltpu.make_async_remote_copy(
                    src_ref=s[i].at[:, _chip_id(chip)], dst_ref=out[i].at[j],
                    send_sem=send.at[3 * i + j], recv_sem=recv.at[3 * i + j], device_id=(*chip, c), device_id_type=MESH))
        for cp in cps:
            cp.start()
        for cp in cps:
            cp.wait()

    return pl.pallas_call(
        body, in_specs=[_HBM] * n, out_specs=[_HBM] * n,
        out_shape=[_sds((3, a.shape[0]) + a.shape[2:], a.dtype) for a in ss],
        scratch_shapes=[pltpu.SemaphoreType.DMA((3 * n,)), pltpu.SemaphoreType.DMA((3 * n,))],
        compiler_params=_params(), name="exchange_chips")(*ss)


def _exchange_copies(s, land, send, recv):
    x, y, c, chips = _place()
    return [pltpu.make_async_remote_copy(
        src_ref=s[i].at[:, _chip_id(chip)], dst_ref=land[i].at[j], send_sem=send.at[3 * i + j], recv_sem=recv.at[3 * i + j],
        device_id=(*chip, c), device_id_type=MESH) for i in range(len(s)) for j, chip in enumerate(chips)]


def _exchange_start(ss, layer):
    n = len(ss)

    def body(*refs):
        s, land = refs[:n], refs[n:2 * n]
        send, recv = refs[2 * n], refs[2 * n + 1]
        token = refs[-1]
        for cp in _exchange_copies(s, land, send, recv):
            cp.start()
        token[...] = jnp.zeros_like(token)

    lands = [lax.empty((3, a.shape[0]) + a.shape[2:], a.dtype) for a in ss]
    operands = [pltpu.with_memory_space_constraint(a, pltpu.HBM) for a in list(ss) + lands]
    out = pl.pallas_call(
        body, name=f"exchange_start_{layer}",
        out_shape=(pltpu.SemaphoreType.DMA((3 * n,)), pltpu.SemaphoreType.DMA((3 * n,)),
                   *[pltpu.HBM(a.shape, a.dtype) for a in operands], _sds((SUBLANES, LANES), F32)),
        in_specs=[_HBM] * (2 * n), out_specs=(_SEM, _SEM, *[_HBM] * (2 * n), pl.BlockSpec(memory_space=pltpu.VMEM)),
        input_output_aliases={i: 2 + i for i in range(2 * n)},
        compiler_params=pltpu.CompilerParams(has_side_effects=_DATAFLOW))(*operands)
    return out[0], out[1], list(out[2:2 + n]), list(out[2 + n:2 + 2 * n]), out[-1]


def _exchange_wait(send, recv, ss, lands, after, layer):
    n = len(ss)

    def body(*refs):
        s, land = refs[:n], refs[n:2 * n]
        for cp in _exchange_copies(s, land, refs[2 * n], refs[2 * n + 1]):
            cp.wait_send()
            cp.wait_recv()

    out = pl.pallas_call(
        body, name=f"exchange_wait_{layer}",
        out_shape=tuple(pltpu.HBM(a.shape, a.dtype) for a in list(ss) + list(lands)),
        in_specs=[_HBM] * (2 * n) + [_SEM, _SEM, pl.BlockSpec(memory_space=pl.ANY)], out_specs=tuple([_HBM] * (2 * n)),
        input_output_aliases={i: i for i in range(2 * n)},
        compiler_params=pltpu.CompilerParams(has_side_effects=_DATAFLOW))(*ss, *lands, send, recv, after)
    return list(out[:n]), list(out[n:])


def _join_halves(fs):
    n = len(fs)

    def body(*refs):
        f, out = refs[:n], refs[n:2 * n]
        send, recv = refs[2 * n:]
        x, y, c, _ = _place()

        def half(ref, which):
            return ref.at[:, _half(which, ref.shape[1]), :]

        for i in range(n):
            pltpu.make_async_remote_copy(
                src_ref=half(f[i], c), dst_ref=half(out[i], c), send_sem=send.at[i], recv_sem=recv.at[i],
                device_id=(x, y, 1 - c), device_id_type=MESH).start()
        for i in range(n):
            arrive = pltpu.make_async_remote_copy(
                src_ref=half(f[i], c), dst_ref=half(out[i], 1 - c), send_sem=send.at[i], recv_sem=recv.at[i],
                device_id=(x, y, 1 - c), device_id_type=MESH)
            arrive.wait_recv()
            arrive.wait_send()

    return pl.pallas_call(
        body, in_specs=[_HBM] * n, out_specs=[_HBM] * n, out_shape=[_sds(a.shape, a.dtype) for a in fs],
        input_output_aliases={i: i for i in range(n)},
        scratch_shapes=[pltpu.SemaphoreType.DMA((n,)), pltpu.SemaphoreType.DMA((n,))],
        compiler_params=_params(), name="join_halves")(*fs)


def _add_sibling(g, got, c_idx):
    nl, ns, r, cols = g.shape
    rh = r // 2
    tb = _row_tile(rh, cols)
    nb = rh // tb
    g3, got3 = g.reshape(nl * ns, r, cols), got.reshape(nl * ns, rh, cols)

    def body(c_ref, a_ref, b_ref, o_ref):
        o_ref[...] = (a_ref[...].astype(F32) + b_ref[...].astype(F32)).astype(o_ref.dtype)

    out = pl.pallas_call(
        body,
        grid_spec=pltpu.PrefetchScalarGridSpec(
            num_scalar_prefetch=1, grid=(nl * ns, nb),
            in_specs=[pl.BlockSpec((None, tb, cols), lambda s, i, c_ref: (s, c_ref[0] * nb + i, 0)),
                      pl.BlockSpec((None, tb, cols), lambda s, i, c_ref: (s, i, 0))],
            out_specs=pl.BlockSpec((None, tb, cols), lambda s, i, c_ref: (s, i, 0))),
        out_shape=_sds((nl * ns, rh, cols), BF16), compiler_params=_params(), name="add_sibling")(c_idx, g3, got3)
    return out.reshape(nl, ns, rh, cols)


def _add_chips(s, got, chip_idx, c_idx, layer, depth, prev):
    _, ns, rh, cols = s.shape
    tb = _row_tile(rh, cols)
    nb = rh // tb

    def body(k_ref, c_ref, a_ref, b0_ref, b1_ref, b2_ref, *rest):
        o_ref = rest[-1]
        o_ref[...] = ((a_ref[...].astype(F32) + b0_ref[...].astype(F32)) + b1_ref[...].astype(F32)) + b2_ref[...].astype(F32)

    extra = [] if prev is None else [prev]
    return pl.pallas_call(
        body,
        grid_spec=pltpu.PrefetchScalarGridSpec(
            num_scalar_prefetch=2, grid=(nb,),
            in_specs=[pl.BlockSpec((None, None, tb, cols), lambda i, k_ref, c_ref: (0, k_ref[0], i, 0))]
            + [pl.BlockSpec((None, None, tb, cols), lambda i, k_ref, c_ref, j=j: (j, 0, i, 0)) for j in range(3)]
            + [pl.BlockSpec(memory_space=pl.ANY)] * len(extra),
            out_specs=pl.BlockSpec((None, tb, cols), lambda i, k_ref, c_ref: (layer, c_ref[0] * nb + i, 0))),
        out_shape=_sds((depth, 2 * rh, cols), F32), input_output_aliases={6: 0} if extra else {},
        compiler_params=_params(), name="add_chips")(chip_idx, c_idx, s, got, got, got, *extra)


def _in_pieces(d, heads, ws):
    groups, start = [], 0
    for k in range(12):
        if k == 6:
            groups.append(("f", 0, start, heads))
            start += heads
        else:
            groups.append(("main", (k - (k > 6)) * d, start, d))
            start += d
    pieces = []
    for chip in range(N_CHIPS):
        lo, hi = chip * ws, (chip + 1) * ws
        for dest, dcol, gstart, w in groups:
            a, b = max(lo, gstart), min(hi, gstart + w)
            if a < b:
                pieces.append((chip, a - lo, dest, dcol + a - gstart, b - a))
    return pieces


def _w_in_aligned(gathered, d, heads):
    rows, ws = gathered.shape[1], gathered.shape[2]
    tb = min(256, rows)
    pieces = _in_pieces(d, heads, ws)

    def body(g_ref, m_ref, f_ref):
        f_ref[...] = jnp.zeros_like(f_ref)
        for chip, sc, dest, dc, w in pieces:
            (m_ref if dest == "main" else f_ref)[:, dc:dc + w] = g_ref[chip, :, sc:sc + w]

    return pl.pallas_call(
        body, grid=(rows // tb,), in_specs=[pl.BlockSpec((N_CHIPS, tb, ws), lambda i: (0, i, 0))],
        out_specs=[pl.BlockSpec((tb, 11 * d), lambda i: (i, 0)), pl.BlockSpec((tb, LANES), lambda i: (i, 0))],
        out_shape=[_sds((rows, 11 * d), gathered.dtype), _sds((rows, LANES), gathered.dtype)],
        compiler_params=_params(), name="w_in_aligned")(gathered)


def _w_in_shards(gm, gf, prev, l, shape, heads):
    rows, d = gm.shape[0], gm.shape[1] // 11
    ws = shape[3]
    tb = min(256, rows)
    pieces = _in_pieces(d, heads, ws)

    def body(m_ref, f_ref, *rest):
        o_ref = rest[-1]
        for chip, sc, dest, dc, w in pieces:
            o_ref[chip, :, sc:sc + w] = (m_ref if dest == "main" else f_ref)[:, dc:dc + w]

    extra = [] if prev is None else [prev]
    return pl.pallas_call(
        body, grid=(rows // tb,),
        in_specs=[pl.BlockSpec((tb, 11 * d), lambda i: (i, 0)), pl.BlockSpec((tb, LANES), lambda i: (i, 0))]
        + [pl.BlockSpec(memory_space=pl.ANY)] * len(extra),
        out_specs=pl.BlockSpec((None, N_CHIPS, tb, ws), lambda i: (l, 0, i, 0)), out_shape=_sds(shape, gm.dtype),
        input_output_aliases={2: 0} if extra else {},
        compiler_params=_params(), name="w_in_shards")(gm, gf, *extra)


def _pad_lanes(a, width=LANES):
    return jnp.pad(a, ((0, 0), (0, width - a.shape[1])))


def _cols(operand):
    return sum(p[2] for p in operand) if isinstance(operand, list) else operand.shape[1]


def _local_step(x, mem, tgt, sp, wt, dests=None, on_layer=None):
    t, d = x.shape
    depth = sp["mix_norm"].shape[0]
    heads = sp["b_forget"].shape[1]
    row = lambda a, l: a[l:l + 1]
    weights = wt if callable(wt) else (lambda l, x_in: {k: v[l] for k, v in wt.items()})
    layer_weights = []

    memn, = _rows("rms_mem", _rms_fn, [_part(mem)], [sp["mem_norm"][None]], [(d, BF16)])
    saved = []
    for l in range(depth):
        wl = weights(l, x)
        layer_weights.append(wl)
        f = _cols(wl["gu"]) // 2
        sv = {"x0": x}
        h1, = _rows("rms_mix", _rms_fn, [_part(x)], [row(sp["mix_norm"], l)], [(d, BF16)])
        wm = wl["main"]
        za = _mm("mm_in_a", h1, [_part(wm, 0, 3 * d)])
        zqkv = _mm("mm_in_qkv", h1, [_part(wm, 3 * d, 3 * d)], out_dtype=BF16)
        zcg = _mm("mm_in_cg", h1, [_part(wm, 6 * d, 5 * d)])
        zf = _mm("mm_in_f", h1, wl["f"])
        bf = _pad_lanes(row(sp["b_forget"], l))
        c = _fox_cumsum(zf, bf)
        crow = c[:, :heads].T.reshape(heads, 1, t)
        o = _fox_fwd(zqkv, c, crow)
        pa = _conv_a_fwd(za, sp["conv_a"][l])
        u2 = _conv_c_fwd(zcg, sp["conv_c"][l], row(sp["conv_c_bias"], l))
        u3, = _rows("ln_silu", _lnsilu_fn, [_part(u2)], [row(sp["ln_c_gain"], l), row(sp["ln_c_bias"], l)], [(d, BF16)])
        ya = _mm("mm_out_a", pa, wl["out_a"])
        yb = _mm("mm_out_b", o, wl["out_b"])
        yc = _mm("mm_out_c", u3, wl["out_c"])
        bg = [sp["b_gate"][l:l + 1, k * d:(k + 1) * d] for k in range(3)]
        gate_rows = [_part(zcg, (2 + k) * d, d) for k in range(3)] + [_part(ya), _part(yb), _part(yc)]
        mg, = _rows("merge", _merge_fn, gate_rows, bg, [(d, BF16)])
        x1 = _mm("mm_o", mg, wl["o"], add=x)
        h2, = _rows("rms_x", _rms_fn, [_part(x1)], [row(sp["xattn_norm"], l)], [(d, BF16)])
        qx = _mm("mm_xq", h2, wl["xq"], out_dtype=BF16)
        kv = _mm("mm_xkv", memn, wl["xkv"], out_dtype=BF16)
        ox = _xattn_fwd(qx, kv)
        x2 = _mm("mm_xo", ox, wl["xo"], add=x1)
        h3, = _rows("rms_ffn", _rms_fn, [_part(x2)], [row(sp["ffn_norm"], l)], [(d, BF16)])
        gu = _mm("mm_gu", h3, wl["gu"])
        act, = _rows("swiglu", _swiglu_fn, [_part(gu, 0, f), _part(gu, f, f)], [], [(f, BF16)])
        x3 = _mm("mm_down", act, wl["down"], add=x2)
        sv.update(h1=h1, za=za, zqkv=zqkv, zcg=zcg, zf=zf, bf=bf, c=c, crow=crow, o=o, pa=pa, u2=u2, u3=u3, ya=ya, yb=yb,
                  yc=yc, bg=bg, gate_rows=gate_rows, mg=mg, x1=x1, h2=h2, qx=qx, kv=kv, ox=ox, x2=x2, h3=h3, gu=gu, act=act)
        saved.append(sv)
        x = x3

    loss_row, dx, d_final = _loss_call(x, tgt, sp["final_norm"][None])

    gs = {k: [None] * depth for k in ("mix_norm", "b_gate", "b_forget", "conv_a", "conv_c", "conv_c_bias", "ln_c_gain",
                                      "ln_c_bias", "xattn_norm", "ffn_norm")}
    gw = {k: [None] * depth for k in ("main", "f", "out_a", "out_b", "out_c", "o", "xq", "xkv", "xo", "gu", "down")}
    def weight_grad(key, l, name, a_op, b_op):
        if dests is None or key not in dests:
            gw[key][l] = _mm(name, a_op, b_op, ta=True, out_dtype=BF16)
        else:
            shape, shards = dests[key]
            gw[key][l] = _mm(name, a_op, b_op, ta=True, out_dtype=BF16, dest=(shape, None, (0,), shards))

    dmemn = None
    for l in reversed(range(depth)):
        sv, wl = saved[l], layer_weights[l]
        dact = _mm("mm_down_dx", dx, wl["down"], tb=True)
        weight_grad("down", l, "mm_down_dw", sv["act"], dx)
        (dg, du), _ = _rows_vjp("swiglu_bwd", _swiglu_fn, [_part(sv["gu"], 0, f), _part(sv["gu"], f, f)], [], [_part(dact)],
                                [BF16, BF16], [])
        dgu = [_part(dg), _part(du)]
        dh3 = _mm("mm_gu_dx", dgu, wl["gu"], tb=True)
        weight_grad("gu", l, "mm_gu_dw", sv["h3"], dgu)
        (dx2,), (gs["ffn_norm"][l],) = _rows_vjp("rms_ffn_bwd", _rms_fn, [_part(sv["x2"])], [row(sp["ffn_norm"], l)],
                                                 [_part(dh3)], [F32], [True], adds={0: _part(dx)})
        dox = _mm("mm_xo_dx", dx2, wl["xo"], tb=True)
        weight_grad("xo", l, "mm_xo_dw", sv["ox"], dx2)
        dqx, dkx, dvx = _xattn_bwd(sv["qx"], sv["kv"], dox)
        dh2 = _mm("mm_xq_dx", dqx, wl["xq"], tb=True)
        weight_grad("xq", l, "mm_xq_dw", sv["h2"], dqx)
        dkv = [_part(dkx), _part(dvx)]
        dmemn = _mm("mm_xkv_dx", dkv, wl["xkv"], tb=True, add=dmemn)
        weight_grad("xkv", l, "mm_xkv_dw", memn, dkv)
        (dx1,), (gs["xattn_norm"][l],) = _rows_vjp("rms_x_bwd", _rms_fn, [_part(sv["x1"])], [row(sp["xattn_norm"], l)],
                                                   [_part(dh2)], [F32], [True], adds={0: _part(dx2)})
        dmg = _mm("mm_o_dx", dx1, wl["o"], tb=True)
        weight_grad("o", l, "mm_o_dw", sv["mg"], dx1)
        (dga, dgb, dgc, dya, dyb, dyc), dbg = _rows_vjp("merge_bwd", _merge_fn, sv["gate_rows"], sv["bg"], [_part(dmg)],
                                                        [BF16] * 6, [True] * 3)
        gs["b_gate"][l] = jnp.concatenate(dbg, axis=1)
        dpa = _mm("mm_out_a_dx", dya, wl["out_a"], tb=True)
        weight_grad("out_a", l, "mm_out_a_dw", sv["pa"], dya)
        do = _mm("mm_out_b_dx", dyb, wl["out_b"], tb=True)
        weight_grad("out_b", l, "mm_out_b_dw", sv["o"], dyb)
        du3 = _mm("mm_out_c_dx", dyc, wl["out_c"], tb=True)
        weight_grad("out_c", l, "mm_out_c_dw", sv["u3"], dyc)
        (du2,), (gs["ln_c_gain"][l], gs["ln_c_bias"][l]) = _rows_vjp(
            "ln_silu_bwd", _lnsilu_fn, [_part(sv["u2"])], [row(sp["ln_c_gain"], l), row(sp["ln_c_bias"], l)], [_part(du3)],
            [F32], [True, True])
        dcv, dcg, gs["conv_c"][l], gs["conv_c_bias"][l] = _conv_c_bwd(sv["zcg"], sp["conv_c"][l], du2)
        dab, dac, dau, gs["conv_a"][l] = _conv_a_bwd(sv["za"], sp["conv_a"][l], dpa)
        dq, dk, dv, dcc, dcr = _fox_bwd(sv["zqkv"], sv["c"], sv["crow"], do)
        dcr_t = _pad_lanes(dcr.reshape(heads, t).T)
        dzf, dbf = _fox_cumsum_bwd(sv["zf"], sv["bf"], dcc, dcr_t)
        gs["b_forget"][l] = dbf[:, :heads]
        dz = [_part(a) for a in (dab, dac, dau, dq, dk, dv, dcv, dcg, dga, dgb, dgc)]
        dh1f = _mm("mm_in_f_dx", dzf, wl["f"], tb=True)
        dh1 = _mm("mm_in_dx", dz, wl["main"], tb=True, add=dh1f)
        gw["main"][l] = _mm("mm_in_dw", sv["h1"], dz, ta=True, out_dtype=BF16)
        gw["f"][l] = _mm("mm_in_f_dw", sv["h1"], dzf, ta=True, out_dtype=BF16)
        (dx,), (gs["mix_norm"][l],) = _rows_vjp("rms_mix_bwd", _rms_fn, [_part(sv["x0"])], [row(sp["mix_norm"], l)],
                                                [_part(dh1)], [F32], [True], adds={0: _part(dx1)})
        if on_layer is not None:
            token = on_layer(l, {k: v[l] for k, v in gw.items()})
            if l > 0:
                dx = dx + token[0, 0]

    _, (d_mem_norm,) = _rows_vjp("rms_mem_bwd", _rms_fn, [_part(mem)], [sp["mem_norm"][None]], [_part(dmemn)], [None], [True])
    small = {k: jnp.stack([a.reshape(sp[k].shape[1:]) for a in v]) for k, v in gs.items()}
    small["mem_norm"] = d_mem_norm[0]
    small["final_norm"] = d_final[0]
    return loss_row, dx, small, gw


_BIG = ("w_in", "w_out_a", "w_out_b", "w_out_c", "w_o", "w_xq", "w_xkv", "w_xo", "w_gate_up", "w_down")
_COL_SHARDED = ("w_in", "w_xo", "w_gate_up")
_KEY = {"w_out_a": "out_a", "w_out_b": "out_b", "w_out_c": "out_c", "w_o": "o", "w_xq": "xq", "w_xkv": "xkv", "w_xo": "xo",
        "w_gate_up": "gu", "w_down": "down"}
_SMALL = ("mix_norm", "b_gate", "b_forget", "conv_c_bias", "ln_c_gain", "ln_c_bias", "xattn_norm", "ffn_norm", "mem_norm",
          "final_norm")
_SMALL_SHARDED = ("conv_a", "conv_c")


def _pack_rows(parts):
    padded = []
    for a in parts:
        pad = -a.shape[0] % SUBLANES
        padded.append(jnp.pad(a, ((0, pad), (0, 0))) if pad else a)
    return jnp.concatenate(padded, axis=0)


def _unpack_rows(packed, shapes):
    out, pos = [], 0
    for r in shapes:
        out.append(packed[pos:pos + r])
        pos += r + (-r % SUBLANES)
    return out


def kernel(x, mem, mix_norm, w_in, b_gate, b_forget, conv_a, w_out_a, w_out_b, conv_c, conv_c_bias, ln_c_gain, ln_c_bias, w_out_c, w_o, xattn_norm, mem_norm, w_xq, w_xkv, w_xo, ffn_norm, w_gate_up, w_down, final_norm, loss_target, m_mix_norm, m_w_in, m_b_gate, m_b_forget, m_conv_a, m_w_out_a, m_w_out_b, m_conv_c, m_conv_c_bias, m_ln_c_gain, m_ln_c_bias, m_w_out_c, m_w_o, m_xattn_norm, m_mem_norm, m_w_xq, m_w_xkv, m_w_xo, m_ffn_norm, m_w_gate_up, m_w_down, m_final_norm, v_mix_norm, v_w_in, v_b_gate, v_b_forget, v_conv_a, v_w_out_a, v_w_out_b, v_conv_c, v_conv_c_bias, v_ln_c_gain, v_ln_c_bias, v_w_out_c, v_w_o, v_xattn_norm, v_mem_norm, v_w_xq, v_w_xkv, v_w_xo, v_ffn_norm, v_w_gate_up, v_w_down, v_final_norm):
    args = dict(locals())
    names = ["mix_norm", "w_in", "b_gate", "b_forget", "conv_a", "w_out_a", "w_out_b", "conv_c", "conv_c_bias", "ln_c_gain",
             "ln_c_bias", "w_out_c", "w_o", "xattn_norm", "mem_norm", "w_xq", "w_xkv", "w_xo", "ffn_norm", "w_gate_up", "w_down",
             "final_norm"]
    depth, d = mix_norm.shape
    heads = b_forget.shape[1]
    cx, cy, cc = lax.axis_index("x"), lax.axis_index("y"), lax.axis_index("c")
    chip = 2 * cx + cy
    c_idx = jnp.reshape(cc, (1,)).astype(jnp.int32)
    chip_idx = jnp.reshape(chip, (1,)).astype(jnp.int32)

    conv_rows = [conv_a.shape[1] * depth, conv_c.shape[1] * depth]
    conv_pack = _pack_rows([conv_a.reshape(conv_rows[0], -1), conv_c.reshape(conv_rows[1], -1)])
    conv_all = _allgather8(conv_pack).reshape(N_CHIPS, 2, conv_pack.shape[0], conv_pack.shape[1])[:, 0]
    conv_all = conv_all.transpose(1, 0, 2).reshape(conv_pack.shape[0], d)
    conv_a_full, conv_c_full = _unpack_rows(conv_all, conv_rows)
    own = [args[n].astype(BF16) for n in _BIG]
    first = _gather_weights(own, 0, conv_all)
    started, tokens = {}, []
    for l in range(1, depth):
        send, recv, thru, lands, token = _gather_start(own, l, first[0])
        started[l] = (send, recv, thru, lands)
        tokens.append(token)

    def layer_weights(l, x_in):
        if l == 0:
            lands = first
        else:
            _, lands = _gather_wait(*started[l], x_in, l)
        full = [lax.dynamic_update_slice(g, w[l][None], (chip, 0, 0)) for g, w in zip(lands, own)]
        wl = {}
        wl["main"], wl["f"] = _w_in_aligned(full[0], d, heads)
        for n, g in zip(_BIG[1:], full[1:]):
            if n in _COL_SHARDED:
                wl[_KEY[n]] = [_part(g, lead=(j,)) for j in range(N_CHIPS)]
            else:
                wl[_KEY[n]] = g.reshape(N_CHIPS * g.shape[1], g.shape[2])
        return wl

    shard_shape = {n: (1, N_CHIPS) + args[n].shape[1:] for n in _BIG}
    dests = {_KEY[n]: (shard_shape[n], N_CHIPS) if n in _COL_SHARDED
             else ((1, N_CHIPS * args[n].shape[1], args[n].shape[2]), 1) for n in _BIG[1:]}
    sp = {n: args[n] for n in _SMALL}
    sp["conv_a"] = conv_a_full.reshape(depth, conv_a.shape[1], d)
    sp["conv_c"] = conv_c_full.reshape(depth, conv_c.shape[1], d)

    in_flight = {}

    def on_layer(l, gl):
        local = [_w_in_shards(gl["main"], gl["f"], None, 0, shard_shape["w_in"], heads)]
        local += [gl[_KEY[n]].reshape(shard_shape[n]) for n in _BIG[1:]]
        chip_sums = [_add_sibling(g, r, c_idx) for g, r in zip(local, _swap_halves(local))]
        send, recv, chip_sums, lands, token = _exchange_start(chip_sums, l)
        in_flight[l] = (send, recv, chip_sums, lands)
        return token

    x_in = x[0]
    for token in tokens:
        x_in = x_in + token[0, 0]
    loss_row, grad_x, gsmall, gw = _local_step(x_in, mem[0], loss_target[0], sp, layer_weights, dests, on_layer)
    halves = [None] * len(_BIG)
    for l in reversed(range(depth)):
        chip_sums, lands = _exchange_wait(*in_flight[l], grad_x, l)
        halves = [_add_chips(s_, r, chip_idx, c_idx, l, depth, h) for s_, r, h in zip(chip_sums, lands, halves)]
    grads = dict(zip(_BIG, _join_halves(halves)))

    def as_rows(n, a):
        return _pad_lanes(a, d) if n == "b_forget" else a.reshape(-1, d)

    pieces = [as_rows(n, gsmall[n]) for n in _SMALL]
    for n in _SMALL_SHARDED:
        pieces.append(gsmall[n].reshape(-1, d))
    pieces.append(_pad_lanes(loss_row, d))
    pack = _pack_rows(pieces)
    total = _sum_devices(_allgather8(pack), pack.shape[0])
    shapes = [p.shape[0] for p in pieces]
    summed = _unpack_rows(total, shapes)
    for n, g in zip(_SMALL, summed):
        grads[n] = g[:, :heads] if n == "b_forget" else g.reshape(args[n].shape)
    for n, g in zip(_SMALL_SHARDED, summed[len(_SMALL):]):
        g = g.reshape(args[n].shape[:2] + (d,))
        grads[n] = lax.dynamic_slice_in_dim(g, chip * args[n].shape[2], args[n].shape[2], axis=2)
    loss = summed[-1][0, 0]

    delta, new_m, new_v = {}, {}, {}
    packed = [_pack_rows([as_rows(n, src[pre + n]) for n in _SMALL])
              for src, pre in ((args, ""), (grads, ""), (args, "m_"), (args, "v_"))]
    srows = [as_rows(n, args[n]).shape[0] for n in _SMALL]
    updated = [_unpack_rows(a, srows) for a in _adamw(*packed)]
    for k, n in enumerate(_SMALL):
        delta[n], new_m[n], new_v[n] = (
            (u[k][:, :heads] if n == "b_forget" else u[k]).reshape(args[n].shape) for u in updated)
    for n in _BIG + _SMALL_SHARDED:
        delta[n], new_m[n], new_v[n] = _adamw(args[n], grads[n], args["m_" + n], args["v_" + n])

    return (loss, grad_x[None], *[grads[n] for n in names], *[delta[n] for n in names], *[new_m[n] for n in names],
            *[new_v[n] for n in names])
```

```python
import functools
import math

import jax
import jax.numpy as jnp
from jax import lax
from jax.experimental import pallas as pl
from jax.experimental.pallas import tpu as pltpu

F32, BF16 = jnp.float32, jnp.bfloat16
EPS = 1e-6
LANES = 128
SUBLANES = 8
VMEM_LIMIT = 56 * 1024 * 1024
MM_VMEM_BUDGET = 44 * 1024 * 1024
MESH = pl.DeviceIdType.MESH
N_CHIPS = 4
N_DEV = 8

ADAM_LR, ADAM_B1, ADAM_B2, ADAM_EPS, ADAM_WD, ADAM_STEP = 0.001, 0.9, 0.999, 1e-08, 0.01, 10


def _params(**kw):
    return pltpu.CompilerParams(vmem_limit_bytes=VMEM_LIMIT, **kw)


def _sds(shape, dtype):
    return jax.ShapeDtypeStruct(tuple(shape), dtype)


def _part(arr, off=0, width=None, lead=()):
    assert arr.ndim == len(lead) + 2
    return (arr, off, arr.shape[-1] - off if width is None else width, tuple(lead))


def _unit(parts):
    u = 0
    for _, off, w, _ in parts:
        u = math.gcd(u, math.gcd(off, w))
    return u


def _pick(dim, unit, cands, lane):
    for c in cands:
        if c <= dim and dim % c == 0 and unit % c == 0 and (not lane or c % LANES == 0):
            return c
    return min(dim, unit)


_COL_TILES = (1024, 1408, 512, 256, 128)
_ROW_TILES = (512, 256, 1408, 128)


def _mm(name, a, b, *, ta=False, tb=False, out_dtype=F32, add=None, dest=None):
    if not isinstance(a, list):
        a = [_part(a)]
    if not isinstance(b, list):
        b = [_part(b)]
    a_rows, a_cols = a[0][0].shape[-2], sum(p[2] for p in a)
    b_rows, b_cols = b[0][0].shape[-2], sum(p[2] for p in b)
    M, Ka = (a_cols, a_rows) if ta else (a_rows, a_cols)
    Kb, N = (b_cols, b_rows) if tb else (b_rows, b_cols)
    assert Ka == Kb, (name, Ka, Kb)
    K = Ka
    ua, ub = _unit(a), _unit(b)
    shards = 1 if dest is None else dest[3]
    tm = _pick(M, ua if ta else M, _ROW_TILES, ta)
    tn = _pick(N, math.gcd(N if tb else ub, N // shards), _COL_TILES, not tb)
    ku = math.gcd(K if ta else ua, ub if tb else K)
    tk = _pick(K, ku, _COL_TILES, (not ta) or tb)
    a_size = sum(p[0].dtype.itemsize for p in a)
    b_size = sum(p[0].dtype.itemsize for p in b)

    def vmem_estimate():
        return (2 * tm * tk * a_size + 2 * tk * tn * b_size + tm * tn * (2 * jnp.dtype(out_dtype).itemsize + 4)
                + (8 * tm * tn if add is not None else 0))

    while vmem_estimate() > MM_VMEM_BUDGET:
        if tk % 256 == 0 and tk >= tn:
            tk //= 2
        elif tn % 256 == 0:
            tn //= 2
        elif tm % 256 == 0:
            tm //= 2
        elif tk % 256 == 0:
            tk //= 2
        else:
            break
    nm, nn, nk = M // tm, N // tn, K // tk
    a_tile, b_tile = (tm if ta else tk), (tk if tb else tn)
    a_nb, b_nb = a[0][2] // a_tile, b[0][2] // b_tile

    def index(parts, nb, tile, p, row, col):
        base = parts[p][1] // tile
        if len(parts) == 1:
            return parts[p][3] + (row, base + col)
        inside = (col // nb) == p
        return parts[p][3] + (jnp.where(inside, row, 0), base + jnp.clip(col - p * nb, 0, nb - 1))

    def spec(parts, p, shape, fn):
        return pl.BlockSpec((None,) * len(parts[p][3]) + shape, fn)

    in_specs, operands = [], []
    for p in range(len(a)):
        if ta:
            in_specs.append(spec(a, p, (tk, tm), lambda mi, nj, kk, p=p: index(a, a_nb, a_tile, p, kk, mi)))
        else:
            in_specs.append(spec(a, p, (tm, tk), lambda mi, nj, kk, p=p: index(a, a_nb, a_tile, p, mi, kk)))
        operands.append(a[p][0])
    for p in range(len(b)):
        if tb:
            in_specs.append(spec(b, p, (tn, tk), lambda mi, nj, kk, p=p: index(b, b_nb, b_tile, p, nj, kk)))
        else:
            in_specs.append(spec(b, p, (tk, tn), lambda mi, nj, kk, p=p: index(b, b_nb, b_tile, p, kk, nj)))
        operands.append(b[p][0])
    if add is not None:
        in_specs.append(pl.BlockSpec((tm, tn), lambda mi, nj, kk: (mi, nj)))
        operands.append(add)
    aliases = {}
    if dest is None:
        out_spec = pl.BlockSpec((tm, tn), lambda mi, nj, kk: (mi, nj))
        out_shape = _sds((M, N), out_dtype)
    else:
        d_shape, d_prev, d_lead, _ = dest
        nbs = N // shards // tn
        if shards == 1:
            out_spec = pl.BlockSpec((None,) * len(d_lead) + (tm, tn), lambda mi, nj, kk: tuple(d_lead) + (mi, nj))
        else:
            out_spec = pl.BlockSpec((None,) * (len(d_lead) + 1) + (tm, tn),
                                    lambda mi, nj, kk: tuple(d_lead) + (nj // nbs, mi, nj % nbs))
        out_shape = _sds(d_shape, out_dtype)
        if d_prev is not None:
            aliases = {len(operands): 0}
            in_specs.append(pl.BlockSpec(memory_space=pl.ANY))
            operands.append(d_prev)
    n_extra = 1 if aliases else 0
    dims = (((0 if ta else 1,), (1 if tb else 0,)), ((), ()))
    na, nb_ = len(a), len(b)

    def body(*refs):
        a_refs, b_refs = refs[:na], refs[na:na + nb_]
        pos = na + nb_
        add_ref = None
        if add is not None:
            add_ref = refs[pos]
            pos += 1
        pos += n_extra
        o_ref = refs[pos]
        acc_ref = refs[pos + 1] if nk > 1 else None
        mi, nj, kk = pl.program_id(0), pl.program_id(1), pl.program_id(2)
        a_col = mi if ta else kk
        b_col = kk if tb else nj

        def finish(r):
            if add_ref is not None:
                r = r + add_ref[...].astype(F32)
            o_ref[...] = r.astype(o_ref.dtype)

        def step(a_ref, b_ref):
            r = lax.dot_general(a_ref[...].astype(BF16), b_ref[...].astype(BF16), dims, preferred_element_type=F32)
            if nk == 1:
                finish(r)
            else:
                @pl.when(kk == 0)
                def _():
                    acc_ref[...] = r

                @pl.when(kk > 0)
                def _():
                    acc_ref[...] += r

        for pa in range(na):
            for pb in range(nb_):
                if na * nb_ == 1:
                    step(a_refs[pa], b_refs[pb])
                else:
                    cond = jnp.logical_and(a_col // a_nb == pa, b_col // b_nb == pb)
                    pl.when(cond)(functools.partial(step, a_refs[pa], b_refs[pb]))
        if nk > 1:
            @pl.when(kk == nk - 1)
            def _():
                finish(acc_ref[...])

    return pl.pallas_call(
        body, grid=(nm, nn, nk), in_specs=in_specs, out_specs=out_spec, out_shape=out_shape, input_output_aliases=aliases,
        scratch_shapes=[pltpu.VMEM((tm, tn), F32)] if nk > 1 else [],
        compiler_params=_params(dimension_semantics=("parallel", "parallel", "arbitrary")),
        name=name)(*operands)


def _row_spec(tb, part):
    _, off, w, _ = part
    return pl.BlockSpec((tb, w), lambda i, o=off // w: (i, o))


def _full_spec(arr):
    return pl.BlockSpec(arr.shape, lambda i: (0,) * arr.ndim)


def _rows(name, fn, rows, pars, outs, tb=256):
    n = rows[0][0].shape[0]
    tb = min(tb, n)
    nr, npar = len(rows), len(pars)

    def body(*refs):
        r = [ref[...].astype(F32) for ref in refs[:nr]]
        p = [ref[...].astype(F32) for ref in refs[nr:nr + npar]]
        res = fn(*r, *p)
        for o_ref, v in zip(refs[nr + npar:], res):
            o_ref[...] = v.astype(o_ref.dtype)

    out = pl.pallas_call(
        body, grid=(n // tb,),
        in_specs=[_row_spec(tb, p) for p in rows] + [_full_spec(p) for p in pars],
        out_specs=[pl.BlockSpec((tb, w), lambda i: (i, 0)) for w, _ in outs],
        out_shape=[_sds((n, w), dt) for w, dt in outs],
        compiler_params=_params(), name=name)(*[p[0] for p in rows], *pars)
    return out


def _rows_vjp(name, fn, rows, pars, cots, row_grads, par_grads, adds=None, tb=256):
    adds = adds or {}
    n = rows[0][0].shape[0]
    tb = min(tb, n)
    nr, npar, nc = len(rows), len(pars), len(cots)
    add_keys = sorted(adds)
    rg = [j for j, dt in enumerate(row_grads) if dt is not None]
    pg = [j for j, f in enumerate(par_grads) if f]

    def body(*refs):
        pos = 0
        r = [ref[...].astype(F32) for ref in refs[pos:pos + nr]]
        pos += nr
        p = [ref[...].astype(F32) for ref in refs[pos:pos + npar]]
        pos += npar
        ct = tuple(ref[...].astype(F32) for ref in refs[pos:pos + nc])
        pos += nc
        add_v = {k: ref[...].astype(F32) for k, ref in zip(add_keys, refs[pos:pos + len(add_keys)])}
        pos += len(add_keys)
        _, vjp = jax.vjp(lambda *args: tuple(fn(*args)), *r, *p)
        g = vjp(ct)
        for j in rg:
            v = g[j]
            if j in add_v:
                v = v + add_v[j]
            refs[pos][...] = v.astype(refs[pos].dtype)
            pos += 1
        first = pl.program_id(0) == 0
        for j in pg:
            ref = refs[pos]
            pos += 1

            @pl.when(first)
            def _(ref=ref):
                ref[...] = jnp.zeros_like(ref)

            ref[...] += g[nr + j]

    in_specs = ([_row_spec(tb, p) for p in rows] + [_full_spec(p) for p in pars] + [_row_spec(tb, p) for p in cots]
                + [_row_spec(tb, adds[k]) for k in add_keys])
    out_specs = ([pl.BlockSpec((tb, rows[j][2]), lambda i: (i, 0)) for j in rg] + [_full_spec(pars[j]) for j in pg])
    out_shape = ([_sds((n, rows[j][2]), row_grads[j]) for j in rg] + [_sds(pars[j].shape, F32) for j in pg])
    out = pl.pallas_call(
        body, grid=(n // tb,), in_specs=in_specs, out_specs=out_specs, out_shape=out_shape,
        compiler_params=_params(), name=name)(
            *[p[0] for p in rows], *pars, *[p[0] for p in cots], *[adds[k][0] for k in add_keys])
    return out[:len(rg)], out[len(rg):]


def _rms_fn(x, g):
    return (x * lax.rsqrt(jnp.mean(x * x, axis=-1, keepdims=True) + EPS) * g,)


def _lnsilu_fn(u, g, b):
    mu = jnp.mean(u, axis=-1, keepdims=True)
    xc = u - mu
    y = xc * lax.rsqrt(jnp.mean(xc * xc, axis=-1, keepdims=True) + EPS) * g + b
    return (y * jax.nn.sigmoid(y),)


def _merge_fn(ga, gb, gc, ya, yb, yc, ba, bb, bc):
    return (jax.nn.sigmoid(ga + ba) * ya + jax.nn.sigmoid(gb + bb) * yb + jax.nn.sigmoid(gc + bc) * yc,)


def _swiglu_fn(g, u):
    return (g * jax.nn.sigmoid(g) * u,)


def _loss_call(x, tgt, gain):
    n, d = x.shape
    tb = min(256, n)

    def body(x_ref, t_ref, g_ref, l_ref, dx_ref, dg_ref):
        (y,), vjp = jax.vjp(lambda a, b: tuple(_rms_fn(a, b)), x_ref[...], g_ref[...])
        e = y - t_ref[...]
        part = 0.5 * jnp.sum(jnp.mean(e * e, axis=-1, keepdims=True))
        dx, dg = vjp((e * (1.0 / d),))
        dx_ref[...] = dx

        @pl.when(pl.program_id(0) == 0)
        def _():
            l_ref[...] = jnp.zeros_like(l_ref)
            dg_ref[...] = jnp.zeros_like(dg_ref)

        l_ref[...] += jnp.full(l_ref.shape, part, F32)
        dg_ref[...] += dg

    return pl.pallas_call(
        body, grid=(n // tb,),
        in_specs=[pl.BlockSpec((tb, d), lambda i: (i, 0)), pl.BlockSpec((tb, d), lambda i: (i, 0)), _full_spec(gain)],
        out_specs=[pl.BlockSpec((1, LANES), lambda i: (0, 0)), pl.BlockSpec((tb, d), lambda i: (i, 0)), _full_spec(gain)],
        out_shape=[_sds((1, LANES), F32), _sds((n, d), F32), _sds(gain.shape, F32)],
        compiler_params=_params(), name="loss_head")(x, tgt, gain)


FOX_BLOCK = 256
CONV_ROWS = 64
CONV_HALO = 32


def _chunk(i):
    return pl.ds(pl.multiple_of(i * CONV_ROWS, CONV_ROWS), CONV_ROWS)


def _delayed(ext, s):
    return (ext if s == 0 else pltpu.roll(ext, s, 0))[CONV_HALO:]


def _advanced(ext, s):
    return (ext if s == 0 else pltpu.roll(ext, ext.shape[0] - s, 0))[:CONV_ROWS]


def _conv_pass(t, taps, w_ref, lanes, get_u, emit, get_dy=None, dw_acc=None):
    def body(i, carry):
        rows = _chunk(i)
        u = get_u(rows)
        ext = jnp.concatenate([carry, u], axis=0)
        dy = None if get_dy is None else get_dy(rows)
        acc = None
        for k in range(taps):
            sh = _delayed(ext, taps - 1 - k)
            term = w_ref[k:k + 1, lanes] * sh
            acc = term if acc is None else acc + term
            if dy is not None:
                dw_acc[k] += jnp.sum((dy * sh).reshape(CONV_ROWS // SUBLANES, SUBLANES, LANES), axis=0)
        emit(rows, u, acc)
        return u[CONV_ROWS - CONV_HALO:]

    lax.fori_loop(0, t // CONV_ROWS, body, jnp.zeros((CONV_HALO, LANES), F32))


def _conv_pass_t(t, taps, w_ref, lanes, get_dy, emit):
    n = t // CONV_ROWS

    def body(j, carry):
        rows = _chunk(n - 1 - j)
        dy = get_dy(rows)
        ext = jnp.concatenate([dy, carry], axis=0)
        acc = None
        for k in range(taps):
            term = w_ref[k:k + 1, lanes] * _advanced(ext, taps - 1 - k)
            acc = term if acc is None else acc + term
        emit(rows, acc)
        return dy[:CONV_HALO]

    lax.fori_loop(0, n, body, jnp.zeros((CONV_HALO, LANES), F32))


def _chan_block(d):
    return min(256, d)


def _lane_groups(cb):
    return [slice(g * LANES, (g + 1) * LANES) for g in range(cb // LANES)]


def _conv_a_fwd(za, w):
    t, d = za.shape[0], za.shape[1] // 3
    taps, cb = w.shape[0], _chan_block(d)
    nb = d // cb
    assert taps - 1 <= CONV_HALO and t % CONV_ROWS == 0

    def body(ab_ref, ac_ref, au_ref, w_ref, o_ref):
        for lanes in _lane_groups(cb):
            def emit(rows, u, s, lanes=lanes):
                o_ref[rows, lanes] = (ab_ref[rows, lanes] * s).astype(o_ref.dtype)

            _conv_pass(t, taps, w_ref, lanes, lambda rows, lanes=lanes: ac_ref[rows, lanes] * au_ref[rows, lanes], emit)

    return pl.pallas_call(
        body, grid=(nb,),
        in_specs=[pl.BlockSpec((t, cb), lambda j, o=o: (0, o * nb + j)) for o in range(3)]
        + [pl.BlockSpec((taps, cb), lambda j: (0, j))],
        out_specs=pl.BlockSpec((t, cb), lambda j: (0, j)),
        out_shape=_sds((t, d), BF16), compiler_params=_params(), name="conv_a_fwd")(za, za, za, w)


def _finish_taps(dw_ref, dw_acc, lanes, taps):
    for k in range(taps):
        dw_ref[k:k + 1, lanes] = jnp.sum(dw_acc[k], axis=0, keepdims=True)


def _conv_a_bwd(za, w, dpa):
    t, d = za.shape[0], za.shape[1] // 3
    taps, cb = w.shape[0], _chan_block(d)
    nb = d // cb

    def body(ab_ref, ac_ref, au_ref, w_ref, dy_ref, dab_ref, dac_ref, dau_ref, dw_ref, dw_acc):
        for lanes in _lane_groups(cb):
            dw_acc[...] = jnp.zeros_like(dw_acc)

            def ds(rows, lanes=lanes):
                return dy_ref[rows, lanes] * ab_ref[rows, lanes]

            def emit_s(rows, u, s, lanes=lanes):
                dab_ref[rows, lanes] = (dy_ref[rows, lanes] * s).astype(dab_ref.dtype)

            def emit_du(rows, du, lanes=lanes):
                dac_ref[rows, lanes] = (du * au_ref[rows, lanes]).astype(dac_ref.dtype)
                dau_ref[rows, lanes] = (du * ac_ref[rows, lanes]).astype(dau_ref.dtype)

            _conv_pass(t, taps, w_ref, lanes, lambda rows, lanes=lanes: ac_ref[rows, lanes] * au_ref[rows, lanes], emit_s,
                       get_dy=ds, dw_acc=dw_acc)
            _conv_pass_t(t, taps, w_ref, lanes, ds, emit_du)
            _finish_taps(dw_ref, dw_acc, lanes, taps)

    blk = pl.BlockSpec((t, cb), lambda j: (0, j))
    return pl.pallas_call(
        body, grid=(nb,),
        in_specs=[pl.BlockSpec((t, cb), lambda j, o=o: (0, o * nb + j)) for o in range(3)]
        + [pl.BlockSpec((taps, cb), lambda j: (0, j)), blk],
        out_specs=[blk, blk, blk, pl.BlockSpec((taps, cb), lambda j: (0, j))],
        out_shape=[_sds((t, d), BF16)] * 3 + [_sds((taps, d), F32)],
        scratch_shapes=[pltpu.VMEM((taps, SUBLANES, LANES), F32)],
        compiler_params=_params(), name="conv_a_bwd")(za, za, za, w, dpa)


def _conv_c_fwd(zcg, w, bias):
    t, d = zcg.shape[0], w.shape[1]
    taps, cb = w.shape[0], _chan_block(d)
    nb = d // cb
    assert taps - 1 <= CONV_HALO and t % CONV_ROWS == 0

    def body(cv_ref, cg_ref, w_ref, b_ref, o_ref):
        for lanes in _lane_groups(cb):
            def emit(rows, u, s, lanes=lanes):
                o_ref[rows, lanes] = s + b_ref[:, lanes]

            _conv_pass(t, taps, w_ref, lanes,
                       lambda rows, lanes=lanes: cv_ref[rows, lanes] * jax.nn.sigmoid(cg_ref[rows, lanes]), emit)

    return pl.pallas_call(
        body, grid=(nb,),
        in_specs=[pl.BlockSpec((t, cb), lambda j, o=o: (0, o * nb + j)) for o in range(2)]
        + [pl.BlockSpec((taps, cb), lambda j: (0, j)), pl.BlockSpec((1, cb), lambda j: (0, j))],
        out_specs=pl.BlockSpec((t, cb), lambda j: (0, j)),
        out_shape=_sds((t, d), F32), compiler_params=_params(), name="conv_c_fwd")(zcg, zcg, w, bias)


def _conv_c_bwd(zcg, w, du2):
    t, d = zcg.shape[0], w.shape[1]
    taps, cb = w.shape[0], _chan_block(d)
    nb = d // cb

    def body(cv_ref, cg_ref, w_ref, dy_ref, dcv_ref, dcg_ref, dw_ref, db_ref, dw_acc, db_acc):
        for lanes in _lane_groups(cb):
            dw_acc[...] = jnp.zeros_like(dw_acc)
            db_acc[...] = jnp.zeros_like(db_acc)

            def dy(rows, lanes=lanes):
                return dy_ref[rows, lanes]

            def emit_s(rows, u, s, lanes=lanes):
                db_acc[...] += jnp.sum(dy_ref[rows, lanes].reshape(CONV_ROWS // SUBLANES, SUBLANES, LANES), axis=0)

            def emit_du(rows, du, lanes=lanes):
                cv, sg = cv_ref[rows, lanes], jax.nn.sigmoid(cg_ref[rows, lanes])
                dcv_ref[rows, lanes] = (du * sg).astype(dcv_ref.dtype)
                dcg_ref[rows, lanes] = (du * cv * sg * (1.0 - sg)).astype(dcg_ref.dtype)

            _conv_pass(t, taps, w_ref, lanes,
                       lambda rows, lanes=lanes: cv_ref[rows, lanes] * jax.nn.sigmoid(cg_ref[rows, lanes]), emit_s,
                       get_dy=dy, dw_acc=dw_acc)
            _conv_pass_t(t, taps, w_ref, lanes, dy, emit_du)
            _finish_taps(dw_ref, dw_acc, lanes, taps)
            db_ref[:, lanes] = jnp.sum(db_acc[...], axis=0, keepdims=True)

    blk = pl.BlockSpec((t, cb), lambda j: (0, j))
    return pl.pallas_call(
        body, grid=(nb,),
        in_specs=[pl.BlockSpec((t, cb), lambda j, o=o: (0, o * nb + j)) for o in range(2)]
        + [pl.BlockSpec((taps, cb), lambda j: (0, j)), blk],
        out_specs=[blk, blk, pl.BlockSpec((taps, cb), lambda j: (0, j)), pl.BlockSpec((1, cb), lambda j: (0, j))],
        out_shape=[_sds((t, d), BF16)] * 2 + [_sds((taps, d), F32), _sds((1, d), F32)],
        scratch_shapes=[pltpu.VMEM((taps, SUBLANES, LANES), F32), pltpu.VMEM((SUBLANES, LANES), F32)],
        compiler_params=_params(), name="conv_c_bwd")(zcg, zcg, w, du2)


def _tri(tb, t, i, lower):
    r = lax.broadcasted_iota(jnp.int32, (tb, t), 0) + i * tb
    c = lax.broadcasted_iota(jnp.int32, (tb, t), 1)
    return ((r >= c) if lower else (c >= r)).astype(F32)


def _fox_cumsum(zf, bf):
    t = zf.shape[0]
    tb = min(256, t)

    def body(z_ref, b_ref, c_ref):
        logf = jax.nn.log_sigmoid(z_ref[...] + b_ref[...])
        c_ref[...] = jnp.dot(_tri(tb, t, pl.program_id(0), True), logf, precision=lax.Precision.HIGHEST,
                             preferred_element_type=F32)

    return pl.pallas_call(
        body, grid=(t // tb,), in_specs=[_full_spec(zf), _full_spec(bf)],
        out_specs=pl.BlockSpec((tb, LANES), lambda i: (i, 0)), out_shape=_sds((t, LANES), F32),
        compiler_params=_params(), name="fox_cumsum")(zf, bf)


def _fox_cumsum_bwd(zf, bf, dc_t):
    t = zf.shape[0]
    tb = min(256, t)

    def body(z_ref, b_ref, dc_ref, dz_ref, db_ref):
        i = pl.program_id(0)

        @pl.when(i == 0)
        def _():
            db_ref[...] = jnp.zeros_like(db_ref)

        dlogf = jnp.dot(_tri(tb, t, i, False), dc_ref[...], precision=lax.Precision.HIGHEST, preferred_element_type=F32)
        dz = dlogf * jax.nn.sigmoid(-(z_ref[...] + b_ref[...]))
        dz_ref[...] = dz.astype(dz_ref.dtype)
        db_ref[...] += jnp.sum(dz, axis=0, keepdims=True)

    return pl.pallas_call(
        body, grid=(t // tb,),
        in_specs=[pl.BlockSpec((tb, LANES), lambda i: (i, 0)), _full_spec(bf), _full_spec(dc_t)],
        out_specs=[pl.BlockSpec((tb, LANES), lambda i: (i, 0)), pl.BlockSpec((1, LANES), lambda i: (0, 0))],
        out_shape=[_sds((t, LANES), BF16), _sds((1, LANES), F32)],
        compiler_params=_params(), name="fox_cumsum_bwd")(zf, bf, dc_t)


def _fox_block(q, k, v, cr, *, q0, hd):
    tq, kext = q.shape[0], k.shape[0]
    nh = LANES // hd
    lane = lax.broadcasted_iota(jnp.int32, (1, LANES), 1)
    causal = (lax.broadcasted_iota(jnp.int32, (tq, kext), 0) + q0) >= lax.broadcasted_iota(jnp.int32, (tq, kext), 1)
    kb, vb = k.astype(BF16), v.astype(BF16)
    out = jnp.zeros((tq, LANES), F32)
    for h in range(nh):
        m = (lane // hd) == h
        qh = jnp.where(m, q * (hd ** -0.5), 0.0).astype(BF16)
        s = lax.dot_general(qh, kb, (((1,), (1,)), ((), ())), preferred_element_type=F32)
        c_k = cr[h] - lax.stop_gradient(cr[h][:, q0:q0 + 1])
        s = jnp.where(causal, s - c_k, -1e30)
        e = jnp.exp(s - lax.stop_gradient(jnp.max(s, axis=1, keepdims=True)))
        prob = e / jnp.sum(e, axis=1, keepdims=True)
        o = jnp.dot(prob.astype(BF16), vb, preferred_element_type=F32)
        out = out + jnp.where(m, o, 0.0)
    return out


def _fox_dims(zqkv, crow):
    t, d = zqkv.shape[0], zqkv.shape[1] // 3
    hd = d // crow.shape[0]
    assert math.log2(hd) % 2 == 0
    return t, d, hd, d // LANES, LANES // hd, min(FOX_BLOCK, t)


def _fox_specs(i, tq, nblk, nh):
    kext = (i + 1) * tq
    return [pl.BlockSpec((tq, LANES), lambda p: (i, p)), pl.BlockSpec((kext, LANES), lambda p: (0, nblk + p)),
            pl.BlockSpec((kext, LANES), lambda p: (0, 2 * nblk + p)), pl.BlockSpec((nh, 1, kext), lambda p: (p, 0, 0))]


def _fox_fwd(zqkv, crow):
    t, d, hd, nblk, nh, tq = _fox_dims(zqkv, crow)
    o = None
    for i in range(t // tq):
        def body(q_ref, k_ref, v_ref, cr_ref, *rest, i=i):
            o_ref = rest[-1]
            o_ref[...] = _fox_block(q_ref[...].astype(F32), k_ref[...], v_ref[...], cr_ref[...],
                                    q0=i * tq, hd=hd).astype(o_ref.dtype)

        prev = [] if o is None else [o]
        o = pl.pallas_call(
            body, grid=(nblk,),
            in_specs=_fox_specs(i, tq, nblk, nh) + [pl.BlockSpec(memory_space=pl.ANY)] * len(prev),
            out_specs=pl.BlockSpec((tq, LANES), lambda p, i=i: (i, p)), out_shape=_sds((t, d), BF16),
            input_output_aliases={4: 0} if prev else {},
            compiler_params=_params(), name=f"fox_fwd_q{i}")(zqkv, zqkv, zqkv, crow, *prev)
    return o


def _fox_bwd(zqkv, crow, do):
    t, d, hd, nblk, nh, tq = _fox_dims(zqkv, crow)
    nq = t // tq
    acc = None
    for i in reversed(range(nq)):
        kext = (i + 1) * tq
        first = acc is None

        def body(q_ref, k_ref, v_ref, cr_ref, do_ref, *rest, first=first, i=i):
            if first:
                dq_ref, dk_ref, dv_ref, dcr_ref = rest
            else:
                dk_in, dv_in, dcr_in = rest[1:4]
                dq_ref, dk_ref, dv_ref, dcr_ref = rest[4:]
            f = functools.partial(_fox_block, q0=i * tq, hd=hd)
            _, vjp = jax.vjp(f, q_ref[...].astype(F32), k_ref[...].astype(F32), v_ref[...].astype(F32), cr_ref[...])
            dq, dk, dv, dcr = vjp(do_ref[...].astype(F32))
            dq_ref[...] = dq.astype(dq_ref.dtype)
            if first:
                dk_ref[...], dv_ref[...], dcr_ref[...] = dk, dv, dcr
            else:
                dk_ref[...] = dk_in[...] + dk
                dv_ref[...] = dv_in[...] + dv
                dcr_ref[...] = dcr_in[...] + dcr

        qblk = pl.BlockSpec((tq, LANES), lambda p, i=i: (i, p))
        kblk = pl.BlockSpec((kext, LANES), lambda p: (0, p))
        rblk = pl.BlockSpec((nh, 1, kext), lambda p: (p, 0, 0))
        if first:
            prev, prev_specs, aliases = [], [], {}
        else:
            prev, prev_specs = list(acc), [pl.BlockSpec(memory_space=pl.ANY), kblk, kblk, rblk]
            aliases = {5: 0, 6: 1, 7: 2, 8: 3}
        acc = pl.pallas_call(
            body, grid=(nblk,),
            in_specs=_fox_specs(i, tq, nblk, nh) + [qblk] + prev_specs,
            out_specs=[qblk, kblk, kblk, rblk],
            out_shape=[_sds((t, d), BF16), _sds((t, d), F32), _sds((t, d), F32), _sds(crow.shape, F32)],
            input_output_aliases=aliases,
            compiler_params=_params(), name=f"fox_bwd_q{i}")(zqkv, zqkv, zqkv, crow, do, *prev)
    return acc


def _xattn_block(q, k, v):
    s = lax.dot_general(q.astype(BF16), k.astype(BF16), (((1,), (1,)), ((), ())), preferred_element_type=F32)
    s = s * (q.shape[1] ** -0.5)
    e = jnp.exp(s - lax.stop_gradient(jnp.max(s, axis=1, keepdims=True)))
    prob = e / jnp.sum(e, axis=1, keepdims=True)
    return jnp.dot(prob.astype(BF16), v.astype(BF16), preferred_element_type=F32)


def _xattn_fwd(qx, kv):
    t, dx = qx.shape
    nm, nh, tq = kv.shape[0], qx.shape[1] // LANES, min(512, qx.shape[0])

    def body(q_ref, k_ref, v_ref, o_ref):
        o_ref[...] = _xattn_block(q_ref[...], k_ref[...], v_ref[...]).astype(o_ref.dtype)

    return pl.pallas_call(
        body, grid=(nh, t // tq),
        in_specs=[pl.BlockSpec((tq, LANES), lambda h, i: (i, h)), pl.BlockSpec((nm, LANES), lambda h, i: (0, h)),
                  pl.BlockSpec((nm, LANES), lambda h, i: (0, nh + h))],
        out_specs=pl.BlockSpec((tq, LANES), lambda h, i: (i, h)), out_shape=_sds((t, dx), BF16),
        compiler_params=_params(), name="xattn_fwd")(qx, kv, kv)


def _xattn_bwd(qx, kv, do):
    t, dx = qx.shape
    nm, nh, tq = kv.shape[0], qx.shape[1] // LANES, min(512, qx.shape[0])

    def body(q_ref, k_ref, v_ref, do_ref, dq_ref, dk_ref, dv_ref):
        _, vjp = jax.vjp(_xattn_block, q_ref[...].astype(F32), k_ref[...].astype(F32), v_ref[...].astype(F32))
        dq, dk, dv = vjp(do_ref[...].astype(F32))
        dq_ref[...] = dq.astype(dq_ref.dtype)

        @pl.when(pl.program_id(1) == 0)
        def _():
            dk_ref[...] = jnp.zeros_like(dk_ref)
            dv_ref[...] = jnp.zeros_like(dv_ref)

        dk_ref[...] += dk
        dv_ref[...] += dv

    qblk = pl.BlockSpec((tq, LANES), lambda h, i: (i, h))
    kblk = pl.BlockSpec((nm, LANES), lambda h, i: (0, h))
    return pl.pallas_call(
        body, grid=(nh, t // tq),
        in_specs=[qblk, kblk, pl.BlockSpec((nm, LANES), lambda h, i: (0, nh + h)), qblk],
        out_specs=[qblk, kblk, kblk],
        out_shape=[_sds((t, dx), BF16), _sds((nm, dx), F32), _sds((nm, dx), F32)],
        compiler_params=_params(), name="xattn_bwd")(qx, kv, kv, do)


def _row_tile(rows, cols, budget=1 << 20):
    step = 2 * SUBLANES
    best = None
    for tb in range(step, rows + 1, step):
        if rows % tb == 0 and tb * cols * 4 <= budget:
            best = tb
    return rows if best is None else best


def _adamw(w, g, m, v):
    shape = w.shape
    cols = shape[-1]
    w2, g2, m2, v2 = (a.reshape(-1, cols) for a in (w, g, m, v))
    rows = w2.shape[0]
    tb = _row_tile(rows, cols)

    def body(w_ref, g_ref, m_ref, v_ref, d_ref, nm_ref, nv_ref):
        gg = g_ref[...]
        nm = ADAM_B1 * m_ref[...] + (1.0 - ADAM_B1) * gg
        nv = ADAM_B2 * v_ref[...] + (1.0 - ADAM_B2) * (gg * gg)
        m_hat = nm / (1.0 - ADAM_B1 ** ADAM_STEP)
        v_hat = nv / (1.0 - ADAM_B2 ** ADAM_STEP)
        d_ref[...] = -ADAM_LR * (m_hat / (jnp.sqrt(v_hat) + ADAM_EPS) + ADAM_WD * w_ref[...])
        nm_ref[...] = nm
        nv_ref[...] = nv

    blk = pl.BlockSpec((tb, cols), lambda i: (i, 0))
    out = pl.pallas_call(
        body, grid=(rows // tb,), in_specs=[blk] * 4, out_specs=[blk] * 3, out_shape=[_sds((rows, cols), F32)] * 3,
        compiler_params=_params(), name="adamw")(w2, g2, m2, v2)
    return tuple(o.reshape(shape) for o in out)


def _sum_devices(gathered, rows):
    cols = gathered.shape[1]

    def body(g_ref, o_ref):
        acc = g_ref[0:rows, :]
        for dev in range(1, N_DEV):
            acc = acc + g_ref[dev * rows:(dev + 1) * rows, :]
        o_ref[...] = acc

    return pl.pallas_call(body, out_shape=_sds((rows, cols), F32), compiler_params=_params(), name="sum_devices")(gathered)


def _place():
    x, y, c = lax.axis_index("x"), lax.axis_index("y"), lax.axis_index("c")
    chips = [(1 - x, y), (x, 1 - y), (1 - x, 1 - y)]
    return x, y, c, chips


def _chip_id(chip):
    return 2 * chip[0] + chip[1]


def _half(c, rows):
    rh = rows // 2
    return pl.ds(pl.multiple_of(c * rh, 16), rh)


_HBM = pl.BlockSpec(memory_space=pltpu.HBM)


def _allgather8(x_shard):
    m_per, n = x_shard.shape

    def body(x_ref, out_ref, send_sems, recv_sems, local_sem):
        x, y, c, chips = _place()
        me, sibling = (x, y, c), (x, y, 1 - c)

        def rows(px, py, pc):
            return out_ref.at[pl.ds((4 * px + 2 * py + pc) * m_per, m_per), :]

        def copy(k, block, to, src=None):
            return pltpu.make_async_remote_copy(
                src_ref=rows(*block) if src is None else src, dst_ref=rows(*block),
                send_sem=send_sems.at[k], recv_sem=recv_sems.at[k], device_id=to, device_id_type=MESH)

        mine = pltpu.make_async_copy(x_ref, rows(*me), local_sem)
        mine.start()
        first = [copy(0, me, sibling, src=x_ref)]
        first += [copy(1 + j, me, (*chip, c), src=x_ref) for j, chip in enumerate(chips)]
        for cp in first:
            cp.start()
        passed = [copy(4 + j, (*chip, c), sibling) for j, chip in enumerate(chips)]
        for j, chip in enumerate(chips):
            copy(1 + j, (*chip, c), me).wait_recv()
            passed[j].start()
        copy(0, sibling, me).wait_recv()
        for j, chip in enumerate(chips):
            copy(4 + j, (*chip, 1 - c), me).wait_recv()
        for cp in first + passed:
            cp.wait_send()
        mine.wait()

    return pl.pallas_call(
        body, out_shape=_sds((N_DEV * m_per, n), x_shard.dtype),
        in_specs=[pl.BlockSpec(memory_space=pltpu.VMEM)], out_specs=pl.BlockSpec(memory_space=pltpu.VMEM),
        scratch_shapes=[pltpu.SemaphoreType.DMA((7,)), pltpu.SemaphoreType.DMA((7,)), pltpu.SemaphoreType.DMA],
        compiler_params=_params(), name="allgather8")(x_shard)


def _gather_weights(ws, layer, after):
    n = len(ws)

    def body(*refs):
        w, out = refs[:n], refs[n + 1:2 * n + 1]
        send_i, recv_i, send_d, recv_d = refs[2 * n + 1:]
        x, y, c, chips = _place()
        sibling = (x, y, 1 - c)
        me_chip = 2 * x + y

        def slab(i, chip, half):
            return out[i].at[chip, _half(half, w[i].shape[1]), :]

        def over_ici(i, j, src_chip, to, src=None):
            return pltpu.make_async_remote_copy(
                src_ref=slab(i, src_chip, c) if src is None else src, dst_ref=slab(i, src_chip, c),
                send_sem=send_i.at[3 * i + j], recv_sem=recv_i.at[3 * i + j], device_id=to, device_id_type=MESH)

        def over_d2d(i, j, half):
            return pltpu.make_async_remote_copy(
                src_ref=slab(i, _chip_id(chips[j]), half), dst_ref=slab(i, _chip_id(chips[j]), half),
                send_sem=send_d.at[3 * i + j], recv_sem=recv_d.at[3 * i + j], device_id=sibling, device_id_type=MESH)

        sent = []
        for i in range(n):
            for j, chip in enumerate(chips):
                cp = over_ici(i, j, me_chip, (*chip, c), src=w[i].at[layer, _half(c, w[i].shape[1]), :])
                cp.start()
                sent.append(cp)
        for i in range(n):
            for j, chip in enumerate(chips):
                over_ici(i, j, _chip_id(chip), (x, y, c)).wait_recv()
                cp = over_d2d(i, j, c)
                cp.start()
                sent.append(cp)
        for i in range(n):
            for j in range(3):
                over_d2d(i, j, 1 - c).wait_recv()
        for cp in sent:
            cp.wait_send()

    return pl.pallas_call(
        body, in_specs=[_HBM] * n + [pl.BlockSpec(memory_space=pl.ANY)], out_specs=[_HBM] * n,
        out_shape=[_sds((N_CHIPS,) + a.shape[1:], a.dtype) for a in ws],
        scratch_shapes=[pltpu.SemaphoreType.DMA((3 * n,)) for _ in range(4)],
        compiler_params=_params(), name="gather_weights")(*ws, after)


_SEM = pl.BlockSpec(memory_space=pltpu.SEMAPHORE)
_DATAFLOW = pltpu.SideEffectType.DATAFLOW_SIDE_EFFECTING


def _gather_copies(w, land, send, recv, layer):
    x, y, c, chips = _place()
    return [pltpu.make_async_remote_copy(
        src_ref=w[i].at[layer], dst_ref=land[i].at[2 * x + y], send_sem=send.at[3 * i + j], recv_sem=recv.at[3 * i + j],
        device_id=(*chip, c), device_id_type=MESH) for i in range(len(w)) for j, chip in enumerate(chips)]


def _gather_start(ws, layer, after):
    n = len(ws)

    def body(*refs):
        w, land = refs[:n], refs[n:2 * n]
        send, recv = refs[2 * n + 1], refs[2 * n + 2]
        for cp in _gather_copies(w, land, send, recv, layer):
            cp.start()
        refs[-1][...] = jnp.zeros_like(refs[-1])

    lands = [lax.empty((N_CHIPS,) + a.shape[1:], a.dtype) for a in ws]
    operands = [pltpu.with_memory_space_constraint(a, pltpu.HBM) for a in list(ws) + lands]
    out = pl.pallas_call(
        body, name=f"gather_start_{layer}",
        out_shape=(pltpu.SemaphoreType.DMA((3 * n,)), pltpu.SemaphoreType.DMA((3 * n,)),
                   *[pltpu.HBM(a.shape, a.dtype) for a in operands], _sds((SUBLANES, LANES), F32)),
        in_specs=[_HBM] * (2 * n) + [pl.BlockSpec(memory_space=pl.ANY)],
        out_specs=(_SEM, _SEM, *[_HBM] * (2 * n), pl.BlockSpec(memory_space=pltpu.VMEM)),
        input_output_aliases={i: 2 + i for i in range(2 * n)},
        compiler_params=pltpu.CompilerParams(has_side_effects=_DATAFLOW))(*operands, after)
    return out[0], out[1], list(out[2:2 + n]), list(out[2 + n:2 + 2 * n]), out[-1]


def _gather_wait(send, recv, ws, lands, after, layer):
    n = len(ws)

    def body(*refs):
        for cp in _gather_copies(refs[:n], refs[n:2 * n], refs[2 * n], refs[2 * n + 1], layer):
            cp.wait_send()
            cp.wait_recv()

    out = pl.pallas_call(
        body, name=f"gather_wait_{layer}",
        out_shape=tuple(pltpu.HBM(a.shape, a.dtype) for a in list(ws) + list(lands)),
        in_specs=[_HBM] * (2 * n) + [_SEM, _SEM, pl.BlockSpec(memory_space=pl.ANY)], out_specs=tuple([_HBM] * (2 * n)),
        input_output_aliases={i: i for i in range(2 * n)},
        compiler_params=pltpu.CompilerParams(has_side_effects=_DATAFLOW))(*ws, *lands, send, recv, after)
    return list(out[:n]), list(out[n:])


def _swap_halves(gs):
    n = len(gs)

    def body(*refs):
        g, out = refs[:n], refs[n:2 * n]
        send, recv = refs[2 * n:]
        x, y, c, _ = _place()
        cps = [pltpu.make_async_remote_copy(
            src_ref=g[i].at[:, :, _half(1 - c, g[i].shape[2]), :], dst_ref=out[i],
            send_sem=send.at[i], recv_sem=recv.at[i], device_id=(x, y, 1 - c), device_id_type=MESH) for i in range(n)]
        for cp in cps:
            cp.start()
        for cp in cps:
            cp.wait()

    return pl.pallas_call(
        body, in_specs=[_HBM] * n, out_specs=[_HBM] * n,
        out_shape=[_sds(a.shape[:2] + (a.shape[2] // 2, a.shape[3]), a.dtype) for a in gs],
        scratch_shapes=[pltpu.SemaphoreType.DMA((n,)), pltpu.SemaphoreType.DMA((n,))],
        compiler_params=_params(), name="swap_halves")(*gs)


def _exchange_chips(ss):
    n = len(ss)

    def body(*refs):
        s, out = refs[:n], refs[n:2 * n]
        send, recv = refs[2 * n:]
        x, y, c, chips = _place()
        cps = []
        for i in range(n):
            for j, chip in enumerate(chips):
                cps.append(pltpu.make_async_remote_copy(
                    src_ref=s[i].at[:, _chip_id(chip)], dst_ref=out[i].at[j],
                    send_sem=send.at[3 * i + j], recv_sem=recv.at[3 * i + j], device_id=(*chip, c), device_id_type=MESH))
        for cp in cps:
            cp.start()
        for cp in cps:
            cp.wait()

    return pl.pallas_call(
        body, in_specs=[_HBM] * n, out_specs=[_HBM] * n,
        out_shape=[_sds((3, a.shape[0]) + a.shape[2:], a.dtype) for a in ss],
        scratch_shapes=[pltpu.SemaphoreType.DMA((3 * n,)), pltpu.SemaphoreType.DMA((3 * n,))],
        compiler_params=_params(), name="exchange_chips")(*ss)


def _exchange_copies(s, land, send, recv):
    x, y, c, chips = _place()
    return [pltpu.make_async_remote_copy(
        src_ref=s[i].at[:, _chip_id(chip)], dst_ref=land[i].at[j], send_sem=send.at[3 * i + j], recv_sem=recv.at[3 * i + j],
        device_id=(*chip, c), device_id_type=MESH) for i in range(len(s)) for j, chip in enumerate(chips)]


def _exchange_start(ss, layer):
    n = len(ss)

    def body(*refs):
        s, land = refs[:n], refs[n:2 * n]
        send, recv = refs[2 * n], refs[2 * n + 1]
        token = refs[-1]
        for cp in _exchange_copies(s, land, send, recv):
            cp.start()
        token[...] = jnp.zeros_like(token)

    lands = [lax.empty((3, a.shape[0]) + a.shape[2:], a.dtype) for a in ss]
    operands = [pltpu.with_memory_space_constraint(a, pltpu.HBM) for a in list(ss) + lands]
    out = pl.pallas_call(
        body, name=f"exchange_start_{layer}",
        out_shape=(pltpu.SemaphoreType.DMA((3 * n,)), pltpu.SemaphoreType.DMA((3 * n,)),
                   *[pltpu.HBM(a.shape, a.dtype) for a in operands], _sds((SUBLANES, LANES), F32)),
        in_specs=[_HBM] * (2 * n), out_specs=(_SEM, _SEM, *[_HBM] * (2 * n), pl.BlockSpec(memory_space=pltpu.VMEM)),
        input_output_aliases={i: 2 + i for i in range(2 * n)},
        compiler_params=pltpu.CompilerParams(has_side_effects=_DATAFLOW))(*operands)
    return out[0], out[1], list(out[2:2 + n]), list(out[2 + n:2 + 2 * n]), out[-1]


def _exchange_wait(send, recv, ss, lands, after, layer):
    n = len(ss)

    def body(*refs):
        s, land = refs[:n], refs[n:2 * n]
        for cp in _exchange_copies(s, land, refs[2 * n], refs[2 * n + 1]):
            cp.wait_send()
            cp.wait_recv()

    out = pl.pallas_call(
        body, name=f"exchange_wait_{layer}",
        out_shape=tuple(pltpu.HBM(a.shape, a.dtype) for a in list(ss) + list(lands)),
        in_specs=[_HBM] * (2 * n) + [_SEM, _SEM, pl.BlockSpec(memory_space=pl.ANY)], out_specs=tuple([_HBM] * (2 * n)),
        input_output_aliases={i: i for i in range(2 * n)},
        compiler_params=pltpu.CompilerParams(has_side_effects=_DATAFLOW))(*ss, *lands, send, recv, after)
    return list(out[:n]), list(out[n:])


def _join_halves(fs):
    n = len(fs)

    def body(*refs):
        f, out = refs[:n], refs[n:2 * n]
        send, recv = refs[2 * n:]
        x, y, c, _ = _place()

        def half(ref, which):
            return ref.at[:, _half(which, ref.shape[1]), :]

        for i in range(n):
            pltpu.make_async_remote_copy(
                src_ref=half(f[i], c), dst_ref=half(out[i], c), send_sem=send.at[i], recv_sem=recv.at[i],
                device_id=(x, y, 1 - c), device_id_type=MESH).start()
        for i in range(n):
            arrive = pltpu.make_async_remote_copy(
                src_ref=half(f[i], c), dst_ref=half(out[i], 1 - c), send_sem=send.at[i], recv_sem=recv.at[i],
                device_id=(x, y, 1 - c), device_id_type=MESH)
            arrive.wait_recv()
            arrive.wait_send()

    return pl.pallas_call(
        body, in_specs=[_HBM] * n, out_specs=[_HBM] * n, out_shape=[_sds(a.shape, a.dtype) for a in fs],
        input_output_aliases={i: i for i in range(n)},
        scratch_shapes=[pltpu.SemaphoreType.DMA((n,)), pltpu.SemaphoreType.DMA((n,))],
        compiler_params=_params(), name="join_halves")(*fs)


def _add_sibling(g, got, c_idx):
    nl, ns, r, cols = g.shape
    rh = r // 2
    tb = _row_tile(rh, cols)
    nb = rh // tb
    g3, got3 = g.reshape(nl * ns, r, cols), got.reshape(nl * ns, rh, cols)

    def body(c_ref, a_ref, b_ref, o_ref):
        o_ref[...] = (a_ref[...].astype(F32) + b_ref[...].astype(F32)).astype(o_ref.dtype)

    out = pl.pallas_call(
        body,
        grid_spec=pltpu.PrefetchScalarGridSpec(
            num_scalar_prefetch=1, grid=(nl * ns, nb),
            in_specs=[pl.BlockSpec((None, tb, cols), lambda s, i, c_ref: (s, c_ref[0] * nb + i, 0)),
                      pl.BlockSpec((None, tb, cols), lambda s, i, c_ref: (s, i, 0))],
            out_specs=pl.BlockSpec((None, tb, cols), lambda s, i, c_ref: (s, i, 0))),
        out_shape=_sds((nl * ns, rh, cols), BF16), compiler_params=_params(), name="add_sibling")(c_idx, g3, got3)
    return out.reshape(nl, ns, rh, cols)


def _add_chips(s, got, chip_idx, c_idx, layer, depth, prev):
    _, ns, rh, cols = s.shape
    tb = _row_tile(rh, cols)
    nb = rh // tb

    def body(k_ref, c_ref, a_ref, b0_ref, b1_ref, b2_ref, *rest):
        o_ref = rest[-1]
        o_ref[...] = ((a_ref[...].astype(F32) + b0_ref[...].astype(F32)) + b1_ref[...].astype(F32)) + b2_ref[...].astype(F32)

    extra = [] if prev is None else [prev]
    return pl.pallas_call(
        body,
        grid_spec=pltpu.PrefetchScalarGridSpec(
            num_scalar_prefetch=2, grid=(nb,),
            in_specs=[pl.BlockSpec((None, None, tb, cols), lambda i, k_ref, c_ref: (0, k_ref[0], i, 0))]
            + [pl.BlockSpec((None, None, tb, cols), lambda i, k_ref, c_ref, j=j: (j, 0, i, 0)) for j in range(3)]
            + [pl.BlockSpec(memory_space=pl.ANY)] * len(extra),
            out_specs=pl.BlockSpec((None, tb, cols), lambda i, k_ref, c_ref: (layer, c_ref[0] * nb + i, 0))),
        out_shape=_sds((depth, 2 * rh, cols), F32), input_output_aliases={6: 0} if extra else {},
        compiler_params=_params(), name="add_chips")(chip_idx, c_idx, s, got, got, got, *extra)


def _in_pieces(d, heads, ws):
    groups, start = [], 0
    for k in range(12):
        if k == 6:
            groups.append(("f", 0, start, heads))
            start += heads
        else:
            groups.append(("main", (k - (k > 6)) * d, start, d))
            start += d
    pieces = []
    for chip in range(N_CHIPS):
        lo, hi = chip * ws, (chip + 1) * ws
        for dest, dcol, gstart, w in groups:
            a, b = max(lo, gstart), min(hi, gstart + w)
            if a < b:
                pieces.append((chip, a - lo, dest, dcol + a - gstart, b - a))
    return pieces


def _w_in_aligned(gathered, d, heads):
    rows, ws = gathered.shape[1], gathered.shape[2]
    tb = min(256, rows)
    pieces = _in_pieces(d, heads, ws)

    def body(g_ref, m_ref, f_ref):
        f_ref[...] = jnp.zeros_like(f_ref)
        for chip, sc, dest, dc, w in pieces:
            (m_ref if dest == "main" else f_ref)[:, dc:dc + w] = g_ref[chip, :, sc:sc + w]

    return pl.pallas_call(
        body, grid=(rows // tb,), in_specs=[pl.BlockSpec((N_CHIPS, tb, ws), lambda i: (0, i, 0))],
        out_specs=[pl.BlockSpec((tb, 11 * d), lambda i: (i, 0)), pl.BlockSpec((tb, LANES), lambda i: (i, 0))],
        out_shape=[_sds((rows, 11 * d), gathered.dtype), _sds((rows, LANES), gathered.dtype)],
        compiler_params=_params(), name="w_in_aligned")(gathered)


def _w_in_shards(gm, gf, prev, l, shape, heads):
    rows, d = gm.shape[0], gm.shape[1] // 11
    ws = shape[3]
    tb = min(256, rows)
    pieces = _in_pieces(d, heads, ws)

    def body(m_ref, f_ref, *rest):
        o_ref = rest[-1]
        for chip, sc, dest, dc, w in pieces:
            o_ref[chip, :, sc:sc + w] = (m_ref if dest == "main" else f_ref)[:, dc:dc + w]

    extra = [] if prev is None else [prev]
    return pl.pallas_call(
        body, grid=(rows // tb,),
        in_specs=[pl.BlockSpec((tb, 11 * d), lambda i: (i, 0)), pl.BlockSpec((tb, LANES), lambda i: (i, 0))]
        + [pl.BlockSpec(memory_space=pl.ANY)] * len(extra),
        out_specs=pl.BlockSpec((None, N_CHIPS, tb, ws), lambda i: (l, 0, i, 0)), out_shape=_sds(shape, gm.dtype),
        input_output_aliases={2: 0} if extra else {},
        compiler_params=_params(), name="w_in_shards")(gm, gf, *extra)


def _pad_lanes(a, width=LANES):
    return jnp.pad(a, ((0, 0), (0, width - a.shape[1])))


def _cols(operand):
    return sum(p[2] for p in operand) if isinstance(operand, list) else operand.shape[1]


def _local_step(x, mem, tgt, sp, wt, dests=None, on_layer=None):
    t, d = x.shape
    depth = sp["mix_norm"].shape[0]
    heads = sp["b_forget"].shape[1]
    row = lambda a, l: a[l:l + 1]
    weights = wt if callable(wt) else (lambda l, x_in: {k: v[l] for k, v in wt.items()})
    layer_weights = []

    memn, = _rows("rms_mem", _rms_fn, [_part(mem)], [sp["mem_norm"][None]], [(d, BF16)])
    saved = []
    for l in range(depth):
        wl = weights(l, x)
        layer_weights.append(wl)
        f = _cols(wl["gu"]) // 2
        sv = {"x0": x}
        h1, = _rows("rms_mix", _rms_fn, [_part(x)], [row(sp["mix_norm"], l)], [(d, BF16)])
        wm = wl["main"]
        za = _mm("mm_in_a", h1, [_part(wm, 0, 3 * d)])
        zqkv = _mm("mm_in_qkv", h1, [_part(wm, 3 * d, 3 * d)], out_dtype=BF16)
        zcg = _mm("mm_in_cg", h1, [_part(wm, 6 * d, 5 * d)])
        zf = _mm("mm_in_f", h1, wl["f"])
        bf = _pad_lanes(row(sp["b_forget"], l))
        c = _fox_cumsum(zf, bf)
        crow = c[:, :heads].T.reshape(heads, 1, t)
        o = _fox_fwd(zqkv, crow)
        pa = _conv_a_fwd(za, sp["conv_a"][l])
        u2 = _conv_c_fwd(zcg, sp["conv_c"][l], row(sp["conv_c_bias"], l))
        u3, = _rows("ln_silu", _lnsilu_fn, [_part(u2)], [row(sp["ln_c_gain"], l), row(sp["ln_c_bias"], l)], [(d, BF16)])
        ya = _mm("mm_out_a", pa, wl["out_a"])
        yb = _mm("mm_out_b", o, wl["out_b"])
        yc = _mm("mm_out_c", u3, wl["out_c"])
        bg = [sp["b_gate"][l:l + 1, k * d:(k + 1) * d] for k in range(3)]
        gate_rows = [_part(zcg, (2 + k) * d, d) for k in range(3)] + [_part(ya), _part(yb), _part(yc)]
        mg, = _rows("merge", _merge_fn, gate_rows, bg, [(d, BF16)])
        x1 = _mm("mm_o", mg, wl["o"], add=x)
        h2, = _rows("rms_x", _rms_fn, [_part(x1)], [row(sp["xattn_norm"], l)], [(d, BF16)])
        qx = _mm("mm_xq", h2, wl["xq"], out_dtype=BF16)
        kv = _mm("mm_xkv", memn, wl["xkv"], out_dtype=BF16)
        ox = _xattn_fwd(qx, kv)
        x2 = _mm("mm_xo", ox, wl["xo"], add=x1)
        h3, = _rows("rms_ffn", _rms_fn, [_part(x2)], [row(sp["ffn_norm"], l)], [(d, BF16)])
        gu = _mm("mm_gu", h3, wl["gu"])
        act, = _rows("swiglu", _swiglu_fn, [_part(gu, 0, f), _part(gu, f, f)], [], [(f, BF16)])
        x3 = _mm("mm_down", act, wl["down"], add=x2)
        sv.update(h1=h1, za=za, zqkv=zqkv, zcg=zcg, zf=zf, bf=bf, crow=crow, o=o, pa=pa, u2=u2, u3=u3, ya=ya, yb=yb,
                  yc=yc, bg=bg, gate_rows=gate_rows, mg=mg, x1=x1, h2=h2, qx=qx, kv=kv, ox=ox, x2=x2, h3=h3, gu=gu, act=act)
        saved.append(sv)
        x = x3

    loss_row, dx, d_final = _loss_call(x, tgt, sp["final_norm"][None])

    gs = {k: [None] * depth for k in ("mix_norm", "b_gate", "b_forget", "conv_a", "conv_c", "conv_c_bias", "ln_c_gain",
                                      "ln_c_bias", "xattn_norm", "ffn_norm")}
    gw = {k: [None] * depth for k in ("main", "f", "out_a", "out_b", "out_c", "o", "xq", "xkv", "xo", "gu", "down")}
    def weight_grad(key, l, name, a_op, b_op):
        if dests is None or key not in dests:
            gw[key][l] = _mm(name, a_op, b_op, ta=True, out_dtype=BF16)
        else:
            shape, shards = dests[key]
            gw[key][l] = _mm(name, a_op, b_op, ta=True, out_dtype=BF16, dest=(shape, None, (0,), shards))

    dmemn = None
    for l in reversed(range(depth)):
        sv, wl = saved[l], layer_weights[l]
        dact = _mm("mm_down_dx", dx, wl["down"], tb=True)
        weight_grad("down", l, "mm_down_dw", sv["act"], dx)
        (dg, du), _ = _rows_vjp("swiglu_bwd", _swiglu_fn, [_part(sv["gu"], 0, f), _part(sv["gu"], f, f)], [], [_part(dact)],
                                [BF16, BF16], [])
        dgu = [_part(dg), _part(du)]
        dh3 = _mm("mm_gu_dx", dgu, wl["gu"], tb=True)
        weight_grad("gu", l, "mm_gu_dw", sv["h3"], dgu)
        (dx2,), (gs["ffn_norm"][l],) = _rows_vjp("rms_ffn_bwd", _rms_fn, [_part(sv["x2"])], [row(sp["ffn_norm"], l)],
                                                 [_part(dh3)], [F32], [True], adds={0: _part(dx)})
        dox = _mm("mm_xo_dx", dx2, wl["xo"], tb=True)
        weight_grad("xo", l, "mm_xo_dw", sv["ox"], dx2)
        dqx, dkx, dvx = _xattn_bwd(sv["qx"], sv["kv"], dox)
        dh2 = _mm("mm_xq_dx", dqx, wl["xq"], tb=True)
        weight_grad("xq", l, "mm_xq_dw", sv["h2"], dqx)
        dkv = [_part(dkx), _part(dvx)]
        dmemn = _mm("mm_xkv_dx", dkv, wl["xkv"], tb=True, add=dmemn)
        weight_grad("xkv", l, "mm_xkv_dw", memn, dkv)
        (dx1,), (gs["xattn_norm"][l],) = _rows_vjp("rms_x_bwd", _rms_fn, [_part(sv["x1"])], [row(sp["xattn_norm"], l)],
                                                   [_part(dh2)], [F32], [True], adds={0: _part(dx2)})
        dmg = _mm("mm_o_dx", dx1, wl["o"], tb=True)
        weight_grad("o", l, "mm_o_dw", sv["mg"], dx1)
        (dga, dgb, dgc, dya, dyb, dyc), dbg = _rows_vjp("merge_bwd", _merge_fn, sv["gate_rows"], sv["bg"], [_part(dmg)],
                                                        [BF16] * 6, [True] * 3)
        gs["b_gate"][l] = jnp.concatenate(dbg, axis=1)
        dpa = _mm("mm_out_a_dx", dya, wl["out_a"], tb=True)
        weight_grad("out_a", l, "mm_out_a_dw", sv["pa"], dya)
        do = _mm("mm_out_b_dx", dyb, wl["out_b"], tb=True)
        weight_grad("out_b", l, "mm_out_b_dw", sv["o"], dyb)
        du3 = _mm("mm_out_c_dx", dyc, wl["out_c"], tb=True)
        weight_grad("out_c", l, "mm_out_c_dw", sv["u3"], dyc)
        (du2,), (gs["ln_c_gain"][l], gs["ln_c_bias"][l]) = _rows_vjp(
            "ln_silu_bwd", _lnsilu_fn, [_part(sv["u2"])], [row(sp["ln_c_gain"], l), row(sp["ln_c_bias"], l)], [_part(du3)],
            [F32], [True, True])
        dcv, dcg, gs["conv_c"][l], gs["conv_c_bias"][l] = _conv_c_bwd(sv["zcg"], sp["conv_c"][l], du2)
        dab, dac, dau, gs["conv_a"][l] = _conv_a_bwd(sv["za"], sp["conv_a"][l], dpa)
        dq, dk, dv, dcr = _fox_bwd(sv["zqkv"], sv["crow"], do)
        dzf, dbf = _fox_cumsum_bwd(sv["zf"], sv["bf"], _pad_lanes(dcr.reshape(heads, t).T))
        gs["b_forget"][l] = dbf[:, :heads]
        dz = [_part(a) for a in (dab, dac, dau, dq, dk, dv, dcv, dcg, dga, dgb, dgc)]
        dh1f = _mm("mm_in_f_dx", dzf, wl["f"], tb=True)
        dh1 = _mm("mm_in_dx", dz, wl["main"], tb=True, add=dh1f)
        gw["main"][l] = _mm("mm_in_dw", sv["h1"], dz, ta=True, out_dtype=BF16)
        gw["f"][l] = _mm("mm_in_f_dw", sv["h1"], dzf, ta=True, out_dtype=BF16)
        (dx,), (gs["mix_norm"][l],) = _rows_vjp("rms_mix_bwd", _rms_fn, [_part(sv["x0"])], [row(sp["mix_norm"], l)],
                                                [_part(dh1)], [F32], [True], adds={0: _part(dx1)})
        if on_layer is not None:
            token = on_layer(l, {k: v[l] for k, v in gw.items()})
            if l > 0:
                dx = dx + token[0, 0]

    _, (d_mem_norm,) = _rows_vjp("rms_mem_bwd", _rms_fn, [_part(mem)], [sp["mem_norm"][None]], [_part(dmemn)], [None], [True])
    small = {k: jnp.stack([a.reshape(sp[k].shape[1:]) for a in v]) for k, v in gs.items()}
    small["mem_norm"] = d_mem_norm[0]
    small["final_norm"] = d_final[0]
    return loss_row, dx, small, gw


_BIG = ("w_in", "w_out_a", "w_out_b", "w_out_c", "w_o", "w_xq", "w_xkv", "w_xo", "w_gate_up", "w_down")
_COL_SHARDED = ("w_in", "w_xo", "w_gate_up")
_KEY = {"w_out_a": "out_a", "w_out_b": "out_b", "w_out_c": "out_c", "w_o": "o", "w_xq": "xq", "w_xkv": "xkv", "w_xo": "xo",
        "w_gate_up": "gu", "w_down": "down"}
_SMALL = ("mix_norm", "b_gate", "b_forget", "conv_c_bias", "ln_c_gain", "ln_c_bias", "xattn_norm", "ffn_norm", "mem_norm",
          "final_norm")
_SMALL_SHARDED = ("conv_a", "conv_c")


def _pack_rows(parts):
    padded = []
    for a in parts:
        pad = -a.shape[0] % SUBLANES
        padded.append(jnp.pad(a, ((0, pad), (0, 0))) if pad else a)
    return jnp.concatenate(padded, axis=0)


def _unpack_rows(packed, shapes):
    out, pos = [], 0
    for r in shapes:
        out.append(packed[pos:pos + r])
        pos += r + (-r % SUBLANES)
    return out


def kernel(x, mem, mix_norm, w_in, b_gate, b_forget, conv_a, w_out_a, w_out_b, conv_c, conv_c_bias, ln_c_gain, ln_c_bias, w_out_c, w_o, xattn_norm, mem_norm, w_xq, w_xkv, w_xo, ffn_norm, w_gate_up, w_down, final_norm, loss_target, m_mix_norm, m_w_in, m_b_gate, m_b_forget, m_conv_a, m_w_out_a, m_w_out_b, m_conv_c, m_conv_c_bias, m_ln_c_gain, m_ln_c_bias, m_w_out_c, m_w_o, m_xattn_norm, m_mem_norm, m_w_xq, m_w_xkv, m_w_xo, m_ffn_norm, m_w_gate_up, m_w_down, m_final_norm, v_mix_norm, v_w_in, v_b_gate, v_b_forget, v_conv_a, v_w_out_a, v_w_out_b, v_conv_c, v_conv_c_bias, v_ln_c_gain, v_ln_c_bias, v_w_out_c, v_w_o, v_xattn_norm, v_mem_norm, v_w_xq, v_w_xkv, v_w_xo, v_ffn_norm, v_w_gate_up, v_w_down, v_final_norm):
    args = dict(locals())
    names = ["mix_norm", "w_in", "b_gate", "b_forget", "conv_a", "w_out_a", "w_out_b", "conv_c", "conv_c_bias", "ln_c_gain",
             "ln_c_bias", "w_out_c", "w_o", "xattn_norm", "mem_norm", "w_xq", "w_xkv", "w_xo", "ffn_norm", "w_gate_up", "w_down",
             "final_norm"]
    depth, d = mix_norm.shape
    heads = b_forget.shape[1]
    cx, cy, cc = lax.axis_index("x"), lax.axis_index("y"), lax.axis_index("c")
    chip = 2 * cx + cy
    c_idx = jnp.reshape(cc, (1,)).astype(jnp.int32)
    chip_idx = jnp.reshape(chip, (1,)).astype(jnp.int32)

    conv_rows = [conv_a.shape[1] * depth, conv_c.shape[1] * depth]
    conv_pack = _pack_rows([conv_a.reshape(conv_rows[0], -1), conv_c.reshape(conv_rows[1], -1)])
    conv_all = _allgather8(conv_pack).reshape(N_CHIPS, 2, conv_pack.shape[0], conv_pack.shape[1])[:, 0]
    conv_all = conv_all.transpose(1, 0, 2).reshape(conv_pack.shape[0], d)
    conv_a_full, conv_c_full = _unpack_rows(conv_all, conv_rows)
    own = [args[n].astype(BF16) for n in _BIG]
    first = _gather_weights(own, 0, conv_all)
    started, tokens = {}, []
    for l in range(1, depth):
        send, recv, thru, lands, token = _gather_start(own, l, first[0])
        started[l] = (send, recv, thru, lands)
        tokens.append(token)

    def layer_weights(l, x_in):
        if l == 0:
            lands = first
        else:
            _, lands = _gather_wait(*started[l], x_in, l)
        full = [lax.dynamic_update_slice(g, w[l][None], (chip, 0, 0)) for g, w in zip(lands, own)]
        wl = {}
        wl["main"], wl["f"] = _w_in_aligned(full[0], d, heads)
        for n, g in zip(_BIG[1:], full[1:]):
            if n in _COL_SHARDED:
                wl[_KEY[n]] = [_part(g, lead=(j,)) for j in range(N_CHIPS)]
            else:
                wl[_KEY[n]] = g.reshape(N_CHIPS * g.shape[1], g.shape[2])
        return wl

    shard_shape = {n: (1, N_CHIPS) + args[n].shape[1:] for n in _BIG}
    dests = {_KEY[n]: (shard_shape[n], N_CHIPS) if n in _COL_SHARDED
             else ((1, N_CHIPS * args[n].shape[1], args[n].shape[2]), 1) for n in _BIG[1:]}
    sp = {n: args[n] for n in _SMALL}
    sp["conv_a"] = conv_a_full.reshape(depth, conv_a.shape[1], d)
    sp["conv_c"] = conv_c_full.reshape(depth, conv_c.shape[1], d)

    in_flight = {}

    def on_layer(l, gl):
        local = [_w_in_shards(gl["main"], gl["f"], None, 0, shard_shape["w_in"], heads)]
        local += [gl[_KEY[n]].reshape(shard_shape[n]) for n in _BIG[1:]]
        chip_sums = [_add_sibling(g, r, c_idx) for g, r in zip(local, _swap_halves(local))]
        send, recv, chip_sums, lands, token = _exchange_start(chip_sums, l)
        in_flight[l] = (send, recv, chip_sums, lands)
        return token

    x_in = x[0]
    for token in tokens:
        x_in = x_in + token[0, 0]
    loss_row, grad_x, gsmall, gw = _local_step(x_in, mem[0], loss_target[0], sp, layer_weights, dests, on_layer)
    halves = [None] * len(_BIG)
    for l in reversed(range(depth)):
        chip_sums, lands = _exchange_wait(*in_flight[l], grad_x, l)
        halves = [_add_chips(s_, r, chip_idx, c_idx, l, depth, h) for s_, r, h in zip(chip_sums, lands, halves)]
    grads = dict(zip(_BIG, _join_halves(halves)))

    def as_rows(n, a):
        return _pad_lanes(a, d) if n == "b_forget" else a.reshape(-1, d)

    pieces = [as_rows(n, gsmall[n]) for n in _SMALL]
    for n in _SMALL_SHARDED:
        pieces.append(gsmall[n].reshape(-1, d))
    pieces.append(_pad_lanes(loss_row, d))
    pack = _pack_rows(pieces)
    total = _sum_devices(_allgather8(pack), pack.shape[0])
    shapes = [p.shape[0] for p in pieces]
    summed = _unpack_rows(total, shapes)
    for n, g in zip(_SMALL, summed):
        grads[n] = g[:, :heads] if n == "b_forget" else g.reshape(args[n].shape)
    for n, g in zip(_SMALL_SHARDED, summed[len(_SMALL):]):
        g = g.reshape(args[n].shape[:2] + (d,))
        grads[n] = lax.dynamic_slice_in_dim(g, chip * args[n].shape[2], args[n].shape[2], axis=2)
    loss = summed[-1][0, 0]

    delta, new_m, new_v = {}, {}, {}
    packed = [_pack_rows([as_rows(n, src[pre + n]) for n in _SMALL])
              for src, pre in ((args, ""), (grads, ""), (args, "m_"), (args, "v_"))]
    srows = [as_rows(n, args[n]).shape[0] for n in _SMALL]
    updated = [_unpack_rows(a, srows) for a in _adamw(*packed)]
    for k, n in enumerate(_SMALL):
        delta[n], new_m[n], new_v[n] = (
            (u[k][:, :heads] if n == "b_forget" else u[k]).reshape(args[n].shape) for u in updated)
    for n in _BIG + _SMALL_SHARDED:
        if n == "w_in":
            to_t, from_t = (lambda a: jnp.transpose(a, (2, 0, 1))), (lambda a: jnp.transpose(a, (1, 2, 0)))
            g_t = to_t(grads[n])
            grads[n] = from_t(g_t)
            delta[n], new_m[n], new_v[n] = (from_t(a) for a in _adamw(to_t(args[n]), g_t, to_t(args["m_" + n]), to_t(args["v_" + n])))
            continue
        delta[n], new_m[n], new_v[n] = _adamw(args[n], grads[n], args["m_" + n], args["v_" + n])

    return (loss, grad_x[None], *[grads[n] for n in names], *[delta[n] for n in names], *[new_m[n] for n in names],
            *[new_v[n] for n in names])
```

```python
import functools
import math

import jax
import jax.numpy as jnp
from jax import lax
from jax.experimental import pallas as pl
from jax.experimental.pallas import tpu as pltpu

F32, BF16 = jnp.float32, jnp.bfloat16
EPS = 1e-6
LANES = 128
SUBLANES = 8
VMEM_LIMIT = 56 * 1024 * 1024
MM_VMEM_BUDGET = 44 * 1024 * 1024
MESH = pl.DeviceIdType.MESH
N_CHIPS = 4
N_DEV = 8

ADAM_LR, ADAM_B1, ADAM_B2, ADAM_EPS, ADAM_WD, ADAM_STEP = 0.001, 0.9, 0.999, 1e-08, 0.01, 10


def _params(**kw):
    return pltpu.CompilerParams(vmem_limit_bytes=VMEM_LIMIT, **kw)


def _sds(shape, dtype):
    return jax.ShapeDtypeStruct(tuple(shape), dtype)


def _part(arr, off=0, width=None, lead=()):
    assert arr.ndim == len(lead) + 2
    return (arr, off, arr.shape[-1] - off if width is None else width, tuple(lead))


def _unit(parts):
    u = 0
    for _, off, w, _ in parts:
        u = math.gcd(u, math.gcd(off, w))
    return u


def _pick(dim, unit, cands, lane):
    for c in cands:
        if c <= dim and dim % c == 0 and unit % c == 0 and (not lane or c % LANES == 0):
            return c
    return min(dim, unit)


_COL_TILES = (1024, 1408, 512, 256, 128)
_ROW_TILES = (512, 256, 1408, 128)


def _mm(name, a, b, *, ta=False, tb=False, out_dtype=F32, add=None, dest=None):
    if not isinstance(a, list):
        a = [_part(a)]
    if not isinstance(b, list):
        b = [_part(b)]
    a_rows, a_cols = a[0][0].shape[-2], sum(p[2] for p in a)
    b_rows, b_cols = b[0][0].shape[-2], sum(p[2] for p in b)
    M, Ka = (a_cols, a_rows) if ta else (a_rows, a_cols)
    Kb, N = (b_cols, b_rows) if tb else (b_rows, b_cols)
    assert Ka == Kb, (name, Ka, Kb)
    K = Ka
    ua, ub = _unit(a), _unit(b)
    shards = 1 if dest is None else dest[3]
    tm = _pick(M, ua if ta else M, _ROW_TILES, ta)
    tn = _pick(N, math.gcd(N if tb else ub, N // shards), _COL_TILES, not tb)
    ku = math.gcd(K if ta else ua, ub if tb else K)
    tk = _pick(K, ku, _COL_TILES, (not ta) or tb)
    a_size = sum(p[0].dtype.itemsize for p in a)
    b_size = sum(p[0].dtype.itemsize for p in b)

    def vmem_estimate():
        return (2 * tm * tk * a_size + 2 * tk * tn * b_size + tm * tn * (2 * jnp.dtype(out_dtype).itemsize + 4)
                + (8 * tm * tn if add is not None else 0))

    while vmem_estimate() > MM_VMEM_BUDGET:
        if tk % 256 == 0 and tk >= tn:
            tk //= 2
        elif tn % 256 == 0:
            tn //= 2
        elif tm % 256 == 0:
            tm //= 2
        elif tk % 256 == 0:
            tk //= 2
        else:
            break
    nm, nn, nk = M // tm, N // tn, K // tk
    a_tile, b_tile = (tm if ta else tk), (tk if tb else tn)
    a_nb, b_nb = a[0][2] // a_tile, b[0][2] // b_tile

    def index(parts, nb, tile, p, row, col):
        base = parts[p][1] // tile
        if len(parts) == 1:
            return parts[p][3] + (row, base + col)
        inside = (col // nb) == p
        return parts[p][3] + (jnp.where(inside, row, 0), base + jnp.clip(col - p * nb, 0, nb - 1))

    def spec(parts, p, shape, fn):
        return pl.BlockSpec((None,) * len(parts[p][3]) + shape, fn)

    in_specs, operands = [], []
    for p in range(len(a)):
        if ta:
            in_specs.append(spec(a, p, (tk, tm), lambda mi, nj, kk, p=p: index(a, a_nb, a_tile, p, kk, mi)))
        else:
            in_specs.append(spec(a, p, (tm, tk), lambda mi, nj, kk, p=p: index(a, a_nb, a_tile, p, mi, kk)))
        operands.append(a[p][0])
    for p in range(len(b)):
        if tb:
            in_specs.append(spec(b, p, (tn, tk), lambda mi, nj, kk, p=p: index(b, b_nb, b_tile, p, nj, kk)))
        else:
            in_specs.append(spec(b, p, (tk, tn), lambda mi, nj, kk, p=p: index(b, b_nb, b_tile, p, kk, nj)))
        operands.append(b[p][0])
    if add is not None:
        in_specs.append(pl.BlockSpec((tm, tn), lambda mi, nj, kk: (mi, nj)))
        operands.append(add)
    aliases = {}
    if dest is None:
        out_spec = pl.BlockSpec((tm, tn), lambda mi, nj, kk: (mi, nj))
        out_shape = _sds((M, N), out_dtype)
    else:
        d_shape, d_prev, d_lead, _ = dest
        nbs = N // shards // tn
        if shards == 1:
            out_spec = pl.BlockSpec((None,) * len(d_lead) + (tm, tn), lambda mi, nj, kk: tuple(d_lead) + (mi, nj))
        else:
            out_spec = pl.BlockSpec((None,) * (len(d_lead) + 1) + (tm, tn),
                                    lambda mi, nj, kk: tuple(d_lead) + (nj // nbs, mi, nj % nbs))
        out_shape = _sds(d_shape, out_dtype)
        if d_prev is not None:
            aliases = {len(operands): 0}
            in_specs.append(pl.BlockSpec(memory_space=pl.ANY))
            operands.append(d_prev)
    n_extra = 1 if aliases else 0
    dims = (((0 if ta else 1,), (1 if tb else 0,)), ((), ()))
    na, nb_ = len(a), len(b)

    def body(*refs):
        a_refs, b_refs = refs[:na], refs[na:na + nb_]
        pos = na + nb_
        add_ref = None
        if add is not None:
            add_ref = refs[pos]
            pos += 1
        pos += n_extra
        o_ref = refs[pos]
        acc_ref = refs[pos + 1] if nk > 1 else None
        mi, nj, kk = pl.program_id(0), pl.program_id(1), pl.program_id(2)
        a_col = mi if ta else kk
        b_col = kk if tb else nj

        def finish(r):
            if add_ref is not None:
                r = r + add_ref[...].astype(F32)
            o_ref[...] = r.astype(o_ref.dtype)

        def step(a_ref, b_ref):
            r = lax.dot_general(a_ref[...].astype(BF16), b_ref[...].astype(BF16), dims, preferred_element_type=F32)
            if nk == 1:
                finish(r)
            else:
                @pl.when(kk == 0)
                def _():
                    acc_ref[...] = r

                @pl.when(kk > 0)
                def _():
                    acc_ref[...] += r

        for pa in range(na):
            for pb in range(nb_):
                if na * nb_ == 1:
                    step(a_refs[pa], b_refs[pb])
                else:
                    cond = jnp.logical_and(a_col // a_nb == pa, b_col // b_nb == pb)
                    pl.when(cond)(functools.partial(step, a_refs[pa], b_refs[pb]))
        if nk > 1:
            @pl.when(kk == nk - 1)
            def _():
                finish(acc_ref[...])

    return pl.pallas_call(
        body, grid=(nm, nn, nk), in_specs=in_specs, out_specs=out_spec, out_shape=out_shape, input_output_aliases=aliases,
        scratch_shapes=[pltpu.VMEM((tm, tn), F32)] if nk > 1 else [],
        compiler_params=_params(dimension_semantics=("parallel", "parallel", "arbitrary")),
        name=name)(*operands)


def _row_spec(tb, part):
    _, off, w, _ = part
    return pl.BlockSpec((tb, w), lambda i, o=off // w: (i, o))


def _full_spec(arr):
    return pl.BlockSpec(arr.shape, lambda i: (0,) * arr.ndim)


def _rows(name, fn, rows, pars, outs, tb=256):
    n = rows[0][0].shape[0]
    tb = min(tb, n)
    nr, npar = len(rows), len(pars)

    def body(*refs):
        r = [ref[...].astype(F32) for ref in refs[:nr]]
        p = [ref[...].astype(F32) for ref in refs[nr:nr + npar]]
        res = fn(*r, *p)
        for o_ref, v in zip(refs[nr + npar:], res):
            o_ref[...] = v.astype(o_ref.dtype)

    out = pl.pallas_call(
        body, grid=(n // tb,),
        in_specs=[_row_spec(tb, p) for p in rows] + [_full_spec(p) for p in pars],
        out_specs=[pl.BlockSpec((tb, w), lambda i: (i, 0)) for w, _ in outs],
        out_shape=[_sds((n, w), dt) for w, dt in outs],
        compiler_params=_params(), name=name)(*[p[0] for p in rows], *pars)
    return out


def _rows_vjp(name, fn, rows, pars, cots, row_grads, par_grads, adds=None, tb=256):
    adds = adds or {}
    n = rows[0][0].shape[0]
    tb = min(tb, n)
    nr, npar, nc = len(rows), len(pars), len(cots)
    add_keys = sorted(adds)
    rg = [j for j, dt in enumerate(row_grads) if dt is not None]
    pg = [j for j, f in enumerate(par_grads) if f]

    def body(*refs):
        pos = 0
        r = [ref[...].astype(F32) for ref in refs[pos:pos + nr]]
        pos += nr
        p = [ref[...].astype(F32) for ref in refs[pos:pos + npar]]
        pos += npar
        ct = tuple(ref[...].astype(F32) for ref in refs[pos:pos + nc])
        pos += nc
        add_v = {k: ref[...].astype(F32) for k, ref in zip(add_keys, refs[pos:pos + len(add_keys)])}
        pos += len(add_keys)
        _, vjp = jax.vjp(lambda *args: tuple(fn(*args)), *r, *p)
        g = vjp(ct)
        for j in rg:
            v = g[j]
            if j in add_v:
                v = v + add_v[j]
            refs[pos][...] = v.astype(refs[pos].dtype)
            pos += 1
        first = pl.program_id(0) == 0
        for j in pg:
            ref = refs[pos]
            pos += 1

            @pl.when(first)
            def _(ref=ref):
                ref[...] = jnp.zeros_like(ref)

            ref[...] += g[nr + j]

    in_specs = ([_row_spec(tb, p) for p in rows] + [_full_spec(p) for p in pars] + [_row_spec(tb, p) for p in cots]
                + [_row_spec(tb, adds[k]) for k in add_keys])
    out_specs = ([pl.BlockSpec((tb, rows[j][2]), lambda i: (i, 0)) for j in rg] + [_full_spec(pars[j]) for j in pg])
    out_shape = ([_sds((n, rows[j][2]), row_grads[j]) for j in rg] + [_sds(pars[j].shape, F32) for j in pg])
    out = pl.pallas_call(
        body, grid=(n // tb,), in_specs=in_specs, out_specs=out_specs, out_shape=out_shape,
        compiler_params=_params(), name=name)(
            *[p[0] for p in rows], *pars, *[p[0] for p in cots], *[adds[k][0] for k in add_keys])
    return out[:len(rg)], out[len(rg):]


def _rms_fn(x, g):
    return (x * lax.rsqrt(jnp.mean(x * x, axis=-1, keepdims=True) + EPS) * g,)


def _lnsilu_fn(u, g, b):
    mu = jnp.mean(u, axis=-1, keepdims=True)
    xc = u - mu
    y = xc * lax.rsqrt(jnp.mean(xc * xc, axis=-1, keepdims=True) + EPS) * g + b
    return (y * jax.nn.sigmoid(y),)


def _merge_fn(ga, gb, gc, ya, yb, yc, ba, bb, bc):
    return (jax.nn.sigmoid(ga + ba) * ya + jax.nn.sigmoid(gb + bb) * yb + jax.nn.sigmoid(gc + bc) * yc,)


def _swiglu_fn(g, u):
    return (g * jax.nn.sigmoid(g) * u,)


def _loss_call(x, tgt, gain):
    n, d = x.shape
    tb = min(256, n)

    def body(x_ref, t_ref, g_ref, l_ref, dx_ref, dg_ref):
        (y,), vjp = jax.vjp(lambda a, b: tuple(_rms_fn(a, b)), x_ref[...], g_ref[...])
        e = y - t_ref[...]
        part = 0.5 * jnp.sum(jnp.mean(e * e, axis=-1, keepdims=True))
        dx, dg = vjp((e * (1.0 / d),))
        dx_ref[...] = dx

        @pl.when(pl.program_id(0) == 0)
        def _():
            l_ref[...] = jnp.zeros_like(l_ref)
            dg_ref[...] = jnp.zeros_like(dg_ref)

        l_ref[...] += jnp.full(l_ref.shape, part, F32)
        dg_ref[...] += dg

    return pl.pallas_call(
        body, grid=(n // tb,),
        in_specs=[pl.BlockSpec((tb, d), lambda i: (i, 0)), pl.BlockSpec((tb, d), lambda i: (i, 0)), _full_spec(gain)],
        out_specs=[pl.BlockSpec((1, LANES), lambda i: (0, 0)), pl.BlockSpec((tb, d), lambda i: (i, 0)), _full_spec(gain)],
        out_shape=[_sds((1, LANES), F32), _sds((n, d), F32), _sds(gain.shape, F32)],
        compiler_params=_params(), name="loss_head")(x, tgt, gain)


FOX_BLOCK = 256
CONV_ROWS = 64
CONV_HALO = 32


def _chunk(i):
    return pl.ds(pl.multiple_of(i * CONV_ROWS, CONV_ROWS), CONV_ROWS)


def _delayed(ext, s):
    return (ext if s == 0 else pltpu.roll(ext, s, 0))[CONV_HALO:]


def _advanced(ext, s):
    return (ext if s == 0 else pltpu.roll(ext, ext.shape[0] - s, 0))[:CONV_ROWS]


def _conv_pass(t, taps, w_ref, lanes, get_u, emit, get_dy=None, dw_acc=None):
    def body(i, carry):
        rows = _chunk(i)
        u = get_u(rows)
        ext = jnp.concatenate([carry, u], axis=0)
        dy = None if get_dy is None else get_dy(rows)
        acc = None
        for k in range(taps):
            sh = _delayed(ext, taps - 1 - k)
            term = w_ref[k:k + 1, lanes] * sh
            acc = term if acc is None else acc + term
            if dy is not None:
                dw_acc[k] += jnp.sum((dy * sh).reshape(CONV_ROWS // SUBLANES, SUBLANES, LANES), axis=0)
        emit(rows, u, acc)
        return u[CONV_ROWS - CONV_HALO:]

    lax.fori_loop(0, t // CONV_ROWS, body, jnp.zeros((CONV_HALO, LANES), F32))


def _conv_pass_t(t, taps, w_ref, lanes, get_dy, emit):
    n = t // CONV_ROWS

    def body(j, carry):
        rows = _chunk(n - 1 - j)
        dy = get_dy(rows)
        ext = jnp.concatenate([dy, carry], axis=0)
        acc = None
        for k in range(taps):
            term = w_ref[k:k + 1, lanes] * _advanced(ext, taps - 1 - k)
            acc = term if acc is None else acc + term
        emit(rows, acc)
        return dy[:CONV_HALO]

    lax.fori_loop(0, n, body, jnp.zeros((CONV_HALO, LANES), F32))


def _chan_block(d):
    return min(256, d)


def _lane_groups(cb):
    return [slice(g * LANES, (g + 1) * LANES) for g in range(cb // LANES)]


def _conv_a_fwd(za, w):
    t, d = za.shape[0], za.shape[1] // 3
    taps, cb = w.shape[0], _chan_block(d)
    nb = d // cb
    assert taps - 1 <= CONV_HALO and t % CONV_ROWS == 0

    def body(ab_ref, ac_ref, au_ref, w_ref, o_ref):
        for lanes in _lane_groups(cb):
            def emit(rows, u, s, lanes=lanes):
                o_ref[rows, lanes] = (ab_ref[rows, lanes] * s).astype(o_ref.dtype)

            _conv_pass(t, taps, w_ref, lanes, lambda rows, lanes=lanes: ac_ref[rows, lanes] * au_ref[rows, lanes], emit)

    return pl.pallas_call(
        body, grid=(nb,),
        in_specs=[pl.BlockSpec((t, cb), lambda j, o=o: (0, o * nb + j)) for o in range(3)]
        + [pl.BlockSpec((taps, cb), lambda j: (0, j))],
        out_specs=pl.BlockSpec((t, cb), lambda j: (0, j)),
        out_shape=_sds((t, d), BF16), compiler_params=_params(), name="conv_a_fwd")(za, za, za, w)


def _finish_taps(dw_ref, dw_acc, lanes, taps):
    for k in range(taps):
        dw_ref[k:k + 1, lanes] = jnp.sum(dw_acc[k], axis=0, keepdims=True)


def _conv_a_bwd(za, w, dpa):
    t, d = za.shape[0], za.shape[1] // 3
    taps, cb = w.shape[0], _chan_block(d)
    nb = d // cb

    def body(ab_ref, ac_ref, au_ref, w_ref, dy_ref, dab_ref, dac_ref, dau_ref, dw_ref, dw_acc):
        for lanes in _lane_groups(cb):
            dw_acc[...] = jnp.zeros_like(dw_acc)

            def ds(rows, lanes=lanes):
                return dy_ref[rows, lanes] * ab_ref[rows, lanes]

            def emit_s(rows, u, s, lanes=lanes):
                dab_ref[rows, lanes] = (dy_ref[rows, lanes] * s).astype(dab_ref.dtype)

            def emit_du(rows, du, lanes=lanes):
                dac_ref[rows, lanes] = (du * au_ref[rows, lanes]).astype(dac_ref.dtype)
                dau_ref[rows, lanes] = (du * ac_ref[rows, lanes]).astype(dau_ref.dtype)

            _conv_pass(t, taps, w_ref, lanes, lambda rows, lanes=lanes: ac_ref[rows, lanes] * au_ref[rows, lanes], emit_s,
                       get_dy=ds, dw_acc=dw_acc)
            _conv_pass_t(t, taps, w_ref, lanes, ds, emit_du)
            _finish_taps(dw_ref, dw_acc, lanes, taps)

    blk = pl.BlockSpec((t, cb), lambda j: (0, j))
    return pl.pallas_call(
        body, grid=(nb,),
        in_specs=[pl.BlockSpec((t, cb), lambda j, o=o: (0, o * nb + j)) for o in range(3)]
        + [pl.BlockSpec((taps, cb), lambda j: (0, j)), blk],
        out_specs=[blk, blk, blk, pl.BlockSpec((taps, cb), lambda j: (0, j))],
        out_shape=[_sds((t, d), BF16)] * 3 + [_sds((taps, d), F32)],
        scratch_shapes=[pltpu.VMEM((taps, SUBLANES, LANES), F32)],
        compiler_params=_params(), name="conv_a_bwd")(za, za, za, w, dpa)


def _conv_c_fwd(zcg, w, bias):
    t, d = zcg.shape[0], w.shape[1]
    taps, cb = w.shape[0], _chan_block(d)
    nb = d // cb
    assert taps - 1 <= CONV_HALO and t % CONV_ROWS == 0

    def body(cv_ref, cg_ref, w_ref, b_ref, o_ref):
        for lanes in _lane_groups(cb):
            def emit(rows, u, s, lanes=lanes):
                o_ref[rows, lanes] = s + b_ref[:, lanes]

            _conv_pass(t, taps, w_ref, lanes,
                       lambda rows, lanes=lanes: cv_ref[rows, lanes] * jax.nn.sigmoid(cg_ref[rows, lanes]), emit)

    return pl.pallas_call(
        body, grid=(nb,),
        in_specs=[pl.BlockSpec((t, cb), lambda j, o=o: (0, o * nb + j)) for o in range(2)]
        + [pl.BlockSpec((taps, cb), lambda j: (0, j)), pl.BlockSpec((1, cb), lambda j: (0, j))],
        out_specs=pl.BlockSpec((t, cb), lambda j: (0, j)),
        out_shape=_sds((t, d), F32), compiler_params=_params(), name="conv_c_fwd")(zcg, zcg, w, bias)


def _conv_c_bwd(zcg, w, du2):
    t, d = zcg.shape[0], w.shape[1]
    taps, cb = w.shape[0], _chan_block(d)
    nb = d // cb

    def body(cv_ref, cg_ref, w_ref, dy_ref, dcv_ref, dcg_ref, dw_ref, db_ref, dw_acc, db_acc):
        for lanes in _lane_groups(cb):
            dw_acc[...] = jnp.zeros_like(dw_acc)
            db_acc[...] = jnp.zeros_like(db_acc)

            def dy(rows, lanes=lanes):
                return dy_ref[rows, lanes]

            def emit_s(rows, u, s, lanes=lanes):
                db_acc[...] += jnp.sum(dy_ref[rows, lanes].reshape(CONV_ROWS // SUBLANES, SUBLANES, LANES), axis=0)

            def emit_du(rows, du, lanes=lanes):
                cv, sg = cv_ref[rows, lanes], jax.nn.sigmoid(cg_ref[rows, lanes])
                dcv_ref[rows, lanes] = (du * sg).astype(dcv_ref.dtype)
                dcg_ref[rows, lanes] = (du * cv * sg * (1.0 - sg)).astype(dcg_ref.dtype)

            _conv_pass(t, taps, w_ref, lanes,
                       lambda rows, lanes=lanes: cv_ref[rows, lanes] * jax.nn.sigmoid(cg_ref[rows, lanes]), emit_s,
                       get_dy=dy, dw_acc=dw_acc)
            _conv_pass_t(t, taps, w_ref, lanes, dy, emit_du)
            _finish_taps(dw_ref, dw_acc, lanes, taps)
            db_ref[:, lanes] = jnp.sum(db_acc[...], axis=0, keepdims=True)

    blk = pl.BlockSpec((t, cb), lambda j: (0, j))
    return pl.pallas_call(
        body, grid=(nb,),
        in_specs=[pl.BlockSpec((t, cb), lambda j, o=o: (0, o * nb + j)) for o in range(2)]
        + [pl.BlockSpec((taps, cb), lambda j: (0, j)), blk],
        out_specs=[blk, blk, pl.BlockSpec((taps, cb), lambda j: (0, j)), pl.BlockSpec((1, cb), lambda j: (0, j))],
        out_shape=[_sds((t, d), BF16)] * 2 + [_sds((taps, d), F32), _sds((1, d), F32)],
        scratch_shapes=[pltpu.VMEM((taps, SUBLANES, LANES), F32), pltpu.VMEM((SUBLANES, LANES), F32)],
        compiler_params=_params(), name="conv_c_bwd")(zcg, zcg, w, du2)


def _tri(tb, t, i, lower):
    r = lax.broadcasted_iota(jnp.int32, (tb, t), 0) + i * tb
    c = lax.broadcasted_iota(jnp.int32, (tb, t), 1)
    return ((r >= c) if lower else (c >= r)).astype(F32)


def _fox_cumsum(zf, bf):
    t = zf.shape[0]
    tb = min(256, t)

    def body(z_ref, b_ref, c_ref):
        logf = jax.nn.log_sigmoid(z_ref[...] + b_ref[...])
        c_ref[...] = jnp.dot(_tri(tb, t, pl.program_id(0), True), logf, precision=lax.Precision.HIGHEST,
                             preferred_element_type=F32)

    return pl.pallas_call(
        body, grid=(t // tb,), in_specs=[_full_spec(zf), _full_spec(bf)],
        out_specs=pl.BlockSpec((tb, LANES), lambda i: (i, 0)), out_shape=_sds((t, LANES), F32),
        compiler_params=_params(), name="fox_cumsum")(zf, bf)


def _fox_cumsum_bwd(zf, bf, dc_t):
    t = zf.shape[0]
    tb = min(256, t)

    def body(z_ref, b_ref, dc_ref, dz_ref, db_ref):
        i = pl.program_id(0)

        @pl.when(i == 0)
        def _():
            db_ref[...] = jnp.zeros_like(db_ref)

        dlogf = jnp.dot(_tri(tb, t, i, False), dc_ref[...], precision=lax.Precision.HIGHEST, preferred_element_type=F32)
        dz = dlogf * jax.nn.sigmoid(-(z_ref[...] + b_ref[...]))
        dz_ref[...] = dz.astype(dz_ref.dtype)
        db_ref[...] += jnp.sum(dz, axis=0, keepdims=True)

    return pl.pallas_call(
        body, grid=(t // tb,),
        in_specs=[pl.BlockSpec((tb, LANES), lambda i: (i, 0)), _full_spec(bf), _full_spec(dc_t)],
        out_specs=[pl.BlockSpec((tb, LANES), lambda i: (i, 0)), pl.BlockSpec((1, LANES), lambda i: (0, 0))],
        out_shape=[_sds((t, LANES), BF16), _sds((1, LANES), F32)],
        compiler_params=_params(), name="fox_cumsum_bwd")(zf, bf, dc_t)


def _fox_block(q, k, v, cr, *, q0, hd):
    tq, kext = q.shape[0], k.shape[0]
    nh = LANES // hd
    lane = lax.broadcasted_iota(jnp.int32, (1, LANES), 1)
    causal = (lax.broadcasted_iota(jnp.int32, (tq, kext), 0) + q0) >= lax.broadcasted_iota(jnp.int32, (tq, kext), 1)
    kb, vb = k.astype(BF16), v.astype(BF16)
    out = jnp.zeros((tq, LANES), F32)
    for h in range(nh):
        m = (lane // hd) == h
        qh = jnp.where(m, q * (hd ** -0.5), 0.0).astype(BF16)
        s = lax.dot_general(qh, kb, (((1,), (1,)), ((), ())), preferred_element_type=F32)
        c_k = cr[h] - lax.stop_gradient(cr[h][:, q0:q0 + 1])
        s = jnp.where(causal, s - c_k, -1e30)
        e = jnp.exp(s - lax.stop_gradient(jnp.max(s, axis=1, keepdims=True)))
        prob = e / jnp.sum(e, axis=1, keepdims=True)
        o = jnp.dot(prob.astype(BF16), vb, preferred_element_type=F32)
        out = out + jnp.where(m, o, 0.0)
    return out


def _fox_dims(zqkv, crow):
    t, d = zqkv.shape[0], zqkv.shape[1] // 3
    hd = d // crow.shape[0]
    assert math.log2(hd) % 2 == 0
    return t, d, hd, d // LANES, LANES // hd, min(FOX_BLOCK, t)


def _fox_specs(i, tq, nblk, nh):
    kext = (i + 1) * tq
    return [pl.BlockSpec((tq, LANES), lambda p: (i, p)), pl.BlockSpec((kext, LANES), lambda p: (0, nblk + p)),
            pl.BlockSpec((kext, LANES), lambda p: (0, 2 * nblk + p)), pl.BlockSpec((nh, 1, kext), lambda p: (p, 0, 0))]


def _fox_fwd(zqkv, crow):
    t, d, hd, nblk, nh, tq = _fox_dims(zqkv, crow)
    o = None
    for i in range(t // tq):
        def body(q_ref, k_ref, v_ref, cr_ref, *rest, i=i):
            o_ref = rest[-1]
            o_ref[...] = _fox_block(q_ref[...].astype(F32), k_ref[...], v_ref[...], cr_ref[...],
                                    q0=i * tq, hd=hd).astype(o_ref.dtype)

        prev = [] if o is None else [o]
        o = pl.pallas_call(
            body, grid=(nblk,),
            in_specs=_fox_specs(i, tq, nblk, nh) + [pl.BlockSpec(memory_space=pl.ANY)] * len(prev),
            out_specs=pl.BlockSpec((tq, LANES), lambda p, i=i: (i, p)), out_shape=_sds((t, d), BF16),
            input_output_aliases={4: 0} if prev else {},
            compiler_params=_params(), name=f"fox_fwd_q{i}")(zqkv, zqkv, zqkv, crow, *prev)
    return o


def _fox_bwd(zqkv, crow, do):
    t, d, hd, nblk, nh, tq = _fox_dims(zqkv, crow)
    nq = t // tq
    acc = None
    for i in reversed(range(nq)):
        kext = (i + 1) * tq
        first = acc is None

        def body(q_ref, k_ref, v_ref, cr_ref, do_ref, *rest, first=first, i=i):
            if first:
                dq_ref, dk_ref, dv_ref, dcr_ref = rest
            else:
                dk_in, dv_in, dcr_in = rest[1:4]
                dq_ref, dk_ref, dv_ref, dcr_ref = rest[4:]
            f = functools.partial(_fox_block, q0=i * tq, hd=hd)
            _, vjp = jax.vjp(f, q_ref[...].astype(F32), k_ref[...].astype(F32), v_ref[...].astype(F32), cr_ref[...])
            dq, dk, dv, dcr = vjp(do_ref[...].astype(F32))
            dq_ref[...] = dq.astype(dq_ref.dtype)
            if first:
                dk_ref[...], dv_ref[...], dcr_ref[...] = dk, dv, dcr
            else:
                dk_ref[...] = dk_in[...] + dk
                dv_ref[...] = dv_in[...] + dv
                dcr_ref[...] = dcr_in[...] + dcr

        qblk = pl.BlockSpec((tq, LANES), lambda p, i=i: (i, p))
        kblk = pl.BlockSpec((kext, LANES), lambda p: (0, p))
        rblk = pl.BlockSpec((nh, 1, kext), lambda p: (p, 0, 0))
        if first:
            prev, prev_specs, aliases = [], [], {}
        else:
            prev, prev_specs = list(acc), [pl.BlockSpec(memory_space=pl.ANY), kblk, kblk, rblk]
            aliases = {5: 0, 6: 1, 7: 2, 8: 3}
        acc = pl.pallas_call(
            body, grid=(nblk,),
            in_specs=_fox_specs(i, tq, nblk, nh) + [qblk] + prev_specs,
            out_specs=[qblk, kblk, kblk, rblk],
            out_shape=[_sds((t, d), BF16), _sds((t, d), F32), _sds((t, d), F32), _sds(crow.shape, F32)],
            input_output_aliases=aliases,
            compiler_params=_params(), name=f"fox_bwd_q{i}")(zqkv, zqkv, zqkv, crow, do, *prev)
    return acc


def _xattn_block(q, k, v):
    s = lax.dot_general(q.astype(BF16), k.astype(BF16), (((1,), (1,)), ((), ())), preferred_element_type=F32)
    s = s * (q.shape[1] ** -0.5)
    e = jnp.exp(s - lax.stop_gradient(jnp.max(s, axis=1, keepdims=True)))
    prob = e / jnp.sum(e, axis=1, keepdims=True)
    return jnp.dot(prob.astype(BF16), v.astype(BF16), preferred_element_type=F32)


def _xattn_fwd(qx, kv):
    t, dx = qx.shape
    nm, nh, tq = kv.shape[0], qx.shape[1] // LANES, min(512, qx.shape[0])

    def body(q_ref, k_ref, v_ref, o_ref):
        o_ref[...] = _xattn_block(q_ref[...], k_ref[...], v_ref[...]).astype(o_ref.dtype)

    return pl.pallas_call(
        body, grid=(nh, t // tq),
        in_specs=[pl.BlockSpec((tq, LANES), lambda h, i: (i, h)), pl.BlockSpec((nm, LANES), lambda h, i: (0, h)),
                  pl.BlockSpec((nm, LANES), lambda h, i: (0, nh + h))],
        out_specs=pl.BlockSpec((tq, LANES), lambda h, i: (i, h)), out_shape=_sds((t, dx), BF16),
        compiler_params=_params(), name="xattn_fwd")(qx, kv, kv)


def _xattn_bwd(qx, kv, do):
    t, dx = qx.shape
    nm, nh, tq = kv.shape[0], qx.shape[1] // LANES, min(512, qx.shape[0])

    def body(q_ref, k_ref, v_ref, do_ref, dq_ref, dk_ref, dv_ref):
        _, vjp = jax.vjp(_xattn_block, q_ref[...].astype(F32), k_ref[...].astype(F32), v_ref[...].astype(F32))
        dq, dk, dv = vjp(do_ref[...].astype(F32))
        dq_ref[...] = dq.astype(dq_ref.dtype)

        @pl.when(pl.program_id(1) == 0)
        def _():
            dk_ref[...] = jnp.zeros_like(dk_ref)
            dv_ref[...] = jnp.zeros_like(dv_ref)

        dk_ref[...] += dk
        dv_ref[...] += dv

    qblk = pl.BlockSpec((tq, LANES), lambda h, i: (i, h))
    kblk = pl.BlockSpec((nm, LANES), lambda h, i: (0, h))
    return pl.pallas_call(
        body, grid=(nh, t // tq),
        in_specs=[qblk, kblk, pl.BlockSpec((nm, LANES), lambda h, i: (0, nh + h)), qblk],
        out_specs=[qblk, kblk, kblk],
        out_shape=[_sds((t, dx), BF16), _sds((nm, dx), F32), _sds((nm, dx), F32)],
        compiler_params=_params(), name="xattn_bwd")(qx, kv, kv, do)


def _row_tile(rows, cols, budget=1 << 20, step=2 * SUBLANES):
    best = None
    for tb in range(step, rows + 1, step):
        if rows % tb == 0 and tb * cols * 4 <= budget:
            best = tb
    return rows if best is None else best


def _adamw(w, g, m, v, keep_rank=False):
    shape = w.shape
    cols = shape[-1]
    if keep_rank:
        w2, g2, m2, v2 = w, g, m, v
        rows, tail = shape[0], shape[1:]
        tb = _row_tile(rows, math.prod(tail), step=1)
    else:
        w2, g2, m2, v2 = (a.reshape(-1, cols) for a in (w, g, m, v))
        rows, tail = w2.shape[0], (cols,)
        tb = _row_tile(rows, cols)

    def body(w_ref, g_ref, m_ref, v_ref, d_ref, nm_ref, nv_ref):
        gg = g_ref[...]
        nm = ADAM_B1 * m_ref[...] + (1.0 - ADAM_B1) * gg
        nv = ADAM_B2 * v_ref[...] + (1.0 - ADAM_B2) * (gg * gg)
        m_hat = nm / (1.0 - ADAM_B1 ** ADAM_STEP)
        v_hat = nv / (1.0 - ADAM_B2 ** ADAM_STEP)
        d_ref[...] = -ADAM_LR * (m_hat / (jnp.sqrt(v_hat) + ADAM_EPS) + ADAM_WD * w_ref[...])
        nm_ref[...] = nm
        nv_ref[...] = nv

    blk = pl.BlockSpec((tb,) + tail, lambda i: (i,) + (0,) * len(tail))
    out = pl.pallas_call(
        body, grid=(rows // tb,), in_specs=[blk] * 4, out_specs=[blk] * 3, out_shape=[_sds(w2.shape, F32)] * 3,
        compiler_params=_params(), name="adamw")(w2, g2, m2, v2)
    return tuple(o.reshape(shape) for o in out)


def _sum_devices(gathered, rows):
    cols = gathered.shape[1]

    def body(g_ref, o_ref):
        acc = g_ref[0:rows, :]
        for dev in range(1, N_DEV):
            acc = acc + g_ref[dev * rows:(dev + 1) * rows, :]
        o_ref[...] = acc

    return pl.pallas_call(body, out_shape=_sds((rows, cols), F32), compiler_params=_params(), name="sum_devices")(gathered)


def _place():
    x, y, c = lax.axis_index("x"), lax.axis_index("y"), lax.axis_index("c")
    chips = [(1 - x, y), (x, 1 - y), (1 - x, 1 - y)]
    return x, y, c, chips


def _chip_id(chip):
    return 2 * chip[0] + chip[1]


def _half(c, rows):
    rh = rows // 2
    return pl.ds(pl.multiple_of(c * rh, 16), rh)


_HBM = pl.BlockSpec(memory_space=pltpu.HBM)


def _allgather8(x_shard):
    m_per, n = x_shard.shape

    def body(x_ref, out_ref, send_sems, recv_sems, local_sem):
        x, y, c, chips = _place()
        me, sibling = (x, y, c), (x, y, 1 - c)

        def rows(px, py, pc):
            return out_ref.at[pl.ds((4 * px + 2 * py + pc) * m_per, m_per), :]

        def copy(k, block, to, src=None):
            return pltpu.make_async_remote_copy(
                src_ref=rows(*block) if src is None else src, dst_ref=rows(*block),
                send_sem=send_sems.at[k], recv_sem=recv_sems.at[k], device_id=to, device_id_type=MESH)

        mine = pltpu.make_async_copy(x_ref, rows(*me), local_sem)
        mine.start()
        first = [copy(0, me, sibling, src=x_ref)]
        first += [copy(1 + j, me, (*chip, c), src=x_ref) for j, chip in enumerate(chips)]
        for cp in first:
            cp.start()
        passed = [copy(4 + j, (*chip, c), sibling) for j, chip in enumerate(chips)]
        for j, chip in enumerate(chips):
            copy(1 + j, (*chip, c), me).wait_recv()
            passed[j].start()
        copy(0, sibling, me).wait_recv()
        for j, chip in enumerate(chips):
            copy(4 + j, (*chip, 1 - c), me).wait_recv()
        for cp in first + passed:
            cp.wait_send()
        mine.wait()

    return pl.pallas_call(
        body, out_shape=_sds((N_DEV * m_per, n), x_shard.dtype),
        in_specs=[pl.BlockSpec(memory_space=pltpu.VMEM)], out_specs=pl.BlockSpec(memory_space=pltpu.VMEM),
        scratch_shapes=[pltpu.SemaphoreType.DMA((7,)), pltpu.SemaphoreType.DMA((7,)), pltpu.SemaphoreType.DMA],
        compiler_params=_params(), name="allgather8")(x_shard)


def _gather_weights(ws, layer, after):
    n = len(ws)

    def body(*refs):
        w, out = refs[:n], refs[n + 1:2 * n + 1]
        send_i, recv_i, send_d, recv_d = refs[2 * n + 1:]
        x, y, c, chips = _place()
        sibling = (x, y, 1 - c)
        me_chip = 2 * x + y

        def slab(i, chip, half):
            return out[i].at[chip, _half(half, w[i].shape[1]), :]

        def over_ici(i, j, src_chip, to, src=None):
            return pltpu.make_async_remote_copy(
                src_ref=slab(i, src_chip, c) if src is None else src, dst_ref=slab(i, src_chip, c),
                send_sem=send_i.at[3 * i + j], recv_sem=recv_i.at[3 * i + j], device_id=to, device_id_type=MESH)

        def over_d2d(i, j, half):
            return pltpu.make_async_remote_copy(
                src_ref=slab(i, _chip_id(chips[j]), half), dst_ref=slab(i, _chip_id(chips[j]), half),
                send_sem=send_d.at[3 * i + j], recv_sem=recv_d.at[3 * i + j], device_id=sibling, device_id_type=MESH)

        sent = []
        for i in range(n):
            for j, chip in enumerate(chips):
                cp = over_ici(i, j, me_chip, (*chip, c), src=w[i].at[layer, _half(c, w[i].shape[1]), :])
                cp.start()
                sent.append(cp)
        for i in range(n):
            for j, chip in enumerate(chips):
                over_ici(i, j, _chip_id(chip), (x, y, c)).wait_recv()
                cp = over_d2d(i, j, c)
                cp.start()
                sent.append(cp)
        for i in range(n):
            for j in range(3):
                over_d2d(i, j, 1 - c).wait_recv()
        for cp in sent:
            cp.wait_send()

    return pl.pallas_call(
        body, in_specs=[_HBM] * n + [pl.BlockSpec(memory_space=pl.ANY)], out_specs=[_HBM] * n,
        out_shape=[_sds((N_CHIPS,) + a.shape[1:], a.dtype) for a in ws],
        scratch_shapes=[pltpu.SemaphoreType.DMA((3 * n,)) for _ in range(4)],
        compiler_params=_params(), name="gather_weights")(*ws, after)


_SEM = pl.BlockSpec(memory_space=pltpu.SEMAPHORE)
_DATAFLOW = pltpu.SideEffectType.DATAFLOW_SIDE_EFFECTING


def _gather_copies(w, land, send, recv, layer):
    x, y, c, chips = _place()
    return [pltpu.make_async_remote_copy(
        src_ref=w[i].at[layer], dst_ref=land[i].at[2 * x + y], send_sem=send.at[3 * i + j], recv_sem=recv.at[3 * i + j],
        device_id=(*chip, c), device_id_type=MESH) for i in range(len(w)) for j, chip in enumerate(chips)]


def _gather_start(ws, layer, after):
    n = len(ws)

    def body(*refs):
        w, land = refs[:n], refs[n:2 * n]
        send, recv = refs[2 * n + 1], refs[2 * n + 2]
        for cp in _gather_copies(w, land, send, recv, layer):
            cp.start()
        refs[-1][...] = jnp.zeros_like(refs[-1])

    lands = [lax.empty((N_CHIPS,) + a.shape[1:], a.dtype) for a in ws]
    operands = [pltpu.with_memory_space_constraint(a, pltpu.HBM) for a in list(ws) + lands]
    out = pl.pallas_call(
        body, name=f"gather_start_{layer}",
        out_shape=(pltpu.SemaphoreType.DMA((3 * n,)), pltpu.SemaphoreType.DMA((3 * n,)),
                   *[pltpu.HBM(a.shape, a.dtype) for a in operands], _sds((SUBLANES, LANES), F32)),
        in_specs=[_HBM] * (2 * n) + [pl.BlockSpec(memory_space=pl.ANY)],
        out_specs=(_SEM, _SEM, *[_HBM] * (2 * n), pl.BlockSpec(memory_space=pltpu.VMEM)),
        input_output_aliases={i: 2 + i for i in range(2 * n)},
        compiler_params=pltpu.CompilerParams(has_side_effects=_DATAFLOW))(*operands, after)
    return out[0], out[1], list(out[2:2 + n]), list(out[2 + n:2 + 2 * n]), out[-1]


def _gather_wait(send, recv, ws, lands, after, layer):
    n = len(ws)

    def body(*refs):
        for cp in _gather_copies(refs[:n], refs[n:2 * n], refs[2 * n], refs[2 * n + 1], layer):
            cp.wait_send()
            cp.wait_recv()

    out = pl.pallas_call(
        body, name=f"gather_wait_{layer}",
        out_shape=tuple(pltpu.HBM(a.shape, a.dtype) for a in list(ws) + list(lands)),
        in_specs=[_HBM] * (2 * n) + [_SEM, _SEM, pl.BlockSpec(memory_space=pl.ANY)], out_specs=tuple([_HBM] * (2 * n)),
        input_output_aliases={i: i for i in range(2 * n)},
        compiler_params=pltpu.CompilerParams(has_side_effects=_DATAFLOW))(*ws, *lands, send, recv, after)
    return list(out[:n]), list(out[n:])


def _swap_halves(gs):
    n = len(gs)

    def body(*refs):
        g, out = refs[:n], refs[n:2 * n]
        send, recv = refs[2 * n:]
        x, y, c, _ = _place()
        cps = [pltpu.make_async_remote_copy(
            src_ref=g[i].at[:, :, _half(1 - c, g[i].shape[2]), :], dst_ref=out[i],
            send_sem=send.at[i], recv_sem=recv.at[i], device_id=(x, y, 1 - c), device_id_type=MESH) for i in range(n)]
        for cp in cps:
            cp.start()
        for cp in cps:
            cp.wait()

    return pl.pallas_call(
        body, in_specs=[_HBM] * n, out_specs=[_HBM] * n,
        out_shape=[_sds(a.shape[:2] + (a.shape[2] // 2, a.shape[3]), a.dtype) for a in gs],
        scratch_shapes=[pltpu.SemaphoreType.DMA((n,)), pltpu.SemaphoreType.DMA((n,))],
        compiler_params=_params(), name="swap_halves")(*gs)


def _exchange_chips(ss):
    n = len(ss)

    def body(*refs):
        s, out = refs[:n], refs[n:2 * n]
        send, recv = refs[2 * n:]
        x, y, c, chips = _place()
        cps = []
        for i in range(n):
            for j, chip in enumerate(chips):
                cps.append(pltpu.make_async_remote_copy(
                    src_ref=s[i].at[:, _chip_id(chip)], dst_ref=out[i].at[j],
                    send_sem=send.at[3 * i + j], recv_sem=recv.at[3 * i + j], device_id=(*chip, c), device_id_type=MESH))
        for cp in cps:
            cp.start()
        for cp in cps:
            cp.wait()

    return pl.pallas_call(
        body, in_specs=[_HBM] * n, out_specs=[_HBM] * n,
        out_shape=[_sds((3, a.shape[0]) + a.shape[2:], a.dtype) for a in ss],
        scratch_shapes=[pltpu.SemaphoreType.DMA((3 * n,)), pltpu.SemaphoreType.DMA((3 * n,))],
        compiler_params=_params(), name="exchange_chips")(*ss)


def _exchange_copies(s, land, send, recv):
    x, y, c, chips = _place()
    return [pltpu.make_async_remote_copy(
        src_ref=s[i].at[:, _chip_id(chip)], dst_ref=land[i].at[j], send_sem=send.at[3 * i + j], recv_sem=recv.at[3 * i + j],
        device_id=(*chip, c), device_id_type=MESH) for i in range(len(s)) for j, chip in enumerate(chips)]


def _exchange_start(ss, layer):
    n = len(ss)

    def body(*refs):
        s, land = refs[:n], refs[n:2 * n]
        send, recv = refs[2 * n], refs[2 * n + 1]
        token = refs[-1]
        for cp in _exchange_copies(s, land, send, recv):
            cp.start()
        token[...] = jnp.zeros_like(token)

    lands = [lax.empty((3, a.shape[0]) + a.shape[2:], a.dtype) for a in ss]
    operands = [pltpu.with_memory_space_constraint(a, pltpu.HBM) for a in list(ss) + lands]
    out = pl.pallas_call(
        body, name=f"exchange_start_{layer}",
        out_shape=(pltpu.SemaphoreType.DMA((3 * n,)), pltpu.SemaphoreType.DMA((3 * n,)),
                   *[pltpu.HBM(a.shape, a.dtype) for a in operands], _sds((SUBLANES, LANES), F32)),
        in_specs=[_HBM] * (2 * n), out_specs=(_SEM, _SEM, *[_HBM] * (2 * n), pl.BlockSpec(memory_space=pltpu.VMEM)),
        input_output_aliases={i: 2 + i for i in range(2 * n)},
        compiler_params=pltpu.CompilerParams(has_side_effects=_DATAFLOW))(*operands)
    return out[0], out[1], list(out[2:2 + n]), list(out[2 + n:2 + 2 * n]), out[-1]


def _exchange_wait(send, recv, ss, lands, after, layer):
    n = len(ss)

    def body(*refs):
        s, land = refs[:n], refs[n:2 * n]
        for cp in _exchange_copies(s, land, refs[2 * n], refs[2 * n + 1]):
            cp.wait_send()
            cp.wait_recv()

    out = pl.pallas_call(
        body, name=f"exchange_wait_{layer}",
        out_shape=tuple(pltpu.HBM(a.shape, a.dtype) for a in list(ss) + list(lands)),
        in_specs=[_HBM] * (2 * n) + [_SEM, _SEM, pl.BlockSpec(memory_space=pl.ANY)], out_specs=tuple([_HBM] * (2 * n)),
        input_output_aliases={i: i for i in range(2 * n)},
        compiler_params=pltpu.CompilerParams(has_side_effects=_DATAFLOW))(*ss, *lands, send, recv, after)
    return list(out[:n]), list(out[n:])


def _join_halves(fs):
    n = len(fs)

    def body(*refs):
        f, out = refs[:n], refs[n:2 * n]
        send, recv = refs[2 * n:]
        x, y, c, _ = _place()

        def half(ref, which):
            return ref.at[:, _half(which, ref.shape[1]), :]

        for i in range(n):
            pltpu.make_async_remote_copy(
                src_ref=half(f[i], c), dst_ref=half(out[i], c), send_sem=send.at[i], recv_sem=recv.at[i],
                device_id=(x, y, 1 - c), device_id_type=MESH).start()
        for i in range(n):
            arrive = pltpu.make_async_remote_copy(
                src_ref=half(f[i], c), dst_ref=half(out[i], 1 - c), send_sem=send.at[i], recv_sem=recv.at[i],
                device_id=(x, y, 1 - c), device_id_type=MESH)
            arrive.wait_recv()
            arrive.wait_send()

    return pl.pallas_call(
        body, in_specs=[_HBM] * n, out_specs=[_HBM] * n, out_shape=[_sds(a.shape, a.dtype) for a in fs],
        input_output_aliases={i: i for i in range(n)},
        scratch_shapes=[pltpu.SemaphoreType.DMA((n,)), pltpu.SemaphoreType.DMA((n,))],
        compiler_params=_params(), name="join_halves")(*fs)


def _add_sibling(g, got, c_idx):
    nl, ns, r, cols = g.shape
    rh = r // 2
    tb = _row_tile(rh, cols)
    nb = rh // tb
    g3, got3 = g.reshape(nl * ns, r, cols), got.reshape(nl * ns, rh, cols)

    def body(c_ref, a_ref, b_ref, o_ref):
        o_ref[...] = (a_ref[...].astype(F32) + b_ref[...].astype(F32)).astype(o_ref.dtype)

    out = pl.pallas_call(
        body,
        grid_spec=pltpu.PrefetchScalarGridSpec(
            num_scalar_prefetch=1, grid=(nl * ns, nb),
            in_specs=[pl.BlockSpec((None, tb, cols), lambda s, i, c_ref: (s, c_ref[0] * nb + i, 0)),
                      pl.BlockSpec((None, tb, cols), lambda s, i, c_ref: (s, i, 0))],
            out_specs=pl.BlockSpec((None, tb, cols), lambda s, i, c_ref: (s, i, 0))),
        out_shape=_sds((nl * ns, rh, cols), BF16), compiler_params=_params(), name="add_sibling")(c_idx, g3, got3)
    return out.reshape(nl, ns, rh, cols)


def _add_chips(s, got, chip_idx, c_idx, layer, depth, prev):
    _, ns, rh, cols = s.shape
    tb = _row_tile(rh, cols)
    nb = rh // tb

    def body(k_ref, c_ref, a_ref, b0_ref, b1_ref, b2_ref, *rest):
        o_ref = rest[-1]
        o_ref[...] = ((a_ref[...].astype(F32) + b0_ref[...].astype(F32)) + b1_ref[...].astype(F32)) + b2_ref[...].astype(F32)

    extra = [] if prev is None else [prev]
    return pl.pallas_call(
        body,
        grid_spec=pltpu.PrefetchScalarGridSpec(
            num_scalar_prefetch=2, grid=(nb,),
            in_specs=[pl.BlockSpec((None, None, tb, cols), lambda i, k_ref, c_ref: (0, k_ref[0], i, 0))]
            + [pl.BlockSpec((None, None, tb, cols), lambda i, k_ref, c_ref, j=j: (j, 0, i, 0)) for j in range(3)]
            + [pl.BlockSpec(memory_space=pl.ANY)] * len(extra),
            out_specs=pl.BlockSpec((None, tb, cols), lambda i, k_ref, c_ref: (layer, c_ref[0] * nb + i, 0))),
        out_shape=_sds((depth, 2 * rh, cols), F32), input_output_aliases={6: 0} if extra else {},
        compiler_params=_params(), name="add_chips")(chip_idx, c_idx, s, got, got, got, *extra)


def _in_pieces(d, heads, ws):
    groups, start = [], 0
    for k in range(12):
        if k == 6:
            groups.append(("f", 0, start, heads))
            start += heads
        else:
            groups.append(("main", (k - (k > 6)) * d, start, d))
            start += d
    pieces = []
    for chip in range(N_CHIPS):
        lo, hi = chip * ws, (chip + 1) * ws
        for dest, dcol, gstart, w in groups:
            a, b = max(lo, gstart), min(hi, gstart + w)
            if a < b:
                pieces.append((chip, a - lo, dest, dcol + a - gstart, b - a))
    return pieces


def _w_in_aligned(gathered, d, heads):
    rows, ws = gathered.shape[1], gathered.shape[2]
    tb = min(256, rows)
    pieces = _in_pieces(d, heads, ws)

    def body(g_ref, m_ref, f_ref):
        f_ref[...] = jnp.zeros_like(f_ref)
        for chip, sc, dest, dc, w in pieces:
            (m_ref if dest == "main" else f_ref)[:, dc:dc + w] = g_ref[chip, :, sc:sc + w]

    return pl.pallas_call(
        body, grid=(rows // tb,), in_specs=[pl.BlockSpec((N_CHIPS, tb, ws), lambda i: (0, i, 0))],
        out_specs=[pl.BlockSpec((tb, 11 * d), lambda i: (i, 0)), pl.BlockSpec((tb, LANES), lambda i: (i, 0))],
        out_shape=[_sds((rows, 11 * d), gathered.dtype), _sds((rows, LANES), gathered.dtype)],
        compiler_params=_params(), name="w_in_aligned")(gathered)


def _w_in_shards(gms, gf, prev, l, shape, heads):
    lo, hi = gms
    rows, d = lo.shape[0], lo.shape[1] // 6
    ws = shape[3]
    tb = min(256, rows)
    pieces = _in_pieces(d, heads, ws)

    def body(lo_ref, hi_ref, f_ref, *rest):
        o_ref = rest[-1]
        for chip, sc, dest, dc, w in pieces:
            if dest == "f":
                src = f_ref[:, dc:dc + w]
            elif dc < 6 * d:
                src = lo_ref[:, dc:dc + w]
            else:
                src = hi_ref[:, dc - 6 * d:dc - 6 * d + w]
            o_ref[chip, :, sc:sc + w] = src

    extra = [] if prev is None else [prev]
    return pl.pallas_call(
        body, grid=(rows // tb,),
        in_specs=[pl.BlockSpec((tb, 6 * d), lambda i: (i, 0)), pl.BlockSpec((tb, 5 * d), lambda i: (i, 0)),
                  pl.BlockSpec((tb, LANES), lambda i: (i, 0))] + [pl.BlockSpec(memory_space=pl.ANY)] * len(extra),
        out_specs=pl.BlockSpec((None, N_CHIPS, tb, ws), lambda i: (l, 0, i, 0)), out_shape=_sds(shape, lo.dtype),
        input_output_aliases={3: 0} if extra else {},
        compiler_params=_params(), name="w_in_shards")(lo, hi, gf, *extra)


def _pad_lanes(a, width=LANES):
    return jnp.pad(a, ((0, 0), (0, width - a.shape[1])))


def _cols(operand):
    return sum(p[2] for p in operand) if isinstance(operand, list) else operand.shape[1]


def _local_step(x, mem, tgt, sp, wt, dests=None, on_layer=None):
    t, d = x.shape
    depth = sp["mix_norm"].shape[0]
    heads = sp["b_forget"].shape[1]
    row = lambda a, l: a[l:l + 1]
    weights = wt if callable(wt) else (lambda l, x_in: {k: v[l] for k, v in wt.items()})
    layer_weights = []

    memn, = _rows("rms_mem", _rms_fn, [_part(mem)], [sp["mem_norm"][None]], [(d, BF16)])
    saved = []
    for l in range(depth):
        wl = weights(l, x)
        layer_weights.append(wl)
        f = _cols(wl["gu"]) // 2
        sv = {"x0": x}
        h1, = _rows("rms_mix", _rms_fn, [_part(x)], [row(sp["mix_norm"], l)], [(d, BF16)])
        wm = wl["main"]
        za = _mm("mm_in_a", h1, [_part(wm, 0, 3 * d)])
        zqkv = _mm("mm_in_qkv", h1, [_part(wm, 3 * d, 3 * d)], out_dtype=BF16)
        zcg = _mm("mm_in_cg", h1, [_part(wm, 6 * d, 5 * d)])
        zf = _mm("mm_in_f", h1, wl["f"])
        bf = _pad_lanes(row(sp["b_forget"], l))
        c = _fox_cumsum(zf, bf)
        crow = c[:, :heads].T.reshape(heads, 1, t)
        o = _fox_fwd(zqkv, crow)
        pa = _conv_a_fwd(za, sp["conv_a"][l])
        u2 = _conv_c_fwd(zcg, sp["conv_c"][l], row(sp["conv_c_bias"], l))
        u3, = _rows("ln_silu", _lnsilu_fn, [_part(u2)], [row(sp["ln_c_gain"], l), row(sp["ln_c_bias"], l)], [(d, BF16)])
        ya = _mm("mm_out_a", pa, wl["out_a"])
        yb = _mm("mm_out_b", o, wl["out_b"])
        yc = _mm("mm_out_c", u3, wl["out_c"])
        bg = [sp["b_gate"][l:l + 1, k * d:(k + 1) * d] for k in range(3)]
        gate_rows = [_part(zcg, (2 + k) * d, d) for k in range(3)] + [_part(ya), _part(yb), _part(yc)]
        mg, = _rows("merge", _merge_fn, gate_rows, bg, [(d, BF16)])
        x1 = _mm("mm_o", mg, wl["o"], add=x)
        h2, = _rows("rms_x", _rms_fn, [_part(x1)], [row(sp["xattn_norm"], l)], [(d, BF16)])
        qx = _mm("mm_xq", h2, wl["xq"], out_dtype=BF16)
        kv = _mm("mm_xkv", memn, wl["xkv"], out_dtype=BF16)
        ox = _xattn_fwd(qx, kv)
        x2 = _mm("mm_xo", ox, wl["xo"], add=x1)
        h3, = _rows("rms_ffn", _rms_fn, [_part(x2)], [row(sp["ffn_norm"], l)], [(d, BF16)])
        gu = _mm("mm_gu", h3, wl["gu"])
        act, = _rows("swiglu", _swiglu_fn, [_part(gu, 0, f), _part(gu, f, f)], [], [(f, BF16)])
        x3 = _mm("mm_down", act, wl["down"], add=x2)
        sv.update(h1=h1, za=za, zqkv=zqkv, zcg=zcg, zf=zf, bf=bf, crow=crow, o=o, pa=pa, u2=u2, u3=u3, ya=ya, yb=yb,
                  yc=yc, bg=bg, gate_rows=gate_rows, mg=mg, x1=x1, h2=h2, qx=qx, kv=kv, ox=ox, x2=x2, h3=h3, gu=gu, act=act)
        saved.append(sv)
        x = x3

    loss_row, dx, d_final = _loss_call(x, tgt, sp["final_norm"][None])

    gs = {k: [None] * depth for k in ("mix_norm", "b_gate", "b_forget", "conv_a", "conv_c", "conv_c_bias", "ln_c_gain",
                                      "ln_c_bias", "xattn_norm", "ffn_norm")}
    gw = {k: [None] * depth for k in ("main", "f", "out_a", "out_b", "out_c", "o", "xq", "xkv", "xo", "gu", "down")}
    def weight_grad(key, l, name, a_op, b_op):
        if dests is None or key not in dests:
            gw[key][l] = _mm(name, a_op, b_op, ta=True, out_dtype=BF16)
        else:
            shape, shards = dests[key]
            gw[key][l] = _mm(name, a_op, b_op, ta=True, out_dtype=BF16, dest=(shape, None, (0,), shards))

    dmemn = None
    for l in reversed(range(depth)):
        sv, wl = saved[l], layer_weights[l]
        dact = _mm("mm_down_dx", dx, wl["down"], tb=True)
        weight_grad("down", l, "mm_down_dw", sv["act"], dx)
        (dg, du), _ = _rows_vjp("swiglu_bwd", _swiglu_fn, [_part(sv["gu"], 0, f), _part(sv["gu"], f, f)], [], [_part(dact)],
                                [BF16, BF16], [])
        dgu = [_part(dg), _part(du)]
        dh3 = _mm("mm_gu_dx", dgu, wl["gu"], tb=True)
        weight_grad("gu", l, "mm_gu_dw", sv["h3"], dgu)
        (dx2,), (gs["ffn_norm"][l],) = _rows_vjp("rms_ffn_bwd", _rms_fn, [_part(sv["x2"])], [row(sp["ffn_norm"], l)],
                                                 [_part(dh3)], [F32], [True], adds={0: _part(dx)})
        dox = _mm("mm_xo_dx", dx2, wl["xo"], tb=True)
        weight_grad("xo", l, "mm_xo_dw", sv["ox"], dx2)
        dqx, dkx, dvx = _xattn_bwd(sv["qx"], sv["kv"], dox)
        dh2 = _mm("mm_xq_dx", dqx, wl["xq"], tb=True)
        weight_grad("xq", l, "mm_xq_dw", sv["h2"], dqx)
        dkv = [_part(dkx), _part(dvx)]
        dmemn = _mm("mm_xkv_dx", dkv, wl["xkv"], tb=True, add=dmemn)
        weight_grad("xkv", l, "mm_xkv_dw", memn, dkv)
        (dx1,), (gs["xattn_norm"][l],) = _rows_vjp("rms_x_bwd", _rms_fn, [_part(sv["x1"])], [row(sp["xattn_norm"], l)],
                                                   [_part(dh2)], [F32], [True], adds={0: _part(dx2)})
        dmg = _mm("mm_o_dx", dx1, wl["o"], tb=True)
        weight_grad("o", l, "mm_o_dw", sv["mg"], dx1)
        (dga, dgb, dgc, dya, dyb, dyc), dbg = _rows_vjp("merge_bwd", _merge_fn, sv["gate_rows"], sv["bg"], [_part(dmg)],
                                                        [BF16] * 6, [True] * 3)
        gs["b_gate"][l] = jnp.concatenate(dbg, axis=1)
        dpa = _mm("mm_out_a_dx", dya, wl["out_a"], tb=True)
        weight_grad("out_a", l, "mm_out_a_dw", sv["pa"], dya)
        do = _mm("mm_out_b_dx", dyb, wl["out_b"], tb=True)
        weight_grad("out_b", l, "mm_out_b_dw", sv["o"], dyb)
        du3 = _mm("mm_out_c_dx", dyc, wl["out_c"], tb=True)
        weight_grad("out_c", l, "mm_out_c_dw", sv["u3"], dyc)
        (du2,), (gs["ln_c_gain"][l], gs["ln_c_bias"][l]) = _rows_vjp(
            "ln_silu_bwd", _lnsilu_fn, [_part(sv["u2"])], [row(sp["ln_c_gain"], l), row(sp["ln_c_bias"], l)], [_part(du3)],
            [F32], [True, True])
        dcv, dcg, gs["conv_c"][l], gs["conv_c_bias"][l] = _conv_c_bwd(sv["zcg"], sp["conv_c"][l], du2)
        dab, dac, dau, gs["conv_a"][l] = _conv_a_bwd(sv["za"], sp["conv_a"][l], dpa)
        dq, dk, dv, dcr = _fox_bwd(sv["zqkv"], sv["crow"], do)
        dzf, dbf = _fox_cumsum_bwd(sv["zf"], sv["bf"], _pad_lanes(dcr.reshape(heads, t).T))
        gs["b_forget"][l] = dbf[:, :heads]
        dz = [_part(a) for a in (dab, dac, dau, dq, dk, dv, dcv, dcg, dga, dgb, dgc)]
        dh1f = _mm("mm_in_f_dx", dzf, wl["f"], tb=True)
        dh1 = _mm("mm_in_dx", dz, wl["main"], tb=True, add=dh1f)
        gw["main"][l] = [_mm("mm_in_dw_lo", sv["h1"], dz[:6], ta=True, out_dtype=BF16),
                         _mm("mm_in_dw_hi", sv["h1"], dz[6:], ta=True, out_dtype=BF16)]
        gw["f"][l] = _mm("mm_in_f_dw", sv["h1"], dzf, ta=True, out_dtype=BF16)
        (dx,), (gs["mix_norm"][l],) = _rows_vjp("rms_mix_bwd", _rms_fn, [_part(sv["x0"])], [row(sp["mix_norm"], l)],
                                                [_part(dh1)], [F32], [True], adds={0: _part(dx1)})
        if on_layer is not None:
            token = on_layer(l, {k: v[l] for k, v in gw.items()})
            if l > 0:
                dx = dx + token[0, 0]

    _, (d_mem_norm,) = _rows_vjp("rms_mem_bwd", _rms_fn, [_part(mem)], [sp["mem_norm"][None]], [_part(dmemn)], [None], [True])
    small = {k: jnp.stack([a.reshape(sp[k].shape[1:]) for a in v]) for k, v in gs.items()}
    small["mem_norm"] = d_mem_norm[0]
    small["final_norm"] = d_final[0]
    return loss_row, dx, small, gw


_BIG = ("w_in", "w_out_a", "w_out_b", "w_out_c", "w_o", "w_xq", "w_xkv", "w_xo", "w_gate_up", "w_down")
_COL_SHARDED = ("w_in", "w_xo", "w_gate_up")
_KEY = {"w_out_a": "out_a", "w_out_b": "out_b", "w_out_c": "out_c", "w_o": "o", "w_xq": "xq", "w_xkv": "xkv", "w_xo": "xo",
        "w_gate_up": "gu", "w_down": "down"}
_SMALL = ("mix_norm", "b_gate", "b_forget", "conv_c_bias", "ln_c_gain", "ln_c_bias", "xattn_norm", "ffn_norm", "mem_norm",
          "final_norm")
_SMALL_SHARDED = ("conv_a", "conv_c")


def _pack_rows(parts):
    padded = []
    for a in parts:
        pad = -a.shape[0] % SUBLANES
        padded.append(jnp.pad(a, ((0, pad), (0, 0))) if pad else a)
    return jnp.concatenate(padded, axis=0)


def _unpack_rows(packed, shapes):
    out, pos = [], 0
    for r in shapes:
        out.append(packed[pos:pos + r])
        pos += r + (-r % SUBLANES)
    return out


def kernel(x, mem, mix_norm, w_in, b_gate, b_forget, conv_a, w_out_a, w_out_b, conv_c, conv_c_bias, ln_c_gain, ln_c_bias, w_out_c, w_o, xattn_norm, mem_norm, w_xq, w_xkv, w_xo, ffn_norm, w_gate_up, w_down, final_norm, loss_target, m_mix_norm, m_w_in, m_b_gate, m_b_forget, m_conv_a, m_w_out_a, m_w_out_b, m_conv_c, m_conv_c_bias, m_ln_c_gain, m_ln_c_bias, m_w_out_c, m_w_o, m_xattn_norm, m_mem_norm, m_w_xq, m_w_xkv, m_w_xo, m_ffn_norm, m_w_gate_up, m_w_down, m_final_norm, v_mix_norm, v_w_in, v_b_gate, v_b_forget, v_conv_a, v_w_out_a, v_w_out_b, v_conv_c, v_conv_c_bias, v_ln_c_gain, v_ln_c_bias, v_w_out_c, v_w_o, v_xattn_norm, v_mem_norm, v_w_xq, v_w_xkv, v_w_xo, v_ffn_norm, v_w_gate_up, v_w_down, v_final_norm):
    args = dict(locals())
    names = ["mix_norm", "w_in", "b_gate", "b_forget", "conv_a", "w_out_a", "w_out_b", "conv_c", "conv_c_bias", "ln_c_gain",
             "ln_c_bias", "w_out_c", "w_o", "xattn_norm", "mem_norm", "w_xq", "w_xkv", "w_xo", "ffn_norm", "w_gate_up", "w_down",
             "final_norm"]
    depth, d = mix_norm.shape
    heads = b_forget.shape[1]
    cx, cy, cc = lax.axis_index("x"), lax.axis_index("y"), lax.axis_index("c")
    chip = 2 * cx + cy
    c_idx = jnp.reshape(cc, (1,)).astype(jnp.int32)
    chip_idx = jnp.reshape(chip, (1,)).astype(jnp.int32)

    conv_rows = [conv_a.shape[1] * depth, conv_c.shape[1] * depth]
    conv_pack = _pack_rows([conv_a.reshape(conv_rows[0], -1), conv_c.reshape(conv_rows[1], -1)])
    conv_all = _allgather8(conv_pack).reshape(N_CHIPS, 2, conv_pack.shape[0], conv_pack.shape[1])[:, 0]
    conv_all = conv_all.transpose(1, 0, 2).reshape(conv_pack.shape[0], d)
    conv_a_full, conv_c_full = _unpack_rows(conv_all, conv_rows)
    own = [args[n].astype(BF16) for n in _BIG]
    first = _gather_weights(own, 0, conv_all)
    started, tokens = {}, []
    for l in range(1, depth):
        send, recv, thru, lands, token = _gather_start(own, l, first[0])
        started[l] = (send, recv, thru, lands)
        tokens.append(token)

    def layer_weights(l, x_in):
        if l == 0:
            lands = first
        else:
            _, lands = _gather_wait(*started[l], x_in, l)
        full = [lax.dynamic_update_slice(g, w[l][None], (chip, 0, 0)) for g, w in zip(lands, own)]
        wl = {}
        wl["main"], wl["f"] = _w_in_aligned(full[0], d, heads)
        for n, g in zip(_BIG[1:], full[1:]):
            if n in _COL_SHARDED:
                wl[_KEY[n]] = [_part(g, lead=(j,)) for j in range(N_CHIPS)]
            else:
                wl[_KEY[n]] = g.reshape(N_CHIPS * g.shape[1], g.shape[2])
        return wl

    shard_shape = {n: (1, N_CHIPS) + args[n].shape[1:] for n in _BIG}
    dests = {_KEY[n]: (shard_shape[n], N_CHIPS) if n in _COL_SHARDED
             else ((1, N_CHIPS * args[n].shape[1], args[n].shape[2]), 1) for n in _BIG[1:]}
    sp = {n: args[n] for n in _SMALL}
    sp["conv_a"] = conv_a_full.reshape(depth, conv_a.shape[1], d)
    sp["conv_c"] = conv_c_full.reshape(depth, conv_c.shape[1], d)

    in_flight = {}

    def on_layer(l, gl):
        local = [_w_in_shards(gl["main"], gl["f"], None, 0, shard_shape["w_in"], heads)]
        local += [gl[_KEY[n]].reshape(shard_shape[n]) for n in _BIG[1:]]
        chip_sums = [_add_sibling(g, r, c_idx) for g, r in zip(local, _swap_halves(local))]
        send, recv, chip_sums, lands, token = _exchange_start(chip_sums, l)
        in_flight[l] = (send, recv, chip_sums, lands)
        return token

    x_in = x[0]
    for token in tokens:
        x_in = x_in + token[0, 0]
    loss_row, grad_x, gsmall, gw = _local_step(x_in, mem[0], loss_target[0], sp, layer_weights, dests, on_layer)
    halves = [None] * len(_BIG)
    for l in reversed(range(depth)):
        chip_sums, lands = _exchange_wait(*in_flight[l], grad_x, l)
        halves = [_add_chips(s_, r, chip_idx, c_idx, l, depth, h) for s_, r, h in zip(chip_sums, lands, halves)]
    grads = dict(zip(_BIG, _join_halves(halves)))

    def as_rows(n, a):
        return _pad_lanes(a, d) if n == "b_forget" else a.reshape(-1, d)

    pieces = [as_rows(n, gsmall[n]) for n in _SMALL]
    for n in _SMALL_SHARDED:
        pieces.append(gsmall[n].reshape(-1, d))
    pieces.append(_pad_lanes(loss_row, d))
    pack = _pack_rows(pieces)
    total = _sum_devices(_allgather8(pack), pack.shape[0])
    shapes = [p.shape[0] for p in pieces]
    summed = _unpack_rows(total, shapes)
    for n, g in zip(_SMALL, summed):
        grads[n] = g[:, :heads] if n == "b_forget" else g.reshape(args[n].shape)
    for n, g in zip(_SMALL_SHARDED, summed[len(_SMALL):]):
        g = g.reshape(args[n].shape[:2] + (d,))
        grads[n] = lax.dynamic_slice_in_dim(g, chip * args[n].shape[2], args[n].shape[2], axis=2)
    loss = summed[-1][0, 0]

    delta, new_m, new_v = {}, {}, {}
    packed = [_pack_rows([as_rows(n, src[pre + n]) for n in _SMALL])
              for src, pre in ((args, ""), (grads, ""), (args, "m_"), (args, "v_"))]
    srows = [as_rows(n, args[n]).shape[0] for n in _SMALL]
    updated = [_unpack_rows(a, srows) for a in _adamw(*packed)]
    for k, n in enumerate(_SMALL):
        delta[n], new_m[n], new_v[n] = (
            (u[k][:, :heads] if n == "b_forget" else u[k]).reshape(args[n].shape) for u in updated)
    for n in _BIG + _SMALL_SHARDED:
        if n == "w_in":
            to_t, from_t = (lambda a: jnp.transpose(a, (2, 0, 1))), (lambda a: jnp.transpose(a, (1, 2, 0)))
            g_t = to_t(grads[n])
            grads[n] = from_t(g_t)
            delta[n], new_m[n], new_v[n] = (from_t(a) for a in _adamw(to_t(args[n]), g_t, to_t(args["m_" + n]), to_t(args["v_" + n]),
                                                                     keep_rank=True))
            continue
        delta[n], new_m[n], new_v[n] = _adamw(args[n], grads[n], args["m_" + n], args["v_" + n])

    return (loss, grad_x[None], *[grads[n] for n in names], *[delta[n] for n in names], *[new_m[n] for n in names],
            *[new_v[n] for n in names])
```

```python
import functools
import math

import jax
import jax.numpy as jnp
from jax import lax
from jax.experimental import pallas as pl
from jax.experimental.pallas import tpu as pltpu

F32, BF16 = jnp.float32, jnp.bfloat16
EPS = 1e-6
LANES = 128
SUBLANES = 8
VMEM_LIMIT = 56 * 1024 * 1024
MM_VMEM_BUDGET = 44 * 1024 * 1024
ADD_BLOCK_BYTES = 4 * 1024 * 1024
MESH = pl.DeviceIdType.MESH
N_CHIPS = 4
N_DEV = 8

ADAM_LR, ADAM_B1, ADAM_B2, ADAM_EPS, ADAM_WD, ADAM_STEP = 0.001, 0.9, 0.999, 1e-08, 0.01, 10


def _params(**kw):
    return pltpu.CompilerParams(vmem_limit_bytes=VMEM_LIMIT, **kw)


def _sds(shape, dtype):
    return jax.ShapeDtypeStruct(tuple(shape), dtype)


def _part(arr, off=0, width=None, lead=()):
    assert arr.ndim == len(lead) + 2
    return (arr, off, arr.shape[-1] - off if width is None else width, tuple(lead))


def _unit(parts):
    u = 0
    for _, off, w, _ in parts:
        u = math.gcd(u, math.gcd(off, w))
    return u


def _pick(dim, unit, cands, lane):
    for c in cands:
        if c <= dim and dim % c == 0 and unit % c == 0 and (not lane or c % LANES == 0):
            return c
    return min(dim, unit)


_COL_TILES = (1024, 1408, 512, 256, 128)
_ROW_TILES = (512, 256, 1408, 128)


def _mm(name, a, b, *, ta=False, tb=False, out_dtype=F32, add=None, dest=None):
    if not isinstance(a, list):
        a = [_part(a)]
    if not isinstance(b, list):
        b = [_part(b)]
    a_rows, a_cols = a[0][0].shape[-2], sum(p[2] for p in a)
    b_rows, b_cols = b[0][0].shape[-2], sum(p[2] for p in b)
    M, Ka = (a_cols, a_rows) if ta else (a_rows, a_cols)
    Kb, N = (b_cols, b_rows) if tb else (b_rows, b_cols)
    assert Ka == Kb, (name, Ka, Kb)
    K = Ka
    ua, ub = _unit(a), _unit(b)
    shards = 1 if dest is None else dest[3]
    tm = _pick(M, ua if ta else M, _ROW_TILES, ta)
    tn = _pick(N, math.gcd(N if tb else ub, N // shards), _COL_TILES, not tb)
    ku = math.gcd(K if ta else ua, ub if tb else K)
    tk = _pick(K, ku, _COL_TILES, (not ta) or tb)
    a_size = sum(p[0].dtype.itemsize for p in a)
    b_size = sum(p[0].dtype.itemsize for p in b)

    def vmem_estimate():
        return (2 * tm * tk * a_size + 2 * tk * tn * b_size + tm * tn * (2 * jnp.dtype(out_dtype).itemsize + 4)
                + (8 * tm * tn if add is not None else 0))

    while vmem_estimate() > MM_VMEM_BUDGET:
        if tk % 256 == 0 and tk >= tn:
            tk //= 2
        elif tn % 256 == 0:
            tn //= 2
        elif tm % 256 == 0:
            tm //= 2
        elif tk % 256 == 0:
            tk //= 2
        else:
            break
    nm, nn, nk = M // tm, N // tn, K // tk
    a_tile, b_tile = (tm if ta else tk), (tk if tb else tn)
    a_nb, b_nb = a[0][2] // a_tile, b[0][2] // b_tile

    def index(parts, nb, tile, p, row, col):
        base = parts[p][1] // tile
        if len(parts) == 1:
            return parts[p][3] + (row, base + col)
        inside = (col // nb) == p
        return parts[p][3] + (jnp.where(inside, row, 0), base + jnp.clip(col - p * nb, 0, nb - 1))

    def spec(parts, p, shape, fn):
        return pl.BlockSpec((None,) * len(parts[p][3]) + shape, fn)

    in_specs, operands = [], []
    for p in range(len(a)):
        if ta:
            in_specs.append(spec(a, p, (tk, tm), lambda mi, nj, kk, p=p: index(a, a_nb, a_tile, p, kk, mi)))
        else:
            in_specs.append(spec(a, p, (tm, tk), lambda mi, nj, kk, p=p: index(a, a_nb, a_tile, p, mi, kk)))
        operands.append(a[p][0])
    for p in range(len(b)):
        if tb:
            in_specs.append(spec(b, p, (tn, tk), lambda mi, nj, kk, p=p: index(b, b_nb, b_tile, p, nj, kk)))
        else:
            in_specs.append(spec(b, p, (tk, tn), lambda mi, nj, kk, p=p: index(b, b_nb, b_tile, p, kk, nj)))
        operands.append(b[p][0])
    if add is not None:
        in_specs.append(pl.BlockSpec((tm, tn), lambda mi, nj, kk: (mi, nj)))
        operands.append(add)
    aliases = {}
    if dest is None:
        out_spec = pl.BlockSpec((tm, tn), lambda mi, nj, kk: (mi, nj))
        out_shape = _sds((M, N), out_dtype)
    else:
        d_shape, d_prev, d_lead, _ = dest
        nbs = N // shards // tn
        if shards == 1:
            out_spec = pl.BlockSpec((None,) * len(d_lead) + (tm, tn), lambda mi, nj, kk: tuple(d_lead) + (mi, nj))
        else:
            out_spec = pl.BlockSpec((None,) * (len(d_lead) + 1) + (tm, tn),
                                    lambda mi, nj, kk: tuple(d_lead) + (nj // nbs, mi, nj % nbs))
        out_shape = _sds(d_shape, out_dtype)
        if d_prev is not None:
            aliases = {len(operands): 0}
            in_specs.append(pl.BlockSpec(memory_space=pl.ANY))
            operands.append(d_prev)
    n_extra = 1 if aliases else 0
    dims = (((0 if ta else 1,), (1 if tb else 0,)), ((), ()))
    na, nb_ = len(a), len(b)

    def body(*refs):
        a_refs, b_refs = refs[:na], refs[na:na + nb_]
        pos = na + nb_
        add_ref = None
        if add is not None:
            add_ref = refs[pos]
            pos += 1
        pos += n_extra
        o_ref = refs[pos]
        acc_ref = refs[pos + 1] if nk > 1 else None
        mi, nj, kk = pl.program_id(0), pl.program_id(1), pl.program_id(2)
        a_col = mi if ta else kk
        b_col = kk if tb else nj

        def finish(r):
            if add_ref is not None:
                r = r + add_ref[...].astype(F32)
            o_ref[...] = r.astype(o_ref.dtype)

        def step(a_ref, b_ref):
            r = lax.dot_general(a_ref[...].astype(BF16), b_ref[...].astype(BF16), dims, preferred_element_type=F32)
            if nk == 1:
                finish(r)
            else:
                @pl.when(kk == 0)
                def _():
                    acc_ref[...] = r

                @pl.when(kk > 0)
                def _():
                    acc_ref[...] += r

        for pa in range(na):
            for pb in range(nb_):
                if na * nb_ == 1:
                    step(a_refs[pa], b_refs[pb])
                else:
                    cond = jnp.logical_and(a_col // a_nb == pa, b_col // b_nb == pb)
                    pl.when(cond)(functools.partial(step, a_refs[pa], b_refs[pb]))
        if nk > 1:
            @pl.when(kk == nk - 1)
            def _():
                finish(acc_ref[...])

    return pl.pallas_call(
        body, grid=(nm, nn, nk), in_specs=in_specs, out_specs=out_spec, out_shape=out_shape, input_output_aliases=aliases,
        scratch_shapes=[pltpu.VMEM((tm, tn), F32)] if nk > 1 else [],
        compiler_params=_params(dimension_semantics=("parallel", "parallel", "arbitrary")),
        name=name)(*operands)


def _row_spec(tb, part):
    _, off, w, _ = part
    return pl.BlockSpec((tb, w), lambda i, o=off // w: (i, o))


def _full_spec(arr):
    return pl.BlockSpec(arr.shape, lambda i: (0,) * arr.ndim)


def _rows(name, fn, rows, pars, outs, tb=256):
    n = rows[0][0].shape[0]
    tb = min(tb, n)
    nr, npar = len(rows), len(pars)

    def body(*refs):
        r = [ref[...].astype(F32) for ref in refs[:nr]]
        p = [ref[...].astype(F32) for ref in refs[nr:nr + npar]]
        res = fn(*r, *p)
        for o_ref, v in zip(refs[nr + npar:], res):
            o_ref[...] = v.astype(o_ref.dtype)

    out = pl.pallas_call(
        body, grid=(n // tb,),
        in_specs=[_row_spec(tb, p) for p in rows] + [_full_spec(p) for p in pars],
        out_specs=[pl.BlockSpec((tb, w), lambda i: (i, 0)) for w, _ in outs],
        out_shape=[_sds((n, w), dt) for w, dt in outs],
        compiler_params=_params(), name=name)(*[p[0] for p in rows], *pars)
    return out


def _rows_vjp(name, fn, rows, pars, cots, row_grads, par_grads, adds=None, tb=256):
    adds = adds or {}
    n = rows[0][0].shape[0]
    tb = min(tb, n)
    nr, npar, nc = len(rows), len(pars), len(cots)
    add_keys = sorted(adds)
    rg = [j for j, dt in enumerate(row_grads) if dt is not None]
    pg = [j for j, f in enumerate(par_grads) if f]

    def body(*refs):
        pos = 0
        r = [ref[...].astype(F32) for ref in refs[pos:pos + nr]]
        pos += nr
        p = [ref[...].astype(F32) for ref in refs[pos:pos + npar]]
        pos += npar
        ct = tuple(ref[...].astype(F32) for ref in refs[pos:pos + nc])
        pos += nc
        add_v = {k: ref[...].astype(F32) for k, ref in zip(add_keys, refs[pos:pos + len(add_keys)])}
        pos += len(add_keys)
        _, vjp = jax.vjp(lambda *args: tuple(fn(*args)), *r, *p)
        g = vjp(ct)
        for j in rg:
            v = g[j]
            if j in add_v:
                v = v + add_v[j]
            refs[pos][...] = v.astype(refs[pos].dtype)
            pos += 1
        first = pl.program_id(0) == 0
        for j in pg:
            ref = refs[pos]
            pos += 1

            @pl.when(first)
            def _(ref=ref):
                ref[...] = jnp.zeros_like(ref)

            ref[...] += g[nr + j]

    in_specs = ([_row_spec(tb, p) for p in rows] + [_full_spec(p) for p in pars] + [_row_spec(tb, p) for p in cots]
                + [_row_spec(tb, adds[k]) for k in add_keys])
    out_specs = ([pl.BlockSpec((tb, rows[j][2]), lambda i: (i, 0)) for j in rg] + [_full_spec(pars[j]) for j in pg])
    out_shape = ([_sds((n, rows[j][2]), row_grads[j]) for j in rg] + [_sds(pars[j].shape, F32) for j in pg])
    out = pl.pallas_call(
        body, grid=(n // tb,), in_specs=in_specs, out_specs=out_specs, out_shape=out_shape,
        compiler_params=_params(), name=name)(
            *[p[0] for p in rows], *pars, *[p[0] for p in cots], *[adds[k][0] for k in add_keys])
    return out[:len(rg)], out[len(rg):]


def _rms_fn(x, g):
    return (x * lax.rsqrt(jnp.mean(x * x, axis=-1, keepdims=True) + EPS) * g,)


def _lnsilu_fn(u, g, b):
    mu = jnp.mean(u, axis=-1, keepdims=True)
    xc = u - mu
    y = xc * lax.rsqrt(jnp.mean(xc * xc, axis=-1, keepdims=True) + EPS) * g + b
    return (y * jax.nn.sigmoid(y),)


def _merge_fn(ga, gb, gc, ya, yb, yc, ba, bb, bc):
    return (jax.nn.sigmoid(ga + ba) * ya + jax.nn.sigmoid(gb + bb) * yb + jax.nn.sigmoid(gc + bc) * yc,)


def _swiglu_fn(g, u):
    return (g * jax.nn.sigmoid(g) * u,)


def _loss_call(x, tgt, gain):
    n, d = x.shape
    tb = min(256, n)

    def body(x_ref, t_ref, g_ref, l_ref, dx_ref, dg_ref):
        (y,), vjp = jax.vjp(lambda a, b: tuple(_rms_fn(a, b)), x_ref[...], g_ref[...])
        e = y - t_ref[...]
        part = 0.5 * jnp.sum(jnp.mean(e * e, axis=-1, keepdims=True))
        dx, dg = vjp((e * (1.0 / d),))
        dx_ref[...] = dx

        @pl.when(pl.program_id(0) == 0)
        def _():
            l_ref[...] = jnp.zeros_like(l_ref)
            dg_ref[...] = jnp.zeros_like(dg_ref)

        l_ref[...] += jnp.full(l_ref.shape, part, F32)
        dg_ref[...] += dg

    return pl.pallas_call(
        body, grid=(n // tb,),
        in_specs=[pl.BlockSpec((tb, d), lambda i: (i, 0)), pl.BlockSpec((tb, d), lambda i: (i, 0)), _full_spec(gain)],
        out_specs=[pl.BlockSpec((1, LANES), lambda i: (0, 0)), pl.BlockSpec((tb, d), lambda i: (i, 0)), _full_spec(gain)],
        out_shape=[_sds((1, LANES), F32), _sds((n, d), F32), _sds(gain.shape, F32)],
        compiler_params=_params(), name="loss_head")(x, tgt, gain)


FOX_BLOCK = 256
CONV_ROWS = 64
CONV_HALO = 32


def _chunk(i):
    return pl.ds(pl.multiple_of(i * CONV_ROWS, CONV_ROWS), CONV_ROWS)


def _delayed(ext, s):
    return (ext if s == 0 else pltpu.roll(ext, s, 0))[CONV_HALO:]


def _advanced(ext, s):
    return (ext if s == 0 else pltpu.roll(ext, ext.shape[0] - s, 0))[:CONV_ROWS]


def _conv_pass(t, taps, w_ref, lanes, get_u, emit, get_dy=None, dw_acc=None):
    def body(i, carry):
        rows = _chunk(i)
        u = get_u(rows)
        ext = jnp.concatenate([carry, u], axis=0)
        dy = None if get_dy is None else get_dy(rows)
        acc = None
        for k in range(taps):
            sh = _delayed(ext, taps - 1 - k)
            term = w_ref[k:k + 1, lanes] * sh
            acc = term if acc is None else acc + term
            if dy is not None:
                dw_acc[k] += jnp.sum((dy * sh).reshape(CONV_ROWS // SUBLANES, SUBLANES, LANES), axis=0)
        emit(rows, u, acc)
        return u[CONV_ROWS - CONV_HALO:]

    lax.fori_loop(0, t // CONV_ROWS, body, jnp.zeros((CONV_HALO, LANES), F32))


def _conv_pass_t(t, taps, w_ref, lanes, get_dy, emit):
    n = t // CONV_ROWS

    def body(j, carry):
        rows = _chunk(n - 1 - j)
        dy = get_dy(rows)
        ext = jnp.concatenate([dy, carry], axis=0)
        acc = None
        for k in range(taps):
            term = w_ref[k:k + 1, lanes] * _advanced(ext, taps - 1 - k)
            acc = term if acc is None else acc + term
        emit(rows, acc)
        return dy[:CONV_HALO]

    lax.fori_loop(0, n, body, jnp.zeros((CONV_HALO, LANES), F32))


def _chan_block(d):
    return min(256, d)


def _lane_groups(cb):
    return [slice(g * LANES, (g + 1) * LANES) for g in range(cb // LANES)]


def _conv_a_fwd(za, w):
    t, d = za.shape[0], za.shape[1] // 3
    taps, cb = w.shape[0], _chan_block(d)
    nb = d // cb
    assert taps - 1 <= CONV_HALO and t % CONV_ROWS == 0

    def body(ab_ref, ac_ref, au_ref, w_ref, o_ref):
        for lanes in _lane_groups(cb):
            def emit(rows, u, s, lanes=lanes):
                o_ref[rows, lanes] = (ab_ref[rows, lanes].astype(F32) * s).astype(o_ref.dtype)

            _conv_pass(t, taps, w_ref, lanes, lambda rows, lanes=lanes: ac_ref[rows, lanes].astype(F32) * au_ref[rows, lanes].astype(F32), emit)

    return pl.pallas_call(
        body, grid=(nb,),
        in_specs=[pl.BlockSpec((t, cb), lambda j, o=o: (0, o * nb + j)) for o in range(3)]
        + [pl.BlockSpec((taps, cb), lambda j: (0, j))],
        out_specs=pl.BlockSpec((t, cb), lambda j: (0, j)),
        out_shape=_sds((t, d), BF16), compiler_params=_params(), name="conv_a_fwd")(za, za, za, w)


def _finish_taps(dw_ref, dw_acc, lanes, taps):
    for k in range(taps):
        dw_ref[k:k + 1, lanes] = jnp.sum(dw_acc[k], axis=0, keepdims=True)


def _conv_a_bwd(za, w, dpa):
    t, d = za.shape[0], za.shape[1] // 3
    taps, cb = w.shape[0], _chan_block(d)
    nb = d // cb

    def body(ab_ref, ac_ref, au_ref, w_ref, dy_ref, dab_ref, dac_ref, dau_ref, dw_ref, dw_acc):
        for lanes in _lane_groups(cb):
            dw_acc[...] = jnp.zeros_like(dw_acc)

            def ds(rows, lanes=lanes):
                return dy_ref[rows, lanes].astype(F32) * ab_ref[rows, lanes].astype(F32)

            def emit_s(rows, u, s, lanes=lanes):
                dab_ref[rows, lanes] = (dy_ref[rows, lanes].astype(F32) * s).astype(dab_ref.dtype)

            def emit_du(rows, du, lanes=lanes):
                dac_ref[rows, lanes] = (du * au_ref[rows, lanes].astype(F32)).astype(dac_ref.dtype)
                dau_ref[rows, lanes] = (du * ac_ref[rows, lanes].astype(F32)).astype(dau_ref.dtype)

            _conv_pass(t, taps, w_ref, lanes, lambda rows, lanes=lanes: ac_ref[rows, lanes].astype(F32) * au_ref[rows, lanes].astype(F32), emit_s,
                       get_dy=ds, dw_acc=dw_acc)
            _conv_pass_t(t, taps, w_ref, lanes, ds, emit_du)
            _finish_taps(dw_ref, dw_acc, lanes, taps)

    blk = pl.BlockSpec((t, cb), lambda j: (0, j))
    return pl.pallas_call(
        body, grid=(nb,),
        in_specs=[pl.BlockSpec((t, cb), lambda j, o=o: (0, o * nb + j)) for o in range(3)]
        + [pl.BlockSpec((taps, cb), lambda j: (0, j)), blk],
        out_specs=[blk, blk, blk, pl.BlockSpec((taps, cb), lambda j: (0, j))],
        out_shape=[_sds((t, d), BF16)] * 3 + [_sds((taps, d), F32)],
        scratch_shapes=[pltpu.VMEM((taps, SUBLANES, LANES), F32)],
        compiler_params=_params(), name="conv_a_bwd")(za, za, za, w, dpa)


def _conv_c_fwd(zcg, w, bias):
    t, d = zcg.shape[0], w.shape[1]
    taps, cb = w.shape[0], _chan_block(d)
    nb = d // cb
    assert taps - 1 <= CONV_HALO and t % CONV_ROWS == 0

    def body(cv_ref, cg_ref, w_ref, b_ref, o_ref):
        for lanes in _lane_groups(cb):
            def emit(rows, u, s, lanes=lanes):
                o_ref[rows, lanes] = s + b_ref[:, lanes]

            _conv_pass(t, taps, w_ref, lanes,
                       lambda rows, lanes=lanes: cv_ref[rows, lanes].astype(F32) * jax.nn.sigmoid(cg_ref[rows, lanes].astype(F32)), emit)

    return pl.pallas_call(
        body, grid=(nb,),
        in_specs=[pl.BlockSpec((t, cb), lambda j, o=o: (0, o * nb + j)) for o in range(2)]
        + [pl.BlockSpec((taps, cb), lambda j: (0, j)), pl.BlockSpec((1, cb), lambda j: (0, j))],
        out_specs=pl.BlockSpec((t, cb), lambda j: (0, j)),
        out_shape=_sds((t, d), F32), compiler_params=_params(), name="conv_c_fwd")(zcg, zcg, w, bias)


def _conv_c_bwd(zcg, w, du2):
    t, d = zcg.shape[0], w.shape[1]
    taps, cb = w.shape[0], _chan_block(d)
    nb = d // cb

    def body(cv_ref, cg_ref, w_ref, dy_ref, dcv_ref, dcg_ref, dw_ref, db_ref, dw_acc, db_acc):
        for lanes in _lane_groups(cb):
            dw_acc[...] = jnp.zeros_like(dw_acc)
            db_acc[...] = jnp.zeros_like(db_acc)

            def dy(rows, lanes=lanes):
                return dy_ref[rows, lanes].astype(F32)

            def emit_s(rows, u, s, lanes=lanes):
                db_acc[...] += jnp.sum(dy_ref[rows, lanes].astype(F32).reshape(CONV_ROWS // SUBLANES, SUBLANES, LANES), axis=0)

            def emit_du(rows, du, lanes=lanes):
                cv, sg = cv_ref[rows, lanes].astype(F32), jax.nn.sigmoid(cg_ref[rows, lanes].astype(F32))
                dcv_ref[rows, lanes] = (du * sg).astype(dcv_ref.dtype)
                dcg_ref[rows, lanes] = (du * cv * sg * (1.0 - sg)).astype(dcg_ref.dtype)

            _conv_pass(t, taps, w_ref, lanes,
                       lambda rows, lanes=lanes: cv_ref[rows, lanes].astype(F32) * jax.nn.sigmoid(cg_ref[rows, lanes].astype(F32)), emit_s,
                       get_dy=dy, dw_acc=dw_acc)
            _conv_pass_t(t, taps, w_ref, lanes, dy, emit_du)
            _finish_taps(dw_ref, dw_acc, lanes, taps)
            db_ref[:, lanes] = jnp.sum(db_acc[...], axis=0, keepdims=True)

    blk = pl.BlockSpec((t, cb), lambda j: (0, j))
    return pl.pallas_call(
        body, grid=(nb,),
        in_specs=[pl.BlockSpec((t, cb), lambda j, o=o: (0, o * nb + j)) for o in range(2)]
        + [pl.BlockSpec((taps, cb), lambda j: (0, j)), blk],
        out_specs=[blk, blk, pl.BlockSpec((taps, cb), lambda j: (0, j)), pl.BlockSpec((1, cb), lambda j: (0, j))],
        out_shape=[_sds((t, d), BF16)] * 2 + [_sds((taps, d), F32), _sds((1, d), F32)],
        scratch_shapes=[pltpu.VMEM((taps, SUBLANES, LANES), F32), pltpu.VMEM((SUBLANES, LANES), F32)],
        compiler_params=_params(), name="conv_c_bwd")(zcg, zcg, w, du2)


def _tri(tb, t, i, lower):
    r = lax.broadcasted_iota(jnp.int32, (tb, t), 0) + i * tb
    c = lax.broadcasted_iota(jnp.int32, (tb, t), 1)
    return ((r >= c) if lower else (c >= r)).astype(F32)


def _fox_cumsum(zf, bf):
    t = zf.shape[0]
    tb = min(256, t)

    def body(z_ref, b_ref, c_ref):
        logf = jax.nn.log_sigmoid(z_ref[...] + b_ref[...])
        c_ref[...] = jnp.dot(_tri(tb, t, pl.program_id(0), True), logf, precision=lax.Precision.HIGHEST,
                             preferred_element_type=F32)

    return pl.pallas_call(
        body, grid=(t // tb,), in_specs=[_full_spec(zf), _full_spec(bf)],
        out_specs=pl.BlockSpec((tb, LANES), lambda i: (i, 0)), out_shape=_sds((t, LANES), F32),
        compiler_params=_params(), name="fox_cumsum")(zf, bf)


def _fox_cumsum_bwd(zf, bf, dc_t):
    t = zf.shape[0]
    tb = min(256, t)

    def body(z_ref, b_ref, dc_ref, dz_ref, db_ref):
        i = pl.program_id(0)

        @pl.when(i == 0)
        def _():
            db_ref[...] = jnp.zeros_like(db_ref)

        dlogf = jnp.dot(_tri(tb, t, i, False), dc_ref[...], precision=lax.Precision.HIGHEST, preferred_element_type=F32)
        dz = dlogf * jax.nn.sigmoid(-(z_ref[...] + b_ref[...]))
        dz_ref[...] = dz.astype(dz_ref.dtype)
        db_ref[...] += jnp.sum(dz, axis=0, keepdims=True)

    return pl.pallas_call(
        body, grid=(t // tb,),
        in_specs=[pl.BlockSpec((tb, LANES), lambda i: (i, 0)), _full_spec(bf), _full_spec(dc_t)],
        out_specs=[pl.BlockSpec((tb, LANES), lambda i: (i, 0)), pl.BlockSpec((1, LANES), lambda i: (0, 0))],
        out_shape=[_sds((t, LANES), BF16), _sds((1, LANES), F32)],
        compiler_params=_params(), name="fox_cumsum_bwd")(zf, bf, dc_t)


def _fox_block(q, k, v, cr, *, q0, hd):
    tq, kext = q.shape[0], k.shape[0]
    nh = LANES // hd
    lane = lax.broadcasted_iota(jnp.int32, (1, LANES), 1)
    causal = (lax.broadcasted_iota(jnp.int32, (tq, kext), 0) + q0) >= lax.broadcasted_iota(jnp.int32, (tq, kext), 1)
    kb, vb = k.astype(BF16), v.astype(BF16)
    out = jnp.zeros((tq, LANES), F32)
    for h in range(nh):
        m = (lane // hd) == h
        qh = jnp.where(m, q * (hd ** -0.5), 0.0).astype(BF16)
        s = lax.dot_general(qh, kb, (((1,), (1,)), ((), ())), preferred_element_type=F32)
        c_k = cr[h] - lax.stop_gradient(cr[h][:, q0:q0 + 1])
        s = jnp.where(causal, s - c_k, -1e30)
        e = jnp.exp(s - lax.stop_gradient(jnp.max(s, axis=1, keepdims=True)))
        prob = e / jnp.sum(e, axis=1, keepdims=True)
        o = jnp.dot(prob.astype(BF16), vb, preferred_element_type=F32)
        out = out + jnp.where(m, o, 0.0)
    return out


def _fox_dims(zqkv, crow):
    t, d = zqkv.shape[0], zqkv.shape[1] // 3
    hd = d // crow.shape[0]
    assert math.log2(hd) % 2 == 0
    return t, d, hd, d // LANES, LANES // hd, min(FOX_BLOCK, t)


def _fox_specs(i, tq, nblk, nh):
    kext = (i + 1) * tq
    return [pl.BlockSpec((tq, LANES), lambda p: (i, p)), pl.BlockSpec((kext, LANES), lambda p: (0, nblk + p)),
            pl.BlockSpec((kext, LANES), lambda p: (0, 2 * nblk + p)), pl.BlockSpec((nh, 1, kext), lambda p: (p, 0, 0))]


def _fox_fwd(zqkv, crow):
    t, d, hd, nblk, nh, tq = _fox_dims(zqkv, crow)
    o = None
    for i in range(t // tq):
        def body(q_ref, k_ref, v_ref, cr_ref, *rest, i=i):
            o_ref = rest[-1]
            o_ref[...] = _fox_block(q_ref[...].astype(F32), k_ref[...], v_ref[...], cr_ref[...],
                                    q0=i * tq, hd=hd).astype(o_ref.dtype)

        prev = [] if o is None else [o]
        o = pl.pallas_call(
            body, grid=(nblk,),
            in_specs=_fox_specs(i, tq, nblk, nh) + [pl.BlockSpec(memory_space=pl.ANY)] * len(prev),
            out_specs=pl.BlockSpec((tq, LANES), lambda p, i=i: (i, p)), out_shape=_sds((t, d), BF16),
            input_output_aliases={4: 0} if prev else {},
            compiler_params=_params(), name=f"fox_fwd_q{i}")(zqkv, zqkv, zqkv, crow, *prev)
    return o


def _fox_bwd(zqkv, crow, do):
    t, d, hd, nblk, nh, tq = _fox_dims(zqkv, crow)
    nq = t // tq
    acc = None
    for i in reversed(range(nq)):
        kext = (i + 1) * tq
        first = acc is None

        def body(q_ref, k_ref, v_ref, cr_ref, do_ref, *rest, first=first, i=i):
            if first:
                dq_ref, dk_ref, dv_ref, dcr_ref = rest
            else:
                dk_in, dv_in, dcr_in = rest[1:4]
                dq_ref, dk_ref, dv_ref, dcr_ref = rest[4:]
            f = functools.partial(_fox_block, q0=i * tq, hd=hd)
            _, vjp = jax.vjp(f, q_ref[...].astype(F32), k_ref[...].astype(F32), v_ref[...].astype(F32), cr_ref[...])
            dq, dk, dv, dcr = vjp(do_ref[...].astype(F32))
            dq_ref[...] = dq.astype(dq_ref.dtype)
            if first:
                dk_ref[...], dv_ref[...], dcr_ref[...] = dk, dv, dcr
            else:
                dk_ref[...] = dk_in[...] + dk
                dv_ref[...] = dv_in[...] + dv
                dcr_ref[...] = dcr_in[...] + dcr

        qblk = pl.BlockSpec((tq, LANES), lambda p, i=i: (i, p))
        kblk = pl.BlockSpec((kext, LANES), lambda p: (0, p))
        rblk = pl.BlockSpec((nh, 1, kext), lambda p: (p, 0, 0))
        if first:
            prev, prev_specs, aliases = [], [], {}
        else:
            prev, prev_specs = list(acc), [pl.BlockSpec(memory_space=pl.ANY), kblk, kblk, rblk]
            aliases = {5: 0, 6: 1, 7: 2, 8: 3}
        acc = pl.pallas_call(
            body, grid=(nblk,),
            in_specs=_fox_specs(i, tq, nblk, nh) + [qblk] + prev_specs,
            out_specs=[qblk, kblk, kblk, rblk],
            out_shape=[_sds((t, d), BF16), _sds((t, d), F32), _sds((t, d), F32), _sds(crow.shape, F32)],
            input_output_aliases=aliases,
            compiler_params=_params(), name=f"fox_bwd_q{i}")(zqkv, zqkv, zqkv, crow, do, *prev)
    return acc


def _xattn_block(q, k, v):
    s = lax.dot_general(q.astype(BF16), k.astype(BF16), (((1,), (1,)), ((), ())), preferred_element_type=F32)
    s = s * (q.shape[1] ** -0.5)
    e = jnp.exp(s - lax.stop_gradient(jnp.max(s, axis=1, keepdims=True)))
    prob = e / jnp.sum(e, axis=1, keepdims=True)
    return jnp.dot(prob.astype(BF16), v.astype(BF16), preferred_element_type=F32)


def _xattn_fwd(qx, kv):
    t, dx = qx.shape
    nm, nh, tq = kv.shape[0], qx.shape[1] // LANES, min(512, qx.shape[0])

    def body(q_ref, k_ref, v_ref, o_ref):
        o_ref[...] = _xattn_block(q_ref[...], k_ref[...], v_ref[...]).astype(o_ref.dtype)

    return pl.pallas_call(
        body, grid=(nh, t // tq),
        in_specs=[pl.BlockSpec((tq, LANES), lambda h, i: (i, h)), pl.BlockSpec((nm, LANES), lambda h, i: (0, h)),
                  pl.BlockSpec((nm, LANES), lambda h, i: (0, nh + h))],
        out_specs=pl.BlockSpec((tq, LANES), lambda h, i: (i, h)), out_shape=_sds((t, dx), BF16),
        compiler_params=_params(), name="xattn_fwd")(qx, kv, kv)


def _xattn_bwd(qx, kv, do):
    t, dx = qx.shape
    nm, nh, tq = kv.shape[0], qx.shape[1] // LANES, min(512, qx.shape[0])

    def body(q_ref, k_ref, v_ref, do_ref, dq_ref, dk_ref, dv_ref):
        _, vjp = jax.vjp(_xattn_block, q_ref[...].astype(F32), k_ref[...].astype(F32), v_ref[...].astype(F32))
        dq, dk, dv = vjp(do_ref[...].astype(F32))
        dq_ref[...] = dq.astype(dq_ref.dtype)

        @pl.when(pl.program_id(1) == 0)
        def _():
            dk_ref[...] = jnp.zeros_like(dk_ref)
            dv_ref[...] = jnp.zeros_like(dv_ref)

        dk_ref[...] += dk
        dv_ref[...] += dv

    qblk = pl.BlockSpec((tq, LANES), lambda h, i: (i, h))
    kblk = pl.BlockSpec((nm, LANES), lambda h, i: (0, h))
    return pl.pallas_call(
        body, grid=(nh, t // tq),
        in_specs=[qblk, kblk, pl.BlockSpec((nm, LANES), lambda h, i: (0, nh + h)), qblk],
        out_specs=[qblk, kblk, kblk],
        out_shape=[_sds((t, dx), BF16), _sds((nm, dx), F32), _sds((nm, dx), F32)],
        compiler_params=_params(), name="xattn_bwd")(qx, kv, kv, do)


def _row_tile(rows, cols, budget=1 << 20, step=2 * SUBLANES):
    best = None
    for tb in range(step, rows + 1, step):
        if rows % tb == 0 and tb * cols * 4 <= budget:
            best = tb
    return rows if best is None else best


def _adamw(w, g, m, v, keep_rank=False):
    shape = w.shape
    cols = shape[-1]
    if keep_rank:
        w2, g2, m2, v2 = w, g, m, v
        rows, tail = shape[0], shape[1:]
        tb = _row_tile(rows, math.prod(tail), step=1)
    else:
        w2, g2, m2, v2 = (a.reshape(-1, cols) for a in (w, g, m, v))
        rows, tail = w2.shape[0], (cols,)
        tb = _row_tile(rows, cols)

    def body(w_ref, g_ref, m_ref, v_ref, d_ref, nm_ref, nv_ref):
        gg = g_ref[...]
        nm = ADAM_B1 * m_ref[...] + (1.0 - ADAM_B1) * gg
        nv = ADAM_B2 * v_ref[...] + (1.0 - ADAM_B2) * (gg * gg)
        m_hat = nm / (1.0 - ADAM_B1 ** ADAM_STEP)
        v_hat = nv / (1.0 - ADAM_B2 ** ADAM_STEP)
        d_ref[...] = -ADAM_LR * (m_hat / (jnp.sqrt(v_hat) + ADAM_EPS) + ADAM_WD * w_ref[...])
        nm_ref[...] = nm
        nv_ref[...] = nv

    blk = pl.BlockSpec((tb,) + tail, lambda i: (i,) + (0,) * len(tail))
    out = pl.pallas_call(
        body, grid=(rows // tb,), in_specs=[blk] * 4, out_specs=[blk] * 3, out_shape=[_sds(w2.shape, F32)] * 3,
        compiler_params=_params(), name="adamw")(w2, g2, m2, v2)
    return tuple(o.reshape(shape) for o in out)


def _sum_devices(gathered, rows):
    cols = gathered.shape[1]

    def body(g_ref, o_ref):
        acc = g_ref[0:rows, :]
        for dev in range(1, N_DEV):
            acc = acc + g_ref[dev * rows:(dev + 1) * rows, :]
        o_ref[...] = acc

    return pl.pallas_call(body, out_shape=_sds((rows, cols), F32), compiler_params=_params(), name="sum_devices")(gathered)


def _place():
    x, y, c = lax.axis_index("x"), lax.axis_index("y"), lax.axis_index("c")
    chips = [(1 - x, y), (x, 1 - y), (1 - x, 1 - y)]
    return x, y, c, chips


def _chip_id(chip):
    return 2 * chip[0] + chip[1]


def _half(c, rows):
    rh = rows // 2
    return pl.ds(pl.multiple_of(c * rh, 16), rh)


_HBM = pl.BlockSpec(memory_space=pltpu.HBM)


def _allgather8(x_shard):
    m_per, n = x_shard.shape

    def body(x_ref, out_ref, send_sems, recv_sems, local_sem):
        x, y, c, chips = _place()
        me, sibling = (x, y, c), (x, y, 1 - c)

        def rows(px, py, pc):
            return out_ref.at[pl.ds((4 * px + 2 * py + pc) * m_per, m_per), :]

        def copy(k, block, to, src=None):
            return pltpu.make_async_remote_copy(
                src_ref=rows(*block) if src is None else src, dst_ref=rows(*block),
                send_sem=send_sems.at[k], recv_sem=recv_sems.at[k], device_id=to, device_id_type=MESH)

        mine = pltpu.make_async_copy(x_ref, rows(*me), local_sem)
        mine.start()
        first = [copy(0, me, sibling, src=x_ref)]
        first += [copy(1 + j, me, (*chip, c), src=x_ref) for j, chip in enumerate(chips)]
        for cp in first:
            cp.start()
        passed = [copy(4 + j, (*chip, c), sibling) for j, chip in enumerate(chips)]
        for j, chip in enumerate(chips):
            copy(1 + j, (*chip, c), me).wait_recv()
            passed[j].start()
        copy(0, sibling, me).wait_recv()
        for j, chip in enumerate(chips):
            copy(4 + j, (*chip, 1 - c), me).wait_recv()
        for cp in first + passed:
            cp.wait_send()
        mine.wait()

    return pl.pallas_call(
        body, out_shape=_sds((N_DEV * m_per, n), x_shard.dtype),
        in_specs=[pl.BlockSpec(memory_space=pltpu.VMEM)], out_specs=pl.BlockSpec(memory_space=pltpu.VMEM),
        scratch_shapes=[pltpu.SemaphoreType.DMA((7,)), pltpu.SemaphoreType.DMA((7,)), pltpu.SemaphoreType.DMA],
        compiler_params=_params(), name="allgather8")(x_shard)


def _gather_weights(ws, layer, after):
    n = len(ws)

    def body(*refs):
        w, out = refs[:n], refs[n + 1:2 * n + 1]
        send_i, recv_i, send_d, recv_d = refs[2 * n + 1:]
        x, y, c, chips = _place()
        sibling = (x, y, 1 - c)
        me_chip = 2 * x + y

        def slab(i, chip, half):
            return out[i].at[chip, _half(half, w[i].shape[1]), :]

        def over_ici(i, j, src_chip, to, src=None):
            return pltpu.make_async_remote_copy(
                src_ref=slab(i, src_chip, c) if src is None else src, dst_ref=slab(i, src_chip, c),
                send_sem=send_i.at[3 * i + j], recv_sem=recv_i.at[3 * i + j], device_id=to, device_id_type=MESH)

        def over_d2d(i, j, half):
            return pltpu.make_async_remote_copy(
                src_ref=slab(i, _chip_id(chips[j]), half), dst_ref=slab(i, _chip_id(chips[j]), half),
                send_sem=send_d.at[3 * i + j], recv_sem=recv_d.at[3 * i + j], device_id=sibling, device_id_type=MESH)

        sent = []
        for i in range(n):
            for j, chip in enumerate(chips):
                cp = over_ici(i, j, me_chip, (*chip, c), src=w[i].at[layer, _half(c, w[i].shape[1]), :])
                cp.start()
                sent.append(cp)
        for i in range(n):
            for j, chip in enumerate(chips):
                over_ici(i, j, _chip_id(chip), (x, y, c)).wait_recv()
                cp = over_d2d(i, j, c)
                cp.start()
                sent.append(cp)
        for i in range(n):
            for j in range(3):
                over_d2d(i, j, 1 - c).wait_recv()
        for cp in sent:
            cp.wait_send()

    return pl.pallas_call(
        body, in_specs=[_HBM] * n + [pl.BlockSpec(memory_space=pl.ANY)], out_specs=[_HBM] * n,
        out_shape=[_sds((N_CHIPS,) + a.shape[1:], a.dtype) for a in ws],
        scratch_shapes=[pltpu.SemaphoreType.DMA((3 * n,)) for _ in range(4)],
        compiler_params=_params(), name="gather_weights")(*ws, after)


_SEM = pl.BlockSpec(memory_space=pltpu.SEMAPHORE)
_DATAFLOW = pltpu.SideEffectType.DATAFLOW_SIDE_EFFECTING


def _gather_copies(w, land, send, recv, layer):
    x, y, c, chips = _place()
    return [pltpu.make_async_remote_copy(
        src_ref=w[i].at[layer], dst_ref=land[i].at[2 * x + y], send_sem=send.at[3 * i + j], recv_sem=recv.at[3 * i + j],
        device_id=(*chip, c), device_id_type=MESH) for i in range(len(w)) for j, chip in enumerate(chips)]


def _gather_start(ws, layer, after):
    n = len(ws)

    def body(*refs):
        w, land = refs[:n], refs[n:2 * n]
        send, recv = refs[2 * n + 1], refs[2 * n + 2]
        for cp in _gather_copies(w, land, send, recv, layer):
            cp.start()
        refs[-1][...] = jnp.zeros_like(refs[-1])

    lands = [lax.empty((N_CHIPS,) + a.shape[1:], a.dtype) for a in ws]
    operands = [pltpu.with_memory_space_constraint(a, pltpu.HBM) for a in list(ws) + lands]
    out = pl.pallas_call(
        body, name=f"gather_start_{layer}",
        out_shape=(pltpu.SemaphoreType.DMA((3 * n,)), pltpu.SemaphoreType.DMA((3 * n,)),
                   *[pltpu.HBM(a.shape, a.dtype) for a in operands], _sds((SUBLANES, LANES), F32)),
        in_specs=[_HBM] * (2 * n) + [pl.BlockSpec(memory_space=pl.ANY)],
        out_specs=(_SEM, _SEM, *[_HBM] * (2 * n), pl.BlockSpec(memory_space=pltpu.VMEM)),
        input_output_aliases={i: 2 + i for i in range(2 * n)},
        compiler_params=pltpu.CompilerParams(has_side_effects=_DATAFLOW))(*operands, after)
    return out[0], out[1], list(out[2:2 + n]), list(out[2 + n:2 + 2 * n]), out[-1]


def _gather_wait(send, recv, ws, lands, after, layer):
    n = len(ws)

    def body(*refs):
        for cp in _gather_copies(refs[:n], refs[n:2 * n], refs[2 * n], refs[2 * n + 1], layer):
            cp.wait_send()
            cp.wait_recv()

    out = pl.pallas_call(
        body, name=f"gather_wait_{layer}",
        out_shape=tuple(pltpu.HBM(a.shape, a.dtype) for a in list(ws) + list(lands)),
        in_specs=[_HBM] * (2 * n) + [_SEM, _SEM, pl.BlockSpec(memory_space=pl.ANY)], out_specs=tuple([_HBM] * (2 * n)),
        input_output_aliases={i: i for i in range(2 * n)},
        compiler_params=pltpu.CompilerParams(has_side_effects=_DATAFLOW))(*ws, *lands, send, recv, after)
    return list(out[:n]), list(out[n:])


def _swap_halves(gs):
    n = len(gs)

    def body(*refs):
        g, out = refs[:n], refs[n:2 * n]
        send, recv = refs[2 * n:]
        x, y, c, _ = _place()
        cps = [pltpu.make_async_remote_copy(
            src_ref=g[i].at[:, :, _half(1 - c, g[i].shape[2]), :], dst_ref=out[i],
            send_sem=send.at[i], recv_sem=recv.at[i], device_id=(x, y, 1 - c), device_id_type=MESH) for i in range(n)]
        for cp in cps:
            cp.start()
        for cp in cps:
            cp.wait()

    return pl.pallas_call(
        body, in_specs=[_HBM] * n, out_specs=[_HBM] * n,
        out_shape=[_sds(a.shape[:2] + (a.shape[2] // 2, a.shape[3]), a.dtype) for a in gs],
        scratch_shapes=[pltpu.SemaphoreType.DMA((n,)), pltpu.SemaphoreType.DMA((n,))],
        compiler_params=_params(), name="swap_halves")(*gs)


def _exchange_chips(ss):
    n = len(ss)

    def body(*refs):
        s, out = refs[:n], refs[n:2 * n]
        send, recv = refs[2 * n:]
        x, y, c, chips = _place()
        cps = []
        for i in range(n):
            for j, chip in enumerate(chips):
                cps.append(pltpu.make_async_remote_copy(
                    src_ref=s[i].at[:, _chip_id(chip)], dst_ref=out[i].at[j],
                    send_sem=send.at[3 * i + j], recv_sem=recv.at[3 * i + j], device_id=(*chip, c), device_id_type=MESH))
        for cp in cps:
            cp.start()
        for cp in cps:
            cp.wait()

    return pl.pallas_call(
        body, in_specs=[_HBM] * n, out_specs=[_HBM] * n,
        out_shape=[_sds((3, a.shape[0]) + a.shape[2:], a.dtype) for a in ss],
        scratch_shapes=[pltpu.SemaphoreType.DMA((3 * n,)), pltpu.SemaphoreType.DMA((3 * n,))],
        compiler_params=_params(), name="exchange_chips")(*ss)


def _exchange_copies(s, land, send, recv):
    x, y, c, chips = _place()
    return [pltpu.make_async_remote_copy(
        src_ref=s[i].at[:, _chip_id(chip)], dst_ref=land[i].at[j], send_sem=send.at[3 * i + j], recv_sem=recv.at[3 * i + j],
        device_id=(*chip, c), device_id_type=MESH) for i in range(len(s)) for j, chip in enumerate(chips)]


def _exchange_start(ss, layer):
    n = len(ss)

    def body(*refs):
        s, land = refs[:n], refs[n:2 * n]
        send, recv = refs[2 * n], refs[2 * n + 1]
        token = refs[-1]
        for cp in _exchange_copies(s, land, send, recv):
            cp.start()
        token[...] = jnp.zeros_like(token)

    lands = [lax.empty((3, a.shape[0]) + a.shape[2:], a.dtype) for a in ss]
    operands = [pltpu.with_memory_space_constraint(a, pltpu.HBM) for a in list(ss) + lands]
    out = pl.pallas_call(
        body, name=f"exchange_start_{layer}",
        out_shape=(pltpu.SemaphoreType.DMA((3 * n,)), pltpu.SemaphoreType.DMA((3 * n,)),
                   *[pltpu.HBM(a.shape, a.dtype) for a in operands], _sds((SUBLANES, LANES), F32)),
        in_specs=[_HBM] * (2 * n), out_specs=(_SEM, _SEM, *[_HBM] * (2 * n), pl.BlockSpec(memory_space=pltpu.VMEM)),
        input_output_aliases={i: 2 + i for i in range(2 * n)},
        compiler_params=pltpu.CompilerParams(has_side_effects=_DATAFLOW))(*operands)
    return out[0], out[1], list(out[2:2 + n]), list(out[2 + n:2 + 2 * n]), out[-1]


def _exchange_wait(send, recv, ss, lands, after, layer):
    n = len(ss)

    def body(*refs):
        s, land = refs[:n], refs[n:2 * n]
        for cp in _exchange_copies(s, land, refs[2 * n], refs[2 * n + 1]):
            cp.wait_send()
            cp.wait_recv()

    out = pl.pallas_call(
        body, name=f"exchange_wait_{layer}",
        out_shape=tuple(pltpu.HBM(a.shape, a.dtype) for a in list(ss) + list(lands)),
        in_specs=[_HBM] * (2 * n) + [_SEM, _SEM, pl.BlockSpec(memory_space=pl.ANY)], out_specs=tuple([_HBM] * (2 * n)),
        input_output_aliases={i: i for i in range(2 * n)},
        compiler_params=pltpu.CompilerParams(has_side_effects=_DATAFLOW))(*ss, *lands, send, recv, after)
    return list(out[:n]), list(out[n:])


def _join_halves(fs):
    n = len(fs)

    def body(*refs):
        f, out = refs[:n], refs[n:2 * n]
        send, recv = refs[2 * n:]
        x, y, c, _ = _place()

        def half(ref, which):
            return ref.at[:, _half(which, ref.shape[1]), :]

        for i in range(n):
            pltpu.make_async_remote_copy(
                src_ref=half(f[i], c), dst_ref=half(out[i], c), send_sem=send.at[i], recv_sem=recv.at[i],
                device_id=(x, y, 1 - c), device_id_type=MESH).start()
        for i in range(n):
            arrive = pltpu.make_async_remote_copy(
                src_ref=half(f[i], c), dst_ref=half(out[i], 1 - c), send_sem=send.at[i], recv_sem=recv.at[i],
                device_id=(x, y, 1 - c), device_id_type=MESH)
            arrive.wait_recv()
            arrive.wait_send()

    return pl.pallas_call(
        body, in_specs=[_HBM] * n, out_specs=[_HBM] * n, out_shape=[_sds(a.shape, a.dtype) for a in fs],
        input_output_aliases={i: i for i in range(n)},
        scratch_shapes=[pltpu.SemaphoreType.DMA((n,)), pltpu.SemaphoreType.DMA((n,))],
        compiler_params=_params(), name="join_halves")(*fs)


def _add_sibling(g, got, c_idx):
    nl, ns, r, cols = g.shape
    rh = r // 2
    tb = _row_tile(rh, cols, budget=ADD_BLOCK_BYTES)
    nb = rh // tb
    g3, got3 = g.reshape(nl * ns, r, cols), got.reshape(nl * ns, rh, cols)

    def body(c_ref, a_ref, b_ref, o_ref):
        o_ref[...] = (a_ref[...].astype(F32) + b_ref[...].astype(F32)).astype(o_ref.dtype)

    out = pl.pallas_call(
        body,
        grid_spec=pltpu.PrefetchScalarGridSpec(
            num_scalar_prefetch=1, grid=(nl * ns, nb),
            in_specs=[pl.BlockSpec((None, tb, cols), lambda s, i, c_ref: (s, c_ref[0] * nb + i, 0)),
                      pl.BlockSpec((None, tb, cols), lambda s, i, c_ref: (s, i, 0))],
            out_specs=pl.BlockSpec((None, tb, cols), lambda s, i, c_ref: (s, i, 0))),
        out_shape=_sds((nl * ns, rh, cols), BF16), compiler_params=_params(), name="add_sibling")(c_idx, g3, got3)
    return out.reshape(nl, ns, rh, cols)


def _add_chips(s, got, chip_idx, c_idx, layer, depth, prev):
    _, ns, rh, cols = s.shape
    tb = _row_tile(rh, cols, budget=ADD_BLOCK_BYTES)
    nb = rh // tb

    def body(k_ref, c_ref, a_ref, b0_ref, b1_ref, b2_ref, *rest):
        o_ref = rest[-1]
        o_ref[...] = ((a_ref[...].astype(F32) + b0_ref[...].astype(F32)) + b1_ref[...].astype(F32)) + b2_ref[...].astype(F32)

    extra = [] if prev is None else [prev]
    return pl.pallas_call(
        body,
        grid_spec=pltpu.PrefetchScalarGridSpec(
            num_scalar_prefetch=2, grid=(nb,),
            in_specs=[pl.BlockSpec((None, None, tb, cols), lambda i, k_ref, c_ref: (0, k_ref[0], i, 0))]
            + [pl.BlockSpec((None, None, tb, cols), lambda i, k_ref, c_ref, j=j: (j, 0, i, 0)) for j in range(3)]
            + [pl.BlockSpec(memory_space=pl.ANY)] * len(extra),
            out_specs=pl.BlockSpec((None, tb, cols), lambda i, k_ref, c_ref: (layer, c_ref[0] * nb + i, 0))),
        out_shape=_sds((depth, 2 * rh, cols), F32), input_output_aliases={6: 0} if extra else {},
        compiler_params=_params(), name="add_chips")(chip_idx, c_idx, s, got, got, got, *extra)


def _in_pieces(d, heads, ws):
    groups, start = [], 0
    for k in range(12):
        if k == 6:
            groups.append(("f", 0, start, heads))
            start += heads
        else:
            groups.append(("main", (k - (k > 6)) * d, start, d))
            start += d
    pieces = []
    for chip in range(N_CHIPS):
        lo, hi = chip * ws, (chip + 1) * ws
        for dest, dcol, gstart, w in groups:
            a, b = max(lo, gstart), min(hi, gstart + w)
            if a < b:
                pieces.append((chip, a - lo, dest, dcol + a - gstart, b - a))
    return pieces


def _w_in_aligned(gathered, d, heads):
    rows, ws = gathered.shape[1], gathered.shape[2]
    tb = min(256, rows)
    pieces = _in_pieces(d, heads, ws)

    def body(g_ref, m_ref, f_ref):
        f_ref[...] = jnp.zeros_like(f_ref)
        for chip, sc, dest, dc, w in pieces:
            (m_ref if dest == "main" else f_ref)[:, dc:dc + w] = g_ref[chip, :, sc:sc + w]

    return pl.pallas_call(
        body, grid=(rows // tb,), in_specs=[pl.BlockSpec((N_CHIPS, tb, ws), lambda i: (0, i, 0))],
        out_specs=[pl.BlockSpec((tb, 11 * d), lambda i: (i, 0)), pl.BlockSpec((tb, LANES), lambda i: (i, 0))],
        out_shape=[_sds((rows, 11 * d), gathered.dtype), _sds((rows, LANES), gathered.dtype)],
        compiler_params=_params(), name="w_in_aligned")(gathered)


def _w_in_shards(gms, gf, prev, l, shape, heads):
    lo, hi = gms
    rows, d = lo.shape[0], lo.shape[1] // 6
    ws = shape[3]
    tb = min(256, rows)
    pieces = _in_pieces(d, heads, ws)

    def body(lo_ref, hi_ref, f_ref, *rest):
        o_ref = rest[-1]
        for chip, sc, dest, dc, w in pieces:
            if dest == "f":
                src = f_ref[:, dc:dc + w]
            elif dc < 6 * d:
                src = lo_ref[:, dc:dc + w]
            else:
                src = hi_ref[:, dc - 6 * d:dc - 6 * d + w]
            o_ref[chip, :, sc:sc + w] = src

    extra = [] if prev is None else [prev]
    return pl.pallas_call(
        body, grid=(rows // tb,),
        in_specs=[pl.BlockSpec((tb, 6 * d), lambda i: (i, 0)), pl.BlockSpec((tb, 5 * d), lambda i: (i, 0)),
                  pl.BlockSpec((tb, LANES), lambda i: (i, 0))] + [pl.BlockSpec(memory_space=pl.ANY)] * len(extra),
        out_specs=pl.BlockSpec((None, N_CHIPS, tb, ws), lambda i: (l, 0, i, 0)), out_shape=_sds(shape, lo.dtype),
        input_output_aliases={3: 0} if extra else {},
        compiler_params=_params(), name="w_in_shards")(lo, hi, gf, *extra)


def _pad_lanes(a, width=LANES):
    return jnp.pad(a, ((0, 0), (0, width - a.shape[1])))


def _cols(operand):
    return sum(p[2] for p in operand) if isinstance(operand, list) else operand.shape[1]


def _local_step(x, mem, tgt, sp, wt, dests=None, on_layer=None):
    t, d = x.shape
    depth = sp["mix_norm"].shape[0]
    heads = sp["b_forget"].shape[1]
    row = lambda a, l: a[l:l + 1]
    weights = wt if callable(wt) else (lambda l, x_in: {k: v[l] for k, v in wt.items()})
    layer_weights = []

    memn, = _rows("rms_mem", _rms_fn, [_part(mem)], [sp["mem_norm"][None]], [(d, BF16)])
    saved = []
    for l in range(depth):
        wl = weights(l, x)
        layer_weights.append(wl)
        f = _cols(wl["gu"]) // 2
        sv = {"x0": x}
        h1, = _rows("rms_mix", _rms_fn, [_part(x)], [row(sp["mix_norm"], l)], [(d, BF16)])
        wm = wl["main"]
        za = _mm("mm_in_a", h1, [_part(wm, 0, 3 * d)], out_dtype=BF16)
        zqkv = _mm("mm_in_qkv", h1, [_part(wm, 3 * d, 3 * d)], out_dtype=BF16)
        zcg = _mm("mm_in_cg", h1, [_part(wm, 6 * d, 5 * d)], out_dtype=BF16)
        zf = _mm("mm_in_f", h1, wl["f"])
        bf = _pad_lanes(row(sp["b_forget"], l))
        c = _fox_cumsum(zf, bf)
        crow = c[:, :heads].T.reshape(heads, 1, t)
        o = _fox_fwd(zqkv, crow)
        pa = _conv_a_fwd(za, sp["conv_a"][l])
        u2 = _conv_c_fwd(zcg, sp["conv_c"][l], row(sp["conv_c_bias"], l))
        u3, = _rows("ln_silu", _lnsilu_fn, [_part(u2)], [row(sp["ln_c_gain"], l), row(sp["ln_c_bias"], l)], [(d, BF16)])
        ya = _mm("mm_out_a", pa, wl["out_a"])
        yb = _mm("mm_out_b", o, wl["out_b"])
        yc = _mm("mm_out_c", u3, wl["out_c"])
        bg = [sp["b_gate"][l:l + 1, k * d:(k + 1) * d] for k in range(3)]
        gate_rows = [_part(zcg, (2 + k) * d, d) for k in range(3)] + [_part(ya), _part(yb), _part(yc)]
        mg, = _rows("merge", _merge_fn, gate_rows, bg, [(d, BF16)])
        x1 = _mm("mm_o", mg, wl["o"], add=x)
        h2, = _rows("rms_x", _rms_fn, [_part(x1)], [row(sp["xattn_norm"], l)], [(d, BF16)])
        qx = _mm("mm_xq", h2, wl["xq"], out_dtype=BF16)
        kv = _mm("mm_xkv", memn, wl["xkv"], out_dtype=BF16)
        ox = _xattn_fwd(qx, kv)
        x2 = _mm("mm_xo", ox, wl["xo"], add=x1)
        h3, = _rows("rms_ffn", _rms_fn, [_part(x2)], [row(sp["ffn_norm"], l)], [(d, BF16)])
        gu = _mm("mm_gu", h3, wl["gu"], out_dtype=BF16)
        act, = _rows("swiglu", _swiglu_fn, [_part(gu, 0, f), _part(gu, f, f)], [], [(f, BF16)])
        x3 = _mm("mm_down", act, wl["down"], add=x2)
        sv.update(h1=h1, za=za, zqkv=zqkv, zcg=zcg, zf=zf, bf=bf, crow=crow, o=o, pa=pa, u2=u2, u3=u3, ya=ya, yb=yb,
                  yc=yc, bg=bg, gate_rows=gate_rows, mg=mg, x1=x1, h2=h2, qx=qx, kv=kv, ox=ox, x2=x2, h3=h3, gu=gu, act=act)
        saved.append(sv)
        x = x3

    loss_row, dx, d_final = _loss_call(x, tgt, sp["final_norm"][None])

    gs = {k: [None] * depth for k in ("mix_norm", "b_gate", "b_forget", "conv_a", "conv_c", "conv_c_bias", "ln_c_gain",
                                      "ln_c_bias", "xattn_norm", "ffn_norm")}
    gw = {k: [None] * depth for k in ("main", "f", "out_a", "out_b", "out_c", "o", "xq", "xkv", "xo", "gu", "down")}
    def weight_grad(key, l, name, a_op, b_op):
        if dests is None or key not in dests:
            gw[key][l] = _mm(name, a_op, b_op, ta=True, out_dtype=BF16)
        else:
            shape, shards = dests[key]
            gw[key][l] = _mm(name, a_op, b_op, ta=True, out_dtype=BF16, dest=(shape, None, (0,), shards))

    dmemn = None
    for l in reversed(range(depth)):
        sv, wl = saved[l], layer_weights[l]
        dact = _mm("mm_down_dx", dx, wl["down"], tb=True)
        weight_grad("down", l, "mm_down_dw", sv["act"], dx)
        (dg, du), _ = _rows_vjp("swiglu_bwd", _swiglu_fn, [_part(sv["gu"], 0, f), _part(sv["gu"], f, f)], [], [_part(dact)],
                                [BF16, BF16], [])
        dgu = [_part(dg), _part(du)]
        dh3 = _mm("mm_gu_dx", dgu, wl["gu"], tb=True)
        weight_grad("gu", l, "mm_gu_dw", sv["h3"], dgu)
        (dx2,), (gs["ffn_norm"][l],) = _rows_vjp("rms_ffn_bwd", _rms_fn, [_part(sv["x2"])], [row(sp["ffn_norm"], l)],
                                                 [_part(dh3)], [F32], [True], adds={0: _part(dx)})
        dox = _mm("mm_xo_dx", dx2, wl["xo"], tb=True)
        weight_grad("xo", l, "mm_xo_dw", sv["ox"], dx2)
        dqx, dkx, dvx = _xattn_bwd(sv["qx"], sv["kv"], dox)
        dh2 = _mm("mm_xq_dx", dqx, wl["xq"], tb=True)
        weight_grad("xq", l, "mm_xq_dw", sv["h2"], dqx)
        dkv = [_part(dkx), _part(dvx)]
        dmemn = _mm("mm_xkv_dx", dkv, wl["xkv"], tb=True, add=dmemn)
        weight_grad("xkv", l, "mm_xkv_dw", memn, dkv)
        (dx1,), (gs["xattn_norm"][l],) = _rows_vjp("rms_x_bwd", _rms_fn, [_part(sv["x1"])], [row(sp["xattn_norm"], l)],
                                                   [_part(dh2)], [F32], [True], adds={0: _part(dx2)})
        dmg = _mm("mm_o_dx", dx1, wl["o"], tb=True)
        weight_grad("o", l, "mm_o_dw", sv["mg"], dx1)
        (dga, dgb, dgc, dya, dyb, dyc), dbg = _rows_vjp("merge_bwd", _merge_fn, sv["gate_rows"], sv["bg"], [_part(dmg)],
                                                        [BF16] * 6, [True] * 3)
        gs["b_gate"][l] = jnp.concatenate(dbg, axis=1)
        dpa = _mm("mm_out_a_dx", dya, wl["out_a"], tb=True)
        weight_grad("out_a", l, "mm_out_a_dw", sv["pa"], dya)
        do = _mm("mm_out_b_dx", dyb, wl["out_b"], tb=True)
        weight_grad("out_b", l, "mm_out_b_dw", sv["o"], dyb)
        du3 = _mm("mm_out_c_dx", dyc, wl["out_c"], tb=True)
        weight_grad("out_c", l, "mm_out_c_dw", sv["u3"], dyc)
        (du2,), (gs["ln_c_gain"][l], gs["ln_c_bias"][l]) = _rows_vjp(
            "ln_silu_bwd", _lnsilu_fn, [_part(sv["u2"])], [row(sp["ln_c_gain"], l), row(sp["ln_c_bias"], l)], [_part(du3)],
            [F32], [True, True])
        dcv, dcg, gs["conv_c"][l], gs["conv_c_bias"][l] = _conv_c_bwd(sv["zcg"], sp["conv_c"][l], du2)
        dab, dac, dau, gs["conv_a"][l] = _conv_a_bwd(sv["za"], sp["conv_a"][l], dpa)
        dq, dk, dv, dcr = _fox_bwd(sv["zqkv"], sv["crow"], do)
        dzf, dbf = _fox_cumsum_bwd(sv["zf"], sv["bf"], _pad_lanes(dcr.reshape(heads, t).T))
        gs["b_forget"][l] = dbf[:, :heads]
        dz = [_part(a) for a in (dab, dac, dau, dq, dk, dv, dcv, dcg, dga, dgb, dgc)]
        dh1f = _mm("mm_in_f_dx", dzf, wl["f"], tb=True)
        dh1 = _mm("mm_in_dx", dz, wl["main"], tb=True, add=dh1f)
        gw["main"][l] = [_mm("mm_in_dw_lo", sv["h1"], dz[:6], ta=True, out_dtype=BF16),
                         _mm("mm_in_dw_hi", sv["h1"], dz[6:], ta=True, out_dtype=BF16)]
        gw["f"][l] = _mm("mm_in_f_dw", sv["h1"], dzf, ta=True, out_dtype=BF16)
        (dx,), (gs["mix_norm"][l],) = _rows_vjp("rms_mix_bwd", _rms_fn, [_part(sv["x0"])], [row(sp["mix_norm"], l)],
                                                [_part(dh1)], [F32], [True], adds={0: _part(dx1)})
        if on_layer is not None:
            token = on_layer(l, {k: v[l] for k, v in gw.items()})
            if l > 0:
                dx = dx + token[0, 0]

    _, (d_mem_norm,) = _rows_vjp("rms_mem_bwd", _rms_fn, [_part(mem)], [sp["mem_norm"][None]], [_part(dmemn)], [None], [True])
    small = {k: jnp.stack([a.reshape(sp[k].shape[1:]) for a in v]) for k, v in gs.items()}
    small["mem_norm"] = d_mem_norm[0]
    small["final_norm"] = d_final[0]
    return loss_row, dx, small, gw


_BIG = ("w_in", "w_out_a", "w_out_b", "w_out_c", "w_o", "w_xq", "w_xkv", "w_xo", "w_gate_up", "w_down")
_COL_SHARDED = ("w_in", "w_xo", "w_gate_up")
_KEY = {"w_out_a": "out_a", "w_out_b": "out_b", "w_out_c": "out_c", "w_o": "o", "w_xq": "xq", "w_xkv": "xkv", "w_xo": "xo",
        "w_gate_up": "gu", "w_down": "down"}
_SMALL = ("mix_norm", "b_gate", "b_forget", "conv_c_bias", "ln_c_gain", "ln_c_bias", "xattn_norm", "ffn_norm", "mem_norm",
          "final_norm")
_SMALL_SHARDED = ("conv_a", "conv_c")


def _pack_rows(parts):
    padded = []
    for a in parts:
        pad = -a.shape[0] % SUBLANES
        padded.append(jnp.pad(a, ((0, pad), (0, 0))) if pad else a)
    return jnp.concatenate(padded, axis=0)


def _unpack_rows(packed, shapes):
    out, pos = [], 0
    for r in shapes:
        out.append(packed[pos:pos + r])
        pos += r + (-r % SUBLANES)
    return out


def kernel(x, mem, mix_norm, w_in, b_gate, b_forget, conv_a, w_out_a, w_out_b, conv_c, conv_c_bias, ln_c_gain, ln_c_bias, w_out_c, w_o, xattn_norm, mem_norm, w_xq, w_xkv, w_xo, ffn_norm, w_gate_up, w_down, final_norm, loss_target, m_mix_norm, m_w_in, m_b_gate, m_b_forget, m_conv_a, m_w_out_a, m_w_out_b, m_conv_c, m_conv_c_bias, m_ln_c_gain, m_ln_c_bias, m_w_out_c, m_w_o, m_xattn_norm, m_mem_norm, m_w_xq, m_w_xkv, m_w_xo, m_ffn_norm, m_w_gate_up, m_w_down, m_final_norm, v_mix_norm, v_w_in, v_b_gate, v_b_forget, v_conv_a, v_w_out_a, v_w_out_b, v_conv_c, v_conv_c_bias, v_ln_c_gain, v_ln_c_bias, v_w_out_c, v_w_o, v_xattn_norm, v_mem_norm, v_w_xq, v_w_xkv, v_w_xo, v_ffn_norm, v_w_gate_up, v_w_down, v_final_norm):
    args = dict(locals())
    names = ["mix_norm", "w_in", "b_gate", "b_forget", "conv_a", "w_out_a", "w_out_b", "conv_c", "conv_c_bias", "ln_c_gain",
             "ln_c_bias", "w_out_c", "w_o", "xattn_norm", "mem_norm", "w_xq", "w_xkv", "w_xo", "ffn_norm", "w_gate_up", "w_down",
             "final_norm"]
    depth, d = mix_norm.shape
    heads = b_forget.shape[1]
    cx, cy, cc = lax.axis_index("x"), lax.axis_index("y"), lax.axis_index("c")
    chip = 2 * cx + cy
    c_idx = jnp.reshape(cc, (1,)).astype(jnp.int32)
    chip_idx = jnp.reshape(chip, (1,)).astype(jnp.int32)

    conv_rows = [conv_a.shape[1] * depth, conv_c.shape[1] * depth]
    conv_pack = _pack_rows([conv_a.reshape(conv_rows[0], -1), conv_c.reshape(conv_rows[1], -1)])
    conv_all = _allgather8(conv_pack).reshape(N_CHIPS, 2, conv_pack.shape[0], conv_pack.shape[1])[:, 0]
    conv_all = conv_all.transpose(1, 0, 2).reshape(conv_pack.shape[0], d)
    conv_a_full, conv_c_full = _unpack_rows(conv_all, conv_rows)
    own = [args[n].astype(BF16) for n in _BIG]
    first = _gather_weights(own, 0, conv_all)
    started, tokens = {}, []
    for l in range(1, depth):
        send, recv, thru, lands, token = _gather_start(own, l, first[0])
        started[l] = (send, recv, thru, lands)
        tokens.append(token)

    def layer_weights(l, x_in):
        if l == 0:
            lands = first
        else:
            _, lands = _gather_wait(*started[l], x_in, l)
        full = [lax.dynamic_update_slice(g, w[l][None], (chip, 0, 0)) for g, w in zip(lands, own)]
        wl = {}
        wl["main"], wl["f"] = _w_in_aligned(full[0], d, heads)
        for n, g in zip(_BIG[1:], full[1:]):
            if n in _COL_SHARDED:
                wl[_KEY[n]] = [_part(g, lead=(j,)) for j in range(N_CHIPS)]
            else:
                wl[_KEY[n]] = g.reshape(N_CHIPS * g.shape[1], g.shape[2])
        return wl

    shard_shape = {n: (1, N_CHIPS) + args[n].shape[1:] for n in _BIG}
    dests = {_KEY[n]: (shard_shape[n], N_CHIPS) if n in _COL_SHARDED
             else ((1, N_CHIPS * args[n].shape[1], args[n].shape[2]), 1) for n in _BIG[1:]}
    sp = {n: args[n] for n in _SMALL}
    sp["conv_a"] = conv_a_full.reshape(depth, conv_a.shape[1], d)
    sp["conv_c"] = conv_c_full.reshape(depth, conv_c.shape[1], d)

    in_flight = {}

    def on_layer(l, gl):
        local = [_w_in_shards(gl["main"], gl["f"], None, 0, shard_shape["w_in"], heads)]
        local += [gl[_KEY[n]].reshape(shard_shape[n]) for n in _BIG[1:]]
        chip_sums = [_add_sibling(g, r, c_idx) for g, r in zip(local, _swap_halves(local))]
        send, recv, chip_sums, lands, token = _exchange_start(chip_sums, l)
        in_flight[l] = (send, recv, chip_sums, lands)
        return token

    x_in = x[0]
    for token in tokens:
        x_in = x_in + token[0, 0]
    loss_row, grad_x, gsmall, gw = _local_step(x_in, mem[0], loss_target[0], sp, layer_weights, dests, on_layer)
    halves = [None] * len(_BIG)
    for l in reversed(range(depth)):
        chip_sums, lands = _exchange_wait(*in_flight[l], grad_x, l)
        halves = [_add_chips(s_, r, chip_idx, c_idx, l, depth, h) for s_, r, h in zip(chip_sums, lands, halves)]
    grads = dict(zip(_BIG, _join_halves(halves)))

    def as_rows(n, a):
        return _pad_lanes(a, d) if n == "b_forget" else a.reshape(-1, d)

    pieces = [as_rows(n, gsmall[n]) for n in _SMALL]
    for n in _SMALL_SHARDED:
        pieces.append(gsmall[n].reshape(-1, d))
    pieces.append(_pad_lanes(loss_row, d))
    pack = _pack_rows(pieces)
    total = _sum_devices(_allgather8(pack), pack.shape[0])
    shapes = [p.shape[0] for p in pieces]
    summed = _unpack_rows(total, shapes)
    for n, g in zip(_SMALL, summed):
        grads[n] = g[:, :heads] if n == "b_forget" else g.reshape(args[n].shape)
    for n, g in zip(_SMALL_SHARDED, summed[len(_SMALL):]):
        g = g.reshape(args[n].shape[:2] + (d,))
        grads[n] = lax.dynamic_slice_in_dim(g, chip * args[n].shape[2], args[n].shape[2], axis=2)
    loss = summed[-1][0, 0]

    delta, new_m, new_v = {}, {}, {}
    packed = [_pack_rows([as_rows(n, src[pre + n]) for n in _SMALL])
              for src, pre in ((args, ""), (grads, ""), (args, "m_"), (args, "v_"))]
    srows = [as_rows(n, args[n]).shape[0] for n in _SMALL]
    updated = [_unpack_rows(a, srows) for a in _adamw(*packed)]
    for k, n in enumerate(_SMALL):
        delta[n], new_m[n], new_v[n] = (
            (u[k][:, :heads] if n == "b_forget" else u[k]).reshape(args[n].shape) for u in updated)
    for n in _BIG + _SMALL_SHARDED:
        if n == "w_in":
            to_t, from_t = (lambda a: jnp.transpose(a, (2, 0, 1))), (lambda a: jnp.transpose(a, (1, 2, 0)))
            g_t = to_t(grads[n])
            grads[n] = from_t(g_t)
            delta[n], new_m[n], new_v[n] = (from_t(a) for a in _adamw(to_t(args[n]), g_t, to_t(args["m_" + n]), to_t(args["v_" + n]),
                                                                     keep_rank=True))
            continue
        delta[n], new_m[n], new_v[n] = _adamw(args[n], grads[n], args["m_" + n], args["v_" + n])

    return (loss, grad_x[None], *[grads[n] for n in names], *[delta[n] for n in names], *[new_m[n] for n in names],
            *[new_v[n] for n in names])
```

```python
import functools
import math

import jax
import jax.numpy as jnp
from jax import lax
from jax.experimental import pallas as pl
from jax.experimental.pallas import tpu as pltpu

F32, BF16 = jnp.float32, jnp.bfloat16
EPS = 1e-6
LANES = 128
SUBLANES = 8
VMEM_LIMIT = 56 * 1024 * 1024
MM_VMEM_BUDGET = 44 * 1024 * 1024
ADD_BLOCK_BYTES = 4 * 1024 * 1024
MESH = pl.DeviceIdType.MESH
N_CHIPS = 4
N_DEV = 8

ADAM_LR, ADAM_B1, ADAM_B2, ADAM_EPS, ADAM_WD, ADAM_STEP = 0.001, 0.9, 0.999, 1e-08, 0.01, 10


def _params(**kw):
    return pltpu.CompilerParams(vmem_limit_bytes=VMEM_LIMIT, **kw)


def _sds(shape, dtype):
    return jax.ShapeDtypeStruct(tuple(shape), dtype)


def _part(arr, off=0, width=None, lead=()):
    assert arr.ndim == len(lead) + 2
    return (arr, off, arr.shape[-1] - off if width is None else width, tuple(lead))


def _unit(parts):
    u = 0
    for _, off, w, _ in parts:
        u = math.gcd(u, math.gcd(off, w))
    return u


def _pick(dim, unit, cands, lane):
    for c in cands:
        if c <= dim and dim % c == 0 and unit % c == 0 and (not lane or c % LANES == 0):
            return c
    return min(dim, unit)


_COL_TILES = (1024, 1408, 512, 256, 128)
_ROW_TILES = (512, 256, 1408, 128)
LARGE_ROW_TILE = 1024


def _mm(name, a, b, *, ta=False, tb=False, out_dtype=F32, add=None, dest=None):
    if not isinstance(a, list):
        a = [_part(a)]
    if not isinstance(b, list):
        b = [_part(b)]
    a_rows, a_cols = a[0][0].shape[-2], sum(p[2] for p in a)
    b_rows, b_cols = b[0][0].shape[-2], sum(p[2] for p in b)
    M, Ka = (a_cols, a_rows) if ta else (a_rows, a_cols)
    Kb, N = (b_cols, b_rows) if tb else (b_rows, b_cols)
    assert Ka == Kb, (name, Ka, Kb)
    K = Ka
    ua, ub = _unit(a), _unit(b)
    shards = 1 if dest is None else dest[3]
    tm = _pick(M, ua if ta else M, _ROW_TILES, ta)
    tn = _pick(N, math.gcd(N if tb else ub, N // shards), _COL_TILES, not tb)
    ku = math.gcd(K if ta else ua, ub if tb else K)
    tk = _pick(K, ku, _COL_TILES, (not ta) or tb)
    tm_large = _pick(M, ua if ta else M, (LARGE_ROW_TILE,), ta)
    a_size = sum(p[0].dtype.itemsize for p in a)
    b_size = sum(p[0].dtype.itemsize for p in b)

    def vmem_estimate():
        return (2 * tm * tk * a_size + 2 * tk * tn * b_size + tm * tn * (2 * jnp.dtype(out_dtype).itemsize + 4)
                + (8 * tm * tn if add is not None else 0))

    if tm_large == LARGE_ROW_TILE:
        tm, tm_default = tm_large, tm
        if vmem_estimate() > MM_VMEM_BUDGET:
            tm = tm_default
    while vmem_estimate() > MM_VMEM_BUDGET:
        if tk % 256 == 0 and tk >= tn:
            tk //= 2
        elif tn % 256 == 0:
            tn //= 2
        elif tm % 256 == 0:
            tm //= 2
        elif tk % 256 == 0:
            tk //= 2
        else:
            break
    nm, nn, nk = M // tm, N // tn, K // tk
    a_tile, b_tile = (tm if ta else tk), (tk if tb else tn)
    a_nb, b_nb = a[0][2] // a_tile, b[0][2] // b_tile

    def index(parts, nb, tile, p, row, col):
        base = parts[p][1] // tile
        if len(parts) == 1:
            return parts[p][3] + (row, base + col)
        inside = (col // nb) == p
        return parts[p][3] + (jnp.where(inside, row, 0), base + jnp.clip(col - p * nb, 0, nb - 1))

    def spec(parts, p, shape, fn):
        return pl.BlockSpec((None,) * len(parts[p][3]) + shape, fn)

    in_specs, operands = [], []
    for p in range(len(a)):
        if ta:
            in_specs.append(spec(a, p, (tk, tm), lambda mi, nj, kk, p=p: index(a, a_nb, a_tile, p, kk, mi)))
        else:
            in_specs.append(spec(a, p, (tm, tk), lambda mi, nj, kk, p=p: index(a, a_nb, a_tile, p, mi, kk)))
        operands.append(a[p][0])
    for p in range(len(b)):
        if tb:
            in_specs.append(spec(b, p, (tn, tk), lambda mi, nj, kk, p=p: index(b, b_nb, b_tile, p, nj, kk)))
        else:
            in_specs.append(spec(b, p, (tk, tn), lambda mi, nj, kk, p=p: index(b, b_nb, b_tile, p, kk, nj)))
        operands.append(b[p][0])
    if add is not None:
        in_specs.append(pl.BlockSpec((tm, tn), lambda mi, nj, kk: (mi, nj)))
        operands.append(add)
    aliases = {}
    if dest is None:
        out_spec = pl.BlockSpec((tm, tn), lambda mi, nj, kk: (mi, nj))
        out_shape = _sds((M, N), out_dtype)
    else:
        d_shape, d_prev, d_lead, _ = dest
        nbs = N // shards // tn
        if shards == 1:
            out_spec = pl.BlockSpec((None,) * len(d_lead) + (tm, tn), lambda mi, nj, kk: tuple(d_lead) + (mi, nj))
        else:
            out_spec = pl.BlockSpec((None,) * (len(d_lead) + 1) + (tm, tn),
                                    lambda mi, nj, kk: tuple(d_lead) + (nj // nbs, mi, nj % nbs))
        out_shape = _sds(d_shape, out_dtype)
        if d_prev is not None:
            aliases = {len(operands): 0}
            in_specs.append(pl.BlockSpec(memory_space=pl.ANY))
            operands.append(d_prev)
    n_extra = 1 if aliases else 0
    dims = (((0 if ta else 1,), (1 if tb else 0,)), ((), ()))
    na, nb_ = len(a), len(b)

    def body(*refs):
        a_refs, b_refs = refs[:na], refs[na:na + nb_]
        pos = na + nb_
        add_ref = None
        if add is not None:
            add_ref = refs[pos]
            pos += 1
        pos += n_extra
        o_ref = refs[pos]
        acc_ref = refs[pos + 1] if nk > 1 else None
        mi, nj, kk = pl.program_id(0), pl.program_id(1), pl.program_id(2)
        a_col = mi if ta else kk
        b_col = kk if tb else nj

        def finish(r):
            if add_ref is not None:
                r = r + add_ref[...].astype(F32)
            o_ref[...] = r.astype(o_ref.dtype)

        def step(a_ref, b_ref):
            r = lax.dot_general(a_ref[...].astype(BF16), b_ref[...].astype(BF16), dims, preferred_element_type=F32)
            if nk == 1:
                finish(r)
            else:
                @pl.when(kk == 0)
                def _():
                    acc_ref[...] = r

                @pl.when(kk > 0)
                def _():
                    acc_ref[...] += r

        for pa in range(na):
            for pb in range(nb_):
                if na * nb_ == 1:
                    step(a_refs[pa], b_refs[pb])
                else:
                    cond = jnp.logical_and(a_col // a_nb == pa, b_col // b_nb == pb)
                    pl.when(cond)(functools.partial(step, a_refs[pa], b_refs[pb]))
        if nk > 1:
            @pl.when(kk == nk - 1)
            def _():
                finish(acc_ref[...])

    return pl.pallas_call(
        body, grid=(nm, nn, nk), in_specs=in_specs, out_specs=out_spec, out_shape=out_shape, input_output_aliases=aliases,
        scratch_shapes=[pltpu.VMEM((tm, tn), F32)] if nk > 1 else [],
        compiler_params=_params(dimension_semantics=("parallel", "parallel", "arbitrary")),
        name=name)(*operands)


def _row_spec(tb, part):
    _, off, w, _ = part
    return pl.BlockSpec((tb, w), lambda i, o=off // w: (i, o))


def _full_spec(arr):
    return pl.BlockSpec(arr.shape, lambda i: (0,) * arr.ndim)


def _rows(name, fn, rows, pars, outs, tb=256):
    n = rows[0][0].shape[0]
    tb = min(tb, n)
    nr, npar = len(rows), len(pars)

    def body(*refs):
        r = [ref[...].astype(F32) for ref in refs[:nr]]
        p = [ref[...].astype(F32) for ref in refs[nr:nr + npar]]
        res = fn(*r, *p)
        for o_ref, v in zip(refs[nr + npar:], res):
            o_ref[...] = v.astype(o_ref.dtype)

    out = pl.pallas_call(
        body, grid=(n // tb,),
        in_specs=[_row_spec(tb, p) for p in rows] + [_full_spec(p) for p in pars],
        out_specs=[pl.BlockSpec((tb, w), lambda i: (i, 0)) for w, _ in outs],
        out_shape=[_sds((n, w), dt) for w, dt in outs],
        compiler_params=_params(), name=name)(*[p[0] for p in rows], *pars)
    return out


def _rows_vjp(name, fn, rows, pars, cots, row_grads, par_grads, adds=None, tb=256):
    adds = adds or {}
    n = rows[0][0].shape[0]
    tb = min(tb, n)
    nr, npar, nc = len(rows), len(pars), len(cots)
    add_keys = sorted(adds)
    rg = [j for j, dt in enumerate(row_grads) if dt is not None]
    pg = [j for j, f in enumerate(par_grads) if f]

    def body(*refs):
        pos = 0
        r = [ref[...].astype(F32) for ref in refs[pos:pos + nr]]
        pos += nr
        p = [ref[...].astype(F32) for ref in refs[pos:pos + npar]]
        pos += npar
        ct = tuple(ref[...].astype(F32) for ref in refs[pos:pos + nc])
        pos += nc
        add_v = {k: ref[...].astype(F32) for k, ref in zip(add_keys, refs[pos:pos + len(add_keys)])}
        pos += len(add_keys)
        _, vjp = jax.vjp(lambda *args: tuple(fn(*args)), *r, *p)
        g = vjp(ct)
        for j in rg:
            v = g[j]
            if j in add_v:
                v = v + add_v[j]
            refs[pos][...] = v.astype(refs[pos].dtype)
            pos += 1
        first = pl.program_id(0) == 0
        for j in pg:
            ref = refs[pos]
            pos += 1

            @pl.when(first)
            def _(ref=ref):
                ref[...] = jnp.zeros_like(ref)

            ref[...] += g[nr + j]

    in_specs = ([_row_spec(tb, p) for p in rows] + [_full_spec(p) for p in pars] + [_row_spec(tb, p) for p in cots]
                + [_row_spec(tb, adds[k]) for k in add_keys])
    out_specs = ([pl.BlockSpec((tb, rows[j][2]), lambda i: (i, 0)) for j in rg] + [_full_spec(pars[j]) for j in pg])
    out_shape = ([_sds((n, rows[j][2]), row_grads[j]) for j in rg] + [_sds(pars[j].shape, F32) for j in pg])
    out = pl.pallas_call(
        body, grid=(n // tb,), in_specs=in_specs, out_specs=out_specs, out_shape=out_shape,
        compiler_params=_params(), name=name)(
            *[p[0] for p in rows], *pars, *[p[0] for p in cots], *[adds[k][0] for k in add_keys])
    return out[:len(rg)], out[len(rg):]


def _rms_fn(x, g):
    return (x * lax.rsqrt(jnp.mean(x * x, axis=-1, keepdims=True) + EPS) * g,)


def _lnsilu_fn(u, g, b):
    mu = jnp.mean(u, axis=-1, keepdims=True)
    xc = u - mu
    y = xc * lax.rsqrt(jnp.mean(xc * xc, axis=-1, keepdims=True) + EPS) * g + b
    return (y * jax.nn.sigmoid(y),)


def _merge_fn(ga, gb, gc, ya, yb, yc, ba, bb, bc):
    return (jax.nn.sigmoid(ga + ba) * ya + jax.nn.sigmoid(gb + bb) * yb + jax.nn.sigmoid(gc + bc) * yc,)


def _swiglu_fn(g, u):
    return (g * jax.nn.sigmoid(g) * u,)


def _loss_call(x, tgt, gain):
    n, d = x.shape
    tb = min(256, n)

    def body(x_ref, t_ref, g_ref, l_ref, dx_ref, dg_ref):
        (y,), vjp = jax.vjp(lambda a, b: tuple(_rms_fn(a, b)), x_ref[...], g_ref[...])
        e = y - t_ref[...]
        part = 0.5 * jnp.sum(jnp.mean(e * e, axis=-1, keepdims=True))
        dx, dg = vjp((e * (1.0 / d),))
        dx_ref[...] = dx

        @pl.when(pl.program_id(0) == 0)
        def _():
            l_ref[...] = jnp.zeros_like(l_ref)
            dg_ref[...] = jnp.zeros_like(dg_ref)

        l_ref[...] += jnp.full(l_ref.shape, part, F32)
        dg_ref[...] += dg

    return pl.pallas_call(
        body, grid=(n // tb,),
        in_specs=[pl.BlockSpec((tb, d), lambda i: (i, 0)), pl.BlockSpec((tb, d), lambda i: (i, 0)), _full_spec(gain)],
        out_specs=[pl.BlockSpec((1, LANES), lambda i: (0, 0)), pl.BlockSpec((tb, d), lambda i: (i, 0)), _full_spec(gain)],
        out_shape=[_sds((1, LANES), F32), _sds((n, d), F32), _sds(gain.shape, F32)],
        compiler_params=_params(), name="loss_head")(x, tgt, gain)


FOX_BLOCK = 256
CONV_ROWS = 64
CONV_HALO = 32


def _chunk(i):
    return pl.ds(pl.multiple_of(i * CONV_ROWS, CONV_ROWS), CONV_ROWS)


def _delayed(ext, s):
    return (ext if s == 0 else pltpu.roll(ext, s, 0))[CONV_HALO:]


def _advanced(ext, s):
    return (ext if s == 0 else pltpu.roll(ext, ext.shape[0] - s, 0))[:CONV_ROWS]


def _conv_pass(t, taps, w_ref, lanes, get_u, emit, get_dy=None, dw_acc=None):
    def body(i, carry):
        rows = _chunk(i)
        u = get_u(rows)
        ext = jnp.concatenate([carry, u], axis=0)
        dy = None if get_dy is None else get_dy(rows)
        acc = None
        for k in range(taps):
            sh = _delayed(ext, taps - 1 - k)
            term = w_ref[k:k + 1, lanes] * sh
            acc = term if acc is None else acc + term
            if dy is not None:
                dw_acc[k] += jnp.sum((dy * sh).reshape(CONV_ROWS // SUBLANES, SUBLANES, LANES), axis=0)
        emit(rows, u, acc)
        return u[CONV_ROWS - CONV_HALO:]

    lax.fori_loop(0, t // CONV_ROWS, body, jnp.zeros((CONV_HALO, LANES), F32))


def _conv_pass_t(t, taps, w_ref, lanes, get_dy, emit):
    n = t // CONV_ROWS

    def body(j, carry):
        rows = _chunk(n - 1 - j)
        dy = get_dy(rows)
        ext = jnp.concatenate([dy, carry], axis=0)
        acc = None
        for k in range(taps):
            term = w_ref[k:k + 1, lanes] * _advanced(ext, taps - 1 - k)
            acc = term if acc is None else acc + term
        emit(rows, acc)
        return dy[:CONV_HALO]

    lax.fori_loop(0, n, body, jnp.zeros((CONV_HALO, LANES), F32))


def _chan_block(d):
    return min(256, d)


def _lane_groups(cb):
    return [slice(g * LANES, (g + 1) * LANES) for g in range(cb // LANES)]


def _conv_a_fwd(za, w):
    t, d = za.shape[0], za.shape[1] // 3
    taps, cb = w.shape[0], _chan_block(d)
    nb = d // cb
    assert taps - 1 <= CONV_HALO and t % CONV_ROWS == 0

    def body(ab_ref, ac_ref, au_ref, w_ref, o_ref):
        for lanes in _lane_groups(cb):
            def emit(rows, u, s, lanes=lanes):
                o_ref[rows, lanes] = (ab_ref[rows, lanes].astype(F32) * s).astype(o_ref.dtype)

            _conv_pass(t, taps, w_ref, lanes, lambda rows, lanes=lanes: ac_ref[rows, lanes].astype(F32) * au_ref[rows, lanes].astype(F32), emit)

    return pl.pallas_call(
        body, grid=(nb,),
        in_specs=[pl.BlockSpec((t, cb), lambda j, o=o: (0, o * nb + j)) for o in range(3)]
        + [pl.BlockSpec((taps, cb), lambda j: (0, j))],
        out_specs=pl.BlockSpec((t, cb), lambda j: (0, j)),
        out_shape=_sds((t, d), BF16), compiler_params=_params(), name="conv_a_fwd")(za, za, za, w)


def _finish_taps(dw_ref, dw_acc, lanes, taps):
    for k in range(taps):
        dw_ref[k:k + 1, lanes] = jnp.sum(dw_acc[k], axis=0, keepdims=True)


def _conv_a_bwd(za, w, dpa):
    t, d = za.shape[0], za.shape[1] // 3
    taps, cb = w.shape[0], _chan_block(d)
    nb = d // cb

    def body(ab_ref, ac_ref, au_ref, w_ref, dy_ref, dab_ref, dac_ref, dau_ref, dw_ref, dw_acc):
        for lanes in _lane_groups(cb):
            dw_acc[...] = jnp.zeros_like(dw_acc)

            def ds(rows, lanes=lanes):
                return dy_ref[rows, lanes].astype(F32) * ab_ref[rows, lanes].astype(F32)

            def emit_s(rows, u, s, lanes=lanes):
                dab_ref[rows, lanes] = (dy_ref[rows, lanes].astype(F32) * s).astype(dab_ref.dtype)

            def emit_du(rows, du, lanes=lanes):
                dac_ref[rows, lanes] = (du * au_ref[rows, lanes].astype(F32)).astype(dac_ref.dtype)
                dau_ref[rows, lanes] = (du * ac_ref[rows, lanes].astype(F32)).astype(dau_ref.dtype)

            _conv_pass(t, taps, w_ref, lanes, lambda rows, lanes=lanes: ac_ref[rows, lanes].astype(F32) * au_ref[rows, lanes].astype(F32), emit_s,
                       get_dy=ds, dw_acc=dw_acc)
            _conv_pass_t(t, taps, w_ref, lanes, ds, emit_du)
            _finish_taps(dw_ref, dw_acc, lanes, taps)

    blk = pl.BlockSpec((t, cb), lambda j: (0, j))
    return pl.pallas_call(
        body, grid=(nb,),
        in_specs=[pl.BlockSpec((t, cb), lambda j, o=o: (0, o * nb + j)) for o in range(3)]
        + [pl.BlockSpec((taps, cb), lambda j: (0, j)), blk],
        out_specs=[blk, blk, blk, pl.BlockSpec((taps, cb), lambda j: (0, j))],
        out_shape=[_sds((t, d), BF16)] * 3 + [_sds((taps, d), F32)],
        scratch_shapes=[pltpu.VMEM((taps, SUBLANES, LANES), F32)],
        compiler_params=_params(), name="conv_a_bwd")(za, za, za, w, dpa)


def _conv_c_fwd(zcg, w, bias):
    t, d = zcg.shape[0], w.shape[1]
    taps, cb = w.shape[0], _chan_block(d)
    nb = d // cb
    assert taps - 1 <= CONV_HALO and t % CONV_ROWS == 0

    def body(cv_ref, cg_ref, w_ref, b_ref, o_ref):
        for lanes in _lane_groups(cb):
            def emit(rows, u, s, lanes=lanes):
                o_ref[rows, lanes] = s + b_ref[:, lanes]

            _conv_pass(t, taps, w_ref, lanes,
                       lambda rows, lanes=lanes: cv_ref[rows, lanes].astype(F32) * jax.nn.sigmoid(cg_ref[rows, lanes].astype(F32)), emit)

    return pl.pallas_call(
        body, grid=(nb,),
        in_specs=[pl.BlockSpec((t, cb), lambda j, o=o: (0, o * nb + j)) for o in range(2)]
        + [pl.BlockSpec((taps, cb), lambda j: (0, j)), pl.BlockSpec((1, cb), lambda j: (0, j))],
        out_specs=pl.BlockSpec((t, cb), lambda j: (0, j)),
        out_shape=_sds((t, d), F32), compiler_params=_params(), name="conv_c_fwd")(zcg, zcg, w, bias)


def _conv_c_bwd(zcg, w, du2):
    t, d = zcg.shape[0], w.shape[1]
    taps, cb = w.shape[0], _chan_block(d)
    nb = d // cb

    def body(cv_ref, cg_ref, w_ref, dy_ref, dcv_ref, dcg_ref, dw_ref, db_ref, dw_acc, db_acc):
        for lanes in _lane_groups(cb):
            dw_acc[...] = jnp.zeros_like(dw_acc)
            db_acc[...] = jnp.zeros_like(db_acc)

            def dy(rows, lanes=lanes):
                return dy_ref[rows, lanes].astype(F32)

            def emit_s(rows, u, s, lanes=lanes):
                db_acc[...] += jnp.sum(dy_ref[rows, lanes].astype(F32).reshape(CONV_ROWS // SUBLANES, SUBLANES, LANES), axis=0)

            def emit_du(rows, du, lanes=lanes):
                cv, sg = cv_ref[rows, lanes].astype(F32), jax.nn.sigmoid(cg_ref[rows, lanes].astype(F32))
                dcv_ref[rows, lanes] = (du * sg).astype(dcv_ref.dtype)
                dcg_ref[rows, lanes] = (du * cv * sg * (1.0 - sg)).astype(dcg_ref.dtype)

            _conv_pass(t, taps, w_ref, lanes,
                       lambda rows, lanes=lanes: cv_ref[rows, lanes].astype(F32) * jax.nn.sigmoid(cg_ref[rows, lanes].astype(F32)), emit_s,
                       get_dy=dy, dw_acc=dw_acc)
            _conv_pass_t(t, taps, w_ref, lanes, dy, emit_du)
            _finish_taps(dw_ref, dw_acc, lanes, taps)
            db_ref[:, lanes] = jnp.sum(db_acc[...], axis=0, keepdims=True)

    blk = pl.BlockSpec((t, cb), lambda j: (0, j))
    return pl.pallas_call(
        body, grid=(nb,),
        in_specs=[pl.BlockSpec((t, cb), lambda j, o=o: (0, o * nb + j)) for o in range(2)]
        + [pl.BlockSpec((taps, cb), lambda j: (0, j)), blk],
        out_specs=[blk, blk, pl.BlockSpec((taps, cb), lambda j: (0, j)), pl.BlockSpec((1, cb), lambda j: (0, j))],
        out_shape=[_sds((t, d), BF16)] * 2 + [_sds((taps, d), F32), _sds((1, d), F32)],
        scratch_shapes=[pltpu.VMEM((taps, SUBLANES, LANES), F32), pltpu.VMEM((SUBLANES, LANES), F32)],
        compiler_params=_params(), name="conv_c_bwd")(zcg, zcg, w, du2)


def _tri(tb, t, i, lower):
    r = lax.broadcasted_iota(jnp.int32, (tb, t), 0) + i * tb
    c = lax.broadcasted_iota(jnp.int32, (tb, t), 1)
    return ((r >= c) if lower else (c >= r)).astype(F32)


def _fox_cumsum(zf, bf):
    t = zf.shape[0]
    tb = min(256, t)

    def body(z_ref, b_ref, c_ref):
        logf = jax.nn.log_sigmoid(z_ref[...] + b_ref[...])
        c_ref[...] = jnp.dot(_tri(tb, t, pl.program_id(0), True), logf, precision=lax.Precision.HIGHEST,
                             preferred_element_type=F32)

    return pl.pallas_call(
        body, grid=(t // tb,), in_specs=[_full_spec(zf), _full_spec(bf)],
        out_specs=pl.BlockSpec((tb, LANES), lambda i: (i, 0)), out_shape=_sds((t, LANES), F32),
        compiler_params=_params(), name="fox_cumsum")(zf, bf)


def _fox_cumsum_bwd(zf, bf, dc_t):
    t = zf.shape[0]
    tb = min(256, t)

    def body(z_ref, b_ref, dc_ref, dz_ref, db_ref):
        i = pl.program_id(0)

        @pl.when(i == 0)
        def _():
            db_ref[...] = jnp.zeros_like(db_ref)

        dlogf = jnp.dot(_tri(tb, t, i, False), dc_ref[...], precision=lax.Precision.HIGHEST, preferred_element_type=F32)
        dz = dlogf * jax.nn.sigmoid(-(z_ref[...] + b_ref[...]))
        dz_ref[...] = dz.astype(dz_ref.dtype)
        db_ref[...] += jnp.sum(dz, axis=0, keepdims=True)

    return pl.pallas_call(
        body, grid=(t // tb,),
        in_specs=[pl.BlockSpec((tb, LANES), lambda i: (i, 0)), _full_spec(bf), _full_spec(dc_t)],
        out_specs=[pl.BlockSpec((tb, LANES), lambda i: (i, 0)), pl.BlockSpec((1, LANES), lambda i: (0, 0))],
        out_shape=[_sds((t, LANES), BF16), _sds((1, LANES), F32)],
        compiler_params=_params(), name="fox_cumsum_bwd")(zf, bf, dc_t)


def _fox_block(q, k, v, cr, *, q0, hd):
    tq, kext = q.shape[0], k.shape[0]
    nh = LANES // hd
    lane = lax.broadcasted_iota(jnp.int32, (1, LANES), 1)
    causal = (lax.broadcasted_iota(jnp.int32, (tq, kext), 0) + q0) >= lax.broadcasted_iota(jnp.int32, (tq, kext), 1)
    kb, vb = k.astype(BF16), v.astype(BF16)
    out = jnp.zeros((tq, LANES), F32)
    for h in range(nh):
        m = (lane // hd) == h
        qh = jnp.where(m, q * (hd ** -0.5), 0.0).astype(BF16)
        s = lax.dot_general(qh, kb, (((1,), (1,)), ((), ())), preferred_element_type=F32)
        c_k = cr[h] - lax.stop_gradient(cr[h][:, q0:q0 + 1])
        s = jnp.where(causal, s - c_k, -1e30)
        e = jnp.exp(s - lax.stop_gradient(jnp.max(s, axis=1, keepdims=True)))
        prob = e / jnp.sum(e, axis=1, keepdims=True)
        o = jnp.dot(prob.astype(BF16), vb, preferred_element_type=F32)
        out = out + jnp.where(m, o, 0.0)
    return out


def _fox_dims(zqkv, crow):
    t, d = zqkv.shape[0], zqkv.shape[1] // 3
    hd = d // crow.shape[0]
    assert math.log2(hd) % 2 == 0
    return t, d, hd, d // LANES, LANES // hd, min(FOX_BLOCK, t)


def _fox_specs(i, tq, nblk, nh):
    kext = (i + 1) * tq
    return [pl.BlockSpec((tq, LANES), lambda p: (i, p)), pl.BlockSpec((kext, LANES), lambda p: (0, nblk + p)),
            pl.BlockSpec((kext, LANES), lambda p: (0, 2 * nblk + p)), pl.BlockSpec((nh, 1, kext), lambda p: (p, 0, 0))]


def _fox_fwd(zqkv, crow):
    t, d, hd, nblk, nh, tq = _fox_dims(zqkv, crow)
    o = None
    for i in range(t // tq):
        def body(q_ref, k_ref, v_ref, cr_ref, *rest, i=i):
            o_ref = rest[-1]
            o_ref[...] = _fox_block(q_ref[...].astype(F32), k_ref[...], v_ref[...], cr_ref[...],
                                    q0=i * tq, hd=hd).astype(o_ref.dtype)

        prev = [] if o is None else [o]
        o = pl.pallas_call(
            body, grid=(nblk,),
            in_specs=_fox_specs(i, tq, nblk, nh) + [pl.BlockSpec(memory_space=pl.ANY)] * len(prev),
            out_specs=pl.BlockSpec((tq, LANES), lambda p, i=i: (i, p)), out_shape=_sds((t, d), BF16),
            input_output_aliases={4: 0} if prev else {},
            compiler_params=_params(), name=f"fox_fwd_q{i}")(zqkv, zqkv, zqkv, crow, *prev)
    return o


def _fox_bwd(zqkv, crow, do):
    t, d, hd, nblk, nh, tq = _fox_dims(zqkv, crow)
    nq = t // tq
    acc = None
    for i in reversed(range(nq)):
        kext = (i + 1) * tq
        first = acc is None

        def body(q_ref, k_ref, v_ref, cr_ref, do_ref, *rest, first=first, i=i):
            if first:
                dq_ref, dk_ref, dv_ref, dcr_ref = rest
            else:
                dk_in, dv_in, dcr_in = rest[1:4]
                dq_ref, dk_ref, dv_ref, dcr_ref = rest[4:]
            f = functools.partial(_fox_block, q0=i * tq, hd=hd)
            _, vjp = jax.vjp(f, q_ref[...].astype(F32), k_ref[...].astype(F32), v_ref[...].astype(F32), cr_ref[...])
            dq, dk, dv, dcr = vjp(do_ref[...].astype(F32))
            dq_ref[...] = dq.astype(dq_ref.dtype)
            if first:
                dk_ref[...], dv_ref[...], dcr_ref[...] = dk, dv, dcr
            else:
                dk_ref[...] = dk_in[...] + dk
                dv_ref[...] = dv_in[...] + dv
                dcr_ref[...] = dcr_in[...] + dcr

        qblk = pl.BlockSpec((tq, LANES), lambda p, i=i: (i, p))
        kblk = pl.BlockSpec((kext, LANES), lambda p: (0, p))
        rblk = pl.BlockSpec((nh, 1, kext), lambda p: (p, 0, 0))
        if first:
            prev, prev_specs, aliases = [], [], {}
        else:
            prev, prev_specs = list(acc), [pl.BlockSpec(memory_space=pl.ANY), kblk, kblk, rblk]
            aliases = {5: 0, 6: 1, 7: 2, 8: 3}
        acc = pl.pallas_call(
            body, grid=(nblk,),
            in_specs=_fox_specs(i, tq, nblk, nh) + [qblk] + prev_specs,
            out_specs=[qblk, kblk, kblk, rblk],
            out_shape=[_sds((t, d), BF16), _sds((t, d), F32), _sds((t, d), F32), _sds(crow.shape, F32)],
            input_output_aliases=aliases,
            compiler_params=_params(), name=f"fox_bwd_q{i}")(zqkv, zqkv, zqkv, crow, do, *prev)
    return acc


def _xattn_block(q, k, v):
    s = lax.dot_general(q.astype(BF16), k.astype(BF16), (((1,), (1,)), ((), ())), preferred_element_type=F32)
    s = s * (q.shape[1] ** -0.5)
    e = jnp.exp(s - lax.stop_gradient(jnp.max(s, axis=1, keepdims=True)))
    prob = e / jnp.sum(e, axis=1, keepdims=True)
    return jnp.dot(prob.astype(BF16), v.astype(BF16), preferred_element_type=F32)


def _xattn_fwd(qx, kv):
    t, dx = qx.shape
    nm, nh, tq = kv.shape[0], qx.shape[1] // LANES, min(512, qx.shape[0])

    def body(q_ref, k_ref, v_ref, o_ref):
        o_ref[...] = _xattn_block(q_ref[...], k_ref[...], v_ref[...]).astype(o_ref.dtype)

    return pl.pallas_call(
        body, grid=(nh, t // tq),
        in_specs=[pl.BlockSpec((tq, LANES), lambda h, i: (i, h)), pl.BlockSpec((nm, LANES), lambda h, i: (0, h)),
                  pl.BlockSpec((nm, LANES), lambda h, i: (0, nh + h))],
        out_specs=pl.BlockSpec((tq, LANES), lambda h, i: (i, h)), out_shape=_sds((t, dx), BF16),
        compiler_params=_params(), name="xattn_fwd")(qx, kv, kv)


def _xattn_bwd(qx, kv, do):
    t, dx = qx.shape
    nm, nh, tq = kv.shape[0], qx.shape[1] // LANES, min(512, qx.shape[0])

    def body(q_ref, k_ref, v_ref, do_ref, dq_ref, dk_ref, dv_ref):
        _, vjp = jax.vjp(_xattn_block, q_ref[...].astype(F32), k_ref[...].astype(F32), v_ref[...].astype(F32))
        dq, dk, dv = vjp(do_ref[...].astype(F32))
        dq_ref[...] = dq.astype(dq_ref.dtype)

        @pl.when(pl.program_id(1) == 0)
        def _():
            dk_ref[...] = jnp.zeros_like(dk_ref)
            dv_ref[...] = jnp.zeros_like(dv_ref)

        dk_ref[...] += dk
        dv_ref[...] += dv

    qblk = pl.BlockSpec((tq, LANES), lambda h, i: (i, h))
    kblk = pl.BlockSpec((nm, LANES), lambda h, i: (0, h))
    return pl.pallas_call(
        body, grid=(nh, t // tq),
        in_specs=[qblk, kblk, pl.BlockSpec((nm, LANES), lambda h, i: (0, nh + h)), qblk],
        out_specs=[qblk, kblk, kblk],
        out_shape=[_sds((t, dx), BF16), _sds((nm, dx), F32), _sds((nm, dx), F32)],
        compiler_params=_params(), name="xattn_bwd")(qx, kv, kv, do)


def _row_tile(rows, cols, budget=1 << 20, step=2 * SUBLANES):
    best = None
    for tb in range(step, rows + 1, step):
        if rows % tb == 0 and tb * cols * 4 <= budget:
            best = tb
    return rows if best is None else best


def _adamw(w, g, m, v, keep_rank=False):
    shape = w.shape
    cols = shape[-1]
    if keep_rank:
        w2, g2, m2, v2 = w, g, m, v
        rows, tail = shape[0], shape[1:]
        tb = _row_tile(rows, math.prod(tail), step=1)
    else:
        w2, g2, m2, v2 = (a.reshape(-1, cols) for a in (w, g, m, v))
        rows, tail = w2.shape[0], (cols,)
        tb = _row_tile(rows, cols)

    def body(w_ref, g_ref, m_ref, v_ref, d_ref, nm_ref, nv_ref):
        gg = g_ref[...]
        nm = ADAM_B1 * m_ref[...] + (1.0 - ADAM_B1) * gg
        nv = ADAM_B2 * v_ref[...] + (1.0 - ADAM_B2) * (gg * gg)
        m_hat = nm / (1.0 - ADAM_B1 ** ADAM_STEP)
        v_hat = nv / (1.0 - ADAM_B2 ** ADAM_STEP)
        d_ref[...] = -ADAM_LR * (m_hat / (jnp.sqrt(v_hat) + ADAM_EPS) + ADAM_WD * w_ref[...])
        nm_ref[...] = nm
        nv_ref[...] = nv

    blk = pl.BlockSpec((tb,) + tail, lambda i: (i,) + (0,) * len(tail))
    out = pl.pallas_call(
        body, grid=(rows // tb,), in_specs=[blk] * 4, out_specs=[blk] * 3, out_shape=[_sds(w2.shape, F32)] * 3,
        compiler_params=_params(), name="adamw")(w2, g2, m2, v2)
    return tuple(o.reshape(shape) for o in out)


def _sum_devices(gathered, rows):
    cols = gathered.shape[1]

    def body(g_ref, o_ref):
        acc = g_ref[0:rows, :]
        for dev in range(1, N_DEV):
            acc = acc + g_ref[dev * rows:(dev + 1) * rows, :]
        o_ref[...] = acc

    return pl.pallas_call(body, out_shape=_sds((rows, cols), F32), compiler_params=_params(), name="sum_devices")(gathered)


def _place():
    x, y, c = lax.axis_index("x"), lax.axis_index("y"), lax.axis_index("c")
    chips = [(1 - x, y), (x, 1 - y), (1 - x, 1 - y)]
    return x, y, c, chips


def _chip_id(chip):
    return 2 * chip[0] + chip[1]


def _half(c, rows):
    rh = rows // 2
    return pl.ds(pl.multiple_of(c * rh, 16), rh)


_HBM = pl.BlockSpec(memory_space=pltpu.HBM)


def _allgather8(x_shard):
    m_per, n = x_shard.shape

    def body(x_ref, out_ref, send_sems, recv_sems, local_sem):
        x, y, c, chips = _place()
        me, sibling = (x, y, c), (x, y, 1 - c)

        def rows(px, py, pc):
            return out_ref.at[pl.ds((4 * px + 2 * py + pc) * m_per, m_per), :]

        def copy(k, block, to, src=None):
            return pltpu.make_async_remote_copy(
                src_ref=rows(*block) if src is None else src, dst_ref=rows(*block),
                send_sem=send_sems.at[k], recv_sem=recv_sems.at[k], device_id=to, device_id_type=MESH)

        mine = pltpu.make_async_copy(x_ref, rows(*me), local_sem)
        mine.start()
        first = [copy(0, me, sibling, src=x_ref)]
        first += [copy(1 + j, me, (*chip, c), src=x_ref) for j, chip in enumerate(chips)]
        for cp in first:
            cp.start()
        passed = [copy(4 + j, (*chip, c), sibling) for j, chip in enumerate(chips)]
        for j, chip in enumerate(chips):
            copy(1 + j, (*chip, c), me).wait_recv()
            passed[j].start()
        copy(0, sibling, me).wait_recv()
        for j, chip in enumerate(chips):
            copy(4 + j, (*chip, 1 - c), me).wait_recv()
        for cp in first + passed:
            cp.wait_send()
        mine.wait()

    return pl.pallas_call(
        body, out_shape=_sds((N_DEV * m_per, n), x_shard.dtype),
        in_specs=[pl.BlockSpec(memory_space=pltpu.VMEM)], out_specs=pl.BlockSpec(memory_space=pltpu.VMEM),
        scratch_shapes=[pltpu.SemaphoreType.DMA((7,)), pltpu.SemaphoreType.DMA((7,)), pltpu.SemaphoreType.DMA],
        compiler_params=_params(), name="allgather8")(x_shard)


def _gather_weights(ws, layer, after):
    n = len(ws)

    def body(*refs):
        w, out = refs[:n], refs[n + 1:2 * n + 1]
        send_i, recv_i, send_d, recv_d = refs[2 * n + 1:]
        x, y, c, chips = _place()
        sibling = (x, y, 1 - c)
        me_chip = 2 * x + y

        def slab(i, chip, half):
            return out[i].at[chip, _half(half, w[i].shape[1]), :]

        def over_ici(i, j, src_chip, to, src=None):
            return pltpu.make_async_remote_copy(
                src_ref=slab(i, src_chip, c) if src is None else src, dst_ref=slab(i, src_chip, c),
                send_sem=send_i.at[3 * i + j], recv_sem=recv_i.at[3 * i + j], device_id=to, device_id_type=MESH)

        def over_d2d(i, j, half):
            return pltpu.make_async_remote_copy(
                src_ref=slab(i, _chip_id(chips[j]), half), dst_ref=slab(i, _chip_id(chips[j]), half),
                send_sem=send_d.at[3 * i + j], recv_sem=recv_d.at[3 * i + j], device_id=sibling, device_id_type=MESH)

        sent = []
        for i in range(n):
            for j, chip in enumerate(chips):
                cp = over_ici(i, j, me_chip, (*chip, c), src=w[i].at[layer, _half(c, w[i].shape[1]), :])
                cp.start()
                sent.append(cp)
        for i in range(n):
            for j, chip in enumerate(chips):
                over_ici(i, j, _chip_id(chip), (x, y, c)).wait_recv()
                cp = over_d2d(i, j, c)
                cp.start()
                sent.append(cp)
        for i in range(n):
            for j in range(3):
                over_d2d(i, j, 1 - c).wait_recv()
        for cp in sent:
            cp.wait_send()

    return pl.pallas_call(
        body, in_specs=[_HBM] * n + [pl.BlockSpec(memory_space=pl.ANY)], out_specs=[_HBM] * n,
        out_shape=[_sds((N_CHIPS,) + a.shape[1:], a.dtype) for a in ws],
        scratch_shapes=[pltpu.SemaphoreType.DMA((3 * n,)) for _ in range(4)],
        compiler_params=_params(), name="gather_weights")(*ws, after)


_SEM = pl.BlockSpec(memory_space=pltpu.SEMAPHORE)
_DATAFLOW = pltpu.SideEffectType.DATAFLOW_SIDE_EFFECTING


def _gather_copies(w, land, send, recv, layer):
    x, y, c, chips = _place()
    return [pltpu.make_async_remote_copy(
        src_ref=w[i].at[layer], dst_ref=land[i].at[2 * x + y], send_sem=send.at[3 * i + j], recv_sem=recv.at[3 * i + j],
        device_id=(*chip, c), device_id_type=MESH) for i in range(len(w)) for j, chip in enumerate(chips)]


def _gather_start(ws, layer, after):
    n = len(ws)

    def body(*refs):
        w, land = refs[:n], refs[n:2 * n]
        send, recv = refs[2 * n + 1], refs[2 * n + 2]
        for cp in _gather_copies(w, land, send, recv, layer):
            cp.start()
        refs[-1][...] = jnp.zeros_like(refs[-1])

    lands = [pltpu.with_memory_space_constraint(lax.empty((N_CHIPS,) + a.shape[1:], a.dtype), pltpu.HBM) for a in ws]
    out = pl.pallas_call(
        body, name=f"gather_start_{layer}",
        out_shape=(pltpu.SemaphoreType.DMA((3 * n,)), pltpu.SemaphoreType.DMA((3 * n,)),
                   *[pltpu.HBM(a.shape, a.dtype) for a in lands], _sds((SUBLANES, LANES), F32)),
        in_specs=[_HBM] * (2 * n) + [pl.BlockSpec(memory_space=pl.ANY)],
        out_specs=(_SEM, _SEM, *[_HBM] * n, pl.BlockSpec(memory_space=pltpu.VMEM)),
        input_output_aliases={n + i: 2 + i for i in range(n)},
        compiler_params=pltpu.CompilerParams(has_side_effects=_DATAFLOW))(*ws, *lands, after)
    return out[0], out[1], list(out[2:2 + n]), out[-1]


def _gather_wait(send, recv, ws, lands, after, layer):
    n = len(ws)

    def body(*refs):
        for cp in _gather_copies(refs[:n], refs[n:2 * n], refs[2 * n], refs[2 * n + 1], layer):
            cp.wait_send()
            cp.wait_recv()

    out = pl.pallas_call(
        body, name=f"gather_wait_{layer}",
        out_shape=tuple(pltpu.HBM(a.shape, a.dtype) for a in lands),
        in_specs=[_HBM] * (2 * n) + [_SEM, _SEM, pl.BlockSpec(memory_space=pl.ANY)], out_specs=tuple([_HBM] * n),
        input_output_aliases={n + i: i for i in range(n)},
        compiler_params=pltpu.CompilerParams(has_side_effects=_DATAFLOW))(*ws, *lands, send, recv, after)
    return list(out)


def _swap_halves(gs):
    n = len(gs)

    def body(*refs):
        g, out = refs[:n], refs[n:2 * n]
        send, recv = refs[2 * n:]
        x, y, c, _ = _place()
        cps = [pltpu.make_async_remote_copy(
            src_ref=g[i].at[:, :, _half(1 - c, g[i].shape[2]), :], dst_ref=out[i],
            send_sem=send.at[i], recv_sem=recv.at[i], device_id=(x, y, 1 - c), device_id_type=MESH) for i in range(n)]
        for cp in cps:
            cp.start()
        for cp in cps:
            cp.wait()

    return pl.pallas_call(
        body, in_specs=[_HBM] * n, out_specs=[_HBM] * n,
        out_shape=[_sds(a.shape[:2] + (a.shape[2] // 2, a.shape[3]), a.dtype) for a in gs],
        scratch_shapes=[pltpu.SemaphoreType.DMA((n,)), pltpu.SemaphoreType.DMA((n,))],
        compiler_params=_params(), name="swap_halves")(*gs)


def _exchange_chips(ss):
    n = len(ss)

    def body(*refs):
        s, out = refs[:n], refs[n:2 * n]
        send, recv = refs[2 * n:]
        x, y, c, chips = _place()
        cps = []
        for i in range(n):
            for j, chip in enumerate(chips):
                cps.append(pltpu.make_async_remote_copy(
                    src_ref=s[i].at[:, _chip_id(chip)], dst_ref=out[i].at[j],
                    send_sem=send.at[3 * i + j], recv_sem=recv.at[3 * i + j], device_id=(*chip, c), device_id_type=MESH))
        for cp in cps:
            cp.start()
        for cp in cps:
            cp.wait()

    return pl.pallas_call(
        body, in_specs=[_HBM] * n, out_specs=[_HBM] * n,
        out_shape=[_sds((3, a.shape[0]) + a.shape[2:], a.dtype) for a in ss],
        scratch_shapes=[pltpu.SemaphoreType.DMA((3 * n,)), pltpu.SemaphoreType.DMA((3 * n,))],
        compiler_params=_params(), name="exchange_chips")(*ss)


def _exchange_copies(s, land, send, recv):
    x, y, c, chips = _place()
    return [pltpu.make_async_remote_copy(
        src_ref=s[i].at[:, _chip_id(chip)], dst_ref=land[i].at[j], send_sem=send.at[3 * i + j], recv_sem=recv.at[3 * i + j],
        device_id=(*chip, c), device_id_type=MESH) for i in range(len(s)) for j, chip in enumerate(chips)]


def _exchange_start(ss, layer):
    n = len(ss)

    def body(*refs):
        s, land = refs[:n], refs[n:2 * n]
        send, recv = refs[2 * n], refs[2 * n + 1]
        token = refs[-1]
        for cp in _exchange_copies(s, land, send, recv):
            cp.start()
        token[...] = jnp.zeros_like(token)

    lands = [lax.empty((3, a.shape[0]) + a.shape[2:], a.dtype) for a in ss]
    operands = [pltpu.with_memory_space_constraint(a, pltpu.HBM) for a in list(ss) + lands]
    out = pl.pallas_call(
        body, name=f"exchange_start_{layer}",
        out_shape=(pltpu.SemaphoreType.DMA((3 * n,)), pltpu.SemaphoreType.DMA((3 * n,)),
                   *[pltpu.HBM(a.shape, a.dtype) for a in operands], _sds((SUBLANES, LANES), F32)),
        in_specs=[_HBM] * (2 * n), out_specs=(_SEM, _SEM, *[_HBM] * (2 * n), pl.BlockSpec(memory_space=pltpu.VMEM)),
        input_output_aliases={i: 2 + i for i in range(2 * n)},
        compiler_params=pltpu.CompilerParams(has_side_effects=_DATAFLOW))(*operands)
    return out[0], out[1], list(out[2:2 + n]), list(out[2 + n:2 + 2 * n]), out[-1]


def _exchange_wait(send, recv, ss, lands, after, layer):
    n = len(ss)

    def body(*refs):
        s, land = refs[:n], refs[n:2 * n]
        for cp in _exchange_copies(s, land, refs[2 * n], refs[2 * n + 1]):
            cp.wait_send()
            cp.wait_recv()

    out = pl.pallas_call(
        body, name=f"exchange_wait_{layer}",
        out_shape=tuple(pltpu.HBM(a.shape, a.dtype) for a in list(ss) + list(lands)),
        in_specs=[_HBM] * (2 * n) + [_SEM, _SEM, pl.BlockSpec(memory_space=pl.ANY)], out_specs=tuple([_HBM] * (2 * n)),
        input_output_aliases={i: i for i in range(2 * n)},
        compiler_params=pltpu.CompilerParams(has_side_effects=_DATAFLOW))(*ss, *lands, send, recv, after)
    return list(out[:n]), list(out[n:])


def _join_halves(fs):
    n = len(fs)

    def body(*refs):
        f, out = refs[:n], refs[n:2 * n]
        send, recv = refs[2 * n:]
        x, y, c, _ = _place()

        def half(ref, which):
            return ref.at[:, _half(which, ref.shape[1]), :]

        for i in range(n):
            pltpu.make_async_remote_copy(
                src_ref=half(f[i], c), dst_ref=half(out[i], c), send_sem=send.at[i], recv_sem=recv.at[i],
                device_id=(x, y, 1 - c), device_id_type=MESH).start()
        for i in range(n):
            arrive = pltpu.make_async_remote_copy(
                src_ref=half(f[i], c), dst_ref=half(out[i], 1 - c), send_sem=send.at[i], recv_sem=recv.at[i],
                device_id=(x, y, 1 - c), device_id_type=MESH)
            arrive.wait_recv()
            arrive.wait_send()

    return pl.pallas_call(
        body, in_specs=[_HBM] * n, out_specs=[_HBM] * n, out_shape=[_sds(a.shape, a.dtype) for a in fs],
        input_output_aliases={i: i for i in range(n)},
        scratch_shapes=[pltpu.SemaphoreType.DMA((n,)), pltpu.SemaphoreType.DMA((n,))],
        compiler_params=_params(), name="join_halves")(*fs)


def _add_sibling(g, got, c_idx):
    nl, ns, r, cols = g.shape
    rh = r // 2
    tb = _row_tile(rh, cols, budget=ADD_BLOCK_BYTES)
    nb = rh // tb
    g3, got3 = g.reshape(nl * ns, r, cols), got.reshape(nl * ns, rh, cols)

    def body(c_ref, a_ref, b_ref, o_ref):
        o_ref[...] = (a_ref[...].astype(F32) + b_ref[...].astype(F32)).astype(o_ref.dtype)

    out = pl.pallas_call(
        body,
        grid_spec=pltpu.PrefetchScalarGridSpec(
            num_scalar_prefetch=1, grid=(nl * ns, nb),
            in_specs=[pl.BlockSpec((None, tb, cols), lambda s, i, c_ref: (s, c_ref[0] * nb + i, 0)),
                      pl.BlockSpec((None, tb, cols), lambda s, i, c_ref: (s, i, 0))],
            out_specs=pl.BlockSpec((None, tb, cols), lambda s, i, c_ref: (s, i, 0))),
        out_shape=_sds((nl * ns, rh, cols), BF16), compiler_params=_params(), name="add_sibling")(c_idx, g3, got3)
    return out.reshape(nl, ns, rh, cols)


def _add_chips(s, got, chip_idx, c_idx, layer, depth, prev):
    _, ns, rh, cols = s.shape
    tb = _row_tile(rh, cols, budget=ADD_BLOCK_BYTES)
    nb = rh // tb

    def body(k_ref, c_ref, a_ref, b0_ref, b1_ref, b2_ref, *rest):
        o_ref = rest[-1]
        o_ref[...] = ((a_ref[...].astype(F32) + b0_ref[...].astype(F32)) + b1_ref[...].astype(F32)) + b2_ref[...].astype(F32)

    extra = [] if prev is None else [prev]
    return pl.pallas_call(
        body,
        grid_spec=pltpu.PrefetchScalarGridSpec(
            num_scalar_prefetch=2, grid=(nb,),
            in_specs=[pl.BlockSpec((None, None, tb, cols), lambda i, k_ref, c_ref: (0, k_ref[0], i, 0))]
            + [pl.BlockSpec((None, None, tb, cols), lambda i, k_ref, c_ref, j=j: (j, 0, i, 0)) for j in range(3)]
            + [pl.BlockSpec(memory_space=pl.ANY)] * len(extra),
            out_specs=pl.BlockSpec((None, tb, cols), lambda i, k_ref, c_ref: (layer, c_ref[0] * nb + i, 0))),
        out_shape=_sds((depth, 2 * rh, cols), F32), input_output_aliases={6: 0} if extra else {},
        compiler_params=_params(), name="add_chips")(chip_idx, c_idx, s, got, got, got, *extra)


def _in_pieces(d, heads, ws):
    groups, start = [], 0
    for k in range(12):
        if k == 6:
            groups.append(("f", 0, start, heads))
            start += heads
        else:
            groups.append(("main", (k - (k > 6)) * d, start, d))
            start += d
    pieces = []
    for chip in range(N_CHIPS):
        lo, hi = chip * ws, (chip + 1) * ws
        for dest, dcol, gstart, w in groups:
            a, b = max(lo, gstart), min(hi, gstart + w)
            if a < b:
                pieces.append((chip, a - lo, dest, dcol + a - gstart, b - a))
    return pieces


def _w_in_aligned(gathered, d, heads):
    rows, ws = gathered.shape[1], gathered.shape[2]
    tb = min(256, rows)
    pieces = _in_pieces(d, heads, ws)

    def body(g_ref, m_ref, f_ref):
        f_ref[...] = jnp.zeros_like(f_ref)
        for chip, sc, dest, dc, w in pieces:
            (m_ref if dest == "main" else f_ref)[:, dc:dc + w] = g_ref[chip, :, sc:sc + w]

    return pl.pallas_call(
        body, grid=(rows // tb,), in_specs=[pl.BlockSpec((N_CHIPS, tb, ws), lambda i: (0, i, 0))],
        out_specs=[pl.BlockSpec((tb, 11 * d), lambda i: (i, 0)), pl.BlockSpec((tb, LANES), lambda i: (i, 0))],
        out_shape=[_sds((rows, 11 * d), gathered.dtype), _sds((rows, LANES), gathered.dtype)],
        compiler_params=_params(), name="w_in_aligned")(gathered)


def _w_in_shards(gms, gf, prev, l, shape, heads):
    lo, hi = gms
    rows, d = lo.shape[0], lo.shape[1] // 6
    ws = shape[3]
    tb = min(256, rows)
    pieces = _in_pieces(d, heads, ws)

    def body(lo_ref, hi_ref, f_ref, *rest):
        o_ref = rest[-1]
        for chip, sc, dest, dc, w in pieces:
            if dest == "f":
                src = f_ref[:, dc:dc + w]
            elif dc < 6 * d:
                src = lo_ref[:, dc:dc + w]
            else:
                src = hi_ref[:, dc - 6 * d:dc - 6 * d + w]
            o_ref[chip, :, sc:sc + w] = src

    extra = [] if prev is None else [prev]
    return pl.pallas_call(
        body, grid=(rows // tb,),
        in_specs=[pl.BlockSpec((tb, 6 * d), lambda i: (i, 0)), pl.BlockSpec((tb, 5 * d), lambda i: (i, 0)),
                  pl.BlockSpec((tb, LANES), lambda i: (i, 0))] + [pl.BlockSpec(memory_space=pl.ANY)] * len(extra),
        out_specs=pl.BlockSpec((None, N_CHIPS, tb, ws), lambda i: (l, 0, i, 0)), out_shape=_sds(shape, lo.dtype),
        input_output_aliases={3: 0} if extra else {},
        compiler_params=_params(), name="w_in_shards")(lo, hi, gf, *extra)


def _pad_lanes(a, width=LANES):
    return jnp.pad(a, ((0, 0), (0, width - a.shape[1])))


def _cols(operand):
    return sum(p[2] for p in operand) if isinstance(operand, list) else operand.shape[1]


def _local_step(x, mem, tgt, sp, wt, dests=None, on_layer=None):
    t, d = x.shape
    depth = sp["mix_norm"].shape[0]
    heads = sp["b_forget"].shape[1]
    row = lambda a, l: a[l:l + 1]
    weights = wt if callable(wt) else (lambda l, x_in: {k: v[l] for k, v in wt.items()})
    layer_weights = []

    memn, = _rows("rms_mem", _rms_fn, [_part(mem)], [sp["mem_norm"][None]], [(d, BF16)])
    saved = []
    for l in range(depth):
        wl = weights(l, x)
        layer_weights.append(wl)
        f = _cols(wl["gu"]) // 2
        sv = {"x0": x}
        h1, = _rows("rms_mix", _rms_fn, [_part(x)], [row(sp["mix_norm"], l)], [(d, BF16)])
        wm = wl["main"]
        za = _mm("mm_in_a", h1, [_part(wm, 0, 3 * d)], out_dtype=BF16)
        zqkv = _mm("mm_in_qkv", h1, [_part(wm, 3 * d, 3 * d)], out_dtype=BF16)
        zcg = _mm("mm_in_cg", h1, [_part(wm, 6 * d, 5 * d)], out_dtype=BF16)
        zf = _mm("mm_in_f", h1, wl["f"])
        bf = _pad_lanes(row(sp["b_forget"], l))
        c = _fox_cumsum(zf, bf)
        crow = c[:, :heads].T.reshape(heads, 1, t)
        o = _fox_fwd(zqkv, crow)
        pa = _conv_a_fwd(za, sp["conv_a"][l])
        u2 = _conv_c_fwd(zcg, sp["conv_c"][l], row(sp["conv_c_bias"], l))
        u3, = _rows("ln_silu", _lnsilu_fn, [_part(u2)], [row(sp["ln_c_gain"], l), row(sp["ln_c_bias"], l)], [(d, BF16)])
        ya = _mm("mm_out_a", pa, wl["out_a"])
        yb = _mm("mm_out_b", o, wl["out_b"])
        yc = _mm("mm_out_c", u3, wl["out_c"])
        bg = [sp["b_gate"][l:l + 1, k * d:(k + 1) * d] for k in range(3)]
        gate_rows = [_part(zcg, (2 + k) * d, d) for k in range(3)] + [_part(ya), _part(yb), _part(yc)]
        mg, = _rows("merge", _merge_fn, gate_rows, bg, [(d, BF16)])
        x1 = _mm("mm_o", mg, wl["o"], add=x)
        h2, = _rows("rms_x", _rms_fn, [_part(x1)], [row(sp["xattn_norm"], l)], [(d, BF16)])
        qx = _mm("mm_xq", h2, wl["xq"], out_dtype=BF16)
        kv = _mm("mm_xkv", memn, wl["xkv"], out_dtype=BF16)
        ox = _xattn_fwd(qx, kv)
        x2 = _mm("mm_xo", ox, wl["xo"], add=x1)
        h3, = _rows("rms_ffn", _rms_fn, [_part(x2)], [row(sp["ffn_norm"], l)], [(d, BF16)])
        gu = _mm("mm_gu", h3, wl["gu"], out_dtype=BF16)
        act, = _rows("swiglu", _swiglu_fn, [_part(gu, 0, f), _part(gu, f, f)], [], [(f, BF16)])
        x3 = _mm("mm_down", act, wl["down"], add=x2)
        sv.update(h1=h1, za=za, zqkv=zqkv, zcg=zcg, zf=zf, bf=bf, crow=crow, o=o, pa=pa, u2=u2, u3=u3, ya=ya, yb=yb,
                  yc=yc, bg=bg, gate_rows=gate_rows, mg=mg, x1=x1, h2=h2, qx=qx, kv=kv, ox=ox, x2=x2, h3=h3, gu=gu, act=act)
        saved.append(sv)
        x = x3

    loss_row, dx, d_final = _loss_call(x, tgt, sp["final_norm"][None])

    gs = {k: [None] * depth for k in ("mix_norm", "b_gate", "b_forget", "conv_a", "conv_c", "conv_c_bias", "ln_c_gain",
                                      "ln_c_bias", "xattn_norm", "ffn_norm")}
    gw = {k: [None] * depth for k in ("main", "f", "out_a", "out_b", "out_c", "o", "xq", "xkv", "xo", "gu", "down")}
    def weight_grad(key, l, name, a_op, b_op):
        if dests is None or key not in dests:
            gw[key][l] = _mm(name, a_op, b_op, ta=True, out_dtype=BF16)
        else:
            shape, shards = dests[key]
            gw[key][l] = _mm(name, a_op, b_op, ta=True, out_dtype=BF16, dest=(shape, None, (0,), shards))

    dmemn = None
    for l in reversed(range(depth)):
        sv, wl = saved[l], layer_weights[l]
        dact = _mm("mm_down_dx", dx, wl["down"], tb=True)
        weight_grad("down", l, "mm_down_dw", sv["act"], dx)
        (dg, du), _ = _rows_vjp("swiglu_bwd", _swiglu_fn, [_part(sv["gu"], 0, f), _part(sv["gu"], f, f)], [], [_part(dact)],
                                [BF16, BF16], [])
        dgu = [_part(dg), _part(du)]
        dh3 = _mm("mm_gu_dx", dgu, wl["gu"], tb=True)
        weight_grad("gu", l, "mm_gu_dw", sv["h3"], dgu)
        (dx2,), (gs["ffn_norm"][l],) = _rows_vjp("rms_ffn_bwd", _rms_fn, [_part(sv["x2"])], [row(sp["ffn_norm"], l)],
                                                 [_part(dh3)], [F32], [True], adds={0: _part(dx)})
        dox = _mm("mm_xo_dx", dx2, wl["xo"], tb=True)
        weight_grad("xo", l, "mm_xo_dw", sv["ox"], dx2)
        dqx, dkx, dvx = _xattn_bwd(sv["qx"], sv["kv"], dox)
        dh2 = _mm("mm_xq_dx", dqx, wl["xq"], tb=True)
        weight_grad("xq", l, "mm_xq_dw", sv["h2"], dqx)
        dkv = [_part(dkx), _part(dvx)]
        dmemn = _mm("mm_xkv_dx", dkv, wl["xkv"], tb=True, add=dmemn)
        weight_grad("xkv", l, "mm_xkv_dw", memn, dkv)
        (dx1,), (gs["xattn_norm"][l],) = _rows_vjp("rms_x_bwd", _rms_fn, [_part(sv["x1"])], [row(sp["xattn_norm"], l)],
                                                   [_part(dh2)], [F32], [True], adds={0: _part(dx2)})
        dmg = _mm("mm_o_dx", dx1, wl["o"], tb=True)
        weight_grad("o", l, "mm_o_dw", sv["mg"], dx1)
        (dga, dgb, dgc, dya, dyb, dyc), dbg = _rows_vjp("merge_bwd", _merge_fn, sv["gate_rows"], sv["bg"], [_part(dmg)],
                                                        [BF16] * 6, [True] * 3)
        gs["b_gate"][l] = jnp.concatenate(dbg, axis=1)
        dpa = _mm("mm_out_a_dx", dya, wl["out_a"], tb=True)
        weight_grad("out_a", l, "mm_out_a_dw", sv["pa"], dya)
        do = _mm("mm_out_b_dx", dyb, wl["out_b"], tb=True)
        weight_grad("out_b", l, "mm_out_b_dw", sv["o"], dyb)
        du3 = _mm("mm_out_c_dx", dyc, wl["out_c"], tb=True)
        weight_grad("out_c", l, "mm_out_c_dw", sv["u3"], dyc)
        if on_layer is not None:
            du3 = du3 + on_layer(l, "early", {k: v[l] for k, v in gw.items() if k not in ("main", "f")})[0, 0]
        (du2,), (gs["ln_c_gain"][l], gs["ln_c_bias"][l]) = _rows_vjp(
            "ln_silu_bwd", _lnsilu_fn, [_part(sv["u2"])], [row(sp["ln_c_gain"], l), row(sp["ln_c_bias"], l)], [_part(du3)],
            [F32], [True, True])
        dcv, dcg, gs["conv_c"][l], gs["conv_c_bias"][l] = _conv_c_bwd(sv["zcg"], sp["conv_c"][l], du2)
        dab, dac, dau, gs["conv_a"][l] = _conv_a_bwd(sv["za"], sp["conv_a"][l], dpa)
        dq, dk, dv, dcr = _fox_bwd(sv["zqkv"], sv["crow"], do)
        dzf, dbf = _fox_cumsum_bwd(sv["zf"], sv["bf"], _pad_lanes(dcr.reshape(heads, t).T))
        gs["b_forget"][l] = dbf[:, :heads]
        dz = [_part(a) for a in (dab, dac, dau, dq, dk, dv, dcv, dcg, dga, dgb, dgc)]
        dh1f = _mm("mm_in_f_dx", dzf, wl["f"], tb=True)
        dh1 = _mm("mm_in_dx", dz, wl["main"], tb=True, add=dh1f)
        gw["main"][l] = [_mm("mm_in_dw_lo", sv["h1"], dz[:6], ta=True, out_dtype=BF16),
                         _mm("mm_in_dw_hi", sv["h1"], dz[6:], ta=True, out_dtype=BF16)]
        gw["f"][l] = _mm("mm_in_f_dw", sv["h1"], dzf, ta=True, out_dtype=BF16)
        (dx,), (gs["mix_norm"][l],) = _rows_vjp("rms_mix_bwd", _rms_fn, [_part(sv["x0"])], [row(sp["mix_norm"], l)],
                                                [_part(dh1)], [F32], [True], adds={0: _part(dx1)})
        if on_layer is not None:
            token = on_layer(l, "late", {"main": gw["main"][l], "f": gw["f"][l]})
            if l > 0:
                dx = dx + token[0, 0]

    _, (d_mem_norm,) = _rows_vjp("rms_mem_bwd", _rms_fn, [_part(mem)], [sp["mem_norm"][None]], [_part(dmemn)], [None], [True])
    small = {k: jnp.stack([a.reshape(sp[k].shape[1:]) for a in v]) for k, v in gs.items()}
    small["mem_norm"] = d_mem_norm[0]
    small["final_norm"] = d_final[0]
    return loss_row, dx, small, gw


_BIG = ("w_in", "w_out_a", "w_out_b", "w_out_c", "w_o", "w_xq", "w_xkv", "w_xo", "w_gate_up", "w_down")
_COL_SHARDED = ("w_in", "w_xo", "w_gate_up")
_KEY = {"w_out_a": "out_a", "w_out_b": "out_b", "w_out_c": "out_c", "w_o": "o", "w_xq": "xq", "w_xkv": "xkv", "w_xo": "xo",
        "w_gate_up": "gu", "w_down": "down"}
_SMALL = ("mix_norm", "b_gate", "b_forget", "conv_c_bias", "ln_c_gain", "ln_c_bias", "xattn_norm", "ffn_norm", "mem_norm",
          "final_norm")
_SMALL_SHARDED = ("conv_a", "conv_c")


def _pack_rows(parts):
    padded = []
    for a in parts:
        pad = -a.shape[0] % SUBLANES
        padded.append(jnp.pad(a, ((0, pad), (0, 0))) if pad else a)
    return jnp.concatenate(padded, axis=0)


def _unpack_rows(packed, shapes):
    out, pos = [], 0
    for r in shapes:
        out.append(packed[pos:pos + r])
        pos += r + (-r % SUBLANES)
    return out


def kernel(x, mem, mix_norm, w_in, b_gate, b_forget, conv_a, w_out_a, w_out_b, conv_c, conv_c_bias, ln_c_gain, ln_c_bias, w_out_c, w_o, xattn_norm, mem_norm, w_xq, w_xkv, w_xo, ffn_norm, w_gate_up, w_down, final_norm, loss_target, m_mix_norm, m_w_in, m_b_gate, m_b_forget, m_conv_a, m_w_out_a, m_w_out_b, m_conv_c, m_conv_c_bias, m_ln_c_gain, m_ln_c_bias, m_w_out_c, m_w_o, m_xattn_norm, m_mem_norm, m_w_xq, m_w_xkv, m_w_xo, m_ffn_norm, m_w_gate_up, m_w_down, m_final_norm, v_mix_norm, v_w_in, v_b_gate, v_b_forget, v_conv_a, v_w_out_a, v_w_out_b, v_conv_c, v_conv_c_bias, v_ln_c_gain, v_ln_c_bias, v_w_out_c, v_w_o, v_xattn_norm, v_mem_norm, v_w_xq, v_w_xkv, v_w_xo, v_ffn_norm, v_w_gate_up, v_w_down, v_final_norm):
    args = dict(locals())
    names = ["mix_norm", "w_in", "b_gate", "b_forget", "conv_a", "w_out_a", "w_out_b", "conv_c", "conv_c_bias", "ln_c_gain",
             "ln_c_bias", "w_out_c", "w_o", "xattn_norm", "mem_norm", "w_xq", "w_xkv", "w_xo", "ffn_norm", "w_gate_up", "w_down",
             "final_norm"]
    depth, d = mix_norm.shape
    heads = b_forget.shape[1]
    cx, cy, cc = lax.axis_index("x"), lax.axis_index("y"), lax.axis_index("c")
    chip = 2 * cx + cy
    c_idx = jnp.reshape(cc, (1,)).astype(jnp.int32)
    chip_idx = jnp.reshape(chip, (1,)).astype(jnp.int32)

    conv_rows = [conv_a.shape[1] * depth, conv_c.shape[1] * depth]
    conv_pack = _pack_rows([conv_a.reshape(conv_rows[0], -1), conv_c.reshape(conv_rows[1], -1)])
    conv_all = _allgather8(conv_pack).reshape(N_CHIPS, 2, conv_pack.shape[0], conv_pack.shape[1])[:, 0]
    conv_all = conv_all.transpose(1, 0, 2).reshape(conv_pack.shape[0], d)
    conv_a_full, conv_c_full = _unpack_rows(conv_all, conv_rows)
    own = [args[n].astype(BF16) for n in _BIG]
    first = _gather_weights(own, 0, conv_all)
    started, tokens = {}, []
    for l in range(1, depth):
        send, recv, lands, token = _gather_start(own, l, first[0])
        started[l] = (send, recv, own, lands)
        tokens.append(token)

    def layer_weights(l, x_in):
        if l == 0:
            lands = first
        else:
            lands = _gather_wait(*started[l], x_in, l)
        full = [lax.dynamic_update_slice(g, w[l][None], (chip, 0, 0)) for g, w in zip(lands, own)]
        wl = {}
        wl["main"], wl["f"] = _w_in_aligned(full[0], d, heads)
        for n, g in zip(_BIG[1:], full[1:]):
            if n in _COL_SHARDED:
                wl[_KEY[n]] = [_part(g, lead=(j,)) for j in range(N_CHIPS)]
            else:
                wl[_KEY[n]] = g.reshape(N_CHIPS * g.shape[1], g.shape[2])
        return wl

    shard_shape = {n: (1, N_CHIPS) + args[n].shape[1:] for n in _BIG}
    dests = {_KEY[n]: (shard_shape[n], N_CHIPS) if n in _COL_SHARDED
             else ((1, N_CHIPS * args[n].shape[1], args[n].shape[2]), 1) for n in _BIG[1:]}
    sp = {n: args[n] for n in _SMALL}
    sp["conv_a"] = conv_a_full.reshape(depth, conv_a.shape[1], d)
    sp["conv_c"] = conv_c_full.reshape(depth, conv_c.shape[1], d)

    in_flight = {}

    def on_layer(l, which, gl):
        if which == "late":
            group = ["w_in"]
            local = [_w_in_shards(gl["main"], gl["f"], None, 0, shard_shape["w_in"], heads)]
        else:
            group = list(_BIG[1:])
            local = [gl[_KEY[n]].reshape(shard_shape[n]) for n in group]
        chip_sums = [_add_sibling(g, r, c_idx) for g, r in zip(local, _swap_halves(local))]
        send, recv, chip_sums, lands, token = _exchange_start(chip_sums, f"{l}_{which}")
        in_flight[(l, which)] = (group, send, recv, chip_sums, lands)
        return token

    x_in = x[0]
    for token in tokens:
        x_in = x_in + token[0, 0]
    loss_row, grad_x, gsmall, gw = _local_step(x_in, mem[0], loss_target[0], sp, layer_weights, dests, on_layer)
    halves = {n: None for n in _BIG}
    for l in reversed(range(depth)):
        for which in ("early", "late"):
            group, send, recv, chip_sums, lands = in_flight[(l, which)]
            chip_sums, lands = _exchange_wait(send, recv, chip_sums, lands, grad_x, f"{l}_{which}")
            for n, s_, r in zip(group, chip_sums, lands):
                halves[n] = _add_chips(s_, r, chip_idx, c_idx, l, depth, halves[n])
    halves = [halves[n] for n in _BIG]
    grads = dict(zip(_BIG, _join_halves(halves)))

    def as_rows(n, a):
        return _pad_lanes(a, d) if n == "b_forget" else a.reshape(-1, d)

    pieces = [as_rows(n, gsmall[n]) for n in _SMALL]
    for n in _SMALL_SHARDED:
        pieces.append(gsmall[n].reshape(-1, d))
    pieces.append(_pad_lanes(loss_row, d))
    pack = _pack_rows(pieces)
    total = _sum_devices(_allgather8(pack), pack.shape[0])
    shapes = [p.shape[0] for p in pieces]
    summed = _unpack_rows(total, shapes)
    for n, g in zip(_SMALL, summed):
        grads[n] = g[:, :heads] if n == "b_forget" else g.reshape(args[n].shape)
    for n, g in zip(_SMALL_SHARDED, summed[len(_SMALL):]):
        g = g.reshape(args[n].shape[:2] + (d,))
        grads[n] = lax.dynamic_slice_in_dim(g, chip * args[n].shape[2], args[n].shape[2], axis=2)
    loss = summed[-1][0, 0]

    delta, new_m, new_v = {}, {}, {}
    packed = [_pack_rows([as_rows(n, src[pre + n]) for n in _SMALL])
              for src, pre in ((args, ""), (grads, ""), (args, "m_"), (args, "v_"))]
    srows = [as_rows(n, args[n]).shape[0] for n in _SMALL]
    updated = [_unpack_rows(a, srows) for a in _adamw(*packed)]
    for k, n in enumerate(_SMALL):
        delta[n], new_m[n], new_v[n] = (
            (u[k][:, :heads] if n == "b_forget" else u[k]).reshape(args[n].shape) for u in updated)
    for n in _BIG + _SMALL_SHARDED:
        if n == "w_in":
            to_t, from_t = (lambda a: jnp.transpose(a, (2, 0, 1))), (lambda a: jnp.transpose(a, (1, 2, 0)))
            g_t = to_t(grads[n])
            grads[n] = from_t(g_t)
            delta[n], new_m[n], new_v[n] = (from_t(a) for a in _adamw(to_t(args[n]), g_t, to_t(args["m_" + n]), to_t(args["v_" + n]),
                                                                     keep_rank=True))
            continue
        delta[n], new_m[n], new_v[n] = _adamw(args[n], grads[n], args["m_" + n], args["v_" + n])

    return (loss, grad_x[None], *[grads[n] for n in names], *[delta[n] for n in names], *[new_m[n] for n in names],
            *[new_v[n] for n in names])
```

```python
import functools
import math

import jax
import jax.numpy as jnp
from jax import lax
from jax.experimental import pallas as pl
from jax.experimental.pallas import tpu as pltpu

F32, BF16 = jnp.float32, jnp.bfloat16
EPS = 1e-6
LANES = 128
SUBLANES = 8
VMEM_LIMIT = 56 * 1024 * 1024
MM_VMEM_BUDGET = 44 * 1024 * 1024
ADD_BLOCK_BYTES = 4 * 1024 * 1024
MESH = pl.DeviceIdType.MESH
N_CHIPS = 4
N_DEV = 8

ADAM_LR, ADAM_B1, ADAM_B2, ADAM_EPS, ADAM_WD, ADAM_STEP = 0.001, 0.9, 0.999, 1e-08, 0.01, 10


def _params(**kw):
    return pltpu.CompilerParams(vmem_limit_bytes=VMEM_LIMIT, **kw)


def _sds(shape, dtype):
    return jax.ShapeDtypeStruct(tuple(shape), dtype)


def _part(arr, off=0, width=None, lead=()):
    assert arr.ndim == len(lead) + 2
    return (arr, off, arr.shape[-1] - off if width is None else width, tuple(lead))


def _unit(parts):
    u = 0
    for _, off, w, _ in parts:
        u = math.gcd(u, math.gcd(off, w))
    return u


def _pick(dim, unit, cands, lane):
    for c in cands:
        if c <= dim and dim % c == 0 and unit % c == 0 and (not lane or c % LANES == 0):
            return c
    return min(dim, unit)


_COL_TILES = (1024, 2816, 1408, 512, 256, 128)
_ROW_TILES = (512, 1408, 256, 128)
LARGE_ROW_TILE = 1024


def _mm(name, a, b, *, ta=False, tb=False, out_dtype=F32, add=None, dest=None):
    if not isinstance(a, list):
        a = [_part(a)]
    if not isinstance(b, list):
        b = [_part(b)]
    a_rows, a_cols = a[0][0].shape[-2], sum(p[2] for p in a)
    b_rows, b_cols = b[0][0].shape[-2], sum(p[2] for p in b)
    M, Ka = (a_cols, a_rows) if ta else (a_rows, a_cols)
    Kb, N = (b_cols, b_rows) if tb else (b_rows, b_cols)
    assert Ka == Kb, (name, Ka, Kb)
    K = Ka
    ua, ub = _unit(a), _unit(b)
    shards = 1 if dest is None else dest[3]
    tm = _pick(M, ua if ta else M, _ROW_TILES, ta)
    tn = _pick(N, math.gcd(N if tb else ub, N // shards), _COL_TILES, not tb)
    ku = math.gcd(K if ta else ua, ub if tb else K)
    tk = _pick(K, ku, _COL_TILES, (not ta) or tb)
    tm_large = _pick(M, ua if ta else M, (LARGE_ROW_TILE,), ta)
    a_size = sum(p[0].dtype.itemsize for p in a)
    b_size = sum(p[0].dtype.itemsize for p in b)

    def vmem_estimate():
        return (2 * tm * tk * a_size + 2 * tk * tn * b_size + tm * tn * (2 * jnp.dtype(out_dtype).itemsize + 4)
                + (8 * tm * tn if add is not None else 0))

    if tm_large == LARGE_ROW_TILE:
        tm, tm_default = tm_large, tm
        if vmem_estimate() > MM_VMEM_BUDGET:
            tm = tm_default
    while vmem_estimate() > MM_VMEM_BUDGET:
        if tk % 256 == 0 and tk >= tn:
            tk //= 2
        elif tn % 256 == 0:
            tn //= 2
        elif tm % 256 == 0:
            tm //= 2
        elif tk % 256 == 0:
            tk //= 2
        else:
            break
    nm, nn, nk = M // tm, N // tn, K // tk
    a_tile, b_tile = (tm if ta else tk), (tk if tb else tn)
    a_nb, b_nb = a[0][2] // a_tile, b[0][2] // b_tile

    def index(parts, nb, tile, p, row, col):
        base = parts[p][1] // tile
        if len(parts) == 1:
            return parts[p][3] + (row, base + col)
        inside = (col // nb) == p
        return parts[p][3] + (jnp.where(inside, row, 0), base + jnp.clip(col - p * nb, 0, nb - 1))

    def spec(parts, p, shape, fn):
        return pl.BlockSpec((None,) * len(parts[p][3]) + shape, fn)

    in_specs, operands = [], []
    for p in range(len(a)):
        if ta:
            in_specs.append(spec(a, p, (tk, tm), lambda mi, nj, kk, p=p: index(a, a_nb, a_tile, p, kk, mi)))
        else:
            in_specs.append(spec(a, p, (tm, tk), lambda mi, nj, kk, p=p: index(a, a_nb, a_tile, p, mi, kk)))
        operands.append(a[p][0])
    for p in range(len(b)):
        if tb:
            in_specs.append(spec(b, p, (tn, tk), lambda mi, nj, kk, p=p: index(b, b_nb, b_tile, p, nj, kk)))
        else:
            in_specs.append(spec(b, p, (tk, tn), lambda mi, nj, kk, p=p: index(b, b_nb, b_tile, p, kk, nj)))
        operands.append(b[p][0])
    if add is not None:
        in_specs.append(pl.BlockSpec((tm, tn), lambda mi, nj, kk: (mi, nj)))
        operands.append(add)
    aliases = {}
    if dest is None:
        out_spec = pl.BlockSpec((tm, tn), lambda mi, nj, kk: (mi, nj))
        out_shape = _sds((M, N), out_dtype)
    else:
        d_shape, d_prev, d_lead, _ = dest
        nbs = N // shards // tn
        if shards == 1:
            out_spec = pl.BlockSpec((None,) * len(d_lead) + (tm, tn), lambda mi, nj, kk: tuple(d_lead) + (mi, nj))
        else:
            out_spec = pl.BlockSpec((None,) * (len(d_lead) + 1) + (tm, tn),
                                    lambda mi, nj, kk: tuple(d_lead) + (nj // nbs, mi, nj % nbs))
        out_shape = _sds(d_shape, out_dtype)
        if d_prev is not None:
            aliases = {len(operands): 0}
            in_specs.append(pl.BlockSpec(memory_space=pl.ANY))
            operands.append(d_prev)
    n_extra = 1 if aliases else 0
    dims = (((0 if ta else 1,), (1 if tb else 0,)), ((), ()))
    na, nb_ = len(a), len(b)

    def body(*refs):
        a_refs, b_refs = refs[:na], refs[na:na + nb_]
        pos = na + nb_
        add_ref = None
        if add is not None:
            add_ref = refs[pos]
            pos += 1
        pos += n_extra
        o_ref = refs[pos]
        acc_ref = refs[pos + 1] if nk > 1 else None
        mi, nj, kk = pl.program_id(0), pl.program_id(1), pl.program_id(2)
        a_col = mi if ta else kk
        b_col = kk if tb else nj

        def finish(r):
            if add_ref is not None:
                r = r + add_ref[...].astype(F32)
            o_ref[...] = r.astype(o_ref.dtype)

        def step(a_ref, b_ref):
            r = lax.dot_general(a_ref[...].astype(BF16), b_ref[...].astype(BF16), dims, preferred_element_type=F32)
            if nk == 1:
                finish(r)
            else:
                @pl.when(kk == 0)
                def _():
                    acc_ref[...] = r

                @pl.when(kk > 0)
                def _():
                    acc_ref[...] += r

        for pa in range(na):
            for pb in range(nb_):
                if na * nb_ == 1:
                    step(a_refs[pa], b_refs[pb])
                else:
                    cond = jnp.logical_and(a_col // a_nb == pa, b_col // b_nb == pb)
                    pl.when(cond)(functools.partial(step, a_refs[pa], b_refs[pb]))
        if nk > 1:
            @pl.when(kk == nk - 1)
            def _():
                finish(acc_ref[...])

    return pl.pallas_call(
        body, grid=(nm, nn, nk), in_specs=in_specs, out_specs=out_spec, out_shape=out_shape, input_output_aliases=aliases,
        scratch_shapes=[pltpu.VMEM((tm, tn), F32)] if nk > 1 else [],
        compiler_params=_params(dimension_semantics=("parallel", "parallel", "arbitrary")),
        name=name)(*operands)


def _row_spec(tb, part):
    _, off, w, _ = part
    return pl.BlockSpec((tb, w), lambda i, o=off // w: (i, o))


def _full_spec(arr):
    return pl.BlockSpec(arr.shape, lambda i: (0,) * arr.ndim)


def _rows(name, fn, rows, pars, outs, tb=256):
    n = rows[0][0].shape[0]
    tb = min(tb, n)
    nr, npar = len(rows), len(pars)

    def body(*refs):
        r = [ref[...].astype(F32) for ref in refs[:nr]]
        p = [ref[...].astype(F32) for ref in refs[nr:nr + npar]]
        res = fn(*r, *p)
        for o_ref, v in zip(refs[nr + npar:], res):
            o_ref[...] = v.astype(o_ref.dtype)

    out = pl.pallas_call(
        body, grid=(n // tb,),
        in_specs=[_row_spec(tb, p) for p in rows] + [_full_spec(p) for p in pars],
        out_specs=[pl.BlockSpec((tb, w), lambda i: (i, 0)) for w, _ in outs],
        out_shape=[_sds((n, w), dt) for w, dt in outs],
        compiler_params=_params(), name=name)(*[p[0] for p in rows], *pars)
    return out


def _rows_vjp(name, fn, rows, pars, cots, row_grads, par_grads, adds=None, tb=256):
    adds = adds or {}
    n = rows[0][0].shape[0]
    tb = min(tb, n)
    nr, npar, nc = len(rows), len(pars), len(cots)
    add_keys = sorted(adds)
    rg = [j for j, dt in enumerate(row_grads) if dt is not None]
    pg = [j for j, f in enumerate(par_grads) if f]

    def body(*refs):
        pos = 0
        r = [ref[...].astype(F32) for ref in refs[pos:pos + nr]]
        pos += nr
        p = [ref[...].astype(F32) for ref in refs[pos:pos + npar]]
        pos += npar
        ct = tuple(ref[...].astype(F32) for ref in refs[pos:pos + nc])
        pos += nc
        add_v = {k: ref[...].astype(F32) for k, ref in zip(add_keys, refs[pos:pos + len(add_keys)])}
        pos += len(add_keys)
        _, vjp = jax.vjp(lambda *args: tuple(fn(*args)), *r, *p)
        g = vjp(ct)
        for j in rg:
            v = g[j]
            if j in add_v:
                v = v + add_v[j]
            refs[pos][...] = v.astype(refs[pos].dtype)
            pos += 1
        first = pl.program_id(0) == 0
        for j in pg:
            ref = refs[pos]
            pos += 1

            @pl.when(first)
            def _(ref=ref):
                ref[...] = jnp.zeros_like(ref)

            ref[...] += g[nr + j]

    in_specs = ([_row_spec(tb, p) for p in rows] + [_full_spec(p) for p in pars] + [_row_spec(tb, p) for p in cots]
                + [_row_spec(tb, adds[k]) for k in add_keys])
    out_specs = ([pl.BlockSpec((tb, rows[j][2]), lambda i: (i, 0)) for j in rg] + [_full_spec(pars[j]) for j in pg])
    out_shape = ([_sds((n, rows[j][2]), row_grads[j]) for j in rg] + [_sds(pars[j].shape, F32) for j in pg])
    out = pl.pallas_call(
        body, grid=(n // tb,), in_specs=in_specs, out_specs=out_specs, out_shape=out_shape,
        compiler_params=_params(), name=name)(
            *[p[0] for p in rows], *pars, *[p[0] for p in cots], *[adds[k][0] for k in add_keys])
    return out[:len(rg)], out[len(rg):]


def _rms_fn(x, g):
    return (x * lax.rsqrt(jnp.mean(x * x, axis=-1, keepdims=True) + EPS) * g,)


def _lnsilu_fn(u, g, b):
    mu = jnp.mean(u, axis=-1, keepdims=True)
    xc = u - mu
    y = xc * lax.rsqrt(jnp.mean(xc * xc, axis=-1, keepdims=True) + EPS) * g + b
    return (y * jax.nn.sigmoid(y),)


def _merge_fn(ga, gb, gc, ya, yb, yc, ba, bb, bc):
    return (jax.nn.sigmoid(ga + ba) * ya + jax.nn.sigmoid(gb + bb) * yb + jax.nn.sigmoid(gc + bc) * yc,)


def _swiglu_fn(g, u):
    return (g * jax.nn.sigmoid(g) * u,)


def _loss_call(x, tgt, gain):
    n, d = x.shape
    tb = min(256, n)

    def body(x_ref, t_ref, g_ref, l_ref, dx_ref, dg_ref):
        (y,), vjp = jax.vjp(lambda a, b: tuple(_rms_fn(a, b)), x_ref[...], g_ref[...])
        e = y - t_ref[...]
        part = 0.5 * jnp.sum(jnp.mean(e * e, axis=-1, keepdims=True))
        dx, dg = vjp((e * (1.0 / d),))
        dx_ref[...] = dx

        @pl.when(pl.program_id(0) == 0)
        def _():
            l_ref[...] = jnp.zeros_like(l_ref)
            dg_ref[...] = jnp.zeros_like(dg_ref)

        l_ref[...] += jnp.full(l_ref.shape, part, F32)
        dg_ref[...] += dg

    return pl.pallas_call(
        body, grid=(n // tb,),
        in_specs=[pl.BlockSpec((tb, d), lambda i: (i, 0)), pl.BlockSpec((tb, d), lambda i: (i, 0)), _full_spec(gain)],
        out_specs=[pl.BlockSpec((1, LANES), lambda i: (0, 0)), pl.BlockSpec((tb, d), lambda i: (i, 0)), _full_spec(gain)],
        out_shape=[_sds((1, LANES), F32), _sds((n, d), F32), _sds(gain.shape, F32)],
        compiler_params=_params(), name="loss_head")(x, tgt, gain)


FOX_BLOCK = 256
CONV_ROWS = 64
CONV_HALO = 32


def _chunk(i):
    return pl.ds(pl.multiple_of(i * CONV_ROWS, CONV_ROWS), CONV_ROWS)


def _delayed(ext, s):
    if s % SUBLANES == 0:
        return ext[CONV_HALO - s:CONV_HALO - s + CONV_ROWS]
    return pltpu.roll(ext, s, 0)[CONV_HALO:]


def _advanced(ext, s):
    if s % SUBLANES == 0:
        return ext[s:s + CONV_ROWS]
    return pltpu.roll(ext, ext.shape[0] - s, 0)[:CONV_ROWS]


def _conv_pass(t, taps, w_ref, lanes, get_u, emit, get_dy=None, dw_acc=None):
    def body(i, carry):
        rows = _chunk(i)
        u = get_u(rows)
        ext = jnp.concatenate([carry, u], axis=0)
        dy = None if get_dy is None else get_dy(rows)
        acc = None
        for k in range(taps):
            sh = _delayed(ext, taps - 1 - k)
            term = w_ref[k:k + 1, lanes] * sh
            acc = term if acc is None else acc + term
            if dy is not None:
                dw_acc[k] += jnp.sum((dy * sh).reshape(CONV_ROWS // SUBLANES, SUBLANES, LANES), axis=0)
        emit(rows, u, acc)
        return u[CONV_ROWS - CONV_HALO:]

    lax.fori_loop(0, t // CONV_ROWS, body, jnp.zeros((CONV_HALO, LANES), F32))


def _conv_pass_t(t, taps, w_ref, lanes, get_dy, emit):
    n = t // CONV_ROWS

    def body(j, carry):
        rows = _chunk(n - 1 - j)
        dy = get_dy(rows)
        ext = jnp.concatenate([dy, carry], axis=0)
        acc = None
        for k in range(taps):
            term = w_ref[k:k + 1, lanes] * _advanced(ext, taps - 1 - k)
            acc = term if acc is None else acc + term
        emit(rows, acc)
        return dy[:CONV_HALO]

    lax.fori_loop(0, n, body, jnp.zeros((CONV_HALO, LANES), F32))


def _chan_block(d):
    return min(256, d)


def _lane_groups(cb):
    return [slice(g * LANES, (g + 1) * LANES) for g in range(cb // LANES)]


def _conv_a_fwd(za, w):
    t, d = za.shape[0], za.shape[1] // 3
    taps, cb = w.shape[0], _chan_block(d)
    nb = d // cb
    assert taps - 1 <= CONV_HALO and t % CONV_ROWS == 0

    def body(ab_ref, ac_ref, au_ref, w_ref, o_ref):
        for lanes in _lane_groups(cb):
            def emit(rows, u, s, lanes=lanes):
                o_ref[rows, lanes] = (ab_ref[rows, lanes].astype(F32) * s).astype(o_ref.dtype)

            _conv_pass(t, taps, w_ref, lanes, lambda rows, lanes=lanes: ac_ref[rows, lanes].astype(F32) * au_ref[rows, lanes].astype(F32), emit)

    return pl.pallas_call(
        body, grid=(nb,),
        in_specs=[pl.BlockSpec((t, cb), lambda j, o=o: (0, o * nb + j)) for o in range(3)]
        + [pl.BlockSpec((taps, cb), lambda j: (0, j))],
        out_specs=pl.BlockSpec((t, cb), lambda j: (0, j)),
        out_shape=_sds((t, d), BF16), compiler_params=_params(), name="conv_a_fwd")(za, za, za, w)


def _finish_taps(dw_ref, dw_acc, lanes, taps):
    for k in range(taps):
        dw_ref[k:k + 1, lanes] = jnp.sum(dw_acc[k], axis=0, keepdims=True)


def _conv_a_bwd(za, w, dpa):
    t, d = za.shape[0], za.shape[1] // 3
    taps, cb = w.shape[0], _chan_block(d)
    nb = d // cb

    def body(ab_ref, ac_ref, au_ref, w_ref, dy_ref, dab_ref, dac_ref, dau_ref, dw_ref, dw_acc):
        for lanes in _lane_groups(cb):
            dw_acc[...] = jnp.zeros_like(dw_acc)

            def ds(rows, lanes=lanes):
                return dy_ref[rows, lanes].astype(F32) * ab_ref[rows, lanes].astype(F32)

            def emit_s(rows, u, s, lanes=lanes):
                dab_ref[rows, lanes] = (dy_ref[rows, lanes].astype(F32) * s).astype(dab_ref.dtype)

            def emit_du(rows, du, lanes=lanes):
                dac_ref[rows, lanes] = (du * au_ref[rows, lanes].astype(F32)).astype(dac_ref.dtype)
                dau_ref[rows, lanes] = (du * ac_ref[rows, lanes].astype(F32)).astype(dau_ref.dtype)

            _conv_pass(t, taps, w_ref, lanes, lambda rows, lanes=lanes: ac_ref[rows, lanes].astype(F32) * au_ref[rows, lanes].astype(F32), emit_s,
                       get_dy=ds, dw_acc=dw_acc)
            _conv_pass_t(t, taps, w_ref, lanes, ds, emit_du)
            _finish_taps(dw_ref, dw_acc, lanes, taps)

    blk = pl.BlockSpec((t, cb), lambda j: (0, j))
    return pl.pallas_call(
        body, grid=(nb,),
        in_specs=[pl.BlockSpec((t, cb), lambda j, o=o: (0, o * nb + j)) for o in range(3)]
        + [pl.BlockSpec((taps, cb), lambda j: (0, j)), blk],
        out_specs=[blk, blk, blk, pl.BlockSpec((taps, cb), lambda j: (0, j))],
        out_shape=[_sds((t, d), BF16)] * 3 + [_sds((taps, d), F32)],
        scratch_shapes=[pltpu.VMEM((taps, SUBLANES, LANES), F32)],
        compiler_params=_params(), name="conv_a_bwd")(za, za, za, w, dpa)


def _conv_c_fwd(zcg, w, bias):
    t, d = zcg.shape[0], w.shape[1]
    taps, cb = w.shape[0], _chan_block(d)
    nb = d // cb
    assert taps - 1 <= CONV_HALO and t % CONV_ROWS == 0

    def body(cv_ref, cg_ref, w_ref, b_ref, o_ref):
        for lanes in _lane_groups(cb):
            def emit(rows, u, s, lanes=lanes):
                o_ref[rows, lanes] = s + b_ref[:, lanes]

            _conv_pass(t, taps, w_ref, lanes,
                       lambda rows, lanes=lanes: cv_ref[rows, lanes].astype(F32) * jax.nn.sigmoid(cg_ref[rows, lanes].astype(F32)), emit)

    return pl.pallas_call(
        body, grid=(nb,),
        in_specs=[pl.BlockSpec((t, cb), lambda j, o=o: (0, o * nb + j)) for o in range(2)]
        + [pl.BlockSpec((taps, cb), lambda j: (0, j)), pl.BlockSpec((1, cb), lambda j: (0, j))],
        out_specs=pl.BlockSpec((t, cb), lambda j: (0, j)),
        out_shape=_sds((t, d), F32), compiler_params=_params(), name="conv_c_fwd")(zcg, zcg, w, bias)


def _conv_c_bwd(zcg, w, du2):
    t, d = zcg.shape[0], w.shape[1]
    taps, cb = w.shape[0], _chan_block(d)
    nb = d // cb

    def body(cv_ref, cg_ref, w_ref, dy_ref, dcv_ref, dcg_ref, dw_ref, db_ref, dw_acc, db_acc):
        for lanes in _lane_groups(cb):
            dw_acc[...] = jnp.zeros_like(dw_acc)
            db_acc[...] = jnp.zeros_like(db_acc)

            def dy(rows, lanes=lanes):
                return dy_ref[rows, lanes].astype(F32)

            def emit_s(rows, u, s, lanes=lanes):
                db_acc[...] += jnp.sum(dy_ref[rows, lanes].astype(F32).reshape(CONV_ROWS // SUBLANES, SUBLANES, LANES), axis=0)

            def emit_du(rows, du, lanes=lanes):
                cv, sg = cv_ref[rows, lanes].astype(F32), jax.nn.sigmoid(cg_ref[rows, lanes].astype(F32))
                dcv_ref[rows, lanes] = (du * sg).astype(dcv_ref.dtype)
                dcg_ref[rows, lanes] = (du * cv * sg * (1.0 - sg)).astype(dcg_ref.dtype)

            _conv_pass(t, taps, w_ref, lanes,
                       lambda rows, lanes=lanes: cv_ref[rows, lanes].astype(F32) * jax.nn.sigmoid(cg_ref[rows, lanes].astype(F32)), emit_s,
                       get_dy=dy, dw_acc=dw_acc)
            _conv_pass_t(t, taps, w_ref, lanes, dy, emit_du)
            _finish_taps(dw_ref, dw_acc, lanes, taps)
            db_ref[:, lanes] = jnp.sum(db_acc[...], axis=0, keepdims=True)

    blk = pl.BlockSpec((t, cb), lambda j: (0, j))
    return pl.pallas_call(
        body, grid=(nb,),
        in_specs=[pl.BlockSpec((t, cb), lambda j, o=o: (0, o * nb + j)) for o in range(2)]
        + [pl.BlockSpec((taps, cb), lambda j: (0, j)), blk],
        out_specs=[blk, blk, pl.BlockSpec((taps, cb), lambda j: (0, j)), pl.BlockSpec((1, cb), lambda j: (0, j))],
        out_shape=[_sds((t, d), BF16)] * 2 + [_sds((taps, d), F32), _sds((1, d), F32)],
        scratch_shapes=[pltpu.VMEM((taps, SUBLANES, LANES), F32), pltpu.VMEM((SUBLANES, LANES), F32)],
        compiler_params=_params(), name="conv_c_bwd")(zcg, zcg, w, du2)


def _tri(tb, t, i, lower):
    r = lax.broadcasted_iota(jnp.int32, (tb, t), 0) + i * tb
    c = lax.broadcasted_iota(jnp.int32, (tb, t), 1)
    return ((r >= c) if lower else (c >= r)).astype(F32)


def _fox_cumsum(zf, bf):
    t = zf.shape[0]
    tb = min(256, t)

    def body(z_ref, b_ref, c_ref):
        logf = jax.nn.log_sigmoid(z_ref[...] + b_ref[...])
        c_ref[...] = jnp.dot(_tri(tb, t, pl.program_id(0), True), logf, precision=lax.Precision.HIGHEST,
                             preferred_element_type=F32)

    return pl.pallas_call(
        body, grid=(t // tb,), in_specs=[_full_spec(zf), _full_spec(bf)],
        out_specs=pl.BlockSpec((tb, LANES), lambda i: (i, 0)), out_shape=_sds((t, LANES), F32),
        compiler_params=_params(), name="fox_cumsum")(zf, bf)


def _fox_cumsum_bwd(zf, bf, dc_t):
    t = zf.shape[0]
    tb = min(256, t)

    def body(z_ref, b_ref, dc_ref, dz_ref, db_ref):
        i = pl.program_id(0)

        @pl.when(i == 0)
        def _():
            db_ref[...] = jnp.zeros_like(db_ref)

        dlogf = jnp.dot(_tri(tb, t, i, False), dc_ref[...], precision=lax.Precision.HIGHEST, preferred_element_type=F32)
        dz = dlogf * jax.nn.sigmoid(-(z_ref[...] + b_ref[...]))
        dz_ref[...] = dz.astype(dz_ref.dtype)
        db_ref[...] += jnp.sum(dz, axis=0, keepdims=True)

    return pl.pallas_call(
        body, grid=(t // tb,),
        in_specs=[pl.BlockSpec((tb, LANES), lambda i: (i, 0)), _full_spec(bf), _full_spec(dc_t)],
        out_specs=[pl.BlockSpec((tb, LANES), lambda i: (i, 0)), pl.BlockSpec((1, LANES), lambda i: (0, 0))],
        out_shape=[_sds((t, LANES), BF16), _sds((1, LANES), F32)],
        compiler_params=_params(), name="fox_cumsum_bwd")(zf, bf, dc_t)


def _fox_block(q, k, v, cr, *, q0, hd):
    tq, kext = q.shape[0], k.shape[0]
    nh = LANES // hd
    lane = lax.broadcasted_iota(jnp.int32, (1, LANES), 1)
    causal = (lax.broadcasted_iota(jnp.int32, (tq, kext), 0) + q0) >= lax.broadcasted_iota(jnp.int32, (tq, kext), 1)
    kb, vb = k.astype(BF16), v.astype(BF16)
    out = jnp.zeros((tq, LANES), F32)
    for h in range(nh):
        m = (lane // hd) == h
        qh = jnp.where(m, q * (hd ** -0.5), 0.0).astype(BF16)
        s = lax.dot_general(qh, kb, (((1,), (1,)), ((), ())), preferred_element_type=F32)
        c_k = cr[h] - lax.stop_gradient(cr[h][:, q0:q0 + 1])
        s = jnp.where(causal, s - c_k, -1e30)
        e = jnp.exp(s - lax.stop_gradient(jnp.max(s, axis=1, keepdims=True)))
        prob = e / jnp.sum(e, axis=1, keepdims=True)
        o = jnp.dot(prob.astype(BF16), vb, preferred_element_type=F32)
        out = out + jnp.where(m, o, 0.0)
    return out


def _fox_dims(zqkv, crow):
    t, d = zqkv.shape[0], zqkv.shape[1] // 3
    hd = d // crow.shape[0]
    assert math.log2(hd) % 2 == 0
    return t, d, hd, d // LANES, LANES // hd, min(FOX_BLOCK, t)


def _fox_specs(i, tq, nblk, nh):
    kext = (i + 1) * tq
    return [pl.BlockSpec((tq, LANES), lambda p: (i, p)), pl.BlockSpec((kext, LANES), lambda p: (0, nblk + p)),
            pl.BlockSpec((kext, LANES), lambda p: (0, 2 * nblk + p)), pl.BlockSpec((nh, 1, kext), lambda p: (p, 0, 0))]


def _fox_fwd(zqkv, crow):
    t, d, hd, nblk, nh, tq = _fox_dims(zqkv, crow)
    o = None
    for i in range(t // tq):
        def body(q_ref, k_ref, v_ref, cr_ref, *rest, i=i):
            o_ref = rest[-1]
            o_ref[...] = _fox_block(q_ref[...].astype(F32), k_ref[...], v_ref[...], cr_ref[...],
                                    q0=i * tq, hd=hd).astype(o_ref.dtype)

        prev = [] if o is None else [o]
        o = pl.pallas_call(
            body, grid=(nblk,),
            in_specs=_fox_specs(i, tq, nblk, nh) + [pl.BlockSpec(memory_space=pl.ANY)] * len(prev),
            out_specs=pl.BlockSpec((tq, LANES), lambda p, i=i: (i, p)), out_shape=_sds((t, d), BF16),
            input_output_aliases={4: 0} if prev else {},
            compiler_params=_params(), name=f"fox_fwd_q{i}")(zqkv, zqkv, zqkv, crow, *prev)
    return o


def _fox_bwd(zqkv, crow, do):
    t, d, hd, nblk, nh, tq = _fox_dims(zqkv, crow)
    nq = t // tq
    acc = None
    for i in reversed(range(nq)):
        kext = (i + 1) * tq
        first = acc is None

        def body(q_ref, k_ref, v_ref, cr_ref, do_ref, *rest, first=first, i=i):
            if first:
                dq_ref, dk_ref, dv_ref, dcr_ref = rest
            else:
                dk_in, dv_in, dcr_in = rest[1:4]
                dq_ref, dk_ref, dv_ref, dcr_ref = rest[4:]
            f = functools.partial(_fox_block, q0=i * tq, hd=hd)
            _, vjp = jax.vjp(f, q_ref[...].astype(F32), k_ref[...].astype(F32), v_ref[...].astype(F32), cr_ref[...])
            dq, dk, dv, dcr = vjp(do_ref[...].astype(F32))
            dq_ref[...] = dq.astype(dq_ref.dtype)
            if first:
                dk_ref[...], dv_ref[...], dcr_ref[...] = dk, dv, dcr
            else:
                dk_ref[...] = dk_in[...] + dk
                dv_ref[...] = dv_in[...] + dv
                dcr_ref[...] = dcr_in[...] + dcr

        qblk = pl.BlockSpec((tq, LANES), lambda p, i=i: (i, p))
        kblk = pl.BlockSpec((kext, LANES), lambda p: (0, p))
        rblk = pl.BlockSpec((nh, 1, kext), lambda p: (p, 0, 0))
        if first:
            prev, prev_specs, aliases = [], [], {}
        else:
            prev, prev_specs = list(acc), [pl.BlockSpec(memory_space=pl.ANY), kblk, kblk, rblk]
            aliases = {5: 0, 6: 1, 7: 2, 8: 3}
        acc = pl.pallas_call(
            body, grid=(nblk,),
            in_specs=_fox_specs(i, tq, nblk, nh) + [qblk] + prev_specs,
            out_specs=[qblk, kblk, kblk, rblk],
            out_shape=[_sds((t, d), BF16), _sds((t, d), F32), _sds((t, d), F32), _sds(crow.shape, F32)],
            input_output_aliases=aliases,
            compiler_params=_params(), name=f"fox_bwd_q{i}")(zqkv, zqkv, zqkv, crow, do, *prev)
    return acc


def _xattn_block(q, k, v):
    s = lax.dot_general(q.astype(BF16), k.astype(BF16), (((1,), (1,)), ((), ())), preferred_element_type=F32)
    s = s * (q.shape[1] ** -0.5)
    e = jnp.exp(s - lax.stop_gradient(jnp.max(s, axis=1, keepdims=True)))
    prob = e / jnp.sum(e, axis=1, keepdims=True)
    return jnp.dot(prob.astype(BF16), v.astype(BF16), preferred_element_type=F32)


def _xattn_fwd(qx, kv):
    t, dx = qx.shape
    nm, nh, tq = kv.shape[0], qx.shape[1] // LANES, min(512, qx.shape[0])

    def body(q_ref, k_ref, v_ref, o_ref):
        o_ref[...] = _xattn_block(q_ref[...], k_ref[...], v_ref[...]).astype(o_ref.dtype)

    return pl.pallas_call(
        body, grid=(nh, t // tq),
        in_specs=[pl.BlockSpec((tq, LANES), lambda h, i: (i, h)), pl.BlockSpec((nm, LANES), lambda h, i: (0, h)),
                  pl.BlockSpec((nm, LANES), lambda h, i: (0, nh + h))],
        out_specs=pl.BlockSpec((tq, LANES), lambda h, i: (i, h)), out_shape=_sds((t, dx), BF16),
        compiler_params=_params(), name="xattn_fwd")(qx, kv, kv)


def _xattn_bwd(qx, kv, do):
    t, dx = qx.shape
    nm, nh, tq = kv.shape[0], qx.shape[1] // LANES, min(512, qx.shape[0])

    def body(q_ref, k_ref, v_ref, do_ref, dq_ref, dk_ref, dv_ref):
        _, vjp = jax.vjp(_xattn_block, q_ref[...].astype(F32), k_ref[...].astype(F32), v_ref[...].astype(F32))
        dq, dk, dv = vjp(do_ref[...].astype(F32))
        dq_ref[...] = dq.astype(dq_ref.dtype)

        @pl.when(pl.program_id(1) == 0)
        def _():
            dk_ref[...] = jnp.zeros_like(dk_ref)
            dv_ref[...] = jnp.zeros_like(dv_ref)

        dk_ref[...] += dk
        dv_ref[...] += dv

    qblk = pl.BlockSpec((tq, LANES), lambda h, i: (i, h))
    kblk = pl.BlockSpec((nm, LANES), lambda h, i: (0, h))
    return pl.pallas_call(
        body, grid=(nh, t // tq),
        in_specs=[qblk, kblk, pl.BlockSpec((nm, LANES), lambda h, i: (0, nh + h)), qblk],
        out_specs=[qblk, kblk, kblk],
        out_shape=[_sds((t, dx), BF16), _sds((nm, dx), F32), _sds((nm, dx), F32)],
        compiler_params=_params(), name="xattn_bwd")(qx, kv, kv, do)


def _row_tile(rows, cols, budget=1 << 20, step=2 * SUBLANES):
    best = None
    for tb in range(step, rows + 1, step):
        if rows % tb == 0 and tb * cols * 4 <= budget:
            best = tb
    return rows if best is None else best


def _adamw(w, g, m, v, keep_rank=False):
    shape = w.shape
    cols = shape[-1]
    if keep_rank:
        w2, g2, m2, v2 = w, g, m, v
        rows, tail = shape[0], shape[1:]
        tb = _row_tile(rows, math.prod(tail), step=1)
    else:
        w2, g2, m2, v2 = (a.reshape(-1, cols) for a in (w, g, m, v))
        rows, tail = w2.shape[0], (cols,)
        tb = _row_tile(rows, cols)

    def body(w_ref, g_ref, m_ref, v_ref, d_ref, nm_ref, nv_ref):
        gg = g_ref[...]
        nm = ADAM_B1 * m_ref[...] + (1.0 - ADAM_B1) * gg
        nv = ADAM_B2 * v_ref[...] + (1.0 - ADAM_B2) * (gg * gg)
        m_hat = nm / (1.0 - ADAM_B1 ** ADAM_STEP)
        v_hat = nv / (1.0 - ADAM_B2 ** ADAM_STEP)
        d_ref[...] = -ADAM_LR * (m_hat / (jnp.sqrt(v_hat) + ADAM_EPS) + ADAM_WD * w_ref[...])
        nm_ref[...] = nm
        nv_ref[...] = nv

    blk = pl.BlockSpec((tb,) + tail, lambda i: (i,) + (0,) * len(tail))
    out = pl.pallas_call(
        body, grid=(rows // tb,), in_specs=[blk] * 4, out_specs=[blk] * 3, out_shape=[_sds(w2.shape, F32)] * 3,
        compiler_params=_params(), name="adamw")(w2, g2, m2, v2)
    return tuple(o.reshape(shape) for o in out)


def _sum_devices(gathered, rows):
    cols = gathered.shape[1]

    def body(g_ref, o_ref):
        acc = g_ref[0:rows, :]
        for dev in range(1, N_DEV):
            acc = acc + g_ref[dev * rows:(dev + 1) * rows, :]
        o_ref[...] = acc

    return pl.pallas_call(body, out_shape=_sds((rows, cols), F32), compiler_params=_params(), name="sum_devices")(gathered)


def _place():
    x, y, c = lax.axis_index("x"), lax.axis_index("y"), lax.axis_index("c")
    chips = [(1 - x, y), (x, 1 - y), (1 - x, 1 - y)]
    return x, y, c, chips


def _chip_id(chip):
    return 2 * chip[0] + chip[1]


def _half(c, rows):
    rh = rows // 2
    return pl.ds(pl.multiple_of(c * rh, 16), rh)


_HBM = pl.BlockSpec(memory_space=pltpu.HBM)


def _allgather8(x_shard):
    m_per, n = x_shard.shape

    def body(x_ref, out_ref, send_sems, recv_sems, local_sem):
        x, y, c, chips = _place()
        me, sibling = (x, y, c), (x, y, 1 - c)

        def rows(px, py, pc):
            return out_ref.at[pl.ds((4 * px + 2 * py + pc) * m_per, m_per), :]

        def copy(k, block, to, src=None):
            return pltpu.make_async_remote_copy(
                src_ref=rows(*block) if src is None else src, dst_ref=rows(*block),
                send_sem=send_sems.at[k], recv_sem=recv_sems.at[k], device_id=to, device_id_type=MESH)

        mine = pltpu.make_async_copy(x_ref, rows(*me), local_sem)
        mine.start()
        first = [copy(0, me, sibling, src=x_ref)]
        first += [copy(1 + j, me, (*chip, c), src=x_ref) for j, chip in enumerate(chips)]
        for cp in first:
            cp.start()
        passed = [copy(4 + j, (*chip, c), sibling) for j, chip in enumerate(chips)]
        for j, chip in enumerate(chips):
            copy(1 + j, (*chip, c), me).wait_recv()
            passed[j].start()
        copy(0, sibling, me).wait_recv()
        for j, chip in enumerate(chips):
            copy(4 + j, (*chip, 1 - c), me).wait_recv()
        for cp in first + passed:
            cp.wait_send()
        mine.wait()

    return pl.pallas_call(
        body, out_shape=_sds((N_DEV * m_per, n), x_shard.dtype),
        in_specs=[pl.BlockSpec(memory_space=pltpu.VMEM)], out_specs=pl.BlockSpec(memory_space=pltpu.VMEM),
        scratch_shapes=[pltpu.SemaphoreType.DMA((7,)), pltpu.SemaphoreType.DMA((7,)), pltpu.SemaphoreType.DMA],
        compiler_params=_params(), name="allgather8")(x_shard)


def _gather_weights(ws, layer, after):
    n = len(ws)

    def body(*refs):
        w, out = refs[:n], refs[n + 1:2 * n + 1]
        send_i, recv_i, send_d, recv_d = refs[2 * n + 1:]
        x, y, c, chips = _place()
        sibling = (x, y, 1 - c)
        me_chip = 2 * x + y

        def slab(i, chip, half):
            return out[i].at[chip, _half(half, w[i].shape[1]), :]

        def over_ici(i, j, src_chip, to, src=None):
            return pltpu.make_async_remote_copy(
                src_ref=slab(i, src_chip, c) if src is None else src, dst_ref=slab(i, src_chip, c),
                send_sem=send_i.at[3 * i + j], recv_sem=recv_i.at[3 * i + j], device_id=to, device_id_type=MESH)

        def over_d2d(i, j, half):
            return pltpu.make_async_remote_copy(
                src_ref=slab(i, _chip_id(chips[j]), half), dst_ref=slab(i, _chip_id(chips[j]), half),
                send_sem=send_d.at[3 * i + j], recv_sem=recv_d.at[3 * i + j], device_id=sibling, device_id_type=MESH)

        sent = []
        for i in range(n):
            for j, chip in enumerate(chips):
                cp = over_ici(i, j, me_chip, (*chip, c), src=w[i].at[layer, _half(c, w[i].shape[1]), :])
                cp.start()
                sent.append(cp)
        for i in range(n):
            for j, chip in enumerate(chips):
                over_ici(i, j, _chip_id(chip), (x, y, c)).wait_recv()
                cp = over_d2d(i, j, c)
                cp.start()
                sent.append(cp)
        for i in range(n):
            for j in range(3):
                over_d2d(i, j, 1 - c).wait_recv()
        for cp in sent:
            cp.wait_send()

    return pl.pallas_call(
        body, in_specs=[_HBM] * n + [pl.BlockSpec(memory_space=pl.ANY)], out_specs=[_HBM] * n,
        out_shape=[_sds((N_CHIPS,) + a.shape[1:], a.dtype) for a in ws],
        scratch_shapes=[pltpu.SemaphoreType.DMA((3 * n,)) for _ in range(4)],
        compiler_params=_params(), name="gather_weights")(*ws, after)


_SEM = pl.BlockSpec(memory_space=pltpu.SEMAPHORE)
_DATAFLOW = pltpu.SideEffectType.DATAFLOW_SIDE_EFFECTING


def _gather_copies(w, land, send, recv, layer):
    x, y, c, chips = _place()
    return [pltpu.make_async_remote_copy(
        src_ref=w[i].at[layer], dst_ref=land[i].at[2 * x + y], send_sem=send.at[3 * i + j], recv_sem=recv.at[3 * i + j],
        device_id=(*chip, c), device_id_type=MESH) for i in range(len(w)) for j, chip in enumerate(chips)]


def _gather_start(ws, layer, after):
    n = len(ws)

    def body(*refs):
        w, land = refs[:n], refs[n:2 * n]
        send, recv = refs[2 * n + 1], refs[2 * n + 2]
        for cp in _gather_copies(w, land, send, recv, layer):
            cp.start()
        refs[-1][...] = jnp.zeros_like(refs[-1])

    lands = [pltpu.with_memory_space_constraint(lax.empty((N_CHIPS,) + a.shape[1:], a.dtype), pltpu.HBM) for a in ws]
    out = pl.pallas_call(
        body, name=f"gather_start_{layer}",
        out_shape=(pltpu.SemaphoreType.DMA((3 * n,)), pltpu.SemaphoreType.DMA((3 * n,)),
                   *[pltpu.HBM(a.shape, a.dtype) for a in lands], _sds((SUBLANES, LANES), F32)),
        in_specs=[_HBM] * (2 * n) + [pl.BlockSpec(memory_space=pl.ANY)],
        out_specs=(_SEM, _SEM, *[_HBM] * n, pl.BlockSpec(memory_space=pltpu.VMEM)),
        input_output_aliases={n + i: 2 + i for i in range(n)},
        compiler_params=pltpu.CompilerParams(has_side_effects=_DATAFLOW))(*ws, *lands, after)
    return out[0], out[1], list(out[2:2 + n]), out[-1]


def _gather_wait(send, recv, ws, lands, after, layer):
    n = len(ws)

    def body(*refs):
        for cp in _gather_copies(refs[:n], refs[n:2 * n], refs[2 * n], refs[2 * n + 1], layer):
            cp.wait_send()
            cp.wait_recv()

    out = pl.pallas_call(
        body, name=f"gather_wait_{layer}",
        out_shape=tuple(pltpu.HBM(a.shape, a.dtype) for a in lands),
        in_specs=[_HBM] * (2 * n) + [_SEM, _SEM, pl.BlockSpec(memory_space=pl.ANY)], out_specs=tuple([_HBM] * n),
        input_output_aliases={n + i: i for i in range(n)},
        compiler_params=pltpu.CompilerParams(has_side_effects=_DATAFLOW))(*ws, *lands, send, recv, after)
    return list(out)


def _swap_halves(gs):
    n = len(gs)

    def body(*refs):
        g, out = refs[:n], refs[n:2 * n]
        send, recv = refs[2 * n:]
        x, y, c, _ = _place()
        cps = [pltpu.make_async_remote_copy(
            src_ref=g[i].at[:, :, _half(1 - c, g[i].shape[2]), :], dst_ref=out[i],
            send_sem=send.at[i], recv_sem=recv.at[i], device_id=(x, y, 1 - c), device_id_type=MESH) for i in range(n)]
        for cp in cps:
            cp.start()
        for cp in cps:
            cp.wait()

    return pl.pallas_call(
        body, in_specs=[_HBM] * n, out_specs=[_HBM] * n,
        out_shape=[_sds(a.shape[:2] + (a.shape[2] // 2, a.shape[3]), a.dtype) for a in gs],
        scratch_shapes=[pltpu.SemaphoreType.DMA((n,)), pltpu.SemaphoreType.DMA((n,))],
        compiler_params=_params(), name="swap_halves")(*gs)


def _exchange_chips(ss):
    n = len(ss)

    def body(*refs):
        s, out = refs[:n], refs[n:2 * n]
        send, recv = refs[2 * n:]
        x, y, c, chips = _place()
        cps = []
        for i in range(n):
            for j, chip in enumerate(chips):
                cps.append(pltpu.make_async_remote_copy(
                    src_ref=s[i].at[:, _chip_id(chip)], dst_ref=out[i].at[j],
                    send_sem=send.at[3 * i + j], recv_sem=recv.at[3 * i + j], device_id=(*chip, c), device_id_type=MESH))
        for cp in cps:
            cp.start()
        for cp in cps:
            cp.wait()

    return pl.pallas_call(
        body, in_specs=[_HBM] * n, out_specs=[_HBM] * n,
        out_shape=[_sds((3, a.shape[0]) + a.shape[2:], a.dtype) for a in ss],
        scratch_shapes=[pltpu.SemaphoreType.DMA((3 * n,)), pltpu.SemaphoreType.DMA((3 * n,))],
        compiler_params=_params(), name="exchange_chips")(*ss)


def _exchange_copies(s, land, send, recv):
    x, y, c, chips = _place()
    return [pltpu.make_async_remote_copy(
        src_ref=s[i].at[:, _chip_id(chip)], dst_ref=land[i].at[j], send_sem=send.at[3 * i + j], recv_sem=recv.at[3 * i + j],
        device_id=(*chip, c), device_id_type=MESH) for i in range(len(s)) for j, chip in enumerate(chips)]


def _exchange_start(ss, layer):
    n = len(ss)

    def body(*refs):
        s, land = refs[:n], refs[n:2 * n]
        send, recv = refs[2 * n], refs[2 * n + 1]
        token = refs[-1]
        for cp in _exchange_copies(s, land, send, recv):
            cp.start()
        token[...] = jnp.zeros_like(token)

    lands = [lax.empty((3, a.shape[0]) + a.shape[2:], a.dtype) for a in ss]
    operands = [pltpu.with_memory_space_constraint(a, pltpu.HBM) for a in list(ss) + lands]
    out = pl.pallas_call(
        body, name=f"exchange_start_{layer}",
        out_shape=(pltpu.SemaphoreType.DMA((3 * n,)), pltpu.SemaphoreType.DMA((3 * n,)),
                   *[pltpu.HBM(a.shape, a.dtype) for a in operands], _sds((SUBLANES, LANES), F32)),
        in_specs=[_HBM] * (2 * n), out_specs=(_SEM, _SEM, *[_HBM] * (2 * n), pl.BlockSpec(memory_space=pltpu.VMEM)),
        input_output_aliases={i: 2 + i for i in range(2 * n)},
        compiler_params=pltpu.CompilerParams(has_side_effects=_DATAFLOW))(*operands)
    return out[0], out[1], list(out[2:2 + n]), list(out[2 + n:2 + 2 * n]), out[-1]


def _exchange_wait(send, recv, ss, lands, after, layer):
    n = len(ss)

    def body(*refs):
        s, land = refs[:n], refs[n:2 * n]
        for cp in _exchange_copies(s, land, refs[2 * n], refs[2 * n + 1]):
            cp.wait_send()
            cp.wait_recv()

    out = pl.pallas_call(
        body, name=f"exchange_wait_{layer}",
        out_shape=tuple(pltpu.HBM(a.shape, a.dtype) for a in list(ss) + list(lands)),
        in_specs=[_HBM] * (2 * n) + [_SEM, _SEM, pl.BlockSpec(memory_space=pl.ANY)], out_specs=tuple([_HBM] * (2 * n)),
        input_output_aliases={i: i for i in range(2 * n)},
        compiler_params=pltpu.CompilerParams(has_side_effects=_DATAFLOW))(*ss, *lands, send, recv, after)
    return list(out[:n]), list(out[n:])


def _join_halves(fs):
    n = len(fs)

    def body(*refs):
        f, out = refs[:n], refs[n:2 * n]
        send, recv = refs[2 * n:]
        x, y, c, _ = _place()

        def half(ref, which):
            return ref.at[:, _half(which, ref.shape[1]), :]

        for i in range(n):
            pltpu.make_async_remote_copy(
                src_ref=half(f[i], c), dst_ref=half(out[i], c), send_sem=send.at[i], recv_sem=recv.at[i],
                device_id=(x, y, 1 - c), device_id_type=MESH).start()
        for i in range(n):
            arrive = pltpu.make_async_remote_copy(
                src_ref=half(f[i], c), dst_ref=half(out[i], 1 - c), send_sem=send.at[i], recv_sem=recv.at[i],
                device_id=(x, y, 1 - c), device_id_type=MESH)
            arrive.wait_recv()
            arrive.wait_send()

    return pl.pallas_call(
        body, in_specs=[_HBM] * n, out_specs=[_HBM] * n, out_shape=[_sds(a.shape, a.dtype) for a in fs],
        input_output_aliases={i: i for i in range(n)},
        scratch_shapes=[pltpu.SemaphoreType.DMA((n,)), pltpu.SemaphoreType.DMA((n,))],
        compiler_params=_params(), name="join_halves")(*fs)


def _add_sibling(g, got, c_idx):
    nl, ns, r, cols = g.shape
    rh = r // 2
    tb = _row_tile(rh, cols, budget=ADD_BLOCK_BYTES)
    nb = rh // tb
    g3, got3 = g.reshape(nl * ns, r, cols), got.reshape(nl * ns, rh, cols)

    def body(c_ref, a_ref, b_ref, o_ref):
        o_ref[...] = (a_ref[...].astype(F32) + b_ref[...].astype(F32)).astype(o_ref.dtype)

    out = pl.pallas_call(
        body,
        grid_spec=pltpu.PrefetchScalarGridSpec(
            num_scalar_prefetch=1, grid=(nl * ns, nb),
            in_specs=[pl.BlockSpec((None, tb, cols), lambda s, i, c_ref: (s, c_ref[0] * nb + i, 0)),
                      pl.BlockSpec((None, tb, cols), lambda s, i, c_ref: (s, i, 0))],
            out_specs=pl.BlockSpec((None, tb, cols), lambda s, i, c_ref: (s, i, 0))),
        out_shape=_sds((nl * ns, rh, cols), BF16), compiler_params=_params(), name="add_sibling")(c_idx, g3, got3)
    return out.reshape(nl, ns, rh, cols)


def _add_chips(s, got, chip_idx, c_idx, layer, depth, prev):
    _, ns, rh, cols = s.shape
    tb = _row_tile(rh, cols, budget=ADD_BLOCK_BYTES)
    nb = rh // tb

    def body(k_ref, c_ref, a_ref, b0_ref, b1_ref, b2_ref, *rest):
        o_ref = rest[-1]
        o_ref[...] = ((a_ref[...].astype(F32) + b0_ref[...].astype(F32)) + b1_ref[...].astype(F32)) + b2_ref[...].astype(F32)

    extra = [] if prev is None else [prev]
    return pl.pallas_call(
        body,
        grid_spec=pltpu.PrefetchScalarGridSpec(
            num_scalar_prefetch=2, grid=(nb,),
            in_specs=[pl.BlockSpec((None, None, tb, cols), lambda i, k_ref, c_ref: (0, k_ref[0], i, 0))]
            + [pl.BlockSpec((None, None, tb, cols), lambda i, k_ref, c_ref, j=j: (j, 0, i, 0)) for j in range(3)]
            + [pl.BlockSpec(memory_space=pl.ANY)] * len(extra),
            out_specs=pl.BlockSpec((None, tb, cols), lambda i, k_ref, c_ref: (layer, c_ref[0] * nb + i, 0))),
        out_shape=_sds((depth, 2 * rh, cols), F32), input_output_aliases={6: 0} if extra else {},
        compiler_params=_params(), name="add_chips")(chip_idx, c_idx, s, got, got, got, *extra)


def _in_pieces(d, heads, ws):
    groups, start = [], 0
    for k in range(12):
        if k == 6:
            groups.append(("f", 0, start, heads))
            start += heads
        else:
            groups.append(("main", (k - (k > 6)) * d, start, d))
            start += d
    pieces = []
    for chip in range(N_CHIPS):
        lo, hi = chip * ws, (chip + 1) * ws
        for dest, dcol, gstart, w in groups:
            a, b = max(lo, gstart), min(hi, gstart + w)
            if a < b:
                pieces.append((chip, a - lo, dest, dcol + a - gstart, b - a))
    return pieces


def _w_in_aligned(gathered, d, heads):
    rows, ws = gathered.shape[1], gathered.shape[2]
    tb = min(256, rows)
    pieces = _in_pieces(d, heads, ws)

    def body(g_ref, m_ref, f_ref):
        f_ref[...] = jnp.zeros_like(f_ref)
        for chip, sc, dest, dc, w in pieces:
            (m_ref if dest == "main" else f_ref)[:, dc:dc + w] = g_ref[chip, :, sc:sc + w]

    return pl.pallas_call(
        body, grid=(rows // tb,), in_specs=[pl.BlockSpec((N_CHIPS, tb, ws), lambda i: (0, i, 0))],
        out_specs=[pl.BlockSpec((tb, 11 * d), lambda i: (i, 0)), pl.BlockSpec((tb, LANES), lambda i: (i, 0))],
        out_shape=[_sds((rows, 11 * d), gathered.dtype), _sds((rows, LANES), gathered.dtype)],
        compiler_params=_params(), name="w_in_aligned")(gathered)


def _w_in_shards(gms, gf, prev, l, shape, heads):
    lo, hi = gms
    rows, d = lo.shape[0], lo.shape[1] // 6
    ws = shape[3]
    tb = min(256, rows)
    pieces = _in_pieces(d, heads, ws)

    def body(lo_ref, hi_ref, f_ref, *rest):
        o_ref = rest[-1]
        for chip, sc, dest, dc, w in pieces:
            if dest == "f":
                src = f_ref[:, dc:dc + w]
            elif dc < 6 * d:
                src = lo_ref[:, dc:dc + w]
            else:
                src = hi_ref[:, dc - 6 * d:dc - 6 * d + w]
            o_ref[chip, :, sc:sc + w] = src

    extra = [] if prev is None else [prev]
    return pl.pallas_call(
        body, grid=(rows // tb,),
        in_specs=[pl.BlockSpec((tb, 6 * d), lambda i: (i, 0)), pl.BlockSpec((tb, 5 * d), lambda i: (i, 0)),
                  pl.BlockSpec((tb, LANES), lambda i: (i, 0))] + [pl.BlockSpec(memory_space=pl.ANY)] * len(extra),
        out_specs=pl.BlockSpec((None, N_CHIPS, tb, ws), lambda i: (l, 0, i, 0)), out_shape=_sds(shape, lo.dtype),
        input_output_aliases={3: 0} if extra else {},
        compiler_params=_params(), name="w_in_shards")(lo, hi, gf, *extra)


def _pad_lanes(a, width=LANES):
    return jnp.pad(a, ((0, 0), (0, width - a.shape[1])))


def _cols(operand):
    return sum(p[2] for p in operand) if isinstance(operand, list) else operand.shape[1]


def _local_step(x, mem, tgt, sp, wt, dests=None, on_layer=None):
    t, d = x.shape
    depth = sp["mix_norm"].shape[0]
    heads = sp["b_forget"].shape[1]
    row = lambda a, l: a[l:l + 1]
    weights = wt if callable(wt) else (lambda l, x_in: {k: v[l] for k, v in wt.items()})
    layer_weights = []

    memn, = _rows("rms_mem", _rms_fn, [_part(mem)], [sp["mem_norm"][None]], [(d, BF16)])
    saved = []
    for l in range(depth):
        wl = weights(l, x)
        layer_weights.append(wl)
        f = _cols(wl["gu"]) // 2
        sv = {"x0": x}
        h1, = _rows("rms_mix", _rms_fn, [_part(x)], [row(sp["mix_norm"], l)], [(d, BF16)])
        wm = wl["main"]
        za = _mm("mm_in_a", h1, [_part(wm, 0, 3 * d)], out_dtype=BF16)
        zqkv = _mm("mm_in_qkv", h1, [_part(wm, 3 * d, 3 * d)], out_dtype=BF16)
        zcg = _mm("mm_in_cg", h1, [_part(wm, 6 * d, 5 * d)], out_dtype=BF16)
        zf = _mm("mm_in_f", h1, wl["f"])
        bf = _pad_lanes(row(sp["b_forget"], l))
        c = _fox_cumsum(zf, bf)
        crow = c[:, :heads].T.reshape(heads, 1, t)
        o = _fox_fwd(zqkv, crow)
        pa = _conv_a_fwd(za, sp["conv_a"][l])
        u2 = _conv_c_fwd(zcg, sp["conv_c"][l], row(sp["conv_c_bias"], l))
        u3, = _rows("ln_silu", _lnsilu_fn, [_part(u2)], [row(sp["ln_c_gain"], l), row(sp["ln_c_bias"], l)], [(d, BF16)])
        ya = _mm("mm_out_a", pa, wl["out_a"])
        yb = _mm("mm_out_b", o, wl["out_b"])
        yc = _mm("mm_out_c", u3, wl["out_c"])
        bg = [sp["b_gate"][l:l + 1, k * d:(k + 1) * d] for k in range(3)]
        gate_rows = [_part(zcg, (2 + k) * d, d) for k in range(3)] + [_part(ya), _part(yb), _part(yc)]
        mg, = _rows("merge", _merge_fn, gate_rows, bg, [(d, BF16)])
        x1 = _mm("mm_o", mg, wl["o"], add=x)
        h2, = _rows("rms_x", _rms_fn, [_part(x1)], [row(sp["xattn_norm"], l)], [(d, BF16)])
        qx = _mm("mm_xq", h2, wl["xq"], out_dtype=BF16)
        kv = _mm("mm_xkv", memn, wl["xkv"], out_dtype=BF16)
        ox = _xattn_fwd(qx, kv)
        x2 = _mm("mm_xo", ox, wl["xo"], add=x1)
        h3, = _rows("rms_ffn", _rms_fn, [_part(x2)], [row(sp["ffn_norm"], l)], [(d, BF16)])
        gu = _mm("mm_gu", h3, wl["gu"], out_dtype=BF16)
        act, = _rows("swiglu", _swiglu_fn, [_part(gu, 0, f), _part(gu, f, f)], [], [(f, BF16)])
        x3 = _mm("mm_down", act, wl["down"], add=x2)
        sv.update(h1=h1, za=za, zqkv=zqkv, zcg=zcg, zf=zf, bf=bf, crow=crow, o=o, pa=pa, u2=u2, u3=u3, ya=ya, yb=yb,
                  yc=yc, bg=bg, gate_rows=gate_rows, mg=mg, x1=x1, h2=h2, qx=qx, kv=kv, ox=ox, x2=x2, h3=h3, gu=gu, act=act)
        saved.append(sv)
        x = x3

    loss_row, dx, d_final = _loss_call(x, tgt, sp["final_norm"][None])

    gs = {k: [None] * depth for k in ("mix_norm", "b_gate", "b_forget", "conv_a", "conv_c", "conv_c_bias", "ln_c_gain",
                                      "ln_c_bias", "xattn_norm", "ffn_norm")}
    gw = {k: [None] * depth for k in ("main", "f", "out_a", "out_b", "out_c", "o", "xq", "xkv", "xo", "gu", "down")}
    def weight_grad(key, l, name, a_op, b_op):
        if dests is None or key not in dests:
            gw[key][l] = _mm(name, a_op, b_op, ta=True, out_dtype=BF16)
        else:
            shape, shards = dests[key]
            gw[key][l] = _mm(name, a_op, b_op, ta=True, out_dtype=BF16, dest=(shape, None, (0,), shards))

    dmemn = None
    for l in reversed(range(depth)):
        sv, wl = saved[l], layer_weights[l]
        dact = _mm("mm_down_dx", dx, wl["down"], tb=True)
        weight_grad("down", l, "mm_down_dw", sv["act"], dx)
        (dg, du), _ = _rows_vjp("swiglu_bwd", _swiglu_fn, [_part(sv["gu"], 0, f), _part(sv["gu"], f, f)], [], [_part(dact)],
                                [BF16, BF16], [])
        dgu = [_part(dg), _part(du)]
        dh3 = _mm("mm_gu_dx", dgu, wl["gu"], tb=True)
        weight_grad("gu", l, "mm_gu_dw", sv["h3"], dgu)
        (dx2,), (gs["ffn_norm"][l],) = _rows_vjp("rms_ffn_bwd", _rms_fn, [_part(sv["x2"])], [row(sp["ffn_norm"], l)],
                                                 [_part(dh3)], [F32], [True], adds={0: _part(dx)})
        dox = _mm("mm_xo_dx", dx2, wl["xo"], tb=True)
        weight_grad("xo", l, "mm_xo_dw", sv["ox"], dx2)
        dqx, dkx, dvx = _xattn_bwd(sv["qx"], sv["kv"], dox)
        dh2 = _mm("mm_xq_dx", dqx, wl["xq"], tb=True)
        weight_grad("xq", l, "mm_xq_dw", sv["h2"], dqx)
        dkv = [_part(dkx), _part(dvx)]
        dmemn = _mm("mm_xkv_dx", dkv, wl["xkv"], tb=True, add=dmemn)
        weight_grad("xkv", l, "mm_xkv_dw", memn, dkv)
        (dx1,), (gs["xattn_norm"][l],) = _rows_vjp("rms_x_bwd", _rms_fn, [_part(sv["x1"])], [row(sp["xattn_norm"], l)],
                                                   [_part(dh2)], [F32], [True], adds={0: _part(dx2)})
        dmg = _mm("mm_o_dx", dx1, wl["o"], tb=True)
        weight_grad("o", l, "mm_o_dw", sv["mg"], dx1)
        (dga, dgb, dgc, dya, dyb, dyc), dbg = _rows_vjp("merge_bwd", _merge_fn, sv["gate_rows"], sv["bg"], [_part(dmg)],
                                                        [BF16] * 6, [True] * 3)
        gs["b_gate"][l] = jnp.concatenate(dbg, axis=1)
        dpa = _mm("mm_out_a_dx", dya, wl["out_a"], tb=True)
        weight_grad("out_a", l, "mm_out_a_dw", sv["pa"], dya)
        do = _mm("mm_out_b_dx", dyb, wl["out_b"], tb=True)
        weight_grad("out_b", l, "mm_out_b_dw", sv["o"], dyb)
        du3 = _mm("mm_out_c_dx", dyc, wl["out_c"], tb=True)
        weight_grad("out_c", l, "mm_out_c_dw", sv["u3"], dyc)
        if on_layer is not None:
            du3 = du3 + on_layer(l, "early", {k: v[l] for k, v in gw.items() if k not in ("main", "f")})[0, 0]
        (du2,), (gs["ln_c_gain"][l], gs["ln_c_bias"][l]) = _rows_vjp(
            "ln_silu_bwd", _lnsilu_fn, [_part(sv["u2"])], [row(sp["ln_c_gain"], l), row(sp["ln_c_bias"], l)], [_part(du3)],
            [F32], [True, True])
        dcv, dcg, gs["conv_c"][l], gs["conv_c_bias"][l] = _conv_c_bwd(sv["zcg"], sp["conv_c"][l], du2)
        dab, dac, dau, gs["conv_a"][l] = _conv_a_bwd(sv["za"], sp["conv_a"][l], dpa)
        dq, dk, dv, dcr = _fox_bwd(sv["zqkv"], sv["crow"], do)
        dzf, dbf = _fox_cumsum_bwd(sv["zf"], sv["bf"], _pad_lanes(dcr.reshape(heads, t).T))
        gs["b_forget"][l] = dbf[:, :heads]
        dz = [_part(a) for a in (dab, dac, dau, dq, dk, dv, dcv, dcg, dga, dgb, dgc)]
        dh1f = _mm("mm_in_f_dx", dzf, wl["f"], tb=True)
        dh1 = _mm("mm_in_dx", dz, wl["main"], tb=True, add=dh1f)
        gw["main"][l] = [_mm("mm_in_dw_lo", sv["h1"], dz[:6], ta=True, out_dtype=BF16),
                         _mm("mm_in_dw_hi", sv["h1"], dz[6:], ta=True, out_dtype=BF16)]
        gw["f"][l] = _mm("mm_in_f_dw", sv["h1"], dzf, ta=True, out_dtype=BF16)
        (dx,), (gs["mix_norm"][l],) = _rows_vjp("rms_mix_bwd", _rms_fn, [_part(sv["x0"])], [row(sp["mix_norm"], l)],
                                                [_part(dh1)], [F32], [True], adds={0: _part(dx1)})
        if on_layer is not None:
            token = on_layer(l, "late", {"main": gw["main"][l], "f": gw["f"][l]})
            if l > 0:
                dx = dx + token[0, 0]

    _, (d_mem_norm,) = _rows_vjp("rms_mem_bwd", _rms_fn, [_part(mem)], [sp["mem_norm"][None]], [_part(dmemn)], [None], [True])
    small = {k: jnp.stack([a.reshape(sp[k].shape[1:]) for a in v]) for k, v in gs.items()}
    small["mem_norm"] = d_mem_norm[0]
    small["final_norm"] = d_final[0]
    return loss_row, dx, small, gw


_BIG = ("w_in", "w_out_a", "w_out_b", "w_out_c", "w_o", "w_xq", "w_xkv", "w_xo", "w_gate_up", "w_down")
_COL_SHARDED = ("w_in", "w_xo", "w_gate_up")
_KEY = {"w_out_a": "out_a", "w_out_b": "out_b", "w_out_c": "out_c", "w_o": "o", "w_xq": "xq", "w_xkv": "xkv", "w_xo": "xo",
        "w_gate_up": "gu", "w_down": "down"}
_SMALL = ("mix_norm", "b_gate", "b_forget", "conv_c_bias", "ln_c_gain", "ln_c_bias", "xattn_norm", "ffn_norm", "mem_norm",
          "final_norm")
_SMALL_SHARDED = ("conv_a", "conv_c")


def _pack_rows(parts):
    padded = []
    for a in parts:
        pad = -a.shape[0] % SUBLANES
        padded.append(jnp.pad(a, ((0, pad), (0, 0))) if pad else a)
    return jnp.concatenate(padded, axis=0)


def _unpack_rows(packed, shapes):
    out, pos = [], 0
    for r in shapes:
        out.append(packed[pos:pos + r])
        pos += r + (-r % SUBLANES)
    return out


def kernel(x, mem, mix_norm, w_in, b_gate, b_forget, conv_a, w_out_a, w_out_b, conv_c, conv_c_bias, ln_c_gain, ln_c_bias, w_out_c, w_o, xattn_norm, mem_norm, w_xq, w_xkv, w_xo, ffn_norm, w_gate_up, w_down, final_norm, loss_target, m_mix_norm, m_w_in, m_b_gate, m_b_forget, m_conv_a, m_w_out_a, m_w_out_b, m_conv_c, m_conv_c_bias, m_ln_c_gain, m_ln_c_bias, m_w_out_c, m_w_o, m_xattn_norm, m_mem_norm, m_w_xq, m_w_xkv, m_w_xo, m_ffn_norm, m_w_gate_up, m_w_down, m_final_norm, v_mix_norm, v_w_in, v_b_gate, v_b_forget, v_conv_a, v_w_out_a, v_w_out_b, v_conv_c, v_conv_c_bias, v_ln_c_gain, v_ln_c_bias, v_w_out_c, v_w_o, v_xattn_norm, v_mem_norm, v_w_xq, v_w_xkv, v_w_xo, v_ffn_norm, v_w_gate_up, v_w_down, v_final_norm):
    args = dict(locals())
    names = ["mix_norm", "w_in", "b_gate", "b_forget", "conv_a", "w_out_a", "w_out_b", "conv_c", "conv_c_bias", "ln_c_gain",
             "ln_c_bias", "w_out_c", "w_o", "xattn_norm", "mem_norm", "w_xq", "w_xkv", "w_xo", "ffn_norm", "w_gate_up", "w_down",
             "final_norm"]
    depth, d = mix_norm.shape
    heads = b_forget.shape[1]
    cx, cy, cc = lax.axis_index("x"), lax.axis_index("y"), lax.axis_index("c")
    chip = 2 * cx + cy
    c_idx = jnp.reshape(cc, (1,)).astype(jnp.int32)
    chip_idx = jnp.reshape(chip, (1,)).astype(jnp.int32)

    conv_rows = [conv_a.shape[1] * depth, conv_c.shape[1] * depth]
    conv_pack = _pack_rows([conv_a.reshape(conv_rows[0], -1), conv_c.reshape(conv_rows[1], -1)])
    conv_all = _allgather8(conv_pack).reshape(N_CHIPS, 2, conv_pack.shape[0], conv_pack.shape[1])[:, 0]
    conv_all = conv_all.transpose(1, 0, 2).reshape(conv_pack.shape[0], d)
    conv_a_full, conv_c_full = _unpack_rows(conv_all, conv_rows)
    own = [args[n].astype(BF16) for n in _BIG]
    first = _gather_weights(own, 0, conv_all)
    started, tokens = {}, []
    for l in range(1, depth):
        send, recv, lands, token = _gather_start(own, l, first[0])
        started[l] = (send, recv, own, lands)
        tokens.append(token)

    def layer_weights(l, x_in):
        if l == 0:
            lands = first
        else:
            lands = _gather_wait(*started[l], x_in, l)
        full = [lax.dynamic_update_slice(g, w[l][None], (chip, 0, 0)) for g, w in zip(lands, own)]
        wl = {}
        wl["main"], wl["f"] = _w_in_aligned(full[0], d, heads)
        for n, g in zip(_BIG[1:], full[1:]):
            if n in _COL_SHARDED:
                wl[_KEY[n]] = [_part(g, lead=(j,)) for j in range(N_CHIPS)]
            else:
                wl[_KEY[n]] = g.reshape(N_CHIPS * g.shape[1], g.shape[2])
        return wl

    shard_shape = {n: (1, N_CHIPS) + args[n].shape[1:] for n in _BIG}
    dests = {_KEY[n]: (shard_shape[n], N_CHIPS) if n in _COL_SHARDED
             else ((1, N_CHIPS * args[n].shape[1], args[n].shape[2]), 1) for n in _BIG[1:]}
    sp = {n: args[n] for n in _SMALL}
    sp["conv_a"] = conv_a_full.reshape(depth, conv_a.shape[1], d)
    sp["conv_c"] = conv_c_full.reshape(depth, conv_c.shape[1], d)

    in_flight = {}

    def on_layer(l, which, gl):
        if which == "late":
            group = ["w_in"]
            local = [_w_in_shards(gl["main"], gl["f"], None, 0, shard_shape["w_in"], heads)]
        else:
            group = list(_BIG[1:])
            local = [gl[_KEY[n]].reshape(shard_shape[n]) for n in group]
        chip_sums = [_add_sibling(g, r, c_idx) for g, r in zip(local, _swap_halves(local))]
        send, recv, chip_sums, lands, token = _exchange_start(chip_sums, f"{l}_{which}")
        in_flight[(l, which)] = (group, send, recv, chip_sums, lands)
        return token

    x_in = x[0]
    for token in tokens:
        x_in = x_in + token[0, 0]
    loss_row, grad_x, gsmall, gw = _local_step(x_in, mem[0], loss_target[0], sp, layer_weights, dests, on_layer)
    halves = {n: None for n in _BIG}
    for l in reversed(range(depth)):
        for which in ("early", "late"):
            group, send, recv, chip_sums, lands = in_flight[(l, which)]
            chip_sums, lands = _exchange_wait(send, recv, chip_sums, lands, grad_x, f"{l}_{which}")
            for n, s_, r in zip(group, chip_sums, lands):
                halves[n] = _add_chips(s_, r, chip_idx, c_idx, l, depth, halves[n])
    halves = [halves[n] for n in _BIG]
    grads = dict(zip(_BIG, _join_halves(halves)))

    def as_rows(n, a):
        return _pad_lanes(a, d) if n == "b_forget" else a.reshape(-1, d)

    pieces = [as_rows(n, gsmall[n]) for n in _SMALL]
    for n in _SMALL_SHARDED:
        pieces.append(gsmall[n].reshape(-1, d))
    pieces.append(_pad_lanes(loss_row, d))
    pack = _pack_rows(pieces)
    total = _sum_devices(_allgather8(pack), pack.shape[0])
    shapes = [p.shape[0] for p in pieces]
    summed = _unpack_rows(total, shapes)
    for n, g in zip(_SMALL, summed):
        grads[n] = g[:, :heads] if n == "b_forget" else g.reshape(args[n].shape)
    for n, g in zip(_SMALL_SHARDED, summed[len(_SMALL):]):
        g = g.reshape(args[n].shape[:2] + (d,))
        grads[n] = lax.dynamic_slice_in_dim(g, chip * args[n].shape[2], args[n].shape[2], axis=2)
    loss = summed[-1][0, 0]

    delta, new_m, new_v = {}, {}, {}
    packed = [_pack_rows([as_rows(n, src[pre + n]) for n in _SMALL])
              for src, pre in ((args, ""), (grads, ""), (args, "m_"), (args, "v_"))]
    srows = [as_rows(n, args[n]).shape[0] for n in _SMALL]
    updated = [_unpack_rows(a, srows) for a in _adamw(*packed)]
    for k, n in enumerate(_SMALL):
        delta[n], new_m[n], new_v[n] = (
            (u[k][:, :heads] if n == "b_forget" else u[k]).reshape(args[n].shape) for u in updated)
    for n in _BIG + _SMALL_SHARDED:
        if n == "w_in":
            to_t, from_t = (lambda a: jnp.transpose(a, (2, 0, 1))), (lambda a: jnp.transpose(a, (1, 2, 0)))
            g_t = to_t(grads[n])
            grads[n] = from_t(g_t)
            delta[n], new_m[n], new_v[n] = (from_t(a) for a in _adamw(to_t(args[n]), g_t, to_t(args["m_" + n]), to_t(args["v_" + n]),
                                                                     keep_rank=True))
            continue
        delta[n], new_m[n], new_v[n] = _adamw(args[n], grads[n], args["m_" + n], args["v_" + n])

    return (loss, grad_x[None], *[grads[n] for n in names], *[delta[n] for n in names], *[new_m[n] for n in names],
            *[new_v[n] for n in names])
```

```python
import functools
import math

import jax
import jax.numpy as jnp
from jax import lax
from jax.experimental import pallas as pl
from jax.experimental.pallas import tpu as pltpu

F32, BF16 = jnp.float32, jnp.bfloat16
EPS = 1e-6
LANES = 128
SUBLANES = 8
VMEM_LIMIT = 56 * 1024 * 1024
MM_VMEM_BUDGET = 44 * 1024 * 1024
ADD_BLOCK_BYTES = 4 * 1024 * 1024
MESH = pl.DeviceIdType.MESH
N_CHIPS = 4
N_DEV = 8

ADAM_LR, ADAM_B1, ADAM_B2, ADAM_EPS, ADAM_WD, ADAM_STEP = 0.001, 0.9, 0.999, 1e-08, 0.01, 10


def _params(**kw):
    return pltpu.CompilerParams(vmem_limit_bytes=VMEM_LIMIT, **kw)


def _sds(shape, dtype):
    return jax.ShapeDtypeStruct(tuple(shape), dtype)


def _part(arr, off=0, width=None, lead=()):
    assert arr.ndim == len(lead) + 2
    return (arr, off, arr.shape[-1] - off if width is None else width, tuple(lead))


def _unit(parts):
    u = 0
    for _, off, w, _ in parts:
        u = math.gcd(u, math.gcd(off, w))
    return u


def _pick(dim, unit, cands, lane):
    for c in cands:
        if c <= dim and dim % c == 0 and unit % c == 0 and (not lane or c % LANES == 0):
            return c
    return min(dim, unit)


_COL_TILES = (1024, 2816, 1408, 512, 256, 128)
_ROW_TILES = (512, 1408, 256, 128)
LARGE_ROW_TILE = 1024


def _mm(name, a, b, *, ta=False, tb=False, out_dtype=F32, add=None, dest=None):
    if not isinstance(a, list):
        a = [_part(a)]
    if not isinstance(b, list):
        b = [_part(b)]
    a_rows, a_cols = a[0][0].shape[-2], sum(p[2] for p in a)
    b_rows, b_cols = b[0][0].shape[-2], sum(p[2] for p in b)
    M, Ka = (a_cols, a_rows) if ta else (a_rows, a_cols)
    Kb, N = (b_cols, b_rows) if tb else (b_rows, b_cols)
    assert Ka == Kb, (name, Ka, Kb)
    K = Ka
    ua, ub = _unit(a), _unit(b)
    shards = 1 if dest is None else dest[3]
    tm = _pick(M, ua if ta else M, _ROW_TILES, ta)
    tn = _pick(N, math.gcd(N if tb else ub, N // shards), _COL_TILES, not tb)
    ku = math.gcd(K if ta else ua, ub if tb else K)
    tk = _pick(K, ku, _COL_TILES, (not ta) or tb)
    tm_large = _pick(M, ua if ta else M, (LARGE_ROW_TILE,), ta)
    a_size = sum(p[0].dtype.itemsize for p in a)
    b_size = sum(p[0].dtype.itemsize for p in b)

    def vmem_estimate():
        return (2 * tm * tk * a_size + 2 * tk * tn * b_size + tm * tn * (2 * jnp.dtype(out_dtype).itemsize + 4)
                + (8 * tm * tn if add is not None else 0))

    if tm_large == LARGE_ROW_TILE:
        tm, tm_default = tm_large, tm
        if vmem_estimate() > MM_VMEM_BUDGET:
            tm = tm_default
    while vmem_estimate() > MM_VMEM_BUDGET:
        if tk % 256 == 0 and tk >= tn:
            tk //= 2
        elif tn % 256 == 0:
            tn //= 2
        elif tm % 256 == 0:
            tm //= 2
        elif tk % 256 == 0:
            tk //= 2
        else:
            break
    nm, nn, nk = M // tm, N // tn, K // tk
    a_tile, b_tile = (tm if ta else tk), (tk if tb else tn)
    a_nb, b_nb = a[0][2] // a_tile, b[0][2] // b_tile

    def index(parts, nb, tile, p, row, col):
        base = parts[p][1] // tile
        if len(parts) == 1:
            return parts[p][3] + (row, base + col)
        inside = (col // nb) == p
        return parts[p][3] + (jnp.where(inside, row, 0), base + jnp.clip(col - p * nb, 0, nb - 1))

    def spec(parts, p, shape, fn):
        return pl.BlockSpec((None,) * len(parts[p][3]) + shape, fn)

    in_specs, operands = [], []
    for p in range(len(a)):
        if ta:
            in_specs.append(spec(a, p, (tk, tm), lambda mi, nj, kk, p=p: index(a, a_nb, a_tile, p, kk, mi)))
        else:
            in_specs.append(spec(a, p, (tm, tk), lambda mi, nj, kk, p=p: index(a, a_nb, a_tile, p, mi, kk)))
        operands.append(a[p][0])
    for p in range(len(b)):
        if tb:
            in_specs.append(spec(b, p, (tn, tk), lambda mi, nj, kk, p=p: index(b, b_nb, b_tile, p, nj, kk)))
        else:
            in_specs.append(spec(b, p, (tk, tn), lambda mi, nj, kk, p=p: index(b, b_nb, b_tile, p, kk, nj)))
        operands.append(b[p][0])
    if add is not None:
        in_specs.append(pl.BlockSpec((tm, tn), lambda mi, nj, kk: (mi, nj)))
        operands.append(add)
    aliases = {}
    if dest is None:
        out_spec = pl.BlockSpec((tm, tn), lambda mi, nj, kk: (mi, nj))
        out_shape = _sds((M, N), out_dtype)
    else:
        d_shape, d_prev, d_lead, _ = dest
        nbs = N // shards // tn
        if shards == 1:
            out_spec = pl.BlockSpec((None,) * len(d_lead) + (tm, tn), lambda mi, nj, kk: tuple(d_lead) + (mi, nj))
        else:
            out_spec = pl.BlockSpec((None,) * (len(d_lead) + 1) + (tm, tn),
                                    lambda mi, nj, kk: tuple(d_lead) + (nj // nbs, mi, nj % nbs))
        out_shape = _sds(d_shape, out_dtype)
        if d_prev is not None:
            aliases = {len(operands): 0}
            in_specs.append(pl.BlockSpec(memory_space=pl.ANY))
            operands.append(d_prev)
    n_extra = 1 if aliases else 0
    dims = (((0 if ta else 1,), (1 if tb else 0,)), ((), ()))
    na, nb_ = len(a), len(b)

    def body(*refs):
        a_refs, b_refs = refs[:na], refs[na:na + nb_]
        pos = na + nb_
        add_ref = None
        if add is not None:
            add_ref = refs[pos]
            pos += 1
        pos += n_extra
        o_ref = refs[pos]
        acc_ref = refs[pos + 1] if nk > 1 else None
        mi, nj, kk = pl.program_id(0), pl.program_id(1), pl.program_id(2)
        a_col = mi if ta else kk
        b_col = kk if tb else nj

        def finish(r):
            if add_ref is not None:
                r = r + add_ref[...].astype(F32)
            o_ref[...] = r.astype(o_ref.dtype)

        def step(a_ref, b_ref):
            r = lax.dot_general(a_ref[...].astype(BF16), b_ref[...].astype(BF16), dims, preferred_element_type=F32)
            if nk == 1:
                finish(r)
            else:
                @pl.when(kk == 0)
                def _():
                    acc_ref[...] = r

                @pl.when(kk > 0)
                def _():
                    acc_ref[...] += r

        for pa in range(na):
            for pb in range(nb_):
                if na * nb_ == 1:
                    step(a_refs[pa], b_refs[pb])
                else:
                    cond = jnp.logical_and(a_col // a_nb == pa, b_col // b_nb == pb)
                    pl.when(cond)(functools.partial(step, a_refs[pa], b_refs[pb]))
        if nk > 1:
            @pl.when(kk == nk - 1)
            def _():
                finish(acc_ref[...])

    return pl.pallas_call(
        body, grid=(nm, nn, nk), in_specs=in_specs, out_specs=out_spec, out_shape=out_shape, input_output_aliases=aliases,
        scratch_shapes=[pltpu.VMEM((tm, tn), F32)] if nk > 1 else [],
        compiler_params=_params(dimension_semantics=("parallel", "parallel", "arbitrary")),
        name=name)(*operands)


def _row_spec(tb, part):
    _, off, w, _ = part
    return pl.BlockSpec((tb, w), lambda i, o=off // w: (i, o))


def _full_spec(arr):
    return pl.BlockSpec(arr.shape, lambda i: (0,) * arr.ndim)


def _rows(name, fn, rows, pars, outs, tb=256):
    n = rows[0][0].shape[0]
    tb = min(tb, n)
    nr, npar = len(rows), len(pars)

    def body(*refs):
        r = [ref[...].astype(F32) for ref in refs[:nr]]
        p = [ref[...].astype(F32) for ref in refs[nr:nr + npar]]
        res = fn(*r, *p)
        for o_ref, v in zip(refs[nr + npar:], res):
            o_ref[...] = v.astype(o_ref.dtype)

    out = pl.pallas_call(
        body, grid=(n // tb,),
        in_specs=[_row_spec(tb, p) for p in rows] + [_full_spec(p) for p in pars],
        out_specs=[pl.BlockSpec((tb, w), lambda i: (i, 0)) for w, _ in outs],
        out_shape=[_sds((n, w), dt) for w, dt in outs],
        compiler_params=_params(), name=name)(*[p[0] for p in rows], *pars)
    return out


def _rows_vjp(name, fn, rows, pars, cots, row_grads, par_grads, adds=None, tb=256):
    adds = adds or {}
    n = rows[0][0].shape[0]
    tb = min(tb, n)
    nr, npar, nc = len(rows), len(pars), len(cots)
    add_keys = sorted(adds)
    rg = [j for j, dt in enumerate(row_grads) if dt is not None]
    pg = [j for j, f in enumerate(par_grads) if f]

    def body(*refs):
        pos = 0
        r = [ref[...].astype(F32) for ref in refs[pos:pos + nr]]
        pos += nr
        p = [ref[...].astype(F32) for ref in refs[pos:pos + npar]]
        pos += npar
        ct = tuple(ref[...].astype(F32) for ref in refs[pos:pos + nc])
        pos += nc
        add_v = {k: ref[...].astype(F32) for k, ref in zip(add_keys, refs[pos:pos + len(add_keys)])}
        pos += len(add_keys)
        _, vjp = jax.vjp(lambda *args: tuple(fn(*args)), *r, *p)
        g = vjp(ct)
        for j in rg:
            v = g[j]
            if j in add_v:
                v = v + add_v[j]
            refs[pos][...] = v.astype(refs[pos].dtype)
            pos += 1
        first = pl.program_id(0) == 0
        for j in pg:
            ref = refs[pos]
            pos += 1

            @pl.when(first)
            def _(ref=ref):
                ref[...] = jnp.zeros_like(ref)

            ref[...] += g[nr + j]

    in_specs = ([_row_spec(tb, p) for p in rows] + [_full_spec(p) for p in pars] + [_row_spec(tb, p) for p in cots]
                + [_row_spec(tb, adds[k]) for k in add_keys])
    out_specs = ([pl.BlockSpec((tb, rows[j][2]), lambda i: (i, 0)) for j in rg] + [_full_spec(pars[j]) for j in pg])
    out_shape = ([_sds((n, rows[j][2]), row_grads[j]) for j in rg] + [_sds(pars[j].shape, F32) for j in pg])
    out = pl.pallas_call(
        body, grid=(n // tb,), in_specs=in_specs, out_specs=out_specs, out_shape=out_shape,
        compiler_params=_params(), name=name)(
            *[p[0] for p in rows], *pars, *[p[0] for p in cots], *[adds[k][0] for k in add_keys])
    return out[:len(rg)], out[len(rg):]


def _rms_fn(x, g):
    return (x * lax.rsqrt(jnp.mean(x * x, axis=-1, keepdims=True) + EPS) * g,)


def _lnsilu_fn(u, g, b):
    mu = jnp.mean(u, axis=-1, keepdims=True)
    xc = u - mu
    y = xc * lax.rsqrt(jnp.mean(xc * xc, axis=-1, keepdims=True) + EPS) * g + b
    return (y * jax.nn.sigmoid(y),)


def _merge_fn(ga, gb, gc, ya, yb, yc, ba, bb, bc):
    return (jax.nn.sigmoid(ga + ba) * ya + jax.nn.sigmoid(gb + bb) * yb + jax.nn.sigmoid(gc + bc) * yc,)


def _swiglu_fn(g, u):
    return (g * jax.nn.sigmoid(g) * u,)


def _loss_call(x, tgt, gain):
    n, d = x.shape
    tb = min(256, n)

    def body(x_ref, t_ref, g_ref, l_ref, dx_ref, dg_ref):
        (y,), vjp = jax.vjp(lambda a, b: tuple(_rms_fn(a, b)), x_ref[...], g_ref[...])
        e = y - t_ref[...]
        part = 0.5 * jnp.sum(jnp.mean(e * e, axis=-1, keepdims=True))
        dx, dg = vjp((e * (1.0 / d),))
        dx_ref[...] = dx

        @pl.when(pl.program_id(0) == 0)
        def _():
            l_ref[...] = jnp.zeros_like(l_ref)
            dg_ref[...] = jnp.zeros_like(dg_ref)

        l_ref[...] += jnp.full(l_ref.shape, part, F32)
        dg_ref[...] += dg

    return pl.pallas_call(
        body, grid=(n // tb,),
        in_specs=[pl.BlockSpec((tb, d), lambda i: (i, 0)), pl.BlockSpec((tb, d), lambda i: (i, 0)), _full_spec(gain)],
        out_specs=[pl.BlockSpec((1, LANES), lambda i: (0, 0)), pl.BlockSpec((tb, d), lambda i: (i, 0)), _full_spec(gain)],
        out_shape=[_sds((1, LANES), F32), _sds((n, d), F32), _sds(gain.shape, F32)],
        compiler_params=_params(), name="loss_head")(x, tgt, gain)


FOX_BLOCK = 256
CONV_ROWS = 64
CONV_HALO = 32


def _chunk(i):
    return pl.ds(pl.multiple_of(i * CONV_ROWS, CONV_ROWS), CONV_ROWS)


def _delayed(ext, s):
    if s % SUBLANES == 0:
        return ext[CONV_HALO - s:CONV_HALO - s + CONV_ROWS]
    return pltpu.roll(ext, s, 0)[CONV_HALO:]


def _advanced(ext, s):
    if s % SUBLANES == 0:
        return ext[s:s + CONV_ROWS]
    return pltpu.roll(ext, ext.shape[0] - s, 0)[:CONV_ROWS]


def _conv_pass(t, taps, w_ref, lanes, get_u, emit, get_dy=None, dw_acc=None):
    def body(i, carry):
        rows = _chunk(i)
        u = get_u(rows)
        ext = jnp.concatenate([carry, u], axis=0)
        dy = None if get_dy is None else get_dy(rows)
        acc = None
        for k in range(taps):
            sh = _delayed(ext, taps - 1 - k)
            term = w_ref[k:k + 1, lanes] * sh
            acc = term if acc is None else acc + term
            if dy is not None:
                dw_acc[k] += jnp.sum((dy * sh).reshape(CONV_ROWS // SUBLANES, SUBLANES, LANES), axis=0)
        emit(rows, u, acc)
        return u[CONV_ROWS - CONV_HALO:]

    lax.fori_loop(0, t // CONV_ROWS, body, jnp.zeros((CONV_HALO, LANES), F32))


def _conv_pass_t(t, taps, w_ref, lanes, get_dy, emit):
    n = t // CONV_ROWS

    def body(j, carry):
        rows = _chunk(n - 1 - j)
        dy = get_dy(rows)
        ext = jnp.concatenate([dy, carry], axis=0)
        acc = None
        for k in range(taps):
            term = w_ref[k:k + 1, lanes] * _advanced(ext, taps - 1 - k)
            acc = term if acc is None else acc + term
        emit(rows, acc)
        return dy[:CONV_HALO]

    lax.fori_loop(0, n, body, jnp.zeros((CONV_HALO, LANES), F32))


def _chan_block(d):
    return min(256, d)


def _lane_groups(cb):
    return [slice(g * LANES, (g + 1) * LANES) for g in range(cb // LANES)]


def _conv_a_fwd(za, w):
    t, d = za.shape[0], za.shape[1] // 3
    taps, cb = w.shape[0], _chan_block(d)
    nb = d // cb
    assert taps - 1 <= CONV_HALO and t % CONV_ROWS == 0

    def body(ab_ref, ac_ref, au_ref, w_ref, o_ref):
        for lanes in _lane_groups(cb):
            def emit(rows, u, s, lanes=lanes):
                o_ref[rows, lanes] = (ab_ref[rows, lanes].astype(F32) * s).astype(o_ref.dtype)

            _conv_pass(t, taps, w_ref, lanes, lambda rows, lanes=lanes: ac_ref[rows, lanes].astype(F32) * au_ref[rows, lanes].astype(F32), emit)

    return pl.pallas_call(
        body, grid=(nb,),
        in_specs=[pl.BlockSpec((t, cb), lambda j, o=o: (0, o * nb + j)) for o in range(3)]
        + [pl.BlockSpec((taps, cb), lambda j: (0, j))],
        out_specs=pl.BlockSpec((t, cb), lambda j: (0, j)),
        out_shape=_sds((t, d), BF16), compiler_params=_params(), name="conv_a_fwd")(za, za, za, w)


def _finish_taps(dw_ref, dw_acc, lanes, taps):
    for k in range(taps):
        dw_ref[k:k + 1, lanes] = jnp.sum(dw_acc[k], axis=0, keepdims=True)


def _conv_a_bwd(za, w, dpa):
    t, d = za.shape[0], za.shape[1] // 3
    taps, cb = w.shape[0], _chan_block(d)
    nb = d // cb

    def body(ab_ref, ac_ref, au_ref, w_ref, dy_ref, dab_ref, dac_ref, dau_ref, dw_ref, dw_acc):
        for lanes in _lane_groups(cb):
            dw_acc[...] = jnp.zeros_like(dw_acc)

            def ds(rows, lanes=lanes):
                return dy_ref[rows, lanes].astype(F32) * ab_ref[rows, lanes].astype(F32)

            def emit_s(rows, u, s, lanes=lanes):
                dab_ref[rows, lanes] = (dy_ref[rows, lanes].astype(F32) * s).astype(dab_ref.dtype)

            def emit_du(rows, du, lanes=lanes):
                dac_ref[rows, lanes] = (du * au_ref[rows, lanes].astype(F32)).astype(dac_ref.dtype)
                dau_ref[rows, lanes] = (du * ac_ref[rows, lanes].astype(F32)).astype(dau_ref.dtype)

            _conv_pass(t, taps, w_ref, lanes, lambda rows, lanes=lanes: ac_ref[rows, lanes].astype(F32) * au_ref[rows, lanes].astype(F32), emit_s,
                       get_dy=ds, dw_acc=dw_acc)
            _conv_pass_t(t, taps, w_ref, lanes, ds, emit_du)
            _finish_taps(dw_ref, dw_acc, lanes, taps)

    blk = pl.BlockSpec((t, cb), lambda j: (0, j))
    return pl.pallas_call(
        body, grid=(nb,),
        in_specs=[pl.BlockSpec((t, cb), lambda j, o=o: (0, o * nb + j)) for o in range(3)]
        + [pl.BlockSpec((taps, cb), lambda j: (0, j)), blk],
        out_specs=[blk, blk, blk, pl.BlockSpec((taps, cb), lambda j: (0, j))],
        out_shape=[_sds((t, d), BF16)] * 3 + [_sds((taps, d), F32)],
        scratch_shapes=[pltpu.VMEM((taps, SUBLANES, LANES), F32)],
        compiler_params=_params(), name="conv_a_bwd")(za, za, za, w, dpa)


def _conv_c_fwd(zcg, w, bias):
    t, d = zcg.shape[0], w.shape[1]
    taps, cb = w.shape[0], _chan_block(d)
    nb = d // cb
    assert taps - 1 <= CONV_HALO and t % CONV_ROWS == 0

    def body(cv_ref, cg_ref, w_ref, b_ref, o_ref):
        for lanes in _lane_groups(cb):
            def emit(rows, u, s, lanes=lanes):
                o_ref[rows, lanes] = s + b_ref[:, lanes]

            _conv_pass(t, taps, w_ref, lanes,
                       lambda rows, lanes=lanes: cv_ref[rows, lanes].astype(F32) * jax.nn.sigmoid(cg_ref[rows, lanes].astype(F32)), emit)

    return pl.pallas_call(
        body, grid=(nb,),
        in_specs=[pl.BlockSpec((t, cb), lambda j, o=o: (0, o * nb + j)) for o in range(2)]
        + [pl.BlockSpec((taps, cb), lambda j: (0, j)), pl.BlockSpec((1, cb), lambda j: (0, j))],
        out_specs=pl.BlockSpec((t, cb), lambda j: (0, j)),
        out_shape=_sds((t, d), F32), compiler_params=_params(), name="conv_c_fwd")(zcg, zcg, w, bias)


def _conv_c_bwd(zcg, w, du2):
    t, d = zcg.shape[0], w.shape[1]
    taps, cb = w.shape[0], _chan_block(d)
    nb = d // cb

    def body(cv_ref, cg_ref, w_ref, dy_ref, dcv_ref, dcg_ref, dw_ref, db_ref, dw_acc, db_acc):
        for lanes in _lane_groups(cb):
            dw_acc[...] = jnp.zeros_like(dw_acc)
            db_acc[...] = jnp.zeros_like(db_acc)

            def dy(rows, lanes=lanes):
                return dy_ref[rows, lanes].astype(F32)

            def emit_s(rows, u, s, lanes=lanes):
                db_acc[...] += jnp.sum(dy_ref[rows, lanes].astype(F32).reshape(CONV_ROWS // SUBLANES, SUBLANES, LANES), axis=0)

            def emit_du(rows, du, lanes=lanes):
                cv, sg = cv_ref[rows, lanes].astype(F32), jax.nn.sigmoid(cg_ref[rows, lanes].astype(F32))
                dcv_ref[rows, lanes] = (du * sg).astype(dcv_ref.dtype)
                dcg_ref[rows, lanes] = (du * cv * sg * (1.0 - sg)).astype(dcg_ref.dtype)

            _conv_pass(t, taps, w_ref, lanes,
                       lambda rows, lanes=lanes: cv_ref[rows, lanes].astype(F32) * jax.nn.sigmoid(cg_ref[rows, lanes].astype(F32)), emit_s,
                       get_dy=dy, dw_acc=dw_acc)
            _conv_pass_t(t, taps, w_ref, lanes, dy, emit_du)
            _finish_taps(dw_ref, dw_acc, lanes, taps)
            db_ref[:, lanes] = jnp.sum(db_acc[...], axis=0, keepdims=True)

    blk = pl.BlockSpec((t, cb), lambda j: (0, j))
    return pl.pallas_call(
        body, grid=(nb,),
        in_specs=[pl.BlockSpec((t, cb), lambda j, o=o: (0, o * nb + j)) for o in range(2)]
        + [pl.BlockSpec((taps, cb), lambda j: (0, j)), blk],
        out_specs=[blk, blk, pl.BlockSpec((taps, cb), lambda j: (0, j)), pl.BlockSpec((1, cb), lambda j: (0, j))],
        out_shape=[_sds((t, d), BF16)] * 2 + [_sds((taps, d), F32), _sds((1, d), F32)],
        scratch_shapes=[pltpu.VMEM((taps, SUBLANES, LANES), F32), pltpu.VMEM((SUBLANES, LANES), F32)],
        compiler_params=_params(), name="conv_c_bwd")(zcg, zcg, w, du2)


def _tri(tb, t, i, lower):
    r = lax.broadcasted_iota(jnp.int32, (tb, t), 0) + i * tb
    c = lax.broadcasted_iota(jnp.int32, (tb, t), 1)
    return ((r >= c) if lower else (c >= r)).astype(F32)


def _fox_cumsum(zf, bf):
    t = zf.shape[0]
    tb = min(256, t)

    def body(z_ref, b_ref, c_ref):
        logf = jax.nn.log_sigmoid(z_ref[...] + b_ref[...])
        c_ref[...] = jnp.dot(_tri(tb, t, pl.program_id(0), True), logf, precision=lax.Precision.HIGHEST,
                             preferred_element_type=F32)

    return pl.pallas_call(
        body, grid=(t // tb,), in_specs=[_full_spec(zf), _full_spec(bf)],
        out_specs=pl.BlockSpec((tb, LANES), lambda i: (i, 0)), out_shape=_sds((t, LANES), F32),
        compiler_params=_params(), name="fox_cumsum")(zf, bf)


def _fox_cumsum_bwd(zf, bf, dc_t):
    t = zf.shape[0]
    tb = min(256, t)

    def body(z_ref, b_ref, dc_ref, dz_ref, db_ref):
        i = pl.program_id(0)

        @pl.when(i == 0)
        def _():
            db_ref[...] = jnp.zeros_like(db_ref)

        dlogf = jnp.dot(_tri(tb, t, i, False), dc_ref[...], precision=lax.Precision.HIGHEST, preferred_element_type=F32)
        dz = dlogf * jax.nn.sigmoid(-(z_ref[...] + b_ref[...]))
        dz_ref[...] = dz.astype(dz_ref.dtype)
        db_ref[...] += jnp.sum(dz, axis=0, keepdims=True)

    return pl.pallas_call(
        body, grid=(t // tb,),
        in_specs=[pl.BlockSpec((tb, LANES), lambda i: (i, 0)), _full_spec(bf), _full_spec(dc_t)],
        out_specs=[pl.BlockSpec((tb, LANES), lambda i: (i, 0)), pl.BlockSpec((1, LANES), lambda i: (0, 0))],
        out_shape=[_sds((t, LANES), BF16), _sds((1, LANES), F32)],
        compiler_params=_params(), name="fox_cumsum_bwd")(zf, bf, dc_t)


def _fox_block(q, k, v, cr, *, q0, hd):
    tq, kext = q.shape[0], k.shape[0]
    nh = LANES // hd
    lane = lax.broadcasted_iota(jnp.int32, (1, LANES), 1)
    causal = (lax.broadcasted_iota(jnp.int32, (tq, kext), 0) + q0) >= lax.broadcasted_iota(jnp.int32, (tq, kext), 1)
    kb, vb = k.astype(BF16), v.astype(BF16)
    out = jnp.zeros((tq, LANES), F32)
    for h in range(nh):
        m = (lane // hd) == h
        qh = jnp.where(m, q * (hd ** -0.5), 0.0).astype(BF16)
        s = lax.dot_general(qh, kb, (((1,), (1,)), ((), ())), preferred_element_type=F32)
        c_k = cr[h] - lax.stop_gradient(cr[h][:, q0:q0 + 1])
        s = jnp.where(causal, s - c_k, -1e30)
        e = jnp.exp(s - lax.stop_gradient(jnp.max(s, axis=1, keepdims=True)))
        prob = e / jnp.sum(e, axis=1, keepdims=True)
        o = jnp.dot(prob.astype(BF16), vb, preferred_element_type=F32)
        out = out + jnp.where(m, o, 0.0)
    return out


def _fox_dims(zqkv, crow):
    t, d = zqkv.shape[0], zqkv.shape[1] // 3
    hd = d // crow.shape[0]
    assert math.log2(hd) % 2 == 0
    return t, d, hd, d // LANES, LANES // hd, min(FOX_BLOCK, t)


def _fox_specs(i, tq, nblk, nh):
    kext = (i + 1) * tq
    return [pl.BlockSpec((tq, LANES), lambda p: (i, p)), pl.BlockSpec((kext, LANES), lambda p: (0, nblk + p)),
            pl.BlockSpec((kext, LANES), lambda p: (0, 2 * nblk + p)), pl.BlockSpec((nh, 1, kext), lambda p: (p, 0, 0))]


def _fox_fwd(zqkv, crow):
    t, d, hd, nblk, nh, tq = _fox_dims(zqkv, crow)
    o = None
    for i in range(t // tq):
        def body(q_ref, k_ref, v_ref, cr_ref, *rest, i=i):
            o_ref = rest[-1]
            o_ref[...] = _fox_block(q_ref[...].astype(F32), k_ref[...], v_ref[...], cr_ref[...],
                                    q0=i * tq, hd=hd).astype(o_ref.dtype)

        prev = [] if o is None else [o]
        o = pl.pallas_call(
            body, grid=(nblk,),
            in_specs=_fox_specs(i, tq, nblk, nh) + [pl.BlockSpec(memory_space=pl.ANY)] * len(prev),
            out_specs=pl.BlockSpec((tq, LANES), lambda p, i=i: (i, p)), out_shape=_sds((t, d), BF16),
            input_output_aliases={4: 0} if prev else {},
            compiler_params=_params(), name=f"fox_fwd_q{i}")(zqkv, zqkv, zqkv, crow, *prev)
    return o


def _fox_bwd(zqkv, crow, do):
    t, d, hd, nblk, nh, tq = _fox_dims(zqkv, crow)
    nq = t // tq
    acc = None
    for i in reversed(range(nq)):
        kext = (i + 1) * tq
        first = acc is None

        def body(q_ref, k_ref, v_ref, cr_ref, do_ref, *rest, first=first, i=i):
            if first:
                dq_ref, dk_ref, dv_ref, dcr_ref = rest
            else:
                dk_in, dv_in, dcr_in = rest[1:4]
                dq_ref, dk_ref, dv_ref, dcr_ref = rest[4:]
            f = functools.partial(_fox_block, q0=i * tq, hd=hd)
            _, vjp = jax.vjp(f, q_ref[...].astype(F32), k_ref[...].astype(F32), v_ref[...].astype(F32), cr_ref[...])
            dq, dk, dv, dcr = vjp(do_ref[...].astype(F32))
            dq_ref[...] = dq.astype(dq_ref.dtype)
            if first:
                dk_ref[...], dv_ref[...], dcr_ref[...] = dk, dv, dcr
            else:
                dk_ref[...] = dk_in[...] + dk
                dv_ref[...] = dv_in[...] + dv
                dcr_ref[...] = dcr_in[...] + dcr

        qblk = pl.BlockSpec((tq, LANES), lambda p, i=i: (i, p))
        kblk = pl.BlockSpec((kext, LANES), lambda p: (0, p))
        rblk = pl.BlockSpec((nh, 1, kext), lambda p: (p, 0, 0))
        if first:
            prev, prev_specs, aliases = [], [], {}
        else:
            prev, prev_specs = list(acc), [pl.BlockSpec(memory_space=pl.ANY), kblk, kblk, rblk]
            aliases = {5: 0, 6: 1, 7: 2, 8: 3}
        acc = pl.pallas_call(
            body, grid=(nblk,),
            in_specs=_fox_specs(i, tq, nblk, nh) + [qblk] + prev_specs,
            out_specs=[qblk, kblk, kblk, rblk],
            out_shape=[_sds((t, d), BF16), _sds((t, d), F32), _sds((t, d), F32), _sds(crow.shape, F32)],
            input_output_aliases=aliases,
            compiler_params=_params(), name=f"fox_bwd_q{i}")(zqkv, zqkv, zqkv, crow, do, *prev)
    return acc


def _xattn_block(q, k, v):
    s = lax.dot_general(q.astype(BF16), k.astype(BF16), (((1,), (1,)), ((), ())), preferred_element_type=F32)
    s = s * (q.shape[1] ** -0.5)
    e = jnp.exp(s - lax.stop_gradient(jnp.max(s, axis=1, keepdims=True)))
    prob = e / jnp.sum(e, axis=1, keepdims=True)
    return jnp.dot(prob.astype(BF16), v.astype(BF16), preferred_element_type=F32)


def _xattn_fwd(qx, kv):
    t, dx = qx.shape
    nm, nh, tq = kv.shape[0], qx.shape[1] // LANES, min(512, qx.shape[0])

    def body(q_ref, k_ref, v_ref, o_ref):
        o_ref[...] = _xattn_block(q_ref[...], k_ref[...], v_ref[...]).astype(o_ref.dtype)

    return pl.pallas_call(
        body, grid=(nh, t // tq),
        in_specs=[pl.BlockSpec((tq, LANES), lambda h, i: (i, h)), pl.BlockSpec((nm, LANES), lambda h, i: (0, h)),
                  pl.BlockSpec((nm, LANES), lambda h, i: (0, nh + h))],
        out_specs=pl.BlockSpec((tq, LANES), lambda h, i: (i, h)), out_shape=_sds((t, dx), BF16),
        compiler_params=_params(), name="xattn_fwd")(qx, kv, kv)


def _xattn_bwd(qx, kv, do):
    t, dx = qx.shape
    nm, nh, tq = kv.shape[0], qx.shape[1] // LANES, min(512, qx.shape[0])

    def body(q_ref, k_ref, v_ref, do_ref, dq_ref, dk_ref, dv_ref):
        _, vjp = jax.vjp(_xattn_block, q_ref[...].astype(F32), k_ref[...].astype(F32), v_ref[...].astype(F32))
        dq, dk, dv = vjp(do_ref[...].astype(F32))
        dq_ref[...] = dq.astype(dq_ref.dtype)

        @pl.when(pl.program_id(1) == 0)
        def _():
            dk_ref[...] = jnp.zeros_like(dk_ref)
            dv_ref[...] = jnp.zeros_like(dv_ref)

        dk_ref[...] += dk
        dv_ref[...] += dv

    qblk = pl.BlockSpec((tq, LANES), lambda h, i: (i, h))
    kblk = pl.BlockSpec((nm, LANES), lambda h, i: (0, h))
    return pl.pallas_call(
        body, grid=(nh, t // tq),
        in_specs=[qblk, kblk, pl.BlockSpec((nm, LANES), lambda h, i: (0, nh + h)), qblk],
        out_specs=[qblk, kblk, kblk],
        out_shape=[_sds((t, dx), BF16), _sds((nm, dx), F32), _sds((nm, dx), F32)],
        compiler_params=_params(), name="xattn_bwd")(qx, kv, kv, do)


def _row_tile(rows, cols, budget=1 << 20, step=2 * SUBLANES):
    best = None
    for tb in range(step, rows + 1, step):
        if rows % tb == 0 and tb * cols * 4 <= budget:
            best = tb
    return rows if best is None else best


def _adamw(w, g, m, v, keep_rank=False):
    shape = w.shape
    cols = shape[-1]
    if keep_rank:
        w2, g2, m2, v2 = w, g, m, v
        rows, tail = shape[0], shape[1:]
        tb = _row_tile(rows, math.prod(tail), step=1)
    else:
        w2, g2, m2, v2 = (a.reshape(-1, cols) for a in (w, g, m, v))
        rows, tail = w2.shape[0], (cols,)
        tb = _row_tile(rows, cols)

    def body(w_ref, g_ref, m_ref, v_ref, d_ref, nm_ref, nv_ref):
        gg = g_ref[...]
        nm = ADAM_B1 * m_ref[...] + (1.0 - ADAM_B1) * gg
        nv = ADAM_B2 * v_ref[...] + (1.0 - ADAM_B2) * (gg * gg)
        m_hat = nm / (1.0 - ADAM_B1 ** ADAM_STEP)
        v_hat = nv / (1.0 - ADAM_B2 ** ADAM_STEP)
        d_ref[...] = -ADAM_LR * (m_hat / (jnp.sqrt(v_hat) + ADAM_EPS) + ADAM_WD * w_ref[...])
        nm_ref[...] = nm
        nv_ref[...] = nv

    blk = pl.BlockSpec((tb,) + tail, lambda i: (i,) + (0,) * len(tail))
    out = pl.pallas_call(
        body, grid=(rows // tb,), in_specs=[blk] * 4, out_specs=[blk] * 3, out_shape=[_sds(w2.shape, F32)] * 3,
        compiler_params=_params(), name="adamw")(w2, g2, m2, v2)
    return tuple(o.reshape(shape) for o in out)


def _sum_devices(gathered, rows):
    cols = gathered.shape[1]

    def body(g_ref, o_ref):
        acc = g_ref[0:rows, :]
        for dev in range(1, N_DEV):
            acc = acc + g_ref[dev * rows:(dev + 1) * rows, :]
        o_ref[...] = acc

    return pl.pallas_call(body, out_shape=_sds((rows, cols), F32), compiler_params=_params(), name="sum_devices")(gathered)


def _place():
    x, y, c = lax.axis_index("x"), lax.axis_index("y"), lax.axis_index("c")
    chips = [(1 - x, y), (x, 1 - y), (1 - x, 1 - y)]
    return x, y, c, chips


def _chip_id(chip):
    return 2 * chip[0] + chip[1]


def _half(c, rows):
    rh = rows // 2
    return pl.ds(pl.multiple_of(c * rh, 16), rh)


_HBM = pl.BlockSpec(memory_space=pltpu.HBM)


def _allgather8(x_shard):
    m_per, n = x_shard.shape

    def body(x_ref, out_ref, send_sems, recv_sems, local_sem):
        x, y, c, chips = _place()
        me, sibling = (x, y, c), (x, y, 1 - c)

        def rows(px, py, pc):
            return out_ref.at[pl.ds((4 * px + 2 * py + pc) * m_per, m_per), :]

        def copy(k, block, to, src=None):
            return pltpu.make_async_remote_copy(
                src_ref=rows(*block) if src is None else src, dst_ref=rows(*block),
                send_sem=send_sems.at[k], recv_sem=recv_sems.at[k], device_id=to, device_id_type=MESH)

        mine = pltpu.make_async_copy(x_ref, rows(*me), local_sem)
        mine.start()
        first = [copy(0, me, sibling, src=x_ref)]
        first += [copy(1 + j, me, (*chip, c), src=x_ref) for j, chip in enumerate(chips)]
        for cp in first:
            cp.start()
        passed = [copy(4 + j, (*chip, c), sibling) for j, chip in enumerate(chips)]
        for j, chip in enumerate(chips):
            copy(1 + j, (*chip, c), me).wait_recv()
            passed[j].start()
        copy(0, sibling, me).wait_recv()
        for j, chip in enumerate(chips):
            copy(4 + j, (*chip, 1 - c), me).wait_recv()
        for cp in first + passed:
            cp.wait_send()
        mine.wait()

    return pl.pallas_call(
        body, out_shape=_sds((N_DEV * m_per, n), x_shard.dtype),
        in_specs=[pl.BlockSpec(memory_space=pltpu.VMEM)], out_specs=pl.BlockSpec(memory_space=pltpu.VMEM),
        scratch_shapes=[pltpu.SemaphoreType.DMA((7,)), pltpu.SemaphoreType.DMA((7,)), pltpu.SemaphoreType.DMA],
        compiler_params=_params(), name="allgather8")(x_shard)


def _gather_weights(ws, layer, after):
    n = len(ws)

    def body(*refs):
        w, out = refs[:n], refs[n + 1:2 * n + 1]
        send_i, recv_i, send_d, recv_d = refs[2 * n + 1:]
        x, y, c, chips = _place()
        sibling = (x, y, 1 - c)
        me_chip = 2 * x + y

        def slab(i, chip, half):
            return out[i].at[chip, _half(half, w[i].shape[1]), :]

        def over_ici(i, j, src_chip, to, src=None):
            return pltpu.make_async_remote_copy(
                src_ref=slab(i, src_chip, c) if src is None else src, dst_ref=slab(i, src_chip, c),
                send_sem=send_i.at[3 * i + j], recv_sem=recv_i.at[3 * i + j], device_id=to, device_id_type=MESH)

        def over_d2d(i, j, half):
            return pltpu.make_async_remote_copy(
                src_ref=slab(i, _chip_id(chips[j]), half), dst_ref=slab(i, _chip_id(chips[j]), half),
                send_sem=send_d.at[3 * i + j], recv_sem=recv_d.at[3 * i + j], device_id=sibling, device_id_type=MESH)

        sent = []
        for i in range(n):
            for j, chip in enumerate(chips):
                cp = over_ici(i, j, me_chip, (*chip, c), src=w[i].at[layer, _half(c, w[i].shape[1]), :])
                cp.start()
                sent.append(cp)
        for i in range(n):
            for j, chip in enumerate(chips):
                over_ici(i, j, _chip_id(chip), (x, y, c)).wait_recv()
                cp = over_d2d(i, j, c)
                cp.start()
                sent.append(cp)
        for i in range(n):
            for j in range(3):
                over_d2d(i, j, 1 - c).wait_recv()
        for cp in sent:
            cp.wait_send()

    return pl.pallas_call(
        body, in_specs=[_HBM] * n + [pl.BlockSpec(memory_space=pl.ANY)], out_specs=[_HBM] * n,
        out_shape=[_sds((N_CHIPS,) + a.shape[1:], a.dtype) for a in ws],
        scratch_shapes=[pltpu.SemaphoreType.DMA((3 * n,)) for _ in range(4)],
        compiler_params=_params(), name="gather_weights")(*ws, after)


_SEM = pl.BlockSpec(memory_space=pltpu.SEMAPHORE)
_DATAFLOW = pltpu.SideEffectType.DATAFLOW_SIDE_EFFECTING


def _gather_copies(w, land, send, recv, layer):
    x, y, c, chips = _place()
    return [pltpu.make_async_remote_copy(
        src_ref=w[i].at[layer], dst_ref=land[i].at[2 * x + y], send_sem=send.at[3 * i + j], recv_sem=recv.at[3 * i + j],
        device_id=(*chip, c), device_id_type=MESH) for i in range(len(w)) for j, chip in enumerate(chips)]


def _gather_start(ws, layer, after):
    n = len(ws)

    def body(*refs):
        w, land = refs[:n], refs[n:2 * n]
        send, recv = refs[2 * n + 1], refs[2 * n + 2]
        for cp in _gather_copies(w, land, send, recv, layer):
            cp.start()
        refs[-1][...] = jnp.zeros_like(refs[-1])

    lands = [pltpu.with_memory_space_constraint(lax.empty((N_CHIPS,) + a.shape[1:], a.dtype), pltpu.HBM) for a in ws]
    out = pl.pallas_call(
        body, name=f"gather_start_{layer}",
        out_shape=(pltpu.SemaphoreType.DMA((3 * n,)), pltpu.SemaphoreType.DMA((3 * n,)),
                   *[pltpu.HBM(a.shape, a.dtype) for a in lands], _sds((SUBLANES, LANES), F32)),
        in_specs=[_HBM] * (2 * n) + [pl.BlockSpec(memory_space=pl.ANY)],
        out_specs=(_SEM, _SEM, *[_HBM] * n, pl.BlockSpec(memory_space=pltpu.VMEM)),
        input_output_aliases={n + i: 2 + i for i in range(n)},
        compiler_params=pltpu.CompilerParams(has_side_effects=_DATAFLOW))(*ws, *lands, after)
    return out[0], out[1], list(out[2:2 + n]), out[-1]


def _gather_wait(send, recv, ws, lands, after, layer):
    n = len(ws)

    def body(*refs):
        for cp in _gather_copies(refs[:n], refs[n:2 * n], refs[2 * n], refs[2 * n + 1], layer):
            cp.wait_send()
            cp.wait_recv()

    out = pl.pallas_call(
        body, name=f"gather_wait_{layer}",
        out_shape=tuple(pltpu.HBM(a.shape, a.dtype) for a in lands),
        in_specs=[_HBM] * (2 * n) + [_SEM, _SEM, pl.BlockSpec(memory_space=pl.ANY)], out_specs=tuple([_HBM] * n),
        input_output_aliases={n + i: i for i in range(n)},
        compiler_params=pltpu.CompilerParams(has_side_effects=_DATAFLOW))(*ws, *lands, send, recv, after)
    return list(out)


def _swap_halves(gs):
    n = len(gs)

    def body(*refs):
        g, out = refs[:n], refs[n:2 * n]
        send, recv = refs[2 * n:]
        x, y, c, _ = _place()
        cps = [pltpu.make_async_remote_copy(
            src_ref=g[i].at[:, :, _half(1 - c, g[i].shape[2]), :], dst_ref=out[i],
            send_sem=send.at[i], recv_sem=recv.at[i], device_id=(x, y, 1 - c), device_id_type=MESH) for i in range(n)]
        for cp in cps:
            cp.start()
        for cp in cps:
            cp.wait()

    return pl.pallas_call(
        body, in_specs=[_HBM] * n, out_specs=[_HBM] * n,
        out_shape=[_sds(a.shape[:2] + (a.shape[2] // 2, a.shape[3]), a.dtype) for a in gs],
        scratch_shapes=[pltpu.SemaphoreType.DMA((n,)), pltpu.SemaphoreType.DMA((n,))],
        compiler_params=_params(), name="swap_halves")(*gs)


def _exchange_chips(ss):
    n = len(ss)

    def body(*refs):
        s, out = refs[:n], refs[n:2 * n]
        send, recv = refs[2 * n:]
        x, y, c, chips = _place()
        cps = []
        for i in range(n):
            for j, chip in enumerate(chips):
                cps.append(pltpu.make_async_remote_copy(
                    src_ref=s[i].at[:, _chip_id(chip)], dst_ref=out[i].at[j],
                    send_sem=send.at[3 * i + j], recv_sem=recv.at[3 * i + j], device_id=(*chip, c), device_id_type=MESH))
        for cp in cps:
            cp.start()
        for cp in cps:
            cp.wait()

    return pl.pallas_call(
        body, in_specs=[_HBM] * n, out_specs=[_HBM] * n,
        out_shape=[_sds((3, a.shape[0]) + a.shape[2:], a.dtype) for a in ss],
        scratch_shapes=[pltpu.SemaphoreType.DMA((3 * n,)), pltpu.SemaphoreType.DMA((3 * n,))],
        compiler_params=_params(), name="exchange_chips")(*ss)


def _exchange_copies(s, land, send, recv):
    x, y, c, chips = _place()
    return [pltpu.make_async_remote_copy(
        src_ref=s[i].at[:, _chip_id(chip)], dst_ref=land[i].at[j], send_sem=send.at[3 * i + j], recv_sem=recv.at[3 * i + j],
        device_id=(*chip, c), device_id_type=MESH) for i in range(len(s)) for j, chip in enumerate(chips)]


def _exchange_start(ss, layer):
    n = len(ss)

    def body(*refs):
        s, land = refs[:n], refs[n:2 * n]
        send, recv = refs[2 * n], refs[2 * n + 1]
        token = refs[-1]
        for cp in _exchange_copies(s, land, send, recv):
            cp.start()
        token[...] = jnp.zeros_like(token)

    lands = [lax.empty((3, a.shape[0]) + a.shape[2:], a.dtype) for a in ss]
    operands = [pltpu.with_memory_space_constraint(a, pltpu.HBM) for a in list(ss) + lands]
    out = pl.pallas_call(
        body, name=f"exchange_start_{layer}",
        out_shape=(pltpu.SemaphoreType.DMA((3 * n,)), pltpu.SemaphoreType.DMA((3 * n,)),
                   *[pltpu.HBM(a.shape, a.dtype) for a in operands], _sds((SUBLANES, LANES), F32)),
        in_specs=[_HBM] * (2 * n), out_specs=(_SEM, _SEM, *[_HBM] * (2 * n), pl.BlockSpec(memory_space=pltpu.VMEM)),
        input_output_aliases={i: 2 + i for i in range(2 * n)},
        compiler_params=pltpu.CompilerParams(has_side_effects=_DATAFLOW))(*operands)
    return out[0], out[1], list(out[2:2 + n]), list(out[2 + n:2 + 2 * n]), out[-1]


def _exchange_wait(send, recv, ss, lands, after, layer):
    n = len(ss)

    def body(*refs):
        s, land = refs[:n], refs[n:2 * n]
        for cp in _exchange_copies(s, land, refs[2 * n], refs[2 * n + 1]):
            cp.wait_send()
            cp.wait_recv()

    out = pl.pallas_call(
        body, name=f"exchange_wait_{layer}",
        out_shape=tuple(pltpu.HBM(a.shape, a.dtype) for a in list(ss) + list(lands)),
        in_specs=[_HBM] * (2 * n) + [_SEM, _SEM, pl.BlockSpec(memory_space=pl.ANY)], out_specs=tuple([_HBM] * (2 * n)),
        input_output_aliases={i: i for i in range(2 * n)},
        compiler_params=pltpu.CompilerParams(has_side_effects=_DATAFLOW))(*ss, *lands, send, recv, after)
    return list(out[:n]), list(out[n:])


def _join_halves(fs):
    n = len(fs)

    def body(*refs):
        f, out = refs[:n], refs[n:2 * n]
        send, recv = refs[2 * n:]
        x, y, c, _ = _place()

        def half(ref, which):
            return ref.at[:, _half(which, ref.shape[1]), :]

        for i in range(n):
            pltpu.make_async_remote_copy(
                src_ref=half(f[i], c), dst_ref=half(out[i], c), send_sem=send.at[i], recv_sem=recv.at[i],
                device_id=(x, y, 1 - c), device_id_type=MESH).start()
        for i in range(n):
            arrive = pltpu.make_async_remote_copy(
                src_ref=half(f[i], c), dst_ref=half(out[i], 1 - c), send_sem=send.at[i], recv_sem=recv.at[i],
                device_id=(x, y, 1 - c), device_id_type=MESH)
            arrive.wait_recv()
            arrive.wait_send()

    return pl.pallas_call(
        body, in_specs=[_HBM] * n, out_specs=[_HBM] * n, out_shape=[_sds(a.shape, a.dtype) for a in fs],
        input_output_aliases={i: i for i in range(n)},
        scratch_shapes=[pltpu.SemaphoreType.DMA((n,)), pltpu.SemaphoreType.DMA((n,))],
        compiler_params=_params(), name="join_halves")(*fs)


def _add_sibling(g, got, c_idx):
    nl, ns, r, cols = g.shape
    rh = r // 2
    tb = _row_tile(rh, cols, budget=ADD_BLOCK_BYTES)
    nb = rh // tb
    g3, got3 = g.reshape(nl * ns, r, cols), got.reshape(nl * ns, rh, cols)

    def body(c_ref, a_ref, b_ref, o_ref):
        o_ref[...] = (a_ref[...].astype(F32) + b_ref[...].astype(F32)).astype(o_ref.dtype)

    out = pl.pallas_call(
        body,
        grid_spec=pltpu.PrefetchScalarGridSpec(
            num_scalar_prefetch=1, grid=(nl * ns, nb),
            in_specs=[pl.BlockSpec((None, tb, cols), lambda s, i, c_ref: (s, c_ref[0] * nb + i, 0)),
                      pl.BlockSpec((None, tb, cols), lambda s, i, c_ref: (s, i, 0))],
            out_specs=pl.BlockSpec((None, tb, cols), lambda s, i, c_ref: (s, i, 0))),
        out_shape=_sds((nl * ns, rh, cols), BF16), compiler_params=_params(), name="add_sibling")(c_idx, g3, got3)
    return out.reshape(nl, ns, rh, cols)


def _add_chips(s, got, chip_idx, c_idx, layer, depth, prev):
    _, ns, rh, cols = s.shape
    tb = _row_tile(rh, cols, budget=ADD_BLOCK_BYTES)
    nb = rh // tb

    def body(k_ref, c_ref, a_ref, b0_ref, b1_ref, b2_ref, *rest):
        o_ref = rest[-1]
        o_ref[...] = ((a_ref[...].astype(F32) + b0_ref[...].astype(F32)) + b1_ref[...].astype(F32)) + b2_ref[...].astype(F32)

    extra = [] if prev is None else [prev]
    return pl.pallas_call(
        body,
        grid_spec=pltpu.PrefetchScalarGridSpec(
            num_scalar_prefetch=2, grid=(nb,),
            in_specs=[pl.BlockSpec((None, None, tb, cols), lambda i, k_ref, c_ref: (0, k_ref[0], i, 0))]
            + [pl.BlockSpec((None, None, tb, cols), lambda i, k_ref, c_ref, j=j: (j, 0, i, 0)) for j in range(3)]
            + [pl.BlockSpec(memory_space=pl.ANY)] * len(extra),
            out_specs=pl.BlockSpec((None, tb, cols), lambda i, k_ref, c_ref: (layer, c_ref[0] * nb + i, 0))),
        out_shape=_sds((depth, 2 * rh, cols), F32), input_output_aliases={6: 0} if extra else {},
        compiler_params=_params(), name="add_chips")(chip_idx, c_idx, s, got, got, got, *extra)


def _in_pieces(d, heads, ws):
    groups, start = [], 0
    for k in range(12):
        if k == 6:
            groups.append(("f", 0, start, heads))
            start += heads
        else:
            groups.append(("main", (k - (k > 6)) * d, start, d))
            start += d
    pieces = []
    for chip in range(N_CHIPS):
        lo, hi = chip * ws, (chip + 1) * ws
        for dest, dcol, gstart, w in groups:
            a, b = max(lo, gstart), min(hi, gstart + w)
            if a < b:
                pieces.append((chip, a - lo, dest, dcol + a - gstart, b - a))
    return pieces


def _w_in_aligned(gathered, d, heads):
    rows, ws = gathered.shape[1], gathered.shape[2]
    tb = min(256, rows)
    pieces = _in_pieces(d, heads, ws)

    def body(g_ref, m_ref, f_ref):
        f_ref[...] = jnp.zeros_like(f_ref)
        for chip, sc, dest, dc, w in pieces:
            (m_ref if dest == "main" else f_ref)[:, dc:dc + w] = g_ref[chip, :, sc:sc + w]

    return pl.pallas_call(
        body, grid=(rows // tb,), in_specs=[pl.BlockSpec((N_CHIPS, tb, ws), lambda i: (0, i, 0))],
        out_specs=[pl.BlockSpec((tb, 11 * d), lambda i: (i, 0)), pl.BlockSpec((tb, LANES), lambda i: (i, 0))],
        out_shape=[_sds((rows, 11 * d), gathered.dtype), _sds((rows, LANES), gathered.dtype)],
        compiler_params=_params(), name="w_in_aligned")(gathered)


def _w_in_shards(gms, gf, prev, l, shape, heads):
    lo, hi = gms
    rows, d = lo.shape[0], lo.shape[1] // 6
    ws = shape[3]
    tb = min(256, rows)
    pieces = _in_pieces(d, heads, ws)

    def body(lo_ref, hi_ref, f_ref, *rest):
        o_ref = rest[-1]
        for chip, sc, dest, dc, w in pieces:
            if dest == "f":
                src = f_ref[:, dc:dc + w]
            elif dc < 6 * d:
                src = lo_ref[:, dc:dc + w]
            else:
                src = hi_ref[:, dc - 6 * d:dc - 6 * d + w]
            o_ref[chip, :, sc:sc + w] = src

    extra = [] if prev is None else [prev]
    return pl.pallas_call(
        body, grid=(rows // tb,),
        in_specs=[pl.BlockSpec((tb, 6 * d), lambda i: (i, 0)), pl.BlockSpec((tb, 5 * d), lambda i: (i, 0)),
                  pl.BlockSpec((tb, LANES), lambda i: (i, 0))] + [pl.BlockSpec(memory_space=pl.ANY)] * len(extra),
        out_specs=pl.BlockSpec((None, N_CHIPS, tb, ws), lambda i: (l, 0, i, 0)), out_shape=_sds(shape, lo.dtype),
        input_output_aliases={3: 0} if extra else {},
        compiler_params=_params(), name="w_in_shards")(lo, hi, gf, *extra)


def _pad_lanes(a, width=LANES):
    return jnp.pad(a, ((0, 0), (0, width - a.shape[1])))


def _cols(operand):
    return sum(p[2] for p in operand) if isinstance(operand, list) else operand.shape[1]


def _local_step(x, mem, tgt, sp, wt, dests=None, on_layer=None):
    t, d = x.shape
    depth = sp["mix_norm"].shape[0]
    heads = sp["b_forget"].shape[1]
    row = lambda a, l: a[l:l + 1]
    weights = wt if callable(wt) else (lambda l, x_in: {k: v[l] for k, v in wt.items()})
    layer_weights = []

    memn, = _rows("rms_mem", _rms_fn, [_part(mem)], [sp["mem_norm"][None]], [(d, BF16)])
    saved = []
    for l in range(depth):
        wl = weights(l, x)
        layer_weights.append(wl)
        f = _cols(wl["gu"]) // 2
        sv = {"x0": x}
        h1, = _rows("rms_mix", _rms_fn, [_part(x)], [row(sp["mix_norm"], l)], [(d, BF16)])
        wm = wl["main"]
        za = _mm("mm_in_a", h1, [_part(wm, 0, 3 * d)], out_dtype=BF16)
        zqkv = _mm("mm_in_qkv", h1, [_part(wm, 3 * d, 3 * d)], out_dtype=BF16)
        zcg = _mm("mm_in_cg", h1, [_part(wm, 6 * d, 5 * d)], out_dtype=BF16)
        zf = _mm("mm_in_f", h1, wl["f"])
        bf = _pad_lanes(row(sp["b_forget"], l))
        c = _fox_cumsum(zf, bf)
        crow = c[:, :heads].T.reshape(heads, 1, t)
        o = _fox_fwd(zqkv, crow)
        pa = _conv_a_fwd(za, sp["conv_a"][l])
        u2 = _conv_c_fwd(zcg, sp["conv_c"][l], row(sp["conv_c_bias"], l))
        u3, = _rows("ln_silu", _lnsilu_fn, [_part(u2)], [row(sp["ln_c_gain"], l), row(sp["ln_c_bias"], l)], [(d, BF16)])
        ya = _mm("mm_out_a", pa, wl["out_a"])
        yb = _mm("mm_out_b", o, wl["out_b"])
        yc = _mm("mm_out_c", u3, wl["out_c"])
        bg = [sp["b_gate"][l:l + 1, k * d:(k + 1) * d] for k in range(3)]
        gate_rows = [_part(zcg, (2 + k) * d, d) for k in range(3)] + [_part(ya), _part(yb), _part(yc)]
        mg, = _rows("merge", _merge_fn, gate_rows, bg, [(d, BF16)])
        x1 = _mm("mm_o", mg, wl["o"], add=x)
        h2, = _rows("rms_x", _rms_fn, [_part(x1)], [row(sp["xattn_norm"], l)], [(d, BF16)])
        qx = _mm("mm_xq", h2, wl["xq"], out_dtype=BF16)
        kv = _mm("mm_xkv", memn, wl["xkv"], out_dtype=BF16)
        ox = _xattn_fwd(qx, kv)
        x2 = _mm("mm_xo", ox, wl["xo"], add=x1)
        h3, = _rows("rms_ffn", _rms_fn, [_part(x2)], [row(sp["ffn_norm"], l)], [(d, BF16)])
        gu = _mm("mm_gu", h3, wl["gu"], out_dtype=BF16)
        act, = _rows("swiglu", _swiglu_fn, [_part(gu, 0, f), _part(gu, f, f)], [], [(f, BF16)])
        x3 = _mm("mm_down", act, wl["down"], add=x2)
        sv.update(h1=h1, za=za, zqkv=zqkv, zcg=zcg, zf=zf, bf=bf, crow=crow, o=o, pa=pa, u2=u2, u3=u3, ya=ya, yb=yb,
                  yc=yc, bg=bg, gate_rows=gate_rows, mg=mg, x1=x1, h2=h2, qx=qx, kv=kv, ox=ox, x2=x2, h3=h3, gu=gu, act=act)
        saved.append(sv)
        x = x3

    loss_row, dx, d_final = _loss_call(x, tgt, sp["final_norm"][None])

    gs = {k: [None] * depth for k in ("mix_norm", "b_gate", "b_forget", "conv_a", "conv_c", "conv_c_bias", "ln_c_gain",
                                      "ln_c_bias", "xattn_norm", "ffn_norm")}
    gw = {k: [None] * depth for k in ("main", "f", "out_a", "out_b", "out_c", "o", "xq", "xkv", "xo", "gu", "down")}
    def weight_grad(key, l, name, a_op, b_op):
        if dests is None or key not in dests:
            gw[key][l] = _mm(name, a_op, b_op, ta=True, out_dtype=BF16)
        else:
            shape, shards = dests[key]
            gw[key][l] = _mm(name, a_op, b_op, ta=True, out_dtype=BF16, dest=(shape, None, (0,), shards))

    dmemn = None
    for l in reversed(range(depth)):
        sv, wl = saved[l], layer_weights[l]
        dact = _mm("mm_down_dx", dx, wl["down"], tb=True)
        weight_grad("down", l, "mm_down_dw", sv["act"], dx)
        (dg, du), _ = _rows_vjp("swiglu_bwd", _swiglu_fn, [_part(sv["gu"], 0, f), _part(sv["gu"], f, f)], [], [_part(dact)],
                                [BF16, BF16], [])
        dgu = [_part(dg), _part(du)]
        dh3 = _mm("mm_gu_dx", dgu, wl["gu"], tb=True)
        weight_grad("gu", l, "mm_gu_dw", sv["h3"], dgu)
        (dx2,), (gs["ffn_norm"][l],) = _rows_vjp("rms_ffn_bwd", _rms_fn, [_part(sv["x2"])], [row(sp["ffn_norm"], l)],
                                                 [_part(dh3)], [F32], [True], adds={0: _part(dx)})
        dox = _mm("mm_xo_dx", dx2, wl["xo"], tb=True)
        weight_grad("xo", l, "mm_xo_dw", sv["ox"], dx2)
        dqx, dkx, dvx = _xattn_bwd(sv["qx"], sv["kv"], dox)
        dh2 = _mm("mm_xq_dx", dqx, wl["xq"], tb=True)
        weight_grad("xq", l, "mm_xq_dw", sv["h2"], dqx)
        dkv = [_part(dkx), _part(dvx)]
        dmemn = _mm("mm_xkv_dx", dkv, wl["xkv"], tb=True, add=dmemn)
        weight_grad("xkv", l, "mm_xkv_dw", memn, dkv)
        (dx1,), (gs["xattn_norm"][l],) = _rows_vjp("rms_x_bwd", _rms_fn, [_part(sv["x1"])], [row(sp["xattn_norm"], l)],
                                                   [_part(dh2)], [F32], [True], adds={0: _part(dx2)})
        dmg = _mm("mm_o_dx", dx1, wl["o"], tb=True)
        weight_grad("o", l, "mm_o_dw", sv["mg"], dx1)
        (dga, dgb, dgc, dya, dyb, dyc), dbg = _rows_vjp("merge_bwd", _merge_fn, sv["gate_rows"], sv["bg"], [_part(dmg)],
                                                        [BF16] * 6, [True] * 3)
        gs["b_gate"][l] = jnp.concatenate(dbg, axis=1)
        dpa = _mm("mm_out_a_dx", dya, wl["out_a"], tb=True)
        weight_grad("out_a", l, "mm_out_a_dw", sv["pa"], dya)
        do = _mm("mm_out_b_dx", dyb, wl["out_b"], tb=True)
        weight_grad("out_b", l, "mm_out_b_dw", sv["o"], dyb)
        du3 = _mm("mm_out_c_dx", dyc, wl["out_c"], tb=True)
        weight_grad("out_c", l, "mm_out_c_dw", sv["u3"], dyc)
        if on_layer is not None:
            du3 = du3 + on_layer(l, "early", {k: v[l] for k, v in gw.items() if k not in ("main", "f")})[0, 0]
        (du2,), (gs["ln_c_gain"][l], gs["ln_c_bias"][l]) = _rows_vjp(
            "ln_silu_bwd", _lnsilu_fn, [_part(sv["u2"])], [row(sp["ln_c_gain"], l), row(sp["ln_c_bias"], l)], [_part(du3)],
            [F32], [True, True])
        dcv, dcg, gs["conv_c"][l], gs["conv_c_bias"][l] = _conv_c_bwd(sv["zcg"], sp["conv_c"][l], du2)
        dab, dac, dau, gs["conv_a"][l] = _conv_a_bwd(sv["za"], sp["conv_a"][l], dpa)
        dq, dk, dv, dcr = _fox_bwd(sv["zqkv"], sv["crow"], do)
        dzf, dbf = _fox_cumsum_bwd(sv["zf"], sv["bf"], _pad_lanes(dcr.reshape(heads, t).T))
        gs["b_forget"][l] = dbf[:, :heads]
        dz = [_part(a) for a in (dab, dac, dau, dq, dk, dv, dcv, dcg, dga, dgb, dgc)]
        dh1f = _mm("mm_in_f_dx", dzf, wl["f"], tb=True)
        dh1 = _mm("mm_in_dx", dz, wl["main"], tb=True, add=dh1f)
        gw["main"][l] = [_mm("mm_in_dw_lo", sv["h1"], dz[:6], ta=True, out_dtype=BF16),
                         _mm("mm_in_dw_hi", sv["h1"], dz[6:], ta=True, out_dtype=BF16)]
        gw["f"][l] = _mm("mm_in_f_dw", sv["h1"], dzf, ta=True, out_dtype=BF16)
        (dx,), (gs["mix_norm"][l],) = _rows_vjp("rms_mix_bwd", _rms_fn, [_part(sv["x0"])], [row(sp["mix_norm"], l)],
                                                [_part(dh1)], [F32], [True], adds={0: _part(dx1)})
        if on_layer is not None:
            token = on_layer(l, "late", {"main": gw["main"][l], "f": gw["f"][l]})
            if l > 0:
                dx = dx + token[0, 0]

    _, (d_mem_norm,) = _rows_vjp("rms_mem_bwd", _rms_fn, [_part(mem)], [sp["mem_norm"][None]], [_part(dmemn)], [None], [True])
    small = {k: jnp.stack([a.reshape(sp[k].shape[1:]) for a in v]) for k, v in gs.items()}
    small["mem_norm"] = d_mem_norm[0]
    small["final_norm"] = d_final[0]
    return loss_row, dx, small, gw


_BIG = ("w_in", "w_out_a", "w_out_b", "w_out_c", "w_o", "w_xq", "w_xkv", "w_xo", "w_gate_up", "w_down")
_COL_SHARDED = ("w_in", "w_xo", "w_gate_up")
_KEY = {"w_out_a": "out_a", "w_out_b": "out_b", "w_out_c": "out_c", "w_o": "o", "w_xq": "xq", "w_xkv": "xkv", "w_xo": "xo",
        "w_gate_up": "gu", "w_down": "down"}
_SMALL = ("mix_norm", "b_gate", "b_forget", "conv_c_bias", "ln_c_gain", "ln_c_bias", "xattn_norm", "ffn_norm", "mem_norm",
          "final_norm")
_SMALL_SHARDED = ("conv_a", "conv_c")


def _pack_rows(parts):
    padded = []
    for a in parts:
        pad = -a.shape[0] % SUBLANES
        padded.append(jnp.pad(a, ((0, pad), (0, 0))) if pad else a)
    return jnp.concatenate(padded, axis=0)


def _unpack_rows(packed, shapes):
    out, pos = [], 0
    for r in shapes:
        out.append(packed[pos:pos + r])
        pos += r + (-r % SUBLANES)
    return out


def kernel(x, mem, mix_norm, w_in, b_gate, b_forget, conv_a, w_out_a, w_out_b, conv_c, conv_c_bias, ln_c_gain, ln_c_bias, w_out_c, w_o, xattn_norm, mem_norm, w_xq, w_xkv, w_xo, ffn_norm, w_gate_up, w_down, final_norm, loss_target, m_mix_norm, m_w_in, m_b_gate, m_b_forget, m_conv_a, m_w_out_a, m_w_out_b, m_conv_c, m_conv_c_bias, m_ln_c_gain, m_ln_c_bias, m_w_out_c, m_w_o, m_xattn_norm, m_mem_norm, m_w_xq, m_w_xkv, m_w_xo, m_ffn_norm, m_w_gate_up, m_w_down, m_final_norm, v_mix_norm, v_w_in, v_b_gate, v_b_forget, v_conv_a, v_w_out_a, v_w_out_b, v_conv_c, v_conv_c_bias, v_ln_c_gain, v_ln_c_bias, v_w_out_c, v_w_o, v_xattn_norm, v_mem_norm, v_w_xq, v_w_xkv, v_w_xo, v_ffn_norm, v_w_gate_up, v_w_down, v_final_norm):
    args = dict(locals())
    names = ["mix_norm", "w_in", "b_gate", "b_forget", "conv_a", "w_out_a", "w_out_b", "conv_c", "conv_c_bias", "ln_c_gain",
             "ln_c_bias", "w_out_c", "w_o", "xattn_norm", "mem_norm", "w_xq", "w_xkv", "w_xo", "ffn_norm", "w_gate_up", "w_down",
             "final_norm"]
    depth, d = mix_norm.shape
    heads = b_forget.shape[1]
    cx, cy, cc = lax.axis_index("x"), lax.axis_index("y"), lax.axis_index("c")
    chip = 2 * cx + cy
    c_idx = jnp.reshape(cc, (1,)).astype(jnp.int32)
    chip_idx = jnp.reshape(chip, (1,)).astype(jnp.int32)

    conv_rows = [conv_a.shape[1] * depth, conv_c.shape[1] * depth]
    conv_pack = _pack_rows([conv_a.reshape(conv_rows[0], -1), conv_c.reshape(conv_rows[1], -1)])
    conv_all = _allgather8(conv_pack).reshape(N_CHIPS, 2, conv_pack.shape[0], conv_pack.shape[1])[:, 0]
    conv_all = conv_all.transpose(1, 0, 2).reshape(conv_pack.shape[0], d)
    conv_a_full, conv_c_full = _unpack_rows(conv_all, conv_rows)
    own = [args[n].astype(BF16) for n in _BIG]
    first = _gather_weights(own, 0, conv_all)
    started, tokens = {}, []
    for l in range(1, depth):
        send, recv, lands, token = _gather_start(own, l, first[0])
        started[l] = (send, recv, own, lands)
        tokens.append(token)

    def layer_weights(l, x_in):
        if l == 0:
            lands = first
        else:
            lands = _gather_wait(*started[l], x_in, l)
        full = [lax.dynamic_update_slice(g, w[l][None], (chip, 0, 0)) for g, w in zip(lands, own)]
        wl = {}
        wl["main"], wl["f"] = _w_in_aligned(full[0], d, heads)
        for n, g in zip(_BIG[1:], full[1:]):
            if n == "w_xo":
                wl[_KEY[n]] = g.transpose(1, 0, 2).reshape(g.shape[1], N_CHIPS * g.shape[2])
            elif n in _COL_SHARDED:
                wl[_KEY[n]] = [_part(g, lead=(j,)) for j in range(N_CHIPS)]
            else:
                wl[_KEY[n]] = g.reshape(N_CHIPS * g.shape[1], g.shape[2])
        return wl

    shard_shape = {n: (1, N_CHIPS) + args[n].shape[1:] for n in _BIG}
    dests = {_KEY[n]: (shard_shape[n], N_CHIPS) if n in _COL_SHARDED
             else ((1, N_CHIPS * args[n].shape[1], args[n].shape[2]), 1) for n in _BIG[1:]}
    sp = {n: args[n] for n in _SMALL}
    sp["conv_a"] = conv_a_full.reshape(depth, conv_a.shape[1], d)
    sp["conv_c"] = conv_c_full.reshape(depth, conv_c.shape[1], d)

    in_flight = {}

    def on_layer(l, which, gl):
        if which == "late":
            group = ["w_in"]
            local = [_w_in_shards(gl["main"], gl["f"], None, 0, shard_shape["w_in"], heads)]
        else:
            group = list(_BIG[1:])
            local = [gl[_KEY[n]].reshape(shard_shape[n]) for n in group]
        chip_sums = [_add_sibling(g, r, c_idx) for g, r in zip(local, _swap_halves(local))]
        send, recv, chip_sums, lands, token = _exchange_start(chip_sums, f"{l}_{which}")
        in_flight[(l, which)] = (group, send, recv, chip_sums, lands)
        return token

    x_in = x[0]
    for token in tokens:
        x_in = x_in + token[0, 0]
    loss_row, grad_x, gsmall, gw = _local_step(x_in, mem[0], loss_target[0], sp, layer_weights, dests, on_layer)
    halves = {n: None for n in _BIG}
    for l in reversed(range(depth)):
        for which in ("early", "late"):
            group, send, recv, chip_sums, lands = in_flight[(l, which)]
            chip_sums, lands = _exchange_wait(send, recv, chip_sums, lands, grad_x, f"{l}_{which}")
            for n, s_, r in zip(group, chip_sums, lands):
                halves[n] = _add_chips(s_, r, chip_idx, c_idx, l, depth, halves[n])
    halves = [halves[n] for n in _BIG]
    grads = dict(zip(_BIG, _join_halves(halves)))

    def as_rows(n, a):
        return _pad_lanes(a, d) if n == "b_forget" else a.reshape(-1, d)

    pieces = [as_rows(n, gsmall[n]) for n in _SMALL]
    for n in _SMALL_SHARDED:
        pieces.append(gsmall[n].reshape(-1, d))
    pieces.append(_pad_lanes(loss_row, d))
    pack = _pack_rows(pieces)
    total = _sum_devices(_allgather8(pack), pack.shape[0])
    shapes = [p.shape[0] for p in pieces]
    summed = _unpack_rows(total, shapes)
    for n, g in zip(_SMALL, summed):
        grads[n] = g[:, :heads] if n == "b_forget" else g.reshape(args[n].shape)
    for n, g in zip(_SMALL_SHARDED, summed[len(_SMALL):]):
        g = g.reshape(args[n].shape[:2] + (d,))
        grads[n] = lax.dynamic_slice_in_dim(g, chip * args[n].shape[2], args[n].shape[2], axis=2)
    loss = summed[-1][0, 0]

    delta, new_m, new_v = {}, {}, {}
    packed = [_pack_rows([as_rows(n, src[pre + n]) for n in _SMALL])
              for src, pre in ((args, ""), (grads, ""), (args, "m_"), (args, "v_"))]
    srows = [as_rows(n, args[n]).shape[0] for n in _SMALL]
    updated = [_unpack_rows(a, srows) for a in _adamw(*packed)]
    for k, n in enumerate(_SMALL):
        delta[n], new_m[n], new_v[n] = (
            (u[k][:, :heads] if n == "b_forget" else u[k]).reshape(args[n].shape) for u in updated)
    for n in _BIG + _SMALL_SHARDED:
        if n == "w_in":
            to_t, from_t = (lambda a: jnp.transpose(a, (2, 0, 1))), (lambda a: jnp.transpose(a, (1, 2, 0)))
            g_t = to_t(grads[n])
            grads[n] = from_t(g_t)
            delta[n], new_m[n], new_v[n] = (from_t(a) for a in _adamw(to_t(args[n]), g_t, to_t(args["m_" + n]), to_t(args["v_" + n]),
                                                                     keep_rank=True))
            continue
        delta[n], new_m[n], new_v[n] = _adamw(args[n], grads[n], args["m_" + n], args["v_" + n])

    return (loss, grad_x[None], *[grads[n] for n in names], *[delta[n] for n in names], *[new_m[n] for n in names],
            *[new_v[n] for n in names])
```

```python
import functools
import math

import jax
import jax.numpy as jnp
from jax import lax
from jax.experimental import pallas as pl
from jax.experimental.pallas import tpu as pltpu

F32, BF16 = jnp.float32, jnp.bfloat16
EPS = 1e-6
LANES = 128
SUBLANES = 8
VMEM_LIMIT = 56 * 1024 * 1024
MM_VMEM_BUDGET = 44 * 1024 * 1024
ADD_BLOCK_BYTES = 4 * 1024 * 1024
MESH = pl.DeviceIdType.MESH
N_CHIPS = 4
N_DEV = 8

ADAM_LR, ADAM_B1, ADAM_B2, ADAM_EPS, ADAM_WD, ADAM_STEP = 0.001, 0.9, 0.999, 1e-08, 0.01, 10


def _params(**kw):
    return pltpu.CompilerParams(vmem_limit_bytes=VMEM_LIMIT, **kw)


def _sds(shape, dtype):
    return jax.ShapeDtypeStruct(tuple(shape), dtype)


def _part(arr, off=0, width=None, lead=()):
    assert arr.ndim == len(lead) + 2
    return (arr, off, arr.shape[-1] - off if width is None else width, tuple(lead))


def _unit(parts):
    u = 0
    for _, off, w, _ in parts:
        u = math.gcd(u, math.gcd(off, w))
    return u


def _pick(dim, unit, cands, lane):
    for c in cands:
        if c <= dim and dim % c == 0 and unit % c == 0 and (not lane or c % LANES == 0):
            return c
    return min(dim, unit)


_COL_TILES = (1024, 2816, 1408, 512, 256, 128)
_ROW_TILES = (512, 1408, 256, 128)
LARGE_ROW_TILE = 1024


def _mm(name, a, b, *, ta=False, tb=False, out_dtype=F32, add=None, dest=None):
    if not isinstance(a, list):
        a = [_part(a)]
    if not isinstance(b, list):
        b = [_part(b)]
    a_rows, a_cols = a[0][0].shape[-2], sum(p[2] for p in a)
    b_rows, b_cols = b[0][0].shape[-2], sum(p[2] for p in b)
    M, Ka = (a_cols, a_rows) if ta else (a_rows, a_cols)
    Kb, N = (b_cols, b_rows) if tb else (b_rows, b_cols)
    assert Ka == Kb, (name, Ka, Kb)
    K = Ka
    ua, ub = _unit(a), _unit(b)
    shards = 1 if dest is None else dest[3]
    tm = _pick(M, ua if ta else M, _ROW_TILES, ta)
    tn = _pick(N, math.gcd(N if tb else ub, N // shards), _COL_TILES, not tb)
    ku = math.gcd(K if ta else ua, ub if tb else K)
    tk = _pick(K, ku, _COL_TILES, (not ta) or tb)
    tm_large = _pick(M, ua if ta else M, (LARGE_ROW_TILE,), ta)
    a_size = sum(p[0].dtype.itemsize for p in a)
    b_size = sum(p[0].dtype.itemsize for p in b)

    def vmem_estimate():
        return (2 * tm * tk * a_size + 2 * tk * tn * b_size + tm * tn * (2 * jnp.dtype(out_dtype).itemsize + 4)
                + (8 * tm * tn if add is not None else 0))

    if tm_large == LARGE_ROW_TILE:
        tm, tm_default = tm_large, tm
        if vmem_estimate() > MM_VMEM_BUDGET:
            tm = tm_default
    while vmem_estimate() > MM_VMEM_BUDGET:
        if tk % 256 == 0 and tk >= tn:
            tk //= 2
        elif tn % 256 == 0:
            tn //= 2
        elif tm % 256 == 0:
            tm //= 2
        elif tk % 256 == 0:
            tk //= 2
        else:
            break
    nm, nn, nk = M // tm, N // tn, K // tk
    a_tile, b_tile = (tm if ta else tk), (tk if tb else tn)
    a_nb, b_nb = a[0][2] // a_tile, b[0][2] // b_tile

    def index(parts, nb, tile, p, row, col):
        base = parts[p][1] // tile
        if len(parts) == 1:
            return parts[p][3] + (row, base + col)
        inside = (col // nb) == p
        return parts[p][3] + (jnp.where(inside, row, 0), base + jnp.clip(col - p * nb, 0, nb - 1))

    def spec(parts, p, shape, fn):
        return pl.BlockSpec((None,) * len(parts[p][3]) + shape, fn)

    in_specs, operands = [], []
    for p in range(len(a)):
        if ta:
            in_specs.append(spec(a, p, (tk, tm), lambda mi, nj, kk, p=p: index(a, a_nb, a_tile, p, kk, mi)))
        else:
            in_specs.append(spec(a, p, (tm, tk), lambda mi, nj, kk, p=p: index(a, a_nb, a_tile, p, mi, kk)))
        operands.append(a[p][0])
    for p in range(len(b)):
        if tb:
            in_specs.append(spec(b, p, (tn, tk), lambda mi, nj, kk, p=p: index(b, b_nb, b_tile, p, nj, kk)))
        else:
            in_specs.append(spec(b, p, (tk, tn), lambda mi, nj, kk, p=p: index(b, b_nb, b_tile, p, kk, nj)))
        operands.append(b[p][0])
    if add is not None:
        in_specs.append(pl.BlockSpec((tm, tn), lambda mi, nj, kk: (mi, nj)))
        operands.append(add)
    aliases = {}
    if dest is None:
        out_spec = pl.BlockSpec((tm, tn), lambda mi, nj, kk: (mi, nj))
        out_shape = _sds((M, N), out_dtype)
    else:
        d_shape, d_prev, d_lead, _ = dest
        nbs = N // shards // tn
        if shards == 1:
            out_spec = pl.BlockSpec((None,) * len(d_lead) + (tm, tn), lambda mi, nj, kk: tuple(d_lead) + (mi, nj))
        else:
            out_spec = pl.BlockSpec((None,) * (len(d_lead) + 1) + (tm, tn),
                                    lambda mi, nj, kk: tuple(d_lead) + (nj // nbs, mi, nj % nbs))
        out_shape = _sds(d_shape, out_dtype)
        if d_prev is not None:
            aliases = {len(operands): 0}
            in_specs.append(pl.BlockSpec(memory_space=pl.ANY))
            operands.append(d_prev)
    n_extra = 1 if aliases else 0
    dims = (((0 if ta else 1,), (1 if tb else 0,)), ((), ()))
    na, nb_ = len(a), len(b)

    def body(*refs):
        a_refs, b_refs = refs[:na], refs[na:na + nb_]
        pos = na + nb_
        add_ref = None
        if add is not None:
            add_ref = refs[pos]
            pos += 1
        pos += n_extra
        o_ref = refs[pos]
        acc_ref = refs[pos + 1] if nk > 1 else None
        mi, nj, kk = pl.program_id(0), pl.program_id(1), pl.program_id(2)
        a_col = mi if ta else kk
        b_col = kk if tb else nj

        def finish(r):
            if add_ref is not None:
                r = r + add_ref[...].astype(F32)
            o_ref[...] = r.astype(o_ref.dtype)

        def step(a_ref, b_ref):
            r = lax.dot_general(a_ref[...].astype(BF16), b_ref[...].astype(BF16), dims, preferred_element_type=F32)
            if nk == 1:
                finish(r)
            else:
                @pl.when(kk == 0)
                def _():
                    acc_ref[...] = r

                @pl.when(kk > 0)
                def _():
                    acc_ref[...] += r

        for pa in range(na):
            for pb in range(nb_):
                if na * nb_ == 1:
                    step(a_refs[pa], b_refs[pb])
                else:
                    cond = jnp.logical_and(a_col // a_nb == pa, b_col // b_nb == pb)
                    pl.when(cond)(functools.partial(step, a_refs[pa], b_refs[pb]))
        if nk > 1:
            @pl.when(kk == nk - 1)
            def _():
                finish(acc_ref[...])

    return pl.pallas_call(
        body, grid=(nm, nn, nk), in_specs=in_specs, out_specs=out_spec, out_shape=out_shape, input_output_aliases=aliases,
        scratch_shapes=[pltpu.VMEM((tm, tn), F32)] if nk > 1 else [],
        compiler_params=_params(dimension_semantics=("parallel", "parallel", "arbitrary")),
        name=name)(*operands)


def _row_spec(tb, part):
    _, off, w, _ = part
    return pl.BlockSpec((tb, w), lambda i, o=off // w: (i, o))


def _full_spec(arr):
    return pl.BlockSpec(arr.shape, lambda i: (0,) * arr.ndim)


def _rows(name, fn, rows, pars, outs, tb=256):
    n = rows[0][0].shape[0]
    tb = min(tb, n)
    nr, npar = len(rows), len(pars)

    def body(*refs):
        r = [ref[...].astype(F32) for ref in refs[:nr]]
        p = [ref[...].astype(F32) for ref in refs[nr:nr + npar]]
        res = fn(*r, *p)
        for o_ref, v in zip(refs[nr + npar:], res):
            o_ref[...] = v.astype(o_ref.dtype)

    out = pl.pallas_call(
        body, grid=(n // tb,),
        in_specs=[_row_spec(tb, p) for p in rows] + [_full_spec(p) for p in pars],
        out_specs=[pl.BlockSpec((tb, w), lambda i: (i, 0)) for w, _ in outs],
        out_shape=[_sds((n, w), dt) for w, dt in outs],
        compiler_params=_params(), name=name)(*[p[0] for p in rows], *pars)
    return out


def _rows_vjp(name, fn, rows, pars, cots, row_grads, par_grads, adds=None, tb=256):
    adds = adds or {}
    n = rows[0][0].shape[0]
    tb = min(tb, n)
    nr, npar, nc = len(rows), len(pars), len(cots)
    add_keys = sorted(adds)
    rg = [j for j, dt in enumerate(row_grads) if dt is not None]
    pg = [j for j, f in enumerate(par_grads) if f]

    def body(*refs):
        pos = 0
        r = [ref[...].astype(F32) for ref in refs[pos:pos + nr]]
        pos += nr
        p = [ref[...].astype(F32) for ref in refs[pos:pos + npar]]
        pos += npar
        ct = tuple(ref[...].astype(F32) for ref in refs[pos:pos + nc])
        pos += nc
        add_v = {k: ref[...].astype(F32) for k, ref in zip(add_keys, refs[pos:pos + len(add_keys)])}
        pos += len(add_keys)
        _, vjp = jax.vjp(lambda *args: tuple(fn(*args)), *r, *p)
        g = vjp(ct)
        for j in rg:
            v = g[j]
            if j in add_v:
                v = v + add_v[j]
            refs[pos][...] = v.astype(refs[pos].dtype)
            pos += 1
        first = pl.program_id(0) == 0
        for j in pg:
            ref = refs[pos]
            pos += 1

            @pl.when(first)
            def _(ref=ref):
                ref[...] = jnp.zeros_like(ref)

            ref[...] += g[nr + j]

    in_specs = ([_row_spec(tb, p) for p in rows] + [_full_spec(p) for p in pars] + [_row_spec(tb, p) for p in cots]
                + [_row_spec(tb, adds[k]) for k in add_keys])
    out_specs = ([pl.BlockSpec((tb, rows[j][2]), lambda i: (i, 0)) for j in rg] + [_full_spec(pars[j]) for j in pg])
    out_shape = ([_sds((n, rows[j][2]), row_grads[j]) for j in rg] + [_sds(pars[j].shape, F32) for j in pg])
    out = pl.pallas_call(
        body, grid=(n // tb,), in_specs=in_specs, out_specs=out_specs, out_shape=out_shape,
        compiler_params=_params(), name=name)(
            *[p[0] for p in rows], *pars, *[p[0] for p in cots], *[adds[k][0] for k in add_keys])
    return out[:len(rg)], out[len(rg):]


def _rms_fn(x, g):
    return (x * lax.rsqrt(jnp.mean(x * x, axis=-1, keepdims=True) + EPS) * g,)


def _lnsilu_fn(u, g, b):
    mu = jnp.mean(u, axis=-1, keepdims=True)
    xc = u - mu
    y = xc * lax.rsqrt(jnp.mean(xc * xc, axis=-1, keepdims=True) + EPS) * g + b
    return (y * jax.nn.sigmoid(y),)


def _merge_fn(ga, gb, gc, ya, yb, yc, ba, bb, bc):
    return (jax.nn.sigmoid(ga + ba) * ya + jax.nn.sigmoid(gb + bb) * yb + jax.nn.sigmoid(gc + bc) * yc,)


def _swiglu_fn(g, u):
    return (g * jax.nn.sigmoid(g) * u,)


def _loss_call(x, tgt, gain):
    n, d = x.shape
    tb = min(256, n)

    def body(x_ref, t_ref, g_ref, l_ref, dx_ref, dg_ref):
        (y,), vjp = jax.vjp(lambda a, b: tuple(_rms_fn(a, b)), x_ref[...], g_ref[...])
        e = y - t_ref[...]
        part = 0.5 * jnp.sum(jnp.mean(e * e, axis=-1, keepdims=True))
        dx, dg = vjp((e * (1.0 / d),))
        dx_ref[...] = dx

        @pl.when(pl.program_id(0) == 0)
        def _():
            l_ref[...] = jnp.zeros_like(l_ref)
            dg_ref[...] = jnp.zeros_like(dg_ref)

        l_ref[...] += jnp.full(l_ref.shape, part, F32)
        dg_ref[...] += dg

    return pl.pallas_call(
        body, grid=(n // tb,),
        in_specs=[pl.BlockSpec((tb, d), lambda i: (i, 0)), pl.BlockSpec((tb, d), lambda i: (i, 0)), _full_spec(gain)],
        out_specs=[pl.BlockSpec((1, LANES), lambda i: (0, 0)), pl.BlockSpec((tb, d), lambda i: (i, 0)), _full_spec(gain)],
        out_shape=[_sds((1, LANES), F32), _sds((n, d), F32), _sds(gain.shape, F32)],
        compiler_params=_params(), name="loss_head")(x, tgt, gain)


FOX_BLOCK = 256
CONV_ROWS = 64
CONV_HALO = 32


def _chunk(i):
    return pl.ds(pl.multiple_of(i * CONV_ROWS, CONV_ROWS), CONV_ROWS)


def _delayed(ext, s):
    if s % SUBLANES == 0:
        return ext[CONV_HALO - s:CONV_HALO - s + CONV_ROWS]
    return pltpu.roll(ext, s, 0)[CONV_HALO:]


def _advanced(ext, s):
    if s % SUBLANES == 0:
        return ext[s:s + CONV_ROWS]
    return pltpu.roll(ext, ext.shape[0] - s, 0)[:CONV_ROWS]


def _conv_pass(t, taps, w_ref, lanes, get_u, emit, get_dy=None, dw_acc=None):
    def body(i, carry):
        rows = _chunk(i)
        u = get_u(rows)
        ext = jnp.concatenate([carry, u], axis=0)
        dy = None if get_dy is None else get_dy(rows)
        acc = None
        for k in range(taps):
            sh = _delayed(ext, taps - 1 - k)
            term = w_ref[k:k + 1, lanes] * sh
            acc = term if acc is None else acc + term
            if dy is not None:
                dw_acc[k] += jnp.sum((dy * sh).reshape(CONV_ROWS // SUBLANES, SUBLANES, LANES), axis=0)
        emit(rows, u, acc)
        return u[CONV_ROWS - CONV_HALO:]

    lax.fori_loop(0, t // CONV_ROWS, body, jnp.zeros((CONV_HALO, LANES), F32))


def _conv_pass_t(t, taps, w_ref, lanes, get_dy, emit):
    n = t // CONV_ROWS

    def body(j, carry):
        rows = _chunk(n - 1 - j)
        dy = get_dy(rows)
        ext = jnp.concatenate([dy, carry], axis=0)
        acc = None
        for k in range(taps):
            term = w_ref[k:k + 1, lanes] * _advanced(ext, taps - 1 - k)
            acc = term if acc is None else acc + term
        emit(rows, acc)
        return dy[:CONV_HALO]

    lax.fori_loop(0, n, body, jnp.zeros((CONV_HALO, LANES), F32))


def _chan_block(d):
    return min(256, d)


def _lane_groups(cb):
    return [slice(g * LANES, (g + 1) * LANES) for g in range(cb // LANES)]


def _conv_a_fwd(za, w):
    t, d = za.shape[0], za.shape[1] // 3
    taps, cb = w.shape[0], _chan_block(d)
    nb = d // cb
    assert taps - 1 <= CONV_HALO and t % CONV_ROWS == 0

    def body(ab_ref, ac_ref, au_ref, w_ref, o_ref):
        for lanes in _lane_groups(cb):
            def emit(rows, u, s, lanes=lanes):
                o_ref[rows, lanes] = (ab_ref[rows, lanes].astype(F32) * s).astype(o_ref.dtype)

            _conv_pass(t, taps, w_ref, lanes, lambda rows, lanes=lanes: ac_ref[rows, lanes].astype(F32) * au_ref[rows, lanes].astype(F32), emit)

    return pl.pallas_call(
        body, grid=(nb,),
        in_specs=[pl.BlockSpec((t, cb), lambda j, o=o: (0, o * nb + j)) for o in range(3)]
        + [pl.BlockSpec((taps, cb), lambda j: (0, j))],
        out_specs=pl.BlockSpec((t, cb), lambda j: (0, j)),
        out_shape=_sds((t, d), BF16), compiler_params=_params(), name="conv_a_fwd")(za, za, za, w)


def _finish_taps(dw_ref, dw_acc, lanes, taps):
    for k in range(taps):
        dw_ref[k:k + 1, lanes] = jnp.sum(dw_acc[k], axis=0, keepdims=True)


def _conv_a_bwd(za, w, dpa):
    t, d = za.shape[0], za.shape[1] // 3
    taps, cb = w.shape[0], _chan_block(d)
    nb = d // cb

    def body(ab_ref, ac_ref, au_ref, w_ref, dy_ref, dab_ref, dac_ref, dau_ref, dw_ref, dw_acc):
        for lanes in _lane_groups(cb):
            dw_acc[...] = jnp.zeros_like(dw_acc)

            def ds(rows, lanes=lanes):
                return dy_ref[rows, lanes].astype(F32) * ab_ref[rows, lanes].astype(F32)

            def emit_s(rows, u, s, lanes=lanes):
                dab_ref[rows, lanes] = (dy_ref[rows, lanes].astype(F32) * s).astype(dab_ref.dtype)

            def emit_du(rows, du, lanes=lanes):
                dac_ref[rows, lanes] = (du * au_ref[rows, lanes].astype(F32)).astype(dac_ref.dtype)
                dau_ref[rows, lanes] = (du * ac_ref[rows, lanes].astype(F32)).astype(dau_ref.dtype)

            _conv_pass(t, taps, w_ref, lanes, lambda rows, lanes=lanes: ac_ref[rows, lanes].astype(F32) * au_ref[rows, lanes].astype(F32), emit_s,
                       get_dy=ds, dw_acc=dw_acc)
            _conv_pass_t(t, taps, w_ref, lanes, ds, emit_du)
            _finish_taps(dw_ref, dw_acc, lanes, taps)

    blk = pl.BlockSpec((t, cb), lambda j: (0, j))
    return pl.pallas_call(
        body, grid=(nb,),
        in_specs=[pl.BlockSpec((t, cb), lambda j, o=o: (0, o * nb + j)) for o in range(3)]
        + [pl.BlockSpec((taps, cb), lambda j: (0, j)), blk],
        out_specs=[blk, blk, blk, pl.BlockSpec((taps, cb), lambda j: (0, j))],
        out_shape=[_sds((t, d), BF16)] * 3 + [_sds((taps, d), F32)],
        scratch_shapes=[pltpu.VMEM((taps, SUBLANES, LANES), F32)],
        compiler_params=_params(), name="conv_a_bwd")(za, za, za, w, dpa)


def _conv_c_fwd(zcg, w, bias):
    t, d = zcg.shape[0], w.shape[1]
    taps, cb = w.shape[0], _chan_block(d)
    nb = d // cb
    assert taps - 1 <= CONV_HALO and t % CONV_ROWS == 0

    def body(cv_ref, cg_ref, w_ref, b_ref, o_ref):
        for lanes in _lane_groups(cb):
            def emit(rows, u, s, lanes=lanes):
                o_ref[rows, lanes] = s + b_ref[:, lanes]

            _conv_pass(t, taps, w_ref, lanes,
                       lambda rows, lanes=lanes: cv_ref[rows, lanes].astype(F32) * jax.nn.sigmoid(cg_ref[rows, lanes].astype(F32)), emit)

    return pl.pallas_call(
        body, grid=(nb,),
        in_specs=[pl.BlockSpec((t, cb), lambda j, o=o: (0, o * nb + j)) for o in range(2)]
        + [pl.BlockSpec((taps, cb), lambda j: (0, j)), pl.BlockSpec((1, cb), lambda j: (0, j))],
        out_specs=pl.BlockSpec((t, cb), lambda j: (0, j)),
        out_shape=_sds((t, d), F32), compiler_params=_params(), name="conv_c_fwd")(zcg, zcg, w, bias)


def _conv_c_bwd(zcg, w, du2):
    t, d = zcg.shape[0], w.shape[1]
    taps, cb = w.shape[0], _chan_block(d)
    nb = d // cb

    def body(cv_ref, cg_ref, w_ref, dy_ref, dcv_ref, dcg_ref, dw_ref, db_ref, dw_acc, db_acc):
        for lanes in _lane_groups(cb):
            dw_acc[...] = jnp.zeros_like(dw_acc)
            db_acc[...] = jnp.zeros_like(db_acc)

            def dy(rows, lanes=lanes):
                return dy_ref[rows, lanes].astype(F32)

            def emit_s(rows, u, s, lanes=lanes):
                db_acc[...] += jnp.sum(dy_ref[rows, lanes].astype(F32).reshape(CONV_ROWS // SUBLANES, SUBLANES, LANES), axis=0)

            def emit_du(rows, du, lanes=lanes):
                cv, sg = cv_ref[rows, lanes].astype(F32), jax.nn.sigmoid(cg_ref[rows, lanes].astype(F32))
                dcv_ref[rows, lanes] = (du * sg).astype(dcv_ref.dtype)
                dcg_ref[rows, lanes] = (du * cv * sg * (1.0 - sg)).astype(dcg_ref.dtype)

            _conv_pass(t, taps, w_ref, lanes,
                       lambda rows, lanes=lanes: cv_ref[rows, lanes].astype(F32) * jax.nn.sigmoid(cg_ref[rows, lanes].astype(F32)), emit_s,
                       get_dy=dy, dw_acc=dw_acc)
            _conv_pass_t(t, taps, w_ref, lanes, dy, emit_du)
            _finish_taps(dw_ref, dw_acc, lanes, taps)
            db_ref[:, lanes] = jnp.sum(db_acc[...], axis=0, keepdims=True)

    blk = pl.BlockSpec((t, cb), lambda j: (0, j))
    return pl.pallas_call(
        body, grid=(nb,),
        in_specs=[pl.BlockSpec((t, cb), lambda j, o=o: (0, o * nb + j)) for o in range(2)]
        + [pl.BlockSpec((taps, cb), lambda j: (0, j)), blk],
        out_specs=[blk, blk, pl.BlockSpec((taps, cb), lambda j: (0, j)), pl.BlockSpec((1, cb), lambda j: (0, j))],
        out_shape=[_sds((t, d), BF16)] * 2 + [_sds((taps, d), F32), _sds((1, d), F32)],
        scratch_shapes=[pltpu.VMEM((taps, SUBLANES, LANES), F32), pltpu.VMEM((SUBLANES, LANES), F32)],
        compiler_params=_params(), name="conv_c_bwd")(zcg, zcg, w, du2)


def _tri(tb, t, i, lower):
    r = lax.broadcasted_iota(jnp.int32, (tb, t), 0) + i * tb
    c = lax.broadcasted_iota(jnp.int32, (tb, t), 1)
    return ((r >= c) if lower else (c >= r)).astype(F32)


def _fox_cumsum(zf, bf):
    t = zf.shape[0]
    tb = min(256, t)

    def body(z_ref, b_ref, c_ref):
        logf = jax.nn.log_sigmoid(z_ref[...] + b_ref[...])
        c_ref[...] = jnp.dot(_tri(tb, t, pl.program_id(0), True), logf, precision=lax.Precision.HIGHEST,
                             preferred_element_type=F32)

    return pl.pallas_call(
        body, grid=(t // tb,), in_specs=[_full_spec(zf), _full_spec(bf)],
        out_specs=pl.BlockSpec((tb, LANES), lambda i: (i, 0)), out_shape=_sds((t, LANES), F32),
        compiler_params=_params(), name="fox_cumsum")(zf, bf)


def _fox_cumsum_bwd(zf, bf, dc_t):
    t = zf.shape[0]
    tb = min(256, t)

    def body(z_ref, b_ref, dc_ref, dz_ref, db_ref):
        i = pl.program_id(0)

        @pl.when(i == 0)
        def _():
            db_ref[...] = jnp.zeros_like(db_ref)

        dlogf = jnp.dot(_tri(tb, t, i, False), dc_ref[...], precision=lax.Precision.HIGHEST, preferred_element_type=F32)
        dz = dlogf * jax.nn.sigmoid(-(z_ref[...] + b_ref[...]))
        dz_ref[...] = dz.astype(dz_ref.dtype)
        db_ref[...] += jnp.sum(dz, axis=0, keepdims=True)

    return pl.pallas_call(
        body, grid=(t // tb,),
        in_specs=[pl.BlockSpec((tb, LANES), lambda i: (i, 0)), _full_spec(bf), _full_spec(dc_t)],
        out_specs=[pl.BlockSpec((tb, LANES), lambda i: (i, 0)), pl.BlockSpec((1, LANES), lambda i: (0, 0))],
        out_shape=[_sds((t, LANES), BF16), _sds((1, LANES), F32)],
        compiler_params=_params(), name="fox_cumsum_bwd")(zf, bf, dc_t)


def _fox_block(q, k, v, cr, *, q0, hd):
    tq, kext = q.shape[0], k.shape[0]
    nh = LANES // hd
    lane = lax.broadcasted_iota(jnp.int32, (1, LANES), 1)
    causal = (lax.broadcasted_iota(jnp.int32, (tq, kext), 0) + q0) >= lax.broadcasted_iota(jnp.int32, (tq, kext), 1)
    kb, vb = k.astype(BF16), v.astype(BF16)
    out = jnp.zeros((tq, LANES), F32)
    for h in range(nh):
        m = (lane // hd) == h
        qh = jnp.where(m, q * (hd ** -0.5), 0.0).astype(BF16)
        s = lax.dot_general(qh, kb, (((1,), (1,)), ((), ())), preferred_element_type=F32)
        c_k = cr[h] - lax.stop_gradient(cr[h][:, q0:q0 + 1])
        s = jnp.where(causal, s - c_k, -1e30)
        e = jnp.exp(s - lax.stop_gradient(jnp.max(s, axis=1, keepdims=True)))
        prob = e / jnp.sum(e, axis=1, keepdims=True)
        o = jnp.dot(prob.astype(BF16), vb, preferred_element_type=F32)
        out = out + jnp.where(m, o, 0.0)
    return out


def _fox_dims(zqkv, crow):
    t, d = zqkv.shape[0], zqkv.shape[1] // 3
    hd = d // crow.shape[0]
    assert math.log2(hd) % 2 == 0
    return t, d, hd, d // LANES, LANES // hd, min(FOX_BLOCK, t)


def _fox_specs(i, tq, nblk, nh):
    kext = (i + 1) * tq
    return [pl.BlockSpec((tq, LANES), lambda p: (i, p)), pl.BlockSpec((kext, LANES), lambda p: (0, nblk + p)),
            pl.BlockSpec((kext, LANES), lambda p: (0, 2 * nblk + p)), pl.BlockSpec((nh, 1, kext), lambda p: (p, 0, 0))]


def _fox_fwd(zqkv, crow):
    t, d, hd, nblk, nh, tq = _fox_dims(zqkv, crow)
    o = None
    for i in range(t // tq):
        def body(q_ref, k_ref, v_ref, cr_ref, *rest, i=i):
            o_ref = rest[-1]
            o_ref[...] = _fox_block(q_ref[...].astype(F32), k_ref[...], v_ref[...], cr_ref[...],
                                    q0=i * tq, hd=hd).astype(o_ref.dtype)

        prev = [] if o is None else [o]
        o = pl.pallas_call(
            body, grid=(nblk,),
            in_specs=_fox_specs(i, tq, nblk, nh) + [pl.BlockSpec(memory_space=pl.ANY)] * len(prev),
            out_specs=pl.BlockSpec((tq, LANES), lambda p, i=i: (i, p)), out_shape=_sds((t, d), BF16),
            input_output_aliases={4: 0} if prev else {},
            compiler_params=_params(), name=f"fox_fwd_q{i}")(zqkv, zqkv, zqkv, crow, *prev)
    return o


def _fox_bwd(zqkv, crow, do):
    t, d, hd, nblk, nh, tq = _fox_dims(zqkv, crow)
    nq = t // tq
    acc = None
    for i in reversed(range(nq)):
        kext = (i + 1) * tq
        first = acc is None

        def body(q_ref, k_ref, v_ref, cr_ref, do_ref, *rest, first=first, i=i):
            if first:
                dq_ref, dk_ref, dv_ref, dcr_ref = rest
            else:
                dk_in, dv_in, dcr_in = rest[1:4]
                dq_ref, dk_ref, dv_ref, dcr_ref = rest[4:]
            f = functools.partial(_fox_block, q0=i * tq, hd=hd)
            _, vjp = jax.vjp(f, q_ref[...].astype(F32), k_ref[...].astype(F32), v_ref[...].astype(F32), cr_ref[...])
            dq, dk, dv, dcr = vjp(do_ref[...].astype(F32))
            dq_ref[...] = dq.astype(dq_ref.dtype)
            if first:
                dk_ref[...], dv_ref[...], dcr_ref[...] = dk, dv, dcr
            else:
                dk_ref[...] = dk_in[...] + dk
                dv_ref[...] = dv_in[...] + dv
                dcr_ref[...] = dcr_in[...] + dcr

        qblk = pl.BlockSpec((tq, LANES), lambda p, i=i: (i, p))
        kblk = pl.BlockSpec((kext, LANES), lambda p: (0, p))
        rblk = pl.BlockSpec((nh, 1, kext), lambda p: (p, 0, 0))
        if first:
            prev, prev_specs, aliases = [], [], {}
        else:
            prev, prev_specs = list(acc), [pl.BlockSpec(memory_space=pl.ANY), kblk, kblk, rblk]
            aliases = {5: 0, 6: 1, 7: 2, 8: 3}
        acc = pl.pallas_call(
            body, grid=(nblk,),
            in_specs=_fox_specs(i, tq, nblk, nh) + [qblk] + prev_specs,
            out_specs=[qblk, kblk, kblk, rblk],
            out_shape=[_sds((t, d), BF16), _sds((t, d), F32), _sds((t, d), F32), _sds(crow.shape, F32)],
            input_output_aliases=aliases,
            compiler_params=_params(), name=f"fox_bwd_q{i}")(zqkv, zqkv, zqkv, crow, do, *prev)
    return acc


def _xattn_block(q, k, v):
    s = lax.dot_general(q.astype(BF16), k.astype(BF16), (((1,), (1,)), ((), ())), preferred_element_type=F32)
    s = s * (q.shape[1] ** -0.5)
    e = jnp.exp(s - lax.stop_gradient(jnp.max(s, axis=1, keepdims=True)))
    prob = e / jnp.sum(e, axis=1, keepdims=True)
    return jnp.dot(prob.astype(BF16), v.astype(BF16), preferred_element_type=F32)


def _xattn_fwd(qx, kv):
    t, dx = qx.shape
    nm, nh, tq = kv.shape[0], qx.shape[1] // LANES, min(512, qx.shape[0])

    def body(q_ref, k_ref, v_ref, o_ref):
        o_ref[...] = _xattn_block(q_ref[...], k_ref[...], v_ref[...]).astype(o_ref.dtype)

    return pl.pallas_call(
        body, grid=(nh, t // tq),
        in_specs=[pl.BlockSpec((tq, LANES), lambda h, i: (i, h)), pl.BlockSpec((nm, LANES), lambda h, i: (0, h)),
                  pl.BlockSpec((nm, LANES), lambda h, i: (0, nh + h))],
        out_specs=pl.BlockSpec((tq, LANES), lambda h, i: (i, h)), out_shape=_sds((t, dx), BF16),
        compiler_params=_params(), name="xattn_fwd")(qx, kv, kv)


def _xattn_bwd(qx, kv, do):
    t, dx = qx.shape
    nm, nh, tq = kv.shape[0], qx.shape[1] // LANES, min(512, qx.shape[0])

    def body(q_ref, k_ref, v_ref, do_ref, dq_ref, dk_ref, dv_ref):
        _, vjp = jax.vjp(_xattn_block, q_ref[...].astype(F32), k_ref[...].astype(F32), v_ref[...].astype(F32))
        dq, dk, dv = vjp(do_ref[...].astype(F32))
        dq_ref[...] = dq.astype(dq_ref.dtype)

        @pl.when(pl.program_id(1) == 0)
        def _():
            dk_ref[...] = jnp.zeros_like(dk_ref)
            dv_ref[...] = jnp.zeros_like(dv_ref)

        dk_ref[...] += dk
        dv_ref[...] += dv

    qblk = pl.BlockSpec((tq, LANES), lambda h, i: (i, h))
    kblk = pl.BlockSpec((nm, LANES), lambda h, i: (0, h))
    return pl.pallas_call(
        body, grid=(nh, t // tq),
        in_specs=[qblk, kblk, pl.BlockSpec((nm, LANES), lambda h, i: (0, nh + h)), qblk],
        out_specs=[qblk, kblk, kblk],
        out_shape=[_sds((t, dx), BF16), _sds((nm, dx), F32), _sds((nm, dx), F32)],
        compiler_params=_params(), name="xattn_bwd")(qx, kv, kv, do)


def _row_tile(rows, cols, budget=1 << 20, step=2 * SUBLANES):
    best = None
    for tb in range(step, rows + 1, step):
        if rows % tb == 0 and tb * cols * 4 <= budget:
            best = tb
    return rows if best is None else best


def _adamw(w, g, m, v, keep_rank=False):
    shape = w.shape
    cols = shape[-1]
    if keep_rank:
        w2, g2, m2, v2 = w, g, m, v
        rows, tail = shape[0], shape[1:]
        tb = _row_tile(rows, math.prod(tail), step=1)
    else:
        w2, g2, m2, v2 = (a.reshape(-1, cols) for a in (w, g, m, v))
        rows, tail = w2.shape[0], (cols,)
        tb = _row_tile(rows, cols)

    def body(w_ref, g_ref, m_ref, v_ref, d_ref, nm_ref, nv_ref):
        gg = g_ref[...]
        nm = ADAM_B1 * m_ref[...] + (1.0 - ADAM_B1) * gg
        nv = ADAM_B2 * v_ref[...] + (1.0 - ADAM_B2) * (gg * gg)
        m_hat = nm / (1.0 - ADAM_B1 ** ADAM_STEP)
        v_hat = nv / (1.0 - ADAM_B2 ** ADAM_STEP)
        d_ref[...] = -ADAM_LR * (m_hat / (jnp.sqrt(v_hat) + ADAM_EPS) + ADAM_WD * w_ref[...])
        nm_ref[...] = nm
        nv_ref[...] = nv

    blk = pl.BlockSpec((tb,) + tail, lambda i: (i,) + (0,) * len(tail))
    out = pl.pallas_call(
        body, grid=(rows // tb,), in_specs=[blk] * 4, out_specs=[blk] * 3, out_shape=[_sds(w2.shape, F32)] * 3,
        compiler_params=_params(), name="adamw")(w2, g2, m2, v2)
    return tuple(o.reshape(shape) for o in out)


def _sum_devices(gathered, rows):
    cols = gathered.shape[1]

    def body(g_ref, o_ref):
        acc = g_ref[0:rows, :]
        for dev in range(1, N_DEV):
            acc = acc + g_ref[dev * rows:(dev + 1) * rows, :]
        o_ref[...] = acc

    return pl.pallas_call(body, out_shape=_sds((rows, cols), F32), compiler_params=_params(), name="sum_devices")(gathered)


def _place():
    x, y, c = lax.axis_index("x"), lax.axis_index("y"), lax.axis_index("c")
    chips = [(1 - x, y), (x, 1 - y), (1 - x, 1 - y)]
    return x, y, c, chips


def _chip_id(chip):
    return 2 * chip[0] + chip[1]


def _half(c, rows):
    rh = rows // 2
    return pl.ds(pl.multiple_of(c * rh, 16), rh)


_HBM = pl.BlockSpec(memory_space=pltpu.HBM)


def _allgather8(x_shard):
    m_per, n = x_shard.shape

    def body(x_ref, out_ref, send_sems, recv_sems, local_sem):
        x, y, c, chips = _place()
        me, sibling = (x, y, c), (x, y, 1 - c)

        def rows(px, py, pc):
            return out_ref.at[pl.ds((4 * px + 2 * py + pc) * m_per, m_per), :]

        def copy(k, block, to, src=None):
            return pltpu.make_async_remote_copy(
                src_ref=rows(*block) if src is None else src, dst_ref=rows(*block),
                send_sem=send_sems.at[k], recv_sem=recv_sems.at[k], device_id=to, device_id_type=MESH)

        mine = pltpu.make_async_copy(x_ref, rows(*me), local_sem)
        mine.start()
        first = [copy(0, me, sibling, src=x_ref)]
        first += [copy(1 + j, me, (*chip, c), src=x_ref) for j, chip in enumerate(chips)]
        for cp in first:
            cp.start()
        passed = [copy(4 + j, (*chip, c), sibling) for j, chip in enumerate(chips)]
        for j, chip in enumerate(chips):
            copy(1 + j, (*chip, c), me).wait_recv()
            passed[j].start()
        copy(0, sibling, me).wait_recv()
        for j, chip in enumerate(chips):
            copy(4 + j, (*chip, 1 - c), me).wait_recv()
        for cp in first + passed:
            cp.wait_send()
        mine.wait()

    return pl.pallas_call(
        body, out_shape=_sds((N_DEV * m_per, n), x_shard.dtype),
        in_specs=[pl.BlockSpec(memory_space=pltpu.VMEM)], out_specs=pl.BlockSpec(memory_space=pltpu.VMEM),
        scratch_shapes=[pltpu.SemaphoreType.DMA((7,)), pltpu.SemaphoreType.DMA((7,)), pltpu.SemaphoreType.DMA],
        compiler_params=_params(), name="allgather8")(x_shard)


def _gather_weights(ws, layer, after):
    n = len(ws)

    def body(*refs):
        w, out = refs[:n], refs[n + 1:2 * n + 1]
        send_i, recv_i, send_d, recv_d, send_o, recv_o = refs[2 * n + 1:]
        x, y, c, chips = _place()
        sibling = (x, y, 1 - c)
        me_chip = 2 * x + y

        def slab(i, chip, half):
            return out[i].at[chip, _half(half, w[i].shape[1]), :]

        def over_ici(i, j, src_chip, to, src=None):
            return pltpu.make_async_remote_copy(
                src_ref=slab(i, src_chip, c) if src is None else src, dst_ref=slab(i, src_chip, c),
                send_sem=send_i.at[3 * i + j], recv_sem=recv_i.at[3 * i + j], device_id=to, device_id_type=MESH)

        def over_d2d(i, j, half):
            return pltpu.make_async_remote_copy(
                src_ref=slab(i, _chip_id(chips[j]), half), dst_ref=slab(i, _chip_id(chips[j]), half),
                send_sem=send_d.at[3 * i + j], recv_sem=recv_d.at[3 * i + j], device_id=sibling, device_id_type=MESH)

        sent = []
        own = [pltpu.make_async_remote_copy(
            src_ref=w[i].at[layer], dst_ref=out[i].at[me_chip], send_sem=send_o.at[i], recv_sem=recv_o.at[i],
            device_id=sibling, device_id_type=MESH) for i in range(n)]
        for cp in own:
            cp.start()
        for i in range(n):
            for j, chip in enumerate(chips):
                cp = over_ici(i, j, me_chip, (*chip, c), src=w[i].at[layer, _half(c, w[i].shape[1]), :])
                cp.start()
                sent.append(cp)
        for i in range(n):
            for j, chip in enumerate(chips):
                over_ici(i, j, _chip_id(chip), (x, y, c)).wait_recv()
                cp = over_d2d(i, j, c)
                cp.start()
                sent.append(cp)
        for i in range(n):
            for j in range(3):
                over_d2d(i, j, 1 - c).wait_recv()
        for cp in sent:
            cp.wait_send()
        for cp in own:
            cp.wait()

    return pl.pallas_call(
        body, in_specs=[_HBM] * n + [pl.BlockSpec(memory_space=pl.ANY)], out_specs=[_HBM] * n,
        out_shape=[_sds((N_CHIPS,) + a.shape[1:], a.dtype) for a in ws],
        scratch_shapes=[pltpu.SemaphoreType.DMA((3 * n,)) for _ in range(4)] + [pltpu.SemaphoreType.DMA((n,))] * 2,
        compiler_params=_params(), name="gather_weights")(*ws, after)


_SEM = pl.BlockSpec(memory_space=pltpu.SEMAPHORE)
_DATAFLOW = pltpu.SideEffectType.DATAFLOW_SIDE_EFFECTING


def _gather_copies(w, land, send, recv, layer):
    x, y, c, chips = _place()
    targets = [(*chip, c) for chip in chips] + [(x, y, 1 - c)]
    return [pltpu.make_async_remote_copy(
        src_ref=w[i].at[layer], dst_ref=land[i].at[2 * x + y], send_sem=send.at[4 * i + j], recv_sem=recv.at[4 * i + j],
        device_id=to, device_id_type=MESH) for i in range(len(w)) for j, to in enumerate(targets)]


def _gather_start(ws, layer, after):
    n = len(ws)

    def body(*refs):
        w, land = refs[:n], refs[n:2 * n]
        send, recv = refs[2 * n + 1], refs[2 * n + 2]
        for cp in _gather_copies(w, land, send, recv, layer):
            cp.start()
        refs[-1][...] = jnp.zeros_like(refs[-1])

    lands = [pltpu.with_memory_space_constraint(lax.empty((N_CHIPS,) + a.shape[1:], a.dtype), pltpu.HBM) for a in ws]
    out = pl.pallas_call(
        body, name=f"gather_start_{layer}",
        out_shape=(pltpu.SemaphoreType.DMA((4 * n,)), pltpu.SemaphoreType.DMA((4 * n,)),
                   *[pltpu.HBM(a.shape, a.dtype) for a in lands], _sds((SUBLANES, LANES), F32)),
        in_specs=[_HBM] * (2 * n) + [pl.BlockSpec(memory_space=pl.ANY)],
        out_specs=(_SEM, _SEM, *[_HBM] * n, pl.BlockSpec(memory_space=pltpu.VMEM)),
        input_output_aliases={n + i: 2 + i for i in range(n)},
        compiler_params=pltpu.CompilerParams(has_side_effects=_DATAFLOW))(*ws, *lands, after)
    return out[0], out[1], list(out[2:2 + n]), out[-1]


def _gather_wait(send, recv, ws, lands, after, layer):
    n = len(ws)

    def body(*refs):
        for cp in _gather_copies(refs[:n], refs[n:2 * n], refs[2 * n], refs[2 * n + 1], layer):
            cp.wait_send()
            cp.wait_recv()

    out = pl.pallas_call(
        body, name=f"gather_wait_{layer}",
        out_shape=tuple(pltpu.HBM(a.shape, a.dtype) for a in lands),
        in_specs=[_HBM] * (2 * n) + [_SEM, _SEM, pl.BlockSpec(memory_space=pl.ANY)], out_specs=tuple([_HBM] * n),
        input_output_aliases={n + i: i for i in range(n)},
        compiler_params=pltpu.CompilerParams(has_side_effects=_DATAFLOW))(*ws, *lands, send, recv, after)
    return list(out)


def _swap_halves(gs):
    n = len(gs)

    def body(*refs):
        g, out = refs[:n], refs[n:2 * n]
        send, recv = refs[2 * n:]
        x, y, c, _ = _place()
        cps = [pltpu.make_async_remote_copy(
            src_ref=g[i].at[:, :, _half(1 - c, g[i].shape[2]), :], dst_ref=out[i],
            send_sem=send.at[i], recv_sem=recv.at[i], device_id=(x, y, 1 - c), device_id_type=MESH) for i in range(n)]
        for cp in cps:
            cp.start()
        for cp in cps:
            cp.wait()

    return pl.pallas_call(
        body, in_specs=[_HBM] * n, out_specs=[_HBM] * n,
        out_shape=[_sds(a.shape[:2] + (a.shape[2] // 2, a.shape[3]), a.dtype) for a in gs],
        scratch_shapes=[pltpu.SemaphoreType.DMA((n,)), pltpu.SemaphoreType.DMA((n,))],
        compiler_params=_params(), name="swap_halves")(*gs)


def _exchange_chips(ss):
    n = len(ss)

    def body(*refs):
        s, out = refs[:n], refs[n:2 * n]
        send, recv = refs[2 * n:]
        x, y, c, chips = _place()
        cps = []
        for i in range(n):
            for j, chip in enumerate(chips):
                cps.append(pltpu.make_async_remote_copy(
                    src_ref=s[i].at[:, _chip_id(chip)], dst_ref=out[i].at[j],
                    send_sem=send.at[3 * i + j], recv_sem=recv.at[3 * i + j], device_id=(*chip, c), device_id_type=MESH))
        for cp in cps:
            cp.start()
        for cp in cps:
            cp.wait()

    return pl.pallas_call(
        body, in_specs=[_HBM] * n, out_specs=[_HBM] * n,
        out_shape=[_sds((3, a.shape[0]) + a.shape[2:], a.dtype) for a in ss],
        scratch_shapes=[pltpu.SemaphoreType.DMA((3 * n,)), pltpu.SemaphoreType.DMA((3 * n,))],
        compiler_params=_params(), name="exchange_chips")(*ss)


def _exchange_copies(s, land, send, recv):
    x, y, c, chips = _place()
    return [pltpu.make_async_remote_copy(
        src_ref=s[i].at[:, _chip_id(chip)], dst_ref=land[i].at[j], send_sem=send.at[3 * i + j], recv_sem=recv.at[3 * i + j],
        device_id=(*chip, c), device_id_type=MESH) for i in range(len(s)) for j, chip in enumerate(chips)]


def _exchange_start(ss, layer):
    n = len(ss)

    def body(*refs):
        s, land = refs[:n], refs[n:2 * n]
        send, recv = refs[2 * n], refs[2 * n + 1]
        token = refs[-1]
        for cp in _exchange_copies(s, land, send, recv):
            cp.start()
        token[...] = jnp.zeros_like(token)

    lands = [lax.empty((3, a.shape[0]) + a.shape[2:], a.dtype) for a in ss]
    operands = [pltpu.with_memory_space_constraint(a, pltpu.HBM) for a in list(ss) + lands]
    out = pl.pallas_call(
        body, name=f"exchange_start_{layer}",
        out_shape=(pltpu.SemaphoreType.DMA((3 * n,)), pltpu.SemaphoreType.DMA((3 * n,)),
                   *[pltpu.HBM(a.shape, a.dtype) for a in operands], _sds((SUBLANES, LANES), F32)),
        in_specs=[_HBM] * (2 * n), out_specs=(_SEM, _SEM, *[_HBM] * (2 * n), pl.BlockSpec(memory_space=pltpu.VMEM)),
        input_output_aliases={i: 2 + i for i in range(2 * n)},
        compiler_params=pltpu.CompilerParams(has_side_effects=_DATAFLOW))(*operands)
    return out[0], out[1], list(out[2:2 + n]), list(out[2 + n:2 + 2 * n]), out[-1]


def _exchange_wait(send, recv, ss, lands, after, layer):
    n = len(ss)

    def body(*refs):
        s, land = refs[:n], refs[n:2 * n]
        for cp in _exchange_copies(s, land, refs[2 * n], refs[2 * n + 1]):
            cp.wait_send()
            cp.wait_recv()

    out = pl.pallas_call(
        body, name=f"exchange_wait_{layer}",
        out_shape=tuple(pltpu.HBM(a.shape, a.dtype) for a in list(ss) + list(lands)),
        in_specs=[_HBM] * (2 * n) + [_SEM, _SEM, pl.BlockSpec(memory_space=pl.ANY)], out_specs=tuple([_HBM] * (2 * n)),
        input_output_aliases={i: i for i in range(2 * n)},
        compiler_params=pltpu.CompilerParams(has_side_effects=_DATAFLOW))(*ss, *lands, send, recv, after)
    return list(out[:n]), list(out[n:])


def _join_halves(fs):
    n = len(fs)

    def body(*refs):
        f, out = refs[:n], refs[n:2 * n]
        send, recv = refs[2 * n:]
        x, y, c, _ = _place()

        def half(ref, which):
            return ref.at[:, _half(which, ref.shape[1]), :]

        for i in range(n):
            pltpu.make_async_remote_copy(
                src_ref=half(f[i], c), dst_ref=half(out[i], c), send_sem=send.at[i], recv_sem=recv.at[i],
                device_id=(x, y, 1 - c), device_id_type=MESH).start()
        for i in range(n):
            arrive = pltpu.make_async_remote_copy(
                src_ref=half(f[i], c), dst_ref=half(out[i], 1 - c), send_sem=send.at[i], recv_sem=recv.at[i],
                device_id=(x, y, 1 - c), device_id_type=MESH)
            arrive.wait_recv()
            arrive.wait_send()

    return pl.pallas_call(
        body, in_specs=[_HBM] * n, out_specs=[_HBM] * n, out_shape=[_sds(a.shape, a.dtype) for a in fs],
        input_output_aliases={i: i for i in range(n)},
        scratch_shapes=[pltpu.SemaphoreType.DMA((n,)), pltpu.SemaphoreType.DMA((n,))],
        compiler_params=_params(), name="join_halves")(*fs)


def _add_sibling(g, got, c_idx):
    nl, ns, r, cols = g.shape
    rh = r // 2
    tb = _row_tile(rh, cols, budget=ADD_BLOCK_BYTES)
    nb = rh // tb
    g3, got3 = g.reshape(nl * ns, r, cols), got.reshape(nl * ns, rh, cols)

    def body(c_ref, a_ref, b_ref, o_ref):
        o_ref[...] = (a_ref[...].astype(F32) + b_ref[...].astype(F32)).astype(o_ref.dtype)

    out = pl.pallas_call(
        body,
        grid_spec=pltpu.PrefetchScalarGridSpec(
            num_scalar_prefetch=1, grid=(nl * ns, nb),
            in_specs=[pl.BlockSpec((None, tb, cols), lambda s, i, c_ref: (s, c_ref[0] * nb + i, 0)),
                      pl.BlockSpec((None, tb, cols), lambda s, i, c_ref: (s, i, 0))],
            out_specs=pl.BlockSpec((None, tb, cols), lambda s, i, c_ref: (s, i, 0))),
        out_shape=_sds((nl * ns, rh, cols), BF16), compiler_params=_params(), name="add_sibling")(c_idx, g3, got3)
    return out.reshape(nl, ns, rh, cols)


def _add_chips(s, got, chip_idx, c_idx, layer, depth, prev):
    _, ns, rh, cols = s.shape
    tb = _row_tile(rh, cols, budget=ADD_BLOCK_BYTES)
    nb = rh // tb

    def body(k_ref, c_ref, a_ref, b0_ref, b1_ref, b2_ref, *rest):
        o_ref = rest[-1]
        o_ref[...] = ((a_ref[...].astype(F32) + b0_ref[...].astype(F32)) + b1_ref[...].astype(F32)) + b2_ref[...].astype(F32)

    extra = [] if prev is None else [prev]
    return pl.pallas_call(
        body,
        grid_spec=pltpu.PrefetchScalarGridSpec(
            num_scalar_prefetch=2, grid=(nb,),
            in_specs=[pl.BlockSpec((None, None, tb, cols), lambda i, k_ref, c_ref: (0, k_ref[0], i, 0))]
            + [pl.BlockSpec((None, None, tb, cols), lambda i, k_ref, c_ref, j=j: (j, 0, i, 0)) for j in range(3)]
            + [pl.BlockSpec(memory_space=pl.ANY)] * len(extra),
            out_specs=pl.BlockSpec((None, tb, cols), lambda i, k_ref, c_ref: (layer, c_ref[0] * nb + i, 0))),
        out_shape=_sds((depth, 2 * rh, cols), F32), input_output_aliases={6: 0} if extra else {},
        compiler_params=_params(), name="add_chips")(chip_idx, c_idx, s, got, got, got, *extra)


def _in_pieces(d, heads, ws):
    groups, start = [], 0
    for k in range(12):
        if k == 6:
            groups.append(("f", 0, start, heads))
            start += heads
        else:
            groups.append(("main", (k - (k > 6)) * d, start, d))
            start += d
    pieces = []
    for chip in range(N_CHIPS):
        lo, hi = chip * ws, (chip + 1) * ws
        for dest, dcol, gstart, w in groups:
            a, b = max(lo, gstart), min(hi, gstart + w)
            if a < b:
                pieces.append((chip, a - lo, dest, dcol + a - gstart, b - a))
    return pieces


def _w_in_aligned(gathered, d, heads):
    rows, ws = gathered.shape[1], gathered.shape[2]
    tb = min(256, rows)
    pieces = _in_pieces(d, heads, ws)

    def body(g_ref, m_ref, f_ref):
        f_ref[...] = jnp.zeros_like(f_ref)
        for chip, sc, dest, dc, w in pieces:
            (m_ref if dest == "main" else f_ref)[:, dc:dc + w] = g_ref[chip, :, sc:sc + w]

    return pl.pallas_call(
        body, grid=(rows // tb,), in_specs=[pl.BlockSpec((N_CHIPS, tb, ws), lambda i: (0, i, 0))],
        out_specs=[pl.BlockSpec((tb, 11 * d), lambda i: (i, 0)), pl.BlockSpec((tb, LANES), lambda i: (i, 0))],
        out_shape=[_sds((rows, 11 * d), gathered.dtype), _sds((rows, LANES), gathered.dtype)],
        compiler_params=_params(), name="w_in_aligned")(gathered)


def _w_in_shards(gms, gf, prev, l, shape, heads):
    lo, hi = gms
    rows, d = lo.shape[0], lo.shape[1] // 6
    ws = shape[3]
    tb = min(256, rows)
    pieces = _in_pieces(d, heads, ws)

    def body(lo_ref, hi_ref, f_ref, *rest):
        o_ref = rest[-1]
        for chip, sc, dest, dc, w in pieces:
            if dest == "f":
                src = f_ref[:, dc:dc + w]
            elif dc < 6 * d:
                src = lo_ref[:, dc:dc + w]
            else:
                src = hi_ref[:, dc - 6 * d:dc - 6 * d + w]
            o_ref[chip, :, sc:sc + w] = src

    extra = [] if prev is None else [prev]
    return pl.pallas_call(
        body, grid=(rows // tb,),
        in_specs=[pl.BlockSpec((tb, 6 * d), lambda i: (i, 0)), pl.BlockSpec((tb, 5 * d), lambda i: (i, 0)),
                  pl.BlockSpec((tb, LANES), lambda i: (i, 0))] + [pl.BlockSpec(memory_space=pl.ANY)] * len(extra),
        out_specs=pl.BlockSpec((None, N_CHIPS, tb, ws), lambda i: (l, 0, i, 0)), out_shape=_sds(shape, lo.dtype),
        input_output_aliases={3: 0} if extra else {},
        compiler_params=_params(), name="w_in_shards")(lo, hi, gf, *extra)


def _pad_lanes(a, width=LANES):
    return jnp.pad(a, ((0, 0), (0, width - a.shape[1])))


def _cols(operand):
    return sum(p[2] for p in operand) if isinstance(operand, list) else operand.shape[1]


def _local_step(x, mem, tgt, sp, wt, dests=None, on_layer=None):
    t, d = x.shape
    depth = sp["mix_norm"].shape[0]
    heads = sp["b_forget"].shape[1]
    row = lambda a, l: a[l:l + 1]
    weights = wt if callable(wt) else (lambda l, x_in: {k: v[l] for k, v in wt.items()})
    layer_weights = []

    memn, = _rows("rms_mem", _rms_fn, [_part(mem)], [sp["mem_norm"][None]], [(d, BF16)])
    saved = []
    for l in range(depth):
        wl = weights(l, x)
        layer_weights.append(wl)
        f = _cols(wl["gu"]) // 2
        sv = {"x0": x}
        h1, = _rows("rms_mix", _rms_fn, [_part(x)], [row(sp["mix_norm"], l)], [(d, BF16)])
        wm = wl["main"]
        za = _mm("mm_in_a", h1, [_part(wm, 0, 3 * d)], out_dtype=BF16)
        zqkv = _mm("mm_in_qkv", h1, [_part(wm, 3 * d, 3 * d)], out_dtype=BF16)
        zcg = _mm("mm_in_cg", h1, [_part(wm, 6 * d, 5 * d)], out_dtype=BF16)
        zf = _mm("mm_in_f", h1, wl["f"])
        bf = _pad_lanes(row(sp["b_forget"], l))
        c = _fox_cumsum(zf, bf)
        crow = c[:, :heads].T.reshape(heads, 1, t)
        o = _fox_fwd(zqkv, crow)
        pa = _conv_a_fwd(za, sp["conv_a"][l])
        u2 = _conv_c_fwd(zcg, sp["conv_c"][l], row(sp["conv_c_bias"], l))
        u3, = _rows("ln_silu", _lnsilu_fn, [_part(u2)], [row(sp["ln_c_gain"], l), row(sp["ln_c_bias"], l)], [(d, BF16)])
        ya = _mm("mm_out_a", pa, wl["out_a"])
        yb = _mm("mm_out_b", o, wl["out_b"])
        yc = _mm("mm_out_c", u3, wl["out_c"])
        bg = [sp["b_gate"][l:l + 1, k * d:(k + 1) * d] for k in range(3)]
        gate_rows = [_part(zcg, (2 + k) * d, d) for k in range(3)] + [_part(ya), _part(yb), _part(yc)]
        mg, = _rows("merge", _merge_fn, gate_rows, bg, [(d, BF16)])
        x1 = _mm("mm_o", mg, wl["o"], add=x)
        h2, = _rows("rms_x", _rms_fn, [_part(x1)], [row(sp["xattn_norm"], l)], [(d, BF16)])
        qx = _mm("mm_xq", h2, wl["xq"], out_dtype=BF16)
        kv = _mm("mm_xkv", memn, wl["xkv"], out_dtype=BF16)
        ox = _xattn_fwd(qx, kv)
        x2 = _mm("mm_xo", ox, wl["xo"], add=x1)
        h3, = _rows("rms_ffn", _rms_fn, [_part(x2)], [row(sp["ffn_norm"], l)], [(d, BF16)])
        gu = _mm("mm_gu", h3, wl["gu"], out_dtype=BF16)
        act, = _rows("swiglu", _swiglu_fn, [_part(gu, 0, f), _part(gu, f, f)], [], [(f, BF16)])
        x3 = _mm("mm_down", act, wl["down"], add=x2)
        sv.update(h1=h1, za=za, zqkv=zqkv, zcg=zcg, zf=zf, bf=bf, crow=crow, o=o, pa=pa, u2=u2, u3=u3, ya=ya, yb=yb,
                  yc=yc, bg=bg, gate_rows=gate_rows, mg=mg, x1=x1, h2=h2, qx=qx, kv=kv, ox=ox, x2=x2, h3=h3, gu=gu, act=act)
        saved.append(sv)
        x = x3

    loss_row, dx, d_final = _loss_call(x, tgt, sp["final_norm"][None])

    gs = {k: [None] * depth for k in ("mix_norm", "b_gate", "b_forget", "conv_a", "conv_c", "conv_c_bias", "ln_c_gain",
                                      "ln_c_bias", "xattn_norm", "ffn_norm")}
    gw = {k: [None] * depth for k in ("main", "f", "out_a", "out_b", "out_c", "o", "xq", "xkv", "xo", "gu", "down")}
    def weight_grad(key, l, name, a_op, b_op):
        if dests is None or key not in dests:
            gw[key][l] = _mm(name, a_op, b_op, ta=True, out_dtype=BF16)
        else:
            shape, shards = dests[key]
            gw[key][l] = _mm(name, a_op, b_op, ta=True, out_dtype=BF16, dest=(shape, None, (0,), shards))

    dmemn = None
    for l in reversed(range(depth)):
        sv, wl = saved[l], layer_weights[l]
        dact = _mm("mm_down_dx", dx, wl["down"], tb=True)
        weight_grad("down", l, "mm_down_dw", sv["act"], dx)
        (dg, du), _ = _rows_vjp("swiglu_bwd", _swiglu_fn, [_part(sv["gu"], 0, f), _part(sv["gu"], f, f)], [], [_part(dact)],
                                [BF16, BF16], [])
        dgu = [_part(dg), _part(du)]
        dh3 = _mm("mm_gu_dx", dgu, wl["gu"], tb=True)
        weight_grad("gu", l, "mm_gu_dw", sv["h3"], dgu)
        (dx2,), (gs["ffn_norm"][l],) = _rows_vjp("rms_ffn_bwd", _rms_fn, [_part(sv["x2"])], [row(sp["ffn_norm"], l)],
                                                 [_part(dh3)], [F32], [True], adds={0: _part(dx)})
        dox = _mm("mm_xo_dx", dx2, wl["xo"], tb=True)
        weight_grad("xo", l, "mm_xo_dw", sv["ox"], dx2)
        dqx, dkx, dvx = _xattn_bwd(sv["qx"], sv["kv"], dox)
        dh2 = _mm("mm_xq_dx", dqx, wl["xq"], tb=True)
        weight_grad("xq", l, "mm_xq_dw", sv["h2"], dqx)
        dkv = [_part(dkx), _part(dvx)]
        dmemn = _mm("mm_xkv_dx", dkv, wl["xkv"], tb=True, add=dmemn)
        weight_grad("xkv", l, "mm_xkv_dw", memn, dkv)
        (dx1,), (gs["xattn_norm"][l],) = _rows_vjp("rms_x_bwd", _rms_fn, [_part(sv["x1"])], [row(sp["xattn_norm"], l)],
                                                   [_part(dh2)], [F32], [True], adds={0: _part(dx2)})
        dmg = _mm("mm_o_dx", dx1, wl["o"], tb=True)
        weight_grad("o", l, "mm_o_dw", sv["mg"], dx1)
        (dga, dgb, dgc, dya, dyb, dyc), dbg = _rows_vjp("merge_bwd", _merge_fn, sv["gate_rows"], sv["bg"], [_part(dmg)],
                                                        [BF16] * 6, [True] * 3)
        gs["b_gate"][l] = jnp.concatenate(dbg, axis=1)
        dpa = _mm("mm_out_a_dx", dya, wl["out_a"], tb=True)
        weight_grad("out_a", l, "mm_out_a_dw", sv["pa"], dya)
        do = _mm("mm_out_b_dx", dyb, wl["out_b"], tb=True)
        weight_grad("out_b", l, "mm_out_b_dw", sv["o"], dyb)
        du3 = _mm("mm_out_c_dx", dyc, wl["out_c"], tb=True)
        weight_grad("out_c", l, "mm_out_c_dw", sv["u3"], dyc)
        if on_layer is not None:
            du3 = du3 + on_layer(l, "early", {k: v[l] for k, v in gw.items() if k not in ("main", "f")})[0, 0]
        (du2,), (gs["ln_c_gain"][l], gs["ln_c_bias"][l]) = _rows_vjp(
            "ln_silu_bwd", _lnsilu_fn, [_part(sv["u2"])], [row(sp["ln_c_gain"], l), row(sp["ln_c_bias"], l)], [_part(du3)],
            [F32], [True, True])
        dcv, dcg, gs["conv_c"][l], gs["conv_c_bias"][l] = _conv_c_bwd(sv["zcg"], sp["conv_c"][l], du2)
        dab, dac, dau, gs["conv_a"][l] = _conv_a_bwd(sv["za"], sp["conv_a"][l], dpa)
        dq, dk, dv, dcr = _fox_bwd(sv["zqkv"], sv["crow"], do)
        dzf, dbf = _fox_cumsum_bwd(sv["zf"], sv["bf"], _pad_lanes(dcr.reshape(heads, t).T))
        gs["b_forget"][l] = dbf[:, :heads]
        dz = [_part(a) for a in (dab, dac, dau, dq, dk, dv, dcv, dcg, dga, dgb, dgc)]
        dh1f = _mm("mm_in_f_dx", dzf, wl["f"], tb=True)
        dh1 = _mm("mm_in_dx", dz, wl["main"], tb=True, add=dh1f)
        gw["main"][l] = [_mm("mm_in_dw_lo", sv["h1"], dz[:6], ta=True, out_dtype=BF16),
                         _mm("mm_in_dw_hi", sv["h1"], dz[6:], ta=True, out_dtype=BF16)]
        gw["f"][l] = _mm("mm_in_f_dw", sv["h1"], dzf, ta=True, out_dtype=BF16)
        (dx,), (gs["mix_norm"][l],) = _rows_vjp("rms_mix_bwd", _rms_fn, [_part(sv["x0"])], [row(sp["mix_norm"], l)],
                                                [_part(dh1)], [F32], [True], adds={0: _part(dx1)})
        if on_layer is not None:
            token = on_layer(l, "late", {"main": gw["main"][l], "f": gw["f"][l]})
            if l > 0:
                dx = dx + token[0, 0]

    _, (d_mem_norm,) = _rows_vjp("rms_mem_bwd", _rms_fn, [_part(mem)], [sp["mem_norm"][None]], [_part(dmemn)], [None], [True])
    small = {k: jnp.stack([a.reshape(sp[k].shape[1:]) for a in v]) for k, v in gs.items()}
    small["mem_norm"] = d_mem_norm[0]
    small["final_norm"] = d_final[0]
    return loss_row, dx, small, gw


_BIG = ("w_in", "w_out_a", "w_out_b", "w_out_c", "w_o", "w_xq", "w_xkv", "w_xo", "w_gate_up", "w_down")
_COL_SHARDED = ("w_in", "w_xo", "w_gate_up")
_KEY = {"w_out_a": "out_a", "w_out_b": "out_b", "w_out_c": "out_c", "w_o": "o", "w_xq": "xq", "w_xkv": "xkv", "w_xo": "xo",
        "w_gate_up": "gu", "w_down": "down"}
_SMALL = ("mix_norm", "b_gate", "b_forget", "conv_c_bias", "ln_c_gain", "ln_c_bias", "xattn_norm", "ffn_norm", "mem_norm",
          "final_norm")
_SMALL_SHARDED = ("conv_a", "conv_c")


def _pack_rows(parts):
    padded = []
    for a in parts:
        pad = -a.shape[0] % SUBLANES
        padded.append(jnp.pad(a, ((0, pad), (0, 0))) if pad else a)
    return jnp.concatenate(padded, axis=0)


def _unpack_rows(packed, shapes):
    out, pos = [], 0
    for r in shapes:
        out.append(packed[pos:pos + r])
        pos += r + (-r % SUBLANES)
    return out


def kernel(x, mem, mix_norm, w_in, b_gate, b_forget, conv_a, w_out_a, w_out_b, conv_c, conv_c_bias, ln_c_gain, ln_c_bias, w_out_c, w_o, xattn_norm, mem_norm, w_xq, w_xkv, w_xo, ffn_norm, w_gate_up, w_down, final_norm, loss_target, m_mix_norm, m_w_in, m_b_gate, m_b_forget, m_conv_a, m_w_out_a, m_w_out_b, m_conv_c, m_conv_c_bias, m_ln_c_gain, m_ln_c_bias, m_w_out_c, m_w_o, m_xattn_norm, m_mem_norm, m_w_xq, m_w_xkv, m_w_xo, m_ffn_norm, m_w_gate_up, m_w_down, m_final_norm, v_mix_norm, v_w_in, v_b_gate, v_b_forget, v_conv_a, v_w_out_a, v_w_out_b, v_conv_c, v_conv_c_bias, v_ln_c_gain, v_ln_c_bias, v_w_out_c, v_w_o, v_xattn_norm, v_mem_norm, v_w_xq, v_w_xkv, v_w_xo, v_ffn_norm, v_w_gate_up, v_w_down, v_final_norm):
    args = dict(locals())
    names = ["mix_norm", "w_in", "b_gate", "b_forget", "conv_a", "w_out_a", "w_out_b", "conv_c", "conv_c_bias", "ln_c_gain",
             "ln_c_bias", "w_out_c", "w_o", "xattn_norm", "mem_norm", "w_xq", "w_xkv", "w_xo", "ffn_norm", "w_gate_up", "w_down",
             "final_norm"]
    depth, d = mix_norm.shape
    heads = b_forget.shape[1]
    cx, cy, cc = lax.axis_index("x"), lax.axis_index("y"), lax.axis_index("c")
    chip = 2 * cx + cy
    c_idx = jnp.reshape(cc, (1,)).astype(jnp.int32)
    chip_idx = jnp.reshape(chip, (1,)).astype(jnp.int32)

    conv_rows = [conv_a.shape[1] * depth, conv_c.shape[1] * depth]
    conv_pack = _pack_rows([conv_a.reshape(conv_rows[0], -1), conv_c.reshape(conv_rows[1], -1)])
    conv_all = _allgather8(conv_pack).reshape(N_CHIPS, 2, conv_pack.shape[0], conv_pack.shape[1])[:, 0]
    conv_all = conv_all.transpose(1, 0, 2).reshape(conv_pack.shape[0], d)
    conv_a_full, conv_c_full = _unpack_rows(conv_all, conv_rows)
    own = [args[n].astype(BF16) for n in _BIG]
    first = _gather_weights(own, 0, conv_all)
    started, tokens = {}, []
    for l in range(1, depth):
        send, recv, lands, token = _gather_start(own, l, first[0])
        started[l] = (send, recv, own, lands)
        tokens.append(token)

    def layer_weights(l, x_in):
        if l == 0:
            lands = first
        else:
            lands = _gather_wait(*started[l], x_in, l)
        full = lands
        wl = {}
        wl["main"], wl["f"] = _w_in_aligned(full[0], d, heads)
        for n, g in zip(_BIG[1:], full[1:]):
            if n == "w_xo":
                wl[_KEY[n]] = g.transpose(1, 0, 2).reshape(g.shape[1], N_CHIPS * g.shape[2])
            elif n in _COL_SHARDED:
                wl[_KEY[n]] = [_part(g, lead=(j,)) for j in range(N_CHIPS)]
            else:
                wl[_KEY[n]] = g.reshape(N_CHIPS * g.shape[1], g.shape[2])
        return wl

    shard_shape = {n: (1, N_CHIPS) + args[n].shape[1:] for n in _BIG}
    dests = {_KEY[n]: (shard_shape[n], N_CHIPS) if n in _COL_SHARDED
             else ((1, N_CHIPS * args[n].shape[1], args[n].shape[2]), 1) for n in _BIG[1:]}
    sp = {n: args[n] for n in _SMALL}
    sp["conv_a"] = conv_a_full.reshape(depth, conv_a.shape[1], d)
    sp["conv_c"] = conv_c_full.reshape(depth, conv_c.shape[1], d)

    in_flight = {}

    def on_layer(l, which, gl):
        if which == "late":
            group = ["w_in"]
            local = [_w_in_shards(gl["main"], gl["f"], None, 0, shard_shape["w_in"], heads)]
        else:
            group = list(_BIG[1:])
            local = [gl[_KEY[n]].reshape(shard_shape[n]) for n in group]
        chip_sums = [_add_sibling(g, r, c_idx) for g, r in zip(local, _swap_halves(local))]
        send, recv, chip_sums, lands, token = _exchange_start(chip_sums, f"{l}_{which}")
        in_flight[(l, which)] = (group, send, recv, chip_sums, lands)
        return token

    x_in = x[0]
    for token in tokens:
        x_in = x_in + token[0, 0]
    loss_row, grad_x, gsmall, gw = _local_step(x_in, mem[0], loss_target[0], sp, layer_weights, dests, on_layer)
    halves = {n: None for n in _BIG}
    for l in reversed(range(depth)):
        for which in ("early", "late"):
            group, send, recv, chip_sums, lands = in_flight[(l, which)]
            chip_sums, lands = _exchange_wait(send, recv, chip_sums, lands, grad_x, f"{l}_{which}")
            for n, s_, r in zip(group, chip_sums, lands):
                halves[n] = _add_chips(s_, r, chip_idx, c_idx, l, depth, halves[n])
    halves = [halves[n] for n in _BIG]
    grads = dict(zip(_BIG, _join_halves(halves)))

    def as_rows(n, a):
        return _pad_lanes(a, d) if n == "b_forget" else a.reshape(-1, d)

    pieces = [as_rows(n, gsmall[n]) for n in _SMALL]
    for n in _SMALL_SHARDED:
        pieces.append(gsmall[n].reshape(-1, d))
    pieces.append(_pad_lanes(loss_row, d))
    pack = _pack_rows(pieces)
    total = _sum_devices(_allgather8(pack), pack.shape[0])
    shapes = [p.shape[0] for p in pieces]
    summed = _unpack_rows(total, shapes)
    for n, g in zip(_SMALL, summed):
        grads[n] = g[:, :heads] if n == "b_forget" else g.reshape(args[n].shape)
    for n, g in zip(_SMALL_SHARDED, summed[len(_SMALL):]):
        g = g.reshape(args[n].shape[:2] + (d,))
        grads[n] = lax.dynamic_slice_in_dim(g, chip * args[n].shape[2], args[n].shape[2], axis=2)
    loss = summed[-1][0, 0]

    delta, new_m, new_v = {}, {}, {}
    packed = [_pack_rows([as_rows(n, src[pre + n]) for n in _SMALL])
              for src, pre in ((args, ""), (grads, ""), (args, "m_"), (args, "v_"))]
    srows = [as_rows(n, args[n]).shape[0] for n in _SMALL]
    updated = [_unpack_rows(a, srows) for a in _adamw(*packed)]
    for k, n in enumerate(_SMALL):
        delta[n], new_m[n], new_v[n] = (
            (u[k][:, :heads] if n == "b_forget" else u[k]).reshape(args[n].shape) for u in updated)
    for n in _BIG + _SMALL_SHARDED:
        if n == "w_in":
            to_t, from_t = (lambda a: jnp.transpose(a, (2, 0, 1))), (lambda a: jnp.transpose(a, (1, 2, 0)))
            g_t = to_t(grads[n])
            grads[n] = from_t(g_t)
            delta[n], new_m[n], new_v[n] = (from_t(a) for a in _adamw(to_t(args[n]), g_t, to_t(args["m_" + n]), to_t(args["v_" + n]),
                                                                     keep_rank=True))
            continue
        delta[n], new_m[n], new_v[n] = _adamw(args[n], grads[n], args["m_" + n], args["v_" + n])

    return (loss, grad_x[None], *[grads[n] for n in names], *[delta[n] for n in names], *[new_m[n] for n in names],
            *[new_v[n] for n in names])
```
